```python
import jax, jax.numpy as jnp
from jax import lax
import numpy as np

D_MODEL = 1024
BATCH = 8
SEQ = 8192
DEPTH = 1

ATTN_WIDTH = D_MODEL // 2
ATTN_HEADS = 8
ATTN_HEAD_DIM = ATTN_WIDTH // ATTN_HEADS
DILATED_PAIRS = ((128, 1), (512, 4), (2048, 16))
ATTN_BLOCK = 128
ROPE_THETA = 10000.0
HGRN_WIDTH = D_MODEL - ATTN_WIDTH
HGRN_EXPAND = 128
HGRN_HEADS = HGRN_WIDTH // HGRN_EXPAND
HGRN_CHUNK = 16
MIX_WIDTH = ATTN_WIDTH + HGRN_WIDTH
IN_PROJ_WIDTH = 3 * ATTN_WIDTH + 4 * HGRN_WIDTH
FFN_HIDDEN = ((-(-8 * D_MODEL // 3) + 255) // 256) * 256
NORM_EPS = 1e-6

kernel_name = "hymba_dilated_attn_hgrn2_block"


def rmsnorm(x, w):
    xf = x.astype(jnp.float32)
    y = xf * lax.rsqrt(jnp.mean(xf * xf, axis=-1, keepdims=True) + NORM_EPS)
    return (y * w.astype(jnp.float32)).astype(x.dtype)


def rotary(x):
    S, Dh = x.shape[1], x.shape[3]
    half = Dh // 2
    inv_freq = ROPE_THETA ** (-jnp.arange(half, dtype=jnp.float32) / half)
    ang = jnp.arange(S, dtype=jnp.float32)[:, None] * inv_freq[None, :]
    cos = jnp.cos(ang)[None, :, None, :]
    sin = jnp.sin(ang)[None, :, None, :]
    xf = x.astype(jnp.float32)
    x1, x2 = xf[..., :half], xf[..., half:]
    return jnp.concatenate([x1 * cos - x2 * sin, x2 * cos + x1 * sin], axis=-1)


def dilated_window_attention(q, k, v, window, dilation):
    B, S, H, Dh = q.shape
    L = S // dilation
    W = window // dilation
    n_blk = -(-L // ATTN_BLOCK)
    Lp = n_blk * ATTN_BLOCK

    def to_blocks(t):
        t = t.reshape(B, L, dilation, H, Dh).transpose(0, 2, 1, 3, 4)
        t = jnp.pad(t, ((0, 0), (0, 0), (0, Lp - L), (0, 0), (0, 0)))
        return t.reshape(B, dilation, n_blk, ATTN_BLOCK, H, Dh)

    def with_prev(t):
        prev = jnp.pad(t, ((0, 0), (0, 0), (1, 0), (0, 0), (0, 0), (0, 0)))[:, :, :-1]
        return jnp.concatenate([prev, t], axis=3)

    qb = to_blocks(q)
    kc = with_prev(to_blocks(k))
    vc = with_prev(to_blocks(v))
    scores = jnp.einsum('bdnqhe,bdnkhe->bdnhqk', qb, kc) * (Dh ** -0.5)
    qi = jnp.arange(ATTN_BLOCK)[:, None]
    kj = jnp.arange(2 * ATTN_BLOCK)[None, :]
    delta = ATTN_BLOCK + qi - kj
    blk = jnp.arange(n_blk)[:, None, None]
    valid = (delta >= 0) & (delta <= W) & ((blk > 0) | (kj >= ATTN_BLOCK))[...]
    scores = jnp.where(valid[None, None, :, None], scores, -jnp.inf)
    m = jnp.max(scores, axis=-1, keepdims=True)
    p = jnp.exp(scores - m)
    s = jnp.sum(p, axis=-1, keepdims=True)
    out = jnp.einsum('bdnhqk,bdnkhe->bdnqhe', p, vc) / s.transpose(0, 1, 2, 4, 3, 5)
    lse = (m + jnp.log(s))[..., 0].transpose(0, 1, 2, 4, 3)
    out = out.reshape(B, dilation, Lp, H, Dh)[:, :, :L].transpose(0, 2, 1, 3, 4).reshape(B, S, H, Dh)
    lse = lse.reshape(B, dilation, Lp, H)[:, :, :L].transpose(0, 2, 1, 3).reshape(B, S, H)
    return out, lse


def dilated_attention_group(q, k, v):
    B, S, _ = q.shape
    qh = rotary(q.reshape(B, S, ATTN_HEADS, ATTN_HEAD_DIM))
    kh = rotary(k.reshape(B, S, ATTN_HEADS, ATTN_HEAD_DIM))
    vh = v.reshape(B, S, ATTN_HEADS, ATTN_HEAD_DIM).astype(jnp.float32)
    outs, lses = [], []
    for window, dilation in DILATED_PAIRS:
        o, l = dilated_window_attention(qh, kh, vh, window, dilation)
        outs.append(o)
        lses.append(l)
    weights = jax.nn.softmax(jnp.stack(lses, axis=0), axis=0)
    y = jnp.sum(weights[..., None] * jnp.stack(outs, axis=0), axis=0)
    return y.reshape(B, S, ATTN_WIDTH)


def hgrn2_group(q, f_logit, i, g, lb, norm_w):
    B, S, _ = q.shape
    H, Dk, C = HGRN_HEADS, HGRN_EXPAND, HGRN_CHUNK
    N = S // C
    f = lb + (1.0 - lb) * jax.nn.sigmoid(f_logit.astype(jnp.float32))
    log_f = jnp.log(f)
    key = 1.0 - f
    qf = jax.nn.silu(q.astype(jnp.float32))

    def chunks(t):
        return t.reshape(B, N, C, H, Dk).transpose(0, 3, 1, 2, 4)

    qc, kc, vc, lfc = chunks(qf), chunks(key), chunks(i.astype(jnp.float32)), chunks(log_f)
    b = jnp.cumsum(lfc, axis=3)
    causal = jnp.tril(jnp.ones((C, C), dtype=bool))
    diff = b[:, :, :, :, None, :] - b[:, :, :, None, :, :]
    decay = jnp.exp(jnp.where(causal[:, :, None], diff, -jnp.inf))
    scores = jnp.einsum('bhntd,bhnsd,bhntsd->bhnts', qc, kc, decay)
    o_intra = jnp.einsum('bhnts,bhnsv->bhntv', scores, vc)

    b_last = b[:, :, :, -1:, :]
    q_inter = qc * jnp.exp(b)
    k_upd = kc * jnp.exp(b_last - b)
    chunk_decay = jnp.exp(b_last[:, :, :, 0, :])

    def step(state, xs):
        qn, kn, vn, dn = xs
        o = jnp.einsum('bhtd,bhdv->bhtv', qn, state)
        state = dn[..., None] * state + jnp.einsum('bhtd,bhtv->bhdv', kn, vn)
        return state, o

    xs = (jnp.moveaxis(q_inter, 2, 0), jnp.moveaxis(k_upd, 2, 0),
          jnp.moveaxis(vc, 2, 0), jnp.moveaxis(chunk_decay, 2, 0))
    state0 = jnp.zeros((B, H, Dk, Dk), dtype=jnp.float32)
    _, o_inter = lax.scan(step, state0, xs)
    o = o_intra + jnp.moveaxis(o_inter, 0, 2)
    o = o.transpose(0, 2, 3, 1, 4).reshape(B, S, H, Dk)
    o = o * lax.rsqrt(jnp.mean(o * o, axis=-1, keepdims=True) + NORM_EPS)
    o = o.reshape(B, S, HGRN_WIDTH) * norm_w.astype(jnp.float32)
    return o * jax.nn.silu(g.astype(jnp.float32))


def _fwd_setup_inputs(seed: int = 0) -> dict:
    key = jax.random.key(seed)
    ks = jax.random.split(key, 10)
    f32 = jnp.float32
    x = jax.random.normal(ks[0], (BATCH, SEQ, D_MODEL), f32)
    norm1_w = 1.0 + 0.02 * jax.random.normal(ks[1], (DEPTH, D_MODEL), f32)
    w_in = jax.random.normal(ks[2], (DEPTH, D_MODEL, IN_PROJ_WIDTH), f32) * D_MODEL ** -0.5
    lb_logits = 0.5 * jax.random.normal(ks[3], (DEPTH + 1, HGRN_WIDTH), f32)
    hgrn_norm_w = 1.0 + 0.02 * jax.random.normal(ks[4], (DEPTH, HGRN_WIDTH), f32)
    w_out = jax.random.normal(ks[5], (DEPTH, MIX_WIDTH, D_MODEL), f32) * MIX_WIDTH ** -0.5
    norm2_w = 1.0 + 0.02 * jax.random.normal(ks[6], (DEPTH, D_MODEL), f32)
    w_gate_up = jax.random.normal(ks[7], (DEPTH, D_MODEL, 2 * FFN_HIDDEN), f32) * D_MODEL ** -0.5
    w_down = jax.random.normal(ks[8], (DEPTH, FFN_HIDDEN, D_MODEL), f32) * FFN_HIDDEN ** -0.5
    final_norm_w = 1.0 + 0.02 * jax.random.normal(ks[9], (D_MODEL,), f32)
    return {"x": x, "norm1_w": norm1_w, "w_in": w_in, "lb_logits": lb_logits,
            "hgrn_norm_w": hgrn_norm_w, "w_out": w_out, "norm2_w": norm2_w,
            "w_gate_up": w_gate_up, "w_down": w_down, "final_norm_w": final_norm_w}


def _fwd_reference(x, norm1_w, w_in, lb_logits, hgrn_norm_w, w_out, norm2_w, w_gate_up, w_down, final_norm_w):
    lb_table = jnp.cumsum(jax.nn.softmax(lb_logits.astype(jnp.float32), axis=0), axis=0)
    h = x
    for l in range(DEPTH):
        u = rmsnorm(h, norm1_w[l])
        proj = jnp.einsum('bsd,de->bse', u, w_in[l])
        a = ATTN_WIDTH
        qa, ka, va = proj[..., :a], proj[..., a:2 * a], proj[..., 2 * a:3 * a]
        o = 3 * a
        w = HGRN_WIDTH
        qb, fb, ib, gb = (proj[..., o:o + w], proj[..., o + w:o + 2 * w],
                          proj[..., o + 2 * w:o + 3 * w], proj[..., o + 3 * w:o + 4 * w])
        ya = dilated_attention_group(qa, ka, va)
        yb = hgrn2_group(qb, fb, ib, gb, lb_table[l], hgrn_norm_w[l])
        mixed = jnp.concatenate([ya, yb], axis=-1).astype(h.dtype)
        h = h + jnp.einsum('bse,ed->bsd', mixed, w_out[l])
        u2 = rmsnorm(h, norm2_w[l])
        gu = jnp.einsum('bsd,df->bsf', u2, w_gate_up[l])
        gate, up = gu[..., :FFN_HIDDEN], gu[..., FFN_HIDDEN:]
        h = h + jnp.einsum('bsf,fd->bsd', jax.nn.silu(gate) * up, w_down[l])
    return rmsnorm(h, final_norm_w)


import jax as _jax
import jax.numpy as _jnp

TWIN_FORMAT = 'train_step'
FWD_PARAMS = ['x', 'norm1_w', 'w_in', 'lb_logits', 'hgrn_norm_w', 'w_out', 'norm2_w', 'w_gate_up', 'w_down', 'final_norm_w']
TWIN_WEIGHTS = ['norm1_w', 'w_in', 'lb_logits', 'hgrn_norm_w', 'w_out', 'norm2_w', 'w_gate_up', 'w_down', 'final_norm_w']
TWIN_DIFF_INPUT = 'x'
TWIN_INPUTS = ['x', 'norm1_w', 'w_in', 'lb_logits', 'hgrn_norm_w', 'w_out', 'norm2_w', 'w_gate_up', 'w_down', 'final_norm_w', 'loss_target', 'm_norm1_w', 'm_w_in', 'm_lb_logits', 'm_hgrn_norm_w', 'm_w_out', 'm_norm2_w', 'm_w_gate_up', 'm_w_down', 'm_final_norm_w', 'v_norm1_w', 'v_w_in', 'v_lb_logits', 'v_hgrn_norm_w', 'v_w_out', 'v_norm2_w', 'v_w_gate_up', 'v_w_down', 'v_final_norm_w']
TWIN_OUTPUTS = ['loss', 'grad_x', 'grad_norm1_w', 'grad_w_in', 'grad_lb_logits', 'grad_hgrn_norm_w', 'grad_w_out', 'grad_norm2_w', 'grad_w_gate_up', 'grad_w_down', 'grad_final_norm_w', 'delta_norm1_w', 'delta_w_in', 'delta_lb_logits', 'delta_hgrn_norm_w', 'delta_w_out', 'delta_norm2_w', 'delta_w_gate_up', 'delta_w_down', 'delta_final_norm_w', 'new_m_norm1_w', 'new_m_w_in', 'new_m_lb_logits', 'new_m_hgrn_norm_w', 'new_m_w_out', 'new_m_norm2_w', 'new_m_w_gate_up', 'new_m_w_down', 'new_m_final_norm_w', 'new_v_norm1_w', 'new_v_w_in', 'new_v_lb_logits', 'new_v_hgrn_norm_w', 'new_v_w_out', 'new_v_norm2_w', 'new_v_w_gate_up', 'new_v_w_down', 'new_v_final_norm_w']
TWIN_LEAF_KINDS = {'loss': 'loss', 'grad_x': 'grad_x', 'grad_norm1_w': 'grad_w', 'grad_w_in': 'grad_w', 'grad_lb_logits': 'grad_w', 'grad_hgrn_norm_w': 'grad_w', 'grad_w_out': 'grad_w', 'grad_norm2_w': 'grad_w', 'grad_w_gate_up': 'grad_w', 'grad_w_down': 'grad_w', 'grad_final_norm_w': 'grad_w', 'delta_norm1_w': 'delta_w', 'delta_w_in': 'delta_w', 'delta_lb_logits': 'delta_w', 'delta_hgrn_norm_w': 'delta_w', 'delta_w_out': 'delta_w', 'delta_norm2_w': 'delta_w', 'delta_w_gate_up': 'delta_w', 'delta_w_down': 'delta_w', 'delta_final_norm_w': 'delta_w', 'new_m_norm1_w': 'new_m', 'new_m_w_in': 'new_m', 'new_m_lb_logits': 'new_m', 'new_m_hgrn_norm_w': 'new_m', 'new_m_w_out': 'new_m', 'new_m_norm2_w': 'new_m', 'new_m_w_gate_up': 'new_m', 'new_m_w_down': 'new_m', 'new_m_final_norm_w': 'new_m', 'new_v_norm1_w': 'new_v', 'new_v_w_in': 'new_v', 'new_v_lb_logits': 'new_v', 'new_v_hgrn_norm_w': 'new_v', 'new_v_w_out': 'new_v', 'new_v_norm2_w': 'new_v', 'new_v_w_gate_up': 'new_v', 'new_v_w_down': 'new_v', 'new_v_final_norm_w': 'new_v'}


def _forward(args):
    return _fwd_reference(*[args[k] for k in FWD_PARAMS])


def _output_shape():
    def fwd():
        inp = _fwd_setup_inputs(0)
        return _fwd_reference(*[inp[k] for k in FWD_PARAMS])
    out = _jax.eval_shape(fwd)
    return out.shape, out.dtype

N_MICROBATCH = 1
ADAM_LR = 0.001
ADAM_B1 = 0.9
ADAM_B2 = 0.999
ADAM_EPS = 1e-08
ADAM_WD = 0.01
ADAM_STEP = 10
PER_EXAMPLE_BATCH_AXIS = {'x': 0, 'loss_target': 0}
SHARED_INPUTS = []
_WEIGHT_DTYPES = {'norm1_w': _jnp.float32, 'w_in': _jnp.float32, 'lb_logits': _jnp.float32, 'hgrn_norm_w': _jnp.float32, 'w_out': _jnp.float32, 'norm2_w': _jnp.float32, 'w_gate_up': _jnp.float32, 'w_down': _jnp.float32, 'final_norm_w': _jnp.float32}
MOMENT_SCALE = {'norm1_w': 1.725784e-01, 'w_in': 8.949087e-02, 'lb_logits': 1.475353e-02, 'hgrn_norm_w': 1.510576e-01, 'w_out': 1.142842e-01, 'norm2_w': 1.846738e-01, 'w_gate_up': 7.408544e-02, 'w_down': 1.211441e-01, 'final_norm_w': 6.402272e+01}


def _to_microbatches(a, axis):
    t = _jnp.moveaxis(a, axis, 0)
    t = t.reshape((N_MICROBATCH, t.shape[0] // N_MICROBATCH) + t.shape[1:])
    return _jnp.moveaxis(t, 1, axis + 1)


def setup_inputs(seed: int = 0) -> dict:
    inp = _fwd_setup_inputs(seed)
    key = _jax.random.fold_in(_jax.random.key(seed), 7919)
    shape, _ = _output_shape()
    out = dict(inp)
    out["loss_target"] = _jax.random.normal(_jax.random.fold_in(key, 0), shape, _jnp.float32)
    for i, name in enumerate(TWIN_WEIGHTS):
        w = inp[name].astype(_jnp.float32)
        if MOMENT_SCALE is None:
            s = _jnp.sqrt(_jnp.mean(_jnp.square(w)) + 1e-30)
        else:
            s = MOMENT_SCALE[name]
        km, kv = _jax.random.split(_jax.random.fold_in(key, i + 1))
        out[name] = w
        out["m_" + name] = s * _jax.random.normal(km, w.shape, _jnp.float32)
        out["v_" + name] = (s * s) * _jax.random.uniform(kv, w.shape, _jnp.float32, 0.5, 1.5)
    if N_MICROBATCH > 1:
        for name, axis in PER_EXAMPLE_BATCH_AXIS.items():
            out[name] = _to_microbatches(out[name], axis)
    return {'x': out['x'], 'norm1_w': out['norm1_w'], 'w_in': out['w_in'], 'lb_logits': out['lb_logits'], 'hgrn_norm_w': out['hgrn_norm_w'], 'w_out': out['w_out'], 'norm2_w': out['norm2_w'], 'w_gate_up': out['w_gate_up'], 'w_down': out['w_down'], 'final_norm_w': out['final_norm_w'], 'loss_target': out['loss_target'], 'm_norm1_w': out['m_norm1_w'], 'm_w_in': out['m_w_in'], 'm_lb_logits': out['m_lb_logits'], 'm_hgrn_norm_w': out['m_hgrn_norm_w'], 'm_w_out': out['m_w_out'], 'm_norm2_w': out['m_norm2_w'], 'm_w_gate_up': out['m_w_gate_up'], 'm_w_down': out['m_w_down'], 'm_final_norm_w': out['m_final_norm_w'], 'v_norm1_w': out['v_norm1_w'], 'v_w_in': out['v_w_in'], 'v_lb_logits': out['v_lb_logits'], 'v_hgrn_norm_w': out['v_hgrn_norm_w'], 'v_w_out': out['v_w_out'], 'v_norm2_w': out['v_norm2_w'], 'v_w_gate_up': out['v_w_gate_up'], 'v_w_down': out['v_w_down'], 'v_final_norm_w': out['v_final_norm_w']}


def _loss(weights, diff, rest, loss_target):
    with _jax.named_scope("forward"):
        args = {**rest, TWIN_DIFF_INPUT: diff, **{k: w.astype(_WEIGHT_DTYPES[k]) for k, w in weights.items()}}
        y = _forward(args)
    with _jax.named_scope("loss_head"):
        err = _jnp.square(y.astype(_jnp.float32) - loss_target)
        return 0.5 * _jnp.sum(_jnp.mean(err, axis=-1)) if err.ndim else 0.5 * err


def _adamw(w, g, m, v):
    m = ADAM_B1 * m + (1.0 - ADAM_B1) * g
    v = ADAM_B2 * v + (1.0 - ADAM_B2) * _jnp.square(g)
    m_hat = m / (1.0 - ADAM_B1 ** ADAM_STEP)
    v_hat = v / (1.0 - ADAM_B2 ** ADAM_STEP)
    delta = -ADAM_LR * (m_hat / (_jnp.sqrt(v_hat) + ADAM_EPS) + ADAM_WD * w)
    return delta, m, v


def reference(x, norm1_w, w_in, lb_logits, hgrn_norm_w, w_out, norm2_w, w_gate_up, w_down, final_norm_w, loss_target, m_norm1_w, m_w_in, m_lb_logits, m_hgrn_norm_w, m_w_out, m_norm2_w, m_w_gate_up, m_w_down, m_final_norm_w, v_norm1_w, v_w_in, v_lb_logits, v_hgrn_norm_w, v_w_out, v_norm2_w, v_w_gate_up, v_w_down, v_final_norm_w):
    given = dict(x=x, norm1_w=norm1_w, w_in=w_in, lb_logits=lb_logits, hgrn_norm_w=hgrn_norm_w, w_out=w_out, norm2_w=norm2_w, w_gate_up=w_gate_up, w_down=w_down, final_norm_w=final_norm_w, loss_target=loss_target, m_norm1_w=m_norm1_w, m_w_in=m_w_in, m_lb_logits=m_lb_logits, m_hgrn_norm_w=m_hgrn_norm_w, m_w_out=m_w_out, m_norm2_w=m_norm2_w, m_w_gate_up=m_w_gate_up, m_w_down=m_w_down, m_final_norm_w=m_final_norm_w, v_norm1_w=v_norm1_w, v_w_in=v_w_in, v_lb_logits=v_lb_logits, v_hgrn_norm_w=v_hgrn_norm_w, v_w_out=v_w_out, v_norm2_w=v_norm2_w, v_w_gate_up=v_w_gate_up, v_w_down=v_w_down, v_final_norm_w=v_final_norm_w)
    weights = {n: given[n] for n in TWIN_WEIGHTS}
    shared = {n: given[n] for n in SHARED_INPUTS}
    per_example = {n: given[n] for n in ['x']}
    grad_fn = _jax.value_and_grad(_loss, argnums=(0, 1))

    def one_microbatch(ex, loss_target):
        ex = dict(ex)
        diff = ex.pop(TWIN_DIFF_INPUT)
        return grad_fn(weights, diff, {**shared, **ex}, loss_target)

    if N_MICROBATCH == 1:
        loss, (grad_w, grad_x) = one_microbatch(per_example, given["loss_target"])
    else:
        def body(carry, xs):
            loss_sum, grad_sum = carry
            l_k, (gw_k, gx_k) = one_microbatch(xs[0], xs[1])
            with _jax.named_scope("update"):
                return (loss_sum + l_k, _jax.tree.map(_jnp.add, grad_sum, gw_k)), gx_k

        init = (_jnp.zeros((), _jnp.float32), _jax.tree.map(_jnp.zeros_like, weights))
        (loss, grad_w), grad_x = _jax.lax.scan(body, init, (per_example, given["loss_target"]))
    with _jax.named_scope("update"):
        delta_w, new_m, new_v = {}, {}, {}
        for n in TWIN_WEIGHTS:
            delta_w[n], new_m[n], new_v[n] = _adamw(weights[n], grad_w[n], given["m_" + n], given["v_" + n])
    return (loss, grad_x, *[grad_w[n] for n in TWIN_WEIGHTS], *[delta_w[n] for n in TWIN_WEIGHTS],
            *[new_m[n] for n in TWIN_WEIGHTS], *[new_v[n] for n in TWIN_WEIGHTS])
```

```python
import functools

import jax
import jax.numpy as jnp
from jax import lax
from jax.experimental import pallas as pl
from jax.experimental.pallas import tpu as pltpu

f32, bf16 = jnp.float32, jnp.bfloat16

D_MODEL = 1024
ATTN_W = 512
HEAD_DIM = 64
ATTN_BLK = 128
DILATIONS = (1, 4, 16)
HGRN_W = 512
HGRN_HD = 128
CHUNK = 16
IN_W = 3 * ATTN_W + 4 * HGRN_W
FFN = 2816
EPS = 1e-6
ROPE_THETA = 10000.0
NEG = -1e30
N_DEV = 8
ADAM_LR, ADAM_B1, ADAM_B2, ADAM_EPS, ADAM_WD, ADAM_STEP = 0.001, 0.9, 0.999, 1e-08, 0.01, 10
VMEM_LIMIT = 56 * 1024 * 1024


def _cp(*sem):
    return pltpu.CompilerParams(dimension_semantics=sem, vmem_limit_bytes=VMEM_LIMIT)


def _dot(a, b):
    return jnp.dot(a, b, preferred_element_type=f32)


def _dot_nt(a, b):
    return lax.dot_general(a, b, (((1,), (1,)), ((), ())), preferred_element_type=f32)


def _dot_tn(a, b):
    return lax.dot_general(a, b, (((0,), (0,)), ((), ())), preferred_element_type=f32)


def _sigmoid(x):
    return 1.0 / (1.0 + jnp.exp(-x))


def _rope_tables(S):
    half = HEAD_DIM // 2
    inv_freq = ROPE_THETA ** (-jnp.arange(half, dtype=f32) / half)
    ang = jnp.arange(S, dtype=f32)[:, None] * inv_freq[None, :]
    cos, sin = jnp.cos(ang), jnp.sin(ang)
    cos_t = jnp.tile(jnp.concatenate([cos, cos], axis=1), (1, ATTN_W // HEAD_DIM))
    sg_t = jnp.tile(jnp.concatenate([-sin, sin], axis=1), (1, ATTN_W // HEAD_DIM))
    return cos_t, sg_t


def _swap_halves(v):
    n = v.shape[1]
    lane = lax.broadcasted_iota(jnp.int32, v.shape, 1)
    return jnp.where((lane % HEAD_DIM) < HEAD_DIM // 2, pltpu.roll(v, n - HEAD_DIM // 2, 1), pltpu.roll(v, HEAD_DIM // 2, 1))


def _in_proj(x, w1, win, cos_t, sg_t):
    S = x.shape[0]
    tm = 256

    def body(x_ref, w1_ref, w_ref, cos_ref, sg_ref, u_ref, qkv_ref, hp_ref):
        xv = x_ref[...]
        r = lax.rsqrt(jnp.mean(xv * xv, axis=-1, keepdims=True) + EPS)
        u = (xv * r * w1_ref[...]).astype(bf16)
        u_ref[...] = u
        cosv, sgv = cos_ref[...], sg_ref[...]
        for j in range(3):
            pj = _dot(u, w_ref[:, j * ATTN_W:(j + 1) * ATTN_W])
            if j < 2:
                pj = pj * cosv + _swap_halves(pj) * sgv
            if j == 0:
                pj = pj * (HEAD_DIM ** -0.5)
            qkv_ref[:, j * ATTN_W:(j + 1) * ATTN_W] = pj.astype(bf16)
        for j in range(4):
            lo = 3 * ATTN_W + j * HGRN_W
            hp_ref[:, j * HGRN_W:(j + 1) * HGRN_W] = _dot(u, w_ref[:, lo:lo + HGRN_W])

    return pl.pallas_call(
        body, name="in_proj", grid=(S // tm,),
        in_specs=[pl.BlockSpec((tm, D_MODEL), lambda i: (i, 0)), pl.BlockSpec((1, D_MODEL), lambda i: (0, 0)),
                  pl.BlockSpec((D_MODEL, IN_W), lambda i: (0, 0)),
                  pl.BlockSpec((tm, ATTN_W), lambda i: (i, 0)), pl.BlockSpec((tm, ATTN_W), lambda i: (i, 0))],
        out_specs=[pl.BlockSpec((tm, D_MODEL), lambda i: (i, 0)), pl.BlockSpec((tm, 3 * ATTN_W), lambda i: (i, 0)),
                   pl.BlockSpec((tm, 4 * HGRN_W), lambda i: (i, 0))],
        out_shape=[jax.ShapeDtypeStruct((S, D_MODEL), bf16), jax.ShapeDtypeStruct((S, 3 * ATTN_W), bf16),
                   jax.ShapeDtypeStruct((S, 4 * HGRN_W), f32)],
        compiler_params=_cp("arbitrary"),
    )(x, w1, win, cos_t, sg_t)


def _valid_mask(n):
    qi = lax.broadcasted_iota(jnp.int32, (ATTN_BLK, 2 * ATTN_BLK), 0)
    kj = lax.broadcasted_iota(jnp.int32, (ATTN_BLK, 2 * ATTN_BLK), 1)
    delta = ATTN_BLK + qi - kj
    return (delta >= 0) & (delta <= ATTN_BLK) & ((n > 0) | (kj >= ATTN_BLK))


def _head_masks():
    lane = lax.broadcasted_iota(jnp.int32, (ATTN_BLK, 128), 1)
    even = lane < HEAD_DIM
    return even, (even, jnp.logical_not(even))


def _attn_fwd(qkv, dil, prev):
    S = qkv.shape[0]
    L = S // dil
    nb = L // ATTN_BLK
    W3 = 3 * ATTN_W
    has_prev = prev is not None

    def body(*refs):
        if has_prev:
            q_ref, kp_ref, kc_ref, vp_ref, vc_ref, yp_ref, lp_ref, y_ref, l_ref = refs
        else:
            q_ref, kp_ref, kc_ref, vp_ref, vc_ref, y_ref, l_ref = refs
        n = pl.program_id(1)
        valid = _valid_mask(n)
        even, masks = _head_masks()
        for p in range(ATTN_W // 128):
            sl = slice(128 * p, 128 * p + 128)
            q2 = q_ref[:, sl].astype(f32)
            k2 = jnp.concatenate([kp_ref[:, sl], kc_ref[:, sl]], axis=0)
            v2 = jnp.concatenate([vp_ref[:, sl], vc_ref[:, sl]], axis=0)
            outs, lses = [], []
            for e in range(2):
                qm = jnp.where(masks[e], q2, 0.0).astype(bf16)
                s = jnp.where(valid, _dot_nt(qm, k2), NEG)
                m = jnp.max(s, axis=-1, keepdims=True)
                pe = jnp.exp(s - m)
                lsum = jnp.sum(pe, axis=-1, keepdims=True)
                acc = _dot(pe.astype(bf16), v2)
                outs.append(acc / lsum)
                lses.append(jnp.broadcast_to(m + jnp.log(lsum), (ATTN_BLK, 128)))
            out = jnp.where(even, outs[0], outs[1])
            lse = jnp.where(even, lses[0], lses[1])
            if has_prev:
                lp = lp_ref[:, sl]
                mx = jnp.maximum(lp, lse)
                a, b = jnp.exp(lp - mx), jnp.exp(lse - mx)
                tot = a + b
                out = (a * yp_ref[:, sl] + b * out) / tot
                lse = mx + jnp.log(tot)
            y_ref[:, sl] = out
            l_ref[:, sl] = lse

    blk = (ATTN_BLK, ATTN_W)
    cur = lambda c: (lambda r, n: (n, 3 * r + c))
    prv = lambda c: (lambda r, n: (jnp.maximum(n - 1, 0), 3 * r + c))
    in_specs = [pl.BlockSpec(blk, cur(0)), pl.BlockSpec(blk, prv(1)), pl.BlockSpec(blk, cur(1)),
                pl.BlockSpec(blk, prv(2)), pl.BlockSpec(blk, cur(2))]
    qv = qkv.reshape(L, dil * W3)
    args = [qv, qv, qv, qv, qv]
    if has_prev:
        in_specs += [pl.BlockSpec(blk, lambda r, n: (n, r))] * 2
        args += [prev[0].reshape(L, dil * ATTN_W), prev[1].reshape(L, dil * ATTN_W)]
    y, lse = pl.pallas_call(
        body, name=f"attn_fwd_d{dil}", grid=(dil, nb), in_specs=in_specs,
        out_specs=[pl.BlockSpec(blk, lambda r, n: (n, r))] * 2,
        out_shape=[jax.ShapeDtypeStruct((L, dil * ATTN_W), f32)] * 2,
        compiler_params=_cp("arbitrary", "arbitrary"),
    )(*args)
    return y.reshape(S, ATTN_W), lse.reshape(S, ATTN_W)


def _attn_bwd(qkv, ya, lse, dmix, dil, prev):
    S = qkv.shape[0]
    L = S // dil
    nb = L // ATTN_BLK
    W3 = 3 * ATTN_W
    has_prev = prev is not None

    def body(*refs):
        q_ref, kp_ref, kc_ref, vp_ref, vc_ref, y_ref, l_ref, dy_ref = refs[:8]
        refs = refs[8:]
        if has_prev:
            dqp_ref, dkp_ref, dvp_ref = refs[:3]
            refs = refs[3:]
        dq_ref, dk_ref, dv_ref, ck, cv = refs
        n = pl.program_id(1)

        @pl.when(n == 0)
        def _():
            ck[...] = jnp.zeros_like(ck)
            cv[...] = jnp.zeros_like(cv)

        @pl.when(n < nb)
        def _():
            valid = _valid_mask(n)
            even, masks = _head_masks()
            li = lax.broadcasted_iota(jnp.int32, (128, 128), 0)
            lj = lax.broadcasted_iota(jnp.int32, (128, 128), 1)
            seg = jnp.where((li // HEAD_DIM) == (lj // HEAD_DIM), 1.0, 0.0).astype(bf16)
            for p in range(ATTN_W // 128):
                sl = slice(128 * p, 128 * p + 128)
                q2 = q_ref[:, sl].astype(f32)
                k2 = jnp.concatenate([kp_ref[:, sl], kc_ref[:, sl]], axis=0)
                v2 = jnp.concatenate([vp_ref[:, sl], vc_ref[:, sl]], axis=0)
                k2f = k2.astype(f32)
                dy2 = dy_ref[:, sl]
                lse2 = l_ref[:, sl]
                dyy = dy2 * y_ref[:, sl]
                hi = dyy.astype(bf16)
                lo = (dyy - hi.astype(f32)).astype(bf16)
                delta2 = _dot(hi, seg) + _dot(lo, seg)
                dq2 = jnp.zeros((ATTN_BLK, 128), f32)
                dk2 = jnp.zeros((2 * ATTN_BLK, 128), f32)
                dv2 = jnp.zeros((2 * ATTN_BLK, 128), f32)
                for e in range(2):
                    c0 = e * HEAD_DIM
                    qm = jnp.where(masks[e], q2, 0.0).astype(bf16)
                    km = jnp.where(masks[e][:1, :], k2f, 0.0).astype(bf16)
                    dym = jnp.where(masks[e], dy2, 0.0).astype(bf16)
                    s = _dot_nt(qm, k2)
                    pe = jnp.where(valid, jnp.exp(s - lse2[:, c0:c0 + 1]), 0.0)
                    dp = _dot_nt(dym, v2)
                    ds = (pe * (dp - delta2[:, c0:c0 + 1])).astype(bf16)
                    dv2 = dv2 + _dot_tn(pe.astype(bf16), dym)
                    dq2 = dq2 + _dot(ds, km)
                    dk2 = dk2 + _dot_tn(ds, qm)
                tk = dk2[:ATTN_BLK] + ck[:, sl]
                tv = dv2[:ATTN_BLK] + cv[:, sl]
                if has_prev:
                    dq2 = dq2 + dqp_ref[:, sl]
                    tk = tk + dkp_ref[:, sl]
                    tv = tv + dvp_ref[:, sl]
                dq_ref[:, sl] = dq2
                dk_ref[:, sl] = tk
                dv_ref[:, sl] = tv
                ck[:, sl] = dk2[ATTN_BLK:]
                cv[:, sl] = dv2[ATTN_BLK:]

        @pl.when(n == nb)
        def _():
            if has_prev:
                dk_ref[...] = ck[...] + dkp_ref[...]
                dv_ref[...] = cv[...] + dvp_ref[...]
            else:
                dk_ref[...] = ck[...]
                dv_ref[...] = cv[...]

    blk = (ATTN_BLK, ATTN_W)
    cn = lambda n: jnp.minimum(n, nb - 1)
    pn = lambda n: jnp.clip(n - 1, 0, nb - 1)
    in_specs = [pl.BlockSpec(blk, lambda r, n: (cn(n), 3 * r)),
                pl.BlockSpec(blk, lambda r, n: (pn(n), 3 * r + 1)), pl.BlockSpec(blk, lambda r, n: (cn(n), 3 * r + 1)),
                pl.BlockSpec(blk, lambda r, n: (pn(n), 3 * r + 2)), pl.BlockSpec(blk, lambda r, n: (cn(n), 3 * r + 2)),
                pl.BlockSpec(blk, lambda r, n: (cn(n), r)), pl.BlockSpec(blk, lambda r, n: (cn(n), r)),
                pl.BlockSpec(blk, lambda r, n: (cn(n), 2 * r))]
    qv = qkv.reshape(L, dil * W3)
    view = lambda a: a.reshape(L, dil * ATTN_W)
    args = [qv, qv, qv, qv, qv, view(ya), view(lse), dmix.reshape(L, dil * D_MODEL)]
    if has_prev:
        in_specs += [pl.BlockSpec(blk, lambda r, n: (cn(n), r)), pl.BlockSpec(blk, lambda r, n: (pn(n), r)),
                     pl.BlockSpec(blk, lambda r, n: (pn(n), r))]
        args += [view(a) for a in prev]
    outs = pl.pallas_call(
        body, name=f"attn_bwd_d{dil}", grid=(dil, nb + 1), in_specs=in_specs,
        out_specs=[pl.BlockSpec(blk, lambda r, n: (cn(n), r)), pl.BlockSpec(blk, lambda r, n: (pn(n), r)),
                   pl.BlockSpec(blk, lambda r, n: (pn(n), r))],
        out_shape=[jax.ShapeDtypeStruct((L, dil * ATTN_W), f32)] * 3,
        scratch_shapes=[pltpu.VMEM(blk, f32), pltpu.VMEM(blk, f32)],
        compiler_params=_cp("arbitrary", "arbitrary"),
    )(*args)
    return tuple(o.reshape(S, ATTN_W) for o in outs)


HG_T = 256


def _row_in_chunk():
    return lax.broadcasted_iota(jnp.int32, (HG_T, HGRN_HD), 0) % CHUNK


def _chunk_cumsum(v, rc):
    for k in (1, 2, 4, 8):
        v = v + jnp.where(rc >= k, pltpu.roll(v, k, 0), 0.0)
    return v


def _chunk_rcumsum(v, rc):
    for k in (1, 2, 4, 8):
        v = v + jnp.where(rc < CHUNK - k, pltpu.roll(v, HG_T - k, 0), 0.0)
    return v


def _hgrn_gates(qb, fb, lb):
    sf = _sigmoid(fb)
    f = lb + (1.0 - lb) * sf
    sq = _sigmoid(qb)
    return sf, f, jnp.log(f), 1.0 - f, sq, qb * sq


def _hgrn_specs(nT, rev):
    ti = (lambda i: nT - 1 - i) if rev else (lambda i: i)
    col = lambda c: pl.BlockSpec((HG_T, HGRN_HD), lambda h, i: (ti(i), 4 * c + h))
    vec = pl.BlockSpec((1, HGRN_HD), lambda h, i: (0, h))
    lbs = pl.BlockSpec((2, HGRN_HD), lambda h, i: (0, h))
    tile = pl.BlockSpec((HG_T, HGRN_HD), lambda h, i: (ti(i), h))
    st = pl.BlockSpec((HG_T // CHUNK, HGRN_HD, HGRN_HD), lambda h, i: (ti(i), h, 0))
    return col, vec, lbs, tile, st


def _hgrn_fwd(hp, lbl, wn):
    S = hp.shape[0]
    nT = S // HG_T

    def body(qb_ref, fb_ref, ib_ref, gb_ref, lbl_ref, wn_ref, yb_ref, o_ref, st_ref, ST, qt_s, kh_s, dec_s, oi_s):
        @pl.when(pl.program_id(1) == 0)
        def _():
            ST[...] = jnp.zeros_like(ST)

        rc = _row_in_chunk()
        lb = _sigmoid(lbl_ref[0:1, :] - lbl_ref[1:2, :])
        _, _, lf, key, _, qf = _hgrn_gates(qb_ref[...], fb_ref[...], lb)
        v = ib_ref[...]
        b = _chunk_cumsum(lf, rc)
        rem = _chunk_rcumsum(lf, rc) - lf
        qt_s[...] = (qf * jnp.exp(b)).astype(bf16)
        kh_s[...] = (key * jnp.exp(rem)).astype(bf16)
        dec_s[...] = jnp.exp(b + rem)

        def step(c, carry):
            rows = pl.ds(pl.multiple_of(c * CHUNK, CHUNK), CHUNK)
            stv = ST[...]
            st_ref[c] = stv
            oi_s[rows, :] = _dot_nt(qt_s[rows, :], stv.astype(bf16))
            dec = dec_s[pl.ds(pl.multiple_of(c * CHUNK, CHUNK), 1), :]
            ST[...] = stv * dec + _dot_tn(ib_ref[rows, :].astype(bf16), kh_s[rows, :])
            return carry

        lax.fori_loop(0, HG_T // CHUNK, step, 0)

        ones = jnp.ones((HGRN_HD, HGRN_HD), bf16)
        o = oi_s[...]
        for l in range(CHUNK):
            if l == 0:
                pr, vs = qf * key, v
            else:
                e = jnp.exp(jnp.where(rc >= l, b - pltpu.roll(b, l, 0), NEG))
                pr, vs = qf * pltpu.roll(key, l, 0) * e, pltpu.roll(v, l, 0)
            o = o + _dot(pr.astype(bf16), ones) * vs
        o_ref[...] = o
        on = o * lax.rsqrt(jnp.mean(o * o, axis=-1, keepdims=True) + EPS)
        g = gb_ref[...]
        yb_ref[...] = on * wn_ref[...] * (g * _sigmoid(g))

    col, vec, lbs, tile, st = _hgrn_specs(nT, False)
    return pl.pallas_call(
        body, name="hgrn_fwd", grid=(HGRN_W // HGRN_HD, nT),
        in_specs=[col(0), col(1), col(2), col(3), lbs, vec],
        out_specs=[tile, tile, st],
        out_shape=[jax.ShapeDtypeStruct((S, HGRN_W), f32), jax.ShapeDtypeStruct((S, HGRN_W), f32),
                   jax.ShapeDtypeStruct((S // CHUNK, HGRN_W, HGRN_HD), f32)],
        scratch_shapes=[pltpu.VMEM((HGRN_HD, HGRN_HD), f32), pltpu.VMEM((HG_T, HGRN_HD), bf16),
                        pltpu.VMEM((HG_T, HGRN_HD), bf16), pltpu.VMEM((HG_T, HGRN_HD), f32),
                        pltpu.VMEM((HG_T, HGRN_HD), f32)],
        compiler_params=_cp("arbitrary", "arbitrary"),
    )(hp, hp, hp, hp, lbl, wn)


def _hgrn_bwd(hp, lbl, wn, o_sav, states, dmix):
    S = hp.shape[0]
    nT = S // HG_T

    def body(qb_ref, fb_ref, ib_ref, gb_ref, lbl_ref, wn_ref, o_ref, st_ref, dy_ref,
             dq_ref, df_ref, di_ref, dg_ref, gwn_ref, glb_ref,
             DST, qt_s, kh_s, dec_s, do_s, dqt_s, dkh_s, dvi_s, dbl_s):
        @pl.when(pl.program_id(1) == 0)
        def _():
            DST[...] = jnp.zeros_like(DST)
            gwn_ref[...] = jnp.zeros_like(gwn_ref)
            glb_ref[...] = jnp.zeros_like(glb_ref)

        rc = _row_in_chunk()
        lb = _sigmoid(lbl_ref[0:1, :] - lbl_ref[1:2, :])
        qb = qb_ref[...]
        sf, f, lf, key, sq, qf = _hgrn_gates(qb, fb_ref[...], lb)
        v = ib_ref[...]
        o = o_ref[...]
        rinv = lax.rsqrt(jnp.mean(o * o, axis=-1, keepdims=True) + EPS)
        on = o * rinv
        g = gb_ref[...]
        sgm = _sigmoid(g)
        silu_g = g * sgm
        dy = dy_ref[...]
        wn_v = wn_ref[...]
        gwn_ref[...] += jnp.sum(dy * on * silu_g, axis=0, keepdims=True)
        dg_ref[...] = (dy * on * wn_v * (sgm * (1.0 + g * (1.0 - sgm)))).astype(bf16)
        t1 = dy * wn_v * silu_g
        do = rinv * (t1 - on * jnp.mean(t1 * on, axis=-1, keepdims=True))
        do_s[...] = do.astype(bf16)

        b = _chunk_cumsum(lf, rc)
        rem = _chunk_rcumsum(lf, rc) - lf
        eb, er = jnp.exp(b), jnp.exp(rem)
        qt, kh = qf * eb, key * er
        qt_s[...] = qt.astype(bf16)
        kh_s[...] = kh.astype(bf16)
        dec_s[...] = jnp.exp(b + rem)

        def step(k, carry):
            c = HG_T // CHUNK - 1 - k
            rows = pl.ds(pl.multiple_of(c * CHUNK, CHUNK), CHUNK)
            stp = st_ref[c]
            dst = DST[...]
            dstb = dst.astype(bf16)
            dob = do_s[rows, :]
            khb = kh_s[rows, :]
            dec = dec_s[pl.ds(pl.multiple_of(c * CHUNK, CHUNK), 1), :]
            dqt_s[rows, :] = _dot(dob, stp.astype(bf16))
            dkh = _dot(ib_ref[rows, :].astype(bf16), dstb)
            dkh_s[rows, :] = dkh
            dvi_s[rows, :] = _dot_nt(khb, dstb)
            dbl = jnp.sum(dst * stp, axis=0, keepdims=True) * dec + jnp.sum(dkh * khb.astype(f32), axis=0, keepdims=True)
            dbl_s[rows, :] = jnp.broadcast_to(dbl, (CHUNK, HGRN_HD))
            DST[...] = dst * dec + _dot_tn(dob, qt_s[rows, :])
            return carry

        lax.fori_loop(0, HG_T // CHUNK, step, 0)

        dqt, dkh = dqt_s[...], dkh_s[...]
        dqf = dqt * eb
        dkey = dkh * er
        db = dqt * qt - dkh * kh + jnp.where(rc == CHUNK - 1, dbl_s[...], 0.0)
        dv = dvi_s[...]
        ones = jnp.ones((HGRN_HD, HGRN_HD), bf16)
        for l in range(CHUNK):
            if l == 0:
                e, ks, vs = None, key, v
                qe = qf
            else:
                e = jnp.exp(jnp.where(rc >= l, b - pltpu.roll(b, l, 0), NEG))
                ks, vs = pltpu.roll(key, l, 0), pltpu.roll(v, l, 0)
                qe = qf * e
            pr = qe * ks
            rl = _dot(pr.astype(bf16), ones)
            drl = _dot((do * vs).astype(bf16), ones)
            if l > 0:
                drl = jnp.where(rc >= l, drl, 0.0)
            gl = drl * pr
            dqf = dqf + drl * ks * (e if l > 0 else 1.0)
            if l == 0:
                dv = dv + rl * do
                dkey = dkey + drl * qe
            else:
                dv = dv + pltpu.roll(rl * do, HG_T - l, 0)
                dkey = dkey + pltpu.roll(drl * qe, HG_T - l, 0)
                db = db + gl - pltpu.roll(gl, HG_T - l, 0)
        dlf = _chunk_rcumsum(db, rc)
        df = dlf / f - dkey
        df_ref[...] = (df * (1.0 - lb) * sf * (1.0 - sf)).astype(bf16)
        glb_ref[...] += jnp.sum(df * (1.0 - sf), axis=0, keepdims=True)
        dq_ref[...] = (dqf * (sq * (1.0 + qb * (1.0 - sq)))).astype(bf16)
        di_ref[...] = dv.astype(bf16)

    col, vec, lbs, tile, st = _hgrn_specs(nT, True)
    dyspec = pl.BlockSpec((HG_T, HGRN_HD), lambda h, i: (nT - 1 - i, 4 + h))
    tb = lambda: pltpu.VMEM((HG_T, HGRN_HD), bf16)
    tf = lambda: pltpu.VMEM((HG_T, HGRN_HD), f32)
    dq, df, di, dg, gwn, glb = pl.pallas_call(
        body, name="hgrn_bwd", grid=(HGRN_W // HGRN_HD, nT),
        in_specs=[col(0), col(1), col(2), col(3), lbs, vec, tile, st, dyspec],
        out_specs=[tile, tile, tile, tile, vec, vec],
        out_shape=[jax.ShapeDtypeStruct((S, HGRN_W), bf16)] * 4 + [jax.ShapeDtypeStruct((1, HGRN_W), f32)] * 2,
        scratch_shapes=[pltpu.VMEM((HGRN_HD, HGRN_HD), f32), tb(), tb(), tf(), tb(), tf(), tf(), tf(), tf()],
        compiler_params=_cp("arbitrary", "arbitrary"),
    )(hp, hp, hp, hp, lbl, wn, o_sav, states, dmix)
    return dq, df, di, dg, gwn, glb


def _out_proj(x, ya, yb, wout, w2):
    S = x.shape[0]
    tm = 512

    def body(x_ref, ya_ref, yb_ref, w_ref, w2_ref, h1_ref, u2_ref, mix_ref):
        mixed = jnp.concatenate([ya_ref[...], yb_ref[...]], axis=1).astype(bf16)
        mix_ref[...] = mixed
        h1 = x_ref[...] + _dot(mixed, w_ref[...])
        h1_ref[...] = h1
        r = lax.rsqrt(jnp.mean(h1 * h1, axis=-1, keepdims=True) + EPS)
        u2_ref[...] = (h1 * r * w2_ref[...]).astype(bf16)

    row = lambda w: pl.BlockSpec((tm, w), lambda i: (i, 0))
    return pl.pallas_call(
        body, name="out_proj", grid=(S // tm,),
        in_specs=[row(D_MODEL), row(ATTN_W), row(HGRN_W), pl.BlockSpec((D_MODEL, D_MODEL), lambda i: (0, 0)),
                  pl.BlockSpec((1, D_MODEL), lambda i: (0, 0))],
        out_specs=[row(D_MODEL), row(D_MODEL), row(D_MODEL)],
        out_shape=[jax.ShapeDtypeStruct((S, D_MODEL), f32), jax.ShapeDtypeStruct((S, D_MODEL), bf16),
                   jax.ShapeDtypeStruct((S, D_MODEL), bf16)],
        compiler_params=_cp("arbitrary"),
    )(x, ya, yb, wout, w2)


def _gate_up(u2, wgu):
    S = u2.shape[0]
    tm, tn = 512, 1408
    nj = FFN // tn

    def body(u_ref, wg_ref, wu_ref, g_ref, up_ref, a_ref):
        u = u_ref[...]
        g = _dot(u, wg_ref[...])
        up = _dot(u, wu_ref[...])
        g_ref[...] = g.astype(bf16)
        up_ref[...] = up.astype(bf16)
        a_ref[...] = (g * _sigmoid(g) * up).astype(bf16)

    out = pl.BlockSpec((tm, tn), lambda j, i: (i, j))
    return pl.pallas_call(
        body, name="gate_up", grid=(nj, S // tm),
        in_specs=[pl.BlockSpec((tm, D_MODEL), lambda j, i: (i, 0)), pl.BlockSpec((D_MODEL, tn), lambda j, i: (0, j)),
                  pl.BlockSpec((D_MODEL, tn), lambda j, i: (0, j + nj))],
        out_specs=[out, out, out],
        out_shape=[jax.ShapeDtypeStruct((S, FFN), bf16)] * 3,
        compiler_params=_cp("arbitrary", "arbitrary"),
    )(u2, wgu, wgu)


def _rms_bwd(dyw, hn, r):
    return r * (dyw - hn * jnp.mean(dyw * hn, axis=-1, keepdims=True))


def _down_loss(act, wdown, h1, tgt, w3):
    S = act.shape[0]
    tm = 256

    def body(a_ref, w_ref, h1_ref, t_ref, w3_ref, dh2_ref, loss_ref, gw3_ref):
        @pl.when(pl.program_id(0) == 0)
        def _():
            loss_ref[...] = jnp.zeros_like(loss_ref)
            gw3_ref[...] = jnp.zeros_like(gw3_ref)

        h2 = h1_ref[...] + _dot(a_ref[...], w_ref[...])
        r = lax.rsqrt(jnp.mean(h2 * h2, axis=-1, keepdims=True) + EPS)
        hn = h2 * r
        w3 = w3_ref[...]
        err = hn * w3 - t_ref[...]
        loss_ref[...] += (0.5 / D_MODEL) * jnp.sum(err * err)
        dy = err * (1.0 / D_MODEL)
        gw3_ref[...] += jnp.sum(dy * hn, axis=0, keepdims=True)
        dh2_ref[...] = _rms_bwd(dy * w3, hn, r)

    row = lambda w: pl.BlockSpec((tm, w), lambda i: (i, 0))
    return pl.pallas_call(
        body, name="down_loss", grid=(S // tm,),
        in_specs=[row(FFN), pl.BlockSpec((FFN, D_MODEL), lambda i: (0, 0)), row(D_MODEL), row(D_MODEL),
                  pl.BlockSpec((1, D_MODEL), lambda i: (0, 0))],
        out_specs=[row(D_MODEL), pl.BlockSpec((1, 128), lambda i: (0, 0)), pl.BlockSpec((1, D_MODEL), lambda i: (0, 0))],
        out_shape=[jax.ShapeDtypeStruct((S, D_MODEL), f32), jax.ShapeDtypeStruct((1, 128), f32),
                   jax.ShapeDtypeStruct((1, D_MODEL), f32)],
        compiler_params=_cp("arbitrary"),
    )(act, wdown, h1, tgt, w3)


def _dact(dh2, wdown, gate, up):
    S = dh2.shape[0]
    tm = 256

    def body(d_ref, w_ref, g_ref, u_ref, dg_ref, du_ref):
        da = _dot_nt(d_ref[...].astype(bf16), w_ref[...])
        g = g_ref[...].astype(f32)
        sg = _sigmoid(g)
        du_ref[...] = (da * g * sg).astype(bf16)
        dg_ref[...] = (da * u_ref[...].astype(f32) * (sg * (1.0 + g * (1.0 - sg)))).astype(bf16)

    row = lambda w: pl.BlockSpec((tm, w), lambda i: (i, 0))
    return pl.pallas_call(
        body, name="dact", grid=(S // tm,),
        in_specs=[row(D_MODEL), pl.BlockSpec((FFN, D_MODEL), lambda i: (0, 0)), row(FFN), row(FFN)],
        out_specs=[row(FFN), row(FFN)],
        out_shape=[jax.ShapeDtypeStruct((S, FFN), bf16)] * 2,
        compiler_params=_cp("arbitrary"),
    )(dh2, wdown, gate, up)


def _dgu(dgate, dup, wgu, h1, w2, dh2):
    S = dgate.shape[0]
    tm = 256

    def body(dg_ref, du_ref, wg_ref, wu_ref, h1_ref, w2_ref, dh2_ref, dh1_ref, gw2_ref):
        @pl.when(pl.program_id(0) == 0)
        def _():
            gw2_ref[...] = jnp.zeros_like(gw2_ref)

        du2 = _dot_nt(dg_ref[...], wg_ref[...]) + _dot_nt(du_ref[...], wu_ref[...])
        h1 = h1_ref[...]
        r = lax.rsqrt(jnp.mean(h1 * h1, axis=-1, keepdims=True) + EPS)
        hn = h1 * r
        gw2_ref[...] += jnp.sum(du2 * hn, axis=0, keepdims=True)
        dh1_ref[...] = dh2_ref[...] + _rms_bwd(du2 * w2_ref[...], hn, r)

    row = lambda w: pl.BlockSpec((tm, w), lambda i: (i, 0))
    return pl.pallas_call(
        body, name="dgu", grid=(S // tm,),
        in_specs=[row(FFN), row(FFN), pl.BlockSpec((D_MODEL, FFN), lambda i: (0, 0)),
                  pl.BlockSpec((D_MODEL, FFN), lambda i: (0, 1)), row(D_MODEL),
                  pl.BlockSpec((1, D_MODEL), lambda i: (0, 0)), row(D_MODEL)],
        out_specs=[row(D_MODEL), pl.BlockSpec((1, D_MODEL), lambda i: (0, 0))],
        out_shape=[jax.ShapeDtypeStruct((S, D_MODEL), f32), jax.ShapeDtypeStruct((1, D_MODEL), f32)],
        compiler_params=_cp("arbitrary"),
    )(dgate, dup, wgu, wgu, h1, w2, dh2)


def _dmixed(dh1, wout):
    S = dh1.shape[0]
    tm = 512

    def body(d_ref, w_ref, o_ref):
        o_ref[...] = _dot_nt(d_ref[...].astype(bf16), w_ref[...])

    row = pl.BlockSpec((tm, D_MODEL), lambda i: (i, 0))
    return pl.pallas_call(
        body, name="dmixed", grid=(S // tm,),
        in_specs=[row, pl.BlockSpec((D_MODEL, D_MODEL), lambda i: (0, 0))], out_specs=row,
        out_shape=jax.ShapeDtypeStruct((S, D_MODEL), f32), compiler_params=_cp("arbitrary"),
    )(dh1, wout)


def _din(dq, dk, dv, dhq, dhf, dhi, dhg, cos_t, sg_t, win, x, w1, dh1):
    S = x.shape[0]
    tm = 256

    def body(dq_ref, dk_ref, dv_ref, dhq_ref, dhf_ref, dhi_ref, dhg_ref, cos_ref, sg_ref, w_ref, x_ref, w1_ref, dh1_ref,
             dp_ref, gx_ref, gw1_ref):
        @pl.when(pl.program_id(0) == 0)
        def _():
            gw1_ref[...] = jnp.zeros_like(gw1_ref)

        cosv, sgv = cos_ref[...], sg_ref[...]
        unrope = lambda d: d * cosv - sgv * _swap_halves(d)
        parts = [(unrope(dq_ref[...]) * (HEAD_DIM ** -0.5)).astype(bf16), unrope(dk_ref[...]).astype(bf16),
                 dv_ref[...].astype(bf16), dhq_ref[...], dhf_ref[...], dhi_ref[...], dhg_ref[...]]
        du = jnp.zeros((tm, D_MODEL), f32)
        for j, pj in enumerate(parts):
            dp_ref[:, j * 512:(j + 1) * 512] = pj
            du = du + _dot_nt(pj, w_ref[:, j * 512:(j + 1) * 512])
        xv = x_ref[...]
        r = lax.rsqrt(jnp.mean(xv * xv, axis=-1, keepdims=True) + EPS)
        xn = xv * r
        gw1_ref[...] += jnp.sum(du * xn, axis=0, keepdims=True)
        gx_ref[...] = dh1_ref[...] + _rms_bwd(du * w1_ref[...], xn, r)

    row = lambda w: pl.BlockSpec((tm, w), lambda i: (i, 0))
    vec = pl.BlockSpec((1, D_MODEL), lambda i: (0, 0))
    return pl.pallas_call(
        body, name="din", grid=(S // tm,),
        in_specs=[row(512)] * 7 + [row(512), row(512), pl.BlockSpec((D_MODEL, IN_W), lambda i: (0, 0)), row(D_MODEL), vec,
                                   row(D_MODEL)],
        out_specs=[row(IN_W), row(D_MODEL), vec],
        out_shape=[jax.ShapeDtypeStruct((S, IN_W), bf16), jax.ShapeDtypeStruct((S, D_MODEL), f32),
                   jax.ShapeDtypeStruct((1, D_MODEL), f32)],
        compiler_params=_cp("arbitrary"),
    )(dq, dk, dv, dhq, dhf, dhi, dhg, cos_t, sg_t, win, x, w1, dh1)


def _gw(a, b, tn, name):
    S, M = a.shape
    N = b.shape[1]
    ts = 512

    def body(a_ref, b_ref, o_ref):
        @pl.when(pl.program_id(1) == 0)
        def _():
            o_ref[...] = jnp.zeros_like(o_ref)

        o_ref[...] += _dot_tn(a_ref[...].astype(bf16), b_ref[...].astype(bf16))

    return pl.pallas_call(
        body, name=name, grid=(N // tn, S // ts),
        in_specs=[pl.BlockSpec((ts, M), lambda j, s: (s, 0)), pl.BlockSpec((ts, tn), lambda j, s: (s, j))],
        out_specs=pl.BlockSpec((M, tn), lambda j, s: (0, j)), out_shape=jax.ShapeDtypeStruct((M, N), f32),
        compiler_params=_cp("arbitrary", "arbitrary"),
    )(a, b)


def _local_step(x, tgt, w1, win, lbl, wn, wout, w2, wgu, wdown, w3):
    S = x.shape[0]
    cos_t, sg_t = _rope_tables(S)
    u, qkv, hp = _in_proj(x, w1, win, cos_t, sg_t)
    att = None
    for dil in DILATIONS:
        att = _attn_fwd(qkv, dil, att)
    ya, lse = att
    yb, o_sav, states = _hgrn_fwd(hp, lbl, wn)
    h1, u2, mixed = _out_proj(x, ya, yb, wout, w2)
    gate, up, act = _gate_up(u2, wgu)
    dh2, loss, g_w3 = _down_loss(act, wdown, h1, tgt, w3)

    g_wdown = _gw(act, dh2, 512, "gw_down")
    dgate, dup = _dact(dh2, wdown, gate, up)
    g_wgu = (_gw(u2, dgate, 1408, "gw_gate"), _gw(u2, dup, 1408, "gw_up"))
    dh1, g_w2 = _dgu(dgate, dup, wgu, h1, w2, dh2)
    g_wout = _gw(mixed, dh1, 1024, "gw_out")
    dmix = _dmixed(dh1, wout)
    dhq, dhf, dhi, dhg, g_wn, g_lb = _hgrn_bwd(hp, lbl, wn, o_sav, states, dmix)
    datt = None
    for dil in DILATIONS:
        datt = _attn_bwd(qkv, ya, lse, dmix, dil, datt)
    dproj, gx, g_w1 = _din(*datt, dhq, dhf, dhi, dhg, cos_t, sg_t, win, x, w1, dh1)
    g_win = _gw(u, dproj, 896, "gw_in")
    return loss, gx, (g_win, g_wout, g_wgu, g_wdown), (g_w1, g_lb, g_wn, g_w2, g_w3)


MESH = pl.DeviceIdType.MESH
ANY = pl.BlockSpec(memory_space=pl.ANY)
VMEM_SPEC = pl.BlockSpec(memory_space=pltpu.VMEM)


def _pos():
    return lax.axis_index("x"), lax.axis_index("y"), lax.axis_index("c")


def _flip(v, bit):
    return 1 - v if bit else v


def _all_gather(shards):
    n = len(shards)

    def body(*refs):
        ins, outs = refs[:n], refs[n:2 * n]
        send_sems, recv_sems, local_sems = refs[2 * n:]
        x, y, c = _pos()
        me, sibling = (x, y, c), (x, y, 1 - c)
        chips = [(1 - x, y), (x, 1 - y), (1 - x, 1 - y)]

        def copy(a, k, block, to, src=None):
            dst = outs[a].at[4 * block[0] + 2 * block[1] + block[2]]
            return pltpu.make_async_remote_copy(src_ref=dst if src is None else src, dst_ref=dst, send_sem=send_sems.at[a, k],
                                                recv_sem=recv_sems.at[a, k], device_id=to, device_id_type=MESH)

        local, sends = [], []
        for a in range(n):
            mine = pltpu.make_async_copy(ins[a], outs[a].at[4 * x + 2 * y + c], local_sems.at[a])
            mine.start()
            local.append(mine)
            first = [copy(a, 0, me, sibling, src=ins[a])] + [copy(a, 1 + j, me, (*chip, c), src=ins[a]) for j, chip in enumerate(chips)]
            for cp in first:
                cp.start()
            sends += first
        for a in range(n):
            for j, chip in enumerate(chips):
                copy(a, 1 + j, (*chip, c), me).wait_recv()
                passed = copy(a, 4 + j, (*chip, c), sibling)
                passed.start()
                sends.append(passed)
        for a in range(n):
            copy(a, 0, sibling, me).wait_recv()
            for j, chip in enumerate(chips):
                copy(a, 4 + j, (*chip, 1 - c), me).wait_recv()
        for cp in sends:
            cp.wait_send()
        for mine in local:
            mine.wait()

    return pl.pallas_call(
        body, name="gather_weights", in_specs=[ANY] * n, out_specs=[ANY] * n,
        out_shape=[jax.ShapeDtypeStruct((N_DEV,) + s.shape, s.dtype) for s in shards],
        scratch_shapes=[pltpu.SemaphoreType.DMA((n, 7)), pltpu.SemaphoreType.DMA((n, 7)), pltpu.SemaphoreType.DMA((n,))],
    )(*shards)


def _rs_sibling(grads):
    n = len(grads)

    def body(*refs):
        g, got, own = refs[:n], refs[n:2 * n], refs[2 * n:3 * n]
        send_sems, recv_sems, local_sems = refs[3 * n:]
        x, y, c = _pos()
        copies, local = [], []
        for a in range(n):
            for q in range(4):
                cp = pltpu.make_async_remote_copy(src_ref=g[a].at[2 * q + (1 - c)], dst_ref=got[a].at[q], send_sem=send_sems.at[a, q],
                                                  recv_sem=recv_sems.at[a, q], device_id=(x, y, 1 - c), device_id_type=MESH)
                cp.start()
                copies.append(cp)
                mine = pltpu.make_async_copy(g[a].at[2 * q + c], own[a].at[q], local_sems.at[a, q])
                mine.start()
                local.append(mine)
        for cp in copies:
            cp.wait()
        for mine in local:
            mine.wait()

    half = [jax.ShapeDtypeStruct((4,) + g.shape[1:], g.dtype) for g in grads]
    outs = pl.pallas_call(
        body, name="reduce_sibling", in_specs=[ANY] * n, out_specs=[ANY] * (2 * n), out_shape=half + half,
        scratch_shapes=[pltpu.SemaphoreType.DMA((n, 4))] * 3,
    )(*grads)
    return outs[:n], outs[n:]


def _rs_chips(sums):
    n = len(sums)

    def body(*refs):
        s, out = refs[:n], refs[n:2 * n]
        send_sems, recv_sems, local_sems = refs[2 * n:]
        x, y, c = _pos()
        qme = 2 * x + y
        sends, recvs, local = [], [], []
        for a in range(n):
            mine = pltpu.make_async_copy(s[a].at[qme], out[a].at[qme], local_sems.at[a])
            mine.start()
            local.append(mine)
            for f in (1, 2, 3):
                peer = (_flip(x, f >> 1), _flip(y, f & 1), c)
                qd = 2 * peer[0] + peer[1]
                cp = pltpu.make_async_remote_copy(src_ref=s[a].at[qd], dst_ref=out[a].at[qme], send_sem=send_sems.at[a, f - 1],
                                                  recv_sem=recv_sems.at[a, f - 1], device_id=peer, device_id_type=MESH)
                cp.start()
                sends.append(cp)
                recvs.append(pltpu.make_async_remote_copy(src_ref=s[a].at[qme], dst_ref=out[a].at[qd], send_sem=send_sems.at[a, f - 1],
                                                          recv_sem=recv_sems.at[a, f - 1], device_id=peer, device_id_type=MESH))
        for cp in recvs:
            cp.wait_recv()
        for cp in sends:
            cp.wait_send()
        for mine in local:
            mine.wait()

    return pl.pallas_call(
        body, name="reduce_chips", in_specs=[ANY] * n, out_specs=[ANY] * n,
        out_shape=[jax.ShapeDtypeStruct(s.shape, s.dtype) for s in sums],
        scratch_shapes=[pltpu.SemaphoreType.DMA((n, 3)), pltpu.SemaphoreType.DMA((n, 3)), pltpu.SemaphoreType.DMA((n,))],
    )(*sums)


def _gather_small(g_w1, g_w2, g_w3, g_lb, g_wn, loss):
    def body(w1_ref, w2_ref, w3_ref, lb_ref, wn_ref, loss_ref, out_ref, pk, send_sems, recv_sems):
        x, y, c = _pos()
        me = 4 * x + 2 * y + c
        pk[...] = jnp.zeros_like(pk)
        pk[0:1, :] = w1_ref[...]
        pk[1:2, :] = w2_ref[...]
        pk[2:3, :] = w3_ref[...]
        pk[3:4, 0:HGRN_W] = lb_ref[...]
        pk[3:4, HGRN_W:2 * HGRN_W] = wn_ref[...]
        pk[4:5, 0:128] = loss_ref[...]
        out_ref[me] = pk[...]
        sends, recvs = [], []
        for k in range(1, N_DEV):
            peer = (_flip(x, k >> 2), _flip(y, (k >> 1) & 1), _flip(c, k & 1))
            cp = pltpu.make_async_remote_copy(src_ref=pk, dst_ref=out_ref.at[me], send_sem=send_sems.at[k - 1],
                                              recv_sem=recv_sems.at[k - 1], device_id=peer, device_id_type=MESH)
            cp.start()
            sends.append(cp)
            recvs.append(pltpu.make_async_remote_copy(src_ref=pk, dst_ref=out_ref.at[4 * peer[0] + 2 * peer[1] + peer[2]],
                                                      send_sem=send_sems.at[k - 1], recv_sem=recv_sems.at[k - 1], device_id=peer,
                                                      device_id_type=MESH))
        for cp in recvs:
            cp.wait_recv()
        for cp in sends:
            cp.wait_send()

    return pl.pallas_call(
        body, name="gather_small", in_specs=[VMEM_SPEC] * 6, out_specs=VMEM_SPEC,
        out_shape=jax.ShapeDtypeStruct((N_DEV, 8, D_MODEL), f32),
        scratch_shapes=[pltpu.VMEM((8, D_MODEL), f32), pltpu.SemaphoreType.DMA((N_DEV - 1,)), pltpu.SemaphoreType.DMA((N_DEV - 1,))],
    )(g_w1, g_w2, g_w3, g_lb, g_wn, loss)


def _row_tile(r):
    return max(t for t in range(8, 257, 8) if r % t == 0)


def _add(a, b, name):
    k, r, c = a.shape
    tr = _row_tile(r)

    def body(a_ref, b_ref, o_ref):
        o_ref[...] = a_ref[...] + b_ref[...]

    blk = pl.BlockSpec((1, tr, c), lambda q, i: (q, i, 0))
    return pl.pallas_call(body, name=name, grid=(k, r // tr), in_specs=[blk, blk], out_specs=blk,
                          out_shape=jax.ShapeDtypeStruct(a.shape, a.dtype), compiler_params=_cp("arbitrary", "arbitrary"))(a, b)


def _adamw(w, g, m, v):
    m = ADAM_B1 * m + (1.0 - ADAM_B1) * g
    v = ADAM_B2 * v + (1.0 - ADAM_B2) * (g * g)
    m_hat = m / (1.0 - ADAM_B1 ** ADAM_STEP)
    v_hat = v / (1.0 - ADAM_B2 ** ADAM_STEP)
    return -ADAM_LR * (m_hat / (jnp.sqrt(v_hat) + ADAM_EPS) + ADAM_WD * w), m, v


def _adam_shard(pieces, w, m, v, name):
    r, c = w.shape
    tr = _row_tile(r)

    def body(p_ref, w_ref, m_ref, v_ref, g_out, d_out, m_out, v_out):
        g = ((p_ref[0] + p_ref[1]) + p_ref[2]) + p_ref[3]
        g_out[...] = g
        d_out[...], m_out[...], v_out[...] = _adamw(w_ref[...], g, m_ref[...], v_ref[...])

    blk = pl.BlockSpec((tr, c), lambda i: (i, 0))
    return pl.pallas_call(
        body, name=name, grid=(r // tr,), in_specs=[pl.BlockSpec((4, tr, c), lambda i: (0, i, 0)), blk, blk, blk],
        out_specs=[blk] * 4, out_shape=[jax.ShapeDtypeStruct((r, c), f32)] * 4, compiler_params=_cp("arbitrary"),
    )(pieces, w, m, v)


def _small_update(gath, params):
    def body(gath_ref, *refs):
        ins, outs = refs[:15], refs[15:]
        gs = gath_ref[0]
        for k in range(1, N_DEV):
            gs = gs + gath_ref[k]
        outs[0][...] = gs[4:5, 0:128]
        l0, l1 = ins[9][0:1, :], ins[9][1:2, :]
        lb = _sigmoid(l0 - l1)
        d0 = gs[3:4, 0:HGRN_W] * lb * (1.0 - lb)
        first_row = lax.broadcasted_iota(jnp.int32, (2, HGRN_W), 0) == 0
        grads = [gs[0:1, :], gs[1:2, :], gs[2:3, :], jnp.where(first_row, d0, -d0), gs[3:4, HGRN_W:2 * HGRN_W]]
        for i, g in enumerate(grads):
            w_ref, m_ref, v_ref = ins[3 * i:3 * i + 3]
            o = outs[1 + 4 * i:5 + 4 * i]
            o[0][...] = g
            o[1][...], o[2][...], o[3][...] = _adamw(w_ref[...], g, m_ref[...], v_ref[...])

    flat = [a for p in params for a in p]
    out_shape = [jax.ShapeDtypeStruct((1, 128), f32)] + [jax.ShapeDtypeStruct(p[0].shape, f32) for p in params for _ in range(4)]
    outs = pl.pallas_call(body, name="small_update", in_specs=[VMEM_SPEC] * 16, out_specs=[VMEM_SPEC] * 21, out_shape=out_shape)(gath, *flat)
    return outs[0], [outs[1 + 4 * i:5 + 4 * i] for i in range(5)]


def kernel(x, norm1_w, w_in, lb_logits, hgrn_norm_w, w_out, norm2_w, w_gate_up, w_down, final_norm_w, loss_target, m_norm1_w, m_w_in, m_lb_logits, m_hgrn_norm_w, m_w_out, m_norm2_w, m_w_gate_up, m_w_down, m_final_norm_w, v_norm1_w, v_w_in, v_lb_logits, v_hgrn_norm_w, v_w_out, v_norm2_w, v_w_gate_up, v_w_down, v_final_norm_w):
    row = lambda a: a.reshape(1, D_MODEL)
    shards = [w_in[0], w_out[0], w_gate_up[0], w_down[0]]
    win_g, wout_g, wgu_g, wdown_g = _all_gather([s.astype(bf16) for s in shards])
    win = jnp.transpose(win_g, (1, 0, 2)).reshape(D_MODEL, IN_W)
    wgu = jnp.transpose(wgu_g, (1, 0, 2)).reshape(D_MODEL, 2 * FFN)
    wout = wout_g.reshape(D_MODEL, D_MODEL)
    wdown = wdown_g.reshape(FFN, D_MODEL)

    loss_p, gx, (g_win, g_wout, g_wgu, g_wdown), (g_w1, g_lb, g_wn, g_w2, g_w3) = _local_step(
        x[0], loss_target[0], norm1_w, win, lb_logits, hgrn_norm_w, wout, norm2_w, wgu, wdown, row(final_norm_w))

    by_owner = lambda g, w: jnp.transpose(g.reshape(g.shape[0], g.shape[1] // w, w), (1, 0, 2))
    grads = [by_owner(g_win, IN_W // N_DEV), g_wout.reshape(N_DEV, D_MODEL // N_DEV, D_MODEL),
             jnp.concatenate([by_owner(g, 2 * FFN // N_DEV) for g in g_wgu], axis=0), g_wdown.reshape(N_DEV, FFN // N_DEV, D_MODEL)]
    got, own = _rs_sibling(grads)
    sums = [_add(a, b, f"add_sibling_{i}") for i, (a, b) in enumerate(zip(own, got))]
    pieces = _rs_chips(sums)
    moms = [(m_w_in[0], v_w_in[0]), (m_w_out[0], v_w_out[0]), (m_w_gate_up[0], v_w_gate_up[0]), (m_w_down[0], v_w_down[0])]
    big = [_adam_shard(p, w, m, v, f"adam_{i}") for i, (p, w, (m, v)) in enumerate(zip(pieces, shards, moms))]
    big = [[a[None] for a in four] for four in big]

    gath = _gather_small(g_w1, g_w2, g_w3, g_lb, g_wn, loss_p)
    params = [(norm1_w, m_norm1_w, v_norm1_w), (norm2_w, m_norm2_w, v_norm2_w),
              (row(final_norm_w), row(m_final_norm_w), row(v_final_norm_w)),
              (lb_logits, m_lb_logits, v_lb_logits), (hgrn_norm_w, m_hgrn_norm_w, v_hgrn_norm_w)]
    loss, (s_w1, s_w2, s_w3, s_lb, s_wn) = _small_update(gath, params)
    s_w3 = [a.reshape(D_MODEL) for a in s_w3]
    per_w = [s_w1, big[0], s_lb, s_wn, big[1], s_w2, big[2], big[3], s_w3]
    return (loss[0, 0], gx[None], *[p[0] for p in per_w], *[p[1] for p in per_w], *[p[2] for p in per_w], *[p[3] for p in per_w])
```

```python
import functools

import jax
import jax.numpy as jnp
from jax import lax
from jax.experimental import pallas as pl
from jax.experimental.pallas import tpu as pltpu

f32, bf16 = jnp.float32, jnp.bfloat16

D_MODEL = 1024
ATTN_W = 512
HEAD_DIM = 64
ATTN_BLK = 128
DILATIONS = (1, 4, 16)
HGRN_W = 512
HGRN_HD = 128
CHUNK = 16
IN_W = 3 * ATTN_W + 4 * HGRN_W
FFN = 2816
EPS = 1e-6
ROPE_THETA = 10000.0
NEG = -1e30
N_DEV = 8
ADAM_LR, ADAM_B1, ADAM_B2, ADAM_EPS, ADAM_WD, ADAM_STEP = 0.001, 0.9, 0.999, 1e-08, 0.01, 10
VMEM_LIMIT = 56 * 1024 * 1024


def _cp(*sem):
    return pltpu.CompilerParams(dimension_semantics=sem, vmem_limit_bytes=VMEM_LIMIT)


def _dot(a, b):
    return jnp.dot(a, b, preferred_element_type=f32)


def _dot_nt(a, b):
    return lax.dot_general(a, b, (((1,), (1,)), ((), ())), preferred_element_type=f32)


def _dot_tn(a, b):
    return lax.dot_general(a, b, (((0,), (0,)), ((), ())), preferred_element_type=f32)


def _sigmoid(x):
    return 1.0 / (1.0 + jnp.exp(-x))


def _rope_tables(S):
    half = HEAD_DIM // 2
    inv_freq = ROPE_THETA ** (-jnp.arange(half, dtype=f32) / half)
    ang = jnp.arange(S, dtype=f32)[:, None] * inv_freq[None, :]
    cos, sin = jnp.cos(ang), jnp.sin(ang)
    cos_t = jnp.tile(jnp.concatenate([cos, cos], axis=1), (1, ATTN_W // HEAD_DIM))
    sg_t = jnp.tile(jnp.concatenate([-sin, sin], axis=1), (1, ATTN_W // HEAD_DIM))
    return cos_t, sg_t


def _swap_halves(v):
    n = v.shape[1]
    lane = lax.broadcasted_iota(jnp.int32, v.shape, 1)
    return jnp.where((lane % HEAD_DIM) < HEAD_DIM // 2, pltpu.roll(v, n - HEAD_DIM // 2, 1), pltpu.roll(v, HEAD_DIM // 2, 1))


def _in_proj(x, w1, win, cos_t, sg_t):
    S = x.shape[0]
    tm = 256

    def body(x_ref, w1_ref, w_ref, cos_ref, sg_ref, u_ref, qkv_ref, hp_ref):
        xv = x_ref[...]
        r = lax.rsqrt(jnp.mean(xv * xv, axis=-1, keepdims=True) + EPS)
        u = (xv * r * w1_ref[...]).astype(bf16)
        u_ref[...] = u
        cosv, sgv = cos_ref[...], sg_ref[...]
        for j in range(3):
            pj = _dot(u, w_ref[:, j * ATTN_W:(j + 1) * ATTN_W])
            if j < 2:
                pj = pj * cosv + _swap_halves(pj) * sgv
            if j == 0:
                pj = pj * (HEAD_DIM ** -0.5)
            qkv_ref[:, j * ATTN_W:(j + 1) * ATTN_W] = pj.astype(bf16)
        for j in range(4):
            lo = 3 * ATTN_W + j * HGRN_W
            hp_ref[:, j * HGRN_W:(j + 1) * HGRN_W] = _dot(u, w_ref[:, lo:lo + HGRN_W])

    return pl.pallas_call(
        body, name="in_proj", grid=(S // tm,),
        in_specs=[pl.BlockSpec((tm, D_MODEL), lambda i: (i, 0)), pl.BlockSpec((1, D_MODEL), lambda i: (0, 0)),
                  pl.BlockSpec((D_MODEL, IN_W), lambda i: (0, 0)),
                  pl.BlockSpec((tm, ATTN_W), lambda i: (i, 0)), pl.BlockSpec((tm, ATTN_W), lambda i: (i, 0))],
        out_specs=[pl.BlockSpec((tm, D_MODEL), lambda i: (i, 0)), pl.BlockSpec((tm, 3 * ATTN_W), lambda i: (i, 0)),
                   pl.BlockSpec((tm, 4 * HGRN_W), lambda i: (i, 0))],
        out_shape=[jax.ShapeDtypeStruct((S, D_MODEL), bf16), jax.ShapeDtypeStruct((S, 3 * ATTN_W), bf16),
                   jax.ShapeDtypeStruct((S, 4 * HGRN_W), f32)],
        compiler_params=_cp("arbitrary"),
    )(x, w1, win, cos_t, sg_t)


def _valid_mask(n):
    qi = lax.broadcasted_iota(jnp.int32, (ATTN_BLK, 2 * ATTN_BLK), 0)
    kj = lax.broadcasted_iota(jnp.int32, (ATTN_BLK, 2 * ATTN_BLK), 1)
    delta = ATTN_BLK + qi - kj
    return (delta >= 0) & (delta <= ATTN_BLK) & ((n > 0) | (kj >= ATTN_BLK))


def _head_masks():
    lane = lax.broadcasted_iota(jnp.int32, (ATTN_BLK, 128), 1)
    even = lane < HEAD_DIM
    return even, (even, jnp.logical_not(even))


def _attn_fwd(qkv, dil, prev):
    S = qkv.shape[0]
    L = S // dil
    nb = L // ATTN_BLK
    W3 = 3 * ATTN_W
    has_prev = prev is not None

    def body(*refs):
        if has_prev:
            q_ref, kp_ref, kc_ref, vp_ref, vc_ref, yp_ref, lp_ref, y_ref, l_ref = refs
        else:
            q_ref, kp_ref, kc_ref, vp_ref, vc_ref, y_ref, l_ref = refs
        n = pl.program_id(1)
        valid = _valid_mask(n)
        even, masks = _head_masks()
        for p in range(ATTN_W // 128):
            sl = slice(128 * p, 128 * p + 128)
            q2 = q_ref[:, sl].astype(f32)
            k2 = jnp.concatenate([kp_ref[:, sl], kc_ref[:, sl]], axis=0)
            v2 = jnp.concatenate([vp_ref[:, sl], vc_ref[:, sl]], axis=0)
            outs, lses = [], []
            for e in range(2):
                qm = jnp.where(masks[e], q2, 0.0).astype(bf16)
                s = jnp.where(valid, _dot_nt(qm, k2), NEG)
                m = jnp.max(s, axis=-1, keepdims=True)
                pe = jnp.exp(s - m)
                lsum = jnp.sum(pe, axis=-1, keepdims=True)
                acc = _dot(pe.astype(bf16), v2)
                outs.append(acc / lsum)
                lses.append(jnp.broadcast_to(m + jnp.log(lsum), (ATTN_BLK, 128)))
            out = jnp.where(even, outs[0], outs[1])
            lse = jnp.where(even, lses[0], lses[1])
            if has_prev:
                lp = lp_ref[:, sl]
                mx = jnp.maximum(lp, lse)
                a, b = jnp.exp(lp - mx), jnp.exp(lse - mx)
                tot = a + b
                out = (a * yp_ref[:, sl] + b * out) / tot
                lse = mx + jnp.log(tot)
            y_ref[:, sl] = out
            l_ref[:, sl] = lse

    blk = (ATTN_BLK, ATTN_W)
    cur = lambda c: (lambda r, n: (n, 3 * r + c))
    prv = lambda c: (lambda r, n: (jnp.maximum(n - 1, 0), 3 * r + c))
    in_specs = [pl.BlockSpec(blk, cur(0)), pl.BlockSpec(blk, prv(1)), pl.BlockSpec(blk, cur(1)),
                pl.BlockSpec(blk, prv(2)), pl.BlockSpec(blk, cur(2))]
    qv = qkv.reshape(L, dil * W3)
    args = [qv, qv, qv, qv, qv]
    if has_prev:
        in_specs += [pl.BlockSpec(blk, lambda r, n: (n, r))] * 2
        args += [prev[0].reshape(L, dil * ATTN_W), prev[1].reshape(L, dil * ATTN_W)]
    y, lse = pl.pallas_call(
        body, name=f"attn_fwd_d{dil}", grid=(dil, nb), in_specs=in_specs,
        out_specs=[pl.BlockSpec(blk, lambda r, n: (n, r))] * 2,
        out_shape=[jax.ShapeDtypeStruct((L, dil * ATTN_W), f32)] * 2,
        compiler_params=_cp("arbitrary", "arbitrary"),
    )(*args)
    return y.reshape(S, ATTN_W), lse.reshape(S, ATTN_W)


def _attn_bwd(qkv, ya, lse, dmix, dil, prev):
    S = qkv.shape[0]
    L = S // dil
    nb = L // ATTN_BLK
    W3 = 3 * ATTN_W
    has_prev = prev is not None

    def body(*refs):
        q_ref, kp_ref, kc_ref, vp_ref, vc_ref, y_ref, l_ref, dy_ref = refs[:8]
        refs = refs[8:]
        if has_prev:
            dqp_ref, dkp_ref, dvp_ref = refs[:3]
            refs = refs[3:]
        dq_ref, dk_ref, dv_ref, ck, cv = refs
        n = pl.program_id(1)

        @pl.when(n == 0)
        def _():
            ck[...] = jnp.zeros_like(ck)
            cv[...] = jnp.zeros_like(cv)

        @pl.when(n < nb)
        def _():
            valid = _valid_mask(n)
            even, masks = _head_masks()
            li = lax.broadcasted_iota(jnp.int32, (128, 128), 0)
            lj = lax.broadcasted_iota(jnp.int32, (128, 128), 1)
            seg = jnp.where((li // HEAD_DIM) == (lj // HEAD_DIM), 1.0, 0.0).astype(bf16)
            for p in range(ATTN_W // 128):
                sl = slice(128 * p, 128 * p + 128)
                q2 = q_ref[:, sl].astype(f32)
                k2 = jnp.concatenate([kp_ref[:, sl], kc_ref[:, sl]], axis=0)
                v2 = jnp.concatenate([vp_ref[:, sl], vc_ref[:, sl]], axis=0)
                k2f = k2.astype(f32)
                dy2 = dy_ref[:, sl]
                lse2 = l_ref[:, sl]
                dyy = dy2 * y_ref[:, sl]
                hi = dyy.astype(bf16)
                lo = (dyy - hi.astype(f32)).astype(bf16)
                delta2 = _dot(hi, seg) + _dot(lo, seg)
                dq2 = jnp.zeros((ATTN_BLK, 128), f32)
                dk2 = jnp.zeros((2 * ATTN_BLK, 128), f32)
                dv2 = jnp.zeros((2 * ATTN_BLK, 128), f32)
                for e in range(2):
                    c0 = e * HEAD_DIM
                    qm = jnp.where(masks[e], q2, 0.0).astype(bf16)
                    km = jnp.where(masks[e][:1, :], k2f, 0.0).astype(bf16)
                    dym = jnp.where(masks[e], dy2, 0.0).astype(bf16)
                    s = _dot_nt(qm, k2)
                    pe = jnp.where(valid, jnp.exp(s - lse2[:, c0:c0 + 1]), 0.0)
                    dp = _dot_nt(dym, v2)
                    ds = (pe * (dp - delta2[:, c0:c0 + 1])).astype(bf16)
                    dv2 = dv2 + _dot_tn(pe.astype(bf16), dym)
                    dq2 = dq2 + _dot(ds, km)
                    dk2 = dk2 + _dot_tn(ds, qm)
                tk = dk2[:ATTN_BLK] + ck[:, sl]
                tv = dv2[:ATTN_BLK] + cv[:, sl]
                if has_prev:
                    dq2 = dq2 + dqp_ref[:, sl]
                    tk = tk + dkp_ref[:, sl]
                    tv = tv + dvp_ref[:, sl]
                dq_ref[:, sl] = dq2
                dk_ref[:, sl] = tk
                dv_ref[:, sl] = tv
                ck[:, sl] = dk2[ATTN_BLK:]
                cv[:, sl] = dv2[ATTN_BLK:]

        @pl.when(n == nb)
        def _():
            if has_prev:
                dk_ref[...] = ck[...] + dkp_ref[...]
                dv_ref[...] = cv[...] + dvp_ref[...]
            else:
                dk_ref[...] = ck[...]
                dv_ref[...] = cv[...]

    blk = (ATTN_BLK, ATTN_W)
    cn = lambda n: jnp.minimum(n, nb - 1)
    pn = lambda n: jnp.clip(n - 1, 0, nb - 1)
    in_specs = [pl.BlockSpec(blk, lambda r, n: (cn(n), 3 * r)),
                pl.BlockSpec(blk, lambda r, n: (pn(n), 3 * r + 1)), pl.BlockSpec(blk, lambda r, n: (cn(n), 3 * r + 1)),
                pl.BlockSpec(blk, lambda r, n: (pn(n), 3 * r + 2)), pl.BlockSpec(blk, lambda r, n: (cn(n), 3 * r + 2)),
                pl.BlockSpec(blk, lambda r, n: (cn(n), r)), pl.BlockSpec(blk, lambda r, n: (cn(n), r)),
                pl.BlockSpec(blk, lambda r, n: (cn(n), 2 * r))]
    qv = qkv.reshape(L, dil * W3)
    view = lambda a: a.reshape(L, dil * ATTN_W)
    args = [qv, qv, qv, qv, qv, view(ya), view(lse), dmix.reshape(L, dil * D_MODEL)]
    if has_prev:
        in_specs += [pl.BlockSpec(blk, lambda r, n: (cn(n), r)), pl.BlockSpec(blk, lambda r, n: (pn(n), r)),
                     pl.BlockSpec(blk, lambda r, n: (pn(n), r))]
        args += [view(a) for a in prev]
    outs = pl.pallas_call(
        body, name=f"attn_bwd_d{dil}", grid=(dil, nb + 1), in_specs=in_specs,
        out_specs=[pl.BlockSpec(blk, lambda r, n: (cn(n), r)), pl.BlockSpec(blk, lambda r, n: (pn(n), r)),
                   pl.BlockSpec(blk, lambda r, n: (pn(n), r))],
        out_shape=[jax.ShapeDtypeStruct((L, dil * ATTN_W), f32)] * 3,
        scratch_shapes=[pltpu.VMEM(blk, f32), pltpu.VMEM(blk, f32)],
        compiler_params=_cp("arbitrary", "arbitrary"),
    )(*args)
    return tuple(o.reshape(S, ATTN_W) for o in outs)


HG_T = 256


def _row_in_chunk():
    return lax.broadcasted_iota(jnp.int32, (HG_T, HGRN_HD), 0) % CHUNK


def _chunk_cumsum(v, rc):
    for k in (1, 2, 4, 8):
        v = v + jnp.where(rc >= k, pltpu.roll(v, k, 0), 0.0)
    return v


def _chunk_rcumsum(v, rc):
    for k in (1, 2, 4, 8):
        v = v + jnp.where(rc < CHUNK - k, pltpu.roll(v, HG_T - k, 0), 0.0)
    return v


def _hgrn_gates(qb, fb, lb):
    sf = _sigmoid(fb)
    f = lb + (1.0 - lb) * sf
    sq = _sigmoid(qb)
    return sf, f, jnp.log(f), 1.0 - f, sq, qb * sq


def _hgrn_specs(nT, rev):
    ti = (lambda i: nT - 1 - i) if rev else (lambda i: i)
    col = lambda c: pl.BlockSpec((HG_T, HGRN_HD), lambda h, i: (ti(i), 4 * c + h))
    vec = pl.BlockSpec((1, HGRN_HD), lambda h, i: (0, h))
    lbs = pl.BlockSpec((2, HGRN_HD), lambda h, i: (0, h))
    tile = pl.BlockSpec((HG_T, HGRN_HD), lambda h, i: (ti(i), h))
    st = pl.BlockSpec((HG_T // CHUNK, HGRN_HD, HGRN_HD), lambda h, i: (ti(i), h, 0))
    return col, vec, lbs, tile, st


def _hgrn_fwd(hp, lbl, wn):
    S = hp.shape[0]
    nT = S // HG_T

    def body(qb_ref, fb_ref, ib_ref, gb_ref, lbl_ref, wn_ref, yb_ref, o_ref, st_ref, ST, qt_s, kh_s, dec_s, oi_s):
        @pl.when(pl.program_id(1) == 0)
        def _():
            ST[...] = jnp.zeros_like(ST)

        rc = _row_in_chunk()
        lb = _sigmoid(lbl_ref[0:1, :] - lbl_ref[1:2, :])
        _, _, lf, key, _, qf = _hgrn_gates(qb_ref[...], fb_ref[...], lb)
        v = ib_ref[...]
        b = _chunk_cumsum(lf, rc)
        rem = _chunk_rcumsum(lf, rc) - lf
        qt_s[...] = (qf * jnp.exp(b)).astype(bf16)
        kh_s[...] = (key * jnp.exp(rem)).astype(bf16)
        dec_s[...] = jnp.exp(b + rem)

        def step(c, carry):
            rows = pl.ds(pl.multiple_of(c * CHUNK, CHUNK), CHUNK)
            stv = ST[...]
            st_ref[c] = stv
            oi_s[rows, :] = _dot_nt(qt_s[rows, :], stv.astype(bf16))
            dec = dec_s[pl.ds(pl.multiple_of(c * CHUNK, CHUNK), 1), :]
            ST[...] = stv * dec + _dot_tn(ib_ref[rows, :].astype(bf16), kh_s[rows, :])
            return carry

        lax.fori_loop(0, HG_T // CHUNK, step, 0)

        ones = jnp.ones((HGRN_HD, HGRN_HD), bf16)
        o = oi_s[...]
        for l in range(CHUNK):
            if l == 0:
                pr, vs = qf * key, v
            else:
                e = jnp.exp(jnp.where(rc >= l, b - pltpu.roll(b, l, 0), NEG))
                pr, vs = qf * pltpu.roll(key, l, 0) * e, pltpu.roll(v, l, 0)
            o = o + _dot(pr.astype(bf16), ones) * vs
        o_ref[...] = o
        on = o * lax.rsqrt(jnp.mean(o * o, axis=-1, keepdims=True) + EPS)
        g = gb_ref[...]
        yb_ref[...] = on * wn_ref[...] * (g * _sigmoid(g))

    col, vec, lbs, tile, st = _hgrn_specs(nT, False)
    return pl.pallas_call(
        body, name="hgrn_fwd", grid=(HGRN_W // HGRN_HD, nT),
        in_specs=[col(0), col(1), col(2), col(3), lbs, vec],
        out_specs=[tile, tile, st],
        out_shape=[jax.ShapeDtypeStruct((S, HGRN_W), f32), jax.ShapeDtypeStruct((S, HGRN_W), f32),
                   jax.ShapeDtypeStruct((S // CHUNK, HGRN_W, HGRN_HD), f32)],
        scratch_shapes=[pltpu.VMEM((HGRN_HD, HGRN_HD), f32), pltpu.VMEM((HG_T, HGRN_HD), bf16),
                        pltpu.VMEM((HG_T, HGRN_HD), bf16), pltpu.VMEM((HG_T, HGRN_HD), f32),
                        pltpu.VMEM((HG_T, HGRN_HD), f32)],
        compiler_params=_cp("arbitrary", "arbitrary"),
    )(hp, hp, hp, hp, lbl, wn)


def _hgrn_bwd(hp, lbl, wn, o_sav, states, dmix):
    S = hp.shape[0]
    nT = S // HG_T

    def body(qb_ref, fb_ref, ib_ref, gb_ref, lbl_ref, wn_ref, o_ref, st_ref, dy_ref,
             dq_ref, df_ref, di_ref, dg_ref, gwn_ref, glb_ref,
             DST, qt_s, kh_s, dec_s, do_s, dqt_s, dkh_s, dvi_s, dbl_s):
        @pl.when(pl.program_id(1) == 0)
        def _():
            DST[...] = jnp.zeros_like(DST)
            gwn_ref[...] = jnp.zeros_like(gwn_ref)
            glb_ref[...] = jnp.zeros_like(glb_ref)

        rc = _row_in_chunk()
        lb = _sigmoid(lbl_ref[0:1, :] - lbl_ref[1:2, :])
        qb = qb_ref[...]
        sf, f, lf, key, sq, qf = _hgrn_gates(qb, fb_ref[...], lb)
        v = ib_ref[...]
        o = o_ref[...]
        rinv = lax.rsqrt(jnp.mean(o * o, axis=-1, keepdims=True) + EPS)
        on = o * rinv
        g = gb_ref[...]
        sgm = _sigmoid(g)
        silu_g = g * sgm
        dy = dy_ref[...]
        wn_v = wn_ref[...]
        gwn_ref[...] += jnp.sum(dy * on * silu_g, axis=0, keepdims=True)
        dg_ref[...] = (dy * on * wn_v * (sgm * (1.0 + g * (1.0 - sgm)))).astype(bf16)
        t1 = dy * wn_v * silu_g
        do = rinv * (t1 - on * jnp.mean(t1 * on, axis=-1, keepdims=True))
        do_s[...] = do.astype(bf16)

        b = _chunk_cumsum(lf, rc)
        rem = _chunk_rcumsum(lf, rc) - lf
        eb, er = jnp.exp(b), jnp.exp(rem)
        qt, kh = qf * eb, key * er
        qt_s[...] = qt.astype(bf16)
        kh_s[...] = kh.astype(bf16)
        dec_s[...] = jnp.exp(b + rem)

        def step(k, carry):
            c = HG_T // CHUNK - 1 - k
            rows = pl.ds(pl.multiple_of(c * CHUNK, CHUNK), CHUNK)
            stp = st_ref[c]
            dst = DST[...]
            dstb = dst.astype(bf16)
            dob = do_s[rows, :]
            khb = kh_s[rows, :]
            dec = dec_s[pl.ds(pl.multiple_of(c * CHUNK, CHUNK), 1), :]
            dqt_s[rows, :] = _dot(dob, stp.astype(bf16))
            dkh = _dot(ib_ref[rows, :].astype(bf16), dstb)
            dkh_s[rows, :] = dkh
            dvi_s[rows, :] = _dot_nt(khb, dstb)
            dbl = jnp.sum(dst * stp, axis=0, keepdims=True) * dec + jnp.sum(dkh * khb.astype(f32), axis=0, keepdims=True)
            dbl_s[rows, :] = jnp.broadcast_to(dbl, (CHUNK, HGRN_HD))
            DST[...] = dst * dec + _dot_tn(dob, qt_s[rows, :])
            return carry

        lax.fori_loop(0, HG_T // CHUNK, step, 0)

        dqt, dkh = dqt_s[...], dkh_s[...]
        dqf = dqt * eb
        dkey = dkh * er
        db = dqt * qt - dkh * kh + jnp.where(rc == CHUNK - 1, dbl_s[...], 0.0)
        dv = dvi_s[...]
        ones = jnp.ones((HGRN_HD, HGRN_HD), bf16)
        for l in range(CHUNK):
            if l == 0:
                e, ks, vs = None, key, v
                qe = qf
            else:
                e = jnp.exp(jnp.where(rc >= l, b - pltpu.roll(b, l, 0), NEG))
                ks, vs = pltpu.roll(key, l, 0), pltpu.roll(v, l, 0)
                qe = qf * e
            pr = qe * ks
            rl = _dot(pr.astype(bf16), ones)
            drl = _dot((do * vs).astype(bf16), ones)
            if l > 0:
                drl = jnp.where(rc >= l, drl, 0.0)
            gl = drl * pr
            dqf = dqf + drl * ks * (e if l > 0 else 1.0)
            if l == 0:
                dv = dv + rl * do
                dkey = dkey + drl * qe
            else:
                dv = dv + pltpu.roll(rl * do, HG_T - l, 0)
                dkey = dkey + pltpu.roll(drl * qe, HG_T - l, 0)
                db = db + gl - pltpu.roll(gl, HG_T - l, 0)
        dlf = _chunk_rcumsum(db, rc)
        df = dlf / f - dkey
        df_ref[...] = (df * (1.0 - lb) * sf * (1.0 - sf)).astype(bf16)
        glb_ref[...] += jnp.sum(df * (1.0 - sf), axis=0, keepdims=True)
        dq_ref[...] = (dqf * (sq * (1.0 + qb * (1.0 - sq)))).astype(bf16)
        di_ref[...] = dv.astype(bf16)

    col, vec, lbs, tile, st = _hgrn_specs(nT, True)
    dyspec = pl.BlockSpec((HG_T, HGRN_HD), lambda h, i: (nT - 1 - i, 4 + h))
    tb = lambda: pltpu.VMEM((HG_T, HGRN_HD), bf16)
    tf = lambda: pltpu.VMEM((HG_T, HGRN_HD), f32)
    dq, df, di, dg, gwn, glb = pl.pallas_call(
        body, name="hgrn_bwd", grid=(HGRN_W // HGRN_HD, nT),
        in_specs=[col(0), col(1), col(2), col(3), lbs, vec, tile, st, dyspec],
        out_specs=[tile, tile, tile, tile, vec, vec],
        out_shape=[jax.ShapeDtypeStruct((S, HGRN_W), bf16)] * 4 + [jax.ShapeDtypeStruct((1, HGRN_W), f32)] * 2,
        scratch_shapes=[pltpu.VMEM((HGRN_HD, HGRN_HD), f32), tb(), tb(), tf(), tb(), tf(), tf(), tf(), tf()],
        compiler_params=_cp("arbitrary", "arbitrary"),
    )(hp, hp, hp, hp, lbl, wn, o_sav, states, dmix)
    return dq, df, di, dg, gwn, glb


def _out_proj(x, ya, yb, wout, w2):
    S = x.shape[0]
    tm = 512

    def body(x_ref, ya_ref, yb_ref, w_ref, w2_ref, h1_ref, u2_ref, mix_ref):
        mixed = jnp.concatenate([ya_ref[...], yb_ref[...]], axis=1).astype(bf16)
        mix_ref[...] = mixed
        h1 = x_ref[...] + _dot(mixed, w_ref[...])
        h1_ref[...] = h1
        r = lax.rsqrt(jnp.mean(h1 * h1, axis=-1, keepdims=True) + EPS)
        u2_ref[...] = (h1 * r * w2_ref[...]).astype(bf16)

    row = lambda w: pl.BlockSpec((tm, w), lambda i: (i, 0))
    return pl.pallas_call(
        body, name="out_proj", grid=(S // tm,),
        in_specs=[row(D_MODEL), row(ATTN_W), row(HGRN_W), pl.BlockSpec((D_MODEL, D_MODEL), lambda i: (0, 0)),
                  pl.BlockSpec((1, D_MODEL), lambda i: (0, 0))],
        out_specs=[row(D_MODEL), row(D_MODEL), row(D_MODEL)],
        out_shape=[jax.ShapeDtypeStruct((S, D_MODEL), f32), jax.ShapeDtypeStruct((S, D_MODEL), bf16),
                   jax.ShapeDtypeStruct((S, D_MODEL), bf16)],
        compiler_params=_cp("arbitrary"),
    )(x, ya, yb, wout, w2)


def _gate_up(u2, wgu):
    S = u2.shape[0]
    tm, tn = 512, 1408
    nj = FFN // tn

    def body(u_ref, wg_ref, wu_ref, g_ref, up_ref, a_ref):
        u = u_ref[...]
        g = _dot(u, wg_ref[...])
        up = _dot(u, wu_ref[...])
        g_ref[...] = g.astype(bf16)
        up_ref[...] = up.astype(bf16)
        a_ref[...] = (g * _sigmoid(g) * up).astype(bf16)

    out = pl.BlockSpec((tm, tn), lambda j, i: (i, j))
    return pl.pallas_call(
        body, name="gate_up", grid=(nj, S // tm),
        in_specs=[pl.BlockSpec((tm, D_MODEL), lambda j, i: (i, 0)), pl.BlockSpec((D_MODEL, tn), lambda j, i: (0, j)),
                  pl.BlockSpec((D_MODEL, tn), lambda j, i: (0, j + nj))],
        out_specs=[out, out, out],
        out_shape=[jax.ShapeDtypeStruct((S, FFN), bf16)] * 3,
        compiler_params=_cp("arbitrary", "arbitrary"),
    )(u2, wgu, wgu)


def _rms_bwd(dyw, hn, r):
    return r * (dyw - hn * jnp.mean(dyw * hn, axis=-1, keepdims=True))


def _down_loss(act, wdown, h1, tgt, w3):
    S = act.shape[0]
    tm = 256

    def body(a_ref, w_ref, h1_ref, t_ref, w3_ref, dh2_ref, loss_ref, gw3_ref):
        @pl.when(pl.program_id(0) == 0)
        def _():
            loss_ref[...] = jnp.zeros_like(loss_ref)
            gw3_ref[...] = jnp.zeros_like(gw3_ref)

        h2 = h1_ref[...] + _dot(a_ref[...], w_ref[...])
        r = lax.rsqrt(jnp.mean(h2 * h2, axis=-1, keepdims=True) + EPS)
        hn = h2 * r
        w3 = w3_ref[...]
        err = hn * w3 - t_ref[...]
        loss_ref[...] += (0.5 / D_MODEL) * jnp.sum(err * err)
        dy = err * (1.0 / D_MODEL)
        gw3_ref[...] += jnp.sum(dy * hn, axis=0, keepdims=True)
        dh2_ref[...] = _rms_bwd(dy * w3, hn, r)

    row = lambda w: pl.BlockSpec((tm, w), lambda i: (i, 0))
    return pl.pallas_call(
        body, name="down_loss", grid=(S // tm,),
        in_specs=[row(FFN), pl.BlockSpec((FFN, D_MODEL), lambda i: (0, 0)), row(D_MODEL), row(D_MODEL),
                  pl.BlockSpec((1, D_MODEL), lambda i: (0, 0))],
        out_specs=[row(D_MODEL), pl.BlockSpec((1, 128), lambda i: (0, 0)), pl.BlockSpec((1, D_MODEL), lambda i: (0, 0))],
        out_shape=[jax.ShapeDtypeStruct((S, D_MODEL), f32), jax.ShapeDtypeStruct((1, 128), f32),
                   jax.ShapeDtypeStruct((1, D_MODEL), f32)],
        compiler_params=_cp("arbitrary"),
    )(act, wdown, h1, tgt, w3)


def _dact(dh2, wdown, gate, up):
    S = dh2.shape[0]
    tm = 256

    def body(d_ref, w_ref, g_ref, u_ref, dg_ref, du_ref):
        da = _dot_nt(d_ref[...].astype(bf16), w_ref[...])
        g = g_ref[...].astype(f32)
        sg = _sigmoid(g)
        du_ref[...] = (da * g * sg).astype(bf16)
        dg_ref[...] = (da * u_ref[...].astype(f32) * (sg * (1.0 + g * (1.0 - sg)))).astype(bf16)

    row = lambda w: pl.BlockSpec((tm, w), lambda i: (i, 0))
    return pl.pallas_call(
        body, name="dact", grid=(S // tm,),
        in_specs=[row(D_MODEL), pl.BlockSpec((FFN, D_MODEL), lambda i: (0, 0)), row(FFN), row(FFN)],
        out_specs=[row(FFN), row(FFN)],
        out_shape=[jax.ShapeDtypeStruct((S, FFN), bf16)] * 2,
        compiler_params=_cp("arbitrary"),
    )(dh2, wdown, gate, up)


def _dgu(dgate, dup, wgu, h1, w2, dh2):
    S = dgate.shape[0]
    tm = 256

    def body(dg_ref, du_ref, wg_ref, wu_ref, h1_ref, w2_ref, dh2_ref, dh1_ref, gw2_ref):
        @pl.when(pl.program_id(0) == 0)
        def _():
            gw2_ref[...] = jnp.zeros_like(gw2_ref)

        du2 = _dot_nt(dg_ref[...], wg_ref[...]) + _dot_nt(du_ref[...], wu_ref[...])
        h1 = h1_ref[...]
        r = lax.rsqrt(jnp.mean(h1 * h1, axis=-1, keepdims=True) + EPS)
        hn = h1 * r
        gw2_ref[...] += jnp.sum(du2 * hn, axis=0, keepdims=True)
        dh1_ref[...] = dh2_ref[...] + _rms_bwd(du2 * w2_ref[...], hn, r)

    row = lambda w: pl.BlockSpec((tm, w), lambda i: (i, 0))
    return pl.pallas_call(
        body, name="dgu", grid=(S // tm,),
        in_specs=[row(FFN), row(FFN), pl.BlockSpec((D_MODEL, FFN), lambda i: (0, 0)),
                  pl.BlockSpec((D_MODEL, FFN), lambda i: (0, 1)), row(D_MODEL),
                  pl.BlockSpec((1, D_MODEL), lambda i: (0, 0)), row(D_MODEL)],
        out_specs=[row(D_MODEL), pl.BlockSpec((1, D_MODEL), lambda i: (0, 0))],
        out_shape=[jax.ShapeDtypeStruct((S, D_MODEL), f32), jax.ShapeDtypeStruct((1, D_MODEL), f32)],
        compiler_params=_cp("arbitrary"),
    )(dgate, dup, wgu, wgu, h1, w2, dh2)


def _dmixed(dh1, wout):
    S = dh1.shape[0]
    tm = 512

    def body(d_ref, w_ref, o_ref):
        o_ref[...] = _dot_nt(d_ref[...].astype(bf16), w_ref[...])

    row = pl.BlockSpec((tm, D_MODEL), lambda i: (i, 0))
    return pl.pallas_call(
        body, name="dmixed", grid=(S // tm,),
        in_specs=[row, pl.BlockSpec((D_MODEL, D_MODEL), lambda i: (0, 0))], out_specs=row,
        out_shape=jax.ShapeDtypeStruct((S, D_MODEL), f32), compiler_params=_cp("arbitrary"),
    )(dh1, wout)


def _din(dq, dk, dv, dhq, dhf, dhi, dhg, cos_t, sg_t, win, x, w1, dh1):
    S = x.shape[0]
    tm = 256

    def body(dq_ref, dk_ref, dv_ref, dhq_ref, dhf_ref, dhi_ref, dhg_ref, cos_ref, sg_ref, w_ref, x_ref, w1_ref, dh1_ref,
             dp_ref, gx_ref, gw1_ref):
        @pl.when(pl.program_id(0) == 0)
        def _():
            gw1_ref[...] = jnp.zeros_like(gw1_ref)

        cosv, sgv = cos_ref[...], sg_ref[...]
        unrope = lambda d: d * cosv - sgv * _swap_halves(d)
        parts = [(unrope(dq_ref[...]) * (HEAD_DIM ** -0.5)).astype(bf16), unrope(dk_ref[...]).astype(bf16),
                 dv_ref[...].astype(bf16), dhq_ref[...], dhf_ref[...], dhi_ref[...], dhg_ref[...]]
        du = jnp.zeros((tm, D_MODEL), f32)
        for j, pj in enumerate(parts):
            dp_ref[:, j * 512:(j + 1) * 512] = pj
            du = du + _dot_nt(pj, w_ref[:, j * 512:(j + 1) * 512])
        xv = x_ref[...]
        r = lax.rsqrt(jnp.mean(xv * xv, axis=-1, keepdims=True) + EPS)
        xn = xv * r
        gw1_ref[...] += jnp.sum(du * xn, axis=0, keepdims=True)
        gx_ref[...] = dh1_ref[...] + _rms_bwd(du * w1_ref[...], xn, r)

    row = lambda w: pl.BlockSpec((tm, w), lambda i: (i, 0))
    vec = pl.BlockSpec((1, D_MODEL), lambda i: (0, 0))
    return pl.pallas_call(
        body, name="din", grid=(S // tm,),
        in_specs=[row(512)] * 7 + [row(512), row(512), pl.BlockSpec((D_MODEL, IN_W), lambda i: (0, 0)), row(D_MODEL), vec,
                                   row(D_MODEL)],
        out_specs=[row(IN_W), row(D_MODEL), vec],
        out_shape=[jax.ShapeDtypeStruct((S, IN_W), bf16), jax.ShapeDtypeStruct((S, D_MODEL), f32),
                   jax.ShapeDtypeStruct((1, D_MODEL), f32)],
        compiler_params=_cp("arbitrary"),
    )(dq, dk, dv, dhq, dhf, dhi, dhg, cos_t, sg_t, win, x, w1, dh1)


def _gw(a, b, tn, name):
    S, M = a.shape
    N = b.shape[1]
    ts = 512

    def body(a_ref, b_ref, o_ref):
        @pl.when(pl.program_id(1) == 0)
        def _():
            o_ref[...] = jnp.zeros_like(o_ref)

        o_ref[...] += _dot_tn(a_ref[...].astype(bf16), b_ref[...].astype(bf16))

    return pl.pallas_call(
        body, name=name, grid=(N // tn, S // ts),
        in_specs=[pl.BlockSpec((ts, M), lambda j, s: (s, 0)), pl.BlockSpec((ts, tn), lambda j, s: (s, j))],
        out_specs=pl.BlockSpec((M, tn), lambda j, s: (0, j)), out_shape=jax.ShapeDtypeStruct((M, N), f32),
        compiler_params=_cp("arbitrary", "arbitrary"),
    )(a, b)


def _local_step(x, tgt, w1, win, lbl, wn, wout, w2, wgu, wdown, w3):
    S = x.shape[0]
    cos_t, sg_t = _rope_tables(S)
    u, qkv, hp = _in_proj(x, w1, win, cos_t, sg_t)
    att = None
    for dil in DILATIONS:
        att = _attn_fwd(qkv, dil, att)
    ya, lse = att
    yb, o_sav, states = _hgrn_fwd(hp, lbl, wn)
    h1, u2, mixed = _out_proj(x, ya, yb, wout, w2)
    gate, up, act = _gate_up(u2, wgu)
    dh2, loss, g_w3 = _down_loss(act, wdown, h1, tgt, w3)

    g_wdown = _gw(act, dh2, 512, "gw_down")
    dgate, dup = _dact(dh2, wdown, gate, up)
    g_wgu = (_gw(u2, dgate, 1408, "gw_gate"), _gw(u2, dup, 1408, "gw_up"))
    dh1, g_w2 = _dgu(dgate, dup, wgu, h1, w2, dh2)
    g_wout = _gw(mixed, dh1, 1024, "gw_out")
    dmix = _dmixed(dh1, wout)
    dhq, dhf, dhi, dhg, g_wn, g_lb = _hgrn_bwd(hp, lbl, wn, o_sav, states, dmix)
    datt = None
    for dil in DILATIONS:
        datt = _attn_bwd(qkv, ya, lse, dmix, dil, datt)
    dproj, gx, g_w1 = _din(*datt, dhq, dhf, dhi, dhg, cos_t, sg_t, win, x, w1, dh1)
    g_win = _gw(u, dproj, 896, "gw_in")
    return loss, gx, (g_win, g_wout, g_wgu, g_wdown), (g_w1, g_lb, g_wn, g_w2, g_w3)


MESH = pl.DeviceIdType.MESH
ANY = pl.BlockSpec(memory_space=pl.ANY)
VMEM_SPEC = pl.BlockSpec(memory_space=pltpu.VMEM)


def _pos():
    return lax.axis_index("x"), lax.axis_index("y"), lax.axis_index("c")


def _flip(v, bit):
    return 1 - v if bit else v


def _all_gather(shards):
    n = len(shards)

    def body(*refs):
        ins, outs, bufs = refs[:n], refs[n:2 * n], refs[2 * n:3 * n]
        send_sems, recv_sems, local_sems = refs[3 * n:]
        x, y, c = _pos()
        me, sibling = (x, y, c), (x, y, 1 - c)
        chips = [(1 - x, y), (x, 1 - y), (1 - x, 1 - y)]

        def copy(a, k, block, to, src=None):
            dst = outs[a].at[4 * block[0] + 2 * block[1] + block[2]]
            return pltpu.make_async_remote_copy(src_ref=dst if src is None else src, dst_ref=dst, send_sem=send_sems.at[a, k],
                                                recv_sem=recv_sems.at[a, k], device_id=to, device_id_type=MESH)

        loads = [pltpu.make_async_copy(ins[a], bufs[a], local_sems.at[a]) for a in range(n)]
        for ld in loads:
            ld.start()
        local, sends = [], []
        for a in range(n):
            loads[a].wait()
            mine = pltpu.make_async_copy(bufs[a], outs[a].at[4 * x + 2 * y + c], local_sems.at[a])
            mine.start()
            local.append(mine)
            first = [copy(a, 0, me, sibling, src=bufs[a])] + [copy(a, 1 + j, me, (*chip, c), src=bufs[a]) for j, chip in enumerate(chips)]
            for cp in first:
                cp.start()
            sends += first
        for a in range(n):
            for j, chip in enumerate(chips):
                copy(a, 1 + j, (*chip, c), me).wait_recv()
                passed = copy(a, 4 + j, (*chip, c), sibling)
                passed.start()
                sends.append(passed)
        for a in range(n):
            copy(a, 0, sibling, me).wait_recv()
            for j, chip in enumerate(chips):
                copy(a, 4 + j, (*chip, 1 - c), me).wait_recv()
        for cp in sends:
            cp.wait_send()
        for mine in local:
            mine.wait()

    return pl.pallas_call(
        body, name="gather_weights", in_specs=[ANY] * n, out_specs=[ANY] * n,
        out_shape=[jax.ShapeDtypeStruct((N_DEV,) + s.shape, s.dtype) for s in shards],
        scratch_shapes=[pltpu.VMEM(s.shape, s.dtype) for s in shards]
        + [pltpu.SemaphoreType.DMA((n, 7)), pltpu.SemaphoreType.DMA((n, 7)), pltpu.SemaphoreType.DMA((n,))],
    )(*shards)


def _rs_sibling(grads):
    n = len(grads)

    def body(*refs):
        g, got = refs[:n], refs[n:2 * n]
        send_sems, recv_sems = refs[2 * n:]
        x, y, c = _pos()
        copies = []
        for a in range(n):
            for q in range(4):
                cp = pltpu.make_async_remote_copy(src_ref=g[a].at[2 * q + (1 - c)], dst_ref=got[a].at[q], send_sem=send_sems.at[a, q],
                                                  recv_sem=recv_sems.at[a, q], device_id=(x, y, 1 - c), device_id_type=MESH)
                cp.start()
                copies.append(cp)
        for cp in copies:
            cp.wait()

    return pl.pallas_call(
        body, name="reduce_sibling", in_specs=[ANY] * n, out_specs=[ANY] * n,
        out_shape=[jax.ShapeDtypeStruct((4,) + g.shape[1:], g.dtype) for g in grads],
        scratch_shapes=[pltpu.SemaphoreType.DMA((n, 4))] * 2,
    )(*grads)


def _rs_chips(sums):
    n = len(sums)

    def body(*refs):
        s, out = refs[:n], refs[n:2 * n]
        send_sems, recv_sems = refs[2 * n:]
        x, y, c = _pos()
        copies = []
        for a in range(n):
            for f in (1, 2, 3):
                peer = (_flip(x, f >> 1), _flip(y, f & 1), c)
                cp = pltpu.make_async_remote_copy(src_ref=s[a].at[2 * peer[0] + peer[1]], dst_ref=out[a].at[f - 1],
                                                  send_sem=send_sems.at[a, f - 1], recv_sem=recv_sems.at[a, f - 1], device_id=peer,
                                                  device_id_type=MESH)
                cp.start()
                copies.append(cp)
        for cp in copies:
            cp.wait()

    return pl.pallas_call(
        body, name="reduce_chips", in_specs=[ANY] * n, out_specs=[ANY] * n,
        out_shape=[jax.ShapeDtypeStruct((3,) + s.shape[1:], s.dtype) for s in sums],
        scratch_shapes=[pltpu.SemaphoreType.DMA((n, 3))] * 2,
    )(*sums)


def _gather_small(g_w1, g_w2, g_w3, g_lb, g_wn, loss):
    def body(w1_ref, w2_ref, w3_ref, lb_ref, wn_ref, loss_ref, out_ref, pk, send_sems, recv_sems):
        x, y, c = _pos()
        me = 4 * x + 2 * y + c
        pk[...] = jnp.zeros_like(pk)
        pk[0:1, :] = w1_ref[...]
        pk[1:2, :] = w2_ref[...]
        pk[2:3, :] = w3_ref[...]
        pk[3:4, 0:HGRN_W] = lb_ref[...]
        pk[3:4, HGRN_W:2 * HGRN_W] = wn_ref[...]
        pk[4:5, 0:128] = loss_ref[...]
        out_ref[me] = pk[...]
        sends, recvs = [], []
        for k in range(1, N_DEV):
            peer = (_flip(x, k >> 2), _flip(y, (k >> 1) & 1), _flip(c, k & 1))
            cp = pltpu.make_async_remote_copy(src_ref=pk, dst_ref=out_ref.at[me], send_sem=send_sems.at[k - 1],
                                              recv_sem=recv_sems.at[k - 1], device_id=peer, device_id_type=MESH)
            cp.start()
            sends.append(cp)
            recvs.append(pltpu.make_async_remote_copy(src_ref=pk, dst_ref=out_ref.at[4 * peer[0] + 2 * peer[1] + peer[2]],
                                                      send_sem=send_sems.at[k - 1], recv_sem=recv_sems.at[k - 1], device_id=peer,
                                                      device_id_type=MESH))
        for cp in recvs:
            cp.wait_recv()
        for cp in sends:
            cp.wait_send()

    return pl.pallas_call(
        body, name="gather_small", in_specs=[VMEM_SPEC] * 6, out_specs=VMEM_SPEC,
        out_shape=jax.ShapeDtypeStruct((N_DEV, 8, D_MODEL), f32),
        scratch_shapes=[pltpu.VMEM((8, D_MODEL), f32), pltpu.SemaphoreType.DMA((N_DEV - 1,)), pltpu.SemaphoreType.DMA((N_DEV - 1,))],
    )(g_w1, g_w2, g_w3, g_lb, g_wn, loss)


def _row_tile(r):
    return max(t for t in range(8, 257, 8) if r % t == 0)


def _add_sibling(core, g, got, name):
    _, r, c = got.shape
    tr = _row_tile(r)

    def body(core_ref, a_ref, b_ref, o_ref):
        o_ref[...] = (a_ref[...] + b_ref[...]).astype(bf16)

    blk = pl.BlockSpec((1, tr, c), lambda q, i, core_ref: (q, i, 0))
    return pl.pallas_call(
        body, name=name, out_shape=jax.ShapeDtypeStruct(got.shape, bf16),
        grid_spec=pltpu.PrefetchScalarGridSpec(
            num_scalar_prefetch=1, grid=(4, r // tr),
            in_specs=[pl.BlockSpec((1, tr, c), lambda q, i, core_ref: (2 * q + core_ref[0], i, 0)), blk], out_specs=blk),
        compiler_params=_cp("arbitrary", "arbitrary"))(core, g, got)


def _adamw(w, g, m, v):
    m = ADAM_B1 * m + (1.0 - ADAM_B1) * g
    v = ADAM_B2 * v + (1.0 - ADAM_B2) * (g * g)
    m_hat = m / (1.0 - ADAM_B1 ** ADAM_STEP)
    v_hat = v / (1.0 - ADAM_B2 ** ADAM_STEP)
    return -ADAM_LR * (m_hat / (jnp.sqrt(v_hat) + ADAM_EPS) + ADAM_WD * w), m, v


def _adam_shard(where, g, got, pieces, w, m, v, name):
    r, c = w.shape
    tr = _row_tile(r)

    def body(where_ref, g_ref, got_ref, p_ref, w_ref, m_ref, v_ref, g_out, d_out, m_out, v_out):
        gsum = g_ref[0] + got_ref[0]
        for f in range(3):
            gsum = gsum + p_ref[f].astype(f32)
        g_out[...] = gsum
        d_out[...], m_out[...], v_out[...] = _adamw(w_ref[...], gsum, m_ref[...], v_ref[...])

    blk = pl.BlockSpec((tr, c), lambda i, where_ref: (i, 0))
    return pl.pallas_call(
        body, name=name, out_shape=[jax.ShapeDtypeStruct((r, c), f32)] * 4,
        grid_spec=pltpu.PrefetchScalarGridSpec(
            num_scalar_prefetch=1, grid=(r // tr,),
            in_specs=[pl.BlockSpec((1, tr, c), lambda i, where_ref: (where_ref[0], i, 0)),
                      pl.BlockSpec((1, tr, c), lambda i, where_ref: (where_ref[1], i, 0)),
                      pl.BlockSpec((3, tr, c), lambda i, where_ref: (0, i, 0)), blk, blk, blk],
            out_specs=[blk] * 4),
        compiler_params=_cp("arbitrary"),
    )(where, g, got, pieces, w, m, v)


def _small_update(gath, params):
    def body(gath_ref, *refs):
        ins, outs = refs[:15], refs[15:]
        gs = gath_ref[0]
        for k in range(1, N_DEV):
            gs = gs + gath_ref[k]
        outs[0][...] = gs[4:5, 0:128]
        l0, l1 = ins[9][0:1, :], ins[9][1:2, :]
        lb = _sigmoid(l0 - l1)
        d0 = gs[3:4, 0:HGRN_W] * lb * (1.0 - lb)
        first_row = lax.broadcasted_iota(jnp.int32, (2, HGRN_W), 0) == 0
        grads = [gs[0:1, :], gs[1:2, :], gs[2:3, :], jnp.where(first_row, d0, -d0), gs[3:4, HGRN_W:2 * HGRN_W]]
        for i, g in enumerate(grads):
            w_ref, m_ref, v_ref = ins[3 * i:3 * i + 3]
            o = outs[1 + 4 * i:5 + 4 * i]
            o[0][...] = g
            o[1][...], o[2][...], o[3][...] = _adamw(w_ref[...], g, m_ref[...], v_ref[...])

    flat = [a for p in params for a in p]
    out_shape = [jax.ShapeDtypeStruct((1, 128), f32)] + [jax.ShapeDtypeStruct(p[0].shape, f32) for p in params for _ in range(4)]
    outs = pl.pallas_call(body, name="small_update", in_specs=[VMEM_SPEC] * 16, out_specs=[VMEM_SPEC] * 21, out_shape=out_shape)(gath, *flat)
    return outs[0], [outs[1 + 4 * i:5 + 4 * i] for i in range(5)]


def kernel(x, norm1_w, w_in, lb_logits, hgrn_norm_w, w_out, norm2_w, w_gate_up, w_down, final_norm_w, loss_target, m_norm1_w, m_w_in, m_lb_logits, m_hgrn_norm_w, m_w_out, m_norm2_w, m_w_gate_up, m_w_down, m_final_norm_w, v_norm1_w, v_w_in, v_lb_logits, v_hgrn_norm_w, v_w_out, v_norm2_w, v_w_gate_up, v_w_down, v_final_norm_w):
    row = lambda a: a.reshape(1, D_MODEL)
    shards = [w_in[0], w_out[0], w_gate_up[0], w_down[0]]
    win_g, wout_g, wgu_g, wdown_g = _all_gather([s.astype(bf16) for s in shards])
    win = jnp.transpose(win_g, (1, 0, 2)).reshape(D_MODEL, IN_W)
    wgu = jnp.transpose(wgu_g, (1, 0, 2)).reshape(D_MODEL, 2 * FFN)
    wout = wout_g.reshape(D_MODEL, D_MODEL)
    wdown = wdown_g.reshape(FFN, D_MODEL)

    loss_p, gx, (g_win, g_wout, g_wgu, g_wdown), (g_w1, g_lb, g_wn, g_w2, g_w3) = _local_step(
        x[0], loss_target[0], norm1_w, win, lb_logits, hgrn_norm_w, wout, norm2_w, wgu, wdown, row(final_norm_w))

    by_owner = lambda g, w: jnp.transpose(g.reshape(g.shape[0], g.shape[1] // w, w), (1, 0, 2))
    grads = [by_owner(g_win, IN_W // N_DEV), g_wout.reshape(N_DEV, D_MODEL // N_DEV, D_MODEL),
             jnp.concatenate([by_owner(g, 2 * FFN // N_DEV) for g in g_wgu], axis=0), g_wdown.reshape(N_DEV, FFN // N_DEV, D_MODEL)]
    ix, iy, ic = lax.axis_index("x"), lax.axis_index("y"), lax.axis_index("c")
    core = jnp.stack([ic]).astype(jnp.int32)
    where = jnp.stack([4 * ix + 2 * iy + ic, 2 * ix + iy]).astype(jnp.int32)
    got = _rs_sibling(grads)
    sums = [_add_sibling(core, g, o, f"add_sibling_{i}") for i, (g, o) in enumerate(zip(grads, got))]
    pieces = _rs_chips(sums)
    moms = [(m_w_in[0], v_w_in[0]), (m_w_out[0], v_w_out[0]), (m_w_gate_up[0], v_w_gate_up[0]), (m_w_down[0], v_w_down[0])]
    big = [_adam_shard(where, g, o, p, w, m, v, f"adam_{i}")
           for i, (g, o, p, w, (m, v)) in enumerate(zip(grads, got, pieces, shards, moms))]
    big = [[a[None] for a in four] for four in big]

    gath = _gather_small(g_w1, g_w2, g_w3, g_lb, g_wn, loss_p)
    params = [(norm1_w, m_norm1_w, v_norm1_w), (norm2_w, m_norm2_w, v_norm2_w),
              (row(final_norm_w), row(m_final_norm_w), row(v_final_norm_w)),
              (lb_logits, m_lb_logits, v_lb_logits), (hgrn_norm_w, m_hgrn_norm_w, v_hgrn_norm_w)]
    loss, (s_w1, s_w2, s_w3, s_lb, s_wn) = _small_update(gath, params)
    s_w3 = [a.reshape(D_MODEL) for a in s_w3]
    per_w = [s_w1, big[0], s_lb, s_wn, big[1], s_w2, big[2], big[3], s_w3]
    return (loss[0, 0], gx[None], *[p[0] for p in per_w], *[p[1] for p in per_w], *[p[2] for p in per_w], *[p[3] for p in per_w])
```

```python
import functools

import jax
import jax.numpy as jnp
from jax import lax
from jax.experimental import pallas as pl
from jax.experimental.pallas import tpu as pltpu

f32, bf16 = jnp.float32, jnp.bfloat16

D_MODEL = 1024
ATTN_W = 512
HEAD_DIM = 64
ATTN_BLK = 128
DILATIONS = (1, 4, 16)
HGRN_W = 512
HGRN_HD = 128
CHUNK = 16
IN_W = 3 * ATTN_W + 4 * HGRN_W
FFN = 2816
EPS = 1e-6
ROPE_THETA = 10000.0
NEG = -1e30
N_DEV = 8
ADAM_LR, ADAM_B1, ADAM_B2, ADAM_EPS, ADAM_WD, ADAM_STEP = 0.001, 0.9, 0.999, 1e-08, 0.01, 10
VMEM_LIMIT = 56 * 1024 * 1024


def _cp(*sem):
    return pltpu.CompilerParams(dimension_semantics=sem, vmem_limit_bytes=VMEM_LIMIT)


def _dot(a, b):
    return jnp.dot(a, b, preferred_element_type=f32)


def _dot_nt(a, b):
    return lax.dot_general(a, b, (((1,), (1,)), ((), ())), preferred_element_type=f32)


def _dot_tn(a, b):
    return lax.dot_general(a, b, (((0,), (0,)), ((), ())), preferred_element_type=f32)


def _sigmoid(x):
    return 1.0 / (1.0 + jnp.exp(-x))


def _rope_tables(S):
    half = HEAD_DIM // 2
    inv_freq = ROPE_THETA ** (-jnp.arange(half, dtype=f32) / half)
    ang = jnp.arange(S, dtype=f32)[:, None] * inv_freq[None, :]
    cos, sin = jnp.cos(ang), jnp.sin(ang)
    cos_t = jnp.tile(jnp.concatenate([cos, cos], axis=1), (1, ATTN_W // HEAD_DIM))
    sg_t = jnp.tile(jnp.concatenate([-sin, sin], axis=1), (1, ATTN_W // HEAD_DIM))
    return cos_t, sg_t


def _swap_halves(v):
    n = v.shape[1]
    lane = lax.broadcasted_iota(jnp.int32, v.shape, 1)
    return jnp.where((lane % HEAD_DIM) < HEAD_DIM // 2, pltpu.roll(v, n - HEAD_DIM // 2, 1), pltpu.roll(v, HEAD_DIM // 2, 1))


def _in_proj(x, w1, win, cos_t, sg_t):
    S = x.shape[0]
    tm = 256

    def body(x_ref, w1_ref, w_ref, cos_ref, sg_ref, u_ref, qkv_ref, hp_ref):
        xv = x_ref[...]
        r = lax.rsqrt(jnp.mean(xv * xv, axis=-1, keepdims=True) + EPS)
        u = (xv * r * w1_ref[...]).astype(bf16)
        u_ref[...] = u
        cosv, sgv = cos_ref[...], sg_ref[...]
        for j in range(3):
            pj = _dot(u, w_ref[:, j * ATTN_W:(j + 1) * ATTN_W])
            if j < 2:
                pj = pj * cosv + _swap_halves(pj) * sgv
            if j == 0:
                pj = pj * (HEAD_DIM ** -0.5)
            qkv_ref[:, j * ATTN_W:(j + 1) * ATTN_W] = pj.astype(bf16)
        for j in range(4):
            lo = 3 * ATTN_W + j * HGRN_W
            hp_ref[:, j * HGRN_W:(j + 1) * HGRN_W] = _dot(u, w_ref[:, lo:lo + HGRN_W])

    return pl.pallas_call(
        body, name="in_proj", grid=(S // tm,),
        in_specs=[pl.BlockSpec((tm, D_MODEL), lambda i: (i, 0)), pl.BlockSpec((1, D_MODEL), lambda i: (0, 0)),
                  pl.BlockSpec((D_MODEL, IN_W), lambda i: (0, 0)),
                  pl.BlockSpec((tm, ATTN_W), lambda i: (i, 0)), pl.BlockSpec((tm, ATTN_W), lambda i: (i, 0))],
        out_specs=[pl.BlockSpec((tm, D_MODEL), lambda i: (i, 0)), pl.BlockSpec((tm, 3 * ATTN_W), lambda i: (i, 0)),
                   pl.BlockSpec((tm, 4 * HGRN_W), lambda i: (i, 0))],
        out_shape=[jax.ShapeDtypeStruct((S, D_MODEL), bf16), jax.ShapeDtypeStruct((S, 3 * ATTN_W), bf16),
                   jax.ShapeDtypeStruct((S, 4 * HGRN_W), f32)],
        compiler_params=_cp("arbitrary"),
    )(x, w1, win, cos_t, sg_t)


def _valid_mask(n):
    qi = lax.broadcasted_iota(jnp.int32, (ATTN_BLK, 2 * ATTN_BLK), 0)
    kj = lax.broadcasted_iota(jnp.int32, (ATTN_BLK, 2 * ATTN_BLK), 1)
    delta = ATTN_BLK + qi - kj
    return (delta >= 0) & (delta <= ATTN_BLK) & ((n > 0) | (kj >= ATTN_BLK))


def _head_masks():
    lane = lax.broadcasted_iota(jnp.int32, (ATTN_BLK, 128), 1)
    even = lane < HEAD_DIM
    return even, (even, jnp.logical_not(even))


def _attn_fwd(qkv, dil, prev):
    S = qkv.shape[0]
    L = S // dil
    nb = L // ATTN_BLK
    W3 = 3 * ATTN_W
    has_prev = prev is not None

    def body(*refs):
        if has_prev:
            q_ref, kp_ref, kc_ref, vp_ref, vc_ref, yp_ref, lp_ref, y_ref, l_ref = refs
        else:
            q_ref, kp_ref, kc_ref, vp_ref, vc_ref, y_ref, l_ref = refs
        n = pl.program_id(1)
        valid = _valid_mask(n)
        even, masks = _head_masks()
        for p in range(ATTN_W // 128):
            sl = slice(128 * p, 128 * p + 128)
            q2 = q_ref[:, sl].astype(f32)
            k2 = jnp.concatenate([kp_ref[:, sl], kc_ref[:, sl]], axis=0)
            v2 = jnp.concatenate([vp_ref[:, sl], vc_ref[:, sl]], axis=0)
            outs, lses = [], []
            for e in range(2):
                qm = jnp.where(masks[e], q2, 0.0).astype(bf16)
                s = jnp.where(valid, _dot_nt(qm, k2), NEG)
                m = jnp.max(s, axis=-1, keepdims=True)
                pe = jnp.exp(s - m)
                lsum = jnp.sum(pe, axis=-1, keepdims=True)
                acc = _dot(pe.astype(bf16), v2)
                outs.append(acc / lsum)
                lses.append(jnp.broadcast_to(m + jnp.log(lsum), (ATTN_BLK, 128)))
            out = jnp.where(even, outs[0], outs[1])
            lse = jnp.where(even, lses[0], lses[1])
            if has_prev:
                lp = lp_ref[:, sl]
                mx = jnp.maximum(lp, lse)
                a, b = jnp.exp(lp - mx), jnp.exp(lse - mx)
                tot = a + b
                out = (a * yp_ref[:, sl] + b * out) / tot
                lse = mx + jnp.log(tot)
            y_ref[:, sl] = out
            l_ref[:, sl] = lse

    blk = (ATTN_BLK, ATTN_W)
    cur = lambda c: (lambda r, n: (n, 3 * r + c))
    prv = lambda c: (lambda r, n: (jnp.maximum(n - 1, 0), 3 * r + c))
    in_specs = [pl.BlockSpec(blk, cur(0)), pl.BlockSpec(blk, prv(1)), pl.BlockSpec(blk, cur(1)),
                pl.BlockSpec(blk, prv(2)), pl.BlockSpec(blk, cur(2))]
    qv = qkv.reshape(L, dil * W3)
    args = [qv, qv, qv, qv, qv]
    if has_prev:
        in_specs += [pl.BlockSpec(blk, lambda r, n: (n, r))] * 2
        args += [prev[0].reshape(L, dil * ATTN_W), prev[1].reshape(L, dil * ATTN_W)]
    y, lse = pl.pallas_call(
        body, name=f"attn_fwd_d{dil}", grid=(dil, nb), in_specs=in_specs,
        out_specs=[pl.BlockSpec(blk, lambda r, n: (n, r))] * 2,
        out_shape=[jax.ShapeDtypeStruct((L, dil * ATTN_W), f32)] * 2,
        compiler_params=_cp("arbitrary", "arbitrary"),
    )(*args)
    return y.reshape(S, ATTN_W), lse.reshape(S, ATTN_W)


def _attn_bwd(qkv, ya, lse, dmix, dil, prev):
    S = qkv.shape[0]
    L = S // dil
    nb = L // ATTN_BLK
    W3 = 3 * ATTN_W
    has_prev = prev is not None

    def body(*refs):
        q_ref, kp_ref, kc_ref, vp_ref, vc_ref, y_ref, l_ref, dy_ref = refs[:8]
        refs = refs[8:]
        if has_prev:
            dqp_ref, dkp_ref, dvp_ref = refs[:3]
            refs = refs[3:]
        dq_ref, dk_ref, dv_ref, ck, cv = refs
        n = pl.program_id(1)

        @pl.when(n == 0)
        def _():
            ck[...] = jnp.zeros_like(ck)
            cv[...] = jnp.zeros_like(cv)

        @pl.when(n < nb)
        def _():
            valid = _valid_mask(n)
            even, masks = _head_masks()
            li = lax.broadcasted_iota(jnp.int32, (128, 128), 0)
            lj = lax.broadcasted_iota(jnp.int32, (128, 128), 1)
            seg = jnp.where((li // HEAD_DIM) == (lj // HEAD_DIM), 1.0, 0.0).astype(bf16)
            for p in range(ATTN_W // 128):
                sl = slice(128 * p, 128 * p + 128)
                q2 = q_ref[:, sl].astype(f32)
                k2 = jnp.concatenate([kp_ref[:, sl], kc_ref[:, sl]], axis=0)
                v2 = jnp.concatenate([vp_ref[:, sl], vc_ref[:, sl]], axis=0)
                k2f = k2.astype(f32)
                dy2 = dy_ref[:, sl]
                lse2 = l_ref[:, sl]
                dyy = dy2 * y_ref[:, sl]
                hi = dyy.astype(bf16)
                lo = (dyy - hi.astype(f32)).astype(bf16)
                delta2 = _dot(hi, seg) + _dot(lo, seg)
                dq2 = jnp.zeros((ATTN_BLK, 128), f32)
                dk2 = jnp.zeros((2 * ATTN_BLK, 128), f32)
                dv2 = jnp.zeros((2 * ATTN_BLK, 128), f32)
                for e in range(2):
                    c0 = e * HEAD_DIM
                    qm = jnp.where(masks[e], q2, 0.0).astype(bf16)
                    km = jnp.where(masks[e][:1, :], k2f, 0.0).astype(bf16)
                    dym = jnp.where(masks[e], dy2, 0.0).astype(bf16)
                    s = _dot_nt(qm, k2)
                    pe = jnp.where(valid, jnp.exp(s - lse2[:, c0:c0 + 1]), 0.0)
                    dp = _dot_nt(dym, v2)
                    ds = (pe * (dp - delta2[:, c0:c0 + 1])).astype(bf16)
                    dv2 = dv2 + _dot_tn(pe.astype(bf16), dym)
                    dq2 = dq2 + _dot(ds, km)
                    dk2 = dk2 + _dot_tn(ds, qm)
                tk = dk2[:ATTN_BLK] + ck[:, sl]
                tv = dv2[:ATTN_BLK] + cv[:, sl]
                if has_prev:
                    dq2 = dq2 + dqp_ref[:, sl]
                    tk = tk + dkp_ref[:, sl]
                    tv = tv + dvp_ref[:, sl]
                dq_ref[:, sl] = dq2
                dk_ref[:, sl] = tk
                dv_ref[:, sl] = tv
                ck[:, sl] = dk2[ATTN_BLK:]
                cv[:, sl] = dv2[ATTN_BLK:]

        @pl.when(n == nb)
        def _():
            if has_prev:
                dk_ref[...] = ck[...] + dkp_ref[...]
                dv_ref[...] = cv[...] + dvp_ref[...]
            else:
                dk_ref[...] = ck[...]
                dv_ref[...] = cv[...]

    blk = (ATTN_BLK, ATTN_W)
    cn = lambda n: jnp.minimum(n, nb - 1)
    pn = lambda n: jnp.clip(n - 1, 0, nb - 1)
    in_specs = [pl.BlockSpec(blk, lambda r, n: (cn(n), 3 * r)),
                pl.BlockSpec(blk, lambda r, n: (pn(n), 3 * r + 1)), pl.BlockSpec(blk, lambda r, n: (cn(n), 3 * r + 1)),
                pl.BlockSpec(blk, lambda r, n: (pn(n), 3 * r + 2)), pl.BlockSpec(blk, lambda r, n: (cn(n), 3 * r + 2)),
                pl.BlockSpec(blk, lambda r, n: (cn(n), r)), pl.BlockSpec(blk, lambda r, n: (cn(n), r)),
                pl.BlockSpec(blk, lambda r, n: (cn(n), 2 * r))]
    qv = qkv.reshape(L, dil * W3)
    view = lambda a: a.reshape(L, dil * ATTN_W)
    args = [qv, qv, qv, qv, qv, view(ya), view(lse), dmix.reshape(L, dil * D_MODEL)]
    if has_prev:
        in_specs += [pl.BlockSpec(blk, lambda r, n: (cn(n), r)), pl.BlockSpec(blk, lambda r, n: (pn(n), r)),
                     pl.BlockSpec(blk, lambda r, n: (pn(n), r))]
        args += [view(a) for a in prev]
    outs = pl.pallas_call(
        body, name=f"attn_bwd_d{dil}", grid=(dil, nb + 1), in_specs=in_specs,
        out_specs=[pl.BlockSpec(blk, lambda r, n: (cn(n), r)), pl.BlockSpec(blk, lambda r, n: (pn(n), r)),
                   pl.BlockSpec(blk, lambda r, n: (pn(n), r))],
        out_shape=[jax.ShapeDtypeStruct((L, dil * ATTN_W), f32)] * 3,
        scratch_shapes=[pltpu.VMEM(blk, f32), pltpu.VMEM(blk, f32)],
        compiler_params=_cp("arbitrary", "arbitrary"),
    )(*args)
    return tuple(o.reshape(S, ATTN_W) for o in outs)


HG_T = 256


def _row_in_chunk():
    return lax.broadcasted_iota(jnp.int32, (HG_T, HGRN_HD), 0) % CHUNK


def _chunk_cumsum(v, rc):
    for k in (1, 2, 4, 8):
        v = v + jnp.where(rc >= k, pltpu.roll(v, k, 0), 0.0)
    return v


def _chunk_rcumsum(v, rc):
    for k in (1, 2, 4, 8):
        v = v + jnp.where(rc < CHUNK - k, pltpu.roll(v, HG_T - k, 0), 0.0)
    return v


def _hgrn_gates(qb, fb, lb):
    sf = _sigmoid(fb)
    f = lb + (1.0 - lb) * sf
    sq = _sigmoid(qb)
    return sf, f, jnp.log(f), 1.0 - f, sq, qb * sq


def _hgrn_specs(nT, rev):
    ti = (lambda i: nT - 1 - i) if rev else (lambda i: i)
    col = lambda c: pl.BlockSpec((HG_T, HGRN_HD), lambda h, i: (ti(i), 4 * c + h))
    vec = pl.BlockSpec((1, HGRN_HD), lambda h, i: (0, h))
    lbs = pl.BlockSpec((2, HGRN_HD), lambda h, i: (0, h))
    tile = pl.BlockSpec((HG_T, HGRN_HD), lambda h, i: (ti(i), h))
    st = pl.BlockSpec((HG_T // CHUNK, HGRN_HD, HGRN_HD), lambda h, i: (ti(i), h, 0))
    return col, vec, lbs, tile, st


def _old_hgrn_fwd(hp, lbl, wn):
    S = hp.shape[0]
    nT = S // HG_T

    def body(qb_ref, fb_ref, ib_ref, gb_ref, lbl_ref, wn_ref, yb_ref, o_ref, st_ref, ST, qt_s, kh_s, dec_s, oi_s):
        @pl.when(pl.program_id(1) == 0)
        def _():
            ST[...] = jnp.zeros_like(ST)

        rc = _row_in_chunk()
        lb = _sigmoid(lbl_ref[0:1, :] - lbl_ref[1:2, :])
        _, _, lf, key, _, qf = _hgrn_gates(qb_ref[...], fb_ref[...], lb)
        v = ib_ref[...]
        b = _chunk_cumsum(lf, rc)
        rem = _chunk_rcumsum(lf, rc) - lf
        qt_s[...] = (qf * jnp.exp(b)).astype(bf16)
        kh_s[...] = (key * jnp.exp(rem)).astype(bf16)
        dec_s[...] = jnp.exp(b + rem)

        def step(c, carry):
            rows = pl.ds(pl.multiple_of(c * CHUNK, CHUNK), CHUNK)
            stv = ST[...]
            st_ref[c] = stv
            oi_s[rows, :] = _dot_nt(qt_s[rows, :], stv.astype(bf16))
            dec = dec_s[pl.ds(pl.multiple_of(c * CHUNK, CHUNK), 1), :]
            ST[...] = stv * dec + _dot_tn(ib_ref[rows, :].astype(bf16), kh_s[rows, :])
            return carry

        lax.fori_loop(0, HG_T // CHUNK, step, 0)

        ones = jnp.ones((HGRN_HD, HGRN_HD), bf16)
        o = oi_s[...]
        for l in range(CHUNK):
            if l == 0:
                pr, vs = qf * key, v
            else:
                e = jnp.exp(jnp.where(rc >= l, b - pltpu.roll(b, l, 0), NEG))
                pr, vs = qf * pltpu.roll(key, l, 0) * e, pltpu.roll(v, l, 0)
            o = o + _dot(pr.astype(bf16), ones) * vs
        o_ref[...] = o
        on = o * lax.rsqrt(jnp.mean(o * o, axis=-1, keepdims=True) + EPS)
        g = gb_ref[...]
        yb_ref[...] = on * wn_ref[...] * (g * _sigmoid(g))

    col, vec, lbs, tile, st = _hgrn_specs(nT, False)
    return pl.pallas_call(
        body, name="hgrn_fwd", grid=(HGRN_W // HGRN_HD, nT),
        in_specs=[col(0), col(1), col(2), col(3), lbs, vec],
        out_specs=[tile, tile, st],
        out_shape=[jax.ShapeDtypeStruct((S, HGRN_W), f32), jax.ShapeDtypeStruct((S, HGRN_W), f32),
                   jax.ShapeDtypeStruct((S // CHUNK, HGRN_W, HGRN_HD), f32)],
        scratch_shapes=[pltpu.VMEM((HGRN_HD, HGRN_HD), f32), pltpu.VMEM((HG_T, HGRN_HD), bf16),
                        pltpu.VMEM((HG_T, HGRN_HD), bf16), pltpu.VMEM((HG_T, HGRN_HD), f32),
                        pltpu.VMEM((HG_T, HGRN_HD), f32)],
        compiler_params=_cp("arbitrary", "arbitrary"),
    )(hp, hp, hp, hp, lbl, wn)


def _old_hgrn_bwd(hp, lbl, wn, o_sav, states, dmix):
    S = hp.shape[0]
    nT = S // HG_T

    def body(qb_ref, fb_ref, ib_ref, gb_ref, lbl_ref, wn_ref, o_ref, st_ref, dy_ref,
             dq_ref, df_ref, di_ref, dg_ref, gwn_ref, glb_ref,
             DST, qt_s, kh_s, dec_s, do_s, dqt_s, dkh_s, dvi_s, dbl_s):
        @pl.when(pl.program_id(1) == 0)
        def _():
            DST[...] = jnp.zeros_like(DST)
            gwn_ref[...] = jnp.zeros_like(gwn_ref)
            glb_ref[...] = jnp.zeros_like(glb_ref)

        rc = _row_in_chunk()
        lb = _sigmoid(lbl_ref[0:1, :] - lbl_ref[1:2, :])
        qb = qb_ref[...]
        sf, f, lf, key, sq, qf = _hgrn_gates(qb, fb_ref[...], lb)
        v = ib_ref[...]
        o = o_ref[...]
        rinv = lax.rsqrt(jnp.mean(o * o, axis=-1, keepdims=True) + EPS)
        on = o * rinv
        g = gb_ref[...]
        sgm = _sigmoid(g)
        silu_g = g * sgm
        dy = dy_ref[...]
        wn_v = wn_ref[...]
        gwn_ref[...] += jnp.sum(dy * on * silu_g, axis=0, keepdims=True)
        dg_ref[...] = (dy * on * wn_v * (sgm * (1.0 + g * (1.0 - sgm)))).astype(bf16)
        t1 = dy * wn_v * silu_g
        do = rinv * (t1 - on * jnp.mean(t1 * on, axis=-1, keepdims=True))
        do_s[...] = do.astype(bf16)

        b = _chunk_cumsum(lf, rc)
        rem = _chunk_rcumsum(lf, rc) - lf
        eb, er = jnp.exp(b), jnp.exp(rem)
        qt, kh = qf * eb, key * er
        qt_s[...] = qt.astype(bf16)
        kh_s[...] = kh.astype(bf16)
        dec_s[...] = jnp.exp(b + rem)

        def step(k, carry):
            c = HG_T // CHUNK - 1 - k
            rows = pl.ds(pl.multiple_of(c * CHUNK, CHUNK), CHUNK)
            stp = st_ref[c]
            dst = DST[...]
            dstb = dst.astype(bf16)
            dob = do_s[rows, :]
            khb = kh_s[rows, :]
            dec = dec_s[pl.ds(pl.multiple_of(c * CHUNK, CHUNK), 1), :]
            dqt_s[rows, :] = _dot(dob, stp.astype(bf16))
            dkh = _dot(ib_ref[rows, :].astype(bf16), dstb)
            dkh_s[rows, :] = dkh
            dvi_s[rows, :] = _dot_nt(khb, dstb)
            dbl = jnp.sum(dst * stp, axis=0, keepdims=True) * dec + jnp.sum(dkh * khb.astype(f32), axis=0, keepdims=True)
            dbl_s[rows, :] = jnp.broadcast_to(dbl, (CHUNK, HGRN_HD))
            DST[...] = dst * dec + _dot_tn(dob, qt_s[rows, :])
            return carry

        lax.fori_loop(0, HG_T // CHUNK, step, 0)

        dqt, dkh = dqt_s[...], dkh_s[...]
        dqf = dqt * eb
        dkey = dkh * er
        db = dqt * qt - dkh * kh + jnp.where(rc == CHUNK - 1, dbl_s[...], 0.0)
        dv = dvi_s[...]
        ones = jnp.ones((HGRN_HD, HGRN_HD), bf16)
        for l in range(CHUNK):
            if l == 0:
                e, ks, vs = None, key, v
                qe = qf
            else:
                e = jnp.exp(jnp.where(rc >= l, b - pltpu.roll(b, l, 0), NEG))
                ks, vs = pltpu.roll(key, l, 0), pltpu.roll(v, l, 0)
                qe = qf * e
            pr = qe * ks
            rl = _dot(pr.astype(bf16), ones)
            drl = _dot((do * vs).astype(bf16), ones)
            if l > 0:
                drl = jnp.where(rc >= l, drl, 0.0)
            gl = drl * pr
            dqf = dqf + drl * ks * (e if l > 0 else 1.0)
            if l == 0:
                dv = dv + rl * do
                dkey = dkey + drl * qe
            else:
                dv = dv + pltpu.roll(rl * do, HG_T - l, 0)
                dkey = dkey + pltpu.roll(drl * qe, HG_T - l, 0)
                db = db + gl - pltpu.roll(gl, HG_T - l, 0)
        dlf = _chunk_rcumsum(db, rc)
        df = dlf / f - dkey
        df_ref[...] = (df * (1.0 - lb) * sf * (1.0 - sf)).astype(bf16)
        glb_ref[...] += jnp.sum(df * (1.0 - sf), axis=0, keepdims=True)
        dq_ref[...] = (dqf * (sq * (1.0 + qb * (1.0 - sq)))).astype(bf16)
        di_ref[...] = dv.astype(bf16)

    col, vec, lbs, tile, st = _hgrn_specs(nT, True)
    dyspec = pl.BlockSpec((HG_T, HGRN_HD), lambda h, i: (nT - 1 - i, 4 + h))
    tb = lambda: pltpu.VMEM((HG_T, HGRN_HD), bf16)
    tf = lambda: pltpu.VMEM((HG_T, HGRN_HD), f32)
    dq, df, di, dg, gwn, glb = pl.pallas_call(
        body, name="hgrn_bwd", grid=(HGRN_W // HGRN_HD, nT),
        in_specs=[col(0), col(1), col(2), col(3), lbs, vec, tile, st, dyspec],
        out_specs=[tile, tile, tile, tile, vec, vec],
        out_shape=[jax.ShapeDtypeStruct((S, HGRN_W), bf16)] * 4 + [jax.ShapeDtypeStruct((1, HGRN_W), f32)] * 2,
        scratch_shapes=[pltpu.VMEM((HGRN_HD, HGRN_HD), f32), tb(), tb(), tf(), tb(), tf(), tf(), tf(), tf()],
        compiler_params=_cp("arbitrary", "arbitrary"),
    )(hp, hp, hp, hp, lbl, wn, o_sav, states, dmix)
    return dq, df, di, dg, gwn, glb


N_HH = HGRN_W // HGRN_HD
HG_SUB = 128
SAFE_RANGE = 80.0


def _hgrn_prep(qb, fb, lbl2, rc):
    lb = _sigmoid(lbl2[0:1, :] - lbl2[1:2, :])
    sf, f, lf, key, sq, qf = _hgrn_gates(qb, fb, lb)
    b = _chunk_cumsum(lf, rc)
    rem = _chunk_rcumsum(lf, rc) - lf
    return dict(lb=lb, sf=sf, f=f, key=key, sq=sq, qf=qf, b=b, rem=rem, eb=jnp.exp(b), er=jnp.exp(rem))


def _chunk_mask():
    r = lax.broadcasted_iota(jnp.int32, (HG_SUB, HG_SUB), 0)
    c = lax.broadcasted_iota(jnp.int32, (HG_SUB, HG_SUB), 1)
    return ((r // CHUNK) == (c // CHUNK)) & (c <= r)


def _hgrn_fwd(hp, lbl, wn):
    S = hp.shape[0]
    nT = S // HG_T

    def body(qb_ref, fb_ref, ib_ref, gb_ref, lbl_ref, wn_ref, yb_ref, o_ref, st_ref, ST, qt_s, kh_s, dec_s, oi_s):
        @pl.when(pl.program_id(0) == 0)
        def _():
            ST[...] = jnp.zeros_like(ST)

        rc = _row_in_chunk()
        for h in range(N_HH):
            sl = slice(HGRN_HD * h, HGRN_HD * (h + 1))
            p = _hgrn_prep(qb_ref[:, sl], fb_ref[:, sl], lbl_ref[:, sl], rc)
            qf, key, b = p["qf"], p["key"], p["b"]
            qt = qf * p["eb"]
            qt_s[:, sl] = qt.astype(bf16)
            kh_s[:, sl] = (key * p["er"]).astype(bf16)
            dec_s[:, sl] = jnp.exp(b + p["rem"])
            rng = jnp.max(-(b + p["rem"]))

            @pl.when(rng < SAFE_RANGE)
            def _():
                kp = (key * jnp.exp(-b)).astype(bf16)
                cmask = _chunk_mask()
                for j in range(HG_T // HG_SUB):
                    rs = slice(HG_SUB * j, HG_SUB * (j + 1))
                    sc = jnp.where(cmask, _dot_nt(qt[rs].astype(bf16), kp[rs]), 0.0).astype(bf16)
                    oi_s[rs, sl] = _dot(sc, ib_ref[rs, sl].astype(bf16))

            @pl.when(rng >= SAFE_RANGE)
            def _():
                v = ib_ref[:, sl]
                ones = jnp.ones((HGRN_HD, HGRN_HD), bf16)
                o = jnp.zeros((HG_T, HGRN_HD), f32)
                for l in range(CHUNK):
                    if l == 0:
                        pr, vs = qf * key, v
                    else:
                        e = jnp.exp(jnp.where(rc >= l, b - pltpu.roll(b, l, 0), NEG))
                        pr, vs = qf * pltpu.roll(key, l, 0) * e, pltpu.roll(v, l, 0)
                    o = o + _dot(pr.astype(bf16), ones) * vs
                oi_s[:, sl] = o

        def step(c, carry):
            rows = pl.ds(pl.multiple_of(c * CHUNK, CHUNK), CHUNK)
            row0 = pl.ds(pl.multiple_of(c * CHUNK, CHUNK), 1)
            for h in range(N_HH):
                sl = slice(HGRN_HD * h, HGRN_HD * (h + 1))
                stv = ST[h]
                st_ref[c, sl, :] = stv
                oi_s[rows, sl] += _dot_nt(qt_s[rows, sl], stv.astype(bf16))
                ST[h] = stv * dec_s[row0, sl] + _dot_tn(ib_ref[rows, sl].astype(bf16), kh_s[rows, sl])
            return carry

        lax.fori_loop(0, HG_T // CHUNK, step, 0)

        for h in range(N_HH):
            sl = slice(HGRN_HD * h, HGRN_HD * (h + 1))
            o = oi_s[:, sl]
            o_ref[:, sl] = o
            on = o * lax.rsqrt(jnp.mean(o * o, axis=-1, keepdims=True) + EPS)
            g = gb_ref[:, sl]
            yb_ref[:, sl] = on * wn_ref[:, sl] * (g * _sigmoid(g))

    col = lambda c: pl.BlockSpec((HG_T, HGRN_W), lambda i: (i, c))
    tile = pl.BlockSpec((HG_T, HGRN_W), lambda i: (i, 0))
    whole = lambda a: pl.BlockSpec(a.shape, lambda i: (0, 0))
    return pl.pallas_call(
        body, name="hgrn_fwd", grid=(nT,),
        in_specs=[col(0), col(1), col(2), col(3), whole(lbl), whole(wn)],
        out_specs=[tile, tile, pl.BlockSpec((HG_T // CHUNK, HGRN_W, HGRN_HD), lambda i: (i, 0, 0))],
        out_shape=[jax.ShapeDtypeStruct((S, HGRN_W), f32), jax.ShapeDtypeStruct((S, HGRN_W), f32),
                   jax.ShapeDtypeStruct((S // CHUNK, HGRN_W, HGRN_HD), f32)],
        scratch_shapes=[pltpu.VMEM((N_HH, HGRN_HD, HGRN_HD), f32), pltpu.VMEM((HG_T, HGRN_W), bf16),
                        pltpu.VMEM((HG_T, HGRN_W), bf16), pltpu.VMEM((HG_T, HGRN_W), f32), pltpu.VMEM((HG_T, HGRN_W), f32)],
        compiler_params=_cp("arbitrary"),
    )(hp, hp, hp, hp, lbl, wn)


def _hgrn_bwd(hp, lbl, wn, o_sav, states, dmix):
    S = hp.shape[0]
    nT = S // HG_T

    def body(qb_ref, fb_ref, ib_ref, gb_ref, lbl_ref, wn_ref, o_ref, st_ref, dy_ref,
             dq_ref, df_ref, di_ref, dg_ref, gwn_ref, glb_ref,
             DST, qt_s, kh_s, dec_s, do_s, dqt_s, dkh_s, dbl_s, dvi_s, dqi_s, dki_s, dbi_s):
        @pl.when(pl.program_id(0) == 0)
        def _():
            DST[...] = jnp.zeros_like(DST)
            gwn_ref[...] = jnp.zeros_like(gwn_ref)
            glb_ref[...] = jnp.zeros_like(glb_ref)

        rc = _row_in_chunk()
        for h in range(N_HH):
            sl = slice(HGRN_HD * h, HGRN_HD * (h + 1))
            p = _hgrn_prep(qb_ref[:, sl], fb_ref[:, sl], lbl_ref[:, sl], rc)
            qf, key, b = p["qf"], p["key"], p["b"]
            v = ib_ref[:, sl]
            o = o_ref[:, sl]
            rinv = lax.rsqrt(jnp.mean(o * o, axis=-1, keepdims=True) + EPS)
            on = o * rinv
            g = gb_ref[:, sl]
            sgm = _sigmoid(g)
            silu_g = g * sgm
            dy = dy_ref[:, sl]
            wn_v = wn_ref[:, sl]
            gwn_ref[:, sl] += jnp.sum(dy * on * silu_g, axis=0, keepdims=True)
            dg_ref[:, sl] = (dy * on * wn_v * (sgm * (1.0 + g * (1.0 - sgm)))).astype(bf16)
            t1 = dy * wn_v * silu_g
            do = rinv * (t1 - on * jnp.mean(t1 * on, axis=-1, keepdims=True))
            do_s[:, sl] = do.astype(bf16)
            qt = qf * p["eb"]
            qt_s[:, sl] = qt.astype(bf16)
            kh_s[:, sl] = (key * p["er"]).astype(bf16)
            dec_s[:, sl] = jnp.exp(b + p["rem"])
            rng = jnp.max(-(b + p["rem"]))

            @pl.when(rng < SAFE_RANGE)
            def _():
                einv = jnp.exp(-b)
                kpf = key * einv
                kp = kpf.astype(bf16)
                cmask = _chunk_mask()
                for j in range(HG_T // HG_SUB):
                    rs = slice(HG_SUB * j, HG_SUB * (j + 1))
                    qtb, dob, vb = qt[rs].astype(bf16), do[rs].astype(bf16), v[rs].astype(bf16)
                    sc = jnp.where(cmask, _dot_nt(qtb, kp[rs]), 0.0).astype(bf16)
                    dsc = jnp.where(cmask, _dot_nt(dob, vb), 0.0).astype(bf16)
                    dqp = _dot(dsc, kp[rs])
                    dkp = _dot_tn(dsc, qtb)
                    dvi_s[rs, sl] = _dot_tn(sc, dob)
                    dqi_s[rs, sl] = dqp * p["eb"][rs]
                    dki_s[rs, sl] = dkp * einv[rs]
                    dbi_s[rs, sl] = dqp * qtb.astype(f32) - dkp * kp[rs].astype(f32)

            @pl.when(rng >= SAFE_RANGE)
            def _():
                ones = jnp.ones((HGRN_HD, HGRN_HD), bf16)
                dqf = jnp.zeros((HG_T, HGRN_HD), f32)
                dkey, db, dv = dqf, dqf, dqf
                for l in range(CHUNK):
                    if l == 0:
                        ks, vs, qe = key, v, qf
                    else:
                        e = jnp.exp(jnp.where(rc >= l, b - pltpu.roll(b, l, 0), NEG))
                        ks, vs, qe = pltpu.roll(key, l, 0), pltpu.roll(v, l, 0), qf * e
                    pr = qe * ks
                    rl = _dot(pr.astype(bf16), ones)
                    drl = _dot((do * vs).astype(bf16), ones)
                    if l == 0:
                        dqf = dqf + drl * ks
                        dv = dv + rl * do
                        dkey = dkey + drl * qe
                    else:
                        drl = jnp.where(rc >= l, drl, 0.0)
                        gl = drl * pr
                        dqf = dqf + drl * ks * e
                        dv = dv + pltpu.roll(rl * do, HG_T - l, 0)
                        dkey = dkey + pltpu.roll(drl * qe, HG_T - l, 0)
                        db = db + gl - pltpu.roll(gl, HG_T - l, 0)
                dvi_s[:, sl] = dv
                dqi_s[:, sl] = dqf
                dki_s[:, sl] = dkey
                dbi_s[:, sl] = db

        def step(k, carry):
            c = HG_T // CHUNK - 1 - k
            rows = pl.ds(pl.multiple_of(c * CHUNK, CHUNK), CHUNK)
            row0 = pl.ds(pl.multiple_of(c * CHUNK, CHUNK), 1)
            for h in range(N_HH):
                sl = slice(HGRN_HD * h, HGRN_HD * (h + 1))
                stp = st_ref[c, sl, :]
                dst = DST[h]
                dstb = dst.astype(bf16)
                dob = do_s[rows, sl]
                khb = kh_s[rows, sl]
                dec = dec_s[row0, sl]
                dqt_s[rows, sl] = _dot(dob, stp.astype(bf16))
                dkh = _dot(ib_ref[rows, sl].astype(bf16), dstb)
                dkh_s[rows, sl] = dkh
                dvi_s[rows, sl] += _dot_nt(khb, dstb)
                dbl = jnp.sum(dst * stp, axis=0, keepdims=True) * dec + jnp.sum(dkh * khb.astype(f32), axis=0, keepdims=True)
                dbl_s[rows, sl] = jnp.broadcast_to(dbl, (CHUNK, HGRN_HD))
                DST[h] = dst * dec + _dot_tn(dob, qt_s[rows, sl])
            return carry

        lax.fori_loop(0, HG_T // CHUNK, step, 0)

        for h in range(N_HH):
            sl = slice(HGRN_HD * h, HGRN_HD * (h + 1))
            qb = qb_ref[:, sl]
            p = _hgrn_prep(qb, fb_ref[:, sl], lbl_ref[:, sl], rc)
            sf, sq, lb = p["sf"], p["sq"], p["lb"]
            dqt, dkh = dqt_s[:, sl], dkh_s[:, sl]
            dqf = dqt * p["eb"] + dqi_s[:, sl]
            dkey = dkh * p["er"] + dki_s[:, sl]
            db = dqt * (p["qf"] * p["eb"]) - dkh * (p["key"] * p["er"]) + jnp.where(rc == CHUNK - 1, dbl_s[:, sl], 0.0) + dbi_s[:, sl]
            df = _chunk_rcumsum(db, rc) / p["f"] - dkey
            df_ref[:, sl] = (df * (1.0 - lb) * sf * (1.0 - sf)).astype(bf16)
            glb_ref[:, sl] += jnp.sum(df * (1.0 - sf), axis=0, keepdims=True)
            dq_ref[:, sl] = (dqf * (sq * (1.0 + qb * (1.0 - sq)))).astype(bf16)
            di_ref[:, sl] = dvi_s[:, sl].astype(bf16)

    rev = lambda i: nT - 1 - i
    col = lambda c: pl.BlockSpec((HG_T, HGRN_W), lambda i: (rev(i), c))
    tile = pl.BlockSpec((HG_T, HGRN_W), lambda i: (rev(i), 0))
    whole = lambda a: pl.BlockSpec(a.shape, lambda i: (0, 0))
    vec = pl.BlockSpec((1, HGRN_W), lambda i: (0, 0))
    tb = lambda: pltpu.VMEM((HG_T, HGRN_W), bf16)
    tf = lambda: pltpu.VMEM((HG_T, HGRN_W), f32)
    return pl.pallas_call(
        body, name="hgrn_bwd", grid=(nT,),
        in_specs=[col(0), col(1), col(2), col(3), whole(lbl), whole(wn), tile,
                  pl.BlockSpec((HG_T // CHUNK, HGRN_W, HGRN_HD), lambda i: (rev(i), 0, 0)),
                  pl.BlockSpec((HG_T, HGRN_W), lambda i: (rev(i), 1))],
        out_specs=[tile, tile, tile, tile, vec, vec],
        out_shape=[jax.ShapeDtypeStruct((S, HGRN_W), bf16)] * 4 + [jax.ShapeDtypeStruct((1, HGRN_W), f32)] * 2,
        scratch_shapes=[pltpu.VMEM((N_HH, HGRN_HD, HGRN_HD), f32), tb(), tb(), tf(), tb(), tf(), tf(), tf(), tf(), tf(), tf(), tf()],
        compiler_params=_cp("arbitrary"),
    )(hp, hp, hp, hp, lbl, wn, o_sav, states, dmix)


def _out_proj(x, ya, yb, wout, w2):
    S = x.shape[0]
    tm = 512

    def body(x_ref, ya_ref, yb_ref, w_ref, w2_ref, h1_ref, u2_ref, mix_ref):
        mixed = jnp.concatenate([ya_ref[...], yb_ref[...]], axis=1).astype(bf16)
        mix_ref[...] = mixed
        h1 = x_ref[...] + _dot(mixed, w_ref[...])
        h1_ref[...] = h1
        r = lax.rsqrt(jnp.mean(h1 * h1, axis=-1, keepdims=True) + EPS)
        u2_ref[...] = (h1 * r * w2_ref[...]).astype(bf16)

    row = lambda w: pl.BlockSpec((tm, w), lambda i: (i, 0))
    return pl.pallas_call(
        body, name="out_proj", grid=(S // tm,),
        in_specs=[row(D_MODEL), row(ATTN_W), row(HGRN_W), pl.BlockSpec((D_MODEL, D_MODEL), lambda i: (0, 0)),
                  pl.BlockSpec((1, D_MODEL), lambda i: (0, 0))],
        out_specs=[row(D_MODEL), row(D_MODEL), row(D_MODEL)],
        out_shape=[jax.ShapeDtypeStruct((S, D_MODEL), f32), jax.ShapeDtypeStruct((S, D_MODEL), bf16),
                   jax.ShapeDtypeStruct((S, D_MODEL), bf16)],
        compiler_params=_cp("arbitrary"),
    )(x, ya, yb, wout, w2)


def _gate_up(u2, wgu):
    S = u2.shape[0]
    tm, tn = 512, 1408
    nj = FFN // tn

    def body(u_ref, wg_ref, wu_ref, g_ref, up_ref, a_ref):
        u = u_ref[...]
        g = _dot(u, wg_ref[...])
        up = _dot(u, wu_ref[...])
        g_ref[...] = g.astype(bf16)
        up_ref[...] = up.astype(bf16)
        a_ref[...] = (g * _sigmoid(g) * up).astype(bf16)

    out = pl.BlockSpec((tm, tn), lambda j, i: (i, j))
    return pl.pallas_call(
        body, name="gate_up", grid=(nj, S // tm),
        in_specs=[pl.BlockSpec((tm, D_MODEL), lambda j, i: (i, 0)), pl.BlockSpec((D_MODEL, tn), lambda j, i: (0, j)),
                  pl.BlockSpec((D_MODEL, tn), lambda j, i: (0, j + nj))],
        out_specs=[out, out, out],
        out_shape=[jax.ShapeDtypeStruct((S, FFN), bf16)] * 3,
        compiler_params=_cp("arbitrary", "arbitrary"),
    )(u2, wgu, wgu)


def _rms_bwd(dyw, hn, r):
    return r * (dyw - hn * jnp.mean(dyw * hn, axis=-1, keepdims=True))


def _down_loss(act, wdown, h1, tgt, w3):
    S = act.shape[0]
    tm = 256

    def body(a_ref, w_ref, h1_ref, t_ref, w3_ref, dh2_ref, loss_ref, gw3_ref):
        @pl.when(pl.program_id(0) == 0)
        def _():
            loss_ref[...] = jnp.zeros_like(loss_ref)
            gw3_ref[...] = jnp.zeros_like(gw3_ref)

        h2 = h1_ref[...] + _dot(a_ref[...], w_ref[...])
        r = lax.rsqrt(jnp.mean(h2 * h2, axis=-1, keepdims=True) + EPS)
        hn = h2 * r
        w3 = w3_ref[...]
        err = hn * w3 - t_ref[...]
        loss_ref[...] += (0.5 / D_MODEL) * jnp.sum(err * err)
        dy = err * (1.0 / D_MODEL)
        gw3_ref[...] += jnp.sum(dy * hn, axis=0, keepdims=True)
        dh2_ref[...] = _rms_bwd(dy * w3, hn, r)

    row = lambda w: pl.BlockSpec((tm, w), lambda i: (i, 0))
    return pl.pallas_call(
        body, name="down_loss", grid=(S // tm,),
        in_specs=[row(FFN), pl.BlockSpec((FFN, D_MODEL), lambda i: (0, 0)), row(D_MODEL), row(D_MODEL),
                  pl.BlockSpec((1, D_MODEL), lambda i: (0, 0))],
        out_specs=[row(D_MODEL), pl.BlockSpec((1, 128), lambda i: (0, 0)), pl.BlockSpec((1, D_MODEL), lambda i: (0, 0))],
        out_shape=[jax.ShapeDtypeStruct((S, D_MODEL), f32), jax.ShapeDtypeStruct((1, 128), f32),
                   jax.ShapeDtypeStruct((1, D_MODEL), f32)],
        compiler_params=_cp("arbitrary"),
    )(act, wdown, h1, tgt, w3)


def _dact(dh2, wdown, gate, up):
    S = dh2.shape[0]
    tm = 256

    def body(d_ref, w_ref, g_ref, u_ref, dg_ref, du_ref):
        da = _dot_nt(d_ref[...].astype(bf16), w_ref[...])
        g = g_ref[...].astype(f32)
        sg = _sigmoid(g)
        du_ref[...] = (da * g * sg).astype(bf16)
        dg_ref[...] = (da * u_ref[...].astype(f32) * (sg * (1.0 + g * (1.0 - sg)))).astype(bf16)

    row = lambda w: pl.BlockSpec((tm, w), lambda i: (i, 0))
    return pl.pallas_call(
        body, name="dact", grid=(S // tm,),
        in_specs=[row(D_MODEL), pl.BlockSpec((FFN, D_MODEL), lambda i: (0, 0)), row(FFN), row(FFN)],
        out_specs=[row(FFN), row(FFN)],
        out_shape=[jax.ShapeDtypeStruct((S, FFN), bf16)] * 2,
        compiler_params=_cp("arbitrary"),
    )(dh2, wdown, gate, up)


def _dgu(dgate, dup, wgu, h1, w2, dh2):
    S = dgate.shape[0]
    tm = 256

    def body(dg_ref, du_ref, wg_ref, wu_ref, h1_ref, w2_ref, dh2_ref, dh1_ref, gw2_ref):
        @pl.when(pl.program_id(0) == 0)
        def _():
            gw2_ref[...] = jnp.zeros_like(gw2_ref)

        du2 = _dot_nt(dg_ref[...], wg_ref[...]) + _dot_nt(du_ref[...], wu_ref[...])
        h1 = h1_ref[...]
        r = lax.rsqrt(jnp.mean(h1 * h1, axis=-1, keepdims=True) + EPS)
        hn = h1 * r
        gw2_ref[...] += jnp.sum(du2 * hn, axis=0, keepdims=True)
        dh1_ref[...] = dh2_ref[...] + _rms_bwd(du2 * w2_ref[...], hn, r)

    row = lambda w: pl.BlockSpec((tm, w), lambda i: (i, 0))
    return pl.pallas_call(
        body, name="dgu", grid=(S // tm,),
        in_specs=[row(FFN), row(FFN), pl.BlockSpec((D_MODEL, FFN), lambda i: (0, 0)),
                  pl.BlockSpec((D_MODEL, FFN), lambda i: (0, 1)), row(D_MODEL),
                  pl.BlockSpec((1, D_MODEL), lambda i: (0, 0)), row(D_MODEL)],
        out_specs=[row(D_MODEL), pl.BlockSpec((1, D_MODEL), lambda i: (0, 0))],
        out_shape=[jax.ShapeDtypeStruct((S, D_MODEL), f32), jax.ShapeDtypeStruct((1, D_MODEL), f32)],
        compiler_params=_cp("arbitrary"),
    )(dgate, dup, wgu, wgu, h1, w2, dh2)


def _dmixed(dh1, wout):
    S = dh1.shape[0]
    tm = 512

    def body(d_ref, w_ref, o_ref):
        o_ref[...] = _dot_nt(d_ref[...].astype(bf16), w_ref[...])

    row = pl.BlockSpec((tm, D_MODEL), lambda i: (i, 0))
    return pl.pallas_call(
        body, name="dmixed", grid=(S // tm,),
        in_specs=[row, pl.BlockSpec((D_MODEL, D_MODEL), lambda i: (0, 0))], out_specs=row,
        out_shape=jax.ShapeDtypeStruct((S, D_MODEL), f32), compiler_params=_cp("arbitrary"),
    )(dh1, wout)


def _din(dq, dk, dv, dhq, dhf, dhi, dhg, cos_t, sg_t, win, x, w1, dh1):
    S = x.shape[0]
    tm = 256

    def body(dq_ref, dk_ref, dv_ref, dhq_ref, dhf_ref, dhi_ref, dhg_ref, cos_ref, sg_ref, w_ref, x_ref, w1_ref, dh1_ref,
             dp_ref, gx_ref, gw1_ref):
        @pl.when(pl.program_id(0) == 0)
        def _():
            gw1_ref[...] = jnp.zeros_like(gw1_ref)

        cosv, sgv = cos_ref[...], sg_ref[...]
        unrope = lambda d: d * cosv - sgv * _swap_halves(d)
        parts = [(unrope(dq_ref[...]) * (HEAD_DIM ** -0.5)).astype(bf16), unrope(dk_ref[...]).astype(bf16),
                 dv_ref[...].astype(bf16), dhq_ref[...], dhf_ref[...], dhi_ref[...], dhg_ref[...]]
        du = jnp.zeros((tm, D_MODEL), f32)
        for j, pj in enumerate(parts):
            dp_ref[:, j * 512:(j + 1) * 512] = pj
            du = du + _dot_nt(pj, w_ref[:, j * 512:(j + 1) * 512])
        xv = x_ref[...]
        r = lax.rsqrt(jnp.mean(xv * xv, axis=-1, keepdims=True) + EPS)
        xn = xv * r
        gw1_ref[...] += jnp.sum(du * xn, axis=0, keepdims=True)
        gx_ref[...] = dh1_ref[...] + _rms_bwd(du * w1_ref[...], xn, r)

    row = lambda w: pl.BlockSpec((tm, w), lambda i: (i, 0))
    vec = pl.BlockSpec((1, D_MODEL), lambda i: (0, 0))
    return pl.pallas_call(
        body, name="din", grid=(S // tm,),
        in_specs=[row(512)] * 7 + [row(512), row(512), pl.BlockSpec((D_MODEL, IN_W), lambda i: (0, 0)), row(D_MODEL), vec,
                                   row(D_MODEL)],
        out_specs=[row(IN_W), row(D_MODEL), vec],
        out_shape=[jax.ShapeDtypeStruct((S, IN_W), bf16), jax.ShapeDtypeStruct((S, D_MODEL), f32),
                   jax.ShapeDtypeStruct((1, D_MODEL), f32)],
        compiler_params=_cp("arbitrary"),
    )(dq, dk, dv, dhq, dhf, dhi, dhg, cos_t, sg_t, win, x, w1, dh1)


def _gw(a, b, tn, name):
    S, M = a.shape
    N = b.shape[1]
    ts = 512

    def body(a_ref, b_ref, o_ref):
        @pl.when(pl.program_id(1) == 0)
        def _():
            o_ref[...] = jnp.zeros_like(o_ref)

        o_ref[...] += _dot_tn(a_ref[...].astype(bf16), b_ref[...].astype(bf16))

    return pl.pallas_call(
        body, name=name, grid=(N // tn, S // ts),
        in_specs=[pl.BlockSpec((ts, M), lambda j, s: (s, 0)), pl.BlockSpec((ts, tn), lambda j, s: (s, j))],
        out_specs=pl.BlockSpec((M, tn), lambda j, s: (0, j)), out_shape=jax.ShapeDtypeStruct((M, N), f32),
        compiler_params=_cp("arbitrary", "arbitrary"),
    )(a, b)


def _local_step(x, tgt, w1, win, lbl, wn, wout, w2, wgu, wdown, w3):
    S = x.shape[0]
    cos_t, sg_t = _rope_tables(S)
    u, qkv, hp = _in_proj(x, w1, win, cos_t, sg_t)
    att = None
    for dil in DILATIONS:
        att = _attn_fwd(qkv, dil, att)
    ya, lse = att
    yb, o_sav, states = _hgrn_fwd(hp, lbl, wn)
    h1, u2, mixed = _out_proj(x, ya, yb, wout, w2)
    gate, up, act = _gate_up(u2, wgu)
    dh2, loss, g_w3 = _down_loss(act, wdown, h1, tgt, w3)

    g_wdown = _gw(act, dh2, 512, "gw_down")
    dgate, dup = _dact(dh2, wdown, gate, up)
    g_wgu = (_gw(u2, dgate, 1408, "gw_gate"), _gw(u2, dup, 1408, "gw_up"))
    dh1, g_w2 = _dgu(dgate, dup, wgu, h1, w2, dh2)
    g_wout = _gw(mixed, dh1, 1024, "gw_out")
    dmix = _dmixed(dh1, wout)
    dhq, dhf, dhi, dhg, g_wn, g_lb = _hgrn_bwd(hp, lbl, wn, o_sav, states, dmix)
    datt = None
    for dil in DILATIONS:
        datt = _attn_bwd(qkv, ya, lse, dmix, dil, datt)
    dproj, gx, g_w1 = _din(*datt, dhq, dhf, dhi, dhg, cos_t, sg_t, win, x, w1, dh1)
    g_win = _gw(u, dproj, 896, "gw_in")
    return loss, gx, (g_win, g_wout, g_wgu, g_wdown), (g_w1, g_lb, g_wn, g_w2, g_w3)


MESH = pl.DeviceIdType.MESH
ANY = pl.BlockSpec(memory_space=pl.ANY)
VMEM_SPEC = pl.BlockSpec(memory_space=pltpu.VMEM)


def _pos():
    return lax.axis_index("x"), lax.axis_index("y"), lax.axis_index("c")


def _flip(v, bit):
    return 1 - v if bit else v


def _all_gather(shards):
    n = len(shards)

    def body(*refs):
        ins, outs, bufs = refs[:n], refs[n:2 * n], refs[2 * n:3 * n]
        send_sems, recv_sems, local_sems = refs[3 * n:]
        x, y, c = _pos()
        me, sibling = (x, y, c), (x, y, 1 - c)
        chips = [(1 - x, y), (x, 1 - y), (1 - x, 1 - y)]

        def copy(a, k, block, to, src=None):
            dst = outs[a].at[4 * block[0] + 2 * block[1] + block[2]]
            return pltpu.make_async_remote_copy(src_ref=dst if src is None else src, dst_ref=dst, send_sem=send_sems.at[a, k],
                                                recv_sem=recv_sems.at[a, k], device_id=to, device_id_type=MESH)

        loads = [pltpu.make_async_copy(ins[a], bufs[a], local_sems.at[a]) for a in range(n)]
        for ld in loads:
            ld.start()
        local, sends = [], []
        for a in range(n):
            loads[a].wait()
            mine = pltpu.make_async_copy(bufs[a], outs[a].at[4 * x + 2 * y + c], local_sems.at[a])
            mine.start()
            local.append(mine)
            first = [copy(a, 0, me, sibling, src=bufs[a])] + [copy(a, 1 + j, me, (*chip, c), src=bufs[a]) for j, chip in enumerate(chips)]
            for cp in first:
                cp.start()
            sends += first
        for a in range(n):
            for j, chip in enumerate(chips):
                copy(a, 1 + j, (*chip, c), me).wait_recv()
                passed = copy(a, 4 + j, (*chip, c), sibling)
                passed.start()
                sends.append(passed)
        for a in range(n):
            copy(a, 0, sibling, me).wait_recv()
            for j, chip in enumerate(chips):
                copy(a, 4 + j, (*chip, 1 - c), me).wait_recv()
        for cp in sends:
            cp.wait_send()
        for mine in local:
            mine.wait()

    return pl.pallas_call(
        body, name="gather_weights", in_specs=[ANY] * n, out_specs=[ANY] * n,
        out_shape=[jax.ShapeDtypeStruct((N_DEV,) + s.shape, s.dtype) for s in shards],
        scratch_shapes=[pltpu.VMEM(s.shape, s.dtype) for s in shards]
        + [pltpu.SemaphoreType.DMA((n, 7)), pltpu.SemaphoreType.DMA((n, 7)), pltpu.SemaphoreType.DMA((n,))],
    )(*shards)


def _rs_sibling(grads):
    n = len(grads)

    def body(*refs):
        g, got = refs[:n], refs[n:2 * n]
        send_sems, recv_sems = refs[2 * n:]
        x, y, c = _pos()
        copies = []
        for a in range(n):
            for q in range(4):
                cp = pltpu.make_async_remote_copy(src_ref=g[a].at[2 * q + (1 - c)], dst_ref=got[a].at[q], send_sem=send_sems.at[a, q],
                                                  recv_sem=recv_sems.at[a, q], device_id=(x, y, 1 - c), device_id_type=MESH)
                cp.start()
                copies.append(cp)
        for cp in copies:
            cp.wait()

    return pl.pallas_call(
        body, name="reduce_sibling", in_specs=[ANY] * n, out_specs=[ANY] * n,
        out_shape=[jax.ShapeDtypeStruct((4,) + g.shape[1:], g.dtype) for g in grads],
        scratch_shapes=[pltpu.SemaphoreType.DMA((n, 4))] * 2,
    )(*grads)


def _rs_chips(sums):
    n = len(sums)

    def body(*refs):
        s, out = refs[:n], refs[n:2 * n]
        send_sems, recv_sems = refs[2 * n:]
        x, y, c = _pos()
        copies = []
        for a in range(n):
            for f in (1, 2, 3):
                peer = (_flip(x, f >> 1), _flip(y, f & 1), c)
                cp = pltpu.make_async_remote_copy(src_ref=s[a].at[2 * peer[0] + peer[1]], dst_ref=out[a].at[f - 1],
                                                  send_sem=send_sems.at[a, f - 1], recv_sem=recv_sems.at[a, f - 1], device_id=peer,
                                                  device_id_type=MESH)
                cp.start()
                copies.append(cp)
        for cp in copies:
            cp.wait()

    return pl.pallas_call(
        body, name="reduce_chips", in_specs=[ANY] * n, out_specs=[ANY] * n,
        out_shape=[jax.ShapeDtypeStruct((3,) + s.shape[1:], s.dtype) for s in sums],
        scratch_shapes=[pltpu.SemaphoreType.DMA((n, 3))] * 2,
    )(*sums)


def _gather_small(g_w1, g_w2, g_w3, g_lb, g_wn, loss):
    def body(w1_ref, w2_ref, w3_ref, lb_ref, wn_ref, loss_ref, out_ref, pk, send_sems, recv_sems):
        x, y, c = _pos()
        me = 4 * x + 2 * y + c
        pk[...] = jnp.zeros_like(pk)
        pk[0:1, :] = w1_ref[...]
        pk[1:2, :] = w2_ref[...]
        pk[2:3, :] = w3_ref[...]
        pk[3:4, 0:HGRN_W] = lb_ref[...]
        pk[3:4, HGRN_W:2 * HGRN_W] = wn_ref[...]
        pk[4:5, 0:128] = loss_ref[...]
        out_ref[me] = pk[...]
        sends, recvs = [], []
        for k in range(1, N_DEV):
            peer = (_flip(x, k >> 2), _flip(y, (k >> 1) & 1), _flip(c, k & 1))
            cp = pltpu.make_async_remote_copy(src_ref=pk, dst_ref=out_ref.at[me], send_sem=send_sems.at[k - 1],
                                              recv_sem=recv_sems.at[k - 1], device_id=peer, device_id_type=MESH)
            cp.start()
            sends.append(cp)
            recvs.append(pltpu.make_async_remote_copy(src_ref=pk, dst_ref=out_ref.at[4 * peer[0] + 2 * peer[1] + peer[2]],
                                                      send_sem=send_sems.at[k - 1], recv_sem=recv_sems.at[k - 1], device_id=peer,
                                                      device_id_type=MESH))
        for cp in recvs:
            cp.wait_recv()
        for cp in sends:
            cp.wait_send()

    return pl.pallas_call(
        body, name="gather_small", in_specs=[VMEM_SPEC] * 6, out_specs=VMEM_SPEC,
        out_shape=jax.ShapeDtypeStruct((N_DEV, 8, D_MODEL), f32),
        scratch_shapes=[pltpu.VMEM((8, D_MODEL), f32), pltpu.SemaphoreType.DMA((N_DEV - 1,)), pltpu.SemaphoreType.DMA((N_DEV - 1,))],
    )(g_w1, g_w2, g_w3, g_lb, g_wn, loss)


def _row_tile(r):
    return max(t for t in range(8, 257, 8) if r % t == 0)


def _add_sibling(core, g, got, name):
    _, r, c = got.shape
    tr = _row_tile(r)

    def body(core_ref, a_ref, b_ref, o_ref):
        o_ref[...] = (a_ref[...] + b_ref[...]).astype(bf16)

    blk = pl.BlockSpec((1, tr, c), lambda q, i, core_ref: (q, i, 0))
    return pl.pallas_call(
        body, name=name, out_shape=jax.ShapeDtypeStruct(got.shape, bf16),
        grid_spec=pltpu.PrefetchScalarGridSpec(
            num_scalar_prefetch=1, grid=(4, r // tr),
            in_specs=[pl.BlockSpec((1, tr, c), lambda q, i, core_ref: (2 * q + core_ref[0], i, 0)), blk], out_specs=blk),
        compiler_params=_cp("arbitrary", "arbitrary"))(core, g, got)


def _adamw(w, g, m, v):
    m = ADAM_B1 * m + (1.0 - ADAM_B1) * g
    v = ADAM_B2 * v + (1.0 - ADAM_B2) * (g * g)
    m_hat = m / (1.0 - ADAM_B1 ** ADAM_STEP)
    v_hat = v / (1.0 - ADAM_B2 ** ADAM_STEP)
    return -ADAM_LR * (m_hat / (jnp.sqrt(v_hat) + ADAM_EPS) + ADAM_WD * w), m, v


def _adam_shard(where, g, got, pieces, w, m, v, name):
    r, c = w.shape
    tr = _row_tile(r)

    def body(where_ref, g_ref, got_ref, p_ref, w_ref, m_ref, v_ref, g_out, d_out, m_out, v_out):
        gsum = g_ref[0] + got_ref[0]
        for f in range(3):
            gsum = gsum + p_ref[f].astype(f32)
        g_out[...] = gsum
        d_out[...], m_out[...], v_out[...] = _adamw(w_ref[...], gsum, m_ref[...], v_ref[...])

    blk = pl.BlockSpec((tr, c), lambda i, where_ref: (i, 0))
    return pl.pallas_call(
        body, name=name, out_shape=[jax.ShapeDtypeStruct((r, c), f32)] * 4,
        grid_spec=pltpu.PrefetchScalarGridSpec(
            num_scalar_prefetch=1, grid=(r // tr,),
            in_specs=[pl.BlockSpec((1, tr, c), lambda i, where_ref: (where_ref[0], i, 0)),
                      pl.BlockSpec((1, tr, c), lambda i, where_ref: (where_ref[1], i, 0)),
                      pl.BlockSpec((3, tr, c), lambda i, where_ref: (0, i, 0)), blk, blk, blk],
            out_specs=[blk] * 4),
        compiler_params=_cp("arbitrary"),
    )(where, g, got, pieces, w, m, v)


def _small_update(gath, params):
    def body(gath_ref, *refs):
        ins, outs = refs[:15], refs[15:]
        gs = gath_ref[0]
        for k in range(1, N_DEV):
            gs = gs + gath_ref[k]
        outs[0][...] = gs[4:5, 0:128]
        l0, l1 = ins[9][0:1, :], ins[9][1:2, :]
        lb = _sigmoid(l0 - l1)
        d0 = gs[3:4, 0:HGRN_W] * lb * (1.0 - lb)
        first_row = lax.broadcasted_iota(jnp.int32, (2, HGRN_W), 0) == 0
        grads = [gs[0:1, :], gs[1:2, :], gs[2:3, :], jnp.where(first_row, d0, -d0), gs[3:4, HGRN_W:2 * HGRN_W]]
        for i, g in enumerate(grads):
            w_ref, m_ref, v_ref = ins[3 * i:3 * i + 3]
            o = outs[1 + 4 * i:5 + 4 * i]
            o[0][...] = g
            o[1][...], o[2][...], o[3][...] = _adamw(w_ref[...], g, m_ref[...], v_ref[...])

    flat = [a for p in params for a in p]
    out_shape = [jax.ShapeDtypeStruct((1, 128), f32)] + [jax.ShapeDtypeStruct(p[0].shape, f32) for p in params for _ in range(4)]
    outs = pl.pallas_call(body, name="small_update", in_specs=[VMEM_SPEC] * 16, out_specs=[VMEM_SPEC] * 21, out_shape=out_shape)(gath, *flat)
    return outs[0], [outs[1 + 4 * i:5 + 4 * i] for i in range(5)]


def kernel(x, norm1_w, w_in, lb_logits, hgrn_norm_w, w_out, norm2_w, w_gate_up, w_down, final_norm_w, loss_target, m_norm1_w, m_w_in, m_lb_logits, m_hgrn_norm_w, m_w_out, m_norm2_w, m_w_gate_up, m_w_down, m_final_norm_w, v_norm1_w, v_w_in, v_lb_logits, v_hgrn_norm_w, v_w_out, v_norm2_w, v_w_gate_up, v_w_down, v_final_norm_w):
    row = lambda a: a.reshape(1, D_MODEL)
    shards = [w_in[0], w_out[0], w_gate_up[0], w_down[0]]
    win_g, wout_g, wgu_g, wdown_g = _all_gather([s.astype(bf16) for s in shards])
    win = jnp.transpose(win_g, (1, 0, 2)).reshape(D_MODEL, IN_W)
    wgu = jnp.transpose(wgu_g, (1, 0, 2)).reshape(D_MODEL, 2 * FFN)
    wout = wout_g.reshape(D_MODEL, D_MODEL)
    wdown = wdown_g.reshape(FFN, D_MODEL)

    loss_p, gx, (g_win, g_wout, g_wgu, g_wdown), (g_w1, g_lb, g_wn, g_w2, g_w3) = _local_step(
        x[0], loss_target[0], norm1_w, win, lb_logits, hgrn_norm_w, wout, norm2_w, wgu, wdown, row(final_norm_w))

    by_owner = lambda g, w: jnp.transpose(g.reshape(g.shape[0], g.shape[1] // w, w), (1, 0, 2))
    grads = [by_owner(g_win, IN_W // N_DEV), g_wout.reshape(N_DEV, D_MODEL // N_DEV, D_MODEL),
             jnp.concatenate([by_owner(g, 2 * FFN // N_DEV) for g in g_wgu], axis=0), g_wdown.reshape(N_DEV, FFN // N_DEV, D_MODEL)]
    ix, iy, ic = lax.axis_index("x"), lax.axis_index("y"), lax.axis_index("c")
    core = jnp.stack([ic]).astype(jnp.int32)
    where = jnp.stack([4 * ix + 2 * iy + ic, 2 * ix + iy]).astype(jnp.int32)
    got = _rs_sibling(grads)
    sums = [_add_sibling(core, g, o, f"add_sibling_{i}") for i, (g, o) in enumerate(zip(grads, got))]
    pieces = _rs_chips(sums)
    moms = [(m_w_in[0], v_w_in[0]), (m_w_out[0], v_w_out[0]), (m_w_gate_up[0], v_w_gate_up[0]), (m_w_down[0], v_w_down[0])]
    big = [_adam_shard(where, g, o, p, w, m, v, f"adam_{i}")
           for i, (g, o, p, w, (m, v)) in enumerate(zip(grads, got, pieces, shards, moms))]
    big = [[a[None] for a in four] for four in big]

    gath = _gather_small(g_w1, g_w2, g_w3, g_lb, g_wn, loss_p)
    params = [(norm1_w, m_norm1_w, v_norm1_w), (norm2_w, m_norm2_w, v_norm2_w),
              (row(final_norm_w), row(m_final_norm_w), row(v_final_norm_w)),
              (lb_logits, m_lb_logits, v_lb_logits), (hgrn_norm_w, m_hgrn_norm_w, v_hgrn_norm_w)]
    loss, (s_w1, s_w2, s_w3, s_lb, s_wn) = _small_update(gath, params)
    s_w3 = [a.reshape(D_MODEL) for a in s_w3]
    per_w = [s_w1, big[0], s_lb, s_wn, big[1], s_w2, big[2], big[3], s_w3]
    return (loss[0, 0], gx[None], *[p[0] for p in per_w], *[p[1] for p in per_w], *[p[2] for p in per_w], *[p[3] for p in per_w])
```

```python
import functools

import jax
import jax.numpy as jnp
from jax import lax
from jax.experimental import pallas as pl
from jax.experimental.pallas import tpu as pltpu

f32, bf16 = jnp.float32, jnp.bfloat16

D_MODEL = 1024
ATTN_W = 512
HEAD_DIM = 64
ATTN_BLK = 128
DILATIONS = (1, 4, 16)
HGRN_W = 512
HGRN_HD = 128
CHUNK = 16
IN_W = 3 * ATTN_W + 4 * HGRN_W
FFN = 2816
EPS = 1e-6
ROPE_THETA = 10000.0
NEG = -1e30
N_DEV = 8
ADAM_LR, ADAM_B1, ADAM_B2, ADAM_EPS, ADAM_WD, ADAM_STEP = 0.001, 0.9, 0.999, 1e-08, 0.01, 10
VMEM_LIMIT = 56 * 1024 * 1024


def _cp(*sem):
    return pltpu.CompilerParams(dimension_semantics=sem, vmem_limit_bytes=VMEM_LIMIT)


def _dot(a, b):
    return jnp.dot(a, b, preferred_element_type=f32)


def _dot_nt(a, b):
    return lax.dot_general(a, b, (((1,), (1,)), ((), ())), preferred_element_type=f32)


def _dot_tn(a, b):
    return lax.dot_general(a, b, (((0,), (0,)), ((), ())), preferred_element_type=f32)


def _sigmoid(x):
    return 1.0 / (1.0 + jnp.exp(-x))


def _rope_tables(S):
    half = HEAD_DIM // 2
    inv_freq = ROPE_THETA ** (-jnp.arange(half, dtype=f32) / half)
    ang = jnp.arange(S, dtype=f32)[:, None] * inv_freq[None, :]
    cos, sin = jnp.cos(ang), jnp.sin(ang)
    cos_t = jnp.tile(jnp.concatenate([cos, cos], axis=1), (1, ATTN_W // HEAD_DIM))
    sg_t = jnp.tile(jnp.concatenate([-sin, sin], axis=1), (1, ATTN_W // HEAD_DIM))
    return cos_t, sg_t


def _swap_halves(v):
    n = v.shape[1]
    lane = lax.broadcasted_iota(jnp.int32, v.shape, 1)
    return jnp.where((lane % HEAD_DIM) < HEAD_DIM // 2, pltpu.roll(v, n - HEAD_DIM // 2, 1), pltpu.roll(v, HEAD_DIM // 2, 1))


def _in_proj(x, w1, win, cos_t, sg_t):
    S = x.shape[0]
    tm = 256

    def body(x_ref, w1_ref, w_ref, cos_ref, sg_ref, u_ref, qkv_ref, hp_ref):
        xv = x_ref[...]
        r = lax.rsqrt(jnp.mean(xv * xv, axis=-1, keepdims=True) + EPS)
        u = (xv * r * w1_ref[...]).astype(bf16)
        u_ref[...] = u
        cosv, sgv = cos_ref[...], sg_ref[...]
        for j in range(3):
            pj = _dot(u, w_ref[:, j * ATTN_W:(j + 1) * ATTN_W])
            if j < 2:
                pj = pj * cosv + _swap_halves(pj) * sgv
            if j == 0:
                pj = pj * (HEAD_DIM ** -0.5)
            qkv_ref[:, j * ATTN_W:(j + 1) * ATTN_W] = pj.astype(bf16)
        for j in range(4):
            lo = 3 * ATTN_W + j * HGRN_W
            hp_ref[:, j * HGRN_W:(j + 1) * HGRN_W] = _dot(u, w_ref[:, lo:lo + HGRN_W])

    return pl.pallas_call(
        body, name="in_proj", grid=(S // tm,),
        in_specs=[pl.BlockSpec((tm, D_MODEL), lambda i: (i, 0)), pl.BlockSpec((1, D_MODEL), lambda i: (0, 0)),
                  pl.BlockSpec((D_MODEL, IN_W), lambda i: (0, 0)),
                  pl.BlockSpec((tm, ATTN_W), lambda i: (i, 0)), pl.BlockSpec((tm, ATTN_W), lambda i: (i, 0))],
        out_specs=[pl.BlockSpec((tm, D_MODEL), lambda i: (i, 0)), pl.BlockSpec((tm, 3 * ATTN_W), lambda i: (i, 0)),
                   pl.BlockSpec((tm, 4 * HGRN_W), lambda i: (i, 0))],
        out_shape=[jax.ShapeDtypeStruct((S, D_MODEL), bf16), jax.ShapeDtypeStruct((S, 3 * ATTN_W), bf16),
                   jax.ShapeDtypeStruct((S, 4 * HGRN_W), f32)],
        compiler_params=_cp("arbitrary"),
    )(x, w1, win, cos_t, sg_t)


def _valid_mask(n):
    qi = lax.broadcasted_iota(jnp.int32, (ATTN_BLK, 2 * ATTN_BLK), 0)
    kj = lax.broadcasted_iota(jnp.int32, (ATTN_BLK, 2 * ATTN_BLK), 1)
    delta = ATTN_BLK + qi - kj
    return (delta >= 0) & (delta <= ATTN_BLK) & ((n > 0) | (kj >= ATTN_BLK))


def _head_masks():
    lane = lax.broadcasted_iota(jnp.int32, (ATTN_BLK, 128), 1)
    even = lane < HEAD_DIM
    return even, (even, jnp.logical_not(even))


def _attn_fwd(qkv, dil, prev):
    S = qkv.shape[0]
    L = S // dil
    nb = L // ATTN_BLK
    W3 = 3 * ATTN_W
    has_prev = prev is not None

    def body(*refs):
        if has_prev:
            q_ref, kp_ref, kc_ref, vp_ref, vc_ref, yp_ref, lp_ref, y_ref, l_ref = refs
        else:
            q_ref, kp_ref, kc_ref, vp_ref, vc_ref, y_ref, l_ref = refs
        n = pl.program_id(1)
        valid = _valid_mask(n)
        even, masks = _head_masks()
        for p in range(ATTN_W // 128):
            sl = slice(128 * p, 128 * p + 128)
            q2 = q_ref[:, sl].astype(f32)
            k2 = jnp.concatenate([kp_ref[:, sl], kc_ref[:, sl]], axis=0)
            v2 = jnp.concatenate([vp_ref[:, sl], vc_ref[:, sl]], axis=0)
            outs, lses = [], []
            for e in range(2):
                qm = jnp.where(masks[e], q2, 0.0).astype(bf16)
                s = jnp.where(valid, _dot_nt(qm, k2), NEG)
                m = jnp.max(s, axis=-1, keepdims=True)
                pe = jnp.exp(s - m)
                lsum = jnp.sum(pe, axis=-1, keepdims=True)
                acc = _dot(pe.astype(bf16), v2)
                outs.append(acc / lsum)
                lses.append(jnp.broadcast_to(m + jnp.log(lsum), (ATTN_BLK, 128)))
            out = jnp.where(even, outs[0], outs[1])
            lse = jnp.where(even, lses[0], lses[1])
            if has_prev:
                lp = lp_ref[:, sl]
                mx = jnp.maximum(lp, lse)
                a, b = jnp.exp(lp - mx), jnp.exp(lse - mx)
                tot = a + b
                out = (a * yp_ref[:, sl] + b * out) / tot
                lse = mx + jnp.log(tot)
            y_ref[:, sl] = out
            l_ref[:, sl] = lse

    blk = (ATTN_BLK, ATTN_W)
    cur = lambda c: (lambda r, n: (n, 3 * r + c))
    prv = lambda c: (lambda r, n: (jnp.maximum(n - 1, 0), 3 * r + c))
    in_specs = [pl.BlockSpec(blk, cur(0)), pl.BlockSpec(blk, prv(1)), pl.BlockSpec(blk, cur(1)),
                pl.BlockSpec(blk, prv(2)), pl.BlockSpec(blk, cur(2))]
    qv = qkv.reshape(L, dil * W3)
    args = [qv, qv, qv, qv, qv]
    if has_prev:
        in_specs += [pl.BlockSpec(blk, lambda r, n: (n, r))] * 2
        args += [prev[0].reshape(L, dil * ATTN_W), prev[1].reshape(L, dil * ATTN_W)]
    y, lse = pl.pallas_call(
        body, name=f"attn_fwd_d{dil}", grid=(dil, nb), in_specs=in_specs,
        out_specs=[pl.BlockSpec(blk, lambda r, n: (n, r))] * 2,
        out_shape=[jax.ShapeDtypeStruct((L, dil * ATTN_W), f32)] * 2,
        compiler_params=_cp("arbitrary", "arbitrary"),
    )(*args)
    return y.reshape(S, ATTN_W), lse.reshape(S, ATTN_W)


def _attn_bwd(qkv, ya, lse, dmix, dil, prev):
    S = qkv.shape[0]
    L = S // dil
    nb = L // ATTN_BLK
    W3 = 3 * ATTN_W
    has_prev = prev is not None

    def body(*refs):
        q_ref, kp_ref, kc_ref, vp_ref, vc_ref, y_ref, l_ref, dy_ref = refs[:8]
        refs = refs[8:]
        if has_prev:
            dqp_ref, dkp_ref, dvp_ref = refs[:3]
            refs = refs[3:]
        dq_ref, dk_ref, dv_ref, ck, cv = refs
        n = pl.program_id(1)

        @pl.when(n == 0)
        def _():
            ck[...] = jnp.zeros_like(ck)
            cv[...] = jnp.zeros_like(cv)

        @pl.when(n < nb)
        def _():
            valid = _valid_mask(n)
            even, masks = _head_masks()
            li = lax.broadcasted_iota(jnp.int32, (128, 128), 0)
            lj = lax.broadcasted_iota(jnp.int32, (128, 128), 1)
            seg = jnp.where((li // HEAD_DIM) == (lj // HEAD_DIM), 1.0, 0.0).astype(bf16)
            for p in range(ATTN_W // 128):
                sl = slice(128 * p, 128 * p + 128)
                q2 = q_ref[:, sl].astype(f32)
                k2 = jnp.concatenate([kp_ref[:, sl], kc_ref[:, sl]], axis=0)
                v2 = jnp.concatenate([vp_ref[:, sl], vc_ref[:, sl]], axis=0)
                k2f = k2.astype(f32)
                dy2 = dy_ref[:, sl]
                lse2 = l_ref[:, sl]
                dyy = dy2 * y_ref[:, sl]
                hi = dyy.astype(bf16)
                lo = (dyy - hi.astype(f32)).astype(bf16)
                delta2 = _dot(hi, seg) + _dot(lo, seg)
                dq2 = jnp.zeros((ATTN_BLK, 128), f32)
                dk2 = jnp.zeros((2 * ATTN_BLK, 128), f32)
                dv2 = jnp.zeros((2 * ATTN_BLK, 128), f32)
                for e in range(2):
                    c0 = e * HEAD_DIM
                    qm = jnp.where(masks[e], q2, 0.0).astype(bf16)
                    km = jnp.where(masks[e][:1, :], k2f, 0.0).astype(bf16)
                    dym = jnp.where(masks[e], dy2, 0.0).astype(bf16)
                    s = _dot_nt(qm, k2)
                    pe = jnp.where(valid, jnp.exp(s - lse2[:, c0:c0 + 1]), 0.0)
                    dp = _dot_nt(dym, v2)
                    ds = (pe * (dp - delta2[:, c0:c0 + 1])).astype(bf16)
                    dv2 = dv2 + _dot_tn(pe.astype(bf16), dym)
                    dq2 = dq2 + _dot(ds, km)
                    dk2 = dk2 + _dot_tn(ds, qm)
                tk = dk2[:ATTN_BLK] + ck[:, sl]
                tv = dv2[:ATTN_BLK] + cv[:, sl]
                if has_prev:
                    dq2 = dq2 + dqp_ref[:, sl]
                    tk = tk + dkp_ref[:, sl]
                    tv = tv + dvp_ref[:, sl]
                dq_ref[:, sl] = dq2
                dk_ref[:, sl] = tk
                dv_ref[:, sl] = tv
                ck[:, sl] = dk2[ATTN_BLK:]
                cv[:, sl] = dv2[ATTN_BLK:]

        @pl.when(n == nb)
        def _():
            if has_prev:
                dk_ref[...] = ck[...] + dkp_ref[...]
                dv_ref[...] = cv[...] + dvp_ref[...]
            else:
                dk_ref[...] = ck[...]
                dv_ref[...] = cv[...]

    blk = (ATTN_BLK, ATTN_W)
    cn = lambda n: jnp.minimum(n, nb - 1)
    pn = lambda n: jnp.clip(n - 1, 0, nb - 1)
    in_specs = [pl.BlockSpec(blk, lambda r, n: (cn(n), 3 * r)),
                pl.BlockSpec(blk, lambda r, n: (pn(n), 3 * r + 1)), pl.BlockSpec(blk, lambda r, n: (cn(n), 3 * r + 1)),
                pl.BlockSpec(blk, lambda r, n: (pn(n), 3 * r + 2)), pl.BlockSpec(blk, lambda r, n: (cn(n), 3 * r + 2)),
                pl.BlockSpec(blk, lambda r, n: (cn(n), r)), pl.BlockSpec(blk, lambda r, n: (cn(n), r)),
                pl.BlockSpec(blk, lambda r, n: (cn(n), 2 * r))]
    qv = qkv.reshape(L, dil * W3)
    view = lambda a: a.reshape(L, dil * ATTN_W)
    args = [qv, qv, qv, qv, qv, view(ya), view(lse), dmix.reshape(L, dil * D_MODEL)]
    if has_prev:
        in_specs += [pl.BlockSpec(blk, lambda r, n: (cn(n), r)), pl.BlockSpec(blk, lambda r, n: (pn(n), r)),
                     pl.BlockSpec(blk, lambda r, n: (pn(n), r))]
        args += [view(a) for a in prev]
    outs = pl.pallas_call(
        body, name=f"attn_bwd_d{dil}", grid=(dil, nb + 1), in_specs=in_specs,
        out_specs=[pl.BlockSpec(blk, lambda r, n: (cn(n), r)), pl.BlockSpec(blk, lambda r, n: (pn(n), r)),
                   pl.BlockSpec(blk, lambda r, n: (pn(n), r))],
        out_shape=[jax.ShapeDtypeStruct((L, dil * ATTN_W), f32)] * 3,
        scratch_shapes=[pltpu.VMEM(blk, f32), pltpu.VMEM(blk, f32)],
        compiler_params=_cp("arbitrary", "arbitrary"),
    )(*args)
    return tuple(o.reshape(S, ATTN_W) for o in outs)


SLAB = ATTN_BLK
RES = 16


def _block_mask(first, dil4):
    qi = lax.broadcasted_iota(jnp.int32, (ATTN_BLK, 2 * ATTN_BLK), 0)
    kj = lax.broadcasted_iota(jnp.int32, (ATTN_BLK, 2 * ATTN_BLK), 1)
    cur = kj >= ATTN_BLK
    if dil4:
        jj = kj % ATTN_BLK
        pq = 4 * (qi % 32) + qi // 32
        pk = 4 * (jj % 32) + jj // 32 + jnp.where(cur, 0, -ATTN_BLK)
        delta = pq - pk
    else:
        delta = ATTN_BLK + qi - kj
    return (delta >= 0) & (delta <= ATTN_BLK) & (jnp.logical_not(first) | cur)


def _pair_fwd(q2, k2, v2, valid):
    even, masks = _head_masks()
    q2 = q2.astype(f32)
    outs, lses = [], []
    for e in range(2):
        qm = jnp.where(masks[e], q2, 0.0).astype(bf16)
        s = jnp.where(valid, _dot_nt(qm, k2), NEG)
        m = jnp.max(s, axis=-1, keepdims=True)
        pe = jnp.exp(s - m)
        lsum = jnp.sum(pe, axis=-1, keepdims=True)
        outs.append(_dot(pe.astype(bf16), v2) / lsum)
        lses.append(jnp.broadcast_to(m + jnp.log(lsum), (ATTN_BLK, 128)))
    return jnp.where(even, outs[0], outs[1]), jnp.where(even, lses[0], lses[1])


def _merge(y0, l0, y1, l1):
    mx = jnp.maximum(l0, l1)
    a, b = jnp.exp(l0 - mx), jnp.exp(l1 - mx)
    tot = a + b
    return (a * y0 + b * y1) / tot, mx + jnp.log(tot)


def _pair_bwd(q2, k2, v2, dy2, lse2, delta2, valid):
    _, masks = _head_masks()
    q2 = q2.astype(f32)
    k2f = k2.astype(f32)
    dq2 = jnp.zeros((ATTN_BLK, 128), f32)
    dk2 = jnp.zeros((2 * ATTN_BLK, 128), f32)
    dv2 = jnp.zeros((2 * ATTN_BLK, 128), f32)
    for e in range(2):
        c0 = e * HEAD_DIM
        qm = jnp.where(masks[e], q2, 0.0).astype(bf16)
        km = jnp.where(masks[e][:1, :], k2f, 0.0).astype(bf16)
        dym = jnp.where(masks[e], dy2, 0.0).astype(bf16)
        pe = jnp.where(valid, jnp.exp(_dot_nt(qm, k2) - lse2[:, c0:c0 + 1]), 0.0)
        ds = (pe * (_dot_nt(dym, v2) - delta2[:, c0:c0 + 1])).astype(bf16)
        dv2 = dv2 + _dot_tn(pe.astype(bf16), dym)
        dq2 = dq2 + _dot(ds, km)
        dk2 = dk2 + _dot_tn(ds, qm)
    return dq2, dk2, dv2


def _g4(ref, m0, r4):
    return jnp.concatenate([ref[pl.ds(m0, 32), r4 + 4 * a, :] for a in range(4)], axis=0)


def _s4(ref, m0, r4, val, add):
    for a in range(4):
        piece = val[32 * a:32 * a + 32]
        if add:
            ref[pl.ds(m0, 32), r4 + 4 * a, :] += piece
        else:
            ref[pl.ds(m0, 32), r4 + 4 * a, :] = piece


def _attention_fwd(qkv):
    S = qkv.shape[0]
    M = S // RES
    nS = M // SLAB

    def body(q_ref, kp_ref, kc_ref, vp_ref, vc_ref, y_ref, l_ref, k2, v2, ay, al):
        n = pl.program_id(1)
        k2[0:SLAB] = kp_ref[...]
        k2[SLAB:2 * SLAB] = kc_ref[...]
        v2[0:SLAB] = vp_ref[...]
        v2[SLAB:2 * SLAB] = vc_ref[...]

        valid16 = _block_mask(n == 0, False)
        for r in range(RES):
            ay[:, r, :], al[:, r, :] = _pair_fwd(q_ref[:, r, :], k2[:, r, :], v2[:, r, :], valid16)

        def blk4(b, carry):
            m0 = pl.multiple_of(32 * b, 32)
            valid = _block_mask((n == 0) & (b == 0), True)
            for r4 in range(4):
                kk = jnp.concatenate([_g4(k2, SLAB + m0 - 32, r4), _g4(k2, SLAB + m0, r4)], axis=0)
                vv = jnp.concatenate([_g4(v2, SLAB + m0 - 32, r4), _g4(v2, SLAB + m0, r4)], axis=0)
                out, lse = _pair_fwd(_g4(q_ref, m0, r4), kk, vv, valid)
                out, lse = _merge(_g4(ay, m0, r4), _g4(al, m0, r4), out, lse)
                _s4(ay, m0, r4, out, False)
                _s4(al, m0, r4, lse, False)
            return carry

        lax.fori_loop(0, SLAB // 32, blk4, 0)

        def blk1(b, carry):
            m0 = pl.multiple_of(8 * b, 8)
            valid = _block_mask((n == 0) & (b == 0), False)
            flat = lambda a: a.reshape(a.shape[0] * RES, 128)
            out, lse = _pair_fwd(flat(q_ref[pl.ds(m0, 8), :, :]), flat(k2[pl.ds(SLAB + m0 - 8, 16), :, :]),
                                 flat(v2[pl.ds(SLAB + m0 - 8, 16), :, :]), valid)
            out, lse = _merge(flat(ay[pl.ds(m0, 8), :, :]), flat(al[pl.ds(m0, 8), :, :]), out, lse)
            y_ref[pl.ds(m0, 8), :, :] = out.reshape(8, RES, 128)
            l_ref[pl.ds(m0, 8), :, :] = lse.reshape(8, RES, 128)
            return carry

        lax.fori_loop(0, SLAB // 8, blk1, 0)

    blk = (SLAB, RES, 128)
    x3 = qkv.reshape(M, RES, 3 * ATTN_W)
    cur = lambda c: pl.BlockSpec(blk, lambda p, n: (n, 0, 4 * c + p))
    prv = lambda c: pl.BlockSpec(blk, lambda p, n: (jnp.maximum(n - 1, 0), 0, 4 * c + p))
    out = pl.BlockSpec(blk, lambda p, n: (n, 0, p))
    y, lse = pl.pallas_call(
        body, name="attention_fwd", grid=(ATTN_W // 128, nS),
        in_specs=[cur(0), prv(1), cur(1), prv(2), cur(2)], out_specs=[out, out],
        out_shape=[jax.ShapeDtypeStruct((M, RES, ATTN_W), f32)] * 2,
        scratch_shapes=[pltpu.VMEM((2 * SLAB, RES, 128), bf16), pltpu.VMEM((2 * SLAB, RES, 128), bf16),
                        pltpu.VMEM(blk, f32), pltpu.VMEM(blk, f32)],
        compiler_params=_cp("arbitrary", "arbitrary"),
    )(x3, x3, x3, x3, x3)
    return y.reshape(S, ATTN_W), lse.reshape(S, ATTN_W)


def _attention_bwd(qkv, ya, lse, dmix):
    S = qkv.shape[0]
    M = S // RES
    nS = M // SLAB

    def body(q_ref, kp_ref, kc_ref, vp_ref, vc_ref, y_ref, l_ref, dy_ref, dq_ref, dk_ref, dv_ref, k2, v2, dk2, dv2, dqa, dl):
        n = pl.program_id(1)

        @pl.when(n == 0)
        def _():
            dk2[...] = jnp.zeros_like(dk2)
            dv2[...] = jnp.zeros_like(dv2)

        @pl.when(n < nS)
        def _():
            k2[0:SLAB] = kp_ref[...]
            k2[SLAB:2 * SLAB] = kc_ref[...]
            v2[0:SLAB] = vp_ref[...]
            v2[SLAB:2 * SLAB] = vc_ref[...]
            li = lax.broadcasted_iota(jnp.int32, (128, 128), 0)
            lj = lax.broadcasted_iota(jnp.int32, (128, 128), 1)
            seg = jnp.where((li // HEAD_DIM) == (lj // HEAD_DIM), 1.0, 0.0).astype(bf16)
            valid16 = _block_mask(n == 0, False)
            for r in range(RES):
                dy2 = dy_ref[:, r, :]
                dyy = dy2 * y_ref[:, r, :]
                hi = dyy.astype(bf16)
                delta2 = _dot(hi, seg) + _dot((dyy - hi.astype(f32)).astype(bf16), seg)
                dl[:, r, :] = delta2
                dq2, dkk, dvv = _pair_bwd(q_ref[:, r, :], k2[:, r, :], v2[:, r, :], dy2, l_ref[:, r, :], delta2, valid16)
                dqa[:, r, :] = dq2
                dk2[:, r, :] += dkk
                dv2[:, r, :] += dvv

            def blk4(b, carry):
                m0 = pl.multiple_of(32 * b, 32)
                valid = _block_mask((n == 0) & (b == 0), True)
                for r4 in range(4):
                    kk = jnp.concatenate([_g4(k2, SLAB + m0 - 32, r4), _g4(k2, SLAB + m0, r4)], axis=0)
                    vv = jnp.concatenate([_g4(v2, SLAB + m0 - 32, r4), _g4(v2, SLAB + m0, r4)], axis=0)
                    dq2, dkk, dvv = _pair_bwd(_g4(q_ref, m0, r4), kk, vv, _g4(dy_ref, m0, r4), _g4(l_ref, m0, r4),
                                              _g4(dl, m0, r4), valid)
                    _s4(dqa, m0, r4, dq2, True)
                    _s4(dk2, SLAB + m0 - 32, r4, dkk[:ATTN_BLK], True)
                    _s4(dk2, SLAB + m0, r4, dkk[ATTN_BLK:], True)
                    _s4(dv2, SLAB + m0 - 32, r4, dvv[:ATTN_BLK], True)
                    _s4(dv2, SLAB + m0, r4, dvv[ATTN_BLK:], True)
                return carry

            lax.fori_loop(0, SLAB // 32, blk4, 0)

            def blk1(b, carry):
                m0 = pl.multiple_of(8 * b, 8)
                valid = _block_mask((n == 0) & (b == 0), False)
                flat = lambda a: a.reshape(a.shape[0] * RES, 128)
                dq2, dkk, dvv = _pair_bwd(flat(q_ref[pl.ds(m0, 8), :, :]), flat(k2[pl.ds(SLAB + m0 - 8, 16), :, :]),
                                          flat(v2[pl.ds(SLAB + m0 - 8, 16), :, :]), flat(dy_ref[pl.ds(m0, 8), :, :]),
                                          flat(l_ref[pl.ds(m0, 8), :, :]), flat(dl[pl.ds(m0, 8), :, :]), valid)
                dq_ref[pl.ds(m0, 8), :, :] = dqa[pl.ds(m0, 8), :, :] + dq2.reshape(8, RES, 128)
                dk2[pl.ds(SLAB + m0 - 8, 16), :, :] += dkk.reshape(16, RES, 128)
                dv2[pl.ds(SLAB + m0 - 8, 16), :, :] += dvv.reshape(16, RES, 128)
                return carry

            lax.fori_loop(0, SLAB // 8, blk1, 0)

        dk_ref[...] = dk2[0:SLAB]
        dv_ref[...] = dv2[0:SLAB]
        dk2[0:SLAB] = dk2[SLAB:2 * SLAB]
        dv2[0:SLAB] = dv2[SLAB:2 * SLAB]
        dk2[SLAB:2 * SLAB] = jnp.zeros((SLAB, RES, 128), f32)
        dv2[SLAB:2 * SLAB] = jnp.zeros((SLAB, RES, 128), f32)

    blk = (SLAB, RES, 128)
    x3 = qkv.reshape(M, RES, 3 * ATTN_W)
    cn = lambda n: jnp.minimum(n, nS - 1)
    pn = lambda n: jnp.clip(n - 1, 0, nS - 1)
    cur = lambda c: pl.BlockSpec(blk, lambda p, n: (cn(n), 0, 4 * c + p))
    prv = lambda c: pl.BlockSpec(blk, lambda p, n: (pn(n), 0, 4 * c + p))
    at_n = pl.BlockSpec(blk, lambda p, n: (cn(n), 0, p))
    at_p = pl.BlockSpec(blk, lambda p, n: (pn(n), 0, p))
    v3 = lambda a: a.reshape(M, RES, a.shape[1])
    outs = pl.pallas_call(
        body, name="attention_bwd", grid=(ATTN_W // 128, nS + 1),
        in_specs=[cur(0), prv(1), cur(1), prv(2), cur(2), at_n, at_n, at_n], out_specs=[at_n, at_p, at_p],
        out_shape=[jax.ShapeDtypeStruct((M, RES, ATTN_W), f32)] * 3,
        scratch_shapes=[pltpu.VMEM((2 * SLAB, RES, 128), bf16), pltpu.VMEM((2 * SLAB, RES, 128), bf16),
                        pltpu.VMEM((2 * SLAB, RES, 128), f32), pltpu.VMEM((2 * SLAB, RES, 128), f32),
                        pltpu.VMEM(blk, f32), pltpu.VMEM(blk, f32)],
        compiler_params=_cp("arbitrary", "arbitrary"),
    )(x3, x3, x3, x3, x3, v3(ya), v3(lse), v3(dmix))
    return tuple(o.reshape(S, ATTN_W) for o in outs)


HG_T = 256


def _row_in_chunk():
    return lax.broadcasted_iota(jnp.int32, (HG_T, HGRN_HD), 0) % CHUNK


def _chunk_cumsum(v, rc):
    for k in (1, 2, 4, 8):
        v = v + jnp.where(rc >= k, pltpu.roll(v, k, 0), 0.0)
    return v


def _chunk_rcumsum(v, rc):
    for k in (1, 2, 4, 8):
        v = v + jnp.where(rc < CHUNK - k, pltpu.roll(v, HG_T - k, 0), 0.0)
    return v


def _hgrn_gates(qb, fb, lb):
    sf = _sigmoid(fb)
    f = lb + (1.0 - lb) * sf
    sq = _sigmoid(qb)
    return sf, f, jnp.log(f), 1.0 - f, sq, qb * sq


def _hgrn_specs(nT, rev):
    ti = (lambda i: nT - 1 - i) if rev else (lambda i: i)
    col = lambda c: pl.BlockSpec((HG_T, HGRN_HD), lambda h, i: (ti(i), 4 * c + h))
    vec = pl.BlockSpec((1, HGRN_HD), lambda h, i: (0, h))
    lbs = pl.BlockSpec((2, HGRN_HD), lambda h, i: (0, h))
    tile = pl.BlockSpec((HG_T, HGRN_HD), lambda h, i: (ti(i), h))
    st = pl.BlockSpec((HG_T // CHUNK, HGRN_HD, HGRN_HD), lambda h, i: (ti(i), h, 0))
    return col, vec, lbs, tile, st


def _old_hgrn_fwd(hp, lbl, wn):
    S = hp.shape[0]
    nT = S // HG_T

    def body(qb_ref, fb_ref, ib_ref, gb_ref, lbl_ref, wn_ref, yb_ref, o_ref, st_ref, ST, qt_s, kh_s, dec_s, oi_s):
        @pl.when(pl.program_id(1) == 0)
        def _():
            ST[...] = jnp.zeros_like(ST)

        rc = _row_in_chunk()
        lb = _sigmoid(lbl_ref[0:1, :] - lbl_ref[1:2, :])
        _, _, lf, key, _, qf = _hgrn_gates(qb_ref[...], fb_ref[...], lb)
        v = ib_ref[...]
        b = _chunk_cumsum(lf, rc)
        rem = _chunk_rcumsum(lf, rc) - lf
        qt_s[...] = (qf * jnp.exp(b)).astype(bf16)
        kh_s[...] = (key * jnp.exp(rem)).astype(bf16)
        dec_s[...] = jnp.exp(b + rem)

        def step(c, carry):
            rows = pl.ds(pl.multiple_of(c * CHUNK, CHUNK), CHUNK)
            stv = ST[...]
            st_ref[c] = stv
            oi_s[rows, :] = _dot_nt(qt_s[rows, :], stv.astype(bf16))
            dec = dec_s[pl.ds(pl.multiple_of(c * CHUNK, CHUNK), 1), :]
            ST[...] = stv * dec + _dot_tn(ib_ref[rows, :].astype(bf16), kh_s[rows, :])
            return carry

        lax.fori_loop(0, HG_T // CHUNK, step, 0)

        ones = jnp.ones((HGRN_HD, HGRN_HD), bf16)
        o = oi_s[...]
        for l in range(CHUNK):
            if l == 0:
                pr, vs = qf * key, v
            else:
                e = jnp.exp(jnp.where(rc >= l, b - pltpu.roll(b, l, 0), NEG))
                pr, vs = qf * pltpu.roll(key, l, 0) * e, pltpu.roll(v, l, 0)
            o = o + _dot(pr.astype(bf16), ones) * vs
        o_ref[...] = o
        on = o * lax.rsqrt(jnp.mean(o * o, axis=-1, keepdims=True) + EPS)
        g = gb_ref[...]
        yb_ref[...] = on * wn_ref[...] * (g * _sigmoid(g))

    col, vec, lbs, tile, st = _hgrn_specs(nT, False)
    return pl.pallas_call(
        body, name="hgrn_fwd", grid=(HGRN_W // HGRN_HD, nT),
        in_specs=[col(0), col(1), col(2), col(3), lbs, vec],
        out_specs=[tile, tile, st],
        out_shape=[jax.ShapeDtypeStruct((S, HGRN_W), f32), jax.ShapeDtypeStruct((S, HGRN_W), f32),
                   jax.ShapeDtypeStruct((S // CHUNK, HGRN_W, HGRN_HD), f32)],
        scratch_shapes=[pltpu.VMEM((HGRN_HD, HGRN_HD), f32), pltpu.VMEM((HG_T, HGRN_HD), bf16),
                        pltpu.VMEM((HG_T, HGRN_HD), bf16), pltpu.VMEM((HG_T, HGRN_HD), f32),
                        pltpu.VMEM((HG_T, HGRN_HD), f32)],
        compiler_params=_cp("arbitrary", "arbitrary"),
    )(hp, hp, hp, hp, lbl, wn)


def _old_hgrn_bwd(hp, lbl, wn, o_sav, states, dmix):
    S = hp.shape[0]
    nT = S // HG_T

    def body(qb_ref, fb_ref, ib_ref, gb_ref, lbl_ref, wn_ref, o_ref, st_ref, dy_ref,
             dq_ref, df_ref, di_ref, dg_ref, gwn_ref, glb_ref,
             DST, qt_s, kh_s, dec_s, do_s, dqt_s, dkh_s, dvi_s, dbl_s):
        @pl.when(pl.program_id(1) == 0)
        def _():
            DST[...] = jnp.zeros_like(DST)
            gwn_ref[...] = jnp.zeros_like(gwn_ref)
            glb_ref[...] = jnp.zeros_like(glb_ref)

        rc = _row_in_chunk()
        lb = _sigmoid(lbl_ref[0:1, :] - lbl_ref[1:2, :])
        qb = qb_ref[...]
        sf, f, lf, key, sq, qf = _hgrn_gates(qb, fb_ref[...], lb)
        v = ib_ref[...]
        o = o_ref[...]
        rinv = lax.rsqrt(jnp.mean(o * o, axis=-1, keepdims=True) + EPS)
        on = o * rinv
        g = gb_ref[...]
        sgm = _sigmoid(g)
        silu_g = g * sgm
        dy = dy_ref[...]
        wn_v = wn_ref[...]
        gwn_ref[...] += jnp.sum(dy * on * silu_g, axis=0, keepdims=True)
        dg_ref[...] = (dy * on * wn_v * (sgm * (1.0 + g * (1.0 - sgm)))).astype(bf16)
        t1 = dy * wn_v * silu_g
        do = rinv * (t1 - on * jnp.mean(t1 * on, axis=-1, keepdims=True))
        do_s[...] = do.astype(bf16)

        b = _chunk_cumsum(lf, rc)
        rem = _chunk_rcumsum(lf, rc) - lf
        eb, er = jnp.exp(b), jnp.exp(rem)
        qt, kh = qf * eb, key * er
        qt_s[...] = qt.astype(bf16)
        kh_s[...] = kh.astype(bf16)
        dec_s[...] = jnp.exp(b + rem)

        def step(k, carry):
            c = HG_T // CHUNK - 1 - k
            rows = pl.ds(pl.multiple_of(c * CHUNK, CHUNK), CHUNK)
            stp = st_ref[c]
            dst = DST[...]
            dstb = dst.astype(bf16)
            dob = do_s[rows, :]
            khb = kh_s[rows, :]
            dec = dec_s[pl.ds(pl.multiple_of(c * CHUNK, CHUNK), 1), :]
            dqt_s[rows, :] = _dot(dob, stp.astype(bf16))
            dkh = _dot(ib_ref[rows, :].astype(bf16), dstb)
            dkh_s[rows, :] = dkh
            dvi_s[rows, :] = _dot_nt(khb, dstb)
            dbl = jnp.sum(dst * stp, axis=0, keepdims=True) * dec + jnp.sum(dkh * khb.astype(f32), axis=0, keepdims=True)
            dbl_s[rows, :] = jnp.broadcast_to(dbl, (CHUNK, HGRN_HD))
            DST[...] = dst * dec + _dot_tn(dob, qt_s[rows, :])
            return carry

        lax.fori_loop(0, HG_T // CHUNK, step, 0)

        dqt, dkh = dqt_s[...], dkh_s[...]
        dqf = dqt * eb
        dkey = dkh * er
        db = dqt * qt - dkh * kh + jnp.where(rc == CHUNK - 1, dbl_s[...], 0.0)
        dv = dvi_s[...]
        ones = jnp.ones((HGRN_HD, HGRN_HD), bf16)
        for l in range(CHUNK):
            if l == 0:
                e, ks, vs = None, key, v
                qe = qf
            else:
                e = jnp.exp(jnp.where(rc >= l, b - pltpu.roll(b, l, 0), NEG))
                ks, vs = pltpu.roll(key, l, 0), pltpu.roll(v, l, 0)
                qe = qf * e
            pr = qe * ks
            rl = _dot(pr.astype(bf16), ones)
            drl = _dot((do * vs).astype(bf16), ones)
            if l > 0:
                drl = jnp.where(rc >= l, drl, 0.0)
            gl = drl * pr
            dqf = dqf + drl * ks * (e if l > 0 else 1.0)
            if l == 0:
                dv = dv + rl * do
                dkey = dkey + drl * qe
            else:
                dv = dv + pltpu.roll(rl * do, HG_T - l, 0)
                dkey = dkey + pltpu.roll(drl * qe, HG_T - l, 0)
                db = db + gl - pltpu.roll(gl, HG_T - l, 0)
        dlf = _chunk_rcumsum(db, rc)
        df = dlf / f - dkey
        df_ref[...] = (df * (1.0 - lb) * sf * (1.0 - sf)).astype(bf16)
        glb_ref[...] += jnp.sum(df * (1.0 - sf), axis=0, keepdims=True)
        dq_ref[...] = (dqf * (sq * (1.0 + qb * (1.0 - sq)))).astype(bf16)
        di_ref[...] = dv.astype(bf16)

    col, vec, lbs, tile, st = _hgrn_specs(nT, True)
    dyspec = pl.BlockSpec((HG_T, HGRN_HD), lambda h, i: (nT - 1 - i, 4 + h))
    tb = lambda: pltpu.VMEM((HG_T, HGRN_HD), bf16)
    tf = lambda: pltpu.VMEM((HG_T, HGRN_HD), f32)
    dq, df, di, dg, gwn, glb = pl.pallas_call(
        body, name="hgrn_bwd", grid=(HGRN_W // HGRN_HD, nT),
        in_specs=[col(0), col(1), col(2), col(3), lbs, vec, tile, st, dyspec],
        out_specs=[tile, tile, tile, tile, vec, vec],
        out_shape=[jax.ShapeDtypeStruct((S, HGRN_W), bf16)] * 4 + [jax.ShapeDtypeStruct((1, HGRN_W), f32)] * 2,
        scratch_shapes=[pltpu.VMEM((HGRN_HD, HGRN_HD), f32), tb(), tb(), tf(), tb(), tf(), tf(), tf(), tf()],
        compiler_params=_cp("arbitrary", "arbitrary"),
    )(hp, hp, hp, hp, lbl, wn, o_sav, states, dmix)
    return dq, df, di, dg, gwn, glb


N_HH = HGRN_W // HGRN_HD
HG_SUB = 128
SAFE_RANGE = 80.0


def _hgrn_prep(qb, fb, lbl2, rc):
    lb = _sigmoid(lbl2[0:1, :] - lbl2[1:2, :])
    sf, f, lf, key, sq, qf = _hgrn_gates(qb, fb, lb)
    b = _chunk_cumsum(lf, rc)
    rem = _chunk_rcumsum(lf, rc) - lf
    return dict(lb=lb, sf=sf, f=f, key=key, sq=sq, qf=qf, b=b, rem=rem, eb=jnp.exp(b), er=jnp.exp(rem))


def _chunk_mask():
    r = lax.broadcasted_iota(jnp.int32, (HG_SUB, HG_SUB), 0)
    c = lax.broadcasted_iota(jnp.int32, (HG_SUB, HG_SUB), 1)
    return ((r // CHUNK) == (c // CHUNK)) & (c <= r)


def _hgrn_fwd(hp, lbl, wn):
    S = hp.shape[0]
    nT = S // HG_T

    def body(qb_ref, fb_ref, ib_ref, gb_ref, lbl_ref, wn_ref, yb_ref, o_ref, st_ref, ST, qt_s, kh_s, dec_s, oi_s):
        @pl.when(pl.program_id(0) == 0)
        def _():
            ST[...] = jnp.zeros_like(ST)

        rc = _row_in_chunk()
        for h in range(N_HH):
            sl = slice(HGRN_HD * h, HGRN_HD * (h + 1))
            p = _hgrn_prep(qb_ref[:, sl], fb_ref[:, sl], lbl_ref[:, sl], rc)
            qf, key, b = p["qf"], p["key"], p["b"]
            qt = qf * p["eb"]
            qt_s[:, sl] = qt.astype(bf16)
            kh_s[:, sl] = (key * p["er"]).astype(bf16)
            dec_s[:, sl] = jnp.exp(b + p["rem"])
            rng = jnp.max(-(b + p["rem"]))

            @pl.when(rng < SAFE_RANGE)
            def _():
                kp = (key * jnp.exp(-b)).astype(bf16)
                cmask = _chunk_mask()
                for j in range(HG_T // HG_SUB):
                    rs = slice(HG_SUB * j, HG_SUB * (j + 1))
                    sc = jnp.where(cmask, _dot_nt(qt[rs].astype(bf16), kp[rs]), 0.0).astype(bf16)
                    oi_s[rs, sl] = _dot(sc, ib_ref[rs, sl].astype(bf16))

            @pl.when(rng >= SAFE_RANGE)
            def _():
                v = ib_ref[:, sl]
                ones = jnp.ones((HGRN_HD, HGRN_HD), bf16)
                o = jnp.zeros((HG_T, HGRN_HD), f32)
                for l in range(CHUNK):
                    if l == 0:
                        pr, vs = qf * key, v
                    else:
                        e = jnp.exp(jnp.where(rc >= l, b - pltpu.roll(b, l, 0), NEG))
                        pr, vs = qf * pltpu.roll(key, l, 0) * e, pltpu.roll(v, l, 0)
                    o = o + _dot(pr.astype(bf16), ones) * vs
                oi_s[:, sl] = o

        def step(c, carry):
            rows = pl.ds(pl.multiple_of(c * CHUNK, CHUNK), CHUNK)
            row0 = pl.ds(pl.multiple_of(c * CHUNK, CHUNK), 1)
            for h in range(N_HH):
                sl = slice(HGRN_HD * h, HGRN_HD * (h + 1))
                stv = ST[h]
                st_ref[c, sl, :] = stv
                oi_s[rows, sl] += _dot_nt(qt_s[rows, sl], stv.astype(bf16))
                ST[h] = stv * dec_s[row0, sl] + _dot_tn(ib_ref[rows, sl].astype(bf16), kh_s[rows, sl])
            return carry

        lax.fori_loop(0, HG_T // CHUNK, step, 0)

        for h in range(N_HH):
            sl = slice(HGRN_HD * h, HGRN_HD * (h + 1))
            o = oi_s[:, sl]
            o_ref[:, sl] = o
            on = o * lax.rsqrt(jnp.mean(o * o, axis=-1, keepdims=True) + EPS)
            g = gb_ref[:, sl]
            yb_ref[:, sl] = on * wn_ref[:, sl] * (g * _sigmoid(g))

    col = lambda c: pl.BlockSpec((HG_T, HGRN_W), lambda i: (i, c))
    tile = pl.BlockSpec((HG_T, HGRN_W), lambda i: (i, 0))
    whole = lambda a: pl.BlockSpec(a.shape, lambda i: (0, 0))
    return pl.pallas_call(
        body, name="hgrn_fwd", grid=(nT,),
        in_specs=[col(0), col(1), col(2), col(3), whole(lbl), whole(wn)],
        out_specs=[tile, tile, pl.BlockSpec((HG_T // CHUNK, HGRN_W, HGRN_HD), lambda i: (i, 0, 0))],
        out_shape=[jax.ShapeDtypeStruct((S, HGRN_W), f32), jax.ShapeDtypeStruct((S, HGRN_W), f32),
                   jax.ShapeDtypeStruct((S // CHUNK, HGRN_W, HGRN_HD), f32)],
        scratch_shapes=[pltpu.VMEM((N_HH, HGRN_HD, HGRN_HD), f32), pltpu.VMEM((HG_T, HGRN_W), bf16),
                        pltpu.VMEM((HG_T, HGRN_W), bf16), pltpu.VMEM((HG_T, HGRN_W), f32), pltpu.VMEM((HG_T, HGRN_W), f32)],
        compiler_params=_cp("arbitrary"),
    )(hp, hp, hp, hp, lbl, wn)


def _hgrn_bwd(hp, lbl, wn, o_sav, states, dmix):
    S = hp.shape[0]
    nT = S // HG_T

    def body(qb_ref, fb_ref, ib_ref, gb_ref, lbl_ref, wn_ref, o_ref, st_ref, dy_ref,
             dq_ref, df_ref, di_ref, dg_ref, gwn_ref, glb_ref,
             DST, qt_s, kh_s, dec_s, do_s, dqt_s, dkh_s, dbl_s, dvi_s, dqi_s, dki_s, dbi_s):
        @pl.when(pl.program_id(0) == 0)
        def _():
            DST[...] = jnp.zeros_like(DST)
            gwn_ref[...] = jnp.zeros_like(gwn_ref)
            glb_ref[...] = jnp.zeros_like(glb_ref)

        rc = _row_in_chunk()
        for h in range(N_HH):
            sl = slice(HGRN_HD * h, HGRN_HD * (h + 1))
            p = _hgrn_prep(qb_ref[:, sl], fb_ref[:, sl], lbl_ref[:, sl], rc)
            qf, key, b = p["qf"], p["key"], p["b"]
            v = ib_ref[:, sl]
            o = o_ref[:, sl]
            rinv = lax.rsqrt(jnp.mean(o * o, axis=-1, keepdims=True) + EPS)
            on = o * rinv
            g = gb_ref[:, sl]
            sgm = _sigmoid(g)
            silu_g = g * sgm
            dy = dy_ref[:, sl]
            wn_v = wn_ref[:, sl]
            gwn_ref[:, sl] += jnp.sum(dy * on * silu_g, axis=0, keepdims=True)
            dg_ref[:, sl] = (dy * on * wn_v * (sgm * (1.0 + g * (1.0 - sgm)))).astype(bf16)
            t1 = dy * wn_v * silu_g
            do = rinv * (t1 - on * jnp.mean(t1 * on, axis=-1, keepdims=True))
            do_s[:, sl] = do.astype(bf16)
            qt = qf * p["eb"]
            qt_s[:, sl] = qt.astype(bf16)
            kh_s[:, sl] = (key * p["er"]).astype(bf16)
            dec_s[:, sl] = jnp.exp(b + p["rem"])
            rng = jnp.max(-(b + p["rem"]))

            @pl.when(rng < SAFE_RANGE)
            def _():
                einv = jnp.exp(-b)
                kpf = key * einv
                kp = kpf.astype(bf16)
                cmask = _chunk_mask()
                for j in range(HG_T // HG_SUB):
                    rs = slice(HG_SUB * j, HG_SUB * (j + 1))
                    qtb, dob, vb = qt[rs].astype(bf16), do[rs].astype(bf16), v[rs].astype(bf16)
                    sc = jnp.where(cmask, _dot_nt(qtb, kp[rs]), 0.0).astype(bf16)
                    dsc = jnp.where(cmask, _dot_nt(dob, vb), 0.0).astype(bf16)
                    dqp = _dot(dsc, kp[rs])
                    dkp = _dot_tn(dsc, qtb)
                    dvi_s[rs, sl] = _dot_tn(sc, dob)
                    dqi_s[rs, sl] = dqp * p["eb"][rs]
                    dki_s[rs, sl] = dkp * einv[rs]
                    dbi_s[rs, sl] = dqp * qtb.astype(f32) - dkp * kp[rs].astype(f32)

            @pl.when(rng >= SAFE_RANGE)
            def _():
                ones = jnp.ones((HGRN_HD, HGRN_HD), bf16)
                dqf = jnp.zeros((HG_T, HGRN_HD), f32)
                dkey, db, dv = dqf, dqf, dqf
                for l in range(CHUNK):
                    if l == 0:
                        ks, vs, qe = key, v, qf
                    else:
                        e = jnp.exp(jnp.where(rc >= l, b - pltpu.roll(b, l, 0), NEG))
                        ks, vs, qe = pltpu.roll(key, l, 0), pltpu.roll(v, l, 0), qf * e
                    pr = qe * ks
                    rl = _dot(pr.astype(bf16), ones)
                    drl = _dot((do * vs).astype(bf16), ones)
                    if l == 0:
                        dqf = dqf + drl * ks
                        dv = dv + rl * do
                        dkey = dkey + drl * qe
                    else:
                        drl = jnp.where(rc >= l, drl, 0.0)
                        gl = drl * pr
                        dqf = dqf + drl * ks * e
                        dv = dv + pltpu.roll(rl * do, HG_T - l, 0)
                        dkey = dkey + pltpu.roll(drl * qe, HG_T - l, 0)
                        db = db + gl - pltpu.roll(gl, HG_T - l, 0)
                dvi_s[:, sl] = dv
                dqi_s[:, sl] = dqf
                dki_s[:, sl] = dkey
                dbi_s[:, sl] = db

        def step(k, carry):
            c = HG_T // CHUNK - 1 - k
            rows = pl.ds(pl.multiple_of(c * CHUNK, CHUNK), CHUNK)
            row0 = pl.ds(pl.multiple_of(c * CHUNK, CHUNK), 1)
            for h in range(N_HH):
                sl = slice(HGRN_HD * h, HGRN_HD * (h + 1))
                stp = st_ref[c, sl, :]
                dst = DST[h]
                dstb = dst.astype(bf16)
                dob = do_s[rows, sl]
                khb = kh_s[rows, sl]
                dec = dec_s[row0, sl]
                dqt_s[rows, sl] = _dot(dob, stp.astype(bf16))
                dkh = _dot(ib_ref[rows, sl].astype(bf16), dstb)
                dkh_s[rows, sl] = dkh
                dvi_s[rows, sl] += _dot_nt(khb, dstb)
                dbl = jnp.sum(dst * stp, axis=0, keepdims=True) * dec + jnp.sum(dkh * khb.astype(f32), axis=0, keepdims=True)
                dbl_s[rows, sl] = jnp.broadcast_to(dbl, (CHUNK, HGRN_HD))
                DST[h] = dst * dec + _dot_tn(dob, qt_s[rows, sl])
            return carry

        lax.fori_loop(0, HG_T // CHUNK, step, 0)

        for h in range(N_HH):
            sl = slice(HGRN_HD * h, HGRN_HD * (h + 1))
            qb = qb_ref[:, sl]
            p = _hgrn_prep(qb, fb_ref[:, sl], lbl_ref[:, sl], rc)
            sf, sq, lb = p["sf"], p["sq"], p["lb"]
            dqt, dkh = dqt_s[:, sl], dkh_s[:, sl]
            dqf = dqt * p["eb"] + dqi_s[:, sl]
            dkey = dkh * p["er"] + dki_s[:, sl]
            db = dqt * (p["qf"] * p["eb"]) - dkh * (p["key"] * p["er"]) + jnp.where(rc == CHUNK - 1, dbl_s[:, sl], 0.0) + dbi_s[:, sl]
            df = _chunk_rcumsum(db, rc) / p["f"] - dkey
            df_ref[:, sl] = (df * (1.0 - lb) * sf * (1.0 - sf)).astype(bf16)
            glb_ref[:, sl] += jnp.sum(df * (1.0 - sf), axis=0, keepdims=True)
            dq_ref[:, sl] = (dqf * (sq * (1.0 + qb * (1.0 - sq)))).astype(bf16)
            di_ref[:, sl] = dvi_s[:, sl].astype(bf16)

    rev = lambda i: nT - 1 - i
    col = lambda c: pl.BlockSpec((HG_T, HGRN_W), lambda i: (rev(i), c))
    tile = pl.BlockSpec((HG_T, HGRN_W), lambda i: (rev(i), 0))
    whole = lambda a: pl.BlockSpec(a.shape, lambda i: (0, 0))
    vec = pl.BlockSpec((1, HGRN_W), lambda i: (0, 0))
    tb = lambda: pltpu.VMEM((HG_T, HGRN_W), bf16)
    tf = lambda: pltpu.VMEM((HG_T, HGRN_W), f32)
    return pl.pallas_call(
        body, name="hgrn_bwd", grid=(nT,),
        in_specs=[col(0), col(1), col(2), col(3), whole(lbl), whole(wn), tile,
                  pl.BlockSpec((HG_T // CHUNK, HGRN_W, HGRN_HD), lambda i: (rev(i), 0, 0)),
                  pl.BlockSpec((HG_T, HGRN_W), lambda i: (rev(i), 1))],
        out_specs=[tile, tile, tile, tile, vec, vec],
        out_shape=[jax.ShapeDtypeStruct((S, HGRN_W), bf16)] * 4 + [jax.ShapeDtypeStruct((1, HGRN_W), f32)] * 2,
        scratch_shapes=[pltpu.VMEM((N_HH, HGRN_HD, HGRN_HD), f32), tb(), tb(), tf(), tb(), tf(), tf(), tf(), tf(), tf(), tf(), tf()],
        compiler_params=_cp("arbitrary"),
    )(hp, hp, hp, hp, lbl, wn, o_sav, states, dmix)


def _out_proj(x, ya, yb, wout, w2):
    S = x.shape[0]
    tm = 512

    def body(x_ref, ya_ref, yb_ref, w_ref, w2_ref, h1_ref, u2_ref, mix_ref):
        mixed = jnp.concatenate([ya_ref[...], yb_ref[...]], axis=1).astype(bf16)
        mix_ref[...] = mixed
        h1 = x_ref[...] + _dot(mixed, w_ref[...])
        h1_ref[...] = h1
        r = lax.rsqrt(jnp.mean(h1 * h1, axis=-1, keepdims=True) + EPS)
        u2_ref[...] = (h1 * r * w2_ref[...]).astype(bf16)

    row = lambda w: pl.BlockSpec((tm, w), lambda i: (i, 0))
    return pl.pallas_call(
        body, name="out_proj", grid=(S // tm,),
        in_specs=[row(D_MODEL), row(ATTN_W), row(HGRN_W), pl.BlockSpec((D_MODEL, D_MODEL), lambda i: (0, 0)),
                  pl.BlockSpec((1, D_MODEL), lambda i: (0, 0))],
        out_specs=[row(D_MODEL), row(D_MODEL), row(D_MODEL)],
        out_shape=[jax.ShapeDtypeStruct((S, D_MODEL), f32), jax.ShapeDtypeStruct((S, D_MODEL), bf16),
                   jax.ShapeDtypeStruct((S, D_MODEL), bf16)],
        compiler_params=_cp("arbitrary"),
    )(x, ya, yb, wout, w2)


def _gate_up(u2, wgu):
    S = u2.shape[0]
    tm, tn = 512, 1408
    nj = FFN // tn

    def body(u_ref, wg_ref, wu_ref, g_ref, up_ref, a_ref):
        u = u_ref[...]
        g = _dot(u, wg_ref[...])
        up = _dot(u, wu_ref[...])
        g_ref[...] = g.astype(bf16)
        up_ref[...] = up.astype(bf16)
        a_ref[...] = (g * _sigmoid(g) * up).astype(bf16)

    out = pl.BlockSpec((tm, tn), lambda j, i: (i, j))
    return pl.pallas_call(
        body, name="gate_up", grid=(nj, S // tm),
        in_specs=[pl.BlockSpec((tm, D_MODEL), lambda j, i: (i, 0)), pl.BlockSpec((D_MODEL, tn), lambda j, i: (0, j)),
                  pl.BlockSpec((D_MODEL, tn), lambda j, i: (0, j + nj))],
        out_specs=[out, out, out],
        out_shape=[jax.ShapeDtypeStruct((S, FFN), bf16)] * 3,
        compiler_params=_cp("arbitrary", "arbitrary"),
    )(u2, wgu, wgu)


def _rms_bwd(dyw, hn, r):
    return r * (dyw - hn * jnp.mean(dyw * hn, axis=-1, keepdims=True))


def _down_loss(act, wdown, h1, tgt, w3):
    S = act.shape[0]
    tm = 256

    def body(a_ref, w_ref, h1_ref, t_ref, w3_ref, dh2_ref, loss_ref, gw3_ref):
        @pl.when(pl.program_id(0) == 0)
        def _():
            loss_ref[...] = jnp.zeros_like(loss_ref)
            gw3_ref[...] = jnp.zeros_like(gw3_ref)

        h2 = h1_ref[...] + _dot(a_ref[...], w_ref[...])
        r = lax.rsqrt(jnp.mean(h2 * h2, axis=-1, keepdims=True) + EPS)
        hn = h2 * r
        w3 = w3_ref[...]
        err = hn * w3 - t_ref[...]
        loss_ref[...] += (0.5 / D_MODEL) * jnp.sum(err * err)
        dy = err * (1.0 / D_MODEL)
        gw3_ref[...] += jnp.sum(dy * hn, axis=0, keepdims=True)
        dh2_ref[...] = _rms_bwd(dy * w3, hn, r)

    row = lambda w: pl.BlockSpec((tm, w), lambda i: (i, 0))
    return pl.pallas_call(
        body, name="down_loss", grid=(S // tm,),
        in_specs=[row(FFN), pl.BlockSpec((FFN, D_MODEL), lambda i: (0, 0)), row(D_MODEL), row(D_MODEL),
                  pl.BlockSpec((1, D_MODEL), lambda i: (0, 0))],
        out_specs=[row(D_MODEL), pl.BlockSpec((1, 128), lambda i: (0, 0)), pl.BlockSpec((1, D_MODEL), lambda i: (0, 0))],
        out_shape=[jax.ShapeDtypeStruct((S, D_MODEL), f32), jax.ShapeDtypeStruct((1, 128), f32),
                   jax.ShapeDtypeStruct((1, D_MODEL), f32)],
        compiler_params=_cp("arbitrary"),
    )(act, wdown, h1, tgt, w3)


def _dact(dh2, wdown, gate, up):
    S = dh2.shape[0]
    tm = 256

    def body(d_ref, w_ref, g_ref, u_ref, dg_ref, du_ref):
        da = _dot_nt(d_ref[...].astype(bf16), w_ref[...])
        g = g_ref[...].astype(f32)
        sg = _sigmoid(g)
        du_ref[...] = (da * g * sg).astype(bf16)
        dg_ref[...] = (da * u_ref[...].astype(f32) * (sg * (1.0 + g * (1.0 - sg)))).astype(bf16)

    row = lambda w: pl.BlockSpec((tm, w), lambda i: (i, 0))
    return pl.pallas_call(
        body, name="dact", grid=(S // tm,),
        in_specs=[row(D_MODEL), pl.BlockSpec((FFN, D_MODEL), lambda i: (0, 0)), row(FFN), row(FFN)],
        out_specs=[row(FFN), row(FFN)],
        out_shape=[jax.ShapeDtypeStruct((S, FFN), bf16)] * 2,
        compiler_params=_cp("arbitrary"),
    )(dh2, wdown, gate, up)


def _dgu(dgate, dup, wgu, h1, w2, dh2):
    S = dgate.shape[0]
    tm = 256

    def body(dg_ref, du_ref, wg_ref, wu_ref, h1_ref, w2_ref, dh2_ref, dh1_ref, gw2_ref):
        @pl.when(pl.program_id(0) == 0)
        def _():
            gw2_ref[...] = jnp.zeros_like(gw2_ref)

        du2 = _dot_nt(dg_ref[...], wg_ref[...]) + _dot_nt(du_ref[...], wu_ref[...])
        h1 = h1_ref[...]
        r = lax.rsqrt(jnp.mean(h1 * h1, axis=-1, keepdims=True) + EPS)
        hn = h1 * r
        gw2_ref[...] += jnp.sum(du2 * hn, axis=0, keepdims=True)
        dh1_ref[...] = dh2_ref[...] + _rms_bwd(du2 * w2_ref[...], hn, r)

    row = lambda w: pl.BlockSpec((tm, w), lambda i: (i, 0))
    return pl.pallas_call(
        body, name="dgu", grid=(S // tm,),
        in_specs=[row(FFN), row(FFN), pl.BlockSpec((D_MODEL, FFN), lambda i: (0, 0)),
                  pl.BlockSpec((D_MODEL, FFN), lambda i: (0, 1)), row(D_MODEL),
                  pl.BlockSpec((1, D_MODEL), lambda i: (0, 0)), row(D_MODEL)],
        out_specs=[row(D_MODEL), pl.BlockSpec((1, D_MODEL), lambda i: (0, 0))],
        out_shape=[jax.ShapeDtypeStruct((S, D_MODEL), f32), jax.ShapeDtypeStruct((1, D_MODEL), f32)],
        compiler_params=_cp("arbitrary"),
    )(dgate, dup, wgu, wgu, h1, w2, dh2)


def _dmixed(dh1, wout):
    S = dh1.shape[0]
    tm = 512

    def body(d_ref, w_ref, o_ref):
        o_ref[...] = _dot_nt(d_ref[...].astype(bf16), w_ref[...])

    row = pl.BlockSpec((tm, D_MODEL), lambda i: (i, 0))
    return pl.pallas_call(
        body, name="dmixed", grid=(S // tm,),
        in_specs=[row, pl.BlockSpec((D_MODEL, D_MODEL), lambda i: (0, 0))], out_specs=row,
        out_shape=jax.ShapeDtypeStruct((S, D_MODEL), f32), compiler_params=_cp("arbitrary"),
    )(dh1, wout)


def _din(dq, dk, dv, dhq, dhf, dhi, dhg, cos_t, sg_t, win, x, w1, dh1):
    S = x.shape[0]
    tm = 256

    def body(dq_ref, dk_ref, dv_ref, dhq_ref, dhf_ref, dhi_ref, dhg_ref, cos_ref, sg_ref, w_ref, x_ref, w1_ref, dh1_ref,
             dp_ref, gx_ref, gw1_ref):
        @pl.when(pl.program_id(0) == 0)
        def _():
            gw1_ref[...] = jnp.zeros_like(gw1_ref)

        cosv, sgv = cos_ref[...], sg_ref[...]
        unrope = lambda d: d * cosv - sgv * _swap_halves(d)
        parts = [(unrope(dq_ref[...]) * (HEAD_DIM ** -0.5)).astype(bf16), unrope(dk_ref[...]).astype(bf16),
                 dv_ref[...].astype(bf16), dhq_ref[...], dhf_ref[...], dhi_ref[...], dhg_ref[...]]
        du = jnp.zeros((tm, D_MODEL), f32)
        for j, pj in enumerate(parts):
            dp_ref[:, j * 512:(j + 1) * 512] = pj
            du = du + _dot_nt(pj, w_ref[:, j * 512:(j + 1) * 512])
        xv = x_ref[...]
        r = lax.rsqrt(jnp.mean(xv * xv, axis=-1, keepdims=True) + EPS)
        xn = xv * r
        gw1_ref[...] += jnp.sum(du * xn, axis=0, keepdims=True)
        gx_ref[...] = dh1_ref[...] + _rms_bwd(du * w1_ref[...], xn, r)

    row = lambda w: pl.BlockSpec((tm, w), lambda i: (i, 0))
    vec = pl.BlockSpec((1, D_MODEL), lambda i: (0, 0))
    return pl.pallas_call(
        body, name="din", grid=(S // tm,),
        in_specs=[row(512)] * 7 + [row(512), row(512), pl.BlockSpec((D_MODEL, IN_W), lambda i: (0, 0)), row(D_MODEL), vec,
                                   row(D_MODEL)],
        out_specs=[row(IN_W), row(D_MODEL), vec],
        out_shape=[jax.ShapeDtypeStruct((S, IN_W), bf16), jax.ShapeDtypeStruct((S, D_MODEL), f32),
                   jax.ShapeDtypeStruct((1, D_MODEL), f32)],
        compiler_params=_cp("arbitrary"),
    )(dq, dk, dv, dhq, dhf, dhi, dhg, cos_t, sg_t, win, x, w1, dh1)


def _gw(a, b, tn, name):
    S, M = a.shape
    N = b.shape[1]
    ts = 512

    def body(a_ref, b_ref, o_ref):
        @pl.when(pl.program_id(1) == 0)
        def _():
            o_ref[...] = jnp.zeros_like(o_ref)

        o_ref[...] += _dot_tn(a_ref[...].astype(bf16), b_ref[...].astype(bf16))

    return pl.pallas_call(
        body, name=name, grid=(N // tn, S // ts),
        in_specs=[pl.BlockSpec((ts, M), lambda j, s: (s, 0)), pl.BlockSpec((ts, tn), lambda j, s: (s, j))],
        out_specs=pl.BlockSpec((M, tn), lambda j, s: (0, j)), out_shape=jax.ShapeDtypeStruct((M, N), f32),
        compiler_params=_cp("arbitrary", "arbitrary"),
    )(a, b)


def _local_step(x, tgt, w1, win, lbl, wn, wout, w2, wgu, wdown, w3):
    S = x.shape[0]
    cos_t, sg_t = _rope_tables(S)
    u, qkv, hp = _in_proj(x, w1, win, cos_t, sg_t)
    ya, lse = _attention_fwd(qkv)
    yb, o_sav, states = _hgrn_fwd(hp, lbl, wn)
    h1, u2, mixed = _out_proj(x, ya, yb, wout, w2)
    gate, up, act = _gate_up(u2, wgu)
    dh2, loss, g_w3 = _down_loss(act, wdown, h1, tgt, w3)

    g_wdown = _gw(act, dh2, 512, "gw_down")
    dgate, dup = _dact(dh2, wdown, gate, up)
    g_wgu = (_gw(u2, dgate, 1408, "gw_gate"), _gw(u2, dup, 1408, "gw_up"))
    dh1, g_w2 = _dgu(dgate, dup, wgu, h1, w2, dh2)
    g_wout = _gw(mixed, dh1, 1024, "gw_out")
    dmix = _dmixed(dh1, wout)
    dhq, dhf, dhi, dhg, g_wn, g_lb = _hgrn_bwd(hp, lbl, wn, o_sav, states, dmix)
    datt = _attention_bwd(qkv, ya, lse, dmix)
    dproj, gx, g_w1 = _din(*datt, dhq, dhf, dhi, dhg, cos_t, sg_t, win, x, w1, dh1)
    g_win = _gw(u, dproj, 896, "gw_in")
    return loss, gx, (g_win, g_wout, g_wgu, g_wdown), (g_w1, g_lb, g_wn, g_w2, g_w3)


MESH = pl.DeviceIdType.MESH
ANY = pl.BlockSpec(memory_space=pl.ANY)
VMEM_SPEC = pl.BlockSpec(memory_space=pltpu.VMEM)


def _pos():
    return lax.axis_index("x"), lax.axis_index("y"), lax.axis_index("c")


def _flip(v, bit):
    return 1 - v if bit else v


def _all_gather(shards):
    n = len(shards)

    def body(*refs):
        ins, outs, bufs = refs[:n], refs[n:2 * n], refs[2 * n:3 * n]
        send_sems, recv_sems, local_sems = refs[3 * n:]
        x, y, c = _pos()
        me, sibling = (x, y, c), (x, y, 1 - c)
        chips = [(1 - x, y), (x, 1 - y), (1 - x, 1 - y)]

        def copy(a, k, block, to, src=None):
            dst = outs[a].at[4 * block[0] + 2 * block[1] + block[2]]
            return pltpu.make_async_remote_copy(src_ref=dst if src is None else src, dst_ref=dst, send_sem=send_sems.at[a, k],
                                                recv_sem=recv_sems.at[a, k], device_id=to, device_id_type=MESH)

        loads = [pltpu.make_async_copy(ins[a], bufs[a], local_sems.at[a]) for a in range(n)]
        for ld in loads:
            ld.start()
        local, sends = [], []
        for a in range(n):
            loads[a].wait()
            mine = pltpu.make_async_copy(bufs[a], outs[a].at[4 * x + 2 * y + c], local_sems.at[a])
            mine.start()
            local.append(mine)
            first = [copy(a, 0, me, sibling, src=bufs[a])] + [copy(a, 1 + j, me, (*chip, c), src=bufs[a]) for j, chip in enumerate(chips)]
            for cp in first:
                cp.start()
            sends += first
        for a in range(n):
            for j, chip in enumerate(chips):
                copy(a, 1 + j, (*chip, c), me).wait_recv()
                passed = copy(a, 4 + j, (*chip, c), sibling)
                passed.start()
                sends.append(passed)
        for a in range(n):
            copy(a, 0, sibling, me).wait_recv()
            for j, chip in enumerate(chips):
                copy(a, 4 + j, (*chip, 1 - c), me).wait_recv()
        for cp in sends:
            cp.wait_send()
        for mine in local:
            mine.wait()

    return pl.pallas_call(
        body, name="gather_weights", in_specs=[ANY] * n, out_specs=[ANY] * n,
        out_shape=[jax.ShapeDtypeStruct((N_DEV,) + s.shape, s.dtype) for s in shards],
        scratch_shapes=[pltpu.VMEM(s.shape, s.dtype) for s in shards]
        + [pltpu.SemaphoreType.DMA((n, 7)), pltpu.SemaphoreType.DMA((n, 7)), pltpu.SemaphoreType.DMA((n,))],
    )(*shards)


def _rs_sibling(grads):
    n = len(grads)

    def body(*refs):
        g, got = refs[:n], refs[n:2 * n]
        send_sems, recv_sems = refs[2 * n:]
        x, y, c = _pos()
        copies = []
        for a in range(n):
            for q in range(4):
                cp = pltpu.make_async_remote_copy(src_ref=g[a].at[2 * q + (1 - c)], dst_ref=got[a].at[q], send_sem=send_sems.at[a, q],
                                                  recv_sem=recv_sems.at[a, q], device_id=(x, y, 1 - c), device_id_type=MESH)
                cp.start()
                copies.append(cp)
        for cp in copies:
            cp.wait()

    return pl.pallas_call(
        body, name="reduce_sibling", in_specs=[ANY] * n, out_specs=[ANY] * n,
        out_shape=[jax.ShapeDtypeStruct((4,) + g.shape[1:], g.dtype) for g in grads],
        scratch_shapes=[pltpu.SemaphoreType.DMA((n, 4))] * 2,
    )(*grads)


def _rs_chips(sums):
    n = len(sums)

    def body(*refs):
        s, out = refs[:n], refs[n:2 * n]
        send_sems, recv_sems = refs[2 * n:]
        x, y, c = _pos()
        copies = []
        for a in range(n):
            for f in (1, 2, 3):
                peer = (_flip(x, f >> 1), _flip(y, f & 1), c)
                cp = pltpu.make_async_remote_copy(src_ref=s[a].at[2 * peer[0] + peer[1]], dst_ref=out[a].at[f - 1],
                                                  send_sem=send_sems.at[a, f - 1], recv_sem=recv_sems.at[a, f - 1], device_id=peer,
                                                  device_id_type=MESH)
                cp.start()
                copies.append(cp)
        for cp in copies:
            cp.wait()

    return pl.pallas_call(
        body, name="reduce_chips", in_specs=[ANY] * n, out_specs=[ANY] * n,
        out_shape=[jax.ShapeDtypeStruct((3,) + s.shape[1:], s.dtype) for s in sums],
        scratch_shapes=[pltpu.SemaphoreType.DMA((n, 3))] * 2,
    )(*sums)


def _gather_small(g_w1, g_w2, g_w3, g_lb, g_wn, loss):
    def body(w1_ref, w2_ref, w3_ref, lb_ref, wn_ref, loss_ref, out_ref, pk, send_sems, recv_sems):
        x, y, c = _pos()
        me = 4 * x + 2 * y + c
        pk[...] = jnp.zeros_like(pk)
        pk[0:1, :] = w1_ref[...]
        pk[1:2, :] = w2_ref[...]
        pk[2:3, :] = w3_ref[...]
        pk[3:4, 0:HGRN_W] = lb_ref[...]
        pk[3:4, HGRN_W:2 * HGRN_W] = wn_ref[...]
        pk[4:5, 0:128] = loss_ref[...]
        out_ref[me] = pk[...]
        sends, recvs = [], []
        for k in range(1, N_DEV):
            peer = (_flip(x, k >> 2), _flip(y, (k >> 1) & 1), _flip(c, k & 1))
            cp = pltpu.make_async_remote_copy(src_ref=pk, dst_ref=out_ref.at[me], send_sem=send_sems.at[k - 1],
                                              recv_sem=recv_sems.at[k - 1], device_id=peer, device_id_type=MESH)
            cp.start()
            sends.append(cp)
            recvs.append(pltpu.make_async_remote_copy(src_ref=pk, dst_ref=out_ref.at[4 * peer[0] + 2 * peer[1] + peer[2]],
                                                      send_sem=send_sems.at[k - 1], recv_sem=recv_sems.at[k - 1], device_id=peer,
                                                      device_id_type=MESH))
        for cp in recvs:
            cp.wait_recv()
        for cp in sends:
            cp.wait_send()

    return pl.pallas_call(
        body, name="gather_small", in_specs=[VMEM_SPEC] * 6, out_specs=VMEM_SPEC,
        out_shape=jax.ShapeDtypeStruct((N_DEV, 8, D_MODEL), f32),
        scratch_shapes=[pltpu.VMEM((8, D_MODEL), f32), pltpu.SemaphoreType.DMA((N_DEV - 1,)), pltpu.SemaphoreType.DMA((N_DEV - 1,))],
    )(g_w1, g_w2, g_w3, g_lb, g_wn, loss)


def _row_tile(r):
    return max(t for t in range(8, 257, 8) if r % t == 0)


def _add_sibling(core, g, got, name):
    _, r, c = got.shape
    tr = _row_tile(r)

    def body(core_ref, a_ref, b_ref, o_ref):
        o_ref[...] = (a_ref[...] + b_ref[...]).astype(bf16)

    blk = pl.BlockSpec((1, tr, c), lambda q, i, core_ref: (q, i, 0))
    return pl.pallas_call(
        body, name=name, out_shape=jax.ShapeDtypeStruct(got.shape, bf16),
        grid_spec=pltpu.PrefetchScalarGridSpec(
            num_scalar_prefetch=1, grid=(4, r // tr),
            in_specs=[pl.BlockSpec((1, tr, c), lambda q, i, core_ref: (2 * q + core_ref[0], i, 0)), blk], out_specs=blk),
        compiler_params=_cp("arbitrary", "arbitrary"))(core, g, got)


def _adamw(w, g, m, v):
    m = ADAM_B1 * m + (1.0 - ADAM_B1) * g
    v = ADAM_B2 * v + (1.0 - ADAM_B2) * (g * g)
    m_hat = m / (1.0 - ADAM_B1 ** ADAM_STEP)
    v_hat = v / (1.0 - ADAM_B2 ** ADAM_STEP)
    return -ADAM_LR * (m_hat / (jnp.sqrt(v_hat) + ADAM_EPS) + ADAM_WD * w), m, v


def _adam_shard(where, g, got, pieces, w, m, v, name):
    r, c = w.shape
    tr = _row_tile(r)

    def body(where_ref, g_ref, got_ref, p_ref, w_ref, m_ref, v_ref, g_out, d_out, m_out, v_out):
        gsum = g_ref[0] + got_ref[0]
        for f in range(3):
            gsum = gsum + p_ref[f].astype(f32)
        g_out[...] = gsum
        d_out[...], m_out[...], v_out[...] = _adamw(w_ref[...], gsum, m_ref[...], v_ref[...])

    blk = pl.BlockSpec((tr, c), lambda i, where_ref: (i, 0))
    return pl.pallas_call(
        body, name=name, out_shape=[jax.ShapeDtypeStruct((r, c), f32)] * 4,
        grid_spec=pltpu.PrefetchScalarGridSpec(
            num_scalar_prefetch=1, grid=(r // tr,),
            in_specs=[pl.BlockSpec((1, tr, c), lambda i, where_ref: (where_ref[0], i, 0)),
                      pl.BlockSpec((1, tr, c), lambda i, where_ref: (where_ref[1], i, 0)),
                      pl.BlockSpec((3, tr, c), lambda i, where_ref: (0, i, 0)), blk, blk, blk],
            out_specs=[blk] * 4),
        compiler_params=_cp("arbitrary"),
    )(where, g, got, pieces, w, m, v)


def _small_update(gath, params):
    def body(gath_ref, *refs):
        ins, outs = refs[:15], refs[15:]
        gs = gath_ref[0]
        for k in range(1, N_DEV):
            gs = gs + gath_ref[k]
        outs[0][...] = gs[4:5, 0:128]
        l0, l1 = ins[9][0:1, :], ins[9][1:2, :]
        lb = _sigmoid(l0 - l1)
        d0 = gs[3:4, 0:HGRN_W] * lb * (1.0 - lb)
        first_row = lax.broadcasted_iota(jnp.int32, (2, HGRN_W), 0) == 0
        grads = [gs[0:1, :], gs[1:2, :], gs[2:3, :], jnp.where(first_row, d0, -d0), gs[3:4, HGRN_W:2 * HGRN_W]]
        for i, g in enumerate(grads):
            w_ref, m_ref, v_ref = ins[3 * i:3 * i + 3]
            o = outs[1 + 4 * i:5 + 4 * i]
            o[0][...] = g
            o[1][...], o[2][...], o[3][...] = _adamw(w_ref[...], g, m_ref[...], v_ref[...])

    flat = [a for p in params for a in p]
    out_shape = [jax.ShapeDtypeStruct((1, 128), f32)] + [jax.ShapeDtypeStruct(p[0].shape, f32) for p in params for _ in range(4)]
    outs = pl.pallas_call(body, name="small_update", in_specs=[VMEM_SPEC] * 16, out_specs=[VMEM_SPEC] * 21, out_shape=out_shape)(gath, *flat)
    return outs[0], [outs[1 + 4 * i:5 + 4 * i] for i in range(5)]


def kernel(x, norm1_w, w_in, lb_logits, hgrn_norm_w, w_out, norm2_w, w_gate_up, w_down, final_norm_w, loss_target, m_norm1_w, m_w_in, m_lb_logits, m_hgrn_norm_w, m_w_out, m_norm2_w, m_w_gate_up, m_w_down, m_final_norm_w, v_norm1_w, v_w_in, v_lb_logits, v_hgrn_norm_w, v_w_out, v_norm2_w, v_w_gate_up, v_w_down, v_final_norm_w):
    row = lambda a: a.reshape(1, D_MODEL)
    shards = [w_in[0], w_out[0], w_gate_up[0], w_down[0]]
    win_g, wout_g, wgu_g, wdown_g = _all_gather([s.astype(bf16) for s in shards])
    win = jnp.transpose(win_g, (1, 0, 2)).reshape(D_MODEL, IN_W)
    wgu = jnp.transpose(wgu_g, (1, 0, 2)).reshape(D_MODEL, 2 * FFN)
    wout = wout_g.reshape(D_MODEL, D_MODEL)
    wdown = wdown_g.reshape(FFN, D_MODEL)

    loss_p, gx, (g_win, g_wout, g_wgu, g_wdown), (g_w1, g_lb, g_wn, g_w2, g_w3) = _local_step(
        x[0], loss_target[0], norm1_w, win, lb_logits, hgrn_norm_w, wout, norm2_w, wgu, wdown, row(final_norm_w))

    by_owner = lambda g, w: jnp.transpose(g.reshape(g.shape[0], g.shape[1] // w, w), (1, 0, 2))
    grads = [by_owner(g_win, IN_W // N_DEV), g_wout.reshape(N_DEV, D_MODEL // N_DEV, D_MODEL),
             jnp.concatenate([by_owner(g, 2 * FFN // N_DEV) for g in g_wgu], axis=0), g_wdown.reshape(N_DEV, FFN // N_DEV, D_MODEL)]
    ix, iy, ic = lax.axis_index("x"), lax.axis_index("y"), lax.axis_index("c")
    core = jnp.stack([ic]).astype(jnp.int32)
    where = jnp.stack([4 * ix + 2 * iy + ic, 2 * ix + iy]).astype(jnp.int32)
    got = _rs_sibling(grads)
    sums = [_add_sibling(core, g, o, f"add_sibling_{i}") for i, (g, o) in enumerate(zip(grads, got))]
    pieces = _rs_chips(sums)
    moms = [(m_w_in[0], v_w_in[0]), (m_w_out[0], v_w_out[0]), (m_w_gate_up[0], v_w_gate_up[0]), (m_w_down[0], v_w_down[0])]
    big = [_adam_shard(where, g, o, p, w, m, v, f"adam_{i}")
           for i, (g, o, p, w, (m, v)) in enumerate(zip(grads, got, pieces, shards, moms))]
    big = [[a[None] for a in four] for four in big]

    gath = _gather_small(g_w1, g_w2, g_w3, g_lb, g_wn, loss_p)
    params = [(norm1_w, m_norm1_w, v_norm1_w), (norm2_w, m_norm2_w, v_norm2_w),
              (row(final_norm_w), row(m_final_norm_w), row(v_final_norm_w)),
              (lb_logits, m_lb_logits, v_lb_logits), (hgrn_norm_w, m_hgrn_norm_w, v_hgrn_norm_w)]
    loss, (s_w1, s_w2, s_w3, s_lb, s_wn) = _small_update(gath, params)
    s_w3 = [a.reshape(D_MODEL) for a in s_w3]
    per_w = [s_w1, big[0], s_lb, s_wn, big[1], s_w2, big[2], big[3], s_w3]
    return (loss[0, 0], gx[None], *[p[0] for p in per_w], *[p[1] for p in per_w], *[p[2] for p in per_w], *[p[3] for p in per_w])
```

```python
import jax
import jax.numpy as jnp
from jax import lax
from jax.experimental import pallas as pl
from jax.experimental.pallas import tpu as pltpu

f32, bf16 = jnp.float32, jnp.bfloat16

D_MODEL = 1024
ATTN_W = 512
HEAD_DIM = 64
ATTN_BLK = 128
DILATIONS = (1, 4, 16)
HGRN_W = 512
HGRN_HD = 128
CHUNK = 16
IN_W = 3 * ATTN_W + 4 * HGRN_W
FFN = 2816
EPS = 1e-6
ROPE_THETA = 10000.0
NEG = -1e30
N_DEV = 8
ADAM_LR, ADAM_B1, ADAM_B2, ADAM_EPS, ADAM_WD, ADAM_STEP = 0.001, 0.9, 0.999, 1e-08, 0.01, 10
VMEM_LIMIT = 56 * 1024 * 1024


def _cp(*sem):
    return pltpu.CompilerParams(dimension_semantics=sem, vmem_limit_bytes=VMEM_LIMIT)


def _dot(a, b):
    return jnp.dot(a, b, preferred_element_type=f32)


def _dot_nt(a, b):
    return lax.dot_general(a, b, (((1,), (1,)), ((), ())), preferred_element_type=f32)


def _dot_tn(a, b):
    return lax.dot_general(a, b, (((0,), (0,)), ((), ())), preferred_element_type=f32)


def _sigmoid(x):
    return 1.0 / (1.0 + jnp.exp(-x))


def _rope_tables(S):
    half = HEAD_DIM // 2
    inv_freq = ROPE_THETA ** (-jnp.arange(half, dtype=f32) / half)
    ang = jnp.arange(S, dtype=f32)[:, None] * inv_freq[None, :]
    cos, sin = jnp.cos(ang), jnp.sin(ang)
    return jnp.concatenate([cos, cos, cos, cos], axis=1), jnp.concatenate([-sin, sin, -sin, sin], axis=1)


def _swap_halves(v):
    n = v.shape[1]
    lane = lax.broadcasted_iota(jnp.int32, v.shape, 1)
    return jnp.where((lane % HEAD_DIM) < HEAD_DIM // 2, pltpu.roll(v, n - HEAD_DIM // 2, 1), pltpu.roll(v, HEAD_DIM // 2, 1))


def _in_proj(x, w1, win, cos_t, sg_t):
    S = x.shape[0]
    tm = 256

    def body(x_ref, w1_ref, w_ref, cos_ref, sg_ref, u_ref, qkv_ref, hp_ref):
        xv = x_ref[...]
        r = lax.rsqrt(jnp.mean(xv * xv, axis=-1, keepdims=True) + EPS)
        u = (xv * r * w1_ref[...]).astype(bf16)
        u_ref[...] = u
        cosv, sgv = jnp.tile(cos_ref[...], (1, ATTN_W // 128)), jnp.tile(sg_ref[...], (1, ATTN_W // 128))
        for j in range(3):
            pj = _dot(u, w_ref[:, j * ATTN_W:(j + 1) * ATTN_W])
            if j < 2:
                pj = pj * cosv + _swap_halves(pj) * sgv
            if j == 0:
                pj = pj * (HEAD_DIM ** -0.5)
            qkv_ref[:, j * ATTN_W:(j + 1) * ATTN_W] = pj.astype(bf16)
        for j in range(4):
            lo = 3 * ATTN_W + j * HGRN_W
            hp_ref[:, j * HGRN_W:(j + 1) * HGRN_W] = _dot(u, w_ref[:, lo:lo + HGRN_W])

    return pl.pallas_call(
        body, name="in_proj", grid=(S // tm,),
        in_specs=[pl.BlockSpec((tm, D_MODEL), lambda i: (i, 0)), pl.BlockSpec((1, D_MODEL), lambda i: (0, 0)),
                  pl.BlockSpec((D_MODEL, IN_W), lambda i: (0, 0)),
                  pl.BlockSpec((tm, 128), lambda i: (i, 0)), pl.BlockSpec((tm, 128), lambda i: (i, 0))],
        out_specs=[pl.BlockSpec((tm, D_MODEL), lambda i: (i, 0)), pl.BlockSpec((tm, 3 * ATTN_W), lambda i: (i, 0)),
                   pl.BlockSpec((tm, 4 * HGRN_W), lambda i: (i, 0))],
        out_shape=[jax.ShapeDtypeStruct((S, D_MODEL), bf16), jax.ShapeDtypeStruct((S, 3 * ATTN_W), bf16),
                   jax.ShapeDtypeStruct((S, 4 * HGRN_W), f32)],
        compiler_params=_cp("arbitrary"),
    )(x, w1, win, cos_t, sg_t)


def _head_masks():
    lane = lax.broadcasted_iota(jnp.int32, (ATTN_BLK, 128), 1)
    even = lane < HEAD_DIM
    return even, (even, jnp.logical_not(even))


def _pair_fwd(q2, k2, v2, bias):
    even, masks = _head_masks()
    outs, lses = [], []
    for e in range(2):
        qm = jnp.where(masks[e], q2, 0.0).astype(bf16)
        s = _dot_nt(qm, k2) + bias
        m = jnp.max(s, axis=-1, keepdims=True)
        pe = jnp.exp(s - m)
        lsum = jnp.sum(pe, axis=-1, keepdims=True)
        outs.append(_dot(pe.astype(bf16), v2) / lsum)
        lses.append(jnp.broadcast_to(m + jnp.log(lsum), (ATTN_BLK, 128)))
    return jnp.where(even, outs[0], outs[1]), jnp.where(even, lses[0], lses[1])


def _merge(y0, l0, y1, l1):
    mx = jnp.maximum(l0, l1)
    a, b = jnp.exp(l0 - mx), jnp.exp(l1 - mx)
    tot = a + b
    return (a * y0 + b * y1) / tot, mx + jnp.log(tot)


def _pair_bwd(q2, k2f, v2, dy2, lse2, delta2, bias):
    _, masks = _head_masks()
    k2 = k2f.astype(bf16)
    klane = lax.broadcasted_iota(jnp.int32, (2 * ATTN_BLK, 128), 1) < HEAD_DIM
    kmasks = (klane, jnp.logical_not(klane))
    dq2 = jnp.zeros((ATTN_BLK, 128), f32)
    dk2 = jnp.zeros((2 * ATTN_BLK, 128), f32)
    dv2 = jnp.zeros((2 * ATTN_BLK, 128), f32)
    for e in range(2):
        c0 = e * HEAD_DIM
        qm = jnp.where(masks[e], q2, 0.0).astype(bf16)
        km = jnp.where(kmasks[e], k2f, 0.0).astype(bf16)
        dym = jnp.where(masks[e], dy2, 0.0).astype(bf16)
        pe = jnp.exp(_dot_nt(qm, k2) + bias - lse2[:, c0:c0 + 1])
        ds = (pe * (_dot_nt(dym, v2) - delta2[:, c0:c0 + 1])).astype(bf16)
        dv2 = dv2 + _dot_tn(pe.astype(bf16), dym)
        dq2 = dq2 + _dot(ds, km)
        dk2 = dk2 + _dot_tn(ds, qm)
    return dq2, dk2, dv2


TOK = 2048


def _key_bias():
    qi = lax.broadcasted_iota(jnp.int32, (ATTN_BLK, 2 * ATTN_BLK), 0)
    kj = lax.broadcasted_iota(jnp.int32, (ATTN_BLK, 2 * ATTN_BLK), 1)
    delta = ATTN_BLK + qi - kj
    seen = (delta >= 0) & (delta <= ATTN_BLK)
    return jnp.where(seen, 0.0, NEG), jnp.where(seen & (kj >= ATTN_BLK), 0.0, NEG)


def _blocks(dil):
    return [(r, TOK // (ATTN_BLK * dil), ATTN_BLK * dil) for r in range(dil)]


def _attn_fwd(qkv):
    S = qkv.shape[0]
    nS = S // TOK

    def body(q_ref, kp_ref, kc_ref, vp_ref, vc_ref, y_ref, l_ref, qs, k2, v2, ay, al):
        n = pl.program_id(1)
        qs[...] = q_ref[...].astype(f32)
        k2[0:TOK] = kp_ref[...].astype(f32)
        k2[TOK:2 * TOK] = kc_ref[...].astype(f32)
        v2[0:TOK] = vp_ref[...].astype(f32)
        v2[TOK:2 * TOK] = vc_ref[...].astype(f32)
        bias_any, bias_first = _key_bias()

        def block(dil, r, b, step, last):
            start = r + (pl.multiple_of(step * b, step) if step < TOK else 0)
            rows = pl.ds(start, ATTN_BLK, stride=dil) if dil > 1 else pl.ds(start, ATTN_BLK)
            keys = (pl.ds(TOK + start - step, 2 * ATTN_BLK, stride=dil) if dil > 1
                    else pl.ds(TOK + start - step, 2 * ATTN_BLK))
            bias = jnp.where((n == 0) & (b == 0), bias_first, bias_any)
            out, lse = _pair_fwd(qs[rows, :], k2[keys, :].astype(bf16), v2[keys, :].astype(bf16), bias)
            if dil < DILATIONS[-1]:
                out, lse = _merge(ay[rows, :], al[rows, :], out, lse)
            if last:
                y_ref[rows, :] = out
                l_ref[rows, :] = lse
            else:
                ay[rows, :] = out
                al[rows, :] = lse

        for dil in reversed(DILATIONS):
            for r, nblk, step in _blocks(dil):
                if nblk == 1:
                    block(dil, r, 0, step, dil == 1)
                else:
                    def loop(b, carry, dil=dil, r=r, step=step):
                        block(dil, r, b, step, dil == 1)
                        return carry
                    lax.fori_loop(0, nblk, loop, 0, unroll=2)

    blk = (TOK, 128)
    cur = lambda c: pl.BlockSpec(blk, lambda p, n: (n, 4 * c + p))
    prv = lambda c: pl.BlockSpec(blk, lambda p, n: (jnp.maximum(n - 1, 0), 4 * c + p))
    out = pl.BlockSpec(blk, lambda p, n: (n, p))
    return pl.pallas_call(
        body, name="attention_fwd", grid=(ATTN_W // 128, nS),
        in_specs=[cur(0), prv(1), cur(1), prv(2), cur(2)], out_specs=[out, out],
        out_shape=[jax.ShapeDtypeStruct((S, ATTN_W), f32)] * 2,
        scratch_shapes=[pltpu.VMEM(blk, f32), pltpu.VMEM((2 * TOK, 128), f32), pltpu.VMEM((2 * TOK, 128), f32),
                        pltpu.VMEM(blk, f32), pltpu.VMEM(blk, f32)],
        compiler_params=_cp("arbitrary", "arbitrary"),
    )(qkv, qkv, qkv, qkv, qkv)


def _attn_bwd(qkv, ya, lse, dmix):
    S = qkv.shape[0]
    nS = S // TOK

    def body(q_ref, kp_ref, kc_ref, vp_ref, vc_ref, y_ref, l_ref, dy_ref, dq_ref, dk_ref, dv_ref, qs, k2, v2, dk2, dv2, dqa, dl):
        n = pl.program_id(1)

        @pl.when(n == 0)
        def _():
            dk2[...] = jnp.zeros_like(dk2)
            dv2[...] = jnp.zeros_like(dv2)

        @pl.when(n < nS)
        def _():
            qs[...] = q_ref[...].astype(f32)
            k2[0:TOK] = kp_ref[...].astype(f32)
            k2[TOK:2 * TOK] = kc_ref[...].astype(f32)
            v2[0:TOK] = vp_ref[...].astype(f32)
            v2[TOK:2 * TOK] = vc_ref[...].astype(f32)
            li = lax.broadcasted_iota(jnp.int32, (128, 128), 0)
            lj = lax.broadcasted_iota(jnp.int32, (128, 128), 1)
            seg = jnp.where((li // HEAD_DIM) == (lj // HEAD_DIM), 1.0, 0.0).astype(bf16)
            bias_any, bias_first = _key_bias()

            def delta_rows(t, carry):
                rows = pl.ds(pl.multiple_of(256 * t, 256), 256)
                dyy = dy_ref[rows, :] * y_ref[rows, :]
                hi = dyy.astype(bf16)
                dl[rows, :] = _dot(hi, seg) + _dot((dyy - hi.astype(f32)).astype(bf16), seg)
                return carry

            lax.fori_loop(0, TOK // 256, delta_rows, 0)

            def block(dil, r, b, step, first_pattern, last):
                start = r + (pl.multiple_of(step * b, step) if step < TOK else 0)
                rows = pl.ds(start, ATTN_BLK, stride=dil) if dil > 1 else pl.ds(start, ATTN_BLK)
                keys = (pl.ds(TOK + start - step, 2 * ATTN_BLK, stride=dil) if dil > 1
                        else pl.ds(TOK + start - step, 2 * ATTN_BLK))
                bias = jnp.where((n == 0) & (b == 0), bias_first, bias_any)
                dq2, dkk, dvv = _pair_bwd(qs[rows, :], k2[keys, :], v2[keys, :].astype(bf16), dy_ref[rows, :],
                                          l_ref[rows, :], dl[rows, :], bias)
                if last:
                    dq_ref[rows, :] = dqa[rows, :] + dq2
                elif first_pattern:
                    dqa[rows, :] = dq2
                else:
                    dqa[rows, :] += dq2
                dk2[keys, :] += dkk
                dv2[keys, :] += dvv

            for dil in reversed(DILATIONS):
                for r, nblk, step in _blocks(dil):
                    if nblk == 1:
                        block(dil, r, 0, step, dil == DILATIONS[-1], dil == 1)
                    else:
                        def loop(b, carry, dil=dil, r=r, step=step):
                            block(dil, r, b, step, dil == DILATIONS[-1], dil == 1)
                            return carry
                        lax.fori_loop(0, nblk, loop, 0)

        dk_ref[...] = dk2[0:TOK]
        dv_ref[...] = dv2[0:TOK]
        dk2[0:TOK] = dk2[TOK:2 * TOK]
        dv2[0:TOK] = dv2[TOK:2 * TOK]
        dk2[TOK:2 * TOK] = jnp.zeros((TOK, 128), f32)
        dv2[TOK:2 * TOK] = jnp.zeros((TOK, 128), f32)

    blk = (TOK, 128)
    cn = lambda n: jnp.minimum(n, nS - 1)
    pn = lambda n: jnp.clip(n - 1, 0, nS - 1)
    cur = lambda c: pl.BlockSpec(blk, lambda p, n: (cn(n), 4 * c + p))
    prv = lambda c: pl.BlockSpec(blk, lambda p, n: (pn(n), 4 * c + p))
    at_n = pl.BlockSpec(blk, lambda p, n: (cn(n), p))
    at_p = pl.BlockSpec(blk, lambda p, n: (pn(n), p))
    big = lambda: pltpu.VMEM((2 * TOK, 128), f32)
    return pl.pallas_call(
        body, name="attention_bwd", grid=(ATTN_W // 128, nS + 1),
        in_specs=[cur(0), prv(1), cur(1), prv(2), cur(2), at_n, at_n, at_n], out_specs=[at_n, at_p, at_p],
        out_shape=[jax.ShapeDtypeStruct((S, ATTN_W), f32)] * 3,
        scratch_shapes=[pltpu.VMEM(blk, f32), big(), big(), big(), big(), pltpu.VMEM(blk, f32), pltpu.VMEM(blk, f32)],
        compiler_params=_cp("arbitrary", "arbitrary"),
    )(qkv, qkv, qkv, qkv, qkv, ya, lse, dmix)


HG_T = 256
N_HH = HGRN_W // HGRN_HD
HG_SUB = 128
SAFE_RANGE = 80.0


def _row_in_chunk():
    return lax.broadcasted_iota(jnp.int32, (HG_T, HGRN_HD), 0) % CHUNK


def _chunk_cumsum(v, rc):
    for k in (1, 2, 4, 8):
        v = v + jnp.where(rc >= k, pltpu.roll(v, k, 0), 0.0)
    return v


def _chunk_rcumsum(v, rc):
    for k in (1, 2, 4, 8):
        v = v + jnp.where(rc < CHUNK - k, pltpu.roll(v, HG_T - k, 0), 0.0)
    return v


def _hgrn_gates(qb, fb, lb):
    sf = _sigmoid(fb)
    f = lb + (1.0 - lb) * sf
    sq = _sigmoid(qb)
    return sf, f, jnp.log(f), 1.0 - f, sq, qb * sq


def _hgrn_prep(qb, fb, lbl2, rc):
    lb = _sigmoid(lbl2[0:1, :] - lbl2[1:2, :])
    sf, f, lf, key, sq, qf = _hgrn_gates(qb, fb, lb)
    b = _chunk_cumsum(lf, rc)
    rem = _chunk_rcumsum(lf, rc) - lf
    return dict(lb=lb, sf=sf, f=f, key=key, sq=sq, qf=qf, b=b, rem=rem, eb=jnp.exp(b), er=jnp.exp(rem))


def _chunk_mask():
    r = lax.broadcasted_iota(jnp.int32, (HG_SUB, HG_SUB), 0)
    c = lax.broadcasted_iota(jnp.int32, (HG_SUB, HG_SUB), 1)
    return ((r // CHUNK) == (c // CHUNK)) & (c <= r)


def _hgrn_fwd(hp, lbl, wn):
    S = hp.shape[0]
    nT = S // HG_T

    def body(qb_ref, fb_ref, ib_ref, gb_ref, lbl_ref, wn_ref, yb_ref, o_ref, st_ref, ST, qt_s, kh_s, dec_s, oi_s):
        @pl.when(pl.program_id(0) == 0)
        def _():
            ST[...] = jnp.zeros_like(ST)

        rc = _row_in_chunk()
        for h in range(N_HH):
            sl = slice(HGRN_HD * h, HGRN_HD * (h + 1))
            p = _hgrn_prep(qb_ref[:, sl], fb_ref[:, sl], lbl_ref[:, sl], rc)
            qf, key, b = p["qf"], p["key"], p["b"]
            qt = qf * p["eb"]
            qt_s[:, sl] = qt.astype(bf16)
            kh_s[:, sl] = (key * p["er"]).astype(bf16)
            dec_s[:, sl] = jnp.exp(b + p["rem"])
            rng = jnp.max(-(b + p["rem"]))

            @pl.when(rng < SAFE_RANGE)
            def _():
                kp = (key * jnp.exp(-b)).astype(bf16)
                cmask = _chunk_mask()
                for j in range(HG_T // HG_SUB):
                    rs = slice(HG_SUB * j, HG_SUB * (j + 1))
                    sc = jnp.where(cmask, _dot_nt(qt[rs].astype(bf16), kp[rs]), 0.0).astype(bf16)
                    oi_s[rs, sl] = _dot(sc, ib_ref[rs, sl].astype(bf16))

            @pl.when(rng >= SAFE_RANGE)
            def _():
                v = ib_ref[:, sl]
                ones = jnp.ones((HGRN_HD, HGRN_HD), bf16)
                o = jnp.zeros((HG_T, HGRN_HD), f32)
                for l in range(CHUNK):
                    if l == 0:
                        pr, vs = qf * key, v
                    else:
                        e = jnp.exp(jnp.where(rc >= l, b - pltpu.roll(b, l, 0), NEG))
                        pr, vs = qf * pltpu.roll(key, l, 0) * e, pltpu.roll(v, l, 0)
                    o = o + _dot(pr.astype(bf16), ones) * vs
                oi_s[:, sl] = o

        def step(c, carry):
            rows = pl.ds(pl.multiple_of(c * CHUNK, CHUNK), CHUNK)
            row0 = pl.ds(pl.multiple_of(c * CHUNK, CHUNK), 1)
            for h in range(N_HH):
                sl = slice(HGRN_HD * h, HGRN_HD * (h + 1))
                stv = ST[h]
                st_ref[c, sl, :] = stv
                oi_s[rows, sl] += _dot_nt(qt_s[rows, sl], stv.astype(bf16))
                ST[h] = stv * dec_s[row0, sl] + _dot_tn(ib_ref[rows, sl].astype(bf16), kh_s[rows, sl])
            return carry

        lax.fori_loop(0, HG_T // CHUNK, step, 0)

        for h in range(N_HH):
            sl = slice(HGRN_HD * h, HGRN_HD * (h + 1))
            o = oi_s[:, sl]
            o_ref[:, sl] = o
            on = o * lax.rsqrt(jnp.mean(o * o, axis=-1, keepdims=True) + EPS)
            g = gb_ref[:, sl]
            yb_ref[:, sl] = on * wn_ref[:, sl] * (g * _sigmoid(g))

    col = lambda c: pl.BlockSpec((HG_T, HGRN_W), lambda i: (i, c))
    tile = pl.BlockSpec((HG_T, HGRN_W), lambda i: (i, 0))
    whole = lambda a: pl.BlockSpec(a.shape, lambda i: (0, 0))
    return pl.pallas_call(
        body, name="hgrn_fwd", grid=(nT,),
        in_specs=[col(0), col(1), col(2), col(3), whole(lbl), whole(wn)],
        out_specs=[tile, tile, pl.BlockSpec((HG_T // CHUNK, HGRN_W, HGRN_HD), lambda i: (i, 0, 0))],
        out_shape=[jax.ShapeDtypeStruct((S, HGRN_W), f32), jax.ShapeDtypeStruct((S, HGRN_W), f32),
                   jax.ShapeDtypeStruct((S // CHUNK, HGRN_W, HGRN_HD), f32)],
        scratch_shapes=[pltpu.VMEM((N_HH, HGRN_HD, HGRN_HD), f32), pltpu.VMEM((HG_T, HGRN_W), bf16),
                        pltpu.VMEM((HG_T, HGRN_W), bf16), pltpu.VMEM((HG_T, HGRN_W), f32), pltpu.VMEM((HG_T, HGRN_W), f32)],
        compiler_params=_cp("arbitrary"),
    )(hp, hp, hp, hp, lbl, wn)


def _hgrn_bwd(hp, lbl, wn, o_sav, states, dmix):
    S = hp.shape[0]
    nT = S // HG_T

    def body(qb_ref, fb_ref, ib_ref, gb_ref, lbl_ref, wn_ref, o_ref, st_ref, dy_ref,
             dq_ref, df_ref, di_ref, dg_ref, gwn_ref, glb_ref,
             DST, qt_s, kh_s, dec_s, do_s, dqt_s, dkh_s, dbl_s, dvi_s, dqi_s, dki_s, dbi_s):
        @pl.when(pl.program_id(0) == 0)
        def _():
            DST[...] = jnp.zeros_like(DST)
            gwn_ref[...] = jnp.zeros_like(gwn_ref)
            glb_ref[...] = jnp.zeros_like(glb_ref)

        rc = _row_in_chunk()
        for h in range(N_HH):
            sl = slice(HGRN_HD * h, HGRN_HD * (h + 1))
            p = _hgrn_prep(qb_ref[:, sl], fb_ref[:, sl], lbl_ref[:, sl], rc)
            qf, key, b = p["qf"], p["key"], p["b"]
            v = ib_ref[:, sl]
            o = o_ref[:, sl]
            rinv = lax.rsqrt(jnp.mean(o * o, axis=-1, keepdims=True) + EPS)
            on = o * rinv
            g = gb_ref[:, sl]
            sgm = _sigmoid(g)
            silu_g = g * sgm
            dy = dy_ref[:, sl]
            wn_v = wn_ref[:, sl]
            gwn_ref[:, sl] += jnp.sum(dy * on * silu_g, axis=0, keepdims=True)
            dg_ref[:, sl] = (dy * on * wn_v * (sgm * (1.0 + g * (1.0 - sgm)))).astype(bf16)
            t1 = dy * wn_v * silu_g
            do = rinv * (t1 - on * jnp.mean(t1 * on, axis=-1, keepdims=True))
            do_s[:, sl] = do.astype(bf16)
            qt = qf * p["eb"]
            qt_s[:, sl] = qt.astype(bf16)
            kh_s[:, sl] = (key * p["er"]).astype(bf16)
            dec_s[:, sl] = jnp.exp(b + p["rem"])
            rng = jnp.max(-(b + p["rem"]))

            @pl.when(rng < SAFE_RANGE)
            def _():
                einv = jnp.exp(-b)
                kp = (key * einv).astype(bf16)
                cmask = _chunk_mask()
                for j in range(HG_T // HG_SUB):
                    rs = slice(HG_SUB * j, HG_SUB * (j + 1))
                    qtb, dob, vb = qt[rs].astype(bf16), do[rs].astype(bf16), v[rs].astype(bf16)
                    sc = jnp.where(cmask, _dot_nt(qtb, kp[rs]), 0.0).astype(bf16)
                    dsc = jnp.where(cmask, _dot_nt(dob, vb), 0.0).astype(bf16)
                    dqp = _dot(dsc, kp[rs])
                    dkp = _dot_tn(dsc, qtb)
                    dvi_s[rs, sl] = _dot_tn(sc, dob)
                    dqi_s[rs, sl] = dqp * p["eb"][rs]
                    dki_s[rs, sl] = dkp * einv[rs]
                    dbi_s[rs, sl] = dqp * qtb.astype(f32) - dkp * kp[rs].astype(f32)

            @pl.when(rng >= SAFE_RANGE)
            def _():
                ones = jnp.ones((HGRN_HD, HGRN_HD), bf16)
                dqf = jnp.zeros((HG_T, HGRN_HD), f32)
                dkey, db, dv = dqf, dqf, dqf
                for l in range(CHUNK):
                    if l == 0:
                        ks, vs, qe = key, v, qf
                    else:
                        e = jnp.exp(jnp.where(rc >= l, b - pltpu.roll(b, l, 0), NEG))
                        ks, vs, qe = pltpu.roll(key, l, 0), pltpu.roll(v, l, 0), qf * e
                    pr = qe * ks
                    rl = _dot(pr.astype(bf16), ones)
                    drl = _dot((do * vs).astype(bf16), ones)
                    if l == 0:
                        dqf = dqf + drl * ks
                        dv = dv + rl * do
                        dkey = dkey + drl * qe
                    else:
                        drl = jnp.where(rc >= l, drl, 0.0)
                        gl = drl * pr
                        dqf = dqf + drl * ks * e
                        dv = dv + pltpu.roll(rl * do, HG_T - l, 0)
                        dkey = dkey + pltpu.roll(drl * qe, HG_T - l, 0)
                        db = db + gl - pltpu.roll(gl, HG_T - l, 0)
                dvi_s[:, sl] = dv
                dqi_s[:, sl] = dqf
                dki_s[:, sl] = dkey
                dbi_s[:, sl] = db

        def step(k, carry):
            c = HG_T // CHUNK - 1 - k
            rows = pl.ds(pl.multiple_of(c * CHUNK, CHUNK), CHUNK)
            row0 = pl.ds(pl.multiple_of(c * CHUNK, CHUNK), 1)
            for h in range(N_HH):
                sl = slice(HGRN_HD * h, HGRN_HD * (h + 1))
                stp = st_ref[c, sl, :]
                dst = DST[h]
                dstb = dst.astype(bf16)
                dob = do_s[rows, sl]
                khb = kh_s[rows, sl]
                dec = dec_s[row0, sl]
                dqt_s[rows, sl] = _dot(dob, stp.astype(bf16))
                dkh = _dot(ib_ref[rows, sl].astype(bf16), dstb)
                dkh_s[rows, sl] = dkh
                dvi_s[rows, sl] += _dot_nt(khb, dstb)
                dbl = jnp.sum(dst * stp, axis=0, keepdims=True) * dec + jnp.sum(dkh * khb.astype(f32), axis=0, keepdims=True)
                dbl_s[rows, sl] = jnp.broadcast_to(dbl, (CHUNK, HGRN_HD))
                DST[h] = dst * dec + _dot_tn(dob, qt_s[rows, sl])
            return carry

        lax.fori_loop(0, HG_T // CHUNK, step, 0)

        for h in range(N_HH):
            sl = slice(HGRN_HD * h, HGRN_HD * (h + 1))
            qb = qb_ref[:, sl]
            p = _hgrn_prep(qb, fb_ref[:, sl], lbl_ref[:, sl], rc)
            sf, sq, lb = p["sf"], p["sq"], p["lb"]
            dqt, dkh = dqt_s[:, sl], dkh_s[:, sl]
            dqf = dqt * p["eb"] + dqi_s[:, sl]
            dkey = dkh * p["er"] + dki_s[:, sl]
            db = dqt * (p["qf"] * p["eb"]) - dkh * (p["key"] * p["er"]) + jnp.where(rc == CHUNK - 1, dbl_s[:, sl], 0.0) + dbi_s[:, sl]
            df = _chunk_rcumsum(db, rc) / p["f"] - dkey
            df_ref[:, sl] = (df * (1.0 - lb) * sf * (1.0 - sf)).astype(bf16)
            glb_ref[:, sl] += jnp.sum(df * (1.0 - sf), axis=0, keepdims=True)
            dq_ref[:, sl] = (dqf * (sq * (1.0 + qb * (1.0 - sq)))).astype(bf16)
            di_ref[:, sl] = dvi_s[:, sl].astype(bf16)

    rev = lambda i: nT - 1 - i
    col = lambda c: pl.BlockSpec((HG_T, HGRN_W), lambda i: (rev(i), c))
    tile = pl.BlockSpec((HG_T, HGRN_W), lambda i: (rev(i), 0))
    whole = lambda a: pl.BlockSpec(a.shape, lambda i: (0, 0))
    vec = pl.BlockSpec((1, HGRN_W), lambda i: (0, 0))
    tb = lambda: pltpu.VMEM((HG_T, HGRN_W), bf16)
    tf = lambda: pltpu.VMEM((HG_T, HGRN_W), f32)
    return pl.pallas_call(
        body, name="hgrn_bwd", grid=(nT,),
        in_specs=[col(0), col(1), col(2), col(3), whole(lbl), whole(wn), tile,
                  pl.BlockSpec((HG_T // CHUNK, HGRN_W, HGRN_HD), lambda i: (rev(i), 0, 0)),
                  pl.BlockSpec((HG_T, HGRN_W), lambda i: (rev(i), 1))],
        out_specs=[tile, tile, tile, tile, vec, vec],
        out_shape=[jax.ShapeDtypeStruct((S, HGRN_W), bf16)] * 4 + [jax.ShapeDtypeStruct((1, HGRN_W), f32)] * 2,
        scratch_shapes=[pltpu.VMEM((N_HH, HGRN_HD, HGRN_HD), f32), tb(), tb(), tf(), tb(), tf(), tf(), tf(), tf(), tf(), tf(), tf()],
        compiler_params=_cp("arbitrary"),
    )(hp, hp, hp, hp, lbl, wn, o_sav, states, dmix)


def _out_proj(x, ya, yb, wout, w2):
    S = x.shape[0]
    tm = 512

    def body(x_ref, ya_ref, yb_ref, w_ref, w2_ref, h1_ref, u2_ref, mix_ref):
        mixed = jnp.concatenate([ya_ref[...], yb_ref[...]], axis=1).astype(bf16)
        mix_ref[...] = mixed
        h1 = x_ref[...] + _dot(mixed, w_ref[...])
        h1_ref[...] = h1
        r = lax.rsqrt(jnp.mean(h1 * h1, axis=-1, keepdims=True) + EPS)
        u2_ref[...] = (h1 * r * w2_ref[...]).astype(bf16)

    row = lambda w: pl.BlockSpec((tm, w), lambda i: (i, 0))
    return pl.pallas_call(
        body, name="out_proj", grid=(S // tm,),
        in_specs=[row(D_MODEL), row(ATTN_W), row(HGRN_W), pl.BlockSpec((D_MODEL, D_MODEL), lambda i: (0, 0)),
                  pl.BlockSpec((1, D_MODEL), lambda i: (0, 0))],
        out_specs=[row(D_MODEL), row(D_MODEL), row(D_MODEL)],
        out_shape=[jax.ShapeDtypeStruct((S, D_MODEL), f32), jax.ShapeDtypeStruct((S, D_MODEL), bf16),
                   jax.ShapeDtypeStruct((S, D_MODEL), bf16)],
        compiler_params=_cp("arbitrary"),
    )(x, ya, yb, wout, w2)


def _gate_up(u2, wgu):
    S = u2.shape[0]
    tm, tn = 512, 1408
    nj = FFN // tn

    def body(u_ref, wg_ref, wu_ref, g_ref, up_ref, a_ref):
        u = u_ref[...]
        g = _dot(u, wg_ref[...])
        up = _dot(u, wu_ref[...])
        g_ref[...] = g.astype(bf16)
        up_ref[...] = up.astype(bf16)
        a_ref[...] = (g * _sigmoid(g) * up).astype(bf16)

    out = pl.BlockSpec((tm, tn), lambda j, i: (i, j))
    return pl.pallas_call(
        body, name="gate_up", grid=(nj, S // tm),
        in_specs=[pl.BlockSpec((tm, D_MODEL), lambda j, i: (i, 0)), pl.BlockSpec((D_MODEL, tn), lambda j, i: (0, j)),
                  pl.BlockSpec((D_MODEL, tn), lambda j, i: (0, j + nj))],
        out_specs=[out, out, out],
        out_shape=[jax.ShapeDtypeStruct((S, FFN), bf16)] * 3,
        compiler_params=_cp("arbitrary", "arbitrary"),
    )(u2, wgu, wgu)


def _rms_bwd(dyw, hn, r):
    return r * (dyw - hn * jnp.mean(dyw * hn, axis=-1, keepdims=True))


def _down_loss(act, wdown, h1, tgt, w3):
    S = act.shape[0]
    tm = 256

    def body(a_ref, w_ref, h1_ref, t_ref, w3_ref, dh2_ref, loss_ref, gw3_ref):
        @pl.when(pl.program_id(0) == 0)
        def _():
            loss_ref[...] = jnp.zeros_like(loss_ref)
            gw3_ref[...] = jnp.zeros_like(gw3_ref)

        h2 = h1_ref[...] + _dot(a_ref[...], w_ref[...])
        r = lax.rsqrt(jnp.mean(h2 * h2, axis=-1, keepdims=True) + EPS)
        hn = h2 * r
        w3 = w3_ref[...]
        err = hn * w3 - t_ref[...]
        loss_ref[...] += (0.5 / D_MODEL) * jnp.sum(err * err)
        dy = err * (1.0 / D_MODEL)
        gw3_ref[...] += jnp.sum(dy * hn, axis=0, keepdims=True)
        dh2_ref[...] = _rms_bwd(dy * w3, hn, r)

    row = lambda w: pl.BlockSpec((tm, w), lambda i: (i, 0))
    return pl.pallas_call(
        body, name="down_loss", grid=(S // tm,),
        in_specs=[row(FFN), pl.BlockSpec((FFN, D_MODEL), lambda i: (0, 0)), row(D_MODEL), row(D_MODEL),
                  pl.BlockSpec((1, D_MODEL), lambda i: (0, 0))],
        out_specs=[row(D_MODEL), pl.BlockSpec((1, 128), lambda i: (0, 0)), pl.BlockSpec((1, D_MODEL), lambda i: (0, 0))],
        out_shape=[jax.ShapeDtypeStruct((S, D_MODEL), f32), jax.ShapeDtypeStruct((1, 128), f32),
                   jax.ShapeDtypeStruct((1, D_MODEL), f32)],
        compiler_params=_cp("arbitrary"),
    )(act, wdown, h1, tgt, w3)


def _dact(dh2, wdown, gate, up):
    S = dh2.shape[0]
    tm = 256

    def body(d_ref, w_ref, g_ref, u_ref, dg_ref, du_ref):
        da = _dot_nt(d_ref[...].astype(bf16), w_ref[...])
        g = g_ref[...].astype(f32)
        sg = _sigmoid(g)
        du_ref[...] = (da * g * sg).astype(bf16)
        dg_ref[...] = (da * u_ref[...].astype(f32) * (sg * (1.0 + g * (1.0 - sg)))).astype(bf16)

    row = lambda w: pl.BlockSpec((tm, w), lambda i: (i, 0))
    return pl.pallas_call(
        body, name="dact", grid=(S // tm,),
        in_specs=[row(D_MODEL), pl.BlockSpec((FFN, D_MODEL), lambda i: (0, 0)), row(FFN), row(FFN)],
        out_specs=[row(FFN), row(FFN)],
        out_shape=[jax.ShapeDtypeStruct((S, FFN), bf16)] * 2,
        compiler_params=_cp("arbitrary"),
    )(dh2, wdown, gate, up)


def _dgu(dgate, dup, wgu, h1, w2, dh2):
    S = dgate.shape[0]
    tm = 256

    def body(dg_ref, du_ref, wg_ref, wu_ref, h1_ref, w2_ref, dh2_ref, dh1_ref, gw2_ref):
        @pl.when(pl.program_id(0) == 0)
        def _():
            gw2_ref[...] = jnp.zeros_like(gw2_ref)

        du2 = _dot_nt(dg_ref[...], wg_ref[...]) + _dot_nt(du_ref[...], wu_ref[...])
        h1 = h1_ref[...]
        r = lax.rsqrt(jnp.mean(h1 * h1, axis=-1, keepdims=True) + EPS)
        hn = h1 * r
        gw2_ref[...] += jnp.sum(du2 * hn, axis=0, keepdims=True)
        dh1_ref[...] = dh2_ref[...] + _rms_bwd(du2 * w2_ref[...], hn, r)

    row = lambda w: pl.BlockSpec((tm, w), lambda i: (i, 0))
    return pl.pallas_call(
        body, name="dgu", grid=(S // tm,),
        in_specs=[row(FFN), row(FFN), pl.BlockSpec((D_MODEL, FFN), lambda i: (0, 0)),
                  pl.BlockSpec((D_MODEL, FFN), lambda i: (0, 1)), row(D_MODEL),
                  pl.BlockSpec((1, D_MODEL), lambda i: (0, 0)), row(D_MODEL)],
        out_specs=[row(D_MODEL), pl.BlockSpec((1, D_MODEL), lambda i: (0, 0))],
        out_shape=[jax.ShapeDtypeStruct((S, D_MODEL), f32), jax.ShapeDtypeStruct((1, D_MODEL), f32)],
        compiler_params=_cp("arbitrary"),
    )(dgate, dup, wgu, wgu, h1, w2, dh2)


def _dmixed(dh1, wout):
    S = dh1.shape[0]
    tm = 512

    def body(d_ref, w_ref, o_ref):
        o_ref[...] = _dot_nt(d_ref[...].astype(bf16), w_ref[...])

    row = pl.BlockSpec((tm, D_MODEL), lambda i: (i, 0))
    return pl.pallas_call(
        body, name="dmixed", grid=(S // tm,),
        in_specs=[row, pl.BlockSpec((D_MODEL, D_MODEL), lambda i: (0, 0))], out_specs=row,
        out_shape=jax.ShapeDtypeStruct((S, D_MODEL), f32), compiler_params=_cp("arbitrary"),
    )(dh1, wout)


def _din(dq, dk, dv, dhq, dhf, dhi, dhg, cos_t, sg_t, win, x, w1, dh1):
    S = x.shape[0]
    tm = 256

    def body(dq_ref, dk_ref, dv_ref, dhq_ref, dhf_ref, dhi_ref, dhg_ref, cos_ref, sg_ref, w_ref, x_ref, w1_ref, dh1_ref,
             dp_ref, gx_ref, gw1_ref):
        @pl.when(pl.program_id(0) == 0)
        def _():
            gw1_ref[...] = jnp.zeros_like(gw1_ref)

        cosv, sgv = jnp.tile(cos_ref[...], (1, ATTN_W // 128)), jnp.tile(sg_ref[...], (1, ATTN_W // 128))
        unrope = lambda d: d * cosv - sgv * _swap_halves(d)
        parts = [(unrope(dq_ref[...]) * (HEAD_DIM ** -0.5)).astype(bf16), unrope(dk_ref[...]).astype(bf16),
                 dv_ref[...].astype(bf16), dhq_ref[...], dhf_ref[...], dhi_ref[...], dhg_ref[...]]
        du = jnp.zeros((tm, D_MODEL), f32)
        for j, pj in enumerate(parts):
            dp_ref[:, j * 512:(j + 1) * 512] = pj
            du = du + _dot_nt(pj, w_ref[:, j * 512:(j + 1) * 512])
        xv = x_ref[...]
        r = lax.rsqrt(jnp.mean(xv * xv, axis=-1, keepdims=True) + EPS)
        xn = xv * r
        gw1_ref[...] += jnp.sum(du * xn, axis=0, keepdims=True)
        gx_ref[...] = dh1_ref[...] + _rms_bwd(du * w1_ref[...], xn, r)

    row = lambda w: pl.BlockSpec((tm, w), lambda i: (i, 0))
    vec = pl.BlockSpec((1, D_MODEL), lambda i: (0, 0))
    return pl.pallas_call(
        body, name="din", grid=(S // tm,),
        in_specs=[row(512)] * 7 + [row(128), row(128), pl.BlockSpec((D_MODEL, IN_W), lambda i: (0, 0)), row(D_MODEL), vec,
                                   row(D_MODEL)],
        out_specs=[row(IN_W), row(D_MODEL), vec],
        out_shape=[jax.ShapeDtypeStruct((S, IN_W), bf16), jax.ShapeDtypeStruct((S, D_MODEL), f32),
                   jax.ShapeDtypeStruct((1, D_MODEL), f32)],
        compiler_params=_cp("arbitrary"),
    )(dq, dk, dv, dhq, dhf, dhi, dhg, cos_t, sg_t, win, x, w1, dh1)


def _gw(a, b, tn, name):
    S, M = a.shape
    N = b.shape[1]
    ts = 512

    def body(a_ref, b_ref, o_ref):
        @pl.when(pl.program_id(1) == 0)
        def _():
            o_ref[...] = jnp.zeros_like(o_ref)

        o_ref[...] += _dot_tn(a_ref[...].astype(bf16), b_ref[...].astype(bf16))

    return pl.pallas_call(
        body, name=name, grid=(N // tn, S // ts),
        in_specs=[pl.BlockSpec((ts, M), lambda j, s: (s, 0)), pl.BlockSpec((ts, tn), lambda j, s: (s, j))],
        out_specs=pl.BlockSpec((M, tn), lambda j, s: (0, j)), out_shape=jax.ShapeDtypeStruct((M, N), f32),
        compiler_params=_cp("arbitrary", "arbitrary"),
    )(a, b)


def _local_step(x, tgt, w1, win, lbl, wn, wout, w2, wgu, wdown, w3):
    S = x.shape[0]
    cos_t, sg_t = _rope_tables(S)
    u, qkv, hp = _in_proj(x, w1, win, cos_t, sg_t)
    ya, lse = _attn_fwd(qkv)
    yb, o_sav, states = _hgrn_fwd(hp, lbl, wn)
    h1, u2, mixed = _out_proj(x, ya, yb, wout, w2)
    gate, up, act = _gate_up(u2, wgu)
    dh2, loss, g_w3 = _down_loss(act, wdown, h1, tgt, w3)

    g_wdown = _gw(act, dh2, 512, "gw_down")
    dgate, dup = _dact(dh2, wdown, gate, up)
    g_wgu = (_gw(u2, dgate, 1408, "gw_gate"), _gw(u2, dup, 1408, "gw_up"))
    dh1, g_w2 = _dgu(dgate, dup, wgu, h1, w2, dh2)
    g_wout = _gw(mixed, dh1, 1024, "gw_out")
    dmix = _dmixed(dh1, wout)
    dhq, dhf, dhi, dhg, g_wn, g_lb = _hgrn_bwd(hp, lbl, wn, o_sav, states, dmix)
    datt = _attn_bwd(qkv, ya, lse, dmix)
    dproj, gx, g_w1 = _din(*datt, dhq, dhf, dhi, dhg, cos_t, sg_t, win, x, w1, dh1)
    g_win = _gw(u, dproj, 896, "gw_in")
    return loss, gx, (g_win, g_wout, g_wgu, g_wdown), (g_w1, g_lb, g_wn, g_w2, g_w3)


MESH = pl.DeviceIdType.MESH
ANY = pl.BlockSpec(memory_space=pl.ANY)
VMEM_SPEC = pl.BlockSpec(memory_space=pltpu.VMEM)


def _pos():
    return lax.axis_index("x"), lax.axis_index("y"), lax.axis_index("c")


def _flip(v, bit):
    return 1 - v if bit else v


def _all_gather(shards):
    n = len(shards)

    def body(*refs):
        ins, outs, bufs = refs[:n], refs[n:2 * n], refs[2 * n:3 * n]
        send_sems, recv_sems, local_sems = refs[3 * n:]
        x, y, c = _pos()
        me, sibling = (x, y, c), (x, y, 1 - c)
        chips = [(1 - x, y), (x, 1 - y), (1 - x, 1 - y)]

        def copy(a, k, block, to, src=None):
            dst = outs[a].at[4 * block[0] + 2 * block[1] + block[2]]
            return pltpu.make_async_remote_copy(src_ref=dst if src is None else src, dst_ref=dst, send_sem=send_sems.at[a, k],
                                                recv_sem=recv_sems.at[a, k], device_id=to, device_id_type=MESH)

        loads = [pltpu.make_async_copy(ins[a], bufs[a], local_sems.at[a]) for a in range(n)]
        for ld in loads:
            ld.start()
        local, sends = [], []
        for a in range(n):
            loads[a].wait()
            mine = pltpu.make_async_copy(bufs[a], outs[a].at[4 * x + 2 * y + c], local_sems.at[a])
            mine.start()
            local.append(mine)
            first = [copy(a, 0, me, sibling, src=bufs[a])] + [copy(a, 1 + j, me, (*chip, c), src=bufs[a]) for j, chip in enumerate(chips)]
            for cp in first:
                cp.start()
            sends += first
        for a in range(n):
            for j, chip in enumerate(chips):
                copy(a, 1 + j, (*chip, c), me).wait_recv()
                passed = copy(a, 4 + j, (*chip, c), sibling)
                passed.start()
                sends.append(passed)
        for a in range(n):
            copy(a, 0, sibling, me).wait_recv()
            for j, chip in enumerate(chips):
                copy(a, 4 + j, (*chip, 1 - c), me).wait_recv()
        for cp in sends:
            cp.wait_send()
        for mine in local:
            mine.wait()

    return pl.pallas_call(
        body, name="gather_weights", in_specs=[ANY] * n, out_specs=[ANY] * n,
        out_shape=[jax.ShapeDtypeStruct((N_DEV,) + s.shape, s.dtype) for s in shards],
        scratch_shapes=[pltpu.VMEM(s.shape, s.dtype) for s in shards]
        + [pltpu.SemaphoreType.DMA((n, 7)), pltpu.SemaphoreType.DMA((n, 7)), pltpu.SemaphoreType.DMA((n,))],
    )(*shards)


def _rs_sibling(grads):
    n = len(grads)

    def body(*refs):
        g, got = refs[:n], refs[n:2 * n]
        send_sems, recv_sems = refs[2 * n:]
        x, y, c = _pos()
        copies = []
        for a in range(n):
            for q in range(4):
                cp = pltpu.make_async_remote_copy(src_ref=g[a].at[2 * q + (1 - c)], dst_ref=got[a].at[q], send_sem=send_sems.at[a, q],
                                                  recv_sem=recv_sems.at[a, q], device_id=(x, y, 1 - c), device_id_type=MESH)
                cp.start()
                copies.append(cp)
        for cp in copies:
            cp.wait()

    return pl.pallas_call(
        body, name="reduce_sibling", in_specs=[ANY] * n, out_specs=[ANY] * n,
        out_shape=[jax.ShapeDtypeStruct((4,) + g.shape[1:], g.dtype) for g in grads],
        scratch_shapes=[pltpu.SemaphoreType.DMA((n, 4))] * 2,
    )(*grads)


def _rs_chips(sums):
    n = len(sums)

    def body(*refs):
        s, out = refs[:n], refs[n:2 * n]
        send_sems, recv_sems = refs[2 * n:]
        x, y, c = _pos()
        copies = []
        for a in range(n):
            for f in (1, 2, 3):
                peer = (_flip(x, f >> 1), _flip(y, f & 1), c)
                cp = pltpu.make_async_remote_copy(src_ref=s[a].at[2 * peer[0] + peer[1]], dst_ref=out[a].at[f - 1],
                                                  send_sem=send_sems.at[a, f - 1], recv_sem=recv_sems.at[a, f - 1], device_id=peer,
                                                  device_id_type=MESH)
                cp.start()
                copies.append(cp)
        for cp in copies:
            cp.wait()

    return pl.pallas_call(
        body, name="reduce_chips", in_specs=[ANY] * n, out_specs=[ANY] * n,
        out_shape=[jax.ShapeDtypeStruct((3,) + s.shape[1:], s.dtype) for s in sums],
        scratch_shapes=[pltpu.SemaphoreType.DMA((n, 3))] * 2,
    )(*sums)


def _gather_small(g_w1, g_w2, g_w3, g_lb, g_wn, loss):
    def body(w1_ref, w2_ref, w3_ref, lb_ref, wn_ref, loss_ref, out_ref, pk, send_sems, recv_sems):
        x, y, c = _pos()
        me = 4 * x + 2 * y + c
        pk[...] = jnp.zeros_like(pk)
        pk[0:1, :] = w1_ref[...]
        pk[1:2, :] = w2_ref[...]
        pk[2:3, :] = w3_ref[...]
        pk[3:4, 0:HGRN_W] = lb_ref[...]
        pk[3:4, HGRN_W:2 * HGRN_W] = wn_ref[...]
        pk[4:5, 0:128] = loss_ref[...]
        out_ref[me] = pk[...]
        sends, recvs = [], []
        for k in range(1, N_DEV):
            peer = (_flip(x, k >> 2), _flip(y, (k >> 1) & 1), _flip(c, k & 1))
            cp = pltpu.make_async_remote_copy(src_ref=pk, dst_ref=out_ref.at[me], send_sem=send_sems.at[k - 1],
                                              recv_sem=recv_sems.at[k - 1], device_id=peer, device_id_type=MESH)
            cp.start()
            sends.append(cp)
            recvs.append(pltpu.make_async_remote_copy(src_ref=pk, dst_ref=out_ref.at[4 * peer[0] + 2 * peer[1] + peer[2]],
                                                      send_sem=send_sems.at[k - 1], recv_sem=recv_sems.at[k - 1], device_id=peer,
                                                      device_id_type=MESH))
        for cp in recvs:
            cp.wait_recv()
        for cp in sends:
            cp.wait_send()

    return pl.pallas_call(
        body, name="gather_small", in_specs=[VMEM_SPEC] * 6, out_specs=VMEM_SPEC,
        out_shape=jax.ShapeDtypeStruct((N_DEV, 8, D_MODEL), f32),
        scratch_shapes=[pltpu.VMEM((8, D_MODEL), f32), pltpu.SemaphoreType.DMA((N_DEV - 1,)), pltpu.SemaphoreType.DMA((N_DEV - 1,))],
    )(g_w1, g_w2, g_w3, g_lb, g_wn, loss)


def _row_tile(r):
    return max(t for t in range(8, 257, 8) if r % t == 0)


def _add_sibling(core, g, got, name):
    _, r, c = got.shape
    tr = _row_tile(r)

    def body(core_ref, a_ref, b_ref, o_ref):
        o_ref[...] = (a_ref[...] + b_ref[...]).astype(bf16)

    blk = pl.BlockSpec((1, tr, c), lambda q, i, core_ref: (q, i, 0))
    return pl.pallas_call(
        body, name=name, out_shape=jax.ShapeDtypeStruct(got.shape, bf16),
        grid_spec=pltpu.PrefetchScalarGridSpec(
            num_scalar_prefetch=1, grid=(4, r // tr),
            in_specs=[pl.BlockSpec((1, tr, c), lambda q, i, core_ref: (2 * q + core_ref[0], i, 0)), blk], out_specs=blk),
        compiler_params=_cp("arbitrary", "arbitrary"))(core, g, got)


def _adamw(w, g, m, v):
    m = ADAM_B1 * m + (1.0 - ADAM_B1) * g
    v = ADAM_B2 * v + (1.0 - ADAM_B2) * (g * g)
    m_hat = m / (1.0 - ADAM_B1 ** ADAM_STEP)
    v_hat = v / (1.0 - ADAM_B2 ** ADAM_STEP)
    return -ADAM_LR * (m_hat / (jnp.sqrt(v_hat) + ADAM_EPS) + ADAM_WD * w), m, v


def _adam_shard(where, g, got, pieces, w, m, v, name):
    r, c = w.shape
    tr = _row_tile(r)

    def body(where_ref, g_ref, got_ref, p_ref, w_ref, m_ref, v_ref, g_out, d_out, m_out, v_out):
        gsum = g_ref[0] + got_ref[0]
        for f in range(3):
            gsum = gsum + p_ref[f].astype(f32)
        g_out[...] = gsum
        d_out[...], m_out[...], v_out[...] = _adamw(w_ref[...], gsum, m_ref[...], v_ref[...])

    blk = pl.BlockSpec((tr, c), lambda i, where_ref: (i, 0))
    return pl.pallas_call(
        body, name=name, out_shape=[jax.ShapeDtypeStruct((r, c), f32)] * 4,
        grid_spec=pltpu.PrefetchScalarGridSpec(
            num_scalar_prefetch=1, grid=(r // tr,),
            in_specs=[pl.BlockSpec((1, tr, c), lambda i, where_ref: (where_ref[0], i, 0)),
                      pl.BlockSpec((1, tr, c), lambda i, where_ref: (where_ref[1], i, 0)),
                      pl.BlockSpec((3, tr, c), lambda i, where_ref: (0, i, 0)), blk, blk, blk],
            out_specs=[blk] * 4),
        compiler_params=_cp("arbitrary"),
    )(where, g, got, pieces, w, m, v)


def _small_update(gath, params):
    def body(gath_ref, *refs):
        ins, outs = refs[:15], refs[15:]
        gs = gath_ref[0]
        for k in range(1, N_DEV):
            gs = gs + gath_ref[k]
        outs[0][...] = gs[4:5, 0:128]
        l0, l1 = ins[9][0:1, :], ins[9][1:2, :]
        lb = _sigmoid(l0 - l1)
        d0 = gs[3:4, 0:HGRN_W] * lb * (1.0 - lb)
        first_row = lax.broadcasted_iota(jnp.int32, (2, HGRN_W), 0) == 0
        grads = [gs[0:1, :], gs[1:2, :], gs[2:3, :], jnp.where(first_row, d0, -d0), gs[3:4, HGRN_W:2 * HGRN_W]]
        for i, g in enumerate(grads):
            w_ref, m_ref, v_ref = ins[3 * i:3 * i + 3]
            o = outs[1 + 4 * i:5 + 4 * i]
            o[0][...] = g
            o[1][...], o[2][...], o[3][...] = _adamw(w_ref[...], g, m_ref[...], v_ref[...])

    flat = [a for p in params for a in p]
    out_shape = [jax.ShapeDtypeStruct((1, 128), f32)] + [jax.ShapeDtypeStruct(p[0].shape, f32) for p in params for _ in range(4)]
    outs = pl.pallas_call(body, name="small_update", in_specs=[VMEM_SPEC] * 16, out_specs=[VMEM_SPEC] * 21, out_shape=out_shape)(gath, *flat)
    return outs[0], [outs[1 + 4 * i:5 + 4 * i] for i in range(5)]


def kernel(x, norm1_w, w_in, lb_logits, hgrn_norm_w, w_out, norm2_w, w_gate_up, w_down, final_norm_w, loss_target, m_norm1_w, m_w_in, m_lb_logits, m_hgrn_norm_w, m_w_out, m_norm2_w, m_w_gate_up, m_w_down, m_final_norm_w, v_norm1_w, v_w_in, v_lb_logits, v_hgrn_norm_w, v_w_out, v_norm2_w, v_w_gate_up, v_w_down, v_final_norm_w):
    row = lambda a: a.reshape(1, D_MODEL)
    shards = [w_in[0], w_out[0], w_gate_up[0], w_down[0]]
    win_g, wout_g, wgu_g, wdown_g = _all_gather([s.astype(bf16) for s in shards])
    win = jnp.transpose(win_g, (1, 0, 2)).reshape(D_MODEL, IN_W)
    wgu = jnp.transpose(wgu_g, (1, 0, 2)).reshape(D_MODEL, 2 * FFN)
    wout = wout_g.reshape(D_MODEL, D_MODEL)
    wdown = wdown_g.reshape(FFN, D_MODEL)

    loss_p, gx, (g_win, g_wout, g_wgu, g_wdown), (g_w1, g_lb, g_wn, g_w2, g_w3) = _local_step(
        x[0], loss_target[0], norm1_w, win, lb_logits, hgrn_norm_w, wout, norm2_w, wgu, wdown, row(final_norm_w))

    by_owner = lambda g, w: jnp.transpose(g.reshape(g.shape[0], g.shape[1] // w, w), (1, 0, 2))
    grads = [by_owner(g_win, IN_W // N_DEV), g_wout.reshape(N_DEV, D_MODEL // N_DEV, D_MODEL),
             jnp.concatenate([by_owner(g, 2 * FFN // N_DEV) for g in g_wgu], axis=0), g_wdown.reshape(N_DEV, FFN // N_DEV, D_MODEL)]
    ix, iy, ic = lax.axis_index("x"), lax.axis_index("y"), lax.axis_index("c")
    core = jnp.stack([ic]).astype(jnp.int32)
    where = jnp.stack([4 * ix + 2 * iy + ic, 2 * ix + iy]).astype(jnp.int32)
    got = _rs_sibling(grads)
    sums = [_add_sibling(core, g, o, f"add_sibling_{i}") for i, (g, o) in enumerate(zip(grads, got))]
    pieces = _rs_chips(sums)
    moms = [(m_w_in[0], v_w_in[0]), (m_w_out[0], v_w_out[0]), (m_w_gate_up[0], v_w_gate_up[0]), (m_w_down[0], v_w_down[0])]
    big = [_adam_shard(where, g, o, p, w, m, v, f"adam_{i}")
           for i, (g, o, p, w, (m, v)) in enumerate(zip(grads, got, pieces, shards, moms))]
    big = [[a[None] for a in four] for four in big]

    gath = _gather_small(g_w1, g_w2, g_w3, g_lb, g_wn, loss_p)
    params = [(norm1_w, m_norm1_w, v_norm1_w), (norm2_w, m_norm2_w, v_norm2_w),
              (row(final_norm_w), row(m_final_norm_w), row(v_final_norm_w)),
              (lb_logits, m_lb_logits, v_lb_logits), (hgrn_norm_w, m_hgrn_norm_w, v_hgrn_norm_w)]
    loss, (s_w1, s_w2, s_w3, s_lb, s_wn) = _small_update(gath, params)
    s_w3 = [a.reshape(D_MODEL) for a in s_w3]
    per_w = [s_w1, big[0], s_lb, s_wn, big[1], s_w2, big[2], big[3], s_w3]
    return (loss[0, 0], gx[None], *[p[0] for p in per_w], *[p[1] for p in per_w], *[p[2] for p in per_w], *[p[3] for p in per_w])
```

```python
import jax
import jax.numpy as jnp
from jax import lax
from jax.experimental import pallas as pl
from jax.experimental.pallas import tpu as pltpu

f32, bf16 = jnp.float32, jnp.bfloat16

D_MODEL = 1024
ATTN_W = 512
HEAD_DIM = 64
ATTN_BLK = 128
DILATIONS = (1, 4, 16)
HGRN_W = 512
HGRN_HD = 128
CHUNK = 16
IN_W = 3 * ATTN_W + 4 * HGRN_W
FFN = 2816
EPS = 1e-6
ROPE_THETA = 10000.0
NEG = -1e30
N_DEV = 8
ADAM_LR, ADAM_B1, ADAM_B2, ADAM_EPS, ADAM_WD, ADAM_STEP = 0.001, 0.9, 0.999, 1e-08, 0.01, 10
VMEM_LIMIT = 56 * 1024 * 1024


def _cp(*sem):
    return pltpu.CompilerParams(dimension_semantics=sem, vmem_limit_bytes=VMEM_LIMIT)


def _dot(a, b):
    return jnp.dot(a, b, preferred_element_type=f32)


def _dot_nt(a, b):
    return lax.dot_general(a, b, (((1,), (1,)), ((), ())), preferred_element_type=f32)


def _dot_tn(a, b):
    return lax.dot_general(a, b, (((0,), (0,)), ((), ())), preferred_element_type=f32)


def _sigmoid(x):
    return 1.0 / (1.0 + jnp.exp(-x))


class _Rider:
    def __init__(self, ins, out_shapes, scratch, first, last, middle=None):
        self.ins, self.out_shapes, self.scratch = list(ins), list(out_shapes), list(scratch)
        self.first, self.middle, self.last = first, middle, last


def _ride(call, rider, body, step, n_steps, n_in, n_out, n_scratch):
    if rider is None:
        return call, body, []
    ri, ro = len(rider.ins), len(rider.out_shapes)
    any_spec = pl.BlockSpec(memory_space=pl.ANY)
    call = dict(call, in_specs=call["in_specs"] + [any_spec] * ri, out_specs=call["out_specs"] + [any_spec] * ro,
                out_shape=call["out_shape"] + rider.out_shapes, scratch_shapes=call["scratch_shapes"] + rider.scratch)

    def riding(*refs):
        a = n_in + ri
        b = a + n_out + ro
        mine = refs[:n_in] + refs[a:a + n_out] + refs[b:b + n_scratch]
        theirs = (refs[n_in:a], refs[a + n_out:b], refs[b + n_scratch:])
        t = step()

        @pl.when(t == 0)
        def _():
            rider.first(*theirs)

        body(*mine)
        if rider.middle is not None:
            @pl.when(t == n_steps // 2)
            def _():
                rider.middle(*theirs)

        @pl.when(t == n_steps - 1)
        def _():
            rider.last(*theirs)

    return call, riding, rider.ins


def _rope_tables(S):
    half = HEAD_DIM // 2
    inv_freq = ROPE_THETA ** (-jnp.arange(half, dtype=f32) / half)
    ang = jnp.arange(S, dtype=f32)[:, None] * inv_freq[None, :]
    cos, sin = jnp.cos(ang), jnp.sin(ang)
    return jnp.concatenate([cos, cos, cos, cos], axis=1), jnp.concatenate([-sin, sin, -sin, sin], axis=1)


def _swap_halves(v):
    n = v.shape[1]
    lane = lax.broadcasted_iota(jnp.int32, v.shape, 1)
    return jnp.where((lane % HEAD_DIM) < HEAD_DIM // 2, pltpu.roll(v, n - HEAD_DIM // 2, 1), pltpu.roll(v, HEAD_DIM // 2, 1))


def _in_proj(x, w1, win, cos_t, sg_t):
    S = x.shape[0]
    tm = 256

    def body(x_ref, w1_ref, w_ref, cos_ref, sg_ref, u_ref, qkv_ref, hp_ref):
        xv = x_ref[...]
        r = lax.rsqrt(jnp.mean(xv * xv, axis=-1, keepdims=True) + EPS)
        u = (xv * r * w1_ref[...]).astype(bf16)
        u_ref[...] = u
        cosv, sgv = jnp.tile(cos_ref[...], (1, ATTN_W // 128)), jnp.tile(sg_ref[...], (1, ATTN_W // 128))
        for j in range(3):
            pj = _dot(u, w_ref[:, j * ATTN_W:(j + 1) * ATTN_W])
            if j < 2:
                pj = pj * cosv + _swap_halves(pj) * sgv
            if j == 0:
                pj = pj * (HEAD_DIM ** -0.5)
            qkv_ref[:, j * ATTN_W:(j + 1) * ATTN_W] = pj.astype(bf16)
        for j in range(4):
            lo = 3 * ATTN_W + j * HGRN_W
            hp_ref[:, j * HGRN_W:(j + 1) * HGRN_W] = _dot(u, w_ref[:, lo:lo + HGRN_W])

    return pl.pallas_call(
        body, name="in_proj", grid=(S // tm,),
        in_specs=[pl.BlockSpec((tm, D_MODEL), lambda i: (i, 0)), pl.BlockSpec((1, D_MODEL), lambda i: (0, 0)),
                  pl.BlockSpec((D_MODEL, IN_W), lambda i: (0, 0)),
                  pl.BlockSpec((tm, 128), lambda i: (i, 0)), pl.BlockSpec((tm, 128), lambda i: (i, 0))],
        out_specs=[pl.BlockSpec((tm, D_MODEL), lambda i: (i, 0)), pl.BlockSpec((tm, 3 * ATTN_W), lambda i: (i, 0)),
                   pl.BlockSpec((tm, 4 * HGRN_W), lambda i: (i, 0))],
        out_shape=[jax.ShapeDtypeStruct((S, D_MODEL), bf16), jax.ShapeDtypeStruct((S, 3 * ATTN_W), bf16),
                   jax.ShapeDtypeStruct((S, 4 * HGRN_W), f32)],
        compiler_params=_cp("arbitrary"),
    )(x, w1, win, cos_t, sg_t)


def _head_masks():
    lane = lax.broadcasted_iota(jnp.int32, (ATTN_BLK, 128), 1)
    even = lane < HEAD_DIM
    return even, (even, jnp.logical_not(even))


def _pair_fwd(q2, k2, v2, bias):
    even, masks = _head_masks()
    outs, lses = [], []
    for e in range(2):
        qm = jnp.where(masks[e], q2, 0.0).astype(bf16)
        s = _dot_nt(qm, k2) + bias
        m = jnp.max(s, axis=-1, keepdims=True)
        pe = jnp.exp(s - m)
        lsum = jnp.sum(pe, axis=-1, keepdims=True)
        outs.append(_dot(pe.astype(bf16), v2) / lsum)
        lses.append(jnp.broadcast_to(m + jnp.log(lsum), (ATTN_BLK, 128)))
    return jnp.where(even, outs[0], outs[1]), jnp.where(even, lses[0], lses[1])


def _merge(y0, l0, y1, l1):
    mx = jnp.maximum(l0, l1)
    a, b = jnp.exp(l0 - mx), jnp.exp(l1 - mx)
    tot = a + b
    return (a * y0 + b * y1) / tot, mx + jnp.log(tot)


def _pair_bwd(q2, k2f, v2, dy2, lse2, delta2, bias):
    _, masks = _head_masks()
    k2 = k2f.astype(bf16)
    klane = lax.broadcasted_iota(jnp.int32, (2 * ATTN_BLK, 128), 1) < HEAD_DIM
    kmasks = (klane, jnp.logical_not(klane))
    dq2 = jnp.zeros((ATTN_BLK, 128), f32)
    dk2 = jnp.zeros((2 * ATTN_BLK, 128), f32)
    dv2 = jnp.zeros((2 * ATTN_BLK, 128), f32)
    for e in range(2):
        c0 = e * HEAD_DIM
        qm = jnp.where(masks[e], q2, 0.0).astype(bf16)
        km = jnp.where(kmasks[e], k2f, 0.0).astype(bf16)
        dym = jnp.where(masks[e], dy2, 0.0).astype(bf16)
        pe = jnp.exp(_dot_nt(qm, k2) + bias - lse2[:, c0:c0 + 1])
        ds = (pe * (_dot_nt(dym, v2) - delta2[:, c0:c0 + 1])).astype(bf16)
        dv2 = dv2 + _dot_tn(pe.astype(bf16), dym)
        dq2 = dq2 + _dot(ds, km)
        dk2 = dk2 + _dot_tn(ds, qm)
    return dq2, dk2, dv2


TOK = 2048


def _key_bias():
    qi = lax.broadcasted_iota(jnp.int32, (ATTN_BLK, 2 * ATTN_BLK), 0)
    kj = lax.broadcasted_iota(jnp.int32, (ATTN_BLK, 2 * ATTN_BLK), 1)
    delta = ATTN_BLK + qi - kj
    seen = (delta >= 0) & (delta <= ATTN_BLK)
    return jnp.where(seen, 0.0, NEG), jnp.where(seen & (kj >= ATTN_BLK), 0.0, NEG)


def _blocks(dil):
    return [(r, TOK // (ATTN_BLK * dil), ATTN_BLK * dil) for r in range(dil)]


def _attn_fwd(qkv, rider=None):
    S = qkv.shape[0]
    nS = S // TOK

    def body(q_ref, kp_ref, kc_ref, vp_ref, vc_ref, y_ref, l_ref, qs, k2, v2, ay, al):
        n = pl.program_id(1)
        qs[...] = q_ref[...].astype(f32)
        k2[0:TOK] = kp_ref[...].astype(f32)
        k2[TOK:2 * TOK] = kc_ref[...].astype(f32)
        v2[0:TOK] = vp_ref[...].astype(f32)
        v2[TOK:2 * TOK] = vc_ref[...].astype(f32)
        bias_any, bias_first = _key_bias()

        def block(dil, r, b, step, last):
            start = r + (pl.multiple_of(step * b, step) if step < TOK else 0)
            rows = pl.ds(start, ATTN_BLK, stride=dil) if dil > 1 else pl.ds(start, ATTN_BLK)
            keys = (pl.ds(TOK + start - step, 2 * ATTN_BLK, stride=dil) if dil > 1
                    else pl.ds(TOK + start - step, 2 * ATTN_BLK))
            bias = jnp.where((n == 0) & (b == 0), bias_first, bias_any)
            out, lse = _pair_fwd(qs[rows, :], k2[keys, :].astype(bf16), v2[keys, :].astype(bf16), bias)
            if dil < DILATIONS[-1]:
                out, lse = _merge(ay[rows, :], al[rows, :], out, lse)
            if last:
                y_ref[rows, :] = out
                l_ref[rows, :] = lse
            else:
                ay[rows, :] = out
                al[rows, :] = lse

        for dil in reversed(DILATIONS):
            for r, nblk, step in _blocks(dil):
                if nblk == 1:
                    block(dil, r, 0, step, dil == 1)
                else:
                    def loop(b, carry, dil=dil, r=r, step=step):
                        block(dil, r, b, step, dil == 1)
                        return carry
                    lax.fori_loop(0, nblk, loop, 0, unroll=2)

    blk = (TOK, 128)
    cur = lambda c: pl.BlockSpec(blk, lambda p, n: (n, 4 * c + p))
    prv = lambda c: pl.BlockSpec(blk, lambda p, n: (jnp.maximum(n - 1, 0), 4 * c + p))
    out = pl.BlockSpec(blk, lambda p, n: (n, p))
    call = dict(in_specs=[cur(0), prv(1), cur(1), prv(2), cur(2)], out_specs=[out, out],
                out_shape=[jax.ShapeDtypeStruct((S, ATTN_W), f32)] * 2,
                scratch_shapes=[pltpu.VMEM(blk, f32), pltpu.VMEM((2 * TOK, 128), f32), pltpu.VMEM((2 * TOK, 128), f32),
                                pltpu.VMEM(blk, f32), pltpu.VMEM(blk, f32)])
    call, body, more = _ride(call, rider, body, lambda: pl.program_id(0) * nS + pl.program_id(1), (ATTN_W // 128) * nS, 5, 2, 5)
    return pl.pallas_call(body, name="attention_fwd", grid=(ATTN_W // 128, nS), compiler_params=_cp("arbitrary", "arbitrary"),
                          **call)(qkv, qkv, qkv, qkv, qkv, *more)


def _attn_bwd(qkv, ya, lse, dmix):
    S = qkv.shape[0]
    nS = S // TOK

    def body(q_ref, kp_ref, kc_ref, vp_ref, vc_ref, y_ref, l_ref, dy_ref, dq_ref, dk_ref, dv_ref, qs, k2, v2, dk2, dv2, dqa, dl):
        n = pl.program_id(1)

        @pl.when(n == 0)
        def _():
            dk2[...] = jnp.zeros_like(dk2)
            dv2[...] = jnp.zeros_like(dv2)

        @pl.when(n < nS)
        def _():
            qs[...] = q_ref[...].astype(f32)
            k2[0:TOK] = kp_ref[...].astype(f32)
            k2[TOK:2 * TOK] = kc_ref[...].astype(f32)
            v2[0:TOK] = vp_ref[...].astype(f32)
            v2[TOK:2 * TOK] = vc_ref[...].astype(f32)
            li = lax.broadcasted_iota(jnp.int32, (128, 128), 0)
            lj = lax.broadcasted_iota(jnp.int32, (128, 128), 1)
            seg = jnp.where((li // HEAD_DIM) == (lj // HEAD_DIM), 1.0, 0.0).astype(bf16)
            bias_any, bias_first = _key_bias()

            def delta_rows(t, carry):
                rows = pl.ds(pl.multiple_of(256 * t, 256), 256)
                dyy = dy_ref[rows, :] * y_ref[rows, :]
                hi = dyy.astype(bf16)
                dl[rows, :] = _dot(hi, seg) + _dot((dyy - hi.astype(f32)).astype(bf16), seg)
                return carry

            lax.fori_loop(0, TOK // 256, delta_rows, 0)

            def block(dil, r, b, step, first_pattern, last):
                start = r + (pl.multiple_of(step * b, step) if step < TOK else 0)
                rows = pl.ds(start, ATTN_BLK, stride=dil) if dil > 1 else pl.ds(start, ATTN_BLK)
                keys = (pl.ds(TOK + start - step, 2 * ATTN_BLK, stride=dil) if dil > 1
                        else pl.ds(TOK + start - step, 2 * ATTN_BLK))
                bias = jnp.where((n == 0) & (b == 0), bias_first, bias_any)
                dq2, dkk, dvv = _pair_bwd(qs[rows, :], k2[keys, :], v2[keys, :].astype(bf16), dy_ref[rows, :],
                                          l_ref[rows, :], dl[rows, :], bias)
                if last:
                    dq_ref[rows, :] = dqa[rows, :] + dq2
                elif first_pattern:
                    dqa[rows, :] = dq2
                else:
                    dqa[rows, :] += dq2
                dk2[keys, :] += dkk
                dv2[keys, :] += dvv

            for dil in reversed(DILATIONS):
                for r, nblk, step in _blocks(dil):
                    if nblk == 1:
                        block(dil, r, 0, step, dil == DILATIONS[-1], dil == 1)
                    else:
                        def loop(b, carry, dil=dil, r=r, step=step):
                            block(dil, r, b, step, dil == DILATIONS[-1], dil == 1)
                            return carry
                        lax.fori_loop(0, nblk, loop, 0, unroll=2)

        dk_ref[...] = dk2[0:TOK]
        dv_ref[...] = dv2[0:TOK]
        dk2[0:TOK] = dk2[TOK:2 * TOK]
        dv2[0:TOK] = dv2[TOK:2 * TOK]
        dk2[TOK:2 * TOK] = jnp.zeros((TOK, 128), f32)
        dv2[TOK:2 * TOK] = jnp.zeros((TOK, 128), f32)

    blk = (TOK, 128)
    cn = lambda n: jnp.minimum(n, nS - 1)
    pn = lambda n: jnp.clip(n - 1, 0, nS - 1)
    cur = lambda c: pl.BlockSpec(blk, lambda p, n: (cn(n), 4 * c + p))
    prv = lambda c: pl.BlockSpec(blk, lambda p, n: (pn(n), 4 * c + p))
    at_n = pl.BlockSpec(blk, lambda p, n: (cn(n), p))
    at_p = pl.BlockSpec(blk, lambda p, n: (pn(n), p))
    big = lambda: pltpu.VMEM((2 * TOK, 128), f32)
    return pl.pallas_call(
        body, name="attention_bwd", grid=(ATTN_W // 128, nS + 1),
        in_specs=[cur(0), prv(1), cur(1), prv(2), cur(2), at_n, at_n, at_n], out_specs=[at_n, at_p, at_p],
        out_shape=[jax.ShapeDtypeStruct((S, ATTN_W), f32)] * 3,
        scratch_shapes=[pltpu.VMEM(blk, f32), big(), big(), big(), big(), pltpu.VMEM(blk, f32), pltpu.VMEM(blk, f32)],
        compiler_params=_cp("arbitrary", "arbitrary"),
    )(qkv, qkv, qkv, qkv, qkv, ya, lse, dmix)


HG_T = 256
N_HH = HGRN_W // HGRN_HD
HG_SUB = 128
SAFE_RANGE = 80.0


def _row_in_chunk():
    return lax.broadcasted_iota(jnp.int32, (HG_T, HGRN_HD), 0) % CHUNK


def _chunk_cumsum(v, rc):
    for k in (1, 2, 4, 8):
        v = v + jnp.where(rc >= k, pltpu.roll(v, k, 0), 0.0)
    return v


def _chunk_rcumsum(v, rc):
    for k in (1, 2, 4, 8):
        v = v + jnp.where(rc < CHUNK - k, pltpu.roll(v, HG_T - k, 0), 0.0)
    return v


def _hgrn_gates(qb, fb, lb):
    sf = _sigmoid(fb)
    f = lb + (1.0 - lb) * sf
    sq = _sigmoid(qb)
    return sf, f, jnp.log(f), 1.0 - f, sq, qb * sq


def _hgrn_prep(qb, fb, lbl2, rc):
    lb = _sigmoid(lbl2[0:1, :] - lbl2[1:2, :])
    sf, f, lf, key, sq, qf = _hgrn_gates(qb, fb, lb)
    b = _chunk_cumsum(lf, rc)
    rem = _chunk_rcumsum(lf, rc) - lf
    return dict(lb=lb, sf=sf, f=f, key=key, sq=sq, qf=qf, b=b, rem=rem, eb=jnp.exp(b), er=jnp.exp(rem))


def _chunk_mask():
    r = lax.broadcasted_iota(jnp.int32, (HG_SUB, HG_SUB), 0)
    c = lax.broadcasted_iota(jnp.int32, (HG_SUB, HG_SUB), 1)
    return ((r // CHUNK) == (c // CHUNK)) & (c <= r)


def _hgrn_fwd(hp, lbl, wn):
    S = hp.shape[0]
    nT = S // HG_T

    def body(qb_ref, fb_ref, ib_ref, gb_ref, lbl_ref, wn_ref, yb_ref, o_ref, st_ref, ST, qt_s, kh_s, dec_s, oi_s):
        @pl.when(pl.program_id(0) == 0)
        def _():
            ST[...] = jnp.zeros_like(ST)

        rc = _row_in_chunk()
        for h in range(N_HH):
            sl = slice(HGRN_HD * h, HGRN_HD * (h + 1))
            p = _hgrn_prep(qb_ref[:, sl], fb_ref[:, sl], lbl_ref[:, sl], rc)
            qf, key, b = p["qf"], p["key"], p["b"]
            qt = qf * p["eb"]
            qt_s[:, sl] = qt.astype(bf16)
            kh_s[:, sl] = (key * p["er"]).astype(bf16)
            dec_s[:, sl] = jnp.exp(b + p["rem"])
            rng = jnp.max(-(b + p["rem"]))

            @pl.when(rng < SAFE_RANGE)
            def _():
                kp = (key * jnp.exp(-b)).astype(bf16)
                cmask = _chunk_mask()
                for j in range(HG_T // HG_SUB):
                    rs = slice(HG_SUB * j, HG_SUB * (j + 1))
                    sc = jnp.where(cmask, _dot_nt(qt[rs].astype(bf16), kp[rs]), 0.0).astype(bf16)
                    oi_s[rs, sl] = _dot(sc, ib_ref[rs, sl].astype(bf16))

            @pl.when(rng >= SAFE_RANGE)
            def _():
                v = ib_ref[:, sl]
                ones = jnp.ones((HGRN_HD, HGRN_HD), bf16)
                o = jnp.zeros((HG_T, HGRN_HD), f32)
                for l in range(CHUNK):
                    if l == 0:
                        pr, vs = qf * key, v
                    else:
                        e = jnp.exp(jnp.where(rc >= l, b - pltpu.roll(b, l, 0), NEG))
                        pr, vs = qf * pltpu.roll(key, l, 0) * e, pltpu.roll(v, l, 0)
                    o = o + _dot(pr.astype(bf16), ones) * vs
                oi_s[:, sl] = o

        def step(c, carry):
            rows = pl.ds(pl.multiple_of(c * CHUNK, CHUNK), CHUNK)
            row0 = pl.ds(pl.multiple_of(c * CHUNK, CHUNK), 1)
            for h in range(N_HH):
                sl = slice(HGRN_HD * h, HGRN_HD * (h + 1))
                stv = ST[h]
                st_ref[c, sl, :] = stv
                oi_s[rows, sl] += _dot_nt(qt_s[rows, sl], stv.astype(bf16))
                ST[h] = stv * dec_s[row0, sl] + _dot_tn(ib_ref[rows, sl].astype(bf16), kh_s[rows, sl])
            return carry

        lax.fori_loop(0, HG_T // CHUNK, step, 0, unroll=8)

        for h in range(N_HH):
            sl = slice(HGRN_HD * h, HGRN_HD * (h + 1))
            o = oi_s[:, sl]
            o_ref[:, sl] = o
            on = o * lax.rsqrt(jnp.mean(o * o, axis=-1, keepdims=True) + EPS)
            g = gb_ref[:, sl]
            yb_ref[:, sl] = on * wn_ref[:, sl] * (g * _sigmoid(g))

    col = lambda c: pl.BlockSpec((HG_T, HGRN_W), lambda i: (i, c))
    tile = pl.BlockSpec((HG_T, HGRN_W), lambda i: (i, 0))
    whole = lambda a: pl.BlockSpec(a.shape, lambda i: (0, 0))
    return pl.pallas_call(
        body, name="hgrn_fwd", grid=(nT,),
        in_specs=[col(0), col(1), col(2), col(3), whole(lbl), whole(wn)],
        out_specs=[tile, tile, pl.BlockSpec((HG_T // CHUNK, HGRN_W, HGRN_HD), lambda i: (i, 0, 0))],
        out_shape=[jax.ShapeDtypeStruct((S, HGRN_W), f32), jax.ShapeDtypeStruct((S, HGRN_W), f32),
                   jax.ShapeDtypeStruct((S // CHUNK, HGRN_W, HGRN_HD), f32)],
        scratch_shapes=[pltpu.VMEM((N_HH, HGRN_HD, HGRN_HD), f32), pltpu.VMEM((HG_T, HGRN_W), bf16),
                        pltpu.VMEM((HG_T, HGRN_W), bf16), pltpu.VMEM((HG_T, HGRN_W), f32), pltpu.VMEM((HG_T, HGRN_W), f32)],
        compiler_params=_cp("arbitrary"),
    )(hp, hp, hp, hp, lbl, wn)


def _hgrn_bwd(hp, lbl, wn, o_sav, states, dmix, rider=None):
    S = hp.shape[0]
    nT = S // HG_T

    def body(qb_ref, fb_ref, ib_ref, gb_ref, lbl_ref, wn_ref, o_ref, st_ref, dy_ref,
             dq_ref, df_ref, di_ref, dg_ref, gwn_ref, glb_ref,
             DST, qt_s, kh_s, dec_s, do_s, dqt_s, dkh_s, dbl_s, dvi_s, dqi_s, dki_s, dbi_s):
        @pl.when(pl.program_id(0) == 0)
        def _():
            DST[...] = jnp.zeros_like(DST)
            gwn_ref[...] = jnp.zeros_like(gwn_ref)
            glb_ref[...] = jnp.zeros_like(glb_ref)

        rc = _row_in_chunk()
        for h in range(N_HH):
            sl = slice(HGRN_HD * h, HGRN_HD * (h + 1))
            p = _hgrn_prep(qb_ref[:, sl], fb_ref[:, sl], lbl_ref[:, sl], rc)
            qf, key, b = p["qf"], p["key"], p["b"]
            v = ib_ref[:, sl]
            o = o_ref[:, sl]
            rinv = lax.rsqrt(jnp.mean(o * o, axis=-1, keepdims=True) + EPS)
            on = o * rinv
            g = gb_ref[:, sl]
            sgm = _sigmoid(g)
            silu_g = g * sgm
            dy = dy_ref[:, sl]
            wn_v = wn_ref[:, sl]
            gwn_ref[:, sl] += jnp.sum(dy * on * silu_g, axis=0, keepdims=True)
            dg_ref[:, sl] = (dy * on * wn_v * (sgm * (1.0 + g * (1.0 - sgm)))).astype(bf16)
            t1 = dy * wn_v * silu_g
            do = rinv * (t1 - on * jnp.mean(t1 * on, axis=-1, keepdims=True))
            do_s[:, sl] = do.astype(bf16)
            qt = qf * p["eb"]
            qt_s[:, sl] = qt.astype(bf16)
            kh_s[:, sl] = (key * p["er"]).astype(bf16)
            dec_s[:, sl] = jnp.exp(b + p["rem"])
            rng = jnp.max(-(b + p["rem"]))

            @pl.when(rng < SAFE_RANGE)
            def _():
                einv = jnp.exp(-b)
                kp = (key * einv).astype(bf16)
                cmask = _chunk_mask()
                for j in range(HG_T // HG_SUB):
                    rs = slice(HG_SUB * j, HG_SUB * (j + 1))
                    qtb, dob, vb = qt[rs].astype(bf16), do[rs].astype(bf16), v[rs].astype(bf16)
                    sc = jnp.where(cmask, _dot_nt(qtb, kp[rs]), 0.0).astype(bf16)
                    dsc = jnp.where(cmask, _dot_nt(dob, vb), 0.0).astype(bf16)
                    dqp = _dot(dsc, kp[rs])
                    dkp = _dot_tn(dsc, qtb)
                    dvi_s[rs, sl] = _dot_tn(sc, dob)
                    dqi_s[rs, sl] = dqp * p["eb"][rs]
                    dki_s[rs, sl] = dkp * einv[rs]
                    dbi_s[rs, sl] = dqp * qtb.astype(f32) - dkp * kp[rs].astype(f32)

            @pl.when(rng >= SAFE_RANGE)
            def _():
                ones = jnp.ones((HGRN_HD, HGRN_HD), bf16)
                dqf = jnp.zeros((HG_T, HGRN_HD), f32)
                dkey, db, dv = dqf, dqf, dqf
                for l in range(CHUNK):
                    if l == 0:
                        ks, vs, qe = key, v, qf
                    else:
                        e = jnp.exp(jnp.where(rc >= l, b - pltpu.roll(b, l, 0), NEG))
                        ks, vs, qe = pltpu.roll(key, l, 0), pltpu.roll(v, l, 0), qf * e
                    pr = qe * ks
                    rl = _dot(pr.astype(bf16), ones)
                    drl = _dot((do * vs).astype(bf16), ones)
                    if l == 0:
                        dqf = dqf + drl * ks
                        dv = dv + rl * do
                        dkey = dkey + drl * qe
                    else:
                        drl = jnp.where(rc >= l, drl, 0.0)
                        gl = drl * pr
                        dqf = dqf + drl * ks * e
                        dv = dv + pltpu.roll(rl * do, HG_T - l, 0)
                        dkey = dkey + pltpu.roll(drl * qe, HG_T - l, 0)
                        db = db + gl - pltpu.roll(gl, HG_T - l, 0)
                dvi_s[:, sl] = dv
                dqi_s[:, sl] = dqf
                dki_s[:, sl] = dkey
                dbi_s[:, sl] = db

        def step(k, carry):
            c = HG_T // CHUNK - 1 - k
            rows = pl.ds(pl.multiple_of(c * CHUNK, CHUNK), CHUNK)
            row0 = pl.ds(pl.multiple_of(c * CHUNK, CHUNK), 1)
            for h in range(N_HH):
                sl = slice(HGRN_HD * h, HGRN_HD * (h + 1))
                stp = st_ref[c, sl, :]
                dst = DST[h]
                dstb = dst.astype(bf16)
                dob = do_s[rows, sl]
                khb = kh_s[rows, sl]
                dec = dec_s[row0, sl]
                dqt_s[rows, sl] = _dot(dob, stp.astype(bf16))
                dkh = _dot(ib_ref[rows, sl].astype(bf16), dstb)
                dkh_s[rows, sl] = dkh
                dvi_s[rows, sl] += _dot_nt(khb, dstb)
                dbl = jnp.sum(dst * stp, axis=0, keepdims=True) * dec + jnp.sum(dkh * khb.astype(f32), axis=0, keepdims=True)
                dbl_s[rows, sl] = jnp.broadcast_to(dbl, (CHUNK, HGRN_HD))
                DST[h] = dst * dec + _dot_tn(dob, qt_s[rows, sl])
            return carry

        lax.fori_loop(0, HG_T // CHUNK, step, 0, unroll=4)

        for h in range(N_HH):
            sl = slice(HGRN_HD * h, HGRN_HD * (h + 1))
            qb = qb_ref[:, sl]
            p = _hgrn_prep(qb, fb_ref[:, sl], lbl_ref[:, sl], rc)
            sf, sq, lb = p["sf"], p["sq"], p["lb"]
            dqt, dkh = dqt_s[:, sl], dkh_s[:, sl]
            dqf = dqt * p["eb"] + dqi_s[:, sl]
            dkey = dkh * p["er"] + dki_s[:, sl]
            db = dqt * (p["qf"] * p["eb"]) - dkh * (p["key"] * p["er"]) + jnp.where(rc == CHUNK - 1, dbl_s[:, sl], 0.0) + dbi_s[:, sl]
            df = _chunk_rcumsum(db, rc) / p["f"] - dkey
            df_ref[:, sl] = (df * (1.0 - lb) * sf * (1.0 - sf)).astype(bf16)
            glb_ref[:, sl] += jnp.sum(df * (1.0 - sf), axis=0, keepdims=True)
            dq_ref[:, sl] = (dqf * (sq * (1.0 + qb * (1.0 - sq)))).astype(bf16)
            di_ref[:, sl] = dvi_s[:, sl].astype(bf16)

    rev = lambda i: nT - 1 - i
    col = lambda c: pl.BlockSpec((HG_T, HGRN_W), lambda i: (rev(i), c))
    tile = pl.BlockSpec((HG_T, HGRN_W), lambda i: (rev(i), 0))
    whole = lambda a: pl.BlockSpec(a.shape, lambda i: (0, 0))
    vec = pl.BlockSpec((1, HGRN_W), lambda i: (0, 0))
    tb = lambda: pltpu.VMEM((HG_T, HGRN_W), bf16)
    tf = lambda: pltpu.VMEM((HG_T, HGRN_W), f32)
    call = dict(in_specs=[col(0), col(1), col(2), col(3), whole(lbl), whole(wn), tile,
                          pl.BlockSpec((HG_T // CHUNK, HGRN_W, HGRN_HD), lambda i: (rev(i), 0, 0)),
                          pl.BlockSpec((HG_T, HGRN_W), lambda i: (rev(i), 1))],
                out_specs=[tile, tile, tile, tile, vec, vec],
                out_shape=[jax.ShapeDtypeStruct((S, HGRN_W), bf16)] * 4 + [jax.ShapeDtypeStruct((1, HGRN_W), f32)] * 2,
                scratch_shapes=[pltpu.VMEM((N_HH, HGRN_HD, HGRN_HD), f32), tb(), tb(), tf(), tb(), tf(), tf(), tf(), tf(), tf(),
                                tf(), tf()])
    call, body, more = _ride(call, rider, body, lambda: pl.program_id(0), nT, 9, 6, 12)
    return pl.pallas_call(body, name="hgrn_bwd", grid=(nT,), compiler_params=_cp("arbitrary"), **call)(
        hp, hp, hp, hp, lbl, wn, o_sav, states, dmix, *more)


def _out_proj(x, ya, yb, wout, w2):
    S = x.shape[0]
    tm = 512

    def body(x_ref, ya_ref, yb_ref, w_ref, w2_ref, h1_ref, u2_ref, mix_ref):
        mixed = jnp.concatenate([ya_ref[...], yb_ref[...]], axis=1).astype(bf16)
        mix_ref[...] = mixed
        h1 = x_ref[...] + _dot(mixed, w_ref[...])
        h1_ref[...] = h1
        r = lax.rsqrt(jnp.mean(h1 * h1, axis=-1, keepdims=True) + EPS)
        u2_ref[...] = (h1 * r * w2_ref[...]).astype(bf16)

    row = lambda w: pl.BlockSpec((tm, w), lambda i: (i, 0))
    return pl.pallas_call(
        body, name="out_proj", grid=(S // tm,),
        in_specs=[row(D_MODEL), row(ATTN_W), row(HGRN_W), pl.BlockSpec((D_MODEL, D_MODEL), lambda i: (0, 0)),
                  pl.BlockSpec((1, D_MODEL), lambda i: (0, 0))],
        out_specs=[row(D_MODEL), row(D_MODEL), row(D_MODEL)],
        out_shape=[jax.ShapeDtypeStruct((S, D_MODEL), f32), jax.ShapeDtypeStruct((S, D_MODEL), bf16),
                   jax.ShapeDtypeStruct((S, D_MODEL), bf16)],
        compiler_params=_cp("arbitrary"),
    )(x, ya, yb, wout, w2)


def _gate_up(u2, wgu):
    S = u2.shape[0]
    tm, tn = 512, 1408
    nj = FFN // tn

    def body(u_ref, wg_ref, wu_ref, g_ref, up_ref, a_ref):
        u = u_ref[...]
        g = _dot(u, wg_ref[...])
        up = _dot(u, wu_ref[...])
        g_ref[...] = g.astype(bf16)
        up_ref[...] = up.astype(bf16)
        a_ref[...] = (g * _sigmoid(g) * up).astype(bf16)

    out = pl.BlockSpec((tm, tn), lambda j, i: (i, j))
    return pl.pallas_call(
        body, name="gate_up", grid=(nj, S // tm),
        in_specs=[pl.BlockSpec((tm, D_MODEL), lambda j, i: (i, 0)), pl.BlockSpec((D_MODEL, tn), lambda j, i: (0, j)),
                  pl.BlockSpec((D_MODEL, tn), lambda j, i: (0, j + nj))],
        out_specs=[out, out, out],
        out_shape=[jax.ShapeDtypeStruct((S, FFN), bf16)] * 3,
        compiler_params=_cp("arbitrary", "arbitrary"),
    )(u2, wgu, wgu)


def _rms_bwd(dyw, hn, r):
    return r * (dyw - hn * jnp.mean(dyw * hn, axis=-1, keepdims=True))


def _down_loss(act, wdown, h1, tgt, w3):
    S = act.shape[0]
    tm = 256

    def body(a_ref, w_ref, h1_ref, t_ref, w3_ref, dh2_ref, loss_ref, gw3_ref):
        @pl.when(pl.program_id(0) == 0)
        def _():
            loss_ref[...] = jnp.zeros_like(loss_ref)
            gw3_ref[...] = jnp.zeros_like(gw3_ref)

        h2 = h1_ref[...] + _dot(a_ref[...], w_ref[...])
        r = lax.rsqrt(jnp.mean(h2 * h2, axis=-1, keepdims=True) + EPS)
        hn = h2 * r
        w3 = w3_ref[...]
        err = hn * w3 - t_ref[...]
        loss_ref[...] += (0.5 / D_MODEL) * jnp.sum(err * err)
        dy = err * (1.0 / D_MODEL)
        gw3_ref[...] += jnp.sum(dy * hn, axis=0, keepdims=True)
        dh2_ref[...] = _rms_bwd(dy * w3, hn, r)

    row = lambda w: pl.BlockSpec((tm, w), lambda i: (i, 0))
    return pl.pallas_call(
        body, name="down_loss", grid=(S // tm,),
        in_specs=[row(FFN), pl.BlockSpec((FFN, D_MODEL), lambda i: (0, 0)), row(D_MODEL), row(D_MODEL),
                  pl.BlockSpec((1, D_MODEL), lambda i: (0, 0))],
        out_specs=[row(D_MODEL), pl.BlockSpec((1, 128), lambda i: (0, 0)), pl.BlockSpec((1, D_MODEL), lambda i: (0, 0))],
        out_shape=[jax.ShapeDtypeStruct((S, D_MODEL), f32), jax.ShapeDtypeStruct((1, 128), f32),
                   jax.ShapeDtypeStruct((1, D_MODEL), f32)],
        compiler_params=_cp("arbitrary"),
    )(act, wdown, h1, tgt, w3)


def _dact(dh2, wdown, gate, up):
    S = dh2.shape[0]
    tm = 256

    def body(d_ref, w_ref, g_ref, u_ref, dg_ref, du_ref):
        da = _dot_nt(d_ref[...].astype(bf16), w_ref[...])
        g = g_ref[...].astype(f32)
        sg = _sigmoid(g)
        du_ref[...] = (da * g * sg).astype(bf16)
        dg_ref[...] = (da * u_ref[...].astype(f32) * (sg * (1.0 + g * (1.0 - sg)))).astype(bf16)

    row = lambda w: pl.BlockSpec((tm, w), lambda i: (i, 0))
    return pl.pallas_call(
        body, name="dact", grid=(S // tm,),
        in_specs=[row(D_MODEL), pl.BlockSpec((FFN, D_MODEL), lambda i: (0, 0)), row(FFN), row(FFN)],
        out_specs=[row(FFN), row(FFN)],
        out_shape=[jax.ShapeDtypeStruct((S, FFN), bf16)] * 2,
        compiler_params=_cp("arbitrary"),
    )(dh2, wdown, gate, up)


def _dgu(dgate, dup, wgu, h1, w2, dh2, rider=None):
    S = dgate.shape[0]
    tm = 256

    def body(dg_ref, du_ref, wg_ref, wu_ref, h1_ref, w2_ref, dh2_ref, dh1_ref, gw2_ref):
        @pl.when(pl.program_id(0) == 0)
        def _():
            gw2_ref[...] = jnp.zeros_like(gw2_ref)

        du2 = _dot_nt(dg_ref[...], wg_ref[...]) + _dot_nt(du_ref[...], wu_ref[...])
        h1 = h1_ref[...]
        r = lax.rsqrt(jnp.mean(h1 * h1, axis=-1, keepdims=True) + EPS)
        hn = h1 * r
        gw2_ref[...] += jnp.sum(du2 * hn, axis=0, keepdims=True)
        dh1_ref[...] = dh2_ref[...] + _rms_bwd(du2 * w2_ref[...], hn, r)

    row = lambda w: pl.BlockSpec((tm, w), lambda i: (i, 0))
    call = dict(in_specs=[row(FFN), row(FFN), pl.BlockSpec((D_MODEL, FFN), lambda i: (0, 0)),
                          pl.BlockSpec((D_MODEL, FFN), lambda i: (0, 1)), row(D_MODEL),
                          pl.BlockSpec((1, D_MODEL), lambda i: (0, 0)), row(D_MODEL)],
                out_specs=[row(D_MODEL), pl.BlockSpec((1, D_MODEL), lambda i: (0, 0))],
                out_shape=[jax.ShapeDtypeStruct((S, D_MODEL), f32), jax.ShapeDtypeStruct((1, D_MODEL), f32)], scratch_shapes=[])
    call, body, more = _ride(call, rider, body, lambda: pl.program_id(0), S // tm, 7, 2, 0)
    return pl.pallas_call(body, name="dgu", grid=(S // tm,), compiler_params=_cp("arbitrary"), **call)(
        dgate, dup, wgu, wgu, h1, w2, dh2, *more)


def _dmixed(dh1, wout):
    S = dh1.shape[0]
    tm = 512

    def body(d_ref, w_ref, o_ref):
        o_ref[...] = _dot_nt(d_ref[...].astype(bf16), w_ref[...])

    row = pl.BlockSpec((tm, D_MODEL), lambda i: (i, 0))
    return pl.pallas_call(
        body, name="dmixed", grid=(S // tm,),
        in_specs=[row, pl.BlockSpec((D_MODEL, D_MODEL), lambda i: (0, 0))], out_specs=row,
        out_shape=jax.ShapeDtypeStruct((S, D_MODEL), f32), compiler_params=_cp("arbitrary"),
    )(dh1, wout)


def _din(dq, dk, dv, dhq, dhf, dhi, dhg, cos_t, sg_t, win, x, w1, dh1):
    S = x.shape[0]
    tm = 256

    def body(dq_ref, dk_ref, dv_ref, dhq_ref, dhf_ref, dhi_ref, dhg_ref, cos_ref, sg_ref, w_ref, x_ref, w1_ref, dh1_ref,
             dp_ref, gx_ref, gw1_ref):
        @pl.when(pl.program_id(0) == 0)
        def _():
            gw1_ref[...] = jnp.zeros_like(gw1_ref)

        cosv, sgv = jnp.tile(cos_ref[...], (1, ATTN_W // 128)), jnp.tile(sg_ref[...], (1, ATTN_W // 128))
        unrope = lambda d: d * cosv - sgv * _swap_halves(d)
        parts = [(unrope(dq_ref[...]) * (HEAD_DIM ** -0.5)).astype(bf16), unrope(dk_ref[...]).astype(bf16),
                 dv_ref[...].astype(bf16), dhq_ref[...], dhf_ref[...], dhi_ref[...], dhg_ref[...]]
        du = jnp.zeros((tm, D_MODEL), f32)
        for j, pj in enumerate(parts):
            dp_ref[:, j * 512:(j + 1) * 512] = pj
            du = du + _dot_nt(pj, w_ref[:, j * 512:(j + 1) * 512])
        xv = x_ref[...]
        r = lax.rsqrt(jnp.mean(xv * xv, axis=-1, keepdims=True) + EPS)
        xn = xv * r
        gw1_ref[...] += jnp.sum(du * xn, axis=0, keepdims=True)
        gx_ref[...] = dh1_ref[...] + _rms_bwd(du * w1_ref[...], xn, r)

    row = lambda w: pl.BlockSpec((tm, w), lambda i: (i, 0))
    vec = pl.BlockSpec((1, D_MODEL), lambda i: (0, 0))
    return pl.pallas_call(
        body, name="din", grid=(S // tm,),
        in_specs=[row(512)] * 7 + [row(128), row(128), pl.BlockSpec((D_MODEL, IN_W), lambda i: (0, 0)), row(D_MODEL), vec,
                                   row(D_MODEL)],
        out_specs=[row(IN_W), row(D_MODEL), vec],
        out_shape=[jax.ShapeDtypeStruct((S, IN_W), bf16), jax.ShapeDtypeStruct((S, D_MODEL), f32),
                   jax.ShapeDtypeStruct((1, D_MODEL), f32)],
        compiler_params=_cp("arbitrary"),
    )(dq, dk, dv, dhq, dhf, dhi, dhg, cos_t, sg_t, win, x, w1, dh1)


def _gw(a, b, tn, name):
    S, M = a.shape
    N = b.shape[1]
    ts = 512

    def body(a_ref, b_ref, o_ref):
        @pl.when(pl.program_id(1) == 0)
        def _():
            o_ref[...] = jnp.zeros_like(o_ref)

        o_ref[...] += _dot_tn(a_ref[...].astype(bf16), b_ref[...].astype(bf16))

    return pl.pallas_call(
        body, name=name, grid=(N // tn, S // ts),
        in_specs=[pl.BlockSpec((ts, M), lambda j, s: (s, 0)), pl.BlockSpec((ts, tn), lambda j, s: (s, j))],
        out_specs=pl.BlockSpec((M, tn), lambda j, s: (0, j)), out_shape=jax.ShapeDtypeStruct((M, N), f32),
        compiler_params=_cp("arbitrary", "arbitrary"),
    )(a, b)


MESH = pl.DeviceIdType.MESH
ANY = pl.BlockSpec(memory_space=pl.ANY)
VMEM_SPEC = pl.BlockSpec(memory_space=pltpu.VMEM)


def _pos():
    return lax.axis_index("x"), lax.axis_index("y"), lax.axis_index("c")


def _flip(v, bit):
    return 1 - v if bit else v


def _gather_rider(shards):
    n = len(shards)

    def parts(outs, scratch):
        send_sems, recv_sems, local_sems = scratch[n:]
        x, y, c = _pos()
        chips = [(1 - x, y), (x, 1 - y), (1 - x, 1 - y)]

        def copy(a, k, block, to, src=None):
            dst = outs[a].at[4 * block[0] + 2 * block[1] + block[2]]
            return pltpu.make_async_remote_copy(src_ref=dst if src is None else src, dst_ref=dst, send_sem=send_sems.at[a, k],
                                                recv_sem=recv_sems.at[a, k], device_id=to, device_id_type=MESH)

        bufs = scratch[:n]
        me, sibling = (x, y, c), (x, y, 1 - c)
        own = lambda a: pltpu.make_async_copy(bufs[a], outs[a].at[4 * x + 2 * y + c], local_sems.at[a])
        sent = lambda a: [copy(a, 0, me, sibling, src=bufs[a])] + [copy(a, 1 + j, me, (*chip, c), src=bufs[a])
                                                                   for j, chip in enumerate(chips)]
        passed = lambda a: [copy(a, 4 + j, (*chip, c), sibling) for j, chip in enumerate(chips)]
        landed = lambda a: [copy(a, 1 + j, (*chip, c), me) for j, chip in enumerate(chips)]
        from_sibling = lambda a: [copy(a, 0, sibling, me)] + [copy(a, 4 + j, (*chip, 1 - c), me) for j, chip in enumerate(chips)]
        return bufs, local_sems, own, sent, passed, landed, from_sibling

    def first(ins, outs, scratch):
        bufs, local_sems, own, sent, _, _, _ = parts(outs, scratch)
        loads = [pltpu.make_async_copy(ins[a], bufs[a], local_sems.at[a]) for a in range(n)]
        for ld in loads:
            ld.start()
        for a in range(n):
            loads[a].wait()
            own(a).start()
            for cp in sent(a):
                cp.start()

    def middle(ins, outs, scratch):
        _, _, _, _, passed, landed, _ = parts(outs, scratch)
        for a in range(n):
            for got, on in zip(landed(a), passed(a)):
                got.wait_recv()
                on.start()

    def last(ins, outs, scratch):
        _, _, own, sent, passed, _, from_sibling = parts(outs, scratch)
        for a in range(n):
            for cp in from_sibling(a):
                cp.wait_recv()
        for a in range(n):
            for cp in sent(a) + passed(a):
                cp.wait_send()
            own(a).wait()

    return _Rider(shards, [jax.ShapeDtypeStruct((N_DEV,) + s.shape, s.dtype) for s in shards],
                  [pltpu.VMEM(s.shape, s.dtype) for s in shards]
                  + [pltpu.SemaphoreType.DMA((n, 7)), pltpu.SemaphoreType.DMA((n, 7)), pltpu.SemaphoreType.DMA((n,))],
                  first, last, middle)


def _sibling_rider(grads):
    n = len(grads)

    def copies(g, got, scratch):
        send_sems, recv_sems = scratch
        x, y, c = _pos()
        return [pltpu.make_async_remote_copy(src_ref=g[a].at[2 * q + (1 - c)], dst_ref=got[a].at[q], send_sem=send_sems.at[a, q],
                                             recv_sem=recv_sems.at[a, q], device_id=(x, y, 1 - c), device_id_type=MESH)
                for a in range(n) for q in range(4)]

    def first(g, got, scratch):
        for cp in copies(g, got, scratch):
            cp.start()

    def last(g, got, scratch):
        for cp in copies(g, got, scratch):
            cp.wait()

    return _Rider(grads, [jax.ShapeDtypeStruct((4,) + g.shape[1:], g.dtype) for g in grads],
                  [pltpu.SemaphoreType.DMA((n, 4))] * 2, first, last)


def _chips_rider(sums):
    n = len(sums)

    def copies(s, out, scratch):
        send_sems, recv_sems = scratch
        x, y, c = _pos()
        cps = []
        for a in range(n):
            for f in (1, 2, 3):
                peer = (_flip(x, f >> 1), _flip(y, f & 1), c)
                cps.append(pltpu.make_async_remote_copy(
                    src_ref=s[a].at[2 * peer[0] + peer[1]], dst_ref=out[a].at[f - 1], send_sem=send_sems.at[a, f - 1],
                    recv_sem=recv_sems.at[a, f - 1], device_id=peer, device_id_type=MESH))
        return cps

    def first(s, out, scratch):
        for cp in copies(s, out, scratch):
            cp.start()

    def last(s, out, scratch):
        for cp in copies(s, out, scratch):
            cp.wait()

    return _Rider(sums, [jax.ShapeDtypeStruct((3,) + s.shape[1:], s.dtype) for s in sums],
                  [pltpu.SemaphoreType.DMA((n, 3))] * 2, first, last)


def _alone(rider, name):
    ri, ro = len(rider.ins), len(rider.out_shapes)

    def body(*refs):
        theirs = (refs[:ri], refs[ri:ri + ro], refs[ri + ro:])
        rider.first(*theirs)
        if rider.middle is not None:
            rider.middle(*theirs)
        rider.last(*theirs)

    return pl.pallas_call(body, name=name, in_specs=[ANY] * ri, out_specs=[ANY] * ro, out_shape=rider.out_shapes,
                          scratch_shapes=rider.scratch)(*rider.ins)


def _gather_small(g_w1, g_w2, g_w3, g_lb, g_wn, loss):
    def body(w1_ref, w2_ref, w3_ref, lb_ref, wn_ref, loss_ref, out_ref, pk, send_sems, recv_sems):
        x, y, c = _pos()
        me = 4 * x + 2 * y + c
        pk[...] = jnp.zeros_like(pk)
        pk[0:1, :] = w1_ref[...]
        pk[1:2, :] = w2_ref[...]
        pk[2:3, :] = w3_ref[...]
        pk[3:4, 0:HGRN_W] = lb_ref[...]
        pk[3:4, HGRN_W:2 * HGRN_W] = wn_ref[...]
        pk[4:5, 0:128] = loss_ref[...]
        out_ref[me] = pk[...]
        sends, recvs = [], []
        for k in range(1, N_DEV):
            peer = (_flip(x, k >> 2), _flip(y, (k >> 1) & 1), _flip(c, k & 1))
            cp = pltpu.make_async_remote_copy(src_ref=pk, dst_ref=out_ref.at[me], send_sem=send_sems.at[k - 1],
                                              recv_sem=recv_sems.at[k - 1], device_id=peer, device_id_type=MESH)
            cp.start()
            sends.append(cp)
            recvs.append(pltpu.make_async_remote_copy(src_ref=pk, dst_ref=out_ref.at[4 * peer[0] + 2 * peer[1] + peer[2]],
                                                      send_sem=send_sems.at[k - 1], recv_sem=recv_sems.at[k - 1], device_id=peer,
                                                      device_id_type=MESH))
        for cp in recvs:
            cp.wait_recv()
        for cp in sends:
            cp.wait_send()

    return pl.pallas_call(
        body, name="gather_small", in_specs=[VMEM_SPEC] * 6, out_specs=VMEM_SPEC,
        out_shape=jax.ShapeDtypeStruct((N_DEV, 8, D_MODEL), f32),
        scratch_shapes=[pltpu.VMEM((8, D_MODEL), f32), pltpu.SemaphoreType.DMA((N_DEV - 1,)), pltpu.SemaphoreType.DMA((N_DEV - 1,))],
    )(g_w1, g_w2, g_w3, g_lb, g_wn, loss)


def _row_tile(r):
    return max(t for t in range(8, 257, 8) if r % t == 0)


def _add_sibling(core, g, got, name):
    _, r, c = got.shape
    tr = _row_tile(r)

    def body(core_ref, a_ref, b_ref, o_ref):
        o_ref[...] = (a_ref[...] + b_ref[...]).astype(bf16)

    blk = pl.BlockSpec((1, tr, c), lambda q, i, core_ref: (q, i, 0))
    return pl.pallas_call(
        body, name=name, out_shape=jax.ShapeDtypeStruct(got.shape, bf16),
        grid_spec=pltpu.PrefetchScalarGridSpec(
            num_scalar_prefetch=1, grid=(4, r // tr),
            in_specs=[pl.BlockSpec((1, tr, c), lambda q, i, core_ref: (2 * q + core_ref[0], i, 0)), blk], out_specs=blk),
        compiler_params=_cp("arbitrary", "arbitrary"))(core, g, got)


def _adamw(w, g, m, v):
    m = ADAM_B1 * m + (1.0 - ADAM_B1) * g
    v = ADAM_B2 * v + (1.0 - ADAM_B2) * (g * g)
    m_hat = m / (1.0 - ADAM_B1 ** ADAM_STEP)
    v_hat = v / (1.0 - ADAM_B2 ** ADAM_STEP)
    return -ADAM_LR * (m_hat / (jnp.sqrt(v_hat) + ADAM_EPS) + ADAM_WD * w), m, v


def _adam_shard(where, g, got, pieces, w, m, v, name):
    r, c = w.shape
    tr = _row_tile(r)

    def body(where_ref, g_ref, got_ref, p_ref, w_ref, m_ref, v_ref, g_out, d_out, m_out, v_out):
        gsum = g_ref[0] + got_ref[0]
        for f in range(3):
            gsum = gsum + p_ref[f].astype(f32)
        g_out[...] = gsum
        d_out[...], m_out[...], v_out[...] = _adamw(w_ref[...], gsum, m_ref[...], v_ref[...])

    blk = pl.BlockSpec((tr, c), lambda i, where_ref: (i, 0))
    return pl.pallas_call(
        body, name=name, out_shape=[jax.ShapeDtypeStruct((r, c), f32)] * 4,
        grid_spec=pltpu.PrefetchScalarGridSpec(
            num_scalar_prefetch=1, grid=(r // tr,),
            in_specs=[pl.BlockSpec((1, tr, c), lambda i, where_ref: (where_ref[0], i, 0)),
                      pl.BlockSpec((1, tr, c), lambda i, where_ref: (where_ref[1], i, 0)),
                      pl.BlockSpec((3, tr, c), lambda i, where_ref: (0, i, 0)), blk, blk, blk],
            out_specs=[blk] * 4),
        compiler_params=_cp("arbitrary"),
    )(where, g, got, pieces, w, m, v)


def _small_update(gath, params):
    def body(gath_ref, *refs):
        ins, outs = refs[:15], refs[15:]
        gs = gath_ref[0]
        for k in range(1, N_DEV):
            gs = gs + gath_ref[k]
        outs[0][...] = gs[4:5, 0:128]
        l0, l1 = ins[9][0:1, :], ins[9][1:2, :]
        lb = _sigmoid(l0 - l1)
        d0 = gs[3:4, 0:HGRN_W] * lb * (1.0 - lb)
        first_row = lax.broadcasted_iota(jnp.int32, (2, HGRN_W), 0) == 0
        grads = [gs[0:1, :], gs[1:2, :], gs[2:3, :], jnp.where(first_row, d0, -d0), gs[3:4, HGRN_W:2 * HGRN_W]]
        for i, g in enumerate(grads):
            w_ref, m_ref, v_ref = ins[3 * i:3 * i + 3]
            o = outs[1 + 4 * i:5 + 4 * i]
            o[0][...] = g
            o[1][...], o[2][...], o[3][...] = _adamw(w_ref[...], g, m_ref[...], v_ref[...])

    flat = [a for p in params for a in p]
    out_shape = [jax.ShapeDtypeStruct((1, 128), f32)] + [jax.ShapeDtypeStruct(p[0].shape, f32) for p in params for _ in range(4)]
    outs = pl.pallas_call(body, name="small_update", in_specs=[VMEM_SPEC] * 16, out_specs=[VMEM_SPEC] * 21, out_shape=out_shape)(gath, *flat)
    return outs[0], [outs[1 + 4 * i:5 + 4 * i] for i in range(5)]


def kernel(x, norm1_w, w_in, lb_logits, hgrn_norm_w, w_out, norm2_w, w_gate_up, w_down, final_norm_w, loss_target, m_norm1_w, m_w_in, m_lb_logits, m_hgrn_norm_w, m_w_out, m_norm2_w, m_w_gate_up, m_w_down, m_final_norm_w, v_norm1_w, v_w_in, v_lb_logits, v_hgrn_norm_w, v_w_out, v_norm2_w, v_w_gate_up, v_w_down, v_final_norm_w):
    row = lambda a: a.reshape(1, D_MODEL)
    by_owner = lambda g, w: jnp.transpose(g.reshape(g.shape[0], g.shape[1] // w, w), (1, 0, 2))
    ix, iy, ic = lax.axis_index("x"), lax.axis_index("y"), lax.axis_index("c")
    core = jnp.stack([ic]).astype(jnp.int32)
    where = jnp.stack([4 * ix + 2 * iy + ic, 2 * ix + iy]).astype(jnp.int32)
    xs, tgt, w3 = x[0], loss_target[0], row(final_norm_w)
    S = xs.shape[0]

    (win_g,) = _alone(_gather_rider([w_in[0].astype(bf16)]), "gather_w_in")
    win = jnp.transpose(win_g, (1, 0, 2)).reshape(D_MODEL, IN_W)
    cos_t, sg_t = _rope_tables(S)
    u, qkv, hp = _in_proj(xs, norm1_w, win, cos_t, sg_t)
    ya, lse, wout_g, wgu_g, wdown_g = _attn_fwd(qkv, _gather_rider([w_out[0].astype(bf16), w_gate_up[0].astype(bf16),
                                                                     w_down[0].astype(bf16)]))
    wout = wout_g.reshape(D_MODEL, D_MODEL)
    wgu = jnp.transpose(wgu_g, (1, 0, 2)).reshape(D_MODEL, 2 * FFN)
    wdown = wdown_g.reshape(FFN, D_MODEL)
    yb, o_sav, states = _hgrn_fwd(hp, lb_logits, hgrn_norm_w)
    h1, u2, mixed = _out_proj(xs, ya, yb, wout, norm2_w)
    gate, up, act = _gate_up(u2, wgu)
    dh2, loss_p, g_w3 = _down_loss(act, wdown, h1, tgt, w3)

    g_wdown = _gw(act, dh2, 512, "gw_down")
    dgate, dup = _dact(dh2, wdown, gate, up)
    g_wgu = (_gw(u2, dgate, 1408, "gw_gate"), _gw(u2, dup, 1408, "gw_up"))
    early = [jnp.concatenate([by_owner(g, 2 * FFN // N_DEV) for g in g_wgu], axis=0), g_wdown.reshape(N_DEV, FFN // N_DEV, D_MODEL)]
    dh1, g_w2, *got_early = _dgu(dgate, dup, wgu, h1, norm2_w, dh2, _sibling_rider(early))
    sums_early = [_add_sibling(core, g, o, f"add_sibling_{i}") for i, (g, o) in enumerate(zip(early, got_early))]
    g_wout = _gw(mixed, dh1, 1024, "gw_out")
    dmix = _dmixed(dh1, wout)
    dhq, dhf, dhi, dhg, g_wn, g_lb, *pieces_early = _hgrn_bwd(hp, lb_logits, hgrn_norm_w, o_sav, states, dmix, _chips_rider(sums_early))
    datt = _attn_bwd(qkv, ya, lse, dmix)
    dproj, gx, g_w1 = _din(*datt, dhq, dhf, dhi, dhg, cos_t, sg_t, win, xs, norm1_w, dh1)
    g_win = _gw(u, dproj, 896, "gw_in")
    late = [by_owner(g_win, IN_W // N_DEV), g_wout.reshape(N_DEV, D_MODEL // N_DEV, D_MODEL)]
    got_late = _alone(_sibling_rider(late), "reduce_sibling")
    sums_late = [_add_sibling(core, g, o, f"add_sibling_{2 + i}") for i, (g, o) in enumerate(zip(late, got_late))]
    pieces_late = _alone(_chips_rider(sums_late), "reduce_chips")

    grads = [late[0], late[1], early[0], early[1]]
    got = [got_late[0], got_late[1], got_early[0], got_early[1]]
    pieces = [pieces_late[0], pieces_late[1], pieces_early[0], pieces_early[1]]
    shards = [w_in[0], w_out[0], w_gate_up[0], w_down[0]]
    moms = [(m_w_in[0], v_w_in[0]), (m_w_out[0], v_w_out[0]), (m_w_gate_up[0], v_w_gate_up[0]), (m_w_down[0], v_w_down[0])]
    big = [_adam_shard(where, g, o, p, w, m, v, f"adam_{i}")
           for i, (g, o, p, w, (m, v)) in enumerate(zip(grads, got, pieces, shards, moms))]
    big = [[a[None] for a in four] for four in big]

    gath = _gather_small(g_w1, g_w2, g_w3, g_lb, g_wn, loss_p)
    params = [(norm1_w, m_norm1_w, v_norm1_w), (norm2_w, m_norm2_w, v_norm2_w),
              (row(final_norm_w), row(m_final_norm_w), row(v_final_norm_w)),
              (lb_logits, m_lb_logits, v_lb_logits), (hgrn_norm_w, m_hgrn_norm_w, v_hgrn_norm_w)]
    loss, (s_w1, s_w2, s_w3, s_lb, s_wn) = _small_update(gath, params)
    s_w3 = [a.reshape(D_MODEL) for a in s_w3]
    per_w = [s_w1, big[0], s_lb, s_wn, big[1], s_w2, big[2], big[3], s_w3]
    return (loss[0, 0], gx[None], *[p[0] for p in per_w], *[p[1] for p in per_w], *[p[2] for p in per_w], *[p[3] for p in per_w])
```

```python
import jax
import jax.numpy as jnp
from jax import lax
from jax.experimental import pallas as pl
from jax.experimental.pallas import tpu as pltpu

f32, bf16 = jnp.float32, jnp.bfloat16

D_MODEL = 1024
ATTN_W = 512
HEAD_DIM = 64
ATTN_BLK = 128
DILATIONS = (1, 4, 16)
HGRN_W = 512
HGRN_HD = 128
CHUNK = 16
IN_W = 3 * ATTN_W + 4 * HGRN_W
FFN = 2816
EPS = 1e-6
ROPE_THETA = 10000.0
NEG = -1e30
N_DEV = 8
ADAM_LR, ADAM_B1, ADAM_B2, ADAM_EPS, ADAM_WD, ADAM_STEP = 0.001, 0.9, 0.999, 1e-08, 0.01, 10
VMEM_LIMIT = 56 * 1024 * 1024


def _cp(*sem):
    return pltpu.CompilerParams(dimension_semantics=sem, vmem_limit_bytes=VMEM_LIMIT)


def _dot(a, b):
    return jnp.dot(a, b, preferred_element_type=f32)


def _dot_nt(a, b):
    return lax.dot_general(a, b, (((1,), (1,)), ((), ())), preferred_element_type=f32)


def _dot_tn(a, b):
    return lax.dot_general(a, b, (((0,), (0,)), ((), ())), preferred_element_type=f32)


def _sigmoid(x):
    return 0.5 * jnp.tanh(0.5 * x) + 0.5


class _Rider:
    def __init__(self, ins, out_shapes, scratch, first, last, middle=None):
        self.ins, self.out_shapes, self.scratch = list(ins), list(out_shapes), list(scratch)
        self.first, self.middle, self.last = first, middle, last


def _ride(call, rider, body, step, n_steps, n_in, n_out, n_scratch):
    if rider is None:
        return call, body, []
    ri, ro = len(rider.ins), len(rider.out_shapes)
    any_spec = pl.BlockSpec(memory_space=pl.ANY)
    call = dict(call, in_specs=call["in_specs"] + [any_spec] * ri, out_specs=call["out_specs"] + [any_spec] * ro,
                out_shape=call["out_shape"] + rider.out_shapes, scratch_shapes=call["scratch_shapes"] + rider.scratch)

    def riding(*refs):
        a = n_in + ri
        b = a + n_out + ro
        mine = refs[:n_in] + refs[a:a + n_out] + refs[b:b + n_scratch]
        theirs = (refs[n_in:a], refs[a + n_out:b], refs[b + n_scratch:])
        t = step()

        @pl.when(t == 0)
        def _():
            rider.first(*theirs)

        body(*mine)
        if rider.middle is not None:
            @pl.when(t == n_steps // 2)
            def _():
                rider.middle(*theirs)

        @pl.when(t == n_steps - 1)
        def _():
            rider.last(*theirs)

    return call, riding, rider.ins


def _rope_tables(S):
    half = HEAD_DIM // 2
    inv_freq = ROPE_THETA ** (-jnp.arange(half, dtype=f32) / half)
    ang = jnp.arange(S, dtype=f32)[:, None] * inv_freq[None, :]
    cos, sin = jnp.cos(ang), jnp.sin(ang)
    return jnp.concatenate([cos, cos, cos, cos], axis=1), jnp.concatenate([-sin, sin, -sin, sin], axis=1)


def _swap_halves(v):
    n = v.shape[1]
    lane = lax.broadcasted_iota(jnp.int32, v.shape, 1)
    return jnp.where((lane % HEAD_DIM) < HEAD_DIM // 2, pltpu.roll(v, n - HEAD_DIM // 2, 1), pltpu.roll(v, HEAD_DIM // 2, 1))


def _in_proj(x, w1, win, cos_t, sg_t):
    S = x.shape[0]
    tm = 256

    def body(x_ref, w1_ref, w_ref, cos_ref, sg_ref, u_ref, qkv_ref, hp_ref):
        xv = x_ref[...]
        r = lax.rsqrt(jnp.mean(xv * xv, axis=-1, keepdims=True) + EPS)
        u = (xv * r * w1_ref[...]).astype(bf16)
        u_ref[...] = u
        cosv, sgv = jnp.tile(cos_ref[...], (1, ATTN_W // 128)), jnp.tile(sg_ref[...], (1, ATTN_W // 128))
        for j in range(3):
            pj = _dot(u, w_ref[:, j * ATTN_W:(j + 1) * ATTN_W])
            if j < 2:
                pj = pj * cosv + _swap_halves(pj) * sgv
            if j == 0:
                pj = pj * (HEAD_DIM ** -0.5)
            qkv_ref[:, j * ATTN_W:(j + 1) * ATTN_W] = pj.astype(bf16)
        for j in range(4):
            lo = 3 * ATTN_W + j * HGRN_W
            hp_ref[:, j * HGRN_W:(j + 1) * HGRN_W] = _dot(u, w_ref[:, lo:lo + HGRN_W])

    return pl.pallas_call(
        body, name="in_proj", grid=(S // tm,),
        in_specs=[pl.BlockSpec((tm, D_MODEL), lambda i: (i, 0)), pl.BlockSpec((1, D_MODEL), lambda i: (0, 0)),
                  pl.BlockSpec((D_MODEL, IN_W), lambda i: (0, 0)),
                  pl.BlockSpec((tm, 128), lambda i: (i, 0)), pl.BlockSpec((tm, 128), lambda i: (i, 0))],
        out_specs=[pl.BlockSpec((tm, D_MODEL), lambda i: (i, 0)), pl.BlockSpec((tm, 3 * ATTN_W), lambda i: (i, 0)),
                   pl.BlockSpec((tm, 4 * HGRN_W), lambda i: (i, 0))],
        out_shape=[jax.ShapeDtypeStruct((S, D_MODEL), bf16), jax.ShapeDtypeStruct((S, 3 * ATTN_W), bf16),
                   jax.ShapeDtypeStruct((S, 4 * HGRN_W), f32)],
        compiler_params=_cp("arbitrary"),
    )(x, w1, win, cos_t, sg_t)


def _head_masks():
    lane = lax.broadcasted_iota(jnp.int32, (ATTN_BLK, 128), 1)
    even = lane < HEAD_DIM
    return even, (even, jnp.logical_not(even))


def _pair_fwd(q2, k2, v2, bias):
    even, masks = _head_masks()
    outs, lses = [], []
    for e in range(2):
        qm = jnp.where(masks[e], q2, 0.0).astype(bf16)
        s = _dot_nt(qm, k2) + bias
        m = jnp.max(s, axis=-1, keepdims=True)
        pe = jnp.exp(s - m)
        lsum = jnp.sum(pe, axis=-1, keepdims=True)
        outs.append(_dot(pe.astype(bf16), v2) / lsum)
        lses.append(jnp.broadcast_to(m + jnp.log(lsum), (ATTN_BLK, 128)))
    return jnp.where(even, outs[0], outs[1]), jnp.where(even, lses[0], lses[1])


def _merge(y0, l0, y1, l1):
    mx = jnp.maximum(l0, l1)
    a, b = jnp.exp(l0 - mx), jnp.exp(l1 - mx)
    tot = a + b
    return (a * y0 + b * y1) / tot, mx + jnp.log(tot)


def _pair_bwd(q2, k2f, v2, dy2, lse2, delta2, bias):
    _, masks = _head_masks()
    k2 = k2f.astype(bf16)
    klane = lax.broadcasted_iota(jnp.int32, (2 * ATTN_BLK, 128), 1) < HEAD_DIM
    kmasks = (klane, jnp.logical_not(klane))
    dq2 = jnp.zeros((ATTN_BLK, 128), f32)
    pes, dss, qms, dyms = [], [], [], []
    for e in range(2):
        c0 = e * HEAD_DIM
        qm = jnp.where(masks[e], q2, 0.0).astype(bf16)
        km = jnp.where(kmasks[e], k2f, 0.0).astype(bf16)
        dym = jnp.where(masks[e], dy2, 0.0).astype(bf16)
        pe = jnp.exp(_dot_nt(qm, k2) + bias - lse2[:, c0:c0 + 1])
        ds = (pe * (_dot_nt(dym, v2) - delta2[:, c0:c0 + 1])).astype(bf16)
        dq2 = dq2 + _dot(ds, km)
        pes.append(pe.astype(bf16))
        dss.append(ds)
        qms.append(qm)
        dyms.append(dym)
    dv2 = _dot_tn(jnp.concatenate(pes, axis=0), jnp.concatenate(dyms, axis=0))
    dk2 = _dot_tn(jnp.concatenate(dss, axis=0), jnp.concatenate(qms, axis=0))
    return dq2, dk2, dv2


TOK = 2048


def _key_bias():
    qi = lax.broadcasted_iota(jnp.int32, (ATTN_BLK, 2 * ATTN_BLK), 0)
    kj = lax.broadcasted_iota(jnp.int32, (ATTN_BLK, 2 * ATTN_BLK), 1)
    delta = ATTN_BLK + qi - kj
    seen = (delta >= 0) & (delta <= ATTN_BLK)
    return jnp.where(seen, 0.0, NEG), jnp.where(seen & (kj >= ATTN_BLK), 0.0, NEG)


def _blocks(dil):
    return [(r, TOK // (ATTN_BLK * dil), ATTN_BLK * dil) for r in range(dil)]


def _attn_fwd(qkv, rider=None):
    S = qkv.shape[0]
    nS = S // TOK

    def body(q_ref, kp_ref, kc_ref, vp_ref, vc_ref, y_ref, l_ref, qs, k2, v2, ay, al):
        n = pl.program_id(1)
        qs[...] = q_ref[...].astype(f32)
        k2[0:TOK] = kp_ref[...].astype(f32)
        k2[TOK:2 * TOK] = kc_ref[...].astype(f32)
        v2[0:TOK] = vp_ref[...].astype(f32)
        v2[TOK:2 * TOK] = vc_ref[...].astype(f32)
        bias_any, bias_first = _key_bias()

        def block(dil, r, b, step, last):
            start = r + (pl.multiple_of(step * b, step) if step < TOK else 0)
            rows = pl.ds(start, ATTN_BLK, stride=dil) if dil > 1 else pl.ds(start, ATTN_BLK)
            keys = (pl.ds(TOK + start - step, 2 * ATTN_BLK, stride=dil) if dil > 1
                    else pl.ds(TOK + start - step, 2 * ATTN_BLK))
            bias = jnp.where((n == 0) & (b == 0), bias_first, bias_any)
            out, lse = _pair_fwd(qs[rows, :], k2[keys, :].astype(bf16), v2[keys, :].astype(bf16), bias)
            if dil < DILATIONS[-1]:
                out, lse = _merge(ay[rows, :], al[rows, :], out, lse)
            if last:
                y_ref[rows, :] = out
                l_ref[rows, :] = lse
            else:
                ay[rows, :] = out
                al[rows, :] = lse

        for dil in reversed(DILATIONS):
            for r, nblk, step in _blocks(dil):
                if nblk == 1:
                    block(dil, r, 0, step, dil == 1)
                else:
                    def loop(b, carry, dil=dil, r=r, step=step):
                        block(dil, r, b, step, dil == 1)
                        return carry
                    lax.fori_loop(0, nblk, loop, 0, unroll=2)

    blk = (TOK, 128)
    cur = lambda c: pl.BlockSpec(blk, lambda p, n: (n, 4 * c + p))
    prv = lambda c: pl.BlockSpec(blk, lambda p, n: (jnp.maximum(n - 1, 0), 4 * c + p))
    out = pl.BlockSpec(blk, lambda p, n: (n, p))
    call = dict(in_specs=[cur(0), prv(1), cur(1), prv(2), cur(2)], out_specs=[out, out],
                out_shape=[jax.ShapeDtypeStruct((S, ATTN_W), f32)] * 2,
                scratch_shapes=[pltpu.VMEM(blk, f32), pltpu.VMEM((2 * TOK, 128), f32), pltpu.VMEM((2 * TOK, 128), f32),
                                pltpu.VMEM(blk, f32), pltpu.VMEM(blk, f32)])
    call, body, more = _ride(call, rider, body, lambda: pl.program_id(0) * nS + pl.program_id(1), (ATTN_W // 128) * nS, 5, 2, 5)
    return pl.pallas_call(body, name="attention_fwd", grid=(ATTN_W // 128, nS), compiler_params=_cp("arbitrary", "arbitrary"),
                          **call)(qkv, qkv, qkv, qkv, qkv, *more)


def _attn_bwd(qkv, ya, lse, dmix):
    S = qkv.shape[0]
    nS = S // TOK

    def body(q_ref, kp_ref, kc_ref, vp_ref, vc_ref, y_ref, l_ref, dy_ref, dq_ref, dk_ref, dv_ref, qs, k2, v2, dk2, dv2, dqa, dl):
        n = pl.program_id(1)

        @pl.when(n == 0)
        def _():
            dk2[...] = jnp.zeros_like(dk2)
            dv2[...] = jnp.zeros_like(dv2)

        @pl.when(n < nS)
        def _():
            qs[...] = q_ref[...].astype(f32)
            k2[0:TOK] = kp_ref[...].astype(f32)
            k2[TOK:2 * TOK] = kc_ref[...].astype(f32)
            v2[0:TOK] = vp_ref[...].astype(f32)
            v2[TOK:2 * TOK] = vc_ref[...].astype(f32)
            li = lax.broadcasted_iota(jnp.int32, (128, 128), 0)
            lj = lax.broadcasted_iota(jnp.int32, (128, 128), 1)
            seg = jnp.where((li // HEAD_DIM) == (lj // HEAD_DIM), 1.0, 0.0).astype(bf16)
            bias_any, bias_first = _key_bias()

            def delta_rows(t, carry):
                rows = pl.ds(pl.multiple_of(256 * t, 256), 256)
                dyy = dy_ref[rows, :] * y_ref[rows, :]
                hi = dyy.astype(bf16)
                dl[rows, :] = _dot(hi, seg) + _dot((dyy - hi.astype(f32)).astype(bf16), seg)
                return carry

            lax.fori_loop(0, TOK // 256, delta_rows, 0)

            def block(dil, r, b, step, first_pattern, last):
                start = r + (pl.multiple_of(step * b, step) if step < TOK else 0)
                rows = pl.ds(start, ATTN_BLK, stride=dil) if dil > 1 else pl.ds(start, ATTN_BLK)
                keys = (pl.ds(TOK + start - step, 2 * ATTN_BLK, stride=dil) if dil > 1
                        else pl.ds(TOK + start - step, 2 * ATTN_BLK))
                bias = jnp.where((n == 0) & (b == 0), bias_first, bias_any)
                dq2, dkk, dvv = _pair_bwd(qs[rows, :], k2[keys, :], v2[keys, :].astype(bf16), dy_ref[rows, :],
                                          l_ref[rows, :], dl[rows, :], bias)
                if last:
                    dq_ref[rows, :] = dqa[rows, :] + dq2
                elif first_pattern:
                    dqa[rows, :] = dq2
                else:
                    dqa[rows, :] += dq2
                dk2[keys, :] += dkk
                dv2[keys, :] += dvv

            for dil in reversed(DILATIONS):
                for r, nblk, step in _blocks(dil):
                    if nblk == 1:
                        block(dil, r, 0, step, dil == DILATIONS[-1], dil == 1)
                    else:
                        def loop(b, carry, dil=dil, r=r, step=step):
                            block(dil, r, b, step, dil == DILATIONS[-1], dil == 1)
                            return carry
                        lax.fori_loop(0, nblk, loop, 0, unroll=2)

        dk_ref[...] = dk2[0:TOK]
        dv_ref[...] = dv2[0:TOK]
        dk2[0:TOK] = dk2[TOK:2 * TOK]
        dv2[0:TOK] = dv2[TOK:2 * TOK]
        dk2[TOK:2 * TOK] = jnp.zeros((TOK, 128), f32)
        dv2[TOK:2 * TOK] = jnp.zeros((TOK, 128), f32)

    blk = (TOK, 128)
    cn = lambda n: jnp.minimum(n, nS - 1)
    pn = lambda n: jnp.clip(n - 1, 0, nS - 1)
    cur = lambda c: pl.BlockSpec(blk, lambda p, n: (cn(n), 4 * c + p))
    prv = lambda c: pl.BlockSpec(blk, lambda p, n: (pn(n), 4 * c + p))
    at_n = pl.BlockSpec(blk, lambda p, n: (cn(n), p))
    at_p = pl.BlockSpec(blk, lambda p, n: (pn(n), p))
    big = lambda: pltpu.VMEM((2 * TOK, 128), f32)
    return pl.pallas_call(
        body, name="attention_bwd", grid=(ATTN_W // 128, nS + 1),
        in_specs=[cur(0), prv(1), cur(1), prv(2), cur(2), at_n, at_n, at_n], out_specs=[at_n, at_p, at_p],
        out_shape=[jax.ShapeDtypeStruct((S, ATTN_W), f32)] * 3,
        scratch_shapes=[pltpu.VMEM(blk, f32), big(), big(), big(), big(), pltpu.VMEM(blk, f32), pltpu.VMEM(blk, f32)],
        compiler_params=_cp("arbitrary", "arbitrary"),
    )(qkv, qkv, qkv, qkv, qkv, ya, lse, dmix)


HG_T = 256
N_HH = HGRN_W // HGRN_HD
HG_SUB = 128
SAFE_RANGE = 80.0


def _row_in_chunk():
    return lax.broadcasted_iota(jnp.int32, (HG_T, HGRN_HD), 0) % CHUNK


def _chunk_cumsum(v, rc):
    for k in (1, 2, 4, 8):
        v = v + jnp.where(rc >= k, pltpu.roll(v, k, 0), 0.0)
    return v


def _chunk_rcumsum(v, rc):
    for k in (1, 2, 4, 8):
        v = v + jnp.where(rc < CHUNK - k, pltpu.roll(v, HG_T - k, 0), 0.0)
    return v


def _hgrn_gates(qb, fb, lb):
    sf = _sigmoid(fb)
    f = lb + (1.0 - lb) * sf
    sq = _sigmoid(qb)
    return sf, f, jnp.log(f), 1.0 - f, sq, qb * sq


def _hgrn_prep(qb, fb, lbl2, rc):
    lb = _sigmoid(lbl2[0:1, :] - lbl2[1:2, :])
    sf, f, lf, key, sq, qf = _hgrn_gates(qb, fb, lb)
    b = _chunk_cumsum(lf, rc)
    rem = _chunk_rcumsum(lf, rc) - lf
    return dict(lb=lb, sf=sf, f=f, key=key, sq=sq, qf=qf, b=b, rem=rem, eb=jnp.exp(b), er=jnp.exp(rem))


def _chunk_mask():
    r = lax.broadcasted_iota(jnp.int32, (HG_SUB, HG_SUB), 0)
    c = lax.broadcasted_iota(jnp.int32, (HG_SUB, HG_SUB), 1)
    return ((r // CHUNK) == (c // CHUNK)) & (c <= r)


def _hgrn_fwd(hp, lbl, wn):
    S = hp.shape[0]
    nT = S // HG_T

    def body(qb_ref, fb_ref, ib_ref, gb_ref, lbl_ref, wn_ref, yb_ref, o_ref, st_ref, ST, qt_s, kh_s, dec_s, oi_s):
        @pl.when(pl.program_id(0) == 0)
        def _():
            ST[...] = jnp.zeros_like(ST)

        rc = _row_in_chunk()
        for h in range(N_HH):
            sl = slice(HGRN_HD * h, HGRN_HD * (h + 1))
            p = _hgrn_prep(qb_ref[:, sl], fb_ref[:, sl], lbl_ref[:, sl], rc)
            qf, key, b = p["qf"], p["key"], p["b"]
            qt = qf * p["eb"]
            qt_s[:, sl] = qt.astype(bf16)
            kh_s[:, sl] = (key * p["er"]).astype(bf16)
            dec_s[:, sl] = jnp.exp(b + p["rem"])
            rng = jnp.max(-(b + p["rem"]))

            @pl.when(rng < SAFE_RANGE)
            def _():
                kp = (key * jnp.exp(-b)).astype(bf16)
                cmask = _chunk_mask()
                for j in range(HG_T // HG_SUB):
                    rs = slice(HG_SUB * j, HG_SUB * (j + 1))
                    sc = jnp.where(cmask, _dot_nt(qt[rs].astype(bf16), kp[rs]), 0.0).astype(bf16)
                    oi_s[rs, sl] = _dot(sc, ib_ref[rs, sl].astype(bf16))

            @pl.when(rng >= SAFE_RANGE)
            def _():
                v = ib_ref[:, sl]
                ones = jnp.ones((HGRN_HD, HGRN_HD), bf16)
                o = jnp.zeros((HG_T, HGRN_HD), f32)
                for l in range(CHUNK):
                    if l == 0:
                        pr, vs = qf * key, v
                    else:
                        e = jnp.exp(jnp.where(rc >= l, b - pltpu.roll(b, l, 0), NEG))
                        pr, vs = qf * pltpu.roll(key, l, 0) * e, pltpu.roll(v, l, 0)
                    o = o + _dot(pr.astype(bf16), ones) * vs
                oi_s[:, sl] = o

        def step(c, carry):
            rows = pl.ds(pl.multiple_of(c * CHUNK, CHUNK), CHUNK)
            row0 = pl.ds(pl.multiple_of(c * CHUNK, CHUNK), 1)
            for h in range(N_HH):
                sl = slice(HGRN_HD * h, HGRN_HD * (h + 1))
                stv = ST[h]
                st_ref[c, sl, :] = stv
                oi_s[rows, sl] += _dot_nt(qt_s[rows, sl], stv.astype(bf16))
                ST[h] = stv * dec_s[row0, sl] + _dot_tn(ib_ref[rows, sl].astype(bf16), kh_s[rows, sl])
            return carry

        lax.fori_loop(0, HG_T // CHUNK, step, 0, unroll=8)

        for h in range(N_HH):
            sl = slice(HGRN_HD * h, HGRN_HD * (h + 1))
            o = oi_s[:, sl]
            o_ref[:, sl] = o
            on = o * lax.rsqrt(jnp.mean(o * o, axis=-1, keepdims=True) + EPS)
            g = gb_ref[:, sl]
            yb_ref[:, sl] = on * wn_ref[:, sl] * (g * _sigmoid(g))

    col = lambda c: pl.BlockSpec((HG_T, HGRN_W), lambda i: (i, c))
    tile = pl.BlockSpec((HG_T, HGRN_W), lambda i: (i, 0))
    whole = lambda a: pl.BlockSpec(a.shape, lambda i: (0, 0))
    return pl.pallas_call(
        body, name="hgrn_fwd", grid=(nT,),
        in_specs=[col(0), col(1), col(2), col(3), whole(lbl), whole(wn)],
        out_specs=[tile, tile, pl.BlockSpec((HG_T // CHUNK, HGRN_W, HGRN_HD), lambda i: (i, 0, 0))],
        out_shape=[jax.ShapeDtypeStruct((S, HGRN_W), f32), jax.ShapeDtypeStruct((S, HGRN_W), f32),
                   jax.ShapeDtypeStruct((S // CHUNK, HGRN_W, HGRN_HD), f32)],
        scratch_shapes=[pltpu.VMEM((N_HH, HGRN_HD, HGRN_HD), f32), pltpu.VMEM((HG_T, HGRN_W), bf16),
                        pltpu.VMEM((HG_T, HGRN_W), bf16), pltpu.VMEM((HG_T, HGRN_W), f32), pltpu.VMEM((HG_T, HGRN_W), f32)],
        compiler_params=_cp("arbitrary"),
    )(hp, hp, hp, hp, lbl, wn)


def _hgrn_bwd(hp, lbl, wn, o_sav, states, dmix, rider=None):
    S = hp.shape[0]
    nT = S // HG_T

    def body(qb_ref, fb_ref, ib_ref, gb_ref, lbl_ref, wn_ref, o_ref, st_ref, dy_ref,
             dq_ref, df_ref, di_ref, dg_ref, gwn_ref, glb_ref,
             DST, qt_s, kh_s, dec_s, do_s, dqt_s, dkh_s, dbl_s, dvi_s, dqi_s, dki_s, dbi_s):
        @pl.when(pl.program_id(0) == 0)
        def _():
            DST[...] = jnp.zeros_like(DST)
            gwn_ref[...] = jnp.zeros_like(gwn_ref)
            glb_ref[...] = jnp.zeros_like(glb_ref)

        rc = _row_in_chunk()
        preps = []
        for h in range(N_HH):
            sl = slice(HGRN_HD * h, HGRN_HD * (h + 1))
            p = _hgrn_prep(qb_ref[:, sl], fb_ref[:, sl], lbl_ref[:, sl], rc)
            preps.append(p)
            qf, key, b = p["qf"], p["key"], p["b"]
            v = ib_ref[:, sl]
            o = o_ref[:, sl]
            rinv = lax.rsqrt(jnp.mean(o * o, axis=-1, keepdims=True) + EPS)
            on = o * rinv
            g = gb_ref[:, sl]
            sgm = _sigmoid(g)
            silu_g = g * sgm
            dy = dy_ref[:, sl]
            wn_v = wn_ref[:, sl]
            gwn_ref[:, sl] += jnp.sum(dy * on * silu_g, axis=0, keepdims=True)
            dg_ref[:, sl] = (dy * on * wn_v * (sgm * (1.0 + g * (1.0 - sgm)))).astype(bf16)
            t1 = dy * wn_v * silu_g
            do = rinv * (t1 - on * jnp.mean(t1 * on, axis=-1, keepdims=True))
            do_s[:, sl] = do.astype(bf16)
            qt = qf * p["eb"]
            qt_s[:, sl] = qt.astype(bf16)
            kh_s[:, sl] = (key * p["er"]).astype(bf16)
            dec_s[:, sl] = jnp.exp(b + p["rem"])
            rng = jnp.max(-(b + p["rem"]))

            @pl.when(rng < SAFE_RANGE)
            def _():
                einv = jnp.exp(-b)
                kp = (key * einv).astype(bf16)
                cmask = _chunk_mask()
                for j in range(HG_T // HG_SUB):
                    rs = slice(HG_SUB * j, HG_SUB * (j + 1))
                    qtb, dob, vb = qt[rs].astype(bf16), do[rs].astype(bf16), v[rs].astype(bf16)
                    sc = jnp.where(cmask, _dot_nt(qtb, kp[rs]), 0.0).astype(bf16)
                    dsc = jnp.where(cmask, _dot_nt(dob, vb), 0.0).astype(bf16)
                    dqp = _dot(dsc, kp[rs])
                    dkp = _dot_tn(dsc, qtb)
                    dvi_s[rs, sl] = _dot_tn(sc, dob)
                    dqi_s[rs, sl] = dqp * p["eb"][rs]
                    dki_s[rs, sl] = dkp * einv[rs]
                    dbi_s[rs, sl] = dqp * qtb.astype(f32) - dkp * kp[rs].astype(f32)

            @pl.when(rng >= SAFE_RANGE)
            def _():
                ones = jnp.ones((HGRN_HD, HGRN_HD), bf16)
                dqf = jnp.zeros((HG_T, HGRN_HD), f32)
                dkey, db, dv = dqf, dqf, dqf
                for l in range(CHUNK):
                    if l == 0:
                        ks, vs, qe = key, v, qf
                    else:
                        e = jnp.exp(jnp.where(rc >= l, b - pltpu.roll(b, l, 0), NEG))
                        ks, vs, qe = pltpu.roll(key, l, 0), pltpu.roll(v, l, 0), qf * e
                    pr = qe * ks
                    rl = _dot(pr.astype(bf16), ones)
                    drl = _dot((do * vs).astype(bf16), ones)
                    if l == 0:
                        dqf = dqf + drl * ks
                        dv = dv + rl * do
                        dkey = dkey + drl * qe
                    else:
                        drl = jnp.where(rc >= l, drl, 0.0)
                        gl = drl * pr
                        dqf = dqf + drl * ks * e
                        dv = dv + pltpu.roll(rl * do, HG_T - l, 0)
                        dkey = dkey + pltpu.roll(drl * qe, HG_T - l, 0)
                        db = db + gl - pltpu.roll(gl, HG_T - l, 0)
                dvi_s[:, sl] = dv
                dqi_s[:, sl] = dqf
                dki_s[:, sl] = dkey
                dbi_s[:, sl] = db

        def step(k, carry):
            c = HG_T // CHUNK - 1 - k
            rows = pl.ds(pl.multiple_of(c * CHUNK, CHUNK), CHUNK)
            row0 = pl.ds(pl.multiple_of(c * CHUNK, CHUNK), 1)
            for h in range(N_HH):
                sl = slice(HGRN_HD * h, HGRN_HD * (h + 1))
                stp = st_ref[c, sl, :]
                dst = DST[h]
                dstb = dst.astype(bf16)
                dob = do_s[rows, sl]
                khb = kh_s[rows, sl]
                dec = dec_s[row0, sl]
                dqt_s[rows, sl] = _dot(dob, stp.astype(bf16))
                dkh = _dot(ib_ref[rows, sl].astype(bf16), dstb)
                dkh_s[rows, sl] = dkh
                dvi_s[rows, sl] += _dot_nt(khb, dstb)
                dbl = jnp.sum(dst * stp, axis=0, keepdims=True) * dec + jnp.sum(dkh * khb.astype(f32), axis=0, keepdims=True)
                dbl_s[rows, sl] = jnp.broadcast_to(dbl, (CHUNK, HGRN_HD))
                DST[h] = dst * dec + _dot_tn(dob, qt_s[rows, sl])
            return carry

        lax.fori_loop(0, HG_T // CHUNK, step, 0, unroll=4)

        for h in range(N_HH):
            sl = slice(HGRN_HD * h, HGRN_HD * (h + 1))
            qb = qb_ref[:, sl]
            p = preps[h]
            sf, sq, lb = p["sf"], p["sq"], p["lb"]
            dqt, dkh = dqt_s[:, sl], dkh_s[:, sl]
            dqf = dqt * p["eb"] + dqi_s[:, sl]
            dkey = dkh * p["er"] + dki_s[:, sl]
            db = dqt * (p["qf"] * p["eb"]) - dkh * (p["key"] * p["er"]) + jnp.where(rc == CHUNK - 1, dbl_s[:, sl], 0.0) + dbi_s[:, sl]
            df = _chunk_rcumsum(db, rc) / p["f"] - dkey
            df_ref[:, sl] = (df * (1.0 - lb) * sf * (1.0 - sf)).astype(bf16)
            glb_ref[:, sl] += jnp.sum(df * (1.0 - sf), axis=0, keepdims=True)
            dq_ref[:, sl] = (dqf * (sq * (1.0 + qb * (1.0 - sq)))).astype(bf16)
            di_ref[:, sl] = dvi_s[:, sl].astype(bf16)

    rev = lambda i: nT - 1 - i
    col = lambda c: pl.BlockSpec((HG_T, HGRN_W), lambda i: (rev(i), c))
    tile = pl.BlockSpec((HG_T, HGRN_W), lambda i: (rev(i), 0))
    whole = lambda a: pl.BlockSpec(a.shape, lambda i: (0, 0))
    vec = pl.BlockSpec((1, HGRN_W), lambda i: (0, 0))
    tb = lambda: pltpu.VMEM((HG_T, HGRN_W), bf16)
    tf = lambda: pltpu.VMEM((HG_T, HGRN_W), f32)
    call = dict(in_specs=[col(0), col(1), col(2), col(3), whole(lbl), whole(wn), tile,
                          pl.BlockSpec((HG_T // CHUNK, HGRN_W, HGRN_HD), lambda i: (rev(i), 0, 0)),
                          pl.BlockSpec((HG_T, HGRN_W), lambda i: (rev(i), 1))],
                out_specs=[tile, tile, tile, tile, vec, vec],
                out_shape=[jax.ShapeDtypeStruct((S, HGRN_W), bf16)] * 4 + [jax.ShapeDtypeStruct((1, HGRN_W), f32)] * 2,
                scratch_shapes=[pltpu.VMEM((N_HH, HGRN_HD, HGRN_HD), f32), tb(), tb(), tf(), tb(), tf(), tf(), tf(), tf(), tf(),
                                tf(), tf()])
    call, body, more = _ride(call, rider, body, lambda: pl.program_id(0), nT, 9, 6, 12)
    return pl.pallas_call(body, name="hgrn_bwd", grid=(nT,), compiler_params=_cp("arbitrary"), **call)(
        hp, hp, hp, hp, lbl, wn, o_sav, states, dmix, *more)


def _out_proj(x, ya, yb, wout, w2):
    S = x.shape[0]
    tm = 512

    def body(x_ref, ya_ref, yb_ref, w_ref, w2_ref, h1_ref, u2_ref, mix_ref):
        mixed = jnp.concatenate([ya_ref[...], yb_ref[...]], axis=1).astype(bf16)
        mix_ref[...] = mixed
        h1 = x_ref[...] + _dot(mixed, w_ref[...])
        h1_ref[...] = h1
        r = lax.rsqrt(jnp.mean(h1 * h1, axis=-1, keepdims=True) + EPS)
        u2_ref[...] = (h1 * r * w2_ref[...]).astype(bf16)

    row = lambda w: pl.BlockSpec((tm, w), lambda i: (i, 0))
    return pl.pallas_call(
        body, name="out_proj", grid=(S // tm,),
        in_specs=[row(D_MODEL), row(ATTN_W), row(HGRN_W), pl.BlockSpec((D_MODEL, D_MODEL), lambda i: (0, 0)),
                  pl.BlockSpec((1, D_MODEL), lambda i: (0, 0))],
        out_specs=[row(D_MODEL), row(D_MODEL), row(D_MODEL)],
        out_shape=[jax.ShapeDtypeStruct((S, D_MODEL), f32), jax.ShapeDtypeStruct((S, D_MODEL), bf16),
                   jax.ShapeDtypeStruct((S, D_MODEL), bf16)],
        compiler_params=_cp("arbitrary"),
    )(x, ya, yb, wout, w2)


def _gate_up(u2, wgu):
    S = u2.shape[0]
    tm, tn = 512, 1408
    nj = FFN // tn

    def body(u_ref, wg_ref, wu_ref, g_ref, up_ref, a_ref):
        u = u_ref[...]
        g = _dot(u, wg_ref[...])
        up = _dot(u, wu_ref[...])
        g_ref[...] = g.astype(bf16)
        up_ref[...] = up.astype(bf16)
        a_ref[...] = (g * _sigmoid(g) * up).astype(bf16)

    out = pl.BlockSpec((tm, tn), lambda j, i: (i, j))
    return pl.pallas_call(
        body, name="gate_up", grid=(nj, S // tm),
        in_specs=[pl.BlockSpec((tm, D_MODEL), lambda j, i: (i, 0)), pl.BlockSpec((D_MODEL, tn), lambda j, i: (0, j)),
                  pl.BlockSpec((D_MODEL, tn), lambda j, i: (0, j + nj))],
        out_specs=[out, out, out],
        out_shape=[jax.ShapeDtypeStruct((S, FFN), bf16)] * 3,
        compiler_params=_cp("arbitrary", "arbitrary"),
    )(u2, wgu, wgu)


def _rms_bwd(dyw, hn, r):
    return r * (dyw - hn * jnp.mean(dyw * hn, axis=-1, keepdims=True))


def _down_loss(act, wdown, h1, tgt, w3):
    S = act.shape[0]
    tm = 256

    def body(a_ref, w_ref, h1_ref, t_ref, w3_ref, dh2_ref, loss_ref, gw3_ref):
        @pl.when(pl.program_id(0) == 0)
        def _():
            loss_ref[...] = jnp.zeros_like(loss_ref)
            gw3_ref[...] = jnp.zeros_like(gw3_ref)

        h2 = h1_ref[...] + _dot(a_ref[...], w_ref[...])
        r = lax.rsqrt(jnp.mean(h2 * h2, axis=-1, keepdims=True) + EPS)
        hn = h2 * r
        w3 = w3_ref[...]
        err = hn * w3 - t_ref[...]
        loss_ref[...] += (0.5 / D_MODEL) * jnp.sum(err * err)
        dy = err * (1.0 / D_MODEL)
        gw3_ref[...] += jnp.sum(dy * hn, axis=0, keepdims=True)
        dh2_ref[...] = _rms_bwd(dy * w3, hn, r)

    row = lambda w: pl.BlockSpec((tm, w), lambda i: (i, 0))
    return pl.pallas_call(
        body, name="down_loss", grid=(S // tm,),
        in_specs=[row(FFN), pl.BlockSpec((FFN, D_MODEL), lambda i: (0, 0)), row(D_MODEL), row(D_MODEL),
                  pl.BlockSpec((1, D_MODEL), lambda i: (0, 0))],
        out_specs=[row(D_MODEL), pl.BlockSpec((1, 128), lambda i: (0, 0)), pl.BlockSpec((1, D_MODEL), lambda i: (0, 0))],
        out_shape=[jax.ShapeDtypeStruct((S, D_MODEL), f32), jax.ShapeDtypeStruct((1, 128), f32),
                   jax.ShapeDtypeStruct((1, D_MODEL), f32)],
        compiler_params=_cp("arbitrary"),
    )(act, wdown, h1, tgt, w3)


def _dact(dh2, wdown, gate, up):
    S = dh2.shape[0]
    tm = 256

    def body(d_ref, w_ref, g_ref, u_ref, dg_ref, du_ref):
        da = _dot_nt(d_ref[...].astype(bf16), w_ref[...])
        g = g_ref[...].astype(f32)
        sg = _sigmoid(g)
        du_ref[...] = (da * g * sg).astype(bf16)
        dg_ref[...] = (da * u_ref[...].astype(f32) * (sg * (1.0 + g * (1.0 - sg)))).astype(bf16)

    row = lambda w: pl.BlockSpec((tm, w), lambda i: (i, 0))
    return pl.pallas_call(
        body, name="dact", grid=(S // tm,),
        in_specs=[row(D_MODEL), pl.BlockSpec((FFN, D_MODEL), lambda i: (0, 0)), row(FFN), row(FFN)],
        out_specs=[row(FFN), row(FFN)],
        out_shape=[jax.ShapeDtypeStruct((S, FFN), bf16)] * 2,
        compiler_params=_cp("arbitrary"),
    )(dh2, wdown, gate, up)


def _dgu(dgate, dup, wgu, h1, w2, dh2, wout, rider=None):
    S = dgate.shape[0]
    tm = 256

    def body(dg_ref, du_ref, wg_ref, wu_ref, h1_ref, w2_ref, dh2_ref, wo_ref, dh1_ref, gw2_ref, dmix_ref):
        @pl.when(pl.program_id(0) == 0)
        def _():
            gw2_ref[...] = jnp.zeros_like(gw2_ref)

        du2 = _dot_nt(dg_ref[...], wg_ref[...]) + _dot_nt(du_ref[...], wu_ref[...])
        h1 = h1_ref[...]
        r = lax.rsqrt(jnp.mean(h1 * h1, axis=-1, keepdims=True) + EPS)
        hn = h1 * r
        gw2_ref[...] += jnp.sum(du2 * hn, axis=0, keepdims=True)
        dh1 = dh2_ref[...] + _rms_bwd(du2 * w2_ref[...], hn, r)
        dh1_ref[...] = dh1
        dmix_ref[...] = _dot_nt(dh1.astype(bf16), wo_ref[...])

    row = lambda w: pl.BlockSpec((tm, w), lambda i: (i, 0))
    call = dict(in_specs=[row(FFN), row(FFN), pl.BlockSpec((D_MODEL, FFN), lambda i: (0, 0)),
                          pl.BlockSpec((D_MODEL, FFN), lambda i: (0, 1)), row(D_MODEL),
                          pl.BlockSpec((1, D_MODEL), lambda i: (0, 0)), row(D_MODEL),
                          pl.BlockSpec((D_MODEL, D_MODEL), lambda i: (0, 0))],
                out_specs=[row(D_MODEL), pl.BlockSpec((1, D_MODEL), lambda i: (0, 0)), row(D_MODEL)],
                out_shape=[jax.ShapeDtypeStruct((S, D_MODEL), f32), jax.ShapeDtypeStruct((1, D_MODEL), f32),
                           jax.ShapeDtypeStruct((S, D_MODEL), f32)], scratch_shapes=[])
    call, body, more = _ride(call, rider, body, lambda: pl.program_id(0), S // tm, 8, 3, 0)
    return pl.pallas_call(body, name="dgu", grid=(S // tm,), compiler_params=_cp("arbitrary"), **call)(
        dgate, dup, wgu, wgu, h1, w2, dh2, wout, *more)


def _din(dq, dk, dv, dhq, dhf, dhi, dhg, cos_t, sg_t, win, x, w1, dh1):
    S = x.shape[0]
    tm = 256

    def body(dq_ref, dk_ref, dv_ref, dhq_ref, dhf_ref, dhi_ref, dhg_ref, cos_ref, sg_ref, w_ref, x_ref, w1_ref, dh1_ref,
             dp_ref, gx_ref, gw1_ref):
        @pl.when(pl.program_id(0) == 0)
        def _():
            gw1_ref[...] = jnp.zeros_like(gw1_ref)

        cosv, sgv = jnp.tile(cos_ref[...], (1, ATTN_W // 128)), jnp.tile(sg_ref[...], (1, ATTN_W // 128))
        unrope = lambda d: d * cosv - sgv * _swap_halves(d)
        parts = [(unrope(dq_ref[...]) * (HEAD_DIM ** -0.5)).astype(bf16), unrope(dk_ref[...]).astype(bf16),
                 dv_ref[...].astype(bf16), dhq_ref[...], dhf_ref[...], dhi_ref[...], dhg_ref[...]]
        du = jnp.zeros((tm, D_MODEL), f32)
        for j, pj in enumerate(parts):
            dp_ref[:, j * 512:(j + 1) * 512] = pj
            du = du + _dot_nt(pj, w_ref[:, j * 512:(j + 1) * 512])
        xv = x_ref[...]
        r = lax.rsqrt(jnp.mean(xv * xv, axis=-1, keepdims=True) + EPS)
        xn = xv * r
        gw1_ref[...] += jnp.sum(du * xn, axis=0, keepdims=True)
        gx_ref[...] = dh1_ref[...] + _rms_bwd(du * w1_ref[...], xn, r)

    row = lambda w: pl.BlockSpec((tm, w), lambda i: (i, 0))
    vec = pl.BlockSpec((1, D_MODEL), lambda i: (0, 0))
    return pl.pallas_call(
        body, name="din", grid=(S // tm,),
        in_specs=[row(512)] * 7 + [row(128), row(128), pl.BlockSpec((D_MODEL, IN_W), lambda i: (0, 0)), row(D_MODEL), vec,
                                   row(D_MODEL)],
        out_specs=[row(IN_W), row(D_MODEL), vec],
        out_shape=[jax.ShapeDtypeStruct((S, IN_W), bf16), jax.ShapeDtypeStruct((S, D_MODEL), f32),
                   jax.ShapeDtypeStruct((1, D_MODEL), f32)],
        compiler_params=_cp("arbitrary"),
    )(dq, dk, dv, dhq, dhf, dhi, dhg, cos_t, sg_t, win, x, w1, dh1)


def _gw(a, bs, tn, name):
    S, M = a.shape
    N = bs[0].shape[1]
    ts = 512
    k = len(bs)

    def body(a_ref, *refs):
        @pl.when(pl.program_id(1) == 0)
        def _():
            for o_ref in refs[k:]:
                o_ref[...] = jnp.zeros_like(o_ref)

        at = a_ref[...].astype(bf16)
        for b_ref, o_ref in zip(refs[:k], refs[k:]):
            o_ref[...] += _dot_tn(at, b_ref[...].astype(bf16))

    return pl.pallas_call(
        body, name=name, grid=(N // tn, S // ts),
        in_specs=[pl.BlockSpec((ts, M), lambda j, s: (s, 0))] + [pl.BlockSpec((ts, tn), lambda j, s: (s, j))] * k,
        out_specs=[pl.BlockSpec((M, tn), lambda j, s: (0, j))] * k, out_shape=[jax.ShapeDtypeStruct((M, N), f32)] * k,
        compiler_params=_cp("arbitrary", "arbitrary"),
    )(a, *bs)


MESH = pl.DeviceIdType.MESH
ANY = pl.BlockSpec(memory_space=pl.ANY)
VMEM_SPEC = pl.BlockSpec(memory_space=pltpu.VMEM)


def _pos():
    return lax.axis_index("x"), lax.axis_index("y"), lax.axis_index("c")


def _flip(v, bit):
    return 1 - v if bit else v


def _gather_rider(shards):
    n = len(shards)

    def parts(outs, scratch):
        send_sems, recv_sems, local_sems = scratch[n:]
        x, y, c = _pos()
        chips = [(1 - x, y), (x, 1 - y), (1 - x, 1 - y)]

        def copy(a, k, block, to, src=None):
            dst = outs[a].at[4 * block[0] + 2 * block[1] + block[2]]
            return pltpu.make_async_remote_copy(src_ref=dst if src is None else src, dst_ref=dst, send_sem=send_sems.at[a, k],
                                                recv_sem=recv_sems.at[a, k], device_id=to, device_id_type=MESH)

        bufs = scratch[:n]
        me, sibling = (x, y, c), (x, y, 1 - c)
        own = lambda a: pltpu.make_async_copy(bufs[a], outs[a].at[4 * x + 2 * y + c], local_sems.at[a])
        sent = lambda a: [copy(a, 0, me, sibling, src=bufs[a])] + [copy(a, 1 + j, me, (*chip, c), src=bufs[a])
                                                                   for j, chip in enumerate(chips)]
        passed = lambda a: [copy(a, 4 + j, (*chip, c), sibling) for j, chip in enumerate(chips)]
        landed = lambda a: [copy(a, 1 + j, (*chip, c), me) for j, chip in enumerate(chips)]
        from_sibling = lambda a: [copy(a, 0, sibling, me)] + [copy(a, 4 + j, (*chip, 1 - c), me) for j, chip in enumerate(chips)]
        return bufs, local_sems, own, sent, passed, landed, from_sibling

    def first(ins, outs, scratch):
        bufs, local_sems, own, sent, _, _, _ = parts(outs, scratch)
        loads = [pltpu.make_async_copy(ins[a], bufs[a], local_sems.at[a]) for a in range(n)]
        for ld in loads:
            ld.start()
        for a in range(n):
            loads[a].wait()
            own(a).start()
            for cp in sent(a):
                cp.start()

    def middle(ins, outs, scratch):
        _, _, _, _, passed, landed, _ = parts(outs, scratch)
        for a in range(n):
            for got, on in zip(landed(a), passed(a)):
                got.wait_recv()
                on.start()

    def last(ins, outs, scratch):
        _, _, own, sent, passed, _, from_sibling = parts(outs, scratch)
        for a in range(n):
            for cp in from_sibling(a):
                cp.wait_recv()
        for a in range(n):
            for cp in sent(a) + passed(a):
                cp.wait_send()
            own(a).wait()

    return _Rider(shards, [jax.ShapeDtypeStruct((N_DEV,) + s.shape, s.dtype) for s in shards],
                  [pltpu.VMEM(s.shape, s.dtype) for s in shards]
                  + [pltpu.SemaphoreType.DMA((n, 7)), pltpu.SemaphoreType.DMA((n, 7)), pltpu.SemaphoreType.DMA((n,))],
                  first, last, middle)


def _sibling_rider(grads):
    n = len(grads)

    def copies(g, got, scratch):
        send_sems, recv_sems = scratch
        x, y, c = _pos()
        return [pltpu.make_async_remote_copy(src_ref=g[a].at[2 * q + (1 - c)], dst_ref=got[a].at[q], send_sem=send_sems.at[a, q],
                                             recv_sem=recv_sems.at[a, q], device_id=(x, y, 1 - c), device_id_type=MESH)
                for a in range(n) for q in range(4)]

    def first(g, got, scratch):
        for cp in copies(g, got, scratch):
            cp.start()

    def last(g, got, scratch):
        for cp in copies(g, got, scratch):
            cp.wait()

    return _Rider(grads, [jax.ShapeDtypeStruct((4,) + g.shape[1:], g.dtype) for g in grads],
                  [pltpu.SemaphoreType.DMA((n, 4))] * 2, first, last)


def _chips_rider(sums):
    n = len(sums)

    def copies(s, out, scratch):
        send_sems, recv_sems = scratch
        x, y, c = _pos()
        cps = []
        for a in range(n):
            for f in (1, 2, 3):
                peer = (_flip(x, f >> 1), _flip(y, f & 1), c)
                cps.append(pltpu.make_async_remote_copy(
                    src_ref=s[a].at[2 * peer[0] + peer[1]], dst_ref=out[a].at[f - 1], send_sem=send_sems.at[a, f - 1],
                    recv_sem=recv_sems.at[a, f - 1], device_id=peer, device_id_type=MESH))
        return cps

    def first(s, out, scratch):
        for cp in copies(s, out, scratch):
            cp.start()

    def last(s, out, scratch):
        for cp in copies(s, out, scratch):
            cp.wait()

    return _Rider(sums, [jax.ShapeDtypeStruct((3,) + s.shape[1:], s.dtype) for s in sums],
                  [pltpu.SemaphoreType.DMA((n, 3))] * 2, first, last)


def _alone(rider, name):
    ri, ro = len(rider.ins), len(rider.out_shapes)

    def body(*refs):
        theirs = (refs[:ri], refs[ri:ri + ro], refs[ri + ro:])
        rider.first(*theirs)
        if rider.middle is not None:
            rider.middle(*theirs)
        rider.last(*theirs)

    return pl.pallas_call(body, name=name, in_specs=[ANY] * ri, out_specs=[ANY] * ro, out_shape=rider.out_shapes,
                          scratch_shapes=rider.scratch)(*rider.ins)


def _gather_small(g_w1, g_w2, g_w3, g_lb, g_wn, loss):
    def body(w1_ref, w2_ref, w3_ref, lb_ref, wn_ref, loss_ref, out_ref, pk, send_sems, recv_sems):
        x, y, c = _pos()
        me = 4 * x + 2 * y + c
        pk[...] = jnp.zeros_like(pk)
        pk[0:1, :] = w1_ref[...]
        pk[1:2, :] = w2_ref[...]
        pk[2:3, :] = w3_ref[...]
        pk[3:4, 0:HGRN_W] = lb_ref[...]
        pk[3:4, HGRN_W:2 * HGRN_W] = wn_ref[...]
        pk[4:5, 0:128] = loss_ref[...]
        out_ref[me] = pk[...]
        sends, recvs = [], []
        for k in range(1, N_DEV):
            peer = (_flip(x, k >> 2), _flip(y, (k >> 1) & 1), _flip(c, k & 1))
            cp = pltpu.make_async_remote_copy(src_ref=pk, dst_ref=out_ref.at[me], send_sem=send_sems.at[k - 1],
                                              recv_sem=recv_sems.at[k - 1], device_id=peer, device_id_type=MESH)
            cp.start()
            sends.append(cp)
            recvs.append(pltpu.make_async_remote_copy(src_ref=pk, dst_ref=out_ref.at[4 * peer[0] + 2 * peer[1] + peer[2]],
                                                      send_sem=send_sems.at[k - 1], recv_sem=recv_sems.at[k - 1], device_id=peer,
                                                      device_id_type=MESH))
        for cp in recvs:
            cp.wait_recv()
        for cp in sends:
            cp.wait_send()

    return pl.pallas_call(
        body, name="gather_small", in_specs=[VMEM_SPEC] * 6, out_specs=VMEM_SPEC,
        out_shape=jax.ShapeDtypeStruct((N_DEV, 8, D_MODEL), f32),
        scratch_shapes=[pltpu.VMEM((8, D_MODEL), f32), pltpu.SemaphoreType.DMA((N_DEV - 1,)), pltpu.SemaphoreType.DMA((N_DEV - 1,))],
    )(g_w1, g_w2, g_w3, g_lb, g_wn, loss)


def _row_tile(r):
    return max(t for t in range(8, 257, 8) if r % t == 0)


def _add_sibling(core, g, got, name):
    _, r, c = got.shape
    tr = _row_tile(r)

    def body(core_ref, a_ref, b_ref, o_ref):
        o_ref[...] = (a_ref[...] + b_ref[...]).astype(bf16)

    blk = pl.BlockSpec((1, tr, c), lambda q, i, core_ref: (q, i, 0))
    return pl.pallas_call(
        body, name=name, out_shape=jax.ShapeDtypeStruct(got.shape, bf16),
        grid_spec=pltpu.PrefetchScalarGridSpec(
            num_scalar_prefetch=1, grid=(4, r // tr),
            in_specs=[pl.BlockSpec((1, tr, c), lambda q, i, core_ref: (2 * q + core_ref[0], i, 0)), blk], out_specs=blk),
        compiler_params=_cp("arbitrary", "arbitrary"))(core, g, got)


def _adamw(w, g, m, v):
    m = ADAM_B1 * m + (1.0 - ADAM_B1) * g
    v = ADAM_B2 * v + (1.0 - ADAM_B2) * (g * g)
    m_hat = m / (1.0 - ADAM_B1 ** ADAM_STEP)
    v_hat = v / (1.0 - ADAM_B2 ** ADAM_STEP)
    return -ADAM_LR * (m_hat / (jnp.sqrt(v_hat) + ADAM_EPS) + ADAM_WD * w), m, v


def _adam_shard(where, g, got, pieces, w, m, v, name):
    r, c = w.shape
    tr = _row_tile(r)

    def body(where_ref, g_ref, got_ref, p_ref, w_ref, m_ref, v_ref, g_out, d_out, m_out, v_out):
        gsum = g_ref[0] + got_ref[0]
        for f in range(3):
            gsum = gsum + p_ref[f].astype(f32)
        g_out[...] = gsum
        d_out[...], m_out[...], v_out[...] = _adamw(w_ref[...], gsum, m_ref[...], v_ref[...])

    blk = pl.BlockSpec((tr, c), lambda i, where_ref: (i, 0))
    return pl.pallas_call(
        body, name=name, out_shape=[jax.ShapeDtypeStruct((r, c), f32)] * 4,
        grid_spec=pltpu.PrefetchScalarGridSpec(
            num_scalar_prefetch=1, grid=(r // tr,),
            in_specs=[pl.BlockSpec((1, tr, c), lambda i, where_ref: (where_ref[0], i, 0)),
                      pl.BlockSpec((1, tr, c), lambda i, where_ref: (where_ref[1], i, 0)),
                      pl.BlockSpec((3, tr, c), lambda i, where_ref: (0, i, 0)), blk, blk, blk],
            out_specs=[blk] * 4),
        compiler_params=_cp("arbitrary"),
    )(where, g, got, pieces, w, m, v)


def _small_update(gath, params):
    def body(gath_ref, *refs):
        ins, outs = refs[:15], refs[15:]
        gs = gath_ref[0]
        for k in range(1, N_DEV):
            gs = gs + gath_ref[k]
        outs[0][...] = gs[4:5, 0:128]
        l0, l1 = ins[9][0:1, :], ins[9][1:2, :]
        lb = _sigmoid(l0 - l1)
        d0 = gs[3:4, 0:HGRN_W] * lb * (1.0 - lb)
        first_row = lax.broadcasted_iota(jnp.int32, (2, HGRN_W), 0) == 0
        grads = [gs[0:1, :], gs[1:2, :], gs[2:3, :], jnp.where(first_row, d0, -d0), gs[3:4, HGRN_W:2 * HGRN_W]]
        for i, g in enumerate(grads):
            w_ref, m_ref, v_ref = ins[3 * i:3 * i + 3]
            o = outs[1 + 4 * i:5 + 4 * i]
            o[0][...] = g
            o[1][...], o[2][...], o[3][...] = _adamw(w_ref[...], g, m_ref[...], v_ref[...])

    flat = [a for p in params for a in p]
    out_shape = [jax.ShapeDtypeStruct((1, 128), f32)] + [jax.ShapeDtypeStruct(p[0].shape, f32) for p in params for _ in range(4)]
    outs = pl.pallas_call(body, name="small_update", in_specs=[VMEM_SPEC] * 16, out_specs=[VMEM_SPEC] * 21, out_shape=out_shape)(gath, *flat)
    return outs[0], [outs[1 + 4 * i:5 + 4 * i] for i in range(5)]


def kernel(x, norm1_w, w_in, lb_logits, hgrn_norm_w, w_out, norm2_w, w_gate_up, w_down, final_norm_w, loss_target, m_norm1_w, m_w_in, m_lb_logits, m_hgrn_norm_w, m_w_out, m_norm2_w, m_w_gate_up, m_w_down, m_final_norm_w, v_norm1_w, v_w_in, v_lb_logits, v_hgrn_norm_w, v_w_out, v_norm2_w, v_w_gate_up, v_w_down, v_final_norm_w):
    row = lambda a: a.reshape(1, D_MODEL)
    by_owner = lambda g, w: jnp.transpose(g.reshape(g.shape[0], g.shape[1] // w, w), (1, 0, 2))
    ix, iy, ic = lax.axis_index("x"), lax.axis_index("y"), lax.axis_index("c")
    core = jnp.stack([ic]).astype(jnp.int32)
    where = jnp.stack([4 * ix + 2 * iy + ic, 2 * ix + iy]).astype(jnp.int32)
    xs, tgt, w3 = x[0], loss_target[0], row(final_norm_w)
    S = xs.shape[0]

    (win_g,) = _alone(_gather_rider([w_in[0].astype(bf16)]), "gather_w_in")
    win = jnp.transpose(win_g, (1, 0, 2)).reshape(D_MODEL, IN_W)
    cos_t, sg_t = _rope_tables(S)
    u, qkv, hp = _in_proj(xs, norm1_w, win, cos_t, sg_t)
    ya, lse, wout_g, wgu_g, wdown_g = _attn_fwd(qkv, _gather_rider([w_out[0].astype(bf16), w_gate_up[0].astype(bf16),
                                                                     w_down[0].astype(bf16)]))
    wout = wout_g.reshape(D_MODEL, D_MODEL)
    wgu = jnp.transpose(wgu_g, (1, 0, 2)).reshape(D_MODEL, 2 * FFN)
    wdown = wdown_g.reshape(FFN, D_MODEL)
    yb, o_sav, states = _hgrn_fwd(hp, lb_logits, hgrn_norm_w)
    h1, u2, mixed = _out_proj(xs, ya, yb, wout, norm2_w)
    gate, up, act = _gate_up(u2, wgu)
    dh2, loss_p, g_w3 = _down_loss(act, wdown, h1, tgt, w3)

    (g_wdown,) = _gw(act, [dh2], 512, "gw_down")
    dgate, dup = _dact(dh2, wdown, gate, up)
    g_wgu = _gw(u2, [dgate, dup], 1408, "gw_gate_up")
    early = [jnp.concatenate([by_owner(g, 2 * FFN // N_DEV) for g in g_wgu], axis=0), g_wdown.reshape(N_DEV, FFN // N_DEV, D_MODEL)]
    dh1, g_w2, dmix, *got_early = _dgu(dgate, dup, wgu, h1, norm2_w, dh2, wout, _sibling_rider(early))
    sums_early = [_add_sibling(core, g, o, f"add_sibling_{i}") for i, (g, o) in enumerate(zip(early, got_early))]
    (g_wout,) = _gw(mixed, [dh1], 1024, "gw_out")
    dhq, dhf, dhi, dhg, g_wn, g_lb, *pieces_early = _hgrn_bwd(hp, lb_logits, hgrn_norm_w, o_sav, states, dmix, _chips_rider(sums_early))
    datt = _attn_bwd(qkv, ya, lse, dmix)
    dproj, gx, g_w1 = _din(*datt, dhq, dhf, dhi, dhg, cos_t, sg_t, win, xs, norm1_w, dh1)
    (g_win,) = _gw(u, [dproj], 896, "gw_in")
    late = [by_owner(g_win, IN_W // N_DEV), g_wout.reshape(N_DEV, D_MODEL // N_DEV, D_MODEL)]
    got_late = _alone(_sibling_rider(late), "reduce_sibling")
    sums_late = [_add_sibling(core, g, o, f"add_sibling_{2 + i}") for i, (g, o) in enumerate(zip(late, got_late))]
    pieces_late = _alone(_chips_rider(sums_late), "reduce_chips")

    grads = [late[0], late[1], early[0], early[1]]
    got = [got_late[0], got_late[1], got_early[0], got_early[1]]
    pieces = [pieces_late[0], pieces_late[1], pieces_early[0], pieces_early[1]]
    shards = [w_in[0], w_out[0], w_gate_up[0], w_down[0]]
    moms = [(m_w_in[0], v_w_in[0]), (m_w_out[0], v_w_out[0]), (m_w_gate_up[0], v_w_gate_up[0]), (m_w_down[0], v_w_down[0])]
    big = [_adam_shard(where, g, o, p, w, m, v, f"adam_{i}")
           for i, (g, o, p, w, (m, v)) in enumerate(zip(grads, got, pieces, shards, moms))]
    big = [[a[None] for a in four] for four in big]

    gath = _gather_small(g_w1, g_w2, g_w3, g_lb, g_wn, loss_p)
    params = [(norm1_w, m_norm1_w, v_norm1_w), (norm2_w, m_norm2_w, v_norm2_w),
              (row(final_norm_w), row(m_final_norm_w), row(v_final_norm_w)),
              (lb_logits, m_lb_logits, v_lb_logits), (hgrn_norm_w, m_hgrn_norm_w, v_hgrn_norm_w)]
    loss, (s_w1, s_w2, s_w3, s_lb, s_wn) = _small_update(gath, params)
    s_w3 = [a.reshape(D_MODEL) for a in s_w3]
    per_w = [s_w1, big[0], s_lb, s_wn, big[1], s_w2, big[2], big[3], s_w3]
    return (loss[0, 0], gx[None], *[p[0] for p in per_w], *[p[1] for p in per_w], *[p[2] for p in per_w], *[p[3] for p in per_w])
```

```python
import jax
import jax.numpy as jnp
from jax import lax
from jax.experimental import pallas as pl
from jax.experimental.pallas import tpu as pltpu

f32, bf16 = jnp.float32, jnp.bfloat16

D_MODEL = 1024
ATTN_W = 512
HEAD_DIM = 64
ATTN_BLK = 128
DILATIONS = (1, 4, 16)
HGRN_W = 512
HGRN_HD = 128
CHUNK = 16
IN_W = 3 * ATTN_W + 4 * HGRN_W
FFN = 2816
EPS = 1e-6
ROPE_THETA = 10000.0
NEG = -1e30
N_DEV = 8
ADAM_LR, ADAM_B1, ADAM_B2, ADAM_EPS, ADAM_WD, ADAM_STEP = 0.001, 0.9, 0.999, 1e-08, 0.01, 10
VMEM_LIMIT = 56 * 1024 * 1024


def _cp(*sem):
    return pltpu.CompilerParams(dimension_semantics=sem, vmem_limit_bytes=VMEM_LIMIT)


def _dot(a, b):
    return jnp.dot(a, b, preferred_element_type=f32)


def _dot_nt(a, b):
    return lax.dot_general(a, b, (((1,), (1,)), ((), ())), preferred_element_type=f32)


def _dot_tn(a, b):
    return lax.dot_general(a, b, (((0,), (0,)), ((), ())), preferred_element_type=f32)


def _sigmoid(x):
    return 0.5 * jnp.tanh(0.5 * x) + 0.5


class _Rider:
    def __init__(self, ins, out_shapes, scratch, first, last, middle=None):
        self.ins, self.out_shapes, self.scratch = list(ins), list(out_shapes), list(scratch)
        self.first, self.middle, self.last = first, middle, last


def _ride(call, rider, body, step, n_steps, n_in, n_out, n_scratch):
    if rider is None:
        return call, body, []
    ri, ro = len(rider.ins), len(rider.out_shapes)
    any_spec = pl.BlockSpec(memory_space=pl.ANY)
    call = dict(call, in_specs=call["in_specs"] + [any_spec] * ri, out_specs=call["out_specs"] + [any_spec] * ro,
                out_shape=call["out_shape"] + rider.out_shapes, scratch_shapes=call["scratch_shapes"] + rider.scratch)

    def riding(*refs):
        a = n_in + ri
        b = a + n_out + ro
        mine = refs[:n_in] + refs[a:a + n_out] + refs[b:b + n_scratch]
        theirs = (refs[n_in:a], refs[a + n_out:b], refs[b + n_scratch:])
        t = step()

        @pl.when(t == 0)
        def _():
            rider.first(*theirs)

        body(*mine)
        if rider.middle is not None:
            @pl.when(t == n_steps // 2)
            def _():
                rider.middle(*theirs)

        @pl.when(t == n_steps - 1)
        def _():
            rider.last(*theirs)

    return call, riding, rider.ins


def _rope_tables(S):
    half = HEAD_DIM // 2
    inv_freq = ROPE_THETA ** (-jnp.arange(half, dtype=f32) / half)
    ang = jnp.arange(S, dtype=f32)[:, None] * inv_freq[None, :]
    cos, sin = jnp.cos(ang), jnp.sin(ang)
    return jnp.concatenate([cos, cos, cos, cos], axis=1), jnp.concatenate([-sin, sin, -sin, sin], axis=1)


def _swap_halves(v):
    n = v.shape[1]
    lane = lax.broadcasted_iota(jnp.int32, v.shape, 1)
    return jnp.where((lane % HEAD_DIM) < HEAD_DIM // 2, pltpu.roll(v, n - HEAD_DIM // 2, 1), pltpu.roll(v, HEAD_DIM // 2, 1))


def _in_proj(x, w1, win, cos_t, sg_t):
    S = x.shape[0]
    tm = 256

    def body(x_ref, w1_ref, w_ref, cos_ref, sg_ref, u_ref, qkv_ref, hp_ref):
        xv = x_ref[...]
        r = lax.rsqrt(jnp.mean(xv * xv, axis=-1, keepdims=True) + EPS)
        u = (xv * r * w1_ref[...]).astype(bf16)
        u_ref[...] = u
        cosv, sgv = jnp.tile(cos_ref[...], (1, ATTN_W // 128)), jnp.tile(sg_ref[...], (1, ATTN_W // 128))
        for j in range(3):
            pj = _dot(u, w_ref[:, j * ATTN_W:(j + 1) * ATTN_W])
            if j < 2:
                pj = pj * cosv + _swap_halves(pj) * sgv
            if j == 0:
                pj = pj * (HEAD_DIM ** -0.5)
            qkv_ref[:, j * ATTN_W:(j + 1) * ATTN_W] = pj.astype(bf16)
        for j in range(4):
            lo = 3 * ATTN_W + j * HGRN_W
            hp_ref[:, j * HGRN_W:(j + 1) * HGRN_W] = _dot(u, w_ref[:, lo:lo + HGRN_W])

    return pl.pallas_call(
        body, name="in_proj", grid=(S // tm,),
        in_specs=[pl.BlockSpec((tm, D_MODEL), lambda i: (i, 0)), pl.BlockSpec((1, D_MODEL), lambda i: (0, 0)),
                  pl.BlockSpec((D_MODEL, IN_W), lambda i: (0, 0)),
                  pl.BlockSpec((tm, 128), lambda i: (i, 0)), pl.BlockSpec((tm, 128), lambda i: (i, 0))],
        out_specs=[pl.BlockSpec((tm, D_MODEL), lambda i: (i, 0)), pl.BlockSpec((tm, 3 * ATTN_W), lambda i: (i, 0)),
                   pl.BlockSpec((tm, 4 * HGRN_W), lambda i: (i, 0))],
        out_shape=[jax.ShapeDtypeStruct((S, D_MODEL), bf16), jax.ShapeDtypeStruct((S, 3 * ATTN_W), bf16),
                   jax.ShapeDtypeStruct((S, 4 * HGRN_W), f32)],
        compiler_params=_cp("arbitrary"),
    )(x, w1, win, cos_t, sg_t)


def _head_masks():
    lane = lax.broadcasted_iota(jnp.int32, (ATTN_BLK, 128), 1)
    even = lane < HEAD_DIM
    return even, (even, jnp.logical_not(even))


def _pair_fwd(q2, k2, v2, bias):
    even, masks = _head_masks()
    outs, lses = [], []
    for e in range(2):
        qm = jnp.where(masks[e], q2, 0.0).astype(bf16)
        s = _dot_nt(qm, k2) + bias
        m = jnp.max(s, axis=-1, keepdims=True)
        pe = jnp.exp(s - m)
        lsum = jnp.sum(pe, axis=-1, keepdims=True)
        outs.append(_dot(pe.astype(bf16), v2) / lsum)
        lses.append(jnp.broadcast_to(m + jnp.log(lsum), (ATTN_BLK, 128)))
    return jnp.where(even, outs[0], outs[1]), jnp.where(even, lses[0], lses[1])


def _merge(y0, l0, y1, l1):
    mx = jnp.maximum(l0, l1)
    a, b = jnp.exp(l0 - mx), jnp.exp(l1 - mx)
    tot = a + b
    return (a * y0 + b * y1) / tot, mx + jnp.log(tot)


def _pair_bwd(q2, k2f, v2, dy2, lse2, delta2, bias):
    _, masks = _head_masks()
    k2 = k2f.astype(bf16)
    klane = lax.broadcasted_iota(jnp.int32, (2 * ATTN_BLK, 128), 1) < HEAD_DIM
    kmasks = (klane, jnp.logical_not(klane))
    dq2 = jnp.zeros((ATTN_BLK, 128), f32)
    pes, dss, qms, dyms = [], [], [], []
    for e in range(2):
        c0 = e * HEAD_DIM
        qm = jnp.where(masks[e], q2, 0.0).astype(bf16)
        km = jnp.where(kmasks[e], k2f, 0.0).astype(bf16)
        dym = jnp.where(masks[e], dy2, 0.0).astype(bf16)
        pe = jnp.exp(_dot_nt(qm, k2) + bias - lse2[:, c0:c0 + 1])
        ds = (pe * (_dot_nt(dym, v2) - delta2[:, c0:c0 + 1])).astype(bf16)
        dq2 = dq2 + _dot(ds, km)
        pes.append(pe.astype(bf16))
        dss.append(ds)
        qms.append(qm)
        dyms.append(dym)
    dv2 = _dot_tn(jnp.concatenate(pes, axis=0), jnp.concatenate(dyms, axis=0))
    dk2 = _dot_tn(jnp.concatenate(dss, axis=0), jnp.concatenate(qms, axis=0))
    return dq2, dk2, dv2


TOK = 2048


def _key_bias():
    qi = lax.broadcasted_iota(jnp.int32, (ATTN_BLK, 2 * ATTN_BLK), 0)
    kj = lax.broadcasted_iota(jnp.int32, (ATTN_BLK, 2 * ATTN_BLK), 1)
    delta = ATTN_BLK + qi - kj
    seen = (delta >= 0) & (delta <= ATTN_BLK)
    return jnp.where(seen, 0.0, NEG), jnp.where(seen & (kj >= ATTN_BLK), 0.0, NEG)


def _attn_fwd(qkv, rider=None):
    S = qkv.shape[0]
    nS = S // TOK

    def body(q_ref, kp_ref, kc_ref, vp_ref, vc_ref, y_ref, l_ref, qs, k2, v2, ay, al):
        n = pl.program_id(1)
        qs[...] = q_ref[...].astype(f32)
        k2[0:TOK] = kp_ref[...].astype(f32)
        k2[TOK:2 * TOK] = kc_ref[...].astype(f32)
        v2[0:TOK] = vp_ref[...].astype(f32)
        v2[TOK:2 * TOK] = vc_ref[...].astype(f32)
        bias_any, bias_first = _key_bias()

        def block(dil, r, b, step, last):
            start = r + pl.multiple_of(step * b, step)
            rows = pl.ds(start, ATTN_BLK, stride=dil) if dil > 1 else pl.ds(start, ATTN_BLK)
            keys = (pl.ds(TOK + start - step, 2 * ATTN_BLK, stride=dil) if dil > 1
                    else pl.ds(TOK + start - step, 2 * ATTN_BLK))
            bias = jnp.where((n == 0) & (b == 0), bias_first, bias_any)
            out, lse = _pair_fwd(qs[rows, :], k2[keys, :].astype(bf16), v2[keys, :].astype(bf16), bias)
            if dil < DILATIONS[-1]:
                out, lse = _merge(ay[rows, :], al[rows, :], out, lse)
            if last:
                y_ref[rows, :] = out
                l_ref[rows, :] = lse
            else:
                ay[rows, :] = out
                al[rows, :] = lse

        for dil in reversed(DILATIONS):
            def loop(i, carry, dil=dil):
                block(dil, i % dil, i // dil, ATTN_BLK * dil, dil == 1)
                return carry
            lax.fori_loop(0, TOK // ATTN_BLK, loop, 0, unroll=4)

    blk = (TOK, 128)
    cur = lambda c: pl.BlockSpec(blk, lambda p, n: (n, 4 * c + p))
    prv = lambda c: pl.BlockSpec(blk, lambda p, n: (jnp.maximum(n - 1, 0), 4 * c + p))
    out = pl.BlockSpec(blk, lambda p, n: (n, p))
    call = dict(in_specs=[cur(0), prv(1), cur(1), prv(2), cur(2)], out_specs=[out, out],
                out_shape=[jax.ShapeDtypeStruct((S, ATTN_W), f32)] * 2,
                scratch_shapes=[pltpu.VMEM(blk, f32), pltpu.VMEM((2 * TOK, 128), f32), pltpu.VMEM((2 * TOK, 128), f32),
                                pltpu.VMEM(blk, f32), pltpu.VMEM(blk, f32)])
    call, body, more = _ride(call, rider, body, lambda: pl.program_id(0) * nS + pl.program_id(1), (ATTN_W // 128) * nS, 5, 2, 5)
    return pl.pallas_call(body, name="attention_fwd", grid=(ATTN_W // 128, nS), compiler_params=_cp("arbitrary", "arbitrary"),
                          **call)(qkv, qkv, qkv, qkv, qkv, *more)


def _attn_bwd(qkv, ya, lse, dmix):
    S = qkv.shape[0]
    nS = S // TOK

    def body(q_ref, kp_ref, kc_ref, vp_ref, vc_ref, y_ref, l_ref, dy_ref, dq_ref, dk_ref, dv_ref, qs, k2, v2, dk2, dv2, dqa, dl):
        n = pl.program_id(1)

        @pl.when(n == 0)
        def _():
            dk2[...] = jnp.zeros_like(dk2)
            dv2[...] = jnp.zeros_like(dv2)

        @pl.when(n < nS)
        def _():
            qs[...] = q_ref[...].astype(f32)
            k2[0:TOK] = kp_ref[...].astype(f32)
            k2[TOK:2 * TOK] = kc_ref[...].astype(f32)
            v2[0:TOK] = vp_ref[...].astype(f32)
            v2[TOK:2 * TOK] = vc_ref[...].astype(f32)
            li = lax.broadcasted_iota(jnp.int32, (128, 128), 0)
            lj = lax.broadcasted_iota(jnp.int32, (128, 128), 1)
            seg = jnp.where((li // HEAD_DIM) == (lj // HEAD_DIM), 1.0, 0.0).astype(bf16)
            bias_any, bias_first = _key_bias()

            def delta_rows(t, carry):
                rows = pl.ds(pl.multiple_of(256 * t, 256), 256)
                dyy = dy_ref[rows, :] * y_ref[rows, :]
                hi = dyy.astype(bf16)
                dl[rows, :] = _dot(hi, seg) + _dot((dyy - hi.astype(f32)).astype(bf16), seg)
                return carry

            lax.fori_loop(0, TOK // 256, delta_rows, 0)

            def block(dil, r, b, step, first_pattern, last):
                start = r + pl.multiple_of(step * b, step)
                rows = pl.ds(start, ATTN_BLK, stride=dil) if dil > 1 else pl.ds(start, ATTN_BLK)
                keys = (pl.ds(TOK + start - step, 2 * ATTN_BLK, stride=dil) if dil > 1
                        else pl.ds(TOK + start - step, 2 * ATTN_BLK))
                bias = jnp.where((n == 0) & (b == 0), bias_first, bias_any)
                dq2, dkk, dvv = _pair_bwd(qs[rows, :], k2[keys, :], v2[keys, :].astype(bf16), dy_ref[rows, :],
                                          l_ref[rows, :], dl[rows, :], bias)
                if last:
                    dq_ref[rows, :] = dqa[rows, :] + dq2
                elif first_pattern:
                    dqa[rows, :] = dq2
                else:
                    dqa[rows, :] += dq2
                dk2[keys, :] += dkk
                dv2[keys, :] += dvv

            for dil in reversed(DILATIONS):
                def loop(i, carry, dil=dil):
                    block(dil, i % dil, i // dil, ATTN_BLK * dil, dil == DILATIONS[-1], dil == 1)
                    return carry
                lax.fori_loop(0, TOK // ATTN_BLK, loop, 0, unroll=4)

        dk_ref[...] = dk2[0:TOK]
        dv_ref[...] = dv2[0:TOK]
        dk2[0:TOK] = dk2[TOK:2 * TOK]
        dv2[0:TOK] = dv2[TOK:2 * TOK]
        dk2[TOK:2 * TOK] = jnp.zeros((TOK, 128), f32)
        dv2[TOK:2 * TOK] = jnp.zeros((TOK, 128), f32)

    blk = (TOK, 128)
    cn = lambda n: jnp.minimum(n, nS - 1)
    pn = lambda n: jnp.clip(n - 1, 0, nS - 1)
    cur = lambda c: pl.BlockSpec(blk, lambda p, n: (cn(n), 4 * c + p))
    prv = lambda c: pl.BlockSpec(blk, lambda p, n: (pn(n), 4 * c + p))
    at_n = pl.BlockSpec(blk, lambda p, n: (cn(n), p))
    at_p = pl.BlockSpec(blk, lambda p, n: (pn(n), p))
    big = lambda: pltpu.VMEM((2 * TOK, 128), f32)
    return pl.pallas_call(
        body, name="attention_bwd", grid=(ATTN_W // 128, nS + 1),
        in_specs=[cur(0), prv(1), cur(1), prv(2), cur(2), at_n, at_n, at_n], out_specs=[at_n, at_p, at_p],
        out_shape=[jax.ShapeDtypeStruct((S, ATTN_W), f32)] * 3,
        scratch_shapes=[pltpu.VMEM(blk, f32), big(), big(), big(), big(), pltpu.VMEM(blk, f32), pltpu.VMEM(blk, f32)],
        compiler_params=_cp("arbitrary", "arbitrary"),
    )(qkv, qkv, qkv, qkv, qkv, ya, lse, dmix)


HG_T = 256
N_HH = HGRN_W // HGRN_HD
HG_SUB = 128
SAFE_RANGE = 80.0


def _row_in_chunk():
    return lax.broadcasted_iota(jnp.int32, (HG_T, HGRN_HD), 0) % CHUNK


def _chunk_cumsum(v, rc):
    for k in (1, 2, 4, 8):
        v = v + jnp.where(rc >= k, pltpu.roll(v, k, 0), 0.0)
    return v


def _chunk_rcumsum(v, rc):
    for k in (1, 2, 4, 8):
        v = v + jnp.where(rc < CHUNK - k, pltpu.roll(v, HG_T - k, 0), 0.0)
    return v


def _hgrn_gates(qb, fb, lb):
    sf = _sigmoid(fb)
    f = lb + (1.0 - lb) * sf
    sq = _sigmoid(qb)
    return sf, f, jnp.log(f), 1.0 - f, sq, qb * sq


def _hgrn_prep(qb, fb, lbl2, rc):
    lb = _sigmoid(lbl2[0:1, :] - lbl2[1:2, :])
    sf, f, lf, key, sq, qf = _hgrn_gates(qb, fb, lb)
    b = _chunk_cumsum(lf, rc)
    rem = _chunk_rcumsum(lf, rc) - lf
    return dict(lb=lb, sf=sf, f=f, key=key, sq=sq, qf=qf, b=b, rem=rem, eb=jnp.exp(b), er=jnp.exp(rem))


def _chunk_mask():
    r = lax.broadcasted_iota(jnp.int32, (HG_SUB, HG_SUB), 0)
    c = lax.broadcasted_iota(jnp.int32, (HG_SUB, HG_SUB), 1)
    return ((r // CHUNK) == (c // CHUNK)) & (c <= r)


def _hgrn_fwd(hp, lbl, wn):
    S = hp.shape[0]
    nT = S // HG_T

    def body(qb_ref, fb_ref, ib_ref, gb_ref, lbl_ref, wn_ref, yb_ref, o_ref, st_ref, ST, qt_s, kh_s, dec_s, oi_s):
        @pl.when(pl.program_id(0) == 0)
        def _():
            ST[...] = jnp.zeros_like(ST)

        rc = _row_in_chunk()
        for h in range(N_HH):
            sl = slice(HGRN_HD * h, HGRN_HD * (h + 1))
            p = _hgrn_prep(qb_ref[:, sl], fb_ref[:, sl], lbl_ref[:, sl], rc)
            qf, key, b = p["qf"], p["key"], p["b"]
            qt = qf * p["eb"]
            qt_s[:, sl] = qt.astype(bf16)
            kh_s[:, sl] = (key * p["er"]).astype(bf16)
            dec_s[:, sl] = jnp.exp(b + p["rem"])
            rng = jnp.max(-(b + p["rem"]))

            @pl.when(rng < SAFE_RANGE)
            def _():
                kp = (key * jnp.exp(-b)).astype(bf16)
                cmask = _chunk_mask()
                for j in range(HG_T // HG_SUB):
                    rs = slice(HG_SUB * j, HG_SUB * (j + 1))
                    sc = jnp.where(cmask, _dot_nt(qt[rs].astype(bf16), kp[rs]), 0.0).astype(bf16)
                    oi_s[rs, sl] = _dot(sc, ib_ref[rs, sl].astype(bf16))

            @pl.when(rng >= SAFE_RANGE)
            def _():
                v = ib_ref[:, sl]
                ones = jnp.ones((HGRN_HD, HGRN_HD), bf16)
                o = jnp.zeros((HG_T, HGRN_HD), f32)
                for l in range(CHUNK):
                    if l == 0:
                        pr, vs = qf * key, v
                    else:
                        e = jnp.exp(jnp.where(rc >= l, b - pltpu.roll(b, l, 0), NEG))
                        pr, vs = qf * pltpu.roll(key, l, 0) * e, pltpu.roll(v, l, 0)
                    o = o + _dot(pr.astype(bf16), ones) * vs
                oi_s[:, sl] = o

        def step(c, carry):
            rows = pl.ds(pl.multiple_of(c * CHUNK, CHUNK), CHUNK)
            row0 = pl.ds(pl.multiple_of(c * CHUNK, CHUNK), 1)
            for h in range(N_HH):
                sl = slice(HGRN_HD * h, HGRN_HD * (h + 1))
                stv = ST[h]
                st_ref[c, sl, :] = stv
                oi_s[rows, sl] += _dot_nt(qt_s[rows, sl], stv.astype(bf16))
                ST[h] = stv * dec_s[row0, sl] + _dot_tn(ib_ref[rows, sl].astype(bf16), kh_s[rows, sl])
            return carry

        lax.fori_loop(0, HG_T // CHUNK, step, 0, unroll=8)

        for h in range(N_HH):
            sl = slice(HGRN_HD * h, HGRN_HD * (h + 1))
            o = oi_s[:, sl]
            o_ref[:, sl] = o
            on = o * lax.rsqrt(jnp.mean(o * o, axis=-1, keepdims=True) + EPS)
            g = gb_ref[:, sl]
            yb_ref[:, sl] = on * wn_ref[:, sl] * (g * _sigmoid(g))

    col = lambda c: pl.BlockSpec((HG_T, HGRN_W), lambda i: (i, c))
    tile = pl.BlockSpec((HG_T, HGRN_W), lambda i: (i, 0))
    whole = lambda a: pl.BlockSpec(a.shape, lambda i: (0, 0))
    return pl.pallas_call(
        body, name="hgrn_fwd", grid=(nT,),
        in_specs=[col(0), col(1), col(2), col(3), whole(lbl), whole(wn)],
        out_specs=[tile, tile, pl.BlockSpec((HG_T // CHUNK, HGRN_W, HGRN_HD), lambda i: (i, 0, 0))],
        out_shape=[jax.ShapeDtypeStruct((S, HGRN_W), f32), jax.ShapeDtypeStruct((S, HGRN_W), f32),
                   jax.ShapeDtypeStruct((S // CHUNK, HGRN_W, HGRN_HD), f32)],
        scratch_shapes=[pltpu.VMEM((N_HH, HGRN_HD, HGRN_HD), f32), pltpu.VMEM((HG_T, HGRN_W), bf16),
                        pltpu.VMEM((HG_T, HGRN_W), bf16), pltpu.VMEM((HG_T, HGRN_W), f32), pltpu.VMEM((HG_T, HGRN_W), f32)],
        compiler_params=_cp("arbitrary"),
    )(hp, hp, hp, hp, lbl, wn)


def _hgrn_bwd(hp, lbl, wn, o_sav, states, dmix, rider=None):
    S = hp.shape[0]
    nT = S // HG_T

    def body(qb_ref, fb_ref, ib_ref, gb_ref, lbl_ref, wn_ref, o_ref, st_ref, dy_ref,
             dq_ref, df_ref, di_ref, dg_ref, gwn_ref, glb_ref,
             DST, qt_s, kh_s, dec_s, do_s, dqt_s, dkh_s, dbl_s, dvi_s, dqi_s, dki_s, dbi_s):
        @pl.when(pl.program_id(0) == 0)
        def _():
            DST[...] = jnp.zeros_like(DST)
            gwn_ref[...] = jnp.zeros_like(gwn_ref)
            glb_ref[...] = jnp.zeros_like(glb_ref)

        rc = _row_in_chunk()
        preps = []
        for h in range(N_HH):
            sl = slice(HGRN_HD * h, HGRN_HD * (h + 1))
            p = _hgrn_prep(qb_ref[:, sl], fb_ref[:, sl], lbl_ref[:, sl], rc)
            preps.append(p)
            qf, key, b = p["qf"], p["key"], p["b"]
            v = ib_ref[:, sl]
            o = o_ref[:, sl]
            rinv = lax.rsqrt(jnp.mean(o * o, axis=-1, keepdims=True) + EPS)
            on = o * rinv
            g = gb_ref[:, sl]
            sgm = _sigmoid(g)
            silu_g = g * sgm
            dy = dy_ref[:, sl]
            wn_v = wn_ref[:, sl]
            gwn_ref[:, sl] += jnp.sum(dy * on * silu_g, axis=0, keepdims=True)
            dg_ref[:, sl] = (dy * on * wn_v * (sgm * (1.0 + g * (1.0 - sgm)))).astype(bf16)
            t1 = dy * wn_v * silu_g
            do = rinv * (t1 - on * jnp.mean(t1 * on, axis=-1, keepdims=True))
            do_s[:, sl] = do.astype(bf16)
            qt = qf * p["eb"]
            qt_s[:, sl] = qt.astype(bf16)
            kh_s[:, sl] = (key * p["er"]).astype(bf16)
            dec_s[:, sl] = jnp.exp(b + p["rem"])
            rng = jnp.max(-(b + p["rem"]))

            @pl.when(rng < SAFE_RANGE)
            def _():
                einv = jnp.exp(-b)
                kp = (key * einv).astype(bf16)
                cmask = _chunk_mask()
                for j in range(HG_T // HG_SUB):
                    rs = slice(HG_SUB * j, HG_SUB * (j + 1))
                    qtb, dob, vb = qt[rs].astype(bf16), do[rs].astype(bf16), v[rs].astype(bf16)
                    sc = jnp.where(cmask, _dot_nt(qtb, kp[rs]), 0.0).astype(bf16)
                    dsc = jnp.where(cmask, _dot_nt(dob, vb), 0.0).astype(bf16)
                    dqp = _dot(dsc, kp[rs])
                    dkp = _dot_tn(dsc, qtb)
                    dvi_s[rs, sl] = _dot_tn(sc, dob)
                    dqi_s[rs, sl] = dqp * p["eb"][rs]
                    dki_s[rs, sl] = dkp * einv[rs]
                    dbi_s[rs, sl] = dqp * qtb.astype(f32) - dkp * kp[rs].astype(f32)

            @pl.when(rng >= SAFE_RANGE)
            def _():
                ones = jnp.ones((HGRN_HD, HGRN_HD), bf16)
                dqf = jnp.zeros((HG_T, HGRN_HD), f32)
                dkey, db, dv = dqf, dqf, dqf
                for l in range(CHUNK):
                    if l == 0:
                        ks, vs, qe = key, v, qf
                    else:
                        e = jnp.exp(jnp.where(rc >= l, b - pltpu.roll(b, l, 0), NEG))
                        ks, vs, qe = pltpu.roll(key, l, 0), pltpu.roll(v, l, 0), qf * e
                    pr = qe * ks
                    rl = _dot(pr.astype(bf16), ones)
                    drl = _dot((do * vs).astype(bf16), ones)
                    if l == 0:
                        dqf = dqf + drl * ks
                        dv = dv + rl * do
                        dkey = dkey + drl * qe
                    else:
                        drl = jnp.where(rc >= l, drl, 0.0)
                        gl = drl * pr
                        dqf = dqf + drl * ks * e
                        dv = dv + pltpu.roll(rl * do, HG_T - l, 0)
                        dkey = dkey + pltpu.roll(drl * qe, HG_T - l, 0)
                        db = db + gl - pltpu.roll(gl, HG_T - l, 0)
                dvi_s[:, sl] = dv
                dqi_s[:, sl] = dqf
                dki_s[:, sl] = dkey
                dbi_s[:, sl] = db

        def step(k, carry):
            c = HG_T // CHUNK - 1 - k
            rows = pl.ds(pl.multiple_of(c * CHUNK, CHUNK), CHUNK)
            row0 = pl.ds(pl.multiple_of(c * CHUNK, CHUNK), 1)
            for h in range(N_HH):
                sl = slice(HGRN_HD * h, HGRN_HD * (h + 1))
                stp = st_ref[c, sl, :]
                dst = DST[h]
                dstb = dst.astype(bf16)
                dob = do_s[rows, sl]
                khb = kh_s[rows, sl]
                dec = dec_s[row0, sl]
                dqt_s[rows, sl] = _dot(dob, stp.astype(bf16))
                dkh = _dot(ib_ref[rows, sl].astype(bf16), dstb)
                dkh_s[rows, sl] = dkh
                dvi_s[rows, sl] += _dot_nt(khb, dstb)
                dbl = jnp.sum(dst * stp, axis=0, keepdims=True) * dec + jnp.sum(dkh * khb.astype(f32), axis=0, keepdims=True)
                dbl_s[rows, sl] = jnp.broadcast_to(dbl, (CHUNK, HGRN_HD))
                DST[h] = dst * dec + _dot_tn(dob, qt_s[rows, sl])
            return carry

        lax.fori_loop(0, HG_T // CHUNK, step, 0, unroll=4)

        for h in range(N_HH):
            sl = slice(HGRN_HD * h, HGRN_HD * (h + 1))
            qb = qb_ref[:, sl]
            p = preps[h]
            sf, sq, lb = p["sf"], p["sq"], p["lb"]
            dqt, dkh = dqt_s[:, sl], dkh_s[:, sl]
            dqf = dqt * p["eb"] + dqi_s[:, sl]
            dkey = dkh * p["er"] + dki_s[:, sl]
            db = dqt * (p["qf"] * p["eb"]) - dkh * (p["key"] * p["er"]) + jnp.where(rc == CHUNK - 1, dbl_s[:, sl], 0.0) + dbi_s[:, sl]
            df = _chunk_rcumsum(db, rc) / p["f"] - dkey
            df_ref[:, sl] = (df * (1.0 - lb) * sf * (1.0 - sf)).astype(bf16)
            glb_ref[:, sl] += jnp.sum(df * (1.0 - sf), axis=0, keepdims=True)
            dq_ref[:, sl] = (dqf * (sq * (1.0 + qb * (1.0 - sq)))).astype(bf16)
            di_ref[:, sl] = dvi_s[:, sl].astype(bf16)

    rev = lambda i: nT - 1 - i
    col = lambda c: pl.BlockSpec((HG_T, HGRN_W), lambda i: (rev(i), c))
    tile = pl.BlockSpec((HG_T, HGRN_W), lambda i: (rev(i), 0))
    whole = lambda a: pl.BlockSpec(a.shape, lambda i: (0, 0))
    vec = pl.BlockSpec((1, HGRN_W), lambda i: (0, 0))
    tb = lambda: pltpu.VMEM((HG_T, HGRN_W), bf16)
    tf = lambda: pltpu.VMEM((HG_T, HGRN_W), f32)
    call = dict(in_specs=[col(0), col(1), col(2), col(3), whole(lbl), whole(wn), tile,
                          pl.BlockSpec((HG_T // CHUNK, HGRN_W, HGRN_HD), lambda i: (rev(i), 0, 0)),
                          pl.BlockSpec((HG_T, HGRN_W), lambda i: (rev(i), 1))],
                out_specs=[tile, tile, tile, tile, vec, vec],
                out_shape=[jax.ShapeDtypeStruct((S, HGRN_W), bf16)] * 4 + [jax.ShapeDtypeStruct((1, HGRN_W), f32)] * 2,
                scratch_shapes=[pltpu.VMEM((N_HH, HGRN_HD, HGRN_HD), f32), tb(), tb(), tf(), tb(), tf(), tf(), tf(), tf(), tf(),
                                tf(), tf()])
    call, body, more = _ride(call, rider, body, lambda: pl.program_id(0), nT, 9, 6, 12)
    return pl.pallas_call(body, name="hgrn_bwd", grid=(nT,), compiler_params=_cp("arbitrary"), **call)(
        hp, hp, hp, hp, lbl, wn, o_sav, states, dmix, *more)


def _out_proj(x, ya, yb, wout, w2):
    S = x.shape[0]
    tm = 512

    def body(x_ref, ya_ref, yb_ref, w_ref, w2_ref, h1_ref, u2_ref, mix_ref):
        mixed = jnp.concatenate([ya_ref[...], yb_ref[...]], axis=1).astype(bf16)
        mix_ref[...] = mixed
        h1 = x_ref[...] + _dot(mixed, w_ref[...])
        h1_ref[...] = h1
        r = lax.rsqrt(jnp.mean(h1 * h1, axis=-1, keepdims=True) + EPS)
        u2_ref[...] = (h1 * r * w2_ref[...]).astype(bf16)

    row = lambda w: pl.BlockSpec((tm, w), lambda i: (i, 0))
    return pl.pallas_call(
        body, name="out_proj", grid=(S // tm,),
        in_specs=[row(D_MODEL), row(ATTN_W), row(HGRN_W), pl.BlockSpec((D_MODEL, D_MODEL), lambda i: (0, 0)),
                  pl.BlockSpec((1, D_MODEL), lambda i: (0, 0))],
        out_specs=[row(D_MODEL), row(D_MODEL), row(D_MODEL)],
        out_shape=[jax.ShapeDtypeStruct((S, D_MODEL), f32), jax.ShapeDtypeStruct((S, D_MODEL), bf16),
                   jax.ShapeDtypeStruct((S, D_MODEL), bf16)],
        compiler_params=_cp("arbitrary"),
    )(x, ya, yb, wout, w2)


def _gate_up(u2, wgu):
    S = u2.shape[0]
    tm, tn = 512, 1408
    nj = FFN // tn

    def body(u_ref, wg_ref, wu_ref, g_ref, up_ref, a_ref):
        u = u_ref[...]
        g = _dot(u, wg_ref[...])
        up = _dot(u, wu_ref[...])
        g_ref[...] = g.astype(bf16)
        up_ref[...] = up.astype(bf16)
        a_ref[...] = (g * _sigmoid(g) * up).astype(bf16)

    out = pl.BlockSpec((tm, tn), lambda j, i: (i, j))
    return pl.pallas_call(
        body, name="gate_up", grid=(nj, S // tm),
        in_specs=[pl.BlockSpec((tm, D_MODEL), lambda j, i: (i, 0)), pl.BlockSpec((D_MODEL, tn), lambda j, i: (0, j)),
                  pl.BlockSpec((D_MODEL, tn), lambda j, i: (0, j + nj))],
        out_specs=[out, out, out],
        out_shape=[jax.ShapeDtypeStruct((S, FFN), bf16)] * 3,
        compiler_params=_cp("arbitrary", "arbitrary"),
    )(u2, wgu, wgu)


def _rms_bwd(dyw, hn, r):
    return r * (dyw - hn * jnp.mean(dyw * hn, axis=-1, keepdims=True))


def _down_loss(act, wdown, h1, tgt, w3):
    S = act.shape[0]
    tm = 256

    def body(a_ref, w_ref, h1_ref, t_ref, w3_ref, dh2_ref, loss_ref, gw3_ref):
        @pl.when(pl.program_id(0) == 0)
        def _():
            loss_ref[...] = jnp.zeros_like(loss_ref)
            gw3_ref[...] = jnp.zeros_like(gw3_ref)

        h2 = h1_ref[...] + _dot(a_ref[...], w_ref[...])
        r = lax.rsqrt(jnp.mean(h2 * h2, axis=-1, keepdims=True) + EPS)
        hn = h2 * r
        w3 = w3_ref[...]
        err = hn * w3 - t_ref[...]
        loss_ref[...] += (0.5 / D_MODEL) * jnp.sum(err * err)
        dy = err * (1.0 / D_MODEL)
        gw3_ref[...] += jnp.sum(dy * hn, axis=0, keepdims=True)
        dh2_ref[...] = _rms_bwd(dy * w3, hn, r)

    row = lambda w: pl.BlockSpec((tm, w), lambda i: (i, 0))
    return pl.pallas_call(
        body, name="down_loss", grid=(S // tm,),
        in_specs=[row(FFN), pl.BlockSpec((FFN, D_MODEL), lambda i: (0, 0)), row(D_MODEL), row(D_MODEL),
                  pl.BlockSpec((1, D_MODEL), lambda i: (0, 0))],
        out_specs=[row(D_MODEL), pl.BlockSpec((1, 128), lambda i: (0, 0)), pl.BlockSpec((1, D_MODEL), lambda i: (0, 0))],
        out_shape=[jax.ShapeDtypeStruct((S, D_MODEL), f32), jax.ShapeDtypeStruct((1, 128), f32),
                   jax.ShapeDtypeStruct((1, D_MODEL), f32)],
        compiler_params=_cp("arbitrary"),
    )(act, wdown, h1, tgt, w3)


def _dact(dh2, wdown, gate, up):
    S = dh2.shape[0]
    tm = 256

    def body(d_ref, w_ref, g_ref, u_ref, dg_ref, du_ref):
        da = _dot_nt(d_ref[...].astype(bf16), w_ref[...])
        g = g_ref[...].astype(f32)
        sg = _sigmoid(g)
        du_ref[...] = (da * g * sg).astype(bf16)
        dg_ref[...] = (da * u_ref[...].astype(f32) * (sg * (1.0 + g * (1.0 - sg)))).astype(bf16)

    row = lambda w: pl.BlockSpec((tm, w), lambda i: (i, 0))
    return pl.pallas_call(
        body, name="dact", grid=(S // tm,),
        in_specs=[row(D_MODEL), pl.BlockSpec((FFN, D_MODEL), lambda i: (0, 0)), row(FFN), row(FFN)],
        out_specs=[row(FFN), row(FFN)],
        out_shape=[jax.ShapeDtypeStruct((S, FFN), bf16)] * 2,
        compiler_params=_cp("arbitrary"),
    )(dh2, wdown, gate, up)


def _dgu(dgate, dup, wgu, h1, w2, dh2, wout, rider=None):
    S = dgate.shape[0]
    tm = 256

    def body(dg_ref, du_ref, wg_ref, wu_ref, h1_ref, w2_ref, dh2_ref, wo_ref, dh1_ref, gw2_ref, dmix_ref):
        @pl.when(pl.program_id(0) == 0)
        def _():
            gw2_ref[...] = jnp.zeros_like(gw2_ref)

        du2 = _dot_nt(dg_ref[...], wg_ref[...]) + _dot_nt(du_ref[...], wu_ref[...])
        h1 = h1_ref[...]
        r = lax.rsqrt(jnp.mean(h1 * h1, axis=-1, keepdims=True) + EPS)
        hn = h1 * r
        gw2_ref[...] += jnp.sum(du2 * hn, axis=0, keepdims=True)
        dh1 = dh2_ref[...] + _rms_bwd(du2 * w2_ref[...], hn, r)
        dh1_ref[...] = dh1
        dmix_ref[...] = _dot_nt(dh1.astype(bf16), wo_ref[...])

    row = lambda w: pl.BlockSpec((tm, w), lambda i: (i, 0))
    call = dict(in_specs=[row(FFN), row(FFN), pl.BlockSpec((D_MODEL, FFN), lambda i: (0, 0)),
                          pl.BlockSpec((D_MODEL, FFN), lambda i: (0, 1)), row(D_MODEL),
                          pl.BlockSpec((1, D_MODEL), lambda i: (0, 0)), row(D_MODEL),
                          pl.BlockSpec((D_MODEL, D_MODEL), lambda i: (0, 0))],
                out_specs=[row(D_MODEL), pl.BlockSpec((1, D_MODEL), lambda i: (0, 0)), row(D_MODEL)],
                out_shape=[jax.ShapeDtypeStruct((S, D_MODEL), f32), jax.ShapeDtypeStruct((1, D_MODEL), f32),
                           jax.ShapeDtypeStruct((S, D_MODEL), f32)], scratch_shapes=[])
    call, body, more = _ride(call, rider, body, lambda: pl.program_id(0), S // tm, 8, 3, 0)
    return pl.pallas_call(body, name="dgu", grid=(S // tm,), compiler_params=_cp("arbitrary"), **call)(
        dgate, dup, wgu, wgu, h1, w2, dh2, wout, *more)


def _din(dq, dk, dv, dhq, dhf, dhi, dhg, cos_t, sg_t, win, x, w1, dh1):
    S = x.shape[0]
    tm = 256

    def body(dq_ref, dk_ref, dv_ref, dhq_ref, dhf_ref, dhi_ref, dhg_ref, cos_ref, sg_ref, w_ref, x_ref, w1_ref, dh1_ref,
             dp_ref, gx_ref, gw1_ref):
        @pl.when(pl.program_id(0) == 0)
        def _():
            gw1_ref[...] = jnp.zeros_like(gw1_ref)

        cosv, sgv = jnp.tile(cos_ref[...], (1, ATTN_W // 128)), jnp.tile(sg_ref[...], (1, ATTN_W // 128))
        unrope = lambda d: d * cosv - sgv * _swap_halves(d)
        parts = [(unrope(dq_ref[...]) * (HEAD_DIM ** -0.5)).astype(bf16), unrope(dk_ref[...]).astype(bf16),
                 dv_ref[...].astype(bf16), dhq_ref[...], dhf_ref[...], dhi_ref[...], dhg_ref[...]]
        du = jnp.zeros((tm, D_MODEL), f32)
        for j, pj in enumerate(parts):
            dp_ref[:, j * 512:(j + 1) * 512] = pj
            du = du + _dot_nt(pj, w_ref[:, j * 512:(j + 1) * 512])
        xv = x_ref[...]
        r = lax.rsqrt(jnp.mean(xv * xv, axis=-1, keepdims=True) + EPS)
        xn = xv * r
        gw1_ref[...] += jnp.sum(du * xn, axis=0, keepdims=True)
        gx_ref[...] = dh1_ref[...] + _rms_bwd(du * w1_ref[...], xn, r)

    row = lambda w: pl.BlockSpec((tm, w), lambda i: (i, 0))
    vec = pl.BlockSpec((1, D_MODEL), lambda i: (0, 0))
    return pl.pallas_call(
        body, name="din", grid=(S // tm,),
        in_specs=[row(512)] * 7 + [row(128), row(128), pl.BlockSpec((D_MODEL, IN_W), lambda i: (0, 0)), row(D_MODEL), vec,
                                   row(D_MODEL)],
        out_specs=[row(IN_W), row(D_MODEL), vec],
        out_shape=[jax.ShapeDtypeStruct((S, IN_W), bf16), jax.ShapeDtypeStruct((S, D_MODEL), f32),
                   jax.ShapeDtypeStruct((1, D_MODEL), f32)],
        compiler_params=_cp("arbitrary"),
    )(dq, dk, dv, dhq, dhf, dhi, dhg, cos_t, sg_t, win, x, w1, dh1)


def _gw(a, bs, tn, name):
    S, M = a.shape
    N = bs[0].shape[1]
    ts = 1024
    k = len(bs)

    def body(a_ref, *refs):
        @pl.when(pl.program_id(1) == 0)
        def _():
            for o_ref in refs[k:]:
                o_ref[...] = jnp.zeros_like(o_ref)

        at = a_ref[...].astype(bf16)
        for b_ref, o_ref in zip(refs[:k], refs[k:]):
            o_ref[...] += _dot_tn(at, b_ref[...].astype(bf16))

    return pl.pallas_call(
        body, name=name, grid=(N // tn, S // ts),
        in_specs=[pl.BlockSpec((ts, M), lambda j, s: (s, 0))] + [pl.BlockSpec((ts, tn), lambda j, s: (s, j))] * k,
        out_specs=[pl.BlockSpec((M, tn), lambda j, s: (0, j))] * k, out_shape=[jax.ShapeDtypeStruct((M, N), f32)] * k,
        compiler_params=_cp("arbitrary", "arbitrary"),
    )(a, *bs)


MESH = pl.DeviceIdType.MESH
ANY = pl.BlockSpec(memory_space=pl.ANY)
VMEM_SPEC = pl.BlockSpec(memory_space=pltpu.VMEM)


def _pos():
    return lax.axis_index("x"), lax.axis_index("y"), lax.axis_index("c")


def _flip(v, bit):
    return 1 - v if bit else v


def _gather_rider(shards):
    n = len(shards)

    def parts(outs, scratch):
        send_sems, recv_sems, local_sems = scratch[n:]
        x, y, c = _pos()
        chips = [(1 - x, y), (x, 1 - y), (1 - x, 1 - y)]

        def copy(a, k, block, to, src=None):
            dst = outs[a].at[4 * block[0] + 2 * block[1] + block[2]]
            return pltpu.make_async_remote_copy(src_ref=dst if src is None else src, dst_ref=dst, send_sem=send_sems.at[a, k],
                                                recv_sem=recv_sems.at[a, k], device_id=to, device_id_type=MESH)

        bufs = scratch[:n]
        me, sibling = (x, y, c), (x, y, 1 - c)
        own = lambda a: pltpu.make_async_copy(bufs[a], outs[a].at[4 * x + 2 * y + c], local_sems.at[a])
        sent = lambda a: [copy(a, 0, me, sibling, src=bufs[a])] + [copy(a, 1 + j, me, (*chip, c), src=bufs[a])
                                                                   for j, chip in enumerate(chips)]
        passed = lambda a: [copy(a, 4 + j, (*chip, c), sibling) for j, chip in enumerate(chips)]
        landed = lambda a: [copy(a, 1 + j, (*chip, c), me) for j, chip in enumerate(chips)]
        from_sibling = lambda a: [copy(a, 0, sibling, me)] + [copy(a, 4 + j, (*chip, 1 - c), me) for j, chip in enumerate(chips)]
        return bufs, local_sems, own, sent, passed, landed, from_sibling

    def first(ins, outs, scratch):
        bufs, local_sems, own, sent, _, _, _ = parts(outs, scratch)
        loads = [pltpu.make_async_copy(ins[a], bufs[a], local_sems.at[a]) for a in range(n)]
        for ld in loads:
            ld.start()
        for a in range(n):
            loads[a].wait()
            own(a).start()
            for cp in sent(a):
                cp.start()

    def middle(ins, outs, scratch):
        _, _, _, _, passed, landed, _ = parts(outs, scratch)
        for a in range(n):
            for got, on in zip(landed(a), passed(a)):
                got.wait_recv()
                on.start()

    def last(ins, outs, scratch):
        _, _, own, sent, passed, _, from_sibling = parts(outs, scratch)
        for a in range(n):
            for cp in from_sibling(a):
                cp.wait_recv()
        for a in range(n):
            for cp in sent(a) + passed(a):
                cp.wait_send()
            own(a).wait()

    return _Rider(shards, [jax.ShapeDtypeStruct((N_DEV,) + s.shape, s.dtype) for s in shards],
                  [pltpu.VMEM(s.shape, s.dtype) for s in shards]
                  + [pltpu.SemaphoreType.DMA((n, 7)), pltpu.SemaphoreType.DMA((n, 7)), pltpu.SemaphoreType.DMA((n,))],
                  first, last, middle)


def _sibling_rider(grads):
    n = len(grads)

    def copies(g, got, scratch):
        send_sems, recv_sems = scratch
        x, y, c = _pos()
        return [pltpu.make_async_remote_copy(src_ref=g[a].at[2 * q + (1 - c)], dst_ref=got[a].at[q], send_sem=send_sems.at[a, q],
                                             recv_sem=recv_sems.at[a, q], device_id=(x, y, 1 - c), device_id_type=MESH)
                for a in range(n) for q in range(4)]

    def first(g, got, scratch):
        for cp in copies(g, got, scratch):
            cp.start()

    def last(g, got, scratch):
        for cp in copies(g, got, scratch):
            cp.wait()

    return _Rider(grads, [jax.ShapeDtypeStruct((4,) + g.shape[1:], g.dtype) for g in grads],
                  [pltpu.SemaphoreType.DMA((n, 4))] * 2, first, last)


def _chips_rider(sums):
    n = len(sums)

    def copies(s, out, scratch):
        send_sems, recv_sems = scratch
        x, y, c = _pos()
        cps = []
        for a in range(n):
            for f in (1, 2, 3):
                peer = (_flip(x, f >> 1), _flip(y, f & 1), c)
                cps.append(pltpu.make_async_remote_copy(
                    src_ref=s[a].at[2 * peer[0] + peer[1]], dst_ref=out[a].at[f - 1], send_sem=send_sems.at[a, f - 1],
                    recv_sem=recv_sems.at[a, f - 1], device_id=peer, device_id_type=MESH))
        return cps

    def first(s, out, scratch):
        for cp in copies(s, out, scratch):
            cp.start()

    def last(s, out, scratch):
        for cp in copies(s, out, scratch):
            cp.wait()

    return _Rider(sums, [jax.ShapeDtypeStruct((3,) + s.shape[1:], s.dtype) for s in sums],
                  [pltpu.SemaphoreType.DMA((n, 3))] * 2, first, last)


def _alone(rider, name):
    ri, ro = len(rider.ins), len(rider.out_shapes)

    def body(*refs):
        theirs = (refs[:ri], refs[ri:ri + ro], refs[ri + ro:])
        rider.first(*theirs)
        if rider.middle is not None:
            rider.middle(*theirs)
        rider.last(*theirs)

    return pl.pallas_call(body, name=name, in_specs=[ANY] * ri, out_specs=[ANY] * ro, out_shape=rider.out_shapes,
                          scratch_shapes=rider.scratch)(*rider.ins)


def _gather_small(g_w1, g_w2, g_w3, g_lb, g_wn, loss):
    def body(w1_ref, w2_ref, w3_ref, lb_ref, wn_ref, loss_ref, out_ref, pk, send_sems, recv_sems):
        x, y, c = _pos()
        me = 4 * x + 2 * y + c
        pk[...] = jnp.zeros_like(pk)
        pk[0:1, :] = w1_ref[...]
        pk[1:2, :] = w2_ref[...]
        pk[2:3, :] = w3_ref[...]
        pk[3:4, 0:HGRN_W] = lb_ref[...]
        pk[3:4, HGRN_W:2 * HGRN_W] = wn_ref[...]
        pk[4:5, 0:128] = loss_ref[...]
        out_ref[me] = pk[...]
        sends, recvs = [], []
        for k in range(1, N_DEV):
            peer = (_flip(x, k >> 2), _flip(y, (k >> 1) & 1), _flip(c, k & 1))
            cp = pltpu.make_async_remote_copy(src_ref=pk, dst_ref=out_ref.at[me], send_sem=send_sems.at[k - 1],
                                              recv_sem=recv_sems.at[k - 1], device_id=peer, device_id_type=MESH)
            cp.start()
            sends.append(cp)
            recvs.append(pltpu.make_async_remote_copy(src_ref=pk, dst_ref=out_ref.at[4 * peer[0] + 2 * peer[1] + peer[2]],
                                                      send_sem=send_sems.at[k - 1], recv_sem=recv_sems.at[k - 1], device_id=peer,
                                                      device_id_type=MESH))
        for cp in recvs:
            cp.wait_recv()
        for cp in sends:
            cp.wait_send()

    return pl.pallas_call(
        body, name="gather_small", in_specs=[VMEM_SPEC] * 6, out_specs=VMEM_SPEC,
        out_shape=jax.ShapeDtypeStruct((N_DEV, 8, D_MODEL), f32),
        scratch_shapes=[pltpu.VMEM((8, D_MODEL), f32), pltpu.SemaphoreType.DMA((N_DEV - 1,)), pltpu.SemaphoreType.DMA((N_DEV - 1,))],
    )(g_w1, g_w2, g_w3, g_lb, g_wn, loss)


def _row_tile(r):
    return max(t for t in range(8, 257, 8) if r % t == 0)


def _add_sibling(core, g, got, name):
    _, r, c = got.shape
    tr = _row_tile(r)

    def body(core_ref, a_ref, b_ref, o_ref):
        o_ref[...] = (a_ref[...] + b_ref[...]).astype(bf16)

    blk = pl.BlockSpec((1, tr, c), lambda q, i, core_ref: (q, i, 0))
    return pl.pallas_call(
        body, name=name, out_shape=jax.ShapeDtypeStruct(got.shape, bf16),
        grid_spec=pltpu.PrefetchScalarGridSpec(
            num_scalar_prefetch=1, grid=(4, r // tr),
            in_specs=[pl.BlockSpec((1, tr, c), lambda q, i, core_ref: (2 * q + core_ref[0], i, 0)), blk], out_specs=blk),
        compiler_params=_cp("arbitrary", "arbitrary"))(core, g, got)


def _adamw(w, g, m, v):
    m = ADAM_B1 * m + (1.0 - ADAM_B1) * g
    v = ADAM_B2 * v + (1.0 - ADAM_B2) * (g * g)
    m_hat = m / (1.0 - ADAM_B1 ** ADAM_STEP)
    v_hat = v / (1.0 - ADAM_B2 ** ADAM_STEP)
    return -ADAM_LR * (m_hat / (jnp.sqrt(v_hat) + ADAM_EPS) + ADAM_WD * w), m, v


def _adam_shard(where, g, got, pieces, w, m, v, name):
    r, c = w.shape
    tr = _row_tile(r)

    def body(where_ref, g_ref, got_ref, p_ref, w_ref, m_ref, v_ref, g_out, d_out, m_out, v_out):
        gsum = g_ref[0] + got_ref[0]
        for f in range(3):
            gsum = gsum + p_ref[f].astype(f32)
        g_out[...] = gsum
        d_out[...], m_out[...], v_out[...] = _adamw(w_ref[...], gsum, m_ref[...], v_ref[...])

    blk = pl.BlockSpec((tr, c), lambda i, where_ref: (i, 0))
    return pl.pallas_call(
        body, name=name, out_shape=[jax.ShapeDtypeStruct((r, c), f32)] * 4,
        grid_spec=pltpu.PrefetchScalarGridSpec(
            num_scalar_prefetch=1, grid=(r // tr,),
            in_specs=[pl.BlockSpec((1, tr, c), lambda i, where_ref: (where_ref[0], i, 0)),
                      pl.BlockSpec((1, tr, c), lambda i, where_ref: (where_ref[1], i, 0)),
                      pl.BlockSpec((3, tr, c), lambda i, where_ref: (0, i, 0)), blk, blk, blk],
            out_specs=[blk] * 4),
        compiler_params=_cp("arbitrary"),
    )(where, g, got, pieces, w, m, v)


def _small_update(gath, params):
    def body(gath_ref, *refs):
        ins, outs = refs[:15], refs[15:]
        gs = gath_ref[0]
        for k in range(1, N_DEV):
            gs = gs + gath_ref[k]
        outs[0][...] = gs[4:5, 0:128]
        l0, l1 = ins[9][0:1, :], ins[9][1:2, :]
        lb = _sigmoid(l0 - l1)
        d0 = gs[3:4, 0:HGRN_W] * lb * (1.0 - lb)
        first_row = lax.broadcasted_iota(jnp.int32, (2, HGRN_W), 0) == 0
        grads = [gs[0:1, :], gs[1:2, :], gs[2:3, :], jnp.where(first_row, d0, -d0), gs[3:4, HGRN_W:2 * HGRN_W]]
        for i, g in enumerate(grads):
            w_ref, m_ref, v_ref = ins[3 * i:3 * i + 3]
            o = outs[1 + 4 * i:5 + 4 * i]
            o[0][...] = g
            o[1][...], o[2][...], o[3][...] = _adamw(w_ref[...], g, m_ref[...], v_ref[...])

    flat = [a for p in params for a in p]
    out_shape = [jax.ShapeDtypeStruct((1, 128), f32)] + [jax.ShapeDtypeStruct(p[0].shape, f32) for p in params for _ in range(4)]
    outs = pl.pallas_call(body, name="small_update", in_specs=[VMEM_SPEC] * 16, out_specs=[VMEM_SPEC] * 21, out_shape=out_shape)(gath, *flat)
    return outs[0], [outs[1 + 4 * i:5 + 4 * i] for i in range(5)]


def kernel(x, norm1_w, w_in, lb_logits, hgrn_norm_w, w_out, norm2_w, w_gate_up, w_down, final_norm_w, loss_target, m_norm1_w, m_w_in, m_lb_logits, m_hgrn_norm_w, m_w_out, m_norm2_w, m_w_gate_up, m_w_down, m_final_norm_w, v_norm1_w, v_w_in, v_lb_logits, v_hgrn_norm_w, v_w_out, v_norm2_w, v_w_gate_up, v_w_down, v_final_norm_w):
    row = lambda a: a.reshape(1, D_MODEL)
    by_owner = lambda g, w: jnp.transpose(g.reshape(g.shape[0], g.shape[1] // w, w), (1, 0, 2))
    ix, iy, ic = lax.axis_index("x"), lax.axis_index("y"), lax.axis_index("c")
    core = jnp.stack([ic]).astype(jnp.int32)
    where = jnp.stack([4 * ix + 2 * iy + ic, 2 * ix + iy]).astype(jnp.int32)
    xs, tgt, w3 = x[0], loss_target[0], row(final_norm_w)
    S = xs.shape[0]

    (win_g,) = _alone(_gather_rider([w_in[0].astype(bf16)]), "gather_w_in")
    win = jnp.transpose(win_g, (1, 0, 2)).reshape(D_MODEL, IN_W)
    cos_t, sg_t = _rope_tables(S)
    u, qkv, hp = _in_proj(xs, norm1_w, win, cos_t, sg_t)
    ya, lse, wout_g, wgu_g, wdown_g = _attn_fwd(qkv, _gather_rider([w_out[0].astype(bf16), w_gate_up[0].astype(bf16),
                                                                     w_down[0].astype(bf16)]))
    wout = wout_g.reshape(D_MODEL, D_MODEL)
    wgu = jnp.transpose(wgu_g, (1, 0, 2)).reshape(D_MODEL, 2 * FFN)
    wdown = wdown_g.reshape(FFN, D_MODEL)
    yb, o_sav, states = _hgrn_fwd(hp, lb_logits, hgrn_norm_w)
    h1, u2, mixed = _out_proj(xs, ya, yb, wout, norm2_w)
    gate, up, act = _gate_up(u2, wgu)
    dh2, loss_p, g_w3 = _down_loss(act, wdown, h1, tgt, w3)

    (g_wdown,) = _gw(act, [dh2], 512, "gw_down")
    dgate, dup = _dact(dh2, wdown, gate, up)
    g_wgu = _gw(u2, [dgate, dup], 1408, "gw_gate_up")
    early = [jnp.concatenate([by_owner(g, 2 * FFN // N_DEV) for g in g_wgu], axis=0), g_wdown.reshape(N_DEV, FFN // N_DEV, D_MODEL)]
    dh1, g_w2, dmix, *got_early = _dgu(dgate, dup, wgu, h1, norm2_w, dh2, wout, _sibling_rider(early))
    sums_early = [_add_sibling(core, g, o, f"add_sibling_{i}") for i, (g, o) in enumerate(zip(early, got_early))]
    (g_wout,) = _gw(mixed, [dh1], 1024, "gw_out")
    dhq, dhf, dhi, dhg, g_wn, g_lb, *pieces_early = _hgrn_bwd(hp, lb_logits, hgrn_norm_w, o_sav, states, dmix, _chips_rider(sums_early))
    datt = _attn_bwd(qkv, ya, lse, dmix)
    dproj, gx, g_w1 = _din(*datt, dhq, dhf, dhi, dhg, cos_t, sg_t, win, xs, norm1_w, dh1)
    (g_win,) = _gw(u, [dproj], 896, "gw_in")
    late = [by_owner(g_win, IN_W // N_DEV), g_wout.reshape(N_DEV, D_MODEL // N_DEV, D_MODEL)]
    got_late = _alone(_sibling_rider(late), "reduce_sibling")
    sums_late = [_add_sibling(core, g, o, f"add_sibling_{2 + i}") for i, (g, o) in enumerate(zip(late, got_late))]
    pieces_late = _alone(_chips_rider(sums_late), "reduce_chips")

    grads = [late[0], late[1], early[0], early[1]]
    got = [got_late[0], got_late[1], got_early[0], got_early[1]]
    pieces = [pieces_late[0], pieces_late[1], pieces_early[0], pieces_early[1]]
    shards = [w_in[0], w_out[0], w_gate_up[0], w_down[0]]
    moms = [(m_w_in[0], v_w_in[0]), (m_w_out[0], v_w_out[0]), (m_w_gate_up[0], v_w_gate_up[0]), (m_w_down[0], v_w_down[0])]
    big = [_adam_shard(where, g, o, p, w, m, v, f"adam_{i}")
           for i, (g, o, p, w, (m, v)) in enumerate(zip(grads, got, pieces, shards, moms))]
    big = [[a[None] for a in four] for four in big]

    gath = _gather_small(g_w1, g_w2, g_w3, g_lb, g_wn, loss_p)
    params = [(norm1_w, m_norm1_w, v_norm1_w), (norm2_w, m_norm2_w, v_norm2_w),
              (row(final_norm_w), row(m_final_norm_w), row(v_final_norm_w)),
              (lb_logits, m_lb_logits, v_lb_logits), (hgrn_norm_w, m_hgrn_norm_w, v_hgrn_norm_w)]
    loss, (s_w1, s_w2, s_w3, s_lb, s_wn) = _small_update(gath, params)
    s_w3 = [a.reshape(D_MODEL) for a in s_w3]
    per_w = [s_w1, big[0], s_lb, s_wn, big[1], s_w2, big[2], big[3], s_w3]
    return (loss[0, 0], gx[None], *[p[0] for p in per_w], *[p[1] for p in per_w], *[p[2] for p in per_w], *[p[3] for p in per_w])
```

```python
import jax
import jax.numpy as jnp
from jax import lax
from jax.experimental import pallas as pl
from jax.experimental.pallas import tpu as pltpu

f32, bf16 = jnp.float32, jnp.bfloat16

D_MODEL = 1024
ATTN_W = 512
HEAD_DIM = 64
ATTN_BLK = 128
DILATIONS = (1, 4, 16)
HGRN_W = 512
HGRN_HD = 128
CHUNK = 16
IN_W = 3 * ATTN_W + 4 * HGRN_W
FFN = 2816
EPS = 1e-6
ROPE_THETA = 10000.0
NEG = -1e30
N_DEV = 8
ADAM_LR, ADAM_B1, ADAM_B2, ADAM_EPS, ADAM_WD, ADAM_STEP = 0.001, 0.9, 0.999, 1e-08, 0.01, 10
VMEM_LIMIT = 56 * 1024 * 1024


def _cp(*sem):
    return pltpu.CompilerParams(dimension_semantics=sem, vmem_limit_bytes=VMEM_LIMIT)


def _dot(a, b):
    return jnp.dot(a, b, preferred_element_type=f32)


def _dot_nt(a, b):
    return lax.dot_general(a, b, (((1,), (1,)), ((), ())), preferred_element_type=f32)


def _dot_tn(a, b):
    return lax.dot_general(a, b, (((0,), (0,)), ((), ())), preferred_element_type=f32)


def _sigmoid(x):
    return 0.5 * jnp.tanh(0.5 * x) + 0.5


class _Rider:
    def __init__(self, ins, out_shapes, scratch, first, last, middle=None):
        self.ins, self.out_shapes, self.scratch = list(ins), list(out_shapes), list(scratch)
        self.first, self.middle, self.last = first, middle, last


def _ride(call, rider, body, step, n_steps, n_in, n_out, n_scratch):
    if rider is None:
        return call, body, []
    ri, ro = len(rider.ins), len(rider.out_shapes)
    any_spec = pl.BlockSpec(memory_space=pl.ANY)
    call = dict(call, in_specs=call["in_specs"] + [any_spec] * ri, out_specs=call["out_specs"] + [any_spec] * ro,
                out_shape=call["out_shape"] + rider.out_shapes, scratch_shapes=call["scratch_shapes"] + rider.scratch)

    def riding(*refs):
        a = n_in + ri
        b = a + n_out + ro
        mine = refs[:n_in] + refs[a:a + n_out] + refs[b:b + n_scratch]
        theirs = (refs[n_in:a], refs[a + n_out:b], refs[b + n_scratch:])
        t = step()

        @pl.when(t == 0)
        def _():
            rider.first(*theirs)

        body(*mine)
        if rider.middle is not None:
            @pl.when(t == n_steps // 2)
            def _():
                rider.middle(*theirs)

        @pl.when(t == n_steps - 1)
        def _():
            rider.last(*theirs)

    return call, riding, rider.ins


def _rope_tables(S, rider=None):
    half = HEAD_DIM // 2
    tm = 256
    inv_freq = jnp.tile(ROPE_THETA ** (-jnp.arange(half, dtype=f32) / half), 128 // half).reshape(1, 128)
    sign = jnp.tile(jnp.concatenate([-jnp.ones((half,), f32), jnp.ones((half,), f32)]), 128 // HEAD_DIM).reshape(1, 128)

    def body(inv_ref, sign_ref, cos_ref, sg_ref):
        pos = (lax.broadcasted_iota(jnp.int32, (tm, 128), 0) + pl.program_id(0) * tm).astype(f32)
        ang = pos * inv_ref[...]
        cos_ref[...] = jnp.cos(ang)
        sg_ref[...] = jnp.sin(ang) * sign_ref[...]

    vec = pl.BlockSpec((1, 128), lambda i: (0, 0))
    out = pl.BlockSpec((tm, 128), lambda i: (i, 0))
    call = dict(in_specs=[vec, vec], out_specs=[out, out], out_shape=[jax.ShapeDtypeStruct((S, 128), f32)] * 2, scratch_shapes=[])
    call, body, more = _ride(call, rider, body, lambda: pl.program_id(0), S // tm, 2, 2, 0)
    return pl.pallas_call(body, name="rope_tables", grid=(S // tm,), compiler_params=_cp("arbitrary"), **call)(inv_freq, sign, *more)


def _swap_halves(v):
    n = v.shape[1]
    lane = lax.broadcasted_iota(jnp.int32, v.shape, 1)
    return jnp.where((lane % HEAD_DIM) < HEAD_DIM // 2, pltpu.roll(v, n - HEAD_DIM // 2, 1), pltpu.roll(v, HEAD_DIM // 2, 1))


def _in_proj(x, w1, win, cos_t, sg_t):
    S = x.shape[0]
    tm = 256

    def body(x_ref, w1_ref, w_ref, cos_ref, sg_ref, u_ref, qkv_ref, hp_ref):
        xv = x_ref[...]
        r = lax.rsqrt(jnp.mean(xv * xv, axis=-1, keepdims=True) + EPS)
        u = (xv * r * w1_ref[...]).astype(bf16)
        u_ref[...] = u
        cosv, sgv = jnp.tile(cos_ref[...], (1, ATTN_W // 128)), jnp.tile(sg_ref[...], (1, ATTN_W // 128))
        for j in range(3):
            pj = _dot(u, w_ref[:, j * ATTN_W:(j + 1) * ATTN_W])
            if j < 2:
                pj = pj * cosv + _swap_halves(pj) * sgv
            if j == 0:
                pj = pj * (HEAD_DIM ** -0.5)
            qkv_ref[:, j * ATTN_W:(j + 1) * ATTN_W] = pj.astype(bf16)
        for j in range(4):
            lo = 3 * ATTN_W + j * HGRN_W
            hp_ref[:, j * HGRN_W:(j + 1) * HGRN_W] = _dot(u, w_ref[:, lo:lo + HGRN_W])

    return pl.pallas_call(
        body, name="in_proj", grid=(S // tm,),
        in_specs=[pl.BlockSpec((tm, D_MODEL), lambda i: (i, 0)), pl.BlockSpec((1, D_MODEL), lambda i: (0, 0)),
                  pl.BlockSpec((D_MODEL, IN_W), lambda i: (0, 0)),
                  pl.BlockSpec((tm, 128), lambda i: (i, 0)), pl.BlockSpec((tm, 128), lambda i: (i, 0))],
        out_specs=[pl.BlockSpec((tm, D_MODEL), lambda i: (i, 0)), pl.BlockSpec((tm, 3 * ATTN_W), lambda i: (i, 0)),
                   pl.BlockSpec((tm, 4 * HGRN_W), lambda i: (i, 0))],
        out_shape=[jax.ShapeDtypeStruct((S, D_MODEL), bf16), jax.ShapeDtypeStruct((S, 3 * ATTN_W), bf16),
                   jax.ShapeDtypeStruct((S, 4 * HGRN_W), f32)],
        compiler_params=_cp("arbitrary"),
    )(x, w1, win, cos_t, sg_t)


def _head_masks():
    lane = lax.broadcasted_iota(jnp.int32, (ATTN_BLK, 128), 1)
    even = lane < HEAD_DIM
    return even, (even, jnp.logical_not(even))


def _pair_fwd(q2, k2, v2, bias):
    even, masks = _head_masks()
    outs, lses = [], []
    for e in range(2):
        qm = jnp.where(masks[e], q2, 0.0).astype(bf16)
        s = _dot_nt(qm, k2) + bias
        m = jnp.max(s, axis=-1, keepdims=True)
        pe = jnp.exp(s - m)
        lsum = jnp.sum(pe, axis=-1, keepdims=True)
        outs.append(_dot(pe.astype(bf16), v2) / lsum)
        lses.append(jnp.broadcast_to(m + jnp.log(lsum), (ATTN_BLK, 128)))
    return jnp.where(even, outs[0], outs[1]), jnp.where(even, lses[0], lses[1])


def _merge(y0, l0, y1, l1):
    mx = jnp.maximum(l0, l1)
    a, b = jnp.exp(l0 - mx), jnp.exp(l1 - mx)
    tot = a + b
    return (a * y0 + b * y1) / tot, mx + jnp.log(tot)


def _pair_bwd(q2, k2f, v2, dy2, lse2, delta2, bias):
    _, masks = _head_masks()
    k2 = k2f.astype(bf16)
    klane = lax.broadcasted_iota(jnp.int32, (2 * ATTN_BLK, 128), 1) < HEAD_DIM
    kmasks = (klane, jnp.logical_not(klane))
    dq2 = jnp.zeros((ATTN_BLK, 128), f32)
    pes, dss, qms, dyms = [], [], [], []
    for e in range(2):
        c0 = e * HEAD_DIM
        qm = jnp.where(masks[e], q2, 0.0).astype(bf16)
        km = jnp.where(kmasks[e], k2f, 0.0).astype(bf16)
        dym = jnp.where(masks[e], dy2, 0.0).astype(bf16)
        pe = jnp.exp(_dot_nt(qm, k2) + bias - lse2[:, c0:c0 + 1])
        ds = (pe * (_dot_nt(dym, v2) - delta2[:, c0:c0 + 1])).astype(bf16)
        dq2 = dq2 + _dot(ds, km)
        pes.append(pe.astype(bf16))
        dss.append(ds)
        qms.append(qm)
        dyms.append(dym)
    dv2 = _dot_tn(jnp.concatenate(pes, axis=0), jnp.concatenate(dyms, axis=0))
    dk2 = _dot_tn(jnp.concatenate(dss, axis=0), jnp.concatenate(qms, axis=0))
    return dq2, dk2, dv2


TOK = 2048


def _key_bias():
    qi = lax.broadcasted_iota(jnp.int32, (ATTN_BLK, 2 * ATTN_BLK), 0)
    kj = lax.broadcasted_iota(jnp.int32, (ATTN_BLK, 2 * ATTN_BLK), 1)
    delta = ATTN_BLK + qi - kj
    seen = (delta >= 0) & (delta <= ATTN_BLK)
    return jnp.where(seen, 0.0, NEG), jnp.where(seen & (kj >= ATTN_BLK), 0.0, NEG)


def _attn_fwd(qkv, rider=None):
    S = qkv.shape[0]
    nS = S // TOK

    def body(q_ref, kp_ref, kc_ref, vp_ref, vc_ref, y_ref, l_ref, qs, k2, v2, ay, al):
        n = pl.program_id(1)
        qs[...] = q_ref[...].astype(f32)
        k2[0:TOK] = kp_ref[...].astype(f32)
        k2[TOK:2 * TOK] = kc_ref[...].astype(f32)
        v2[0:TOK] = vp_ref[...].astype(f32)
        v2[TOK:2 * TOK] = vc_ref[...].astype(f32)
        bias_any, bias_first = _key_bias()

        def block(dil, r, b, step, last):
            start = r + pl.multiple_of(step * b, step)
            rows = pl.ds(start, ATTN_BLK, stride=dil) if dil > 1 else pl.ds(start, ATTN_BLK)
            keys = (pl.ds(TOK + start - step, 2 * ATTN_BLK, stride=dil) if dil > 1
                    else pl.ds(TOK + start - step, 2 * ATTN_BLK))
            bias = jnp.where((n == 0) & (b == 0), bias_first, bias_any)
            out, lse = _pair_fwd(qs[rows, :], k2[keys, :].astype(bf16), v2[keys, :].astype(bf16), bias)
            if dil < DILATIONS[-1]:
                out, lse = _merge(ay[rows, :], al[rows, :], out, lse)
            if last:
                y_ref[rows, :] = out
                l_ref[rows, :] = lse
            else:
                ay[rows, :] = out
                al[rows, :] = lse

        for dil in reversed(DILATIONS):
            def loop(i, carry, dil=dil):
                block(dil, i % dil, i // dil, ATTN_BLK * dil, dil == 1)
                return carry
            lax.fori_loop(0, TOK // ATTN_BLK, loop, 0, unroll=4)

    blk = (TOK, 128)
    cur = lambda c: pl.BlockSpec(blk, lambda p, n: (n, 4 * c + p))
    prv = lambda c: pl.BlockSpec(blk, lambda p, n: (jnp.maximum(n - 1, 0), 4 * c + p))
    out = pl.BlockSpec(blk, lambda p, n: (n, p))
    call = dict(in_specs=[cur(0), prv(1), cur(1), prv(2), cur(2)], out_specs=[out, out],
                out_shape=[jax.ShapeDtypeStruct((S, ATTN_W), f32)] * 2,
                scratch_shapes=[pltpu.VMEM(blk, f32), pltpu.VMEM((2 * TOK, 128), f32), pltpu.VMEM((2 * TOK, 128), f32),
                                pltpu.VMEM(blk, f32), pltpu.VMEM(blk, f32)])
    call, body, more = _ride(call, rider, body, lambda: pl.program_id(0) * nS + pl.program_id(1), (ATTN_W // 128) * nS, 5, 2, 5)
    return pl.pallas_call(body, name="attention_fwd", grid=(ATTN_W // 128, nS), compiler_params=_cp("arbitrary", "arbitrary"),
                          **call)(qkv, qkv, qkv, qkv, qkv, *more)


def _attn_bwd(qkv, ya, lse, dmix):
    S = qkv.shape[0]
    nS = S // TOK

    def body(q_ref, kp_ref, kc_ref, vp_ref, vc_ref, y_ref, l_ref, dy_ref, dq_ref, dk_ref, dv_ref, qs, k2, v2, dk2, dv2, dqa, dl):
        n = pl.program_id(1)

        @pl.when(n == 0)
        def _():
            dk2[...] = jnp.zeros_like(dk2)
            dv2[...] = jnp.zeros_like(dv2)

        @pl.when(n < nS)
        def _():
            qs[...] = q_ref[...].astype(f32)
            k2[0:TOK] = kp_ref[...].astype(f32)
            k2[TOK:2 * TOK] = kc_ref[...].astype(f32)
            v2[0:TOK] = vp_ref[...].astype(f32)
            v2[TOK:2 * TOK] = vc_ref[...].astype(f32)
            li = lax.broadcasted_iota(jnp.int32, (128, 128), 0)
            lj = lax.broadcasted_iota(jnp.int32, (128, 128), 1)
            seg = jnp.where((li // HEAD_DIM) == (lj // HEAD_DIM), 1.0, 0.0).astype(bf16)
            bias_any, bias_first = _key_bias()

            def delta_rows(t, carry):
                rows = pl.ds(pl.multiple_of(256 * t, 256), 256)
                dyy = dy_ref[rows, :] * y_ref[rows, :]
                hi = dyy.astype(bf16)
                dl[rows, :] = _dot(hi, seg) + _dot((dyy - hi.astype(f32)).astype(bf16), seg)
                return carry

            lax.fori_loop(0, TOK // 256, delta_rows, 0)

            def block(dil, r, b, step, first_pattern, last):
                start = r + pl.multiple_of(step * b, step)
                rows = pl.ds(start, ATTN_BLK, stride=dil) if dil > 1 else pl.ds(start, ATTN_BLK)
                keys = (pl.ds(TOK + start - step, 2 * ATTN_BLK, stride=dil) if dil > 1
                        else pl.ds(TOK + start - step, 2 * ATTN_BLK))
                bias = jnp.where((n == 0) & (b == 0), bias_first, bias_any)
                dq2, dkk, dvv = _pair_bwd(qs[rows, :], k2[keys, :], v2[keys, :].astype(bf16), dy_ref[rows, :],
                                          l_ref[rows, :], dl[rows, :], bias)
                if last:
                    dq_ref[rows, :] = dqa[rows, :] + dq2
                elif first_pattern:
                    dqa[rows, :] = dq2
                else:
                    dqa[rows, :] += dq2
                dk2[keys, :] += dkk
                dv2[keys, :] += dvv

            for dil in reversed(DILATIONS):
                def loop(i, carry, dil=dil):
                    block(dil, i % dil, i // dil, ATTN_BLK * dil, dil == DILATIONS[-1], dil == 1)
                    return carry
                lax.fori_loop(0, TOK // ATTN_BLK, loop, 0, unroll=4)

        dk_ref[...] = dk2[0:TOK]
        dv_ref[...] = dv2[0:TOK]
        dk2[0:TOK] = dk2[TOK:2 * TOK]
        dv2[0:TOK] = dv2[TOK:2 * TOK]
        dk2[TOK:2 * TOK] = jnp.zeros((TOK, 128), f32)
        dv2[TOK:2 * TOK] = jnp.zeros((TOK, 128), f32)

    blk = (TOK, 128)
    cn = lambda n: jnp.minimum(n, nS - 1)
    pn = lambda n: jnp.clip(n - 1, 0, nS - 1)
    cur = lambda c: pl.BlockSpec(blk, lambda p, n: (cn(n), 4 * c + p))
    prv = lambda c: pl.BlockSpec(blk, lambda p, n: (pn(n), 4 * c + p))
    at_n = pl.BlockSpec(blk, lambda p, n: (cn(n), p))
    at_p = pl.BlockSpec(blk, lambda p, n: (pn(n), p))
    big = lambda: pltpu.VMEM((2 * TOK, 128), f32)
    return pl.pallas_call(
        body, name="attention_bwd", grid=(ATTN_W // 128, nS + 1),
        in_specs=[cur(0), prv(1), cur(1), prv(2), cur(2), at_n, at_n, at_n], out_specs=[at_n, at_p, at_p],
        out_shape=[jax.ShapeDtypeStruct((S, ATTN_W), f32)] * 3,
        scratch_shapes=[pltpu.VMEM(blk, f32), big(), big(), big(), big(), pltpu.VMEM(blk, f32), pltpu.VMEM(blk, f32)],
        compiler_params=_cp("arbitrary", "arbitrary"),
    )(qkv, qkv, qkv, qkv, qkv, ya, lse, dmix)


HG_T = 256
N_HH = HGRN_W // HGRN_HD
HG_SUB = 128
SAFE_RANGE = 80.0


def _row_in_chunk():
    return lax.broadcasted_iota(jnp.int32, (HG_T, HGRN_HD), 0) % CHUNK


def _chunk_cumsum(v, rc):
    for k in (1, 2, 4, 8):
        v = v + jnp.where(rc >= k, pltpu.roll(v, k, 0), 0.0)
    return v


def _chunk_rcumsum(v, rc):
    for k in (1, 2, 4, 8):
        v = v + jnp.where(rc < CHUNK - k, pltpu.roll(v, HG_T - k, 0), 0.0)
    return v


def _hgrn_gates(qb, fb, lb):
    sf = _sigmoid(fb)
    f = lb + (1.0 - lb) * sf
    sq = _sigmoid(qb)
    return sf, f, jnp.log(f), 1.0 - f, sq, qb * sq


def _hgrn_prep(qb, fb, lbl2, rc):
    lb = _sigmoid(lbl2[0:1, :] - lbl2[1:2, :])
    sf, f, lf, key, sq, qf = _hgrn_gates(qb, fb, lb)
    b = _chunk_cumsum(lf, rc)
    rem = _chunk_rcumsum(lf, rc) - lf
    return dict(lb=lb, sf=sf, f=f, key=key, sq=sq, qf=qf, b=b, rem=rem, eb=jnp.exp(b), er=jnp.exp(rem))


def _chunk_mask():
    r = lax.broadcasted_iota(jnp.int32, (HG_SUB, HG_SUB), 0)
    c = lax.broadcasted_iota(jnp.int32, (HG_SUB, HG_SUB), 1)
    return ((r // CHUNK) == (c // CHUNK)) & (c <= r)


def _hgrn_fwd(hp, lbl, wn):
    S = hp.shape[0]
    nT = S // HG_T

    def body(qb_ref, fb_ref, ib_ref, gb_ref, lbl_ref, wn_ref, yb_ref, o_ref, st_ref, ST, qt_s, kh_s, dec_s, oi_s):
        @pl.when(pl.program_id(0) == 0)
        def _():
            ST[...] = jnp.zeros_like(ST)

        rc = _row_in_chunk()
        for h in range(N_HH):
            sl = slice(HGRN_HD * h, HGRN_HD * (h + 1))
            p = _hgrn_prep(qb_ref[:, sl], fb_ref[:, sl], lbl_ref[:, sl], rc)
            qf, key, b = p["qf"], p["key"], p["b"]
            qt = qf * p["eb"]
            qt_s[:, sl] = qt.astype(bf16)
            kh_s[:, sl] = (key * p["er"]).astype(bf16)
            dec_s[:, sl] = jnp.exp(b + p["rem"])
            rng = jnp.max(-(b + p["rem"]))

            @pl.when(rng < SAFE_RANGE)
            def _():
                kp = (key * jnp.exp(-b)).astype(bf16)
                cmask = _chunk_mask()
                for j in range(HG_T // HG_SUB):
                    rs = slice(HG_SUB * j, HG_SUB * (j + 1))
                    sc = jnp.where(cmask, _dot_nt(qt[rs].astype(bf16), kp[rs]), 0.0).astype(bf16)
                    oi_s[rs, sl] = _dot(sc, ib_ref[rs, sl].astype(bf16))

            @pl.when(rng >= SAFE_RANGE)
            def _():
                v = ib_ref[:, sl]
                ones = jnp.ones((HGRN_HD, HGRN_HD), bf16)
                o = jnp.zeros((HG_T, HGRN_HD), f32)
                for l in range(CHUNK):
                    if l == 0:
                        pr, vs = qf * key, v
                    else:
                        e = jnp.exp(jnp.where(rc >= l, b - pltpu.roll(b, l, 0), NEG))
                        pr, vs = qf * pltpu.roll(key, l, 0) * e, pltpu.roll(v, l, 0)
                    o = o + _dot(pr.astype(bf16), ones) * vs
                oi_s[:, sl] = o

        def step(c, carry):
            rows = pl.ds(pl.multiple_of(c * CHUNK, CHUNK), CHUNK)
            row0 = pl.ds(pl.multiple_of(c * CHUNK, CHUNK), 1)
            for h in range(N_HH):
                sl = slice(HGRN_HD * h, HGRN_HD * (h + 1))
                stv = ST[h]
                st_ref[c, sl, :] = stv
                oi_s[rows, sl] += _dot_nt(qt_s[rows, sl], stv.astype(bf16))
                ST[h] = stv * dec_s[row0, sl] + _dot_tn(ib_ref[rows, sl].astype(bf16), kh_s[rows, sl])
            return carry

        lax.fori_loop(0, HG_T // CHUNK, step, 0, unroll=8)

        for h in range(N_HH):
            sl = slice(HGRN_HD * h, HGRN_HD * (h + 1))
            o = oi_s[:, sl]
            o_ref[:, sl] = o
            on = o * lax.rsqrt(jnp.mean(o * o, axis=-1, keepdims=True) + EPS)
            g = gb_ref[:, sl]
            yb_ref[:, sl] = on * wn_ref[:, sl] * (g * _sigmoid(g))

    col = lambda c: pl.BlockSpec((HG_T, HGRN_W), lambda i: (i, c))
    tile = pl.BlockSpec((HG_T, HGRN_W), lambda i: (i, 0))
    whole = lambda a: pl.BlockSpec(a.shape, lambda i: (0, 0))
    return pl.pallas_call(
        body, name="hgrn_fwd", grid=(nT,),
        in_specs=[col(0), col(1), col(2), col(3), whole(lbl), whole(wn)],
        out_specs=[tile, tile, pl.BlockSpec((HG_T // CHUNK, HGRN_W, HGRN_HD), lambda i: (i, 0, 0))],
        out_shape=[jax.ShapeDtypeStruct((S, HGRN_W), f32), jax.ShapeDtypeStruct((S, HGRN_W), f32),
                   jax.ShapeDtypeStruct((S // CHUNK, HGRN_W, HGRN_HD), f32)],
        scratch_shapes=[pltpu.VMEM((N_HH, HGRN_HD, HGRN_HD), f32), pltpu.VMEM((HG_T, HGRN_W), bf16),
                        pltpu.VMEM((HG_T, HGRN_W), bf16), pltpu.VMEM((HG_T, HGRN_W), f32), pltpu.VMEM((HG_T, HGRN_W), f32)],
        compiler_params=_cp("arbitrary"),
    )(hp, hp, hp, hp, lbl, wn)


def _hgrn_bwd(hp, lbl, wn, o_sav, states, dmix, rider=None):
    S = hp.shape[0]
    nT = S // HG_T

    def body(qb_ref, fb_ref, ib_ref, gb_ref, lbl_ref, wn_ref, o_ref, st_ref, dy_ref,
             dq_ref, df_ref, di_ref, dg_ref, gwn_ref, glb_ref,
             DST, qt_s, kh_s, dec_s, do_s, dqt_s, dkh_s, dbl_s, dvi_s, dqi_s, dki_s, dbi_s):
        @pl.when(pl.program_id(0) == 0)
        def _():
            DST[...] = jnp.zeros_like(DST)
            gwn_ref[...] = jnp.zeros_like(gwn_ref)
            glb_ref[...] = jnp.zeros_like(glb_ref)

        rc = _row_in_chunk()
        preps = []
        for h in range(N_HH):
            sl = slice(HGRN_HD * h, HGRN_HD * (h + 1))
            p = _hgrn_prep(qb_ref[:, sl], fb_ref[:, sl], lbl_ref[:, sl], rc)
            preps.append(p)
            qf, key, b = p["qf"], p["key"], p["b"]
            v = ib_ref[:, sl]
            o = o_ref[:, sl]
            rinv = lax.rsqrt(jnp.mean(o * o, axis=-1, keepdims=True) + EPS)
            on = o * rinv
            g = gb_ref[:, sl]
            sgm = _sigmoid(g)
            silu_g = g * sgm
            dy = dy_ref[:, sl]
            wn_v = wn_ref[:, sl]
            gwn_ref[:, sl] += jnp.sum(dy * on * silu_g, axis=0, keepdims=True)
            dg_ref[:, sl] = (dy * on * wn_v * (sgm * (1.0 + g * (1.0 - sgm)))).astype(bf16)
            t1 = dy * wn_v * silu_g
            do = rinv * (t1 - on * jnp.mean(t1 * on, axis=-1, keepdims=True))
            do_s[:, sl] = do.astype(bf16)
            qt = qf * p["eb"]
            qt_s[:, sl] = qt.astype(bf16)
            kh_s[:, sl] = (key * p["er"]).astype(bf16)
            dec_s[:, sl] = jnp.exp(b + p["rem"])
            rng = jnp.max(-(b + p["rem"]))

            @pl.when(rng < SAFE_RANGE)
            def _():
                einv = jnp.exp(-b)
                kp = (key * einv).astype(bf16)
                cmask = _chunk_mask()
                for j in range(HG_T // HG_SUB):
                    rs = slice(HG_SUB * j, HG_SUB * (j + 1))
                    qtb, dob, vb = qt[rs].astype(bf16), do[rs].astype(bf16), v[rs].astype(bf16)
                    sc = jnp.where(cmask, _dot_nt(qtb, kp[rs]), 0.0).astype(bf16)
                    dsc = jnp.where(cmask, _dot_nt(dob, vb), 0.0).astype(bf16)
                    dqp = _dot(dsc, kp[rs])
                    dkp = _dot_tn(dsc, qtb)
                    dvi_s[rs, sl] = _dot_tn(sc, dob)
                    dqi_s[rs, sl] = dqp * p["eb"][rs]
                    dki_s[rs, sl] = dkp * einv[rs]
                    dbi_s[rs, sl] = dqp * qtb.astype(f32) - dkp * kp[rs].astype(f32)

            @pl.when(rng >= SAFE_RANGE)
            def _():
                ones = jnp.ones((HGRN_HD, HGRN_HD), bf16)
                dqf = jnp.zeros((HG_T, HGRN_HD), f32)
                dkey, db, dv = dqf, dqf, dqf
                for l in range(CHUNK):
                    if l == 0:
                        ks, vs, qe = key, v, qf
                    else:
                        e = jnp.exp(jnp.where(rc >= l, b - pltpu.roll(b, l, 0), NEG))
                        ks, vs, qe = pltpu.roll(key, l, 0), pltpu.roll(v, l, 0), qf * e
                    pr = qe * ks
                    rl = _dot(pr.astype(bf16), ones)
                    drl = _dot((do * vs).astype(bf16), ones)
                    if l == 0:
                        dqf = dqf + drl * ks
                        dv = dv + rl * do
                        dkey = dkey + drl * qe
                    else:
                        drl = jnp.where(rc >= l, drl, 0.0)
                        gl = drl * pr
                        dqf = dqf + drl * ks * e
                        dv = dv + pltpu.roll(rl * do, HG_T - l, 0)
                        dkey = dkey + pltpu.roll(drl * qe, HG_T - l, 0)
                        db = db + gl - pltpu.roll(gl, HG_T - l, 0)
                dvi_s[:, sl] = dv
                dqi_s[:, sl] = dqf
                dki_s[:, sl] = dkey
                dbi_s[:, sl] = db

        def step(k, carry):
            c = HG_T // CHUNK - 1 - k
            rows = pl.ds(pl.multiple_of(c * CHUNK, CHUNK), CHUNK)
            row0 = pl.ds(pl.multiple_of(c * CHUNK, CHUNK), 1)
            for h in range(N_HH):
                sl = slice(HGRN_HD * h, HGRN_HD * (h + 1))
                stp = st_ref[c, sl, :]
                dst = DST[h]
                dstb = dst.astype(bf16)
                dob = do_s[rows, sl]
                khb = kh_s[rows, sl]
                dec = dec_s[row0, sl]
                dqt_s[rows, sl] = _dot(dob, stp.astype(bf16))
                dkh = _dot(ib_ref[rows, sl].astype(bf16), dstb)
                dkh_s[rows, sl] = dkh
                dvi_s[rows, sl] += _dot_nt(khb, dstb)
                dbl = jnp.sum(dst * stp, axis=0, keepdims=True) * dec + jnp.sum(dkh * khb.astype(f32), axis=0, keepdims=True)
                dbl_s[rows, sl] = jnp.broadcast_to(dbl, (CHUNK, HGRN_HD))
                DST[h] = dst * dec + _dot_tn(dob, qt_s[rows, sl])
            return carry

        lax.fori_loop(0, HG_T // CHUNK, step, 0, unroll=4)

        for h in range(N_HH):
            sl = slice(HGRN_HD * h, HGRN_HD * (h + 1))
            qb = qb_ref[:, sl]
            p = preps[h]
            sf, sq, lb = p["sf"], p["sq"], p["lb"]
            dqt, dkh = dqt_s[:, sl], dkh_s[:, sl]
            dqf = dqt * p["eb"] + dqi_s[:, sl]
            dkey = dkh * p["er"] + dki_s[:, sl]
            db = dqt * (p["qf"] * p["eb"]) - dkh * (p["key"] * p["er"]) + jnp.where(rc == CHUNK - 1, dbl_s[:, sl], 0.0) + dbi_s[:, sl]
            df = _chunk_rcumsum(db, rc) / p["f"] - dkey
            df_ref[:, sl] = (df * (1.0 - lb) * sf * (1.0 - sf)).astype(bf16)
            glb_ref[:, sl] += jnp.sum(df * (1.0 - sf), axis=0, keepdims=True)
            dq_ref[:, sl] = (dqf * (sq * (1.0 + qb * (1.0 - sq)))).astype(bf16)
            di_ref[:, sl] = dvi_s[:, sl].astype(bf16)

    rev = lambda i: nT - 1 - i
    col = lambda c: pl.BlockSpec((HG_T, HGRN_W), lambda i: (rev(i), c))
    tile = pl.BlockSpec((HG_T, HGRN_W), lambda i: (rev(i), 0))
    whole = lambda a: pl.BlockSpec(a.shape, lambda i: (0, 0))
    vec = pl.BlockSpec((1, HGRN_W), lambda i: (0, 0))
    tb = lambda: pltpu.VMEM((HG_T, HGRN_W), bf16)
    tf = lambda: pltpu.VMEM((HG_T, HGRN_W), f32)
    call = dict(in_specs=[col(0), col(1), col(2), col(3), whole(lbl), whole(wn), tile,
                          pl.BlockSpec((HG_T // CHUNK, HGRN_W, HGRN_HD), lambda i: (rev(i), 0, 0)),
                          pl.BlockSpec((HG_T, HGRN_W), lambda i: (rev(i), 1))],
                out_specs=[tile, tile, tile, tile, vec, vec],
                out_shape=[jax.ShapeDtypeStruct((S, HGRN_W), bf16)] * 4 + [jax.ShapeDtypeStruct((1, HGRN_W), f32)] * 2,
                scratch_shapes=[pltpu.VMEM((N_HH, HGRN_HD, HGRN_HD), f32), tb(), tb(), tf(), tb(), tf(), tf(), tf(), tf(), tf(),
                                tf(), tf()])
    call, body, more = _ride(call, rider, body, lambda: pl.program_id(0), nT, 9, 6, 12)
    return pl.pallas_call(body, name="hgrn_bwd", grid=(nT,), compiler_params=_cp("arbitrary"), **call)(
        hp, hp, hp, hp, lbl, wn, o_sav, states, dmix, *more)


def _out_proj(x, ya, yb, wout, w2):
    S = x.shape[0]
    tm = 512

    def body(x_ref, ya_ref, yb_ref, w_ref, w2_ref, h1_ref, u2_ref, mix_ref):
        mixed = jnp.concatenate([ya_ref[...], yb_ref[...]], axis=1).astype(bf16)
        mix_ref[...] = mixed
        h1 = x_ref[...] + _dot(mixed, w_ref[...])
        h1_ref[...] = h1
        r = lax.rsqrt(jnp.mean(h1 * h1, axis=-1, keepdims=True) + EPS)
        u2_ref[...] = (h1 * r * w2_ref[...]).astype(bf16)

    row = lambda w: pl.BlockSpec((tm, w), lambda i: (i, 0))
    return pl.pallas_call(
        body, name="out_proj", grid=(S // tm,),
        in_specs=[row(D_MODEL), row(ATTN_W), row(HGRN_W), pl.BlockSpec((D_MODEL, D_MODEL), lambda i: (0, 0)),
                  pl.BlockSpec((1, D_MODEL), lambda i: (0, 0))],
        out_specs=[row(D_MODEL), row(D_MODEL), row(D_MODEL)],
        out_shape=[jax.ShapeDtypeStruct((S, D_MODEL), f32), jax.ShapeDtypeStruct((S, D_MODEL), bf16),
                   jax.ShapeDtypeStruct((S, D_MODEL), bf16)],
        compiler_params=_cp("arbitrary"),
    )(x, ya, yb, wout, w2)


def _gate_up(u2, wgu):
    S = u2.shape[0]
    tm, tn = 512, 1408
    nj = FFN // tn

    def body(u_ref, wg_ref, wu_ref, g_ref, up_ref, a_ref):
        u = u_ref[...]
        g = _dot(u, wg_ref[...])
        up = _dot(u, wu_ref[...])
        g_ref[...] = g.astype(bf16)
        up_ref[...] = up.astype(bf16)
        a_ref[...] = (g * _sigmoid(g) * up).astype(bf16)

    out = pl.BlockSpec((tm, tn), lambda j, i: (i, j))
    return pl.pallas_call(
        body, name="gate_up", grid=(nj, S // tm),
        in_specs=[pl.BlockSpec((tm, D_MODEL), lambda j, i: (i, 0)), pl.BlockSpec((D_MODEL, tn), lambda j, i: (0, j)),
                  pl.BlockSpec((D_MODEL, tn), lambda j, i: (0, j + nj))],
        out_specs=[out, out, out],
        out_shape=[jax.ShapeDtypeStruct((S, FFN), bf16)] * 3,
        compiler_params=_cp("arbitrary", "arbitrary"),
    )(u2, wgu, wgu)


def _rms_bwd(dyw, hn, r):
    return r * (dyw - hn * jnp.mean(dyw * hn, axis=-1, keepdims=True))


def _down_loss(act, wdown, h1, tgt, w3):
    S = act.shape[0]
    tm = 256

    def body(a_ref, w_ref, h1_ref, t_ref, w3_ref, dh2_ref, loss_ref, gw3_ref):
        @pl.when(pl.program_id(0) == 0)
        def _():
            loss_ref[...] = jnp.zeros_like(loss_ref)
            gw3_ref[...] = jnp.zeros_like(gw3_ref)

        h2 = h1_ref[...] + _dot(a_ref[...], w_ref[...])
        r = lax.rsqrt(jnp.mean(h2 * h2, axis=-1, keepdims=True) + EPS)
        hn = h2 * r
        w3 = w3_ref[...]
        err = hn * w3 - t_ref[...]
        loss_ref[...] += (0.5 / D_MODEL) * jnp.sum(err * err)
        dy = err * (1.0 / D_MODEL)
        gw3_ref[...] += jnp.sum(dy * hn, axis=0, keepdims=True)
        dh2_ref[...] = _rms_bwd(dy * w3, hn, r)

    row = lambda w: pl.BlockSpec((tm, w), lambda i: (i, 0))
    return pl.pallas_call(
        body, name="down_loss", grid=(S // tm,),
        in_specs=[row(FFN), pl.BlockSpec((FFN, D_MODEL), lambda i: (0, 0)), row(D_MODEL), row(D_MODEL),
                  pl.BlockSpec((1, D_MODEL), lambda i: (0, 0))],
        out_specs=[row(D_MODEL), pl.BlockSpec((1, 128), lambda i: (0, 0)), pl.BlockSpec((1, D_MODEL), lambda i: (0, 0))],
        out_shape=[jax.ShapeDtypeStruct((S, D_MODEL), f32), jax.ShapeDtypeStruct((1, 128), f32),
                   jax.ShapeDtypeStruct((1, D_MODEL), f32)],
        compiler_params=_cp("arbitrary"),
    )(act, wdown, h1, tgt, w3)


def _dact(dh2, wdown, gate, up):
    S = dh2.shape[0]
    tm = 256

    def body(d_ref, w_ref, g_ref, u_ref, dg_ref, du_ref):
        da = _dot_nt(d_ref[...].astype(bf16), w_ref[...])
        g = g_ref[...].astype(f32)
        sg = _sigmoid(g)
        du_ref[...] = (da * g * sg).astype(bf16)
        dg_ref[...] = (da * u_ref[...].astype(f32) * (sg * (1.0 + g * (1.0 - sg)))).astype(bf16)

    row = lambda w: pl.BlockSpec((tm, w), lambda i: (i, 0))
    return pl.pallas_call(
        body, name="dact", grid=(S // tm,),
        in_specs=[row(D_MODEL), pl.BlockSpec((FFN, D_MODEL), lambda i: (0, 0)), row(FFN), row(FFN)],
        out_specs=[row(FFN), row(FFN)],
        out_shape=[jax.ShapeDtypeStruct((S, FFN), bf16)] * 2,
        compiler_params=_cp("arbitrary"),
    )(dh2, wdown, gate, up)


def _dgu(dgate, dup, wgu, h1, w2, dh2, wout, rider=None):
    S = dgate.shape[0]
    tm = 256

    def body(dg_ref, du_ref, wg_ref, wu_ref, h1_ref, w2_ref, dh2_ref, wo_ref, dh1_ref, gw2_ref, dmix_ref):
        @pl.when(pl.program_id(0) == 0)
        def _():
            gw2_ref[...] = jnp.zeros_like(gw2_ref)

        du2 = _dot_nt(dg_ref[...], wg_ref[...]) + _dot_nt(du_ref[...], wu_ref[...])
        h1 = h1_ref[...]
        r = lax.rsqrt(jnp.mean(h1 * h1, axis=-1, keepdims=True) + EPS)
        hn = h1 * r
        gw2_ref[...] += jnp.sum(du2 * hn, axis=0, keepdims=True)
        dh1 = dh2_ref[...] + _rms_bwd(du2 * w2_ref[...], hn, r)
        dh1_ref[...] = dh1
        dmix_ref[...] = _dot_nt(dh1.astype(bf16), wo_ref[...])

    row = lambda w: pl.BlockSpec((tm, w), lambda i: (i, 0))
    call = dict(in_specs=[row(FFN), row(FFN), pl.BlockSpec((D_MODEL, FFN), lambda i: (0, 0)),
                          pl.BlockSpec((D_MODEL, FFN), lambda i: (0, 1)), row(D_MODEL),
                          pl.BlockSpec((1, D_MODEL), lambda i: (0, 0)), row(D_MODEL),
                          pl.BlockSpec((D_MODEL, D_MODEL), lambda i: (0, 0))],
                out_specs=[row(D_MODEL), pl.BlockSpec((1, D_MODEL), lambda i: (0, 0)), row(D_MODEL)],
                out_shape=[jax.ShapeDtypeStruct((S, D_MODEL), f32), jax.ShapeDtypeStruct((1, D_MODEL), f32),
                           jax.ShapeDtypeStruct((S, D_MODEL), f32)], scratch_shapes=[])
    call, body, more = _ride(call, rider, body, lambda: pl.program_id(0), S // tm, 8, 3, 0)
    return pl.pallas_call(body, name="dgu", grid=(S // tm,), compiler_params=_cp("arbitrary"), **call)(
        dgate, dup, wgu, wgu, h1, w2, dh2, wout, *more)


def _din(dq, dk, dv, dhq, dhf, dhi, dhg, cos_t, sg_t, win, x, w1, dh1):
    S = x.shape[0]
    tm = 256

    def body(dq_ref, dk_ref, dv_ref, dhq_ref, dhf_ref, dhi_ref, dhg_ref, cos_ref, sg_ref, w_ref, x_ref, w1_ref, dh1_ref,
             dp_ref, gx_ref, gw1_ref):
        @pl.when(pl.program_id(0) == 0)
        def _():
            gw1_ref[...] = jnp.zeros_like(gw1_ref)

        cosv, sgv = jnp.tile(cos_ref[...], (1, ATTN_W // 128)), jnp.tile(sg_ref[...], (1, ATTN_W // 128))
        unrope = lambda d: d * cosv - sgv * _swap_halves(d)
        parts = [(unrope(dq_ref[...]) * (HEAD_DIM ** -0.5)).astype(bf16), unrope(dk_ref[...]).astype(bf16),
                 dv_ref[...].astype(bf16), dhq_ref[...], dhf_ref[...], dhi_ref[...], dhg_ref[...]]
        du = jnp.zeros((tm, D_MODEL), f32)
        for j, pj in enumerate(parts):
            dp_ref[:, j * 512:(j + 1) * 512] = pj
            du = du + _dot_nt(pj, w_ref[:, j * 512:(j + 1) * 512])
        xv = x_ref[...]
        r = lax.rsqrt(jnp.mean(xv * xv, axis=-1, keepdims=True) + EPS)
        xn = xv * r
        gw1_ref[...] += jnp.sum(du * xn, axis=0, keepdims=True)
        gx_ref[...] = dh1_ref[...] + _rms_bwd(du * w1_ref[...], xn, r)

    row = lambda w: pl.BlockSpec((tm, w), lambda i: (i, 0))
    vec = pl.BlockSpec((1, D_MODEL), lambda i: (0, 0))
    return pl.pallas_call(
        body, name="din", grid=(S // tm,),
        in_specs=[row(512)] * 7 + [row(128), row(128), pl.BlockSpec((D_MODEL, IN_W), lambda i: (0, 0)), row(D_MODEL), vec,
                                   row(D_MODEL)],
        out_specs=[row(IN_W), row(D_MODEL), vec],
        out_shape=[jax.ShapeDtypeStruct((S, IN_W), bf16), jax.ShapeDtypeStruct((S, D_MODEL), f32),
                   jax.ShapeDtypeStruct((1, D_MODEL), f32)],
        compiler_params=_cp("arbitrary"),
    )(dq, dk, dv, dhq, dhf, dhi, dhg, cos_t, sg_t, win, x, w1, dh1)


def _gw(a, bs, tn, name):
    S, M = a.shape
    N = bs[0].shape[1]
    ts = 1024
    k = len(bs)

    def body(a_ref, *refs):
        @pl.when(pl.program_id(1) == 0)
        def _():
            for o_ref in refs[k:]:
                o_ref[...] = jnp.zeros_like(o_ref)

        at = a_ref[...].astype(bf16)
        for b_ref, o_ref in zip(refs[:k], refs[k:]):
            o_ref[...] += _dot_tn(at, b_ref[...].astype(bf16))

    return pl.pallas_call(
        body, name=name, grid=(N // tn, S // ts),
        in_specs=[pl.BlockSpec((ts, M), lambda j, s: (s, 0))] + [pl.BlockSpec((ts, tn), lambda j, s: (s, j))] * k,
        out_specs=[pl.BlockSpec((M, tn), lambda j, s: (0, j))] * k, out_shape=[jax.ShapeDtypeStruct((M, N), f32)] * k,
        compiler_params=_cp("arbitrary", "arbitrary"),
    )(a, *bs)


MESH = pl.DeviceIdType.MESH
ANY = pl.BlockSpec(memory_space=pl.ANY)
VMEM_SPEC = pl.BlockSpec(memory_space=pltpu.VMEM)


def _pos():
    return lax.axis_index("x"), lax.axis_index("y"), lax.axis_index("c")


def _flip(v, bit):
    return 1 - v if bit else v


def _gather_rider(shards):
    n = len(shards)

    def parts(outs, scratch):
        send_sems, recv_sems, local_sems = scratch[n:]
        x, y, c = _pos()
        chips = [(1 - x, y), (x, 1 - y), (1 - x, 1 - y)]

        def copy(a, k, block, to, src=None):
            dst = outs[a].at[4 * block[0] + 2 * block[1] + block[2]]
            return pltpu.make_async_remote_copy(src_ref=dst if src is None else src, dst_ref=dst, send_sem=send_sems.at[a, k],
                                                recv_sem=recv_sems.at[a, k], device_id=to, device_id_type=MESH)

        bufs = scratch[:n]
        me, sibling = (x, y, c), (x, y, 1 - c)
        own = lambda a: pltpu.make_async_copy(bufs[a], outs[a].at[4 * x + 2 * y + c], local_sems.at[a])
        sent = lambda a: [copy(a, 0, me, sibling, src=bufs[a])] + [copy(a, 1 + j, me, (*chip, c), src=bufs[a])
                                                                   for j, chip in enumerate(chips)]
        passed = lambda a: [copy(a, 4 + j, (*chip, c), sibling) for j, chip in enumerate(chips)]
        landed = lambda a: [copy(a, 1 + j, (*chip, c), me) for j, chip in enumerate(chips)]
        from_sibling = lambda a: [copy(a, 0, sibling, me)] + [copy(a, 4 + j, (*chip, 1 - c), me) for j, chip in enumerate(chips)]
        return bufs, local_sems, own, sent, passed, landed, from_sibling

    def first(ins, outs, scratch):
        bufs, local_sems, own, sent, _, _, _ = parts(outs, scratch)
        loads = [pltpu.make_async_copy(ins[a], bufs[a], local_sems.at[a]) for a in range(n)]
        for ld in loads:
            ld.start()
        for a in range(n):
            loads[a].wait()
            own(a).start()
            for cp in sent(a):
                cp.start()

    def middle(ins, outs, scratch):
        _, _, _, _, passed, landed, _ = parts(outs, scratch)
        for a in range(n):
            for got, on in zip(landed(a), passed(a)):
                got.wait_recv()
                on.start()

    def last(ins, outs, scratch):
        _, _, own, sent, passed, _, from_sibling = parts(outs, scratch)
        for a in range(n):
            for cp in from_sibling(a):
                cp.wait_recv()
        for a in range(n):
            for cp in sent(a) + passed(a):
                cp.wait_send()
            own(a).wait()

    return _Rider(shards, [jax.ShapeDtypeStruct((N_DEV,) + s.shape, s.dtype) for s in shards],
                  [pltpu.VMEM(s.shape, s.dtype) for s in shards]
                  + [pltpu.SemaphoreType.DMA((n, 7)), pltpu.SemaphoreType.DMA((n, 7)), pltpu.SemaphoreType.DMA((n,))],
                  first, last, middle)


def _sibling_rider(grads):
    n = len(grads)

    def copies(g, got, scratch):
        send_sems, recv_sems = scratch
        x, y, c = _pos()
        return [pltpu.make_async_remote_copy(src_ref=g[a].at[2 * q + (1 - c)], dst_ref=got[a].at[q], send_sem=send_sems.at[a, q],
                                             recv_sem=recv_sems.at[a, q], device_id=(x, y, 1 - c), device_id_type=MESH)
                for a in range(n) for q in range(4)]

    def first(g, got, scratch):
        for cp in copies(g, got, scratch):
            cp.start()

    def last(g, got, scratch):
        for cp in copies(g, got, scratch):
            cp.wait()

    return _Rider(grads, [jax.ShapeDtypeStruct((4,) + g.shape[1:], g.dtype) for g in grads],
                  [pltpu.SemaphoreType.DMA((n, 4))] * 2, first, last)


def _chips_rider(sums):
    n = len(sums)

    def copies(s, out, scratch):
        send_sems, recv_sems = scratch
        x, y, c = _pos()
        cps = []
        for a in range(n):
            for f in (1, 2, 3):
                peer = (_flip(x, f >> 1), _flip(y, f & 1), c)
                cps.append(pltpu.make_async_remote_copy(
                    src_ref=s[a].at[2 * peer[0] + peer[1]], dst_ref=out[a].at[f - 1], send_sem=send_sems.at[a, f - 1],
                    recv_sem=recv_sems.at[a, f - 1], device_id=peer, device_id_type=MESH))
        return cps

    def first(s, out, scratch):
        for cp in copies(s, out, scratch):
            cp.start()

    def last(s, out, scratch):
        for cp in copies(s, out, scratch):
            cp.wait()

    return _Rider(sums, [jax.ShapeDtypeStruct((3,) + s.shape[1:], s.dtype) for s in sums],
                  [pltpu.SemaphoreType.DMA((n, 3))] * 2, first, last)


def _alone(rider, name):
    ri, ro = len(rider.ins), len(rider.out_shapes)

    def body(*refs):
        theirs = (refs[:ri], refs[ri:ri + ro], refs[ri + ro:])
        rider.first(*theirs)
        if rider.middle is not None:
            rider.middle(*theirs)
        rider.last(*theirs)

    return pl.pallas_call(body, name=name, in_specs=[ANY] * ri, out_specs=[ANY] * ro, out_shape=rider.out_shapes,
                          scratch_shapes=rider.scratch)(*rider.ins)


def _gather_small(g_w1, g_w2, g_w3, g_lb, g_wn, loss):
    def body(w1_ref, w2_ref, w3_ref, lb_ref, wn_ref, loss_ref, out_ref, pk, send_sems, recv_sems):
        x, y, c = _pos()
        me = 4 * x + 2 * y + c
        pk[...] = jnp.zeros_like(pk)
        pk[0:1, :] = w1_ref[...]
        pk[1:2, :] = w2_ref[...]
        pk[2:3, :] = w3_ref[...]
        pk[3:4, 0:HGRN_W] = lb_ref[...]
        pk[3:4, HGRN_W:2 * HGRN_W] = wn_ref[...]
        pk[4:5, 0:128] = loss_ref[...]
        out_ref[me] = pk[...]
        sends, recvs = [], []
        for k in range(1, N_DEV):
            peer = (_flip(x, k >> 2), _flip(y, (k >> 1) & 1), _flip(c, k & 1))
            cp = pltpu.make_async_remote_copy(src_ref=pk, dst_ref=out_ref.at[me], send_sem=send_sems.at[k - 1],
                                              recv_sem=recv_sems.at[k - 1], device_id=peer, device_id_type=MESH)
            cp.start()
            sends.append(cp)
            recvs.append(pltpu.make_async_remote_copy(src_ref=pk, dst_ref=out_ref.at[4 * peer[0] + 2 * peer[1] + peer[2]],
                                                      send_sem=send_sems.at[k - 1], recv_sem=recv_sems.at[k - 1], device_id=peer,
                                                      device_id_type=MESH))
        for cp in recvs:
            cp.wait_recv()
        for cp in sends:
            cp.wait_send()

    return pl.pallas_call(
        body, name="gather_small", in_specs=[VMEM_SPEC] * 6, out_specs=VMEM_SPEC,
        out_shape=jax.ShapeDtypeStruct((N_DEV, 8, D_MODEL), f32),
        scratch_shapes=[pltpu.VMEM((8, D_MODEL), f32), pltpu.SemaphoreType.DMA((N_DEV - 1,)), pltpu.SemaphoreType.DMA((N_DEV - 1,))],
    )(g_w1, g_w2, g_w3, g_lb, g_wn, loss)


def _row_tile(r):
    return max(t for t in range(8, 257, 8) if r % t == 0)


def _add_sibling(core, g, got, name):
    _, r, c = got.shape
    tr = _row_tile(r)

    def body(core_ref, a_ref, b_ref, o_ref):
        o_ref[...] = (a_ref[...] + b_ref[...]).astype(bf16)

    blk = pl.BlockSpec((1, tr, c), lambda q, i, core_ref: (q, i, 0))
    return pl.pallas_call(
        body, name=name, out_shape=jax.ShapeDtypeStruct(got.shape, bf16),
        grid_spec=pltpu.PrefetchScalarGridSpec(
            num_scalar_prefetch=1, grid=(4, r // tr),
            in_specs=[pl.BlockSpec((1, tr, c), lambda q, i, core_ref: (2 * q + core_ref[0], i, 0)), blk], out_specs=blk),
        compiler_params=_cp("arbitrary", "arbitrary"))(core, g, got)


def _adamw(w, g, m, v):
    m = ADAM_B1 * m + (1.0 - ADAM_B1) * g
    v = ADAM_B2 * v + (1.0 - ADAM_B2) * (g * g)
    m_hat = m / (1.0 - ADAM_B1 ** ADAM_STEP)
    v_hat = v / (1.0 - ADAM_B2 ** ADAM_STEP)
    return -ADAM_LR * (m_hat / (jnp.sqrt(v_hat) + ADAM_EPS) + ADAM_WD * w), m, v


def _adam_shard(where, g, got, pieces, w, m, v, name):
    r, c = w.shape
    tr = _row_tile(r)

    def body(where_ref, g_ref, got_ref, p_ref, w_ref, m_ref, v_ref, g_out, d_out, m_out, v_out):
        gsum = g_ref[0] + got_ref[0]
        for f in range(3):
            gsum = gsum + p_ref[f].astype(f32)
        g_out[...] = gsum
        d_out[...], m_out[...], v_out[...] = _adamw(w_ref[...], gsum, m_ref[...], v_ref[...])

    blk = pl.BlockSpec((tr, c), lambda i, where_ref: (i, 0))
    return pl.pallas_call(
        body, name=name, out_shape=[jax.ShapeDtypeStruct((r, c), f32)] * 4,
        grid_spec=pltpu.PrefetchScalarGridSpec(
            num_scalar_prefetch=1, grid=(r // tr,),
            in_specs=[pl.BlockSpec((1, tr, c), lambda i, where_ref: (where_ref[0], i, 0)),
                      pl.BlockSpec((1, tr, c), lambda i, where_ref: (where_ref[1], i, 0)),
                      pl.BlockSpec((3, tr, c), lambda i, where_ref: (0, i, 0)), blk, blk, blk],
            out_specs=[blk] * 4),
        compiler_params=_cp("arbitrary"),
    )(where, g, got, pieces, w, m, v)


def _small_update(gath, params):
    def body(gath_ref, *refs):
        ins, outs = refs[:15], refs[15:]
        gs = gath_ref[0]
        for k in range(1, N_DEV):
            gs = gs + gath_ref[k]
        outs[0][...] = gs[4:5, 0:128]
        l0, l1 = ins[9][0:1, :], ins[9][1:2, :]
        lb = _sigmoid(l0 - l1)
        d0 = gs[3:4, 0:HGRN_W] * lb * (1.0 - lb)
        first_row = lax.broadcasted_iota(jnp.int32, (2, HGRN_W), 0) == 0
        grads = [gs[0:1, :], gs[1:2, :], gs[2:3, :], jnp.where(first_row, d0, -d0), gs[3:4, HGRN_W:2 * HGRN_W]]
        for i, g in enumerate(grads):
            w_ref, m_ref, v_ref = ins[3 * i:3 * i + 3]
            o = outs[1 + 4 * i:5 + 4 * i]
            o[0][...] = g
            o[1][...], o[2][...], o[3][...] = _adamw(w_ref[...], g, m_ref[...], v_ref[...])

    flat = [a for p in params for a in p]
    out_shape = [jax.ShapeDtypeStruct((1, 128), f32)] + [jax.ShapeDtypeStruct(p[0].shape, f32) for p in params for _ in range(4)]
    outs = pl.pallas_call(body, name="small_update", in_specs=[VMEM_SPEC] * 16, out_specs=[VMEM_SPEC] * 21, out_shape=out_shape)(gath, *flat)
    return outs[0], [outs[1 + 4 * i:5 + 4 * i] for i in range(5)]


def kernel(x, norm1_w, w_in, lb_logits, hgrn_norm_w, w_out, norm2_w, w_gate_up, w_down, final_norm_w, loss_target, m_norm1_w, m_w_in, m_lb_logits, m_hgrn_norm_w, m_w_out, m_norm2_w, m_w_gate_up, m_w_down, m_final_norm_w, v_norm1_w, v_w_in, v_lb_logits, v_hgrn_norm_w, v_w_out, v_norm2_w, v_w_gate_up, v_w_down, v_final_norm_w):
    row = lambda a: a.reshape(1, D_MODEL)
    by_owner = lambda g, w: jnp.transpose(g.reshape(g.shape[0], g.shape[1] // w, w), (1, 0, 2))
    ix, iy, ic = lax.axis_index("x"), lax.axis_index("y"), lax.axis_index("c")
    core = jnp.stack([ic]).astype(jnp.int32)
    where = jnp.stack([4 * ix + 2 * iy + ic, 2 * ix + iy]).astype(jnp.int32)
    xs, tgt, w3 = x[0], loss_target[0], row(final_norm_w)
    S = xs.shape[0]

    cos_t, sg_t, win_g = _rope_tables(S, _gather_rider([w_in[0].astype(bf16)]))
    win = jnp.transpose(win_g, (1, 0, 2)).reshape(D_MODEL, IN_W)
    u, qkv, hp = _in_proj(xs, norm1_w, win, cos_t, sg_t)
    ya, lse, wout_g, wgu_g, wdown_g = _attn_fwd(qkv, _gather_rider([w_out[0].astype(bf16), w_gate_up[0].astype(bf16),
                                                                     w_down[0].astype(bf16)]))
    wout = wout_g.reshape(D_MODEL, D_MODEL)
    wgu = jnp.transpose(wgu_g, (1, 0, 2)).reshape(D_MODEL, 2 * FFN)
    wdown = wdown_g.reshape(FFN, D_MODEL)
    yb, o_sav, states = _hgrn_fwd(hp, lb_logits, hgrn_norm_w)
    h1, u2, mixed = _out_proj(xs, ya, yb, wout, norm2_w)
    gate, up, act = _gate_up(u2, wgu)
    dh2, loss_p, g_w3 = _down_loss(act, wdown, h1, tgt, w3)

    (g_wdown,) = _gw(act, [dh2], 512, "gw_down")
    dgate, dup = _dact(dh2, wdown, gate, up)
    g_wgu = _gw(u2, [dgate, dup], 1408, "gw_gate_up")
    early = [jnp.concatenate([by_owner(g, 2 * FFN // N_DEV) for g in g_wgu], axis=0), g_wdown.reshape(N_DEV, FFN // N_DEV, D_MODEL)]
    dh1, g_w2, dmix, *got_early = _dgu(dgate, dup, wgu, h1, norm2_w, dh2, wout, _sibling_rider(early))
    sums_early = [_add_sibling(core, g, o, f"add_sibling_{i}") for i, (g, o) in enumerate(zip(early, got_early))]
    (g_wout,) = _gw(mixed, [dh1], 1024, "gw_out")
    dhq, dhf, dhi, dhg, g_wn, g_lb, *pieces_early = _hgrn_bwd(hp, lb_logits, hgrn_norm_w, o_sav, states, dmix, _chips_rider(sums_early))
    datt = _attn_bwd(qkv, ya, lse, dmix)
    dproj, gx, g_w1 = _din(*datt, dhq, dhf, dhi, dhg, cos_t, sg_t, win, xs, norm1_w, dh1)
    (g_win,) = _gw(u, [dproj], 896, "gw_in")
    late = [by_owner(g_win, IN_W // N_DEV), g_wout.reshape(N_DEV, D_MODEL // N_DEV, D_MODEL)]
    got_late = _alone(_sibling_rider(late), "reduce_sibling")
    sums_late = [_add_sibling(core, g, o, f"add_sibling_{2 + i}") for i, (g, o) in enumerate(zip(late, got_late))]
    pieces_late = _alone(_chips_rider(sums_late), "reduce_chips")

    grads = [late[0], late[1], early[0], early[1]]
    got = [got_late[0], got_late[1], got_early[0], got_early[1]]
    pieces = [pieces_late[0], pieces_late[1], pieces_early[0], pieces_early[1]]
    shards = [w_in[0], w_out[0], w_gate_up[0], w_down[0]]
    moms = [(m_w_in[0], v_w_in[0]), (m_w_out[0], v_w_out[0]), (m_w_gate_up[0], v_w_gate_up[0]), (m_w_down[0], v_w_down[0])]
    big = [_adam_shard(where, g, o, p, w, m, v, f"adam_{i}")
           for i, (g, o, p, w, (m, v)) in enumerate(zip(grads, got, pieces, shards, moms))]
    big = [[a[None] for a in four] for four in big]

    gath = _gather_small(g_w1, g_w2, g_w3, g_lb, g_wn, loss_p)
    params = [(norm1_w, m_norm1_w, v_norm1_w), (norm2_w, m_norm2_w, v_norm2_w),
              (row(final_norm_w), row(m_final_norm_w), row(v_final_norm_w)),
              (lb_logits, m_lb_logits, v_lb_logits), (hgrn_norm_w, m_hgrn_norm_w, v_hgrn_norm_w)]
    loss, (s_w1, s_w2, s_w3, s_lb, s_wn) = _small_update(gath, params)
    s_w3 = [a.reshape(D_MODEL) for a in s_w3]
    per_w = [s_w1, big[0], s_lb, s_wn, big[1], s_w2, big[2], big[3], s_w3]
    return (loss[0, 0], gx[None], *[p[0] for p in per_w], *[p[1] for p in per_w], *[p[2] for p in per_w], *[p[3] for p in per_w])
```

```python
import jax
import jax.numpy as jnp
from jax import lax
from jax.experimental import pallas as pl
from jax.experimental.pallas import tpu as pltpu

f32, bf16 = jnp.float32, jnp.bfloat16

D_MODEL = 1024
ATTN_W = 512
HEAD_DIM = 64
ATTN_BLK = 128
DILATIONS = (1, 4, 16)
HGRN_W = 512
HGRN_HD = 128
CHUNK = 16
IN_W = 3 * ATTN_W + 4 * HGRN_W
FFN = 2816
EPS = 1e-6
ROPE_THETA = 10000.0
NEG = -1e30
N_DEV = 8
ADAM_LR, ADAM_B1, ADAM_B2, ADAM_EPS, ADAM_WD, ADAM_STEP = 0.001, 0.9, 0.999, 1e-08, 0.01, 10
VMEM_LIMIT = 56 * 1024 * 1024


def _cp(*sem):
    return pltpu.CompilerParams(dimension_semantics=sem, vmem_limit_bytes=VMEM_LIMIT)


def _dot(a, b):
    return jnp.dot(a, b, preferred_element_type=f32)


def _dot_nt(a, b):
    return lax.dot_general(a, b, (((1,), (1,)), ((), ())), preferred_element_type=f32)


def _dot_tn(a, b):
    return lax.dot_general(a, b, (((0,), (0,)), ((), ())), preferred_element_type=f32)


def _sigmoid(x):
    return 0.5 * jnp.tanh(0.5 * x) + 0.5


class _Rider:
    def __init__(self, ins, out_shapes, scratch, first, last, middle=None):
        self.ins, self.out_shapes, self.scratch = list(ins), list(out_shapes), list(scratch)
        self.first, self.middle, self.last = first, middle, last


def _ride(call, rider, body, step, n_steps, n_in, n_out, n_scratch):
    if rider is None:
        return call, body, []
    ri, ro = len(rider.ins), len(rider.out_shapes)
    any_spec = pl.BlockSpec(memory_space=pl.ANY)
    call = dict(call, in_specs=call["in_specs"] + [any_spec] * ri, out_specs=call["out_specs"] + [any_spec] * ro,
                out_shape=call["out_shape"] + rider.out_shapes, scratch_shapes=call["scratch_shapes"] + rider.scratch)

    def riding(*refs):
        a = n_in + ri
        b = a + n_out + ro
        mine = refs[:n_in] + refs[a:a + n_out] + refs[b:b + n_scratch]
        theirs = (refs[n_in:a], refs[a + n_out:b], refs[b + n_scratch:])
        t = step()

        @pl.when(t == 0)
        def _():
            rider.first(*theirs)

        body(*mine)
        if rider.middle is not None:
            @pl.when(t == n_steps // 2)
            def _():
                rider.middle(*theirs)

        @pl.when(t == n_steps - 1)
        def _():
            rider.last(*theirs)

    return call, riding, rider.ins


def _rope_tables(S, rider=None):
    half = HEAD_DIM // 2
    tm = 256
    inv_freq = jnp.tile(ROPE_THETA ** (-jnp.arange(half, dtype=f32) / half), 128 // half).reshape(1, 128)
    sign = jnp.tile(jnp.concatenate([-jnp.ones((half,), f32), jnp.ones((half,), f32)]), 128 // HEAD_DIM).reshape(1, 128)

    def body(inv_ref, sign_ref, cos_ref, sg_ref):
        pos = (lax.broadcasted_iota(jnp.int32, (tm, 128), 0) + pl.program_id(0) * tm).astype(f32)
        ang = pos * inv_ref[...]
        cos_ref[...] = jnp.cos(ang)
        sg_ref[...] = jnp.sin(ang) * sign_ref[...]

    vec = pl.BlockSpec((1, 128), lambda i: (0, 0))
    out = pl.BlockSpec((tm, 128), lambda i: (i, 0))
    call = dict(in_specs=[vec, vec], out_specs=[out, out], out_shape=[jax.ShapeDtypeStruct((S, 128), f32)] * 2, scratch_shapes=[])
    call, body, more = _ride(call, rider, body, lambda: pl.program_id(0), S // tm, 2, 2, 0)
    return pl.pallas_call(body, name="rope_tables", grid=(S // tm,), compiler_params=_cp("arbitrary"), **call)(inv_freq, sign, *more)


def _swap_halves(v):
    n = v.shape[1]
    lane = lax.broadcasted_iota(jnp.int32, v.shape, 1)
    return jnp.where((lane % HEAD_DIM) < HEAD_DIM // 2, pltpu.roll(v, n - HEAD_DIM // 2, 1), pltpu.roll(v, HEAD_DIM // 2, 1))


def _in_proj(x, w1, win, cos_t, sg_t):
    S = x.shape[0]
    tm = 256

    def body(x_ref, w1_ref, w_ref, cos_ref, sg_ref, u_ref, qkv_ref, hp_ref):
        xv = x_ref[...]
        r = lax.rsqrt(jnp.mean(xv * xv, axis=-1, keepdims=True) + EPS)
        u = (xv * r * w1_ref[...]).astype(bf16)
        u_ref[...] = u
        cosv, sgv = jnp.tile(cos_ref[...], (1, ATTN_W // 128)), jnp.tile(sg_ref[...], (1, ATTN_W // 128))
        for j in range(3):
            pj = _dot(u, w_ref[:, j * ATTN_W:(j + 1) * ATTN_W])
            if j < 2:
                pj = pj * cosv + _swap_halves(pj) * sgv
            if j == 0:
                pj = pj * (HEAD_DIM ** -0.5)
            qkv_ref[:, j * ATTN_W:(j + 1) * ATTN_W] = pj.astype(bf16)
        for j in range(4):
            lo = 3 * ATTN_W + j * HGRN_W
            hp_ref[:, j * HGRN_W:(j + 1) * HGRN_W] = _dot(u, w_ref[:, lo:lo + HGRN_W])

    return pl.pallas_call(
        body, name="in_proj", grid=(S // tm,),
        in_specs=[pl.BlockSpec((tm, D_MODEL), lambda i: (i, 0)), pl.BlockSpec((1, D_MODEL), lambda i: (0, 0)),
                  pl.BlockSpec((D_MODEL, IN_W), lambda i: (0, 0)),
                  pl.BlockSpec((tm, 128), lambda i: (i, 0)), pl.BlockSpec((tm, 128), lambda i: (i, 0))],
        out_specs=[pl.BlockSpec((tm, D_MODEL), lambda i: (i, 0)), pl.BlockSpec((tm, 3 * ATTN_W), lambda i: (i, 0)),
                   pl.BlockSpec((tm, 4 * HGRN_W), lambda i: (i, 0))],
        out_shape=[jax.ShapeDtypeStruct((S, D_MODEL), bf16), jax.ShapeDtypeStruct((S, 3 * ATTN_W), bf16),
                   jax.ShapeDtypeStruct((S, 4 * HGRN_W), f32)],
        compiler_params=_cp("arbitrary"),
    )(x, w1, win, cos_t, sg_t)


def _head_masks():
    lane = lax.broadcasted_iota(jnp.int32, (ATTN_BLK, 128), 1)
    even = lane < HEAD_DIM
    return even, (even, jnp.logical_not(even))


def _pair_fwd(q2, k2, v2, bias):
    even, masks = _head_masks()
    outs, lses = [], []
    for e in range(2):
        qm = jnp.where(masks[e], q2, 0.0).astype(bf16)
        s = _dot_nt(qm, k2) + bias
        m = jnp.max(s, axis=-1, keepdims=True)
        pe = jnp.exp(s - m)
        lsum = jnp.sum(pe, axis=-1, keepdims=True)
        outs.append(_dot(pe.astype(bf16), v2) / lsum)
        lses.append(jnp.broadcast_to(m + jnp.log(lsum), (ATTN_BLK, 128)))
    return jnp.where(even, outs[0], outs[1]), jnp.where(even, lses[0], lses[1])


def _merge(y0, l0, y1, l1):
    mx = jnp.maximum(l0, l1)
    a, b = jnp.exp(l0 - mx), jnp.exp(l1 - mx)
    tot = a + b
    return (a * y0 + b * y1) / tot, mx + jnp.log(tot)


def _pair_bwd(q2, k2f, v2, dy2, lse2, delta2, bias):
    _, masks = _head_masks()
    k2 = k2f.astype(bf16)
    klane = lax.broadcasted_iota(jnp.int32, (2 * ATTN_BLK, 128), 1) < HEAD_DIM
    kmasks = (klane, jnp.logical_not(klane))
    dq2 = jnp.zeros((ATTN_BLK, 128), f32)
    pes, dss, qms, dyms = [], [], [], []
    for e in range(2):
        c0 = e * HEAD_DIM
        qm = jnp.where(masks[e], q2, 0.0).astype(bf16)
        km = jnp.where(kmasks[e], k2f, 0.0).astype(bf16)
        dym = jnp.where(masks[e], dy2, 0.0).astype(bf16)
        pe = jnp.exp(_dot_nt(qm, k2) + bias - lse2[:, c0:c0 + 1])
        ds = (pe * (_dot_nt(dym, v2) - delta2[:, c0:c0 + 1])).astype(bf16)
        dq2 = dq2 + _dot(ds, km)
        pes.append(pe.astype(bf16))
        dss.append(ds)
        qms.append(qm)
        dyms.append(dym)
    dv2 = _dot_tn(jnp.concatenate(pes, axis=0), jnp.concatenate(dyms, axis=0))
    dk2 = _dot_tn(jnp.concatenate(dss, axis=0), jnp.concatenate(qms, axis=0))
    return dq2, dk2, dv2


TOK = 2048


def _key_bias():
    qi = lax.broadcasted_iota(jnp.int32, (ATTN_BLK, 2 * ATTN_BLK), 0)
    kj = lax.broadcasted_iota(jnp.int32, (ATTN_BLK, 2 * ATTN_BLK), 1)
    delta = ATTN_BLK + qi - kj
    seen = (delta >= 0) & (delta <= ATTN_BLK)
    return jnp.where(seen, 0.0, NEG), jnp.where(seen & (kj >= ATTN_BLK), 0.0, NEG)


def _attn_fwd(qkv, rider=None):
    S = qkv.shape[0]
    nS = S // TOK

    def body(q_ref, kp_ref, kc_ref, vp_ref, vc_ref, y_ref, l_ref, qs, k2, v2, ay, al):
        n = pl.program_id(1)
        qs[...] = q_ref[...].astype(f32)
        k2[0:TOK] = kp_ref[...].astype(f32)
        k2[TOK:2 * TOK] = kc_ref[...].astype(f32)
        v2[0:TOK] = vp_ref[...].astype(f32)
        v2[TOK:2 * TOK] = vc_ref[...].astype(f32)
        bias_any, bias_first = _key_bias()

        def block(dil, r, b, step, last):
            start = r + pl.multiple_of(step * b, step)
            rows = pl.ds(start, ATTN_BLK, stride=dil) if dil > 1 else pl.ds(start, ATTN_BLK)
            keys = (pl.ds(TOK + start - step, 2 * ATTN_BLK, stride=dil) if dil > 1
                    else pl.ds(TOK + start - step, 2 * ATTN_BLK))
            bias = jnp.where((n == 0) & (b == 0), bias_first, bias_any)
            out, lse = _pair_fwd(qs[rows, :], k2[keys, :].astype(bf16), v2[keys, :].astype(bf16), bias)
            if dil < DILATIONS[-1]:
                out, lse = _merge(ay[rows, :], al[rows, :], out, lse)
            if last:
                y_ref[rows, :] = out
                l_ref[rows, :] = lse
            else:
                ay[rows, :] = out
                al[rows, :] = lse

        for dil in reversed(DILATIONS):
            def loop(i, carry, dil=dil):
                block(dil, i % dil, i // dil, ATTN_BLK * dil, dil == 1)
                return carry
            lax.fori_loop(0, TOK // ATTN_BLK, loop, 0, unroll=4)

    blk = (TOK, 128)
    cur = lambda c: pl.BlockSpec(blk, lambda p, n: (n, 4 * c + p))
    prv = lambda c: pl.BlockSpec(blk, lambda p, n: (jnp.maximum(n - 1, 0), 4 * c + p))
    out = pl.BlockSpec(blk, lambda p, n: (n, p))
    call = dict(in_specs=[cur(0), prv(1), cur(1), prv(2), cur(2)], out_specs=[out, out],
                out_shape=[jax.ShapeDtypeStruct((S, ATTN_W), f32)] * 2,
                scratch_shapes=[pltpu.VMEM(blk, f32), pltpu.VMEM((2 * TOK, 128), f32), pltpu.VMEM((2 * TOK, 128), f32),
                                pltpu.VMEM(blk, f32), pltpu.VMEM(blk, f32)])
    call, body, more = _ride(call, rider, body, lambda: pl.program_id(0) * nS + pl.program_id(1), (ATTN_W // 128) * nS, 5, 2, 5)
    return pl.pallas_call(body, name="attention_fwd", grid=(ATTN_W // 128, nS), compiler_params=_cp("arbitrary", "arbitrary"),
                          **call)(qkv, qkv, qkv, qkv, qkv, *more)


def _attn_bwd(qkv, ya, lse, dmix, rider=None):
    S = qkv.shape[0]
    nS = S // TOK

    def body(q_ref, kp_ref, kc_ref, vp_ref, vc_ref, y_ref, l_ref, dy_ref, dq_ref, dk_ref, dv_ref, qs, k2, v2, dk2, dv2, dqa, dl):
        n = pl.program_id(1)

        @pl.when(n == 0)
        def _():
            dk2[...] = jnp.zeros_like(dk2)
            dv2[...] = jnp.zeros_like(dv2)

        @pl.when(n < nS)
        def _():
            qs[...] = q_ref[...].astype(f32)
            k2[0:TOK] = kp_ref[...].astype(f32)
            k2[TOK:2 * TOK] = kc_ref[...].astype(f32)
            v2[0:TOK] = vp_ref[...].astype(f32)
            v2[TOK:2 * TOK] = vc_ref[...].astype(f32)
            li = lax.broadcasted_iota(jnp.int32, (128, 128), 0)
            lj = lax.broadcasted_iota(jnp.int32, (128, 128), 1)
            seg = jnp.where((li // HEAD_DIM) == (lj // HEAD_DIM), 1.0, 0.0).astype(bf16)
            bias_any, bias_first = _key_bias()

            def delta_rows(t, carry):
                rows = pl.ds(pl.multiple_of(256 * t, 256), 256)
                dyy = dy_ref[rows, :] * y_ref[rows, :]
                hi = dyy.astype(bf16)
                dl[rows, :] = _dot(hi, seg) + _dot((dyy - hi.astype(f32)).astype(bf16), seg)
                return carry

            lax.fori_loop(0, TOK // 256, delta_rows, 0)

            def block(dil, r, b, step, first_pattern, last):
                start = r + pl.multiple_of(step * b, step)
                rows = pl.ds(start, ATTN_BLK, stride=dil) if dil > 1 else pl.ds(start, ATTN_BLK)
                keys = (pl.ds(TOK + start - step, 2 * ATTN_BLK, stride=dil) if dil > 1
                        else pl.ds(TOK + start - step, 2 * ATTN_BLK))
                bias = jnp.where((n == 0) & (b == 0), bias_first, bias_any)
                dq2, dkk, dvv = _pair_bwd(qs[rows, :], k2[keys, :], v2[keys, :].astype(bf16), dy_ref[rows, :],
                                          l_ref[rows, :], dl[rows, :], bias)
                if last:
                    dq_ref[rows, :] = dqa[rows, :] + dq2
                elif first_pattern:
                    dqa[rows, :] = dq2
                else:
                    dqa[rows, :] += dq2
                dk2[keys, :] += dkk
                dv2[keys, :] += dvv

            for dil in reversed(DILATIONS):
                def loop(i, carry, dil=dil):
                    block(dil, i % dil, i // dil, ATTN_BLK * dil, dil == DILATIONS[-1], dil == 1)
                    return carry
                lax.fori_loop(0, TOK // ATTN_BLK, loop, 0, unroll=4)

        dk_ref[...] = dk2[0:TOK]
        dv_ref[...] = dv2[0:TOK]
        dk2[0:TOK] = dk2[TOK:2 * TOK]
        dv2[0:TOK] = dv2[TOK:2 * TOK]
        dk2[TOK:2 * TOK] = jnp.zeros((TOK, 128), f32)
        dv2[TOK:2 * TOK] = jnp.zeros((TOK, 128), f32)

    blk = (TOK, 128)
    cn = lambda n: jnp.minimum(n, nS - 1)
    pn = lambda n: jnp.clip(n - 1, 0, nS - 1)
    cur = lambda c: pl.BlockSpec(blk, lambda p, n: (cn(n), 4 * c + p))
    prv = lambda c: pl.BlockSpec(blk, lambda p, n: (pn(n), 4 * c + p))
    at_n = pl.BlockSpec(blk, lambda p, n: (cn(n), p))
    at_p = pl.BlockSpec(blk, lambda p, n: (pn(n), p))
    big = lambda: pltpu.VMEM((2 * TOK, 128), f32)
    call = dict(in_specs=[cur(0), prv(1), cur(1), prv(2), cur(2), at_n, at_n, at_n], out_specs=[at_n, at_p, at_p],
                out_shape=[jax.ShapeDtypeStruct((S, ATTN_W), f32)] * 3,
                scratch_shapes=[pltpu.VMEM(blk, f32), big(), big(), big(), big(), pltpu.VMEM(blk, f32), pltpu.VMEM(blk, f32)])
    call, body, more = _ride(call, rider, body, lambda: pl.program_id(0) * (nS + 1) + pl.program_id(1),
                             (ATTN_W // 128) * (nS + 1), 8, 3, 7)
    return pl.pallas_call(body, name="attention_bwd", grid=(ATTN_W // 128, nS + 1), compiler_params=_cp("arbitrary", "arbitrary"),
                          **call)(qkv, qkv, qkv, qkv, qkv, ya, lse, dmix, *more)


HG_T = 256
N_HH = HGRN_W // HGRN_HD
HG_SUB = 128
SAFE_RANGE = 80.0


def _row_in_chunk():
    return lax.broadcasted_iota(jnp.int32, (HG_T, HGRN_HD), 0) % CHUNK


def _chunk_cumsum(v, rc):
    for k in (1, 2, 4, 8):
        v = v + jnp.where(rc >= k, pltpu.roll(v, k, 0), 0.0)
    return v


def _chunk_rcumsum(v, rc):
    for k in (1, 2, 4, 8):
        v = v + jnp.where(rc < CHUNK - k, pltpu.roll(v, HG_T - k, 0), 0.0)
    return v


def _hgrn_gates(qb, fb, lb):
    sf = _sigmoid(fb)
    f = lb + (1.0 - lb) * sf
    sq = _sigmoid(qb)
    return sf, f, jnp.log(f), 1.0 - f, sq, qb * sq


def _hgrn_prep(qb, fb, lbl2, rc):
    lb = _sigmoid(lbl2[0:1, :] - lbl2[1:2, :])
    sf, f, lf, key, sq, qf = _hgrn_gates(qb, fb, lb)
    b = _chunk_cumsum(lf, rc)
    rem = _chunk_rcumsum(lf, rc) - lf
    return dict(lb=lb, sf=sf, f=f, key=key, sq=sq, qf=qf, b=b, rem=rem, eb=jnp.exp(b), er=jnp.exp(rem))


def _chunk_mask():
    r = lax.broadcasted_iota(jnp.int32, (HG_SUB, HG_SUB), 0)
    c = lax.broadcasted_iota(jnp.int32, (HG_SUB, HG_SUB), 1)
    return ((r // CHUNK) == (c // CHUNK)) & (c <= r)


def _hgrn_fwd(hp, lbl, wn):
    S = hp.shape[0]
    nT = S // HG_T

    def body(qb_ref, fb_ref, ib_ref, gb_ref, lbl_ref, wn_ref, yb_ref, o_ref, st_ref, ST, qt_s, kh_s, dec_s, oi_s):
        @pl.when(pl.program_id(0) == 0)
        def _():
            ST[...] = jnp.zeros_like(ST)

        rc = _row_in_chunk()
        for h in range(N_HH):
            sl = slice(HGRN_HD * h, HGRN_HD * (h + 1))
            p = _hgrn_prep(qb_ref[:, sl], fb_ref[:, sl], lbl_ref[:, sl], rc)
            qf, key, b = p["qf"], p["key"], p["b"]
            qt = qf * p["eb"]
            qt_s[:, sl] = qt.astype(bf16)
            kh_s[:, sl] = (key * p["er"]).astype(bf16)
            dec_s[:, sl] = jnp.exp(b + p["rem"])
            rng = jnp.max(-(b + p["rem"]))

            @pl.when(rng < SAFE_RANGE)
            def _():
                kp = (key * jnp.exp(-b)).astype(bf16)
                cmask = _chunk_mask()
                for j in range(HG_T // HG_SUB):
                    rs = slice(HG_SUB * j, HG_SUB * (j + 1))
                    sc = jnp.where(cmask, _dot_nt(qt[rs].astype(bf16), kp[rs]), 0.0).astype(bf16)
                    oi_s[rs, sl] = _dot(sc, ib_ref[rs, sl].astype(bf16))

            @pl.when(rng >= SAFE_RANGE)
            def _():
                v = ib_ref[:, sl]
                ones = jnp.ones((HGRN_HD, HGRN_HD), bf16)
                o = jnp.zeros((HG_T, HGRN_HD), f32)
                for l in range(CHUNK):
                    if l == 0:
                        pr, vs = qf * key, v
                    else:
                        e = jnp.exp(jnp.where(rc >= l, b - pltpu.roll(b, l, 0), NEG))
                        pr, vs = qf * pltpu.roll(key, l, 0) * e, pltpu.roll(v, l, 0)
                    o = o + _dot(pr.astype(bf16), ones) * vs
                oi_s[:, sl] = o

        def step(c, carry):
            rows = pl.ds(pl.multiple_of(c * CHUNK, CHUNK), CHUNK)
            row0 = pl.ds(pl.multiple_of(c * CHUNK, CHUNK), 1)
            for h in range(N_HH):
                sl = slice(HGRN_HD * h, HGRN_HD * (h + 1))
                stv = ST[h]
                st_ref[c, sl, :] = stv
                oi_s[rows, sl] += _dot_nt(qt_s[rows, sl], stv.astype(bf16))
                ST[h] = stv * dec_s[row0, sl] + _dot_tn(ib_ref[rows, sl].astype(bf16), kh_s[rows, sl])
            return carry

        lax.fori_loop(0, HG_T // CHUNK, step, 0, unroll=8)

        for h in range(N_HH):
            sl = slice(HGRN_HD * h, HGRN_HD * (h + 1))
            o = oi_s[:, sl]
            o_ref[:, sl] = o
            on = o * lax.rsqrt(jnp.mean(o * o, axis=-1, keepdims=True) + EPS)
            g = gb_ref[:, sl]
            yb_ref[:, sl] = on * wn_ref[:, sl] * (g * _sigmoid(g))

    col = lambda c: pl.BlockSpec((HG_T, HGRN_W), lambda i: (i, c))
    tile = pl.BlockSpec((HG_T, HGRN_W), lambda i: (i, 0))
    whole = lambda a: pl.BlockSpec(a.shape, lambda i: (0, 0))
    return pl.pallas_call(
        body, name="hgrn_fwd", grid=(nT,),
        in_specs=[col(0), col(1), col(2), col(3), whole(lbl), whole(wn)],
        out_specs=[tile, tile, pl.BlockSpec((HG_T // CHUNK, HGRN_W, HGRN_HD), lambda i: (i, 0, 0))],
        out_shape=[jax.ShapeDtypeStruct((S, HGRN_W), f32), jax.ShapeDtypeStruct((S, HGRN_W), f32),
                   jax.ShapeDtypeStruct((S // CHUNK, HGRN_W, HGRN_HD), f32)],
        scratch_shapes=[pltpu.VMEM((N_HH, HGRN_HD, HGRN_HD), f32), pltpu.VMEM((HG_T, HGRN_W), bf16),
                        pltpu.VMEM((HG_T, HGRN_W), bf16), pltpu.VMEM((HG_T, HGRN_W), f32), pltpu.VMEM((HG_T, HGRN_W), f32)],
        compiler_params=_cp("arbitrary"),
    )(hp, hp, hp, hp, lbl, wn)


def _hgrn_bwd(hp, lbl, wn, o_sav, states, dmix, rider=None):
    S = hp.shape[0]
    nT = S // HG_T

    def body(qb_ref, fb_ref, ib_ref, gb_ref, lbl_ref, wn_ref, o_ref, st_ref, dy_ref,
             dq_ref, df_ref, di_ref, dg_ref, gwn_ref, glb_ref,
             DST, qt_s, kh_s, dec_s, do_s, dqt_s, dkh_s, dbl_s, dvi_s, dqi_s, dki_s, dbi_s):
        @pl.when(pl.program_id(0) == 0)
        def _():
            DST[...] = jnp.zeros_like(DST)
            gwn_ref[...] = jnp.zeros_like(gwn_ref)
            glb_ref[...] = jnp.zeros_like(glb_ref)

        rc = _row_in_chunk()
        preps = []
        for h in range(N_HH):
            sl = slice(HGRN_HD * h, HGRN_HD * (h + 1))
            p = _hgrn_prep(qb_ref[:, sl], fb_ref[:, sl], lbl_ref[:, sl], rc)
            preps.append(p)
            qf, key, b = p["qf"], p["key"], p["b"]
            v = ib_ref[:, sl]
            o = o_ref[:, sl]
            rinv = lax.rsqrt(jnp.mean(o * o, axis=-1, keepdims=True) + EPS)
            on = o * rinv
            g = gb_ref[:, sl]
            sgm = _sigmoid(g)
            silu_g = g * sgm
            dy = dy_ref[:, sl]
            wn_v = wn_ref[:, sl]
            gwn_ref[:, sl] += jnp.sum(dy * on * silu_g, axis=0, keepdims=True)
            dg_ref[:, sl] = (dy * on * wn_v * (sgm * (1.0 + g * (1.0 - sgm)))).astype(bf16)
            t1 = dy * wn_v * silu_g
            do = rinv * (t1 - on * jnp.mean(t1 * on, axis=-1, keepdims=True))
            do_s[:, sl] = do.astype(bf16)
            qt = qf * p["eb"]
            qt_s[:, sl] = qt.astype(bf16)
            kh_s[:, sl] = (key * p["er"]).astype(bf16)
            dec_s[:, sl] = jnp.exp(b + p["rem"])
            rng = jnp.max(-(b + p["rem"]))

            @pl.when(rng < SAFE_RANGE)
            def _():
                einv = jnp.exp(-b)
                kp = (key * einv).astype(bf16)
                cmask = _chunk_mask()
                for j in range(HG_T // HG_SUB):
                    rs = slice(HG_SUB * j, HG_SUB * (j + 1))
                    qtb, dob, vb = qt[rs].astype(bf16), do[rs].astype(bf16), v[rs].astype(bf16)
                    sc = jnp.where(cmask, _dot_nt(qtb, kp[rs]), 0.0).astype(bf16)
                    dsc = jnp.where(cmask, _dot_nt(dob, vb), 0.0).astype(bf16)
                    dqp = _dot(dsc, kp[rs])
                    dkp = _dot_tn(dsc, qtb)
                    dvi_s[rs, sl] = _dot_tn(sc, dob)
                    dqi_s[rs, sl] = dqp * p["eb"][rs]
                    dki_s[rs, sl] = dkp * einv[rs]
                    dbi_s[rs, sl] = dqp * qtb.astype(f32) - dkp * kp[rs].astype(f32)

            @pl.when(rng >= SAFE_RANGE)
            def _():
                ones = jnp.ones((HGRN_HD, HGRN_HD), bf16)
                dqf = jnp.zeros((HG_T, HGRN_HD), f32)
                dkey, db, dv = dqf, dqf, dqf
                for l in range(CHUNK):
                    if l == 0:
                        ks, vs, qe = key, v, qf
                    else:
                        e = jnp.exp(jnp.where(rc >= l, b - pltpu.roll(b, l, 0), NEG))
                        ks, vs, qe = pltpu.roll(key, l, 0), pltpu.roll(v, l, 0), qf * e
                    pr = qe * ks
                    rl = _dot(pr.astype(bf16), ones)
                    drl = _dot((do * vs).astype(bf16), ones)
                    if l == 0:
                        dqf = dqf + drl * ks
                        dv = dv + rl * do
                        dkey = dkey + drl * qe
                    else:
                        drl = jnp.where(rc >= l, drl, 0.0)
                        gl = drl * pr
                        dqf = dqf + drl * ks * e
                        dv = dv + pltpu.roll(rl * do, HG_T - l, 0)
                        dkey = dkey + pltpu.roll(drl * qe, HG_T - l, 0)
                        db = db + gl - pltpu.roll(gl, HG_T - l, 0)
                dvi_s[:, sl] = dv
                dqi_s[:, sl] = dqf
                dki_s[:, sl] = dkey
                dbi_s[:, sl] = db

        def step(k, carry):
            c = HG_T // CHUNK - 1 - k
            rows = pl.ds(pl.multiple_of(c * CHUNK, CHUNK), CHUNK)
            row0 = pl.ds(pl.multiple_of(c * CHUNK, CHUNK), 1)
            for h in range(N_HH):
                sl = slice(HGRN_HD * h, HGRN_HD * (h + 1))
                stp = st_ref[c, sl, :]
                dst = DST[h]
                dstb = dst.astype(bf16)
                dob = do_s[rows, sl]
                khb = kh_s[rows, sl]
                dec = dec_s[row0, sl]
                dqt_s[rows, sl] = _dot(dob, stp.astype(bf16))
                dkh = _dot(ib_ref[rows, sl].astype(bf16), dstb)
                dkh_s[rows, sl] = dkh
                dvi_s[rows, sl] += _dot_nt(khb, dstb)
                dbl = jnp.sum(dst * stp, axis=0, keepdims=True) * dec + jnp.sum(dkh * khb.astype(f32), axis=0, keepdims=True)
                dbl_s[rows, sl] = jnp.broadcast_to(dbl, (CHUNK, HGRN_HD))
                DST[h] = dst * dec + _dot_tn(dob, qt_s[rows, sl])
            return carry

        lax.fori_loop(0, HG_T // CHUNK, step, 0, unroll=4)

        for h in range(N_HH):
            sl = slice(HGRN_HD * h, HGRN_HD * (h + 1))
            qb = qb_ref[:, sl]
            p = preps[h]
            sf, sq, lb = p["sf"], p["sq"], p["lb"]
            dqt, dkh = dqt_s[:, sl], dkh_s[:, sl]
            dqf = dqt * p["eb"] + dqi_s[:, sl]
            dkey = dkh * p["er"] + dki_s[:, sl]
            db = dqt * (p["qf"] * p["eb"]) - dkh * (p["key"] * p["er"]) + jnp.where(rc == CHUNK - 1, dbl_s[:, sl], 0.0) + dbi_s[:, sl]
            df = _chunk_rcumsum(db, rc) / p["f"] - dkey
            df_ref[:, sl] = (df * (1.0 - lb) * sf * (1.0 - sf)).astype(bf16)
            glb_ref[:, sl] += jnp.sum(df * (1.0 - sf), axis=0, keepdims=True)
            dq_ref[:, sl] = (dqf * (sq * (1.0 + qb * (1.0 - sq)))).astype(bf16)
            di_ref[:, sl] = dvi_s[:, sl].astype(bf16)

    rev = lambda i: nT - 1 - i
    col = lambda c: pl.BlockSpec((HG_T, HGRN_W), lambda i: (rev(i), c))
    tile = pl.BlockSpec((HG_T, HGRN_W), lambda i: (rev(i), 0))
    whole = lambda a: pl.BlockSpec(a.shape, lambda i: (0, 0))
    vec = pl.BlockSpec((1, HGRN_W), lambda i: (0, 0))
    tb = lambda: pltpu.VMEM((HG_T, HGRN_W), bf16)
    tf = lambda: pltpu.VMEM((HG_T, HGRN_W), f32)
    call = dict(in_specs=[col(0), col(1), col(2), col(3), whole(lbl), whole(wn), tile,
                          pl.BlockSpec((HG_T // CHUNK, HGRN_W, HGRN_HD), lambda i: (rev(i), 0, 0)),
                          pl.BlockSpec((HG_T, HGRN_W), lambda i: (rev(i), 1))],
                out_specs=[tile, tile, tile, tile, vec, vec],
                out_shape=[jax.ShapeDtypeStruct((S, HGRN_W), bf16)] * 4 + [jax.ShapeDtypeStruct((1, HGRN_W), f32)] * 2,
                scratch_shapes=[pltpu.VMEM((N_HH, HGRN_HD, HGRN_HD), f32), tb(), tb(), tf(), tb(), tf(), tf(), tf(), tf(), tf(),
                                tf(), tf()])
    call, body, more = _ride(call, rider, body, lambda: pl.program_id(0), nT, 9, 6, 12)
    return pl.pallas_call(body, name="hgrn_bwd", grid=(nT,), compiler_params=_cp("arbitrary"), **call)(
        hp, hp, hp, hp, lbl, wn, o_sav, states, dmix, *more)


def _out_proj(x, ya, yb, wout, w2):
    S = x.shape[0]
    tm = 512

    def body(x_ref, ya_ref, yb_ref, w_ref, w2_ref, h1_ref, u2_ref, mix_ref):
        mixed = jnp.concatenate([ya_ref[...], yb_ref[...]], axis=1).astype(bf16)
        mix_ref[...] = mixed
        h1 = x_ref[...] + _dot(mixed, w_ref[...])
        h1_ref[...] = h1
        r = lax.rsqrt(jnp.mean(h1 * h1, axis=-1, keepdims=True) + EPS)
        u2_ref[...] = (h1 * r * w2_ref[...]).astype(bf16)

    row = lambda w: pl.BlockSpec((tm, w), lambda i: (i, 0))
    return pl.pallas_call(
        body, name="out_proj", grid=(S // tm,),
        in_specs=[row(D_MODEL), row(ATTN_W), row(HGRN_W), pl.BlockSpec((D_MODEL, D_MODEL), lambda i: (0, 0)),
                  pl.BlockSpec((1, D_MODEL), lambda i: (0, 0))],
        out_specs=[row(D_MODEL), row(D_MODEL), row(D_MODEL)],
        out_shape=[jax.ShapeDtypeStruct((S, D_MODEL), f32), jax.ShapeDtypeStruct((S, D_MODEL), bf16),
                   jax.ShapeDtypeStruct((S, D_MODEL), bf16)],
        compiler_params=_cp("arbitrary"),
    )(x, ya, yb, wout, w2)


def _gate_up(u2, wgu):
    S = u2.shape[0]
    tm, tn = 512, 1408
    nj = FFN // tn

    def body(u_ref, wg_ref, wu_ref, g_ref, up_ref, a_ref):
        u = u_ref[...]
        g = _dot(u, wg_ref[...])
        up = _dot(u, wu_ref[...])
        g_ref[...] = g.astype(bf16)
        up_ref[...] = up.astype(bf16)
        a_ref[...] = (g * _sigmoid(g) * up).astype(bf16)

    out = pl.BlockSpec((tm, tn), lambda j, i: (i, j))
    return pl.pallas_call(
        body, name="gate_up", grid=(nj, S // tm),
        in_specs=[pl.BlockSpec((tm, D_MODEL), lambda j, i: (i, 0)), pl.BlockSpec((D_MODEL, tn), lambda j, i: (0, j)),
                  pl.BlockSpec((D_MODEL, tn), lambda j, i: (0, j + nj))],
        out_specs=[out, out, out],
        out_shape=[jax.ShapeDtypeStruct((S, FFN), bf16)] * 3,
        compiler_params=_cp("arbitrary", "arbitrary"),
    )(u2, wgu, wgu)


def _rms_bwd(dyw, hn, r):
    return r * (dyw - hn * jnp.mean(dyw * hn, axis=-1, keepdims=True))


def _down_loss(act, wdown, h1, tgt, w3):
    S = act.shape[0]
    tm = 256

    def body(a_ref, w_ref, h1_ref, t_ref, w3_ref, dh2_ref, loss_ref, gw3_ref):
        @pl.when(pl.program_id(0) == 0)
        def _():
            loss_ref[...] = jnp.zeros_like(loss_ref)
            gw3_ref[...] = jnp.zeros_like(gw3_ref)

        h2 = h1_ref[...] + _dot(a_ref[...], w_ref[...])
        r = lax.rsqrt(jnp.mean(h2 * h2, axis=-1, keepdims=True) + EPS)
        hn = h2 * r
        w3 = w3_ref[...]
        err = hn * w3 - t_ref[...]
        loss_ref[...] += (0.5 / D_MODEL) * jnp.sum(err * err)
        dy = err * (1.0 / D_MODEL)
        gw3_ref[...] += jnp.sum(dy * hn, axis=0, keepdims=True)
        dh2_ref[...] = _rms_bwd(dy * w3, hn, r)

    row = lambda w: pl.BlockSpec((tm, w), lambda i: (i, 0))
    return pl.pallas_call(
        body, name="down_loss", grid=(S // tm,),
        in_specs=[row(FFN), pl.BlockSpec((FFN, D_MODEL), lambda i: (0, 0)), row(D_MODEL), row(D_MODEL),
                  pl.BlockSpec((1, D_MODEL), lambda i: (0, 0))],
        out_specs=[row(D_MODEL), pl.BlockSpec((1, 128), lambda i: (0, 0)), pl.BlockSpec((1, D_MODEL), lambda i: (0, 0))],
        out_shape=[jax.ShapeDtypeStruct((S, D_MODEL), f32), jax.ShapeDtypeStruct((1, 128), f32),
                   jax.ShapeDtypeStruct((1, D_MODEL), f32)],
        compiler_params=_cp("arbitrary"),
    )(act, wdown, h1, tgt, w3)


def _dact(dh2, wdown, gate, up):
    S = dh2.shape[0]
    tm = 256

    def body(d_ref, w_ref, g_ref, u_ref, dg_ref, du_ref):
        da = _dot_nt(d_ref[...].astype(bf16), w_ref[...])
        g = g_ref[...].astype(f32)
        sg = _sigmoid(g)
        du_ref[...] = (da * g * sg).astype(bf16)
        dg_ref[...] = (da * u_ref[...].astype(f32) * (sg * (1.0 + g * (1.0 - sg)))).astype(bf16)

    row = lambda w: pl.BlockSpec((tm, w), lambda i: (i, 0))
    return pl.pallas_call(
        body, name="dact", grid=(S // tm,),
        in_specs=[row(D_MODEL), pl.BlockSpec((FFN, D_MODEL), lambda i: (0, 0)), row(FFN), row(FFN)],
        out_specs=[row(FFN), row(FFN)],
        out_shape=[jax.ShapeDtypeStruct((S, FFN), bf16)] * 2,
        compiler_params=_cp("arbitrary"),
    )(dh2, wdown, gate, up)


def _dgu(dgate, dup, wgu, h1, w2, dh2, wout, rider=None):
    S = dgate.shape[0]
    tm = 256

    def body(dg_ref, du_ref, wg_ref, wu_ref, h1_ref, w2_ref, dh2_ref, wo_ref, dh1_ref, gw2_ref, dmix_ref):
        @pl.when(pl.program_id(0) == 0)
        def _():
            gw2_ref[...] = jnp.zeros_like(gw2_ref)

        du2 = _dot_nt(dg_ref[...], wg_ref[...]) + _dot_nt(du_ref[...], wu_ref[...])
        h1 = h1_ref[...]
        r = lax.rsqrt(jnp.mean(h1 * h1, axis=-1, keepdims=True) + EPS)
        hn = h1 * r
        gw2_ref[...] += jnp.sum(du2 * hn, axis=0, keepdims=True)
        dh1 = dh2_ref[...] + _rms_bwd(du2 * w2_ref[...], hn, r)
        dh1_ref[...] = dh1
        dmix_ref[...] = _dot_nt(dh1.astype(bf16), wo_ref[...])

    row = lambda w: pl.BlockSpec((tm, w), lambda i: (i, 0))
    call = dict(in_specs=[row(FFN), row(FFN), pl.BlockSpec((D_MODEL, FFN), lambda i: (0, 0)),
                          pl.BlockSpec((D_MODEL, FFN), lambda i: (0, 1)), row(D_MODEL),
                          pl.BlockSpec((1, D_MODEL), lambda i: (0, 0)), row(D_MODEL),
                          pl.BlockSpec((D_MODEL, D_MODEL), lambda i: (0, 0))],
                out_specs=[row(D_MODEL), pl.BlockSpec((1, D_MODEL), lambda i: (0, 0)), row(D_MODEL)],
                out_shape=[jax.ShapeDtypeStruct((S, D_MODEL), f32), jax.ShapeDtypeStruct((1, D_MODEL), f32),
                           jax.ShapeDtypeStruct((S, D_MODEL), f32)], scratch_shapes=[])
    call, body, more = _ride(call, rider, body, lambda: pl.program_id(0), S // tm, 8, 3, 0)
    return pl.pallas_call(body, name="dgu", grid=(S // tm,), compiler_params=_cp("arbitrary"), **call)(
        dgate, dup, wgu, wgu, h1, w2, dh2, wout, *more)


def _din(dq, dk, dv, dhq, dhf, dhi, dhg, cos_t, sg_t, win, x, w1, dh1):
    S = x.shape[0]
    tm = 256

    def body(dq_ref, dk_ref, dv_ref, dhq_ref, dhf_ref, dhi_ref, dhg_ref, cos_ref, sg_ref, w_ref, x_ref, w1_ref, dh1_ref,
             dp_ref, gx_ref, gw1_ref):
        @pl.when(pl.program_id(0) == 0)
        def _():
            gw1_ref[...] = jnp.zeros_like(gw1_ref)

        cosv, sgv = jnp.tile(cos_ref[...], (1, ATTN_W // 128)), jnp.tile(sg_ref[...], (1, ATTN_W // 128))
        unrope = lambda d: d * cosv - sgv * _swap_halves(d)
        parts = [(unrope(dq_ref[...]) * (HEAD_DIM ** -0.5)).astype(bf16), unrope(dk_ref[...]).astype(bf16),
                 dv_ref[...].astype(bf16), dhq_ref[...], dhf_ref[...], dhi_ref[...], dhg_ref[...]]
        for j, pj in enumerate(parts):
            dp_ref[:, j * 512:(j + 1) * 512] = pj
        du = _dot_nt(dp_ref[...], w_ref[...])
        xv = x_ref[...]
        r = lax.rsqrt(jnp.mean(xv * xv, axis=-1, keepdims=True) + EPS)
        xn = xv * r
        gw1_ref[...] += jnp.sum(du * xn, axis=0, keepdims=True)
        gx_ref[...] = dh1_ref[...] + _rms_bwd(du * w1_ref[...], xn, r)

    row = lambda w: pl.BlockSpec((tm, w), lambda i: (i, 0))
    vec = pl.BlockSpec((1, D_MODEL), lambda i: (0, 0))
    return pl.pallas_call(
        body, name="din", grid=(S // tm,),
        in_specs=[row(512)] * 7 + [row(128), row(128), pl.BlockSpec((D_MODEL, IN_W), lambda i: (0, 0)), row(D_MODEL), vec,
                                   row(D_MODEL)],
        out_specs=[row(IN_W), row(D_MODEL), vec],
        out_shape=[jax.ShapeDtypeStruct((S, IN_W), bf16), jax.ShapeDtypeStruct((S, D_MODEL), f32),
                   jax.ShapeDtypeStruct((1, D_MODEL), f32)],
        compiler_params=_cp("arbitrary"),
    )(dq, dk, dv, dhq, dhf, dhi, dhg, cos_t, sg_t, win, x, w1, dh1)


def _gw(a, bs, tn, name, ts=2048):
    S, M = a.shape
    N = bs[0].shape[1]
    k = len(bs)

    def body(a_ref, *refs):
        @pl.when(pl.program_id(1) == 0)
        def _():
            for o_ref in refs[k:]:
                o_ref[...] = jnp.zeros_like(o_ref)

        at = a_ref[...].astype(bf16)
        for b_ref, o_ref in zip(refs[:k], refs[k:]):
            o_ref[...] += _dot_tn(at, b_ref[...].astype(bf16))

    return pl.pallas_call(
        body, name=name, grid=(N // tn, S // ts),
        in_specs=[pl.BlockSpec((ts, M), lambda j, s: (s, 0))] + [pl.BlockSpec((ts, tn), lambda j, s: (s, j))] * k,
        out_specs=[pl.BlockSpec((M, tn), lambda j, s: (0, j))] * k, out_shape=[jax.ShapeDtypeStruct((M, N), f32)] * k,
        compiler_params=_cp("arbitrary", "arbitrary"),
    )(a, *bs)


MESH = pl.DeviceIdType.MESH
ANY = pl.BlockSpec(memory_space=pl.ANY)
VMEM_SPEC = pl.BlockSpec(memory_space=pltpu.VMEM)


def _pos():
    return lax.axis_index("x"), lax.axis_index("y"), lax.axis_index("c")


def _flip(v, bit):
    return 1 - v if bit else v


def _gather_rider(shards):
    n = len(shards)

    def parts(outs, scratch):
        send_sems, recv_sems, local_sems = scratch[n:]
        x, y, c = _pos()
        chips = [(1 - x, y), (x, 1 - y), (1 - x, 1 - y)]

        def copy(a, k, block, to, src=None):
            dst = outs[a].at[4 * block[0] + 2 * block[1] + block[2]]
            return pltpu.make_async_remote_copy(src_ref=dst if src is None else src, dst_ref=dst, send_sem=send_sems.at[a, k],
                                                recv_sem=recv_sems.at[a, k], device_id=to, device_id_type=MESH)

        bufs = scratch[:n]
        me, sibling = (x, y, c), (x, y, 1 - c)
        own = lambda a: pltpu.make_async_copy(bufs[a], outs[a].at[4 * x + 2 * y + c], local_sems.at[a])
        sent = lambda a: [copy(a, 0, me, sibling, src=bufs[a])] + [copy(a, 1 + j, me, (*chip, c), src=bufs[a])
                                                                   for j, chip in enumerate(chips)]
        passed = lambda a: [copy(a, 4 + j, (*chip, c), sibling) for j, chip in enumerate(chips)]
        landed = lambda a: [copy(a, 1 + j, (*chip, c), me) for j, chip in enumerate(chips)]
        from_sibling = lambda a: [copy(a, 0, sibling, me)] + [copy(a, 4 + j, (*chip, 1 - c), me) for j, chip in enumerate(chips)]
        return bufs, local_sems, own, sent, passed, landed, from_sibling

    def first(ins, outs, scratch):
        bufs, local_sems, own, sent, _, _, _ = parts(outs, scratch)
        loads = [pltpu.make_async_copy(ins[a], bufs[a], local_sems.at[a]) for a in range(n)]
        for ld in loads:
            ld.start()
        for a in range(n):
            loads[a].wait()
            own(a).start()
            for cp in sent(a):
                cp.start()

    def middle(ins, outs, scratch):
        _, _, _, _, passed, landed, _ = parts(outs, scratch)
        for a in range(n):
            for got, on in zip(landed(a), passed(a)):
                got.wait_recv()
                on.start()

    def last(ins, outs, scratch):
        _, _, own, sent, passed, _, from_sibling = parts(outs, scratch)
        for a in range(n):
            for cp in from_sibling(a):
                cp.wait_recv()
        for a in range(n):
            for cp in sent(a) + passed(a):
                cp.wait_send()
            own(a).wait()

    return _Rider(shards, [jax.ShapeDtypeStruct((N_DEV,) + s.shape, s.dtype) for s in shards],
                  [pltpu.VMEM(s.shape, s.dtype) for s in shards]
                  + [pltpu.SemaphoreType.DMA((n, 7)), pltpu.SemaphoreType.DMA((n, 7)), pltpu.SemaphoreType.DMA((n,))],
                  first, last, middle)


def _sibling_rider(grads):
    n = len(grads)

    def copies(g, got, scratch):
        send_sems, recv_sems = scratch
        x, y, c = _pos()
        return [pltpu.make_async_remote_copy(src_ref=g[a].at[2 * q + (1 - c)], dst_ref=got[a].at[q], send_sem=send_sems.at[a, q],
                                             recv_sem=recv_sems.at[a, q], device_id=(x, y, 1 - c), device_id_type=MESH)
                for a in range(n) for q in range(4)]

    def first(g, got, scratch):
        for cp in copies(g, got, scratch):
            cp.start()

    def last(g, got, scratch):
        for cp in copies(g, got, scratch):
            cp.wait()

    return _Rider(grads, [jax.ShapeDtypeStruct((4,) + g.shape[1:], g.dtype) for g in grads],
                  [pltpu.SemaphoreType.DMA((n, 4))] * 2, first, last)


def _chips_rider(sums):
    n = len(sums)

    def copies(s, out, scratch):
        send_sems, recv_sems = scratch
        x, y, c = _pos()
        cps = []
        for a in range(n):
            for f in (1, 2, 3):
                peer = (_flip(x, f >> 1), _flip(y, f & 1), c)
                cps.append(pltpu.make_async_remote_copy(
                    src_ref=s[a].at[2 * peer[0] + peer[1]], dst_ref=out[a].at[f - 1], send_sem=send_sems.at[a, f - 1],
                    recv_sem=recv_sems.at[a, f - 1], device_id=peer, device_id_type=MESH))
        return cps

    def first(s, out, scratch):
        for cp in copies(s, out, scratch):
            cp.start()

    def last(s, out, scratch):
        for cp in copies(s, out, scratch):
            cp.wait()

    return _Rider(sums, [jax.ShapeDtypeStruct((3,) + s.shape[1:], s.dtype) for s in sums],
                  [pltpu.SemaphoreType.DMA((n, 3))] * 2, first, last)


def _both(a, b):
    na = (len(a.ins), len(a.out_shapes), len(a.scratch))

    def split(fa, fb):
        def f(ins, outs, scratch):
            fa(ins[:na[0]], outs[:na[1]], scratch[:na[2]])
            fb(ins[na[0]:], outs[na[1]:], scratch[na[2]:])
        return f

    return _Rider(a.ins + b.ins, a.out_shapes + b.out_shapes, a.scratch + b.scratch, split(a.first, b.first), split(a.last, b.last))


def _alone(rider, name):
    ri, ro = len(rider.ins), len(rider.out_shapes)

    def body(*refs):
        theirs = (refs[:ri], refs[ri:ri + ro], refs[ri + ro:])
        rider.first(*theirs)
        if rider.middle is not None:
            rider.middle(*theirs)
        rider.last(*theirs)

    return pl.pallas_call(body, name=name, in_specs=[ANY] * ri, out_specs=[ANY] * ro, out_shape=rider.out_shapes,
                          scratch_shapes=rider.scratch)(*rider.ins)


def _gather_small(g_w1, g_w2, g_w3, g_lb, g_wn, loss):
    def body(w1_ref, w2_ref, w3_ref, lb_ref, wn_ref, loss_ref, out_ref, pk, send_sems, recv_sems):
        x, y, c = _pos()
        me = 4 * x + 2 * y + c
        pk[...] = jnp.zeros_like(pk)
        pk[0:1, :] = w1_ref[...]
        pk[1:2, :] = w2_ref[...]
        pk[2:3, :] = w3_ref[...]
        pk[3:4, 0:HGRN_W] = lb_ref[...]
        pk[3:4, HGRN_W:2 * HGRN_W] = wn_ref[...]
        pk[4:5, 0:128] = loss_ref[...]
        out_ref[me] = pk[...]
        sends, recvs = [], []
        for k in range(1, N_DEV):
            peer = (_flip(x, k >> 2), _flip(y, (k >> 1) & 1), _flip(c, k & 1))
            cp = pltpu.make_async_remote_copy(src_ref=pk, dst_ref=out_ref.at[me], send_sem=send_sems.at[k - 1],
                                              recv_sem=recv_sems.at[k - 1], device_id=peer, device_id_type=MESH)
            cp.start()
            sends.append(cp)
            recvs.append(pltpu.make_async_remote_copy(src_ref=pk, dst_ref=out_ref.at[4 * peer[0] + 2 * peer[1] + peer[2]],
                                                      send_sem=send_sems.at[k - 1], recv_sem=recv_sems.at[k - 1], device_id=peer,
                                                      device_id_type=MESH))
        for cp in recvs:
            cp.wait_recv()
        for cp in sends:
            cp.wait_send()

    return pl.pallas_call(
        body, name="gather_small", in_specs=[VMEM_SPEC] * 6, out_specs=VMEM_SPEC,
        out_shape=jax.ShapeDtypeStruct((N_DEV, 8, D_MODEL), f32),
        scratch_shapes=[pltpu.VMEM((8, D_MODEL), f32), pltpu.SemaphoreType.DMA((N_DEV - 1,)), pltpu.SemaphoreType.DMA((N_DEV - 1,))],
    )(g_w1, g_w2, g_w3, g_lb, g_wn, loss)


def _row_tile(r):
    return max(t for t in range(8, 257, 8) if r % t == 0)


def _add_sibling(core, g, got, name):
    _, r, c = got.shape
    tr = _row_tile(r)

    def body(core_ref, a_ref, b_ref, o_ref):
        o_ref[...] = (a_ref[...] + b_ref[...]).astype(bf16)

    blk = pl.BlockSpec((1, tr, c), lambda q, i, core_ref: (q, i, 0))
    return pl.pallas_call(
        body, name=name, out_shape=jax.ShapeDtypeStruct(got.shape, bf16),
        grid_spec=pltpu.PrefetchScalarGridSpec(
            num_scalar_prefetch=1, grid=(4, r // tr),
            in_specs=[pl.BlockSpec((1, tr, c), lambda q, i, core_ref: (2 * q + core_ref[0], i, 0)), blk], out_specs=blk),
        compiler_params=_cp("arbitrary", "arbitrary"))(core, g, got)


def _adamw(w, g, m, v):
    m = ADAM_B1 * m + (1.0 - ADAM_B1) * g
    v = ADAM_B2 * v + (1.0 - ADAM_B2) * (g * g)
    m_hat = m / (1.0 - ADAM_B1 ** ADAM_STEP)
    v_hat = v / (1.0 - ADAM_B2 ** ADAM_STEP)
    return -ADAM_LR * (m_hat / (jnp.sqrt(v_hat) + ADAM_EPS) + ADAM_WD * w), m, v


def _adam_shard(where, g, got, pieces, w, m, v, name):
    r, c = w.shape
    tr = _row_tile(r)

    def body(where_ref, g_ref, got_ref, p_ref, w_ref, m_ref, v_ref, g_out, d_out, m_out, v_out):
        gsum = g_ref[0] + got_ref[0]
        for f in range(3):
            gsum = gsum + p_ref[f].astype(f32)
        g_out[...] = gsum
        d_out[...], m_out[...], v_out[...] = _adamw(w_ref[...], gsum, m_ref[...], v_ref[...])

    blk = pl.BlockSpec((tr, c), lambda i, where_ref: (i, 0))
    return pl.pallas_call(
        body, name=name, out_shape=[jax.ShapeDtypeStruct((r, c), f32)] * 4,
        grid_spec=pltpu.PrefetchScalarGridSpec(
            num_scalar_prefetch=1, grid=(r // tr,),
            in_specs=[pl.BlockSpec((1, tr, c), lambda i, where_ref: (where_ref[0], i, 0)),
                      pl.BlockSpec((1, tr, c), lambda i, where_ref: (where_ref[1], i, 0)),
                      pl.BlockSpec((3, tr, c), lambda i, where_ref: (0, i, 0)), blk, blk, blk],
            out_specs=[blk] * 4),
        compiler_params=_cp("arbitrary"),
    )(where, g, got, pieces, w, m, v)


def _small_update(gath, params):
    def body(gath_ref, *refs):
        ins, outs = refs[:15], refs[15:]
        gs = gath_ref[0]
        for k in range(1, N_DEV):
            gs = gs + gath_ref[k]
        outs[0][...] = gs[4:5, 0:128]
        l0, l1 = ins[9][0:1, :], ins[9][1:2, :]
        lb = _sigmoid(l0 - l1)
        d0 = gs[3:4, 0:HGRN_W] * lb * (1.0 - lb)
        first_row = lax.broadcasted_iota(jnp.int32, (2, HGRN_W), 0) == 0
        grads = [gs[0:1, :], gs[1:2, :], gs[2:3, :], jnp.where(first_row, d0, -d0), gs[3:4, HGRN_W:2 * HGRN_W]]
        for i, g in enumerate(grads):
            w_ref, m_ref, v_ref = ins[3 * i:3 * i + 3]
            o = outs[1 + 4 * i:5 + 4 * i]
            o[0][...] = g
            o[1][...], o[2][...], o[3][...] = _adamw(w_ref[...], g, m_ref[...], v_ref[...])

    flat = [a for p in params for a in p]
    out_shape = [jax.ShapeDtypeStruct((1, 128), f32)] + [jax.ShapeDtypeStruct(p[0].shape, f32) for p in params for _ in range(4)]
    outs = pl.pallas_call(body, name="small_update", in_specs=[VMEM_SPEC] * 16, out_specs=[VMEM_SPEC] * 21, out_shape=out_shape)(gath, *flat)
    return outs[0], [outs[1 + 4 * i:5 + 4 * i] for i in range(5)]


def kernel(x, norm1_w, w_in, lb_logits, hgrn_norm_w, w_out, norm2_w, w_gate_up, w_down, final_norm_w, loss_target, m_norm1_w, m_w_in, m_lb_logits, m_hgrn_norm_w, m_w_out, m_norm2_w, m_w_gate_up, m_w_down, m_final_norm_w, v_norm1_w, v_w_in, v_lb_logits, v_hgrn_norm_w, v_w_out, v_norm2_w, v_w_gate_up, v_w_down, v_final_norm_w):
    row = lambda a: a.reshape(1, D_MODEL)
    by_owner = lambda g, w: jnp.transpose(g.reshape(g.shape[0], g.shape[1] // w, w), (1, 0, 2))
    ix, iy, ic = lax.axis_index("x"), lax.axis_index("y"), lax.axis_index("c")
    core = jnp.stack([ic]).astype(jnp.int32)
    where = jnp.stack([4 * ix + 2 * iy + ic, 2 * ix + iy]).astype(jnp.int32)
    xs, tgt, w3 = x[0], loss_target[0], row(final_norm_w)
    S = xs.shape[0]

    cos_t, sg_t, win_g = _rope_tables(S, _gather_rider([w_in[0].astype(bf16)]))
    win = jnp.transpose(win_g, (1, 0, 2)).reshape(D_MODEL, IN_W)
    u, qkv, hp = _in_proj(xs, norm1_w, win, cos_t, sg_t)
    ya, lse, wout_g, wgu_g, wdown_g = _attn_fwd(qkv, _gather_rider([w_out[0].astype(bf16), w_gate_up[0].astype(bf16),
                                                                     w_down[0].astype(bf16)]))
    wout = wout_g.reshape(D_MODEL, D_MODEL)
    wgu = jnp.transpose(wgu_g, (1, 0, 2)).reshape(D_MODEL, 2 * FFN)
    wdown = wdown_g.reshape(FFN, D_MODEL)
    yb, o_sav, states = _hgrn_fwd(hp, lb_logits, hgrn_norm_w)
    h1, u2, mixed = _out_proj(xs, ya, yb, wout, norm2_w)
    gate, up, act = _gate_up(u2, wgu)
    dh2, loss_p, g_w3 = _down_loss(act, wdown, h1, tgt, w3)

    (g_wdown,) = _gw(act, [dh2], 512, "gw_down")
    dgate, dup = _dact(dh2, wdown, gate, up)
    g_wgu = _gw(u2, [dgate, dup], 1408, "gw_gate_up", ts=1024)
    early = [jnp.concatenate([by_owner(g, 2 * FFN // N_DEV) for g in g_wgu], axis=0), g_wdown.reshape(N_DEV, FFN // N_DEV, D_MODEL)]
    dh1, g_w2, dmix, *got_early = _dgu(dgate, dup, wgu, h1, norm2_w, dh2, wout, _sibling_rider(early))
    sums_early = [_add_sibling(core, g, o, f"add_sibling_{i}") for i, (g, o) in enumerate(zip(early, got_early))]
    (g_wout,) = _gw(mixed, [dh1], 1024, "gw_out")
    mid = [g_wout.reshape(N_DEV, D_MODEL // N_DEV, D_MODEL)]
    dhq, dhf, dhi, dhg, g_wn, g_lb, *rode = _hgrn_bwd(hp, lb_logits, hgrn_norm_w, o_sav, states, dmix,
                                                      _both(_chips_rider(sums_early), _sibling_rider(mid)))
    pieces_early, got_mid = rode[:2], rode[2:]
    sums_mid = [_add_sibling(core, mid[0], got_mid[0], "add_sibling_2")]
    dq, dk, dv, *pieces_mid = _attn_bwd(qkv, ya, lse, dmix, _chips_rider(sums_mid))
    dproj, gx, g_w1 = _din(dq, dk, dv, dhq, dhf, dhi, dhg, cos_t, sg_t, win, xs, norm1_w, dh1)
    (g_win,) = _gw(u, [dproj], 896, "gw_in")
    late = [by_owner(g_win, IN_W // N_DEV)]
    got_late = _alone(_sibling_rider(late), "reduce_sibling")
    sums_late = [_add_sibling(core, late[0], got_late[0], "add_sibling_3")]
    pieces_late = _alone(_chips_rider(sums_late), "reduce_chips")

    grads = [late[0], mid[0], early[0], early[1]]
    got = [got_late[0], got_mid[0], got_early[0], got_early[1]]
    pieces = [pieces_late[0], pieces_mid[0], pieces_early[0], pieces_early[1]]
    shards = [w_in[0], w_out[0], w_gate_up[0], w_down[0]]
    moms = [(m_w_in[0], v_w_in[0]), (m_w_out[0], v_w_out[0]), (m_w_gate_up[0], v_w_gate_up[0]), (m_w_down[0], v_w_down[0])]
    big = [_adam_shard(where, g, o, p, w, m, v, f"adam_{i}")
           for i, (g, o, p, w, (m, v)) in enumerate(zip(grads, got, pieces, shards, moms))]
    big = [[a[None] for a in four] for four in big]

    gath = _gather_small(g_w1, g_w2, g_w3, g_lb, g_wn, loss_p)
    params = [(norm1_w, m_norm1_w, v_norm1_w), (norm2_w, m_norm2_w, v_norm2_w),
              (row(final_norm_w), row(m_final_norm_w), row(v_final_norm_w)),
              (lb_logits, m_lb_logits, v_lb_logits), (hgrn_norm_w, m_hgrn_norm_w, v_hgrn_norm_w)]
    loss, (s_w1, s_w2, s_w3, s_lb, s_wn) = _small_update(gath, params)
    s_w3 = [a.reshape(D_MODEL) for a in s_w3]
    per_w = [s_w1, big[0], s_lb, s_wn, big[1], s_w2, big[2], big[3], s_w3]
    return (loss[0, 0], gx[None], *[p[0] for p in per_w], *[p[1] for p in per_w], *[p[2] for p in per_w], *[p[3] for p in per_w])
```

```python
import jax
import jax.numpy as jnp
from jax import lax
from jax.experimental import pallas as pl
from jax.experimental.pallas import tpu as pltpu

f32, bf16 = jnp.float32, jnp.bfloat16

D_MODEL = 1024
ATTN_W = 512
HEAD_DIM = 64
ATTN_BLK = 128
DILATIONS = (1, 4, 16)
HGRN_W = 512
HGRN_HD = 128
CHUNK = 64
IN_W = 3 * ATTN_W + 4 * HGRN_W
FFN = 2816
EPS = 1e-6
ROPE_THETA = 10000.0
NEG = -1e30
N_DEV = 8
ADAM_LR, ADAM_B1, ADAM_B2, ADAM_EPS, ADAM_WD, ADAM_STEP = 0.001, 0.9, 0.999, 1e-08, 0.01, 10
VMEM_LIMIT = 56 * 1024 * 1024


def _cp(*sem):
    return pltpu.CompilerParams(dimension_semantics=sem, vmem_limit_bytes=VMEM_LIMIT)


def _dot(a, b):
    return jnp.dot(a, b, preferred_element_type=f32)


def _dot_nt(a, b):
    return lax.dot_general(a, b, (((1,), (1,)), ((), ())), preferred_element_type=f32)


def _dot_tn(a, b):
    return lax.dot_general(a, b, (((0,), (0,)), ((), ())), preferred_element_type=f32)


def _sigmoid(x):
    return 0.5 * jnp.tanh(0.5 * x) + 0.5


class _Rider:
    def __init__(self, ins, out_shapes, scratch, first, last, middle=None):
        self.ins, self.out_shapes, self.scratch = list(ins), list(out_shapes), list(scratch)
        self.first, self.middle, self.last = first, middle, last


def _ride(call, rider, body, step, n_steps, n_in, n_out, n_scratch):
    if rider is None:
        return call, body, []
    ri, ro = len(rider.ins), len(rider.out_shapes)
    any_spec = pl.BlockSpec(memory_space=pl.ANY)
    call = dict(call, in_specs=call["in_specs"] + [any_spec] * ri, out_specs=call["out_specs"] + [any_spec] * ro,
                out_shape=call["out_shape"] + rider.out_shapes, scratch_shapes=call["scratch_shapes"] + rider.scratch)

    def riding(*refs):
        a = n_in + ri
        b = a + n_out + ro
        mine = refs[:n_in] + refs[a:a + n_out] + refs[b:b + n_scratch]
        theirs = (refs[n_in:a], refs[a + n_out:b], refs[b + n_scratch:])
        t = step()

        @pl.when(t == 0)
        def _():
            rider.first(*theirs)

        body(*mine)
        if rider.middle is not None:
            @pl.when(t == n_steps // 2)
            def _():
                rider.middle(*theirs)

        @pl.when(t == n_steps - 1)
        def _():
            rider.last(*theirs)

    return call, riding, rider.ins


def _rope_tables(S, rider=None):
    half = HEAD_DIM // 2
    tm = 256
    inv_freq = jnp.tile(ROPE_THETA ** (-jnp.arange(half, dtype=f32) / half), 128 // half).reshape(1, 128)
    sign = jnp.tile(jnp.concatenate([-jnp.ones((half,), f32), jnp.ones((half,), f32)]), 128 // HEAD_DIM).reshape(1, 128)

    def body(inv_ref, sign_ref, cos_ref, sg_ref):
        pos = (lax.broadcasted_iota(jnp.int32, (tm, 128), 0) + pl.program_id(0) * tm).astype(f32)
        ang = pos * inv_ref[...]
        cos_ref[...] = jnp.cos(ang)
        sg_ref[...] = jnp.sin(ang) * sign_ref[...]

    vec = pl.BlockSpec((1, 128), lambda i: (0, 0))
    out = pl.BlockSpec((tm, 128), lambda i: (i, 0))
    call = dict(in_specs=[vec, vec], out_specs=[out, out], out_shape=[jax.ShapeDtypeStruct((S, 128), f32)] * 2, scratch_shapes=[])
    call, body, more = _ride(call, rider, body, lambda: pl.program_id(0), S // tm, 2, 2, 0)
    return pl.pallas_call(body, name="rope_tables", grid=(S // tm,), compiler_params=_cp("arbitrary"), **call)(inv_freq, sign, *more)


def _swap_halves(v):
    n = v.shape[1]
    lane = lax.broadcasted_iota(jnp.int32, v.shape, 1)
    return jnp.where((lane % HEAD_DIM) < HEAD_DIM // 2, pltpu.roll(v, n - HEAD_DIM // 2, 1), pltpu.roll(v, HEAD_DIM // 2, 1))


def _in_proj(x, w1, win, cos_t, sg_t):
    S = x.shape[0]
    tm = 256

    def body(x_ref, w1_ref, w_ref, cos_ref, sg_ref, u_ref, qkv_ref, hp_ref):
        xv = x_ref[...]
        r = lax.rsqrt(jnp.mean(xv * xv, axis=-1, keepdims=True) + EPS)
        u = (xv * r * w1_ref[...]).astype(bf16)
        u_ref[...] = u
        cosv, sgv = jnp.tile(cos_ref[...], (1, ATTN_W // 128)), jnp.tile(sg_ref[...], (1, ATTN_W // 128))
        for j in range(3):
            pj = _dot(u, w_ref[:, j * ATTN_W:(j + 1) * ATTN_W])
            if j < 2:
                pj = pj * cosv + _swap_halves(pj) * sgv
            if j == 0:
                pj = pj * (HEAD_DIM ** -0.5)
            qkv_ref[:, j * ATTN_W:(j + 1) * ATTN_W] = pj.astype(bf16)
        for j in range(4):
            lo = 3 * ATTN_W + j * HGRN_W
            hp_ref[:, j * HGRN_W:(j + 1) * HGRN_W] = _dot(u, w_ref[:, lo:lo + HGRN_W])

    return pl.pallas_call(
        body, name="in_proj", grid=(S // tm,),
        in_specs=[pl.BlockSpec((tm, D_MODEL), lambda i: (i, 0)), pl.BlockSpec((1, D_MODEL), lambda i: (0, 0)),
                  pl.BlockSpec((D_MODEL, IN_W), lambda i: (0, 0)),
                  pl.BlockSpec((tm, 128), lambda i: (i, 0)), pl.BlockSpec((tm, 128), lambda i: (i, 0))],
        out_specs=[pl.BlockSpec((tm, D_MODEL), lambda i: (i, 0)), pl.BlockSpec((tm, 3 * ATTN_W), lambda i: (i, 0)),
                   pl.BlockSpec((tm, 4 * HGRN_W), lambda i: (i, 0))],
        out_shape=[jax.ShapeDtypeStruct((S, D_MODEL), bf16), jax.ShapeDtypeStruct((S, 3 * ATTN_W), bf16),
                   jax.ShapeDtypeStruct((S, 4 * HGRN_W), f32)],
        compiler_params=_cp("arbitrary"),
    )(x, w1, win, cos_t, sg_t)


def _head_masks():
    lane = lax.broadcasted_iota(jnp.int32, (ATTN_BLK, 128), 1)
    even = lane < HEAD_DIM
    return even, (even, jnp.logical_not(even))


def _pair_fwd(q2, k2, v2, bias):
    even, masks = _head_masks()
    outs, lses = [], []
    for e in range(2):
        qm = jnp.where(masks[e], q2, 0.0).astype(bf16)
        s = _dot_nt(qm, k2) + bias
        m = jnp.max(s, axis=-1, keepdims=True)
        pe = jnp.exp(s - m)
        lsum = jnp.sum(pe, axis=-1, keepdims=True)
        outs.append(_dot(pe.astype(bf16), v2) / lsum)
        lses.append(jnp.broadcast_to(m + jnp.log(lsum), (ATTN_BLK, 128)))
    return jnp.where(even, outs[0], outs[1]), jnp.where(even, lses[0], lses[1])


def _merge(y0, l0, y1, l1):
    mx = jnp.maximum(l0, l1)
    a, b = jnp.exp(l0 - mx), jnp.exp(l1 - mx)
    tot = a + b
    return (a * y0 + b * y1) / tot, mx + jnp.log(tot)


def _pair_bwd(q2, k2f, v2, dy2, lse2, delta2, bias):
    _, masks = _head_masks()
    k2 = k2f.astype(bf16)
    klane = lax.broadcasted_iota(jnp.int32, (2 * ATTN_BLK, 128), 1) < HEAD_DIM
    kmasks = (klane, jnp.logical_not(klane))
    dq2 = jnp.zeros((ATTN_BLK, 128), f32)
    pes, dss, qms, dyms = [], [], [], []
    for e in range(2):
        c0 = e * HEAD_DIM
        qm = jnp.where(masks[e], q2, 0.0).astype(bf16)
        km = jnp.where(kmasks[e], k2f, 0.0).astype(bf16)
        dym = jnp.where(masks[e], dy2, 0.0).astype(bf16)
        pe = jnp.exp(_dot_nt(qm, k2) + bias - lse2[:, c0:c0 + 1])
        ds = (pe * (_dot_nt(dym, v2) - delta2[:, c0:c0 + 1])).astype(bf16)
        dq2 = dq2 + _dot(ds, km)
        pes.append(pe.astype(bf16))
        dss.append(ds)
        qms.append(qm)
        dyms.append(dym)
    dv2 = _dot_tn(jnp.concatenate(pes, axis=0), jnp.concatenate(dyms, axis=0))
    dk2 = _dot_tn(jnp.concatenate(dss, axis=0), jnp.concatenate(qms, axis=0))
    return dq2, dk2, dv2


TOK = 2048


def _key_bias():
    qi = lax.broadcasted_iota(jnp.int32, (ATTN_BLK, 2 * ATTN_BLK), 0)
    kj = lax.broadcasted_iota(jnp.int32, (ATTN_BLK, 2 * ATTN_BLK), 1)
    delta = ATTN_BLK + qi - kj
    seen = (delta >= 0) & (delta <= ATTN_BLK)
    return jnp.where(seen, 0.0, NEG), jnp.where(seen & (kj >= ATTN_BLK), 0.0, NEG)


def _attn_fwd(qkv, rider=None):
    S = qkv.shape[0]
    nS = S // TOK

    def body(q_ref, kp_ref, kc_ref, vp_ref, vc_ref, y_ref, l_ref, qs, k2, v2, ay, al):
        n = pl.program_id(1)
        qs[...] = q_ref[...].astype(f32)
        k2[0:TOK] = kp_ref[...].astype(f32)
        k2[TOK:2 * TOK] = kc_ref[...].astype(f32)
        v2[0:TOK] = vp_ref[...].astype(f32)
        v2[TOK:2 * TOK] = vc_ref[...].astype(f32)
        bias_any, bias_first = _key_bias()

        def block(dil, r, b, step, last):
            start = r + pl.multiple_of(step * b, step)
            rows = pl.ds(start, ATTN_BLK, stride=dil) if dil > 1 else pl.ds(start, ATTN_BLK)
            keys = (pl.ds(TOK + start - step, 2 * ATTN_BLK, stride=dil) if dil > 1
                    else pl.ds(TOK + start - step, 2 * ATTN_BLK))
            bias = jnp.where((n == 0) & (b == 0), bias_first, bias_any)
            out, lse = _pair_fwd(qs[rows, :], k2[keys, :].astype(bf16), v2[keys, :].astype(bf16), bias)
            if dil < DILATIONS[-1]:
                out, lse = _merge(ay[rows, :], al[rows, :], out, lse)
            if last:
                y_ref[rows, :] = out
                l_ref[rows, :] = lse
            else:
                ay[rows, :] = out
                al[rows, :] = lse

        for dil in reversed(DILATIONS):
            def loop(i, carry, dil=dil):
                block(dil, i % dil, i // dil, ATTN_BLK * dil, dil == 1)
                return carry
            lax.fori_loop(0, TOK // ATTN_BLK, loop, 0, unroll=4)

    blk = (TOK, 128)
    cur = lambda c: pl.BlockSpec(blk, lambda p, n: (n, 4 * c + p))
    prv = lambda c: pl.BlockSpec(blk, lambda p, n: (jnp.maximum(n - 1, 0), 4 * c + p))
    out = pl.BlockSpec(blk, lambda p, n: (n, p))
    call = dict(in_specs=[cur(0), prv(1), cur(1), prv(2), cur(2)], out_specs=[out, out],
                out_shape=[jax.ShapeDtypeStruct((S, ATTN_W), f32)] * 2,
                scratch_shapes=[pltpu.VMEM(blk, f32), pltpu.VMEM((2 * TOK, 128), f32), pltpu.VMEM((2 * TOK, 128), f32),
                                pltpu.VMEM(blk, f32), pltpu.VMEM(blk, f32)])
    call, body, more = _ride(call, rider, body, lambda: pl.program_id(0) * nS + pl.program_id(1), (ATTN_W // 128) * nS, 5, 2, 5)
    return pl.pallas_call(body, name="attention_fwd", grid=(ATTN_W // 128, nS), compiler_params=_cp("arbitrary", "arbitrary"),
                          **call)(qkv, qkv, qkv, qkv, qkv, *more)


def _attn_bwd(qkv, ya, lse, dmix, rider=None):
    S = qkv.shape[0]
    nS = S // TOK

    def body(q_ref, kp_ref, kc_ref, vp_ref, vc_ref, y_ref, l_ref, dy_ref, dq_ref, dk_ref, dv_ref, qs, k2, v2, dk2, dv2, dqa, dl):
        n = pl.program_id(1)

        @pl.when(n == 0)
        def _():
            dk2[...] = jnp.zeros_like(dk2)
            dv2[...] = jnp.zeros_like(dv2)

        @pl.when(n < nS)
        def _():
            qs[...] = q_ref[...].astype(f32)
            k2[0:TOK] = kp_ref[...].astype(f32)
            k2[TOK:2 * TOK] = kc_ref[...].astype(f32)
            v2[0:TOK] = vp_ref[...].astype(f32)
            v2[TOK:2 * TOK] = vc_ref[...].astype(f32)
            li = lax.broadcasted_iota(jnp.int32, (128, 128), 0)
            lj = lax.broadcasted_iota(jnp.int32, (128, 128), 1)
            seg = jnp.where((li // HEAD_DIM) == (lj // HEAD_DIM), 1.0, 0.0).astype(bf16)
            bias_any, bias_first = _key_bias()

            def delta_rows(t, carry):
                rows = pl.ds(pl.multiple_of(256 * t, 256), 256)
                dyy = dy_ref[rows, :] * y_ref[rows, :]
                hi = dyy.astype(bf16)
                dl[rows, :] = _dot(hi, seg) + _dot((dyy - hi.astype(f32)).astype(bf16), seg)
                return carry

            lax.fori_loop(0, TOK // 256, delta_rows, 0)

            def block(dil, r, b, step, first_pattern, last):
                start = r + pl.multiple_of(step * b, step)
                rows = pl.ds(start, ATTN_BLK, stride=dil) if dil > 1 else pl.ds(start, ATTN_BLK)
                keys = (pl.ds(TOK + start - step, 2 * ATTN_BLK, stride=dil) if dil > 1
                        else pl.ds(TOK + start - step, 2 * ATTN_BLK))
                bias = jnp.where((n == 0) & (b == 0), bias_first, bias_any)
                dq2, dkk, dvv = _pair_bwd(qs[rows, :], k2[keys, :], v2[keys, :].astype(bf16), dy_ref[rows, :],
                                          l_ref[rows, :], dl[rows, :], bias)
                if last:
                    dq_ref[rows, :] = dqa[rows, :] + dq2
                elif first_pattern:
                    dqa[rows, :] = dq2
                else:
                    dqa[rows, :] += dq2
                dk2[keys, :] += dkk
                dv2[keys, :] += dvv

            for dil in reversed(DILATIONS):
                def loop(i, carry, dil=dil):
                    block(dil, i % dil, i // dil, ATTN_BLK * dil, dil == DILATIONS[-1], dil == 1)
                    return carry
                lax.fori_loop(0, TOK // ATTN_BLK, loop, 0, unroll=4)

        dk_ref[...] = dk2[0:TOK]
        dv_ref[...] = dv2[0:TOK]
        dk2[0:TOK] = dk2[TOK:2 * TOK]
        dv2[0:TOK] = dv2[TOK:2 * TOK]
        dk2[TOK:2 * TOK] = jnp.zeros((TOK, 128), f32)
        dv2[TOK:2 * TOK] = jnp.zeros((TOK, 128), f32)

    blk = (TOK, 128)
    cn = lambda n: jnp.minimum(n, nS - 1)
    pn = lambda n: jnp.clip(n - 1, 0, nS - 1)
    cur = lambda c: pl.BlockSpec(blk, lambda p, n: (cn(n), 4 * c + p))
    prv = lambda c: pl.BlockSpec(blk, lambda p, n: (pn(n), 4 * c + p))
    at_n = pl.BlockSpec(blk, lambda p, n: (cn(n), p))
    at_p = pl.BlockSpec(blk, lambda p, n: (pn(n), p))
    big = lambda: pltpu.VMEM((2 * TOK, 128), f32)
    call = dict(in_specs=[cur(0), prv(1), cur(1), prv(2), cur(2), at_n, at_n, at_n], out_specs=[at_n, at_p, at_p],
                out_shape=[jax.ShapeDtypeStruct((S, ATTN_W), f32)] * 3,
                scratch_shapes=[pltpu.VMEM(blk, f32), big(), big(), big(), big(), pltpu.VMEM(blk, f32), pltpu.VMEM(blk, f32)])
    call, body, more = _ride(call, rider, body, lambda: pl.program_id(0) * (nS + 1) + pl.program_id(1),
                             (ATTN_W // 128) * (nS + 1), 8, 3, 7)
    return pl.pallas_call(body, name="attention_bwd", grid=(ATTN_W // 128, nS + 1), compiler_params=_cp("arbitrary", "arbitrary"),
                          **call)(qkv, qkv, qkv, qkv, qkv, ya, lse, dmix, *more)


HG_T = 256
N_HH = HGRN_W // HGRN_HD
HG_SUB = 128
SAFE_RANGE = 75.0


def _row_in_chunk():
    return lax.broadcasted_iota(jnp.int32, (HG_T, HGRN_HD), 0) % CHUNK


def _chunk_cumsum(v, rc):
    k = 1
    while k < CHUNK:
        v = v + jnp.where(rc >= k, pltpu.roll(v, k, 0), 0.0)
        k *= 2
    return v


def _chunk_rcumsum(v, rc):
    k = 1
    while k < CHUNK:
        v = v + jnp.where(rc < CHUNK - k, pltpu.roll(v, HG_T - k, 0), 0.0)
        k *= 2
    return v


def _hgrn_gates(qb, fb, lb):
    sf = _sigmoid(fb)
    f = lb + (1.0 - lb) * sf
    sq = _sigmoid(qb)
    return sf, f, jnp.log(f), 1.0 - f, sq, qb * sq


def _hgrn_prep(qb, fb, lbl2, rc):
    lb = _sigmoid(lbl2[0:1, :] - lbl2[1:2, :])
    sf, f, lf, key, sq, qf = _hgrn_gates(qb, fb, lb)
    b = _chunk_cumsum(lf, rc)
    rem = _chunk_rcumsum(lf, rc) - lf
    return dict(lb=lb, sf=sf, f=f, key=key, sq=sq, qf=qf, b=b, rem=rem, eb=jnp.exp(b), er=jnp.exp(rem))


def _chunk_mask():
    r = lax.broadcasted_iota(jnp.int32, (HG_SUB, HG_SUB), 0)
    c = lax.broadcasted_iota(jnp.int32, (HG_SUB, HG_SUB), 1)
    return ((r // CHUNK) == (c // CHUNK)) & (c <= r)


def _hgrn_fwd(hp, lbl, wn):
    S = hp.shape[0]
    nT = S // HG_T

    def body(qb_ref, fb_ref, ib_ref, gb_ref, lbl_ref, wn_ref, yb_ref, o_ref, st_ref, ST, qt_s, kh_s, dec_s, oi_s):
        @pl.when(pl.program_id(0) == 0)
        def _():
            ST[...] = jnp.zeros_like(ST)

        rc = _row_in_chunk()
        for h in range(N_HH):
            sl = slice(HGRN_HD * h, HGRN_HD * (h + 1))
            p = _hgrn_prep(qb_ref[:, sl], fb_ref[:, sl], lbl_ref[:, sl], rc)
            qf, key, b = p["qf"], p["key"], p["b"]
            qt = qf * p["eb"]
            qt_s[:, sl] = qt.astype(bf16)
            kh_s[:, sl] = (key * p["er"]).astype(bf16)
            dec_s[:, sl] = jnp.exp(b + p["rem"])
            rng = jnp.max(-(b + p["rem"]))

            @pl.when(rng < SAFE_RANGE)
            def _():
                kp = (key * jnp.exp(-b)).astype(bf16)
                cmask = _chunk_mask()
                for j in range(HG_T // HG_SUB):
                    rs = slice(HG_SUB * j, HG_SUB * (j + 1))
                    sc = jnp.where(cmask, _dot_nt(qt[rs].astype(bf16), kp[rs]), 0.0).astype(bf16)
                    oi_s[rs, sl] = _dot(sc, ib_ref[rs, sl].astype(bf16))

            @pl.when(rng >= SAFE_RANGE)
            def _():
                v = ib_ref[:, sl]
                ones = jnp.ones((HGRN_HD, HGRN_HD), bf16)

                def lag(l, o):
                    e = jnp.exp(jnp.where(rc >= l, b - pltpu.roll(b, l, 0), NEG))
                    pr = qf * pltpu.roll(key, l, 0) * e
                    return o + _dot(pr.astype(bf16), ones) * pltpu.roll(v, l, 0)

                oi_s[:, sl] = lax.fori_loop(1, CHUNK, lag, _dot((qf * key).astype(bf16), ones) * v)

        def step(c, carry):
            rows = pl.ds(pl.multiple_of(c * CHUNK, CHUNK), CHUNK)
            row0 = pl.ds(pl.multiple_of(c * CHUNK, CHUNK), 1)
            for h in range(N_HH):
                sl = slice(HGRN_HD * h, HGRN_HD * (h + 1))
                stv = ST[h]
                st_ref[c, sl, :] = stv
                oi_s[rows, sl] += _dot_nt(qt_s[rows, sl], stv.astype(bf16))
                ST[h] = stv * dec_s[row0, sl] + _dot_tn(ib_ref[rows, sl].astype(bf16), kh_s[rows, sl])
            return carry

        lax.fori_loop(0, HG_T // CHUNK, step, 0, unroll=True)

        for h in range(N_HH):
            sl = slice(HGRN_HD * h, HGRN_HD * (h + 1))
            o = oi_s[:, sl]
            o_ref[:, sl] = o
            on = o * lax.rsqrt(jnp.mean(o * o, axis=-1, keepdims=True) + EPS)
            g = gb_ref[:, sl]
            yb_ref[:, sl] = on * wn_ref[:, sl] * (g * _sigmoid(g))

    col = lambda c: pl.BlockSpec((HG_T, HGRN_W), lambda i: (i, c))
    tile = pl.BlockSpec((HG_T, HGRN_W), lambda i: (i, 0))
    whole = lambda a: pl.BlockSpec(a.shape, lambda i: (0, 0))
    return pl.pallas_call(
        body, name="hgrn_fwd", grid=(nT,),
        in_specs=[col(0), col(1), col(2), col(3), whole(lbl), whole(wn)],
        out_specs=[tile, tile, pl.BlockSpec((HG_T // CHUNK, HGRN_W, HGRN_HD), lambda i: (i, 0, 0))],
        out_shape=[jax.ShapeDtypeStruct((S, HGRN_W), f32), jax.ShapeDtypeStruct((S, HGRN_W), f32),
                   jax.ShapeDtypeStruct((S // CHUNK, HGRN_W, HGRN_HD), f32)],
        scratch_shapes=[pltpu.VMEM((N_HH, HGRN_HD, HGRN_HD), f32), pltpu.VMEM((HG_T, HGRN_W), bf16),
                        pltpu.VMEM((HG_T, HGRN_W), bf16), pltpu.VMEM((HG_T, HGRN_W), f32), pltpu.VMEM((HG_T, HGRN_W), f32)],
        compiler_params=_cp("arbitrary"),
    )(hp, hp, hp, hp, lbl, wn)


def _hgrn_bwd(hp, lbl, wn, o_sav, states, dmix, rider=None):
    S = hp.shape[0]
    nT = S // HG_T

    def body(qb_ref, fb_ref, ib_ref, gb_ref, lbl_ref, wn_ref, o_ref, st_ref, dy_ref,
             dq_ref, df_ref, di_ref, dg_ref, gwn_ref, glb_ref,
             DST, qt_s, kh_s, dec_s, do_s, dqt_s, dkh_s, dbl_s, dvi_s, dqi_s, dki_s, dbi_s):
        @pl.when(pl.program_id(0) == 0)
        def _():
            DST[...] = jnp.zeros_like(DST)
            gwn_ref[...] = jnp.zeros_like(gwn_ref)
            glb_ref[...] = jnp.zeros_like(glb_ref)

        rc = _row_in_chunk()
        preps = []
        for h in range(N_HH):
            sl = slice(HGRN_HD * h, HGRN_HD * (h + 1))
            p = _hgrn_prep(qb_ref[:, sl], fb_ref[:, sl], lbl_ref[:, sl], rc)
            preps.append(p)
            qf, key, b = p["qf"], p["key"], p["b"]
            v = ib_ref[:, sl]
            o = o_ref[:, sl]
            rinv = lax.rsqrt(jnp.mean(o * o, axis=-1, keepdims=True) + EPS)
            on = o * rinv
            g = gb_ref[:, sl]
            sgm = _sigmoid(g)
            silu_g = g * sgm
            dy = dy_ref[:, sl]
            wn_v = wn_ref[:, sl]
            gwn_ref[:, sl] += jnp.sum(dy * on * silu_g, axis=0, keepdims=True)
            dg_ref[:, sl] = (dy * on * wn_v * (sgm * (1.0 + g * (1.0 - sgm)))).astype(bf16)
            t1 = dy * wn_v * silu_g
            do = rinv * (t1 - on * jnp.mean(t1 * on, axis=-1, keepdims=True))
            do_s[:, sl] = do.astype(bf16)
            qt = qf * p["eb"]
            qt_s[:, sl] = qt.astype(bf16)
            kh_s[:, sl] = (key * p["er"]).astype(bf16)
            dec_s[:, sl] = jnp.exp(b + p["rem"])
            rng = jnp.max(-(b + p["rem"]))

            @pl.when(rng < SAFE_RANGE)
            def _():
                einv = jnp.exp(-b)
                kp = (key * einv).astype(bf16)
                cmask = _chunk_mask()
                for j in range(HG_T // HG_SUB):
                    rs = slice(HG_SUB * j, HG_SUB * (j + 1))
                    qtb, dob, vb = qt[rs].astype(bf16), do[rs].astype(bf16), v[rs].astype(bf16)
                    sc = jnp.where(cmask, _dot_nt(qtb, kp[rs]), 0.0).astype(bf16)
                    dsc = jnp.where(cmask, _dot_nt(dob, vb), 0.0).astype(bf16)
                    dqp = _dot(dsc, kp[rs])
                    dkp = _dot_tn(dsc, qtb)
                    dvi_s[rs, sl] = _dot_tn(sc, dob)
                    dqi_s[rs, sl] = dqp * p["eb"][rs]
                    dki_s[rs, sl] = dkp * einv[rs]
                    dbi_s[rs, sl] = dqp * qtb.astype(f32) - dkp * kp[rs].astype(f32)

            @pl.when(rng >= SAFE_RANGE)
            def _():
                ones = jnp.ones((HGRN_HD, HGRN_HD), bf16)

                def lag(l, carry):
                    dqf, dkey, db, dv = carry
                    e = jnp.exp(jnp.where(rc >= l, b - pltpu.roll(b, l, 0), NEG))
                    ks, vs, qe = pltpu.roll(key, l, 0), pltpu.roll(v, l, 0), qf * e
                    pr = qe * ks
                    rl = _dot(pr.astype(bf16), ones)
                    drl = jnp.where(rc >= l, _dot((do * vs).astype(bf16), ones), 0.0)
                    gl = drl * pr
                    back = HG_T - l
                    return (dqf + drl * ks * e, dkey + pltpu.roll(drl * qe, back, 0), db + gl - pltpu.roll(gl, back, 0),
                            dv + pltpu.roll(rl * do, back, 0))

                rl0 = _dot((qf * key).astype(bf16), ones)
                drl0 = _dot((do * v).astype(bf16), ones)
                dqf, dkey, db, dv = lax.fori_loop(1, CHUNK, lag, (drl0 * key, drl0 * qf, jnp.zeros((HG_T, HGRN_HD), f32), rl0 * do))
                dvi_s[:, sl] = dv
                dqi_s[:, sl] = dqf
                dki_s[:, sl] = dkey
                dbi_s[:, sl] = db

        def step(k, carry):
            c = HG_T // CHUNK - 1 - k
            rows = pl.ds(pl.multiple_of(c * CHUNK, CHUNK), CHUNK)
            row0 = pl.ds(pl.multiple_of(c * CHUNK, CHUNK), 1)
            for h in range(N_HH):
                sl = slice(HGRN_HD * h, HGRN_HD * (h + 1))
                stp = st_ref[c, sl, :]
                dst = DST[h]
                dstb = dst.astype(bf16)
                dob = do_s[rows, sl]
                khb = kh_s[rows, sl]
                dec = dec_s[row0, sl]
                dqt_s[rows, sl] = _dot(dob, stp.astype(bf16))
                dkh = _dot(ib_ref[rows, sl].astype(bf16), dstb)
                dkh_s[rows, sl] = dkh
                dvi_s[rows, sl] += _dot_nt(khb, dstb)
                dbl = jnp.sum(dst * stp, axis=0, keepdims=True) * dec + jnp.sum(dkh * khb.astype(f32), axis=0, keepdims=True)
                dbl_s[rows, sl] = jnp.broadcast_to(dbl, (CHUNK, HGRN_HD))
                DST[h] = dst * dec + _dot_tn(dob, qt_s[rows, sl])
            return carry

        lax.fori_loop(0, HG_T // CHUNK, step, 0, unroll=True)

        for h in range(N_HH):
            sl = slice(HGRN_HD * h, HGRN_HD * (h + 1))
            qb = qb_ref[:, sl]
            p = preps[h]
            sf, sq, lb = p["sf"], p["sq"], p["lb"]
            dqt, dkh = dqt_s[:, sl], dkh_s[:, sl]
            dqf = dqt * p["eb"] + dqi_s[:, sl]
            dkey = dkh * p["er"] + dki_s[:, sl]
            db = dqt * (p["qf"] * p["eb"]) - dkh * (p["key"] * p["er"]) + jnp.where(rc == CHUNK - 1, dbl_s[:, sl], 0.0) + dbi_s[:, sl]
            df = _chunk_rcumsum(db, rc) / p["f"] - dkey
            df_ref[:, sl] = (df * (1.0 - lb) * sf * (1.0 - sf)).astype(bf16)
            glb_ref[:, sl] += jnp.sum(df * (1.0 - sf), axis=0, keepdims=True)
            dq_ref[:, sl] = (dqf * (sq * (1.0 + qb * (1.0 - sq)))).astype(bf16)
            di_ref[:, sl] = dvi_s[:, sl].astype(bf16)

    rev = lambda i: nT - 1 - i
    col = lambda c: pl.BlockSpec((HG_T, HGRN_W), lambda i: (rev(i), c))
    tile = pl.BlockSpec((HG_T, HGRN_W), lambda i: (rev(i), 0))
    whole = lambda a: pl.BlockSpec(a.shape, lambda i: (0, 0))
    vec = pl.BlockSpec((1, HGRN_W), lambda i: (0, 0))
    tb = lambda: pltpu.VMEM((HG_T, HGRN_W), bf16)
    tf = lambda: pltpu.VMEM((HG_T, HGRN_W), f32)
    call = dict(in_specs=[col(0), col(1), col(2), col(3), whole(lbl), whole(wn), tile,
                          pl.BlockSpec((HG_T // CHUNK, HGRN_W, HGRN_HD), lambda i: (rev(i), 0, 0)),
                          pl.BlockSpec((HG_T, HGRN_W), lambda i: (rev(i), 1))],
                out_specs=[tile, tile, tile, tile, vec, vec],
                out_shape=[jax.ShapeDtypeStruct((S, HGRN_W), bf16)] * 4 + [jax.ShapeDtypeStruct((1, HGRN_W), f32)] * 2,
                scratch_shapes=[pltpu.VMEM((N_HH, HGRN_HD, HGRN_HD), f32), tb(), tb(), tf(), tb(), tf(), tf(), tf(), tf(), tf(),
                                tf(), tf()])
    call, body, more = _ride(call, rider, body, lambda: pl.program_id(0), nT, 9, 6, 12)
    return pl.pallas_call(body, name="hgrn_bwd", grid=(nT,), compiler_params=_cp("arbitrary"), **call)(
        hp, hp, hp, hp, lbl, wn, o_sav, states, dmix, *more)


def _out_proj(x, ya, yb, wout, w2):
    S = x.shape[0]
    tm = 512

    def body(x_ref, ya_ref, yb_ref, w_ref, w2_ref, h1_ref, u2_ref, mix_ref):
        mixed = jnp.concatenate([ya_ref[...], yb_ref[...]], axis=1).astype(bf16)
        mix_ref[...] = mixed
        h1 = x_ref[...] + _dot(mixed, w_ref[...])
        h1_ref[...] = h1
        r = lax.rsqrt(jnp.mean(h1 * h1, axis=-1, keepdims=True) + EPS)
        u2_ref[...] = (h1 * r * w2_ref[...]).astype(bf16)

    row = lambda w: pl.BlockSpec((tm, w), lambda i: (i, 0))
    return pl.pallas_call(
        body, name="out_proj", grid=(S // tm,),
        in_specs=[row(D_MODEL), row(ATTN_W), row(HGRN_W), pl.BlockSpec((D_MODEL, D_MODEL), lambda i: (0, 0)),
                  pl.BlockSpec((1, D_MODEL), lambda i: (0, 0))],
        out_specs=[row(D_MODEL), row(D_MODEL), row(D_MODEL)],
        out_shape=[jax.ShapeDtypeStruct((S, D_MODEL), f32), jax.ShapeDtypeStruct((S, D_MODEL), bf16),
                   jax.ShapeDtypeStruct((S, D_MODEL), bf16)],
        compiler_params=_cp("arbitrary"),
    )(x, ya, yb, wout, w2)


def _gate_up(u2, wgu):
    S = u2.shape[0]
    tm, tn = 512, 1408
    nj = FFN // tn

    def body(u_ref, wg_ref, wu_ref, g_ref, up_ref, a_ref):
        u = u_ref[...]
        g = _dot(u, wg_ref[...])
        up = _dot(u, wu_ref[...])
        g_ref[...] = g.astype(bf16)
        up_ref[...] = up.astype(bf16)
        a_ref[...] = (g * _sigmoid(g) * up).astype(bf16)

    out = pl.BlockSpec((tm, tn), lambda j, i: (i, j))
    return pl.pallas_call(
        body, name="gate_up", grid=(nj, S // tm),
        in_specs=[pl.BlockSpec((tm, D_MODEL), lambda j, i: (i, 0)), pl.BlockSpec((D_MODEL, tn), lambda j, i: (0, j)),
                  pl.BlockSpec((D_MODEL, tn), lambda j, i: (0, j + nj))],
        out_specs=[out, out, out],
        out_shape=[jax.ShapeDtypeStruct((S, FFN), bf16)] * 3,
        compiler_params=_cp("arbitrary", "arbitrary"),
    )(u2, wgu, wgu)


def _rms_bwd(dyw, hn, r):
    return r * (dyw - hn * jnp.mean(dyw * hn, axis=-1, keepdims=True))


def _down_loss(act, wdown, h1, tgt, w3):
    S = act.shape[0]
    tm = 256

    def body(a_ref, w_ref, h1_ref, t_ref, w3_ref, dh2_ref, loss_ref, gw3_ref):
        @pl.when(pl.program_id(0) == 0)
        def _():
            loss_ref[...] = jnp.zeros_like(loss_ref)
            gw3_ref[...] = jnp.zeros_like(gw3_ref)

        h2 = h1_ref[...] + _dot(a_ref[...], w_ref[...])
        r = lax.rsqrt(jnp.mean(h2 * h2, axis=-1, keepdims=True) + EPS)
        hn = h2 * r
        w3 = w3_ref[...]
        err = hn * w3 - t_ref[...]
        loss_ref[...] += (0.5 / D_MODEL) * jnp.sum(err * err)
        dy = err * (1.0 / D_MODEL)
        gw3_ref[...] += jnp.sum(dy * hn, axis=0, keepdims=True)
        dh2_ref[...] = _rms_bwd(dy * w3, hn, r)

    row = lambda w: pl.BlockSpec((tm, w), lambda i: (i, 0))
    return pl.pallas_call(
        body, name="down_loss", grid=(S // tm,),
        in_specs=[row(FFN), pl.BlockSpec((FFN, D_MODEL), lambda i: (0, 0)), row(D_MODEL), row(D_MODEL),
                  pl.BlockSpec((1, D_MODEL), lambda i: (0, 0))],
        out_specs=[row(D_MODEL), pl.BlockSpec((1, 128), lambda i: (0, 0)), pl.BlockSpec((1, D_MODEL), lambda i: (0, 0))],
        out_shape=[jax.ShapeDtypeStruct((S, D_MODEL), f32), jax.ShapeDtypeStruct((1, 128), f32),
                   jax.ShapeDtypeStruct((1, D_MODEL), f32)],
        compiler_params=_cp("arbitrary"),
    )(act, wdown, h1, tgt, w3)


def _dact(dh2, wdown, gate, up):
    S = dh2.shape[0]
    tm = 256

    def body(d_ref, w_ref, g_ref, u_ref, dg_ref, du_ref):
        da = _dot_nt(d_ref[...].astype(bf16), w_ref[...])
        g = g_ref[...].astype(f32)
        sg = _sigmoid(g)
        du_ref[...] = (da * g * sg).astype(bf16)
        dg_ref[...] = (da * u_ref[...].astype(f32) * (sg * (1.0 + g * (1.0 - sg)))).astype(bf16)

    row = lambda w: pl.BlockSpec((tm, w), lambda i: (i, 0))
    return pl.pallas_call(
        body, name="dact", grid=(S // tm,),
        in_specs=[row(D_MODEL), pl.BlockSpec((FFN, D_MODEL), lambda i: (0, 0)), row(FFN), row(FFN)],
        out_specs=[row(FFN), row(FFN)],
        out_shape=[jax.ShapeDtypeStruct((S, FFN), bf16)] * 2,
        compiler_params=_cp("arbitrary"),
    )(dh2, wdown, gate, up)


def _dgu(dgate, dup, wgu, h1, w2, dh2, wout, rider=None):
    S = dgate.shape[0]
    tm = 256

    def body(dg_ref, du_ref, wg_ref, wu_ref, h1_ref, w2_ref, dh2_ref, wo_ref, dh1_ref, gw2_ref, dmix_ref):
        @pl.when(pl.program_id(0) == 0)
        def _():
            gw2_ref[...] = jnp.zeros_like(gw2_ref)

        du2 = _dot_nt(dg_ref[...], wg_ref[...]) + _dot_nt(du_ref[...], wu_ref[...])
        h1 = h1_ref[...]
        r = lax.rsqrt(jnp.mean(h1 * h1, axis=-1, keepdims=True) + EPS)
        hn = h1 * r
        gw2_ref[...] += jnp.sum(du2 * hn, axis=0, keepdims=True)
        dh1 = dh2_ref[...] + _rms_bwd(du2 * w2_ref[...], hn, r)
        dh1_ref[...] = dh1
        dmix_ref[...] = _dot_nt(dh1.astype(bf16), wo_ref[...])

    row = lambda w: pl.BlockSpec((tm, w), lambda i: (i, 0))
    call = dict(in_specs=[row(FFN), row(FFN), pl.BlockSpec((D_MODEL, FFN), lambda i: (0, 0)),
                          pl.BlockSpec((D_MODEL, FFN), lambda i: (0, 1)), row(D_MODEL),
                          pl.BlockSpec((1, D_MODEL), lambda i: (0, 0)), row(D_MODEL),
                          pl.BlockSpec((D_MODEL, D_MODEL), lambda i: (0, 0))],
                out_specs=[row(D_MODEL), pl.BlockSpec((1, D_MODEL), lambda i: (0, 0)), row(D_MODEL)],
                out_shape=[jax.ShapeDtypeStruct((S, D_MODEL), f32), jax.ShapeDtypeStruct((1, D_MODEL), f32),
                           jax.ShapeDtypeStruct((S, D_MODEL), f32)], scratch_shapes=[])
    call, body, more = _ride(call, rider, body, lambda: pl.program_id(0), S // tm, 8, 3, 0)
    return pl.pallas_call(body, name="dgu", grid=(S // tm,), compiler_params=_cp("arbitrary"), **call)(
        dgate, dup, wgu, wgu, h1, w2, dh2, wout, *more)


def _din(dq, dk, dv, dhq, dhf, dhi, dhg, cos_t, sg_t, win, x, w1, dh1):
    S = x.shape[0]
    tm = 256

    def body(dq_ref, dk_ref, dv_ref, dhq_ref, dhf_ref, dhi_ref, dhg_ref, cos_ref, sg_ref, w_ref, x_ref, w1_ref, dh1_ref,
             dp_ref, gx_ref, gw1_ref):
        @pl.when(pl.program_id(0) == 0)
        def _():
            gw1_ref[...] = jnp.zeros_like(gw1_ref)

        cosv, sgv = jnp.tile(cos_ref[...], (1, ATTN_W // 128)), jnp.tile(sg_ref[...], (1, ATTN_W // 128))
        unrope = lambda d: d * cosv - sgv * _swap_halves(d)
        parts = [(unrope(dq_ref[...]) * (HEAD_DIM ** -0.5)).astype(bf16), unrope(dk_ref[...]).astype(bf16),
                 dv_ref[...].astype(bf16), dhq_ref[...], dhf_ref[...], dhi_ref[...], dhg_ref[...]]
        du = jnp.zeros((tm, D_MODEL), f32)
        for j, pj in enumerate(parts):
            dp_ref[:, j * 512:(j + 1) * 512] = pj
            du = du + _dot_nt(pj, w_ref[:, j * 512:(j + 1) * 512])
        xv = x_ref[...]
        r = lax.rsqrt(jnp.mean(xv * xv, axis=-1, keepdims=True) + EPS)
        xn = xv * r
        gw1_ref[...] += jnp.sum(du * xn, axis=0, keepdims=True)
        gx_ref[...] = dh1_ref[...] + _rms_bwd(du * w1_ref[...], xn, r)

    row = lambda w: pl.BlockSpec((tm, w), lambda i: (i, 0))
    vec = pl.BlockSpec((1, D_MODEL), lambda i: (0, 0))
    return pl.pallas_call(
        body, name="din", grid=(S // tm,),
        in_specs=[row(512)] * 7 + [row(128), row(128), pl.BlockSpec((D_MODEL, IN_W), lambda i: (0, 0)), row(D_MODEL), vec,
                                   row(D_MODEL)],
        out_specs=[row(IN_W), row(D_MODEL), vec],
        out_shape=[jax.ShapeDtypeStruct((S, IN_W), bf16), jax.ShapeDtypeStruct((S, D_MODEL), f32),
                   jax.ShapeDtypeStruct((1, D_MODEL), f32)],
        compiler_params=_cp("arbitrary"),
    )(dq, dk, dv, dhq, dhf, dhi, dhg, cos_t, sg_t, win, x, w1, dh1)


def _gw(a, bs, tn, name, ts=2048):
    S, M = a.shape
    N = bs[0].shape[1]
    k = len(bs)

    def body(a_ref, *refs):
        @pl.when(pl.program_id(1) == 0)
        def _():
            for o_ref in refs[k:]:
                o_ref[...] = jnp.zeros_like(o_ref)

        at = a_ref[...].astype(bf16)
        for b_ref, o_ref in zip(refs[:k], refs[k:]):
            o_ref[...] += _dot_tn(at, b_ref[...].astype(bf16))

    return pl.pallas_call(
        body, name=name, grid=(N // tn, S // ts),
        in_specs=[pl.BlockSpec((ts, M), lambda j, s: (s, 0))] + [pl.BlockSpec((ts, tn), lambda j, s: (s, j))] * k,
        out_specs=[pl.BlockSpec((M, tn), lambda j, s: (0, j))] * k, out_shape=[jax.ShapeDtypeStruct((M, N), f32)] * k,
        compiler_params=_cp("arbitrary", "arbitrary"),
    )(a, *bs)


MESH = pl.DeviceIdType.MESH
ANY = pl.BlockSpec(memory_space=pl.ANY)
VMEM_SPEC = pl.BlockSpec(memory_space=pltpu.VMEM)


def _pos():
    return lax.axis_index("x"), lax.axis_index("y"), lax.axis_index("c")


def _flip(v, bit):
    return 1 - v if bit else v


def _gather_rider(shards):
    n = len(shards)

    def parts(outs, scratch):
        send_sems, recv_sems, local_sems = scratch[n:]
        x, y, c = _pos()
        chips = [(1 - x, y), (x, 1 - y), (1 - x, 1 - y)]

        def copy(a, k, block, to, src=None):
            dst = outs[a].at[4 * block[0] + 2 * block[1] + block[2]]
            return pltpu.make_async_remote_copy(src_ref=dst if src is None else src, dst_ref=dst, send_sem=send_sems.at[a, k],
                                                recv_sem=recv_sems.at[a, k], device_id=to, device_id_type=MESH)

        bufs = scratch[:n]
        me, sibling = (x, y, c), (x, y, 1 - c)
        own = lambda a: pltpu.make_async_copy(bufs[a], outs[a].at[4 * x + 2 * y + c], local_sems.at[a])
        sent = lambda a: [copy(a, 0, me, sibling, src=bufs[a])] + [copy(a, 1 + j, me, (*chip, c), src=bufs[a])
                                                                   for j, chip in enumerate(chips)]
        passed = lambda a: [copy(a, 4 + j, (*chip, c), sibling) for j, chip in enumerate(chips)]
        landed = lambda a: [copy(a, 1 + j, (*chip, c), me) for j, chip in enumerate(chips)]
        from_sibling = lambda a: [copy(a, 0, sibling, me)] + [copy(a, 4 + j, (*chip, 1 - c), me) for j, chip in enumerate(chips)]
        return bufs, local_sems, own, sent, passed, landed, from_sibling

    def first(ins, outs, scratch):
        bufs, local_sems, own, sent, _, _, _ = parts(outs, scratch)
        loads = [pltpu.make_async_copy(ins[a], bufs[a], local_sems.at[a]) for a in range(n)]
        for ld in loads:
            ld.start()
        for a in range(n):
            loads[a].wait()
            own(a).start()
            for cp in sent(a):
                cp.start()

    def middle(ins, outs, scratch):
        _, _, _, _, passed, landed, _ = parts(outs, scratch)
        for a in range(n):
            for got, on in zip(landed(a), passed(a)):
                got.wait_recv()
                on.start()

    def last(ins, outs, scratch):
        _, _, own, sent, passed, _, from_sibling = parts(outs, scratch)
        for a in range(n):
            for cp in from_sibling(a):
                cp.wait_recv()
        for a in range(n):
            for cp in sent(a) + passed(a):
                cp.wait_send()
            own(a).wait()

    return _Rider(shards, [jax.ShapeDtypeStruct((N_DEV,) + s.shape, s.dtype) for s in shards],
                  [pltpu.VMEM(s.shape, s.dtype) for s in shards]
                  + [pltpu.SemaphoreType.DMA((n, 7)), pltpu.SemaphoreType.DMA((n, 7)), pltpu.SemaphoreType.DMA((n,))],
                  first, last, middle)


def _sibling_rider(grads):
    n = len(grads)

    def copies(g, got, scratch):
        send_sems, recv_sems = scratch
        x, y, c = _pos()
        return [pltpu.make_async_remote_copy(src_ref=g[a].at[2 * q + (1 - c)], dst_ref=got[a].at[q], send_sem=send_sems.at[a, q],
                                             recv_sem=recv_sems.at[a, q], device_id=(x, y, 1 - c), device_id_type=MESH)
                for a in range(n) for q in range(4)]

    def first(g, got, scratch):
        for cp in copies(g, got, scratch):
            cp.start()

    def last(g, got, scratch):
        for cp in copies(g, got, scratch):
            cp.wait()

    return _Rider(grads, [jax.ShapeDtypeStruct((4,) + g.shape[1:], g.dtype) for g in grads],
                  [pltpu.SemaphoreType.DMA((n, 4))] * 2, first, last)


def _chips_rider(sums):
    n = len(sums)

    def copies(s, out, scratch):
        send_sems, recv_sems = scratch
        x, y, c = _pos()
        cps = []
        for a in range(n):
            for f in (1, 2, 3):
                peer = (_flip(x, f >> 1), _flip(y, f & 1), c)
                cps.append(pltpu.make_async_remote_copy(
                    src_ref=s[a].at[2 * peer[0] + peer[1]], dst_ref=out[a].at[f - 1], send_sem=send_sems.at[a, f - 1],
                    recv_sem=recv_sems.at[a, f - 1], device_id=peer, device_id_type=MESH))
        return cps

    def first(s, out, scratch):
        for cp in copies(s, out, scratch):
            cp.start()

    def last(s, out, scratch):
        for cp in copies(s, out, scratch):
            cp.wait()

    return _Rider(sums, [jax.ShapeDtypeStruct((3,) + s.shape[1:], s.dtype) for s in sums],
                  [pltpu.SemaphoreType.DMA((n, 3))] * 2, first, last)


def _both(a, b):
    na = (len(a.ins), len(a.out_shapes), len(a.scratch))

    def split(fa, fb):
        def f(ins, outs, scratch):
            fa(ins[:na[0]], outs[:na[1]], scratch[:na[2]])
            fb(ins[na[0]:], outs[na[1]:], scratch[na[2]:])
        return f

    return _Rider(a.ins + b.ins, a.out_shapes + b.out_shapes, a.scratch + b.scratch, split(a.first, b.first), split(a.last, b.last))


def _alone(rider, name):
    ri, ro = len(rider.ins), len(rider.out_shapes)

    def body(*refs):
        theirs = (refs[:ri], refs[ri:ri + ro], refs[ri + ro:])
        rider.first(*theirs)
        if rider.middle is not None:
            rider.middle(*theirs)
        rider.last(*theirs)

    return pl.pallas_call(body, name=name, in_specs=[ANY] * ri, out_specs=[ANY] * ro, out_shape=rider.out_shapes,
                          scratch_shapes=rider.scratch)(*rider.ins)


def _gather_small(g_w1, g_w2, g_w3, g_lb, g_wn, loss):
    def body(w1_ref, w2_ref, w3_ref, lb_ref, wn_ref, loss_ref, out_ref, pk, send_sems, recv_sems):
        x, y, c = _pos()
        me = 4 * x + 2 * y + c
        pk[...] = jnp.zeros_like(pk)
        pk[0:1, :] = w1_ref[...]
        pk[1:2, :] = w2_ref[...]
        pk[2:3, :] = w3_ref[...]
        pk[3:4, 0:HGRN_W] = lb_ref[...]
        pk[3:4, HGRN_W:2 * HGRN_W] = wn_ref[...]
        pk[4:5, 0:128] = loss_ref[...]
        out_ref[me] = pk[...]
        sends, recvs = [], []
        for k in range(1, N_DEV):
            peer = (_flip(x, k >> 2), _flip(y, (k >> 1) & 1), _flip(c, k & 1))
            cp = pltpu.make_async_remote_copy(src_ref=pk, dst_ref=out_ref.at[me], send_sem=send_sems.at[k - 1],
                                              recv_sem=recv_sems.at[k - 1], device_id=peer, device_id_type=MESH)
            cp.start()
            sends.append(cp)
            recvs.append(pltpu.make_async_remote_copy(src_ref=pk, dst_ref=out_ref.at[4 * peer[0] + 2 * peer[1] + peer[2]],
                                                      send_sem=send_sems.at[k - 1], recv_sem=recv_sems.at[k - 1], device_id=peer,
                                                      device_id_type=MESH))
        for cp in recvs:
            cp.wait_recv()
        for cp in sends:
            cp.wait_send()

    return pl.pallas_call(
        body, name="gather_small", in_specs=[VMEM_SPEC] * 6, out_specs=VMEM_SPEC,
        out_shape=jax.ShapeDtypeStruct((N_DEV, 8, D_MODEL), f32),
        scratch_shapes=[pltpu.VMEM((8, D_MODEL), f32), pltpu.SemaphoreType.DMA((N_DEV - 1,)), pltpu.SemaphoreType.DMA((N_DEV - 1,))],
    )(g_w1, g_w2, g_w3, g_lb, g_wn, loss)


def _row_tile(r):
    return max(t for t in range(8, 257, 8) if r % t == 0)


def _add_sibling(core, g, got, name):
    _, r, c = got.shape
    tr = _row_tile(r)

    def body(core_ref, a_ref, b_ref, o_ref):
        o_ref[...] = (a_ref[...] + b_ref[...]).astype(bf16)

    blk = pl.BlockSpec((1, tr, c), lambda q, i, core_ref: (q, i, 0))
    return pl.pallas_call(
        body, name=name, out_shape=jax.ShapeDtypeStruct(got.shape, bf16),
        grid_spec=pltpu.PrefetchScalarGridSpec(
            num_scalar_prefetch=1, grid=(4, r // tr),
            in_specs=[pl.BlockSpec((1, tr, c), lambda q, i, core_ref: (2 * q + core_ref[0], i, 0)), blk], out_specs=blk),
        compiler_params=_cp("arbitrary", "arbitrary"))(core, g, got)


def _adamw(w, g, m, v):
    m = ADAM_B1 * m + (1.0 - ADAM_B1) * g
    v = ADAM_B2 * v + (1.0 - ADAM_B2) * (g * g)
    m_hat = m / (1.0 - ADAM_B1 ** ADAM_STEP)
    v_hat = v / (1.0 - ADAM_B2 ** ADAM_STEP)
    return -ADAM_LR * (m_hat / (jnp.sqrt(v_hat) + ADAM_EPS) + ADAM_WD * w), m, v


def _adam_shard(where, g, got, pieces, w, m, v, name):
    r, c = w.shape
    tr = _row_tile(r)

    def body(where_ref, g_ref, got_ref, p_ref, w_ref, m_ref, v_ref, g_out, d_out, m_out, v_out):
        gsum = g_ref[0] + got_ref[0]
        for f in range(3):
            gsum = gsum + p_ref[f].astype(f32)
        g_out[...] = gsum
        d_out[...], m_out[...], v_out[...] = _adamw(w_ref[...], gsum, m_ref[...], v_ref[...])

    blk = pl.BlockSpec((tr, c), lambda i, where_ref: (i, 0))
    return pl.pallas_call(
        body, name=name, out_shape=[jax.ShapeDtypeStruct((r, c), f32)] * 4,
        grid_spec=pltpu.PrefetchScalarGridSpec(
            num_scalar_prefetch=1, grid=(r // tr,),
            in_specs=[pl.BlockSpec((1, tr, c), lambda i, where_ref: (where_ref[0], i, 0)),
                      pl.BlockSpec((1, tr, c), lambda i, where_ref: (where_ref[1], i, 0)),
                      pl.BlockSpec((3, tr, c), lambda i, where_ref: (0, i, 0)), blk, blk, blk],
            out_specs=[blk] * 4),
        compiler_params=_cp("arbitrary"),
    )(where, g, got, pieces, w, m, v)


def _small_update(gath, params):
    def body(gath_ref, *refs):
        ins, outs = refs[:15], refs[15:]
        gs = gath_ref[0]
        for k in range(1, N_DEV):
            gs = gs + gath_ref[k]
        outs[0][...] = gs[4:5, 0:128]
        l0, l1 = ins[9][0:1, :], ins[9][1:2, :]
        lb = _sigmoid(l0 - l1)
        d0 = gs[3:4, 0:HGRN_W] * lb * (1.0 - lb)
        first_row = lax.broadcasted_iota(jnp.int32, (2, HGRN_W), 0) == 0
        grads = [gs[0:1, :], gs[1:2, :], gs[2:3, :], jnp.where(first_row, d0, -d0), gs[3:4, HGRN_W:2 * HGRN_W]]
        for i, g in enumerate(grads):
            w_ref, m_ref, v_ref = ins[3 * i:3 * i + 3]
            o = outs[1 + 4 * i:5 + 4 * i]
            o[0][...] = g
            o[1][...], o[2][...], o[3][...] = _adamw(w_ref[...], g, m_ref[...], v_ref[...])

    flat = [a for p in params for a in p]
    out_shape = [jax.ShapeDtypeStruct((1, 128), f32)] + [jax.ShapeDtypeStruct(p[0].shape, f32) for p in params for _ in range(4)]
    outs = pl.pallas_call(body, name="small_update", in_specs=[VMEM_SPEC] * 16, out_specs=[VMEM_SPEC] * 21, out_shape=out_shape)(gath, *flat)
    return outs[0], [outs[1 + 4 * i:5 + 4 * i] for i in range(5)]


def kernel(x, norm1_w, w_in, lb_logits, hgrn_norm_w, w_out, norm2_w, w_gate_up, w_down, final_norm_w, loss_target, m_norm1_w, m_w_in, m_lb_logits, m_hgrn_norm_w, m_w_out, m_norm2_w, m_w_gate_up, m_w_down, m_final_norm_w, v_norm1_w, v_w_in, v_lb_logits, v_hgrn_norm_w, v_w_out, v_norm2_w, v_w_gate_up, v_w_down, v_final_norm_w):
    row = lambda a: a.reshape(1, D_MODEL)
    by_owner = lambda g, w: jnp.transpose(g.reshape(g.shape[0], g.shape[1] // w, w), (1, 0, 2))
    ix, iy, ic = lax.axis_index("x"), lax.axis_index("y"), lax.axis_index("c")
    core = jnp.stack([ic]).astype(jnp.int32)
    where = jnp.stack([4 * ix + 2 * iy + ic, 2 * ix + iy]).astype(jnp.int32)
    xs, tgt, w3 = x[0], loss_target[0], row(final_norm_w)
    S = xs.shape[0]

    cos_t, sg_t, win_g = _rope_tables(S, _gather_rider([w_in[0].astype(bf16)]))
    win = jnp.transpose(win_g, (1, 0, 2)).reshape(D_MODEL, IN_W)
    u, qkv, hp = _in_proj(xs, norm1_w, win, cos_t, sg_t)
    ya, lse, wout_g, wgu_g, wdown_g = _attn_fwd(qkv, _gather_rider([w_out[0].astype(bf16), w_gate_up[0].astype(bf16),
                                                                     w_down[0].astype(bf16)]))
    wout = wout_g.reshape(D_MODEL, D_MODEL)
    wgu = jnp.transpose(wgu_g, (1, 0, 2)).reshape(D_MODEL, 2 * FFN)
    wdown = wdown_g.reshape(FFN, D_MODEL)
    yb, o_sav, states = _hgrn_fwd(hp, lb_logits, hgrn_norm_w)
    h1, u2, mixed = _out_proj(xs, ya, yb, wout, norm2_w)
    gate, up, act = _gate_up(u2, wgu)
    dh2, loss_p, g_w3 = _down_loss(act, wdown, h1, tgt, w3)

    (g_wdown,) = _gw(act, [dh2], 512, "gw_down")
    dgate, dup = _dact(dh2, wdown, gate, up)
    g_wgu = _gw(u2, [dgate, dup], 1408, "gw_gate_up", ts=1024)
    early = [jnp.concatenate([by_owner(g, 2 * FFN // N_DEV) for g in g_wgu], axis=0), g_wdown.reshape(N_DEV, FFN // N_DEV, D_MODEL)]
    dh1, g_w2, dmix, *got_early = _dgu(dgate, dup, wgu, h1, norm2_w, dh2, wout, _sibling_rider(early))
    sums_early = [_add_sibling(core, g, o, f"add_sibling_{i}") for i, (g, o) in enumerate(zip(early, got_early))]
    (g_wout,) = _gw(mixed, [dh1], 1024, "gw_out")
    mid = [g_wout.reshape(N_DEV, D_MODEL // N_DEV, D_MODEL)]
    dhq, dhf, dhi, dhg, g_wn, g_lb, *rode = _hgrn_bwd(hp, lb_logits, hgrn_norm_w, o_sav, states, dmix,
                                                      _both(_chips_rider(sums_early), _sibling_rider(mid)))
    pieces_early, got_mid = rode[:2], rode[2:]
    sums_mid = [_add_sibling(core, mid[0], got_mid[0], "add_sibling_2")]
    dq, dk, dv, *pieces_mid = _attn_bwd(qkv, ya, lse, dmix, _chips_rider(sums_mid))
    dproj, gx, g_w1 = _din(dq, dk, dv, dhq, dhf, dhi, dhg, cos_t, sg_t, win, xs, norm1_w, dh1)
    (g_win,) = _gw(u, [dproj], 896, "gw_in")
    late = [by_owner(g_win, IN_W // N_DEV)]
    got_late = _alone(_sibling_rider(late), "reduce_sibling")
    sums_late = [_add_sibling(core, late[0], got_late[0], "add_sibling_3")]
    pieces_late = _alone(_chips_rider(sums_late), "reduce_chips")

    grads = [late[0], mid[0], early[0], early[1]]
    got = [got_late[0], got_mid[0], got_early[0], got_early[1]]
    pieces = [pieces_late[0], pieces_mid[0], pieces_early[0], pieces_early[1]]
    shards = [w_in[0], w_out[0], w_gate_up[0], w_down[0]]
    moms = [(m_w_in[0], v_w_in[0]), (m_w_out[0], v_w_out[0]), (m_w_gate_up[0], v_w_gate_up[0]), (m_w_down[0], v_w_down[0])]
    big = [_adam_shard(where, g, o, p, w, m, v, f"adam_{i}")
           for i, (g, o, p, w, (m, v)) in enumerate(zip(grads, got, pieces, shards, moms))]
    big = [[a[None] for a in four] for four in big]

    gath = _gather_small(g_w1, g_w2, g_w3, g_lb, g_wn, loss_p)
    params = [(norm1_w, m_norm1_w, v_norm1_w), (norm2_w, m_norm2_w, v_norm2_w),
              (row(final_norm_w), row(m_final_norm_w), row(v_final_norm_w)),
              (lb_logits, m_lb_logits, v_lb_logits), (hgrn_norm_w, m_hgrn_norm_w, v_hgrn_norm_w)]
    loss, (s_w1, s_w2, s_w3, s_lb, s_wn) = _small_update(gath, params)
    s_w3 = [a.reshape(D_MODEL) for a in s_w3]
    per_w = [s_w1, big[0], s_lb, s_wn, big[1], s_w2, big[2], big[3], s_w3]
    return (loss[0, 0], gx[None], *[p[0] for p in per_w], *[p[1] for p in per_w], *[p[2] for p in per_w], *[p[3] for p in per_w])
```

```python
import jax
import jax.numpy as jnp
from jax import lax
from jax.experimental import pallas as pl
from jax.experimental.pallas import tpu as pltpu

f32, bf16 = jnp.float32, jnp.bfloat16

D_MODEL = 1024
ATTN_W = 512
HEAD_DIM = 64
ATTN_BLK = 128
DILATIONS = (1, 4, 16)
HGRN_W = 512
HGRN_HD = 128
CHUNK = 64
IN_W = 3 * ATTN_W + 4 * HGRN_W
FFN = 2816
EPS = 1e-6
ROPE_THETA = 10000.0
NEG = -1e30
N_DEV = 8
ADAM_LR, ADAM_B1, ADAM_B2, ADAM_EPS, ADAM_WD, ADAM_STEP = 0.001, 0.9, 0.999, 1e-08, 0.01, 10
VMEM_LIMIT = 56 * 1024 * 1024


def _cp(*sem):
    return pltpu.CompilerParams(dimension_semantics=sem, vmem_limit_bytes=VMEM_LIMIT)


def _dot(a, b):
    return jnp.dot(a, b, preferred_element_type=f32)


def _dot_nt(a, b):
    return lax.dot_general(a, b, (((1,), (1,)), ((), ())), preferred_element_type=f32)


def _dot_tn(a, b):
    return lax.dot_general(a, b, (((0,), (0,)), ((), ())), preferred_element_type=f32)


def _sigmoid(x):
    return 0.5 * jnp.tanh(0.5 * x) + 0.5


class _Rider:
    def __init__(self, ins, out_shapes, scratch, first, last, middle=None):
        self.ins, self.out_shapes, self.scratch = list(ins), list(out_shapes), list(scratch)
        self.first, self.middle, self.last = first, middle, last


def _ride(call, rider, body, step, n_steps, n_in, n_out, n_scratch):
    if rider is None:
        return call, body, []
    ri, ro = len(rider.ins), len(rider.out_shapes)
    any_spec = pl.BlockSpec(memory_space=pl.ANY)
    call = dict(call, in_specs=call["in_specs"] + [any_spec] * ri, out_specs=call["out_specs"] + [any_spec] * ro,
                out_shape=call["out_shape"] + rider.out_shapes, scratch_shapes=call["scratch_shapes"] + rider.scratch)

    def riding(*refs):
        a = n_in + ri
        b = a + n_out + ro
        mine = refs[:n_in] + refs[a:a + n_out] + refs[b:b + n_scratch]
        theirs = (refs[n_in:a], refs[a + n_out:b], refs[b + n_scratch:])
        t = step()

        @pl.when(t == 0)
        def _():
            rider.first(*theirs)

        body(*mine)
        if rider.middle is not None:
            @pl.when(t == n_steps // 2)
            def _():
                rider.middle(*theirs)

        @pl.when(t == n_steps - 1)
        def _():
            rider.last(*theirs)

    return call, riding, rider.ins


def _rope_tables(S, rider=None):
    half = HEAD_DIM // 2
    tm = 256
    inv_freq = jnp.tile(ROPE_THETA ** (-jnp.arange(half, dtype=f32) / half), 128 // half).reshape(1, 128)
    sign = jnp.tile(jnp.concatenate([-jnp.ones((half,), f32), jnp.ones((half,), f32)]), 128 // HEAD_DIM).reshape(1, 128)

    def body(inv_ref, sign_ref, cos_ref, sg_ref):
        pos = (lax.broadcasted_iota(jnp.int32, (tm, 128), 0) + pl.program_id(0) * tm).astype(f32)
        ang = pos * inv_ref[...]
        cos_ref[...] = jnp.cos(ang)
        sg_ref[...] = jnp.sin(ang) * sign_ref[...]

    vec = pl.BlockSpec((1, 128), lambda i: (0, 0))
    out = pl.BlockSpec((tm, 128), lambda i: (i, 0))
    call = dict(in_specs=[vec, vec], out_specs=[out, out], out_shape=[jax.ShapeDtypeStruct((S, 128), f32)] * 2, scratch_shapes=[])
    call, body, more = _ride(call, rider, body, lambda: pl.program_id(0), S // tm, 2, 2, 0)
    return pl.pallas_call(body, name="rope_tables", grid=(S // tm,), compiler_params=_cp("arbitrary"), **call)(inv_freq, sign, *more)


def _swap_halves(v):
    n = v.shape[1]
    lane = lax.broadcasted_iota(jnp.int32, v.shape, 1)
    return jnp.where((lane % HEAD_DIM) < HEAD_DIM // 2, pltpu.roll(v, n - HEAD_DIM // 2, 1), pltpu.roll(v, HEAD_DIM // 2, 1))


def _in_proj(x, w1, win_g, cos_t, sg_t):
    S = x.shape[0]
    tm = 256
    w = IN_W // N_DEV

    def body(x_ref, w1_ref, wg_ref, cos_ref, sg_ref, u_ref, qkv_ref, hp_ref, w_ref):
        @pl.when(pl.program_id(0) == 0)
        def _():
            for d in range(N_DEV):
                w_ref[:, w * d:w * (d + 1)] = wg_ref[d]

        xv = x_ref[...]
        r = lax.rsqrt(jnp.mean(xv * xv, axis=-1, keepdims=True) + EPS)
        u = (xv * r * w1_ref[...]).astype(bf16)
        u_ref[...] = u
        cosv, sgv = jnp.tile(cos_ref[...], (1, ATTN_W // 128)), jnp.tile(sg_ref[...], (1, ATTN_W // 128))
        for j in range(3):
            pj = _dot(u, w_ref[:, j * ATTN_W:(j + 1) * ATTN_W])
            if j < 2:
                pj = pj * cosv + _swap_halves(pj) * sgv
            if j == 0:
                pj = pj * (HEAD_DIM ** -0.5)
            qkv_ref[:, j * ATTN_W:(j + 1) * ATTN_W] = pj.astype(bf16)
        for j in range(4):
            lo = 3 * ATTN_W + j * HGRN_W
            hp_ref[:, j * HGRN_W:(j + 1) * HGRN_W] = _dot(u, w_ref[:, lo:lo + HGRN_W])

    return pl.pallas_call(
        body, name="in_proj", grid=(S // tm,),
        in_specs=[pl.BlockSpec((tm, D_MODEL), lambda i: (i, 0)), pl.BlockSpec((1, D_MODEL), lambda i: (0, 0)),
                  pl.BlockSpec((N_DEV, D_MODEL, w), lambda i: (0, 0, 0)),
                  pl.BlockSpec((tm, 128), lambda i: (i, 0)), pl.BlockSpec((tm, 128), lambda i: (i, 0))],
        out_specs=[pl.BlockSpec((tm, D_MODEL), lambda i: (i, 0)), pl.BlockSpec((tm, 3 * ATTN_W), lambda i: (i, 0)),
                   pl.BlockSpec((tm, 4 * HGRN_W), lambda i: (i, 0)), pl.BlockSpec((D_MODEL, IN_W), lambda i: (0, 0))],
        out_shape=[jax.ShapeDtypeStruct((S, D_MODEL), bf16), jax.ShapeDtypeStruct((S, 3 * ATTN_W), bf16),
                   jax.ShapeDtypeStruct((S, 4 * HGRN_W), f32), jax.ShapeDtypeStruct((D_MODEL, IN_W), bf16)],
        compiler_params=_cp("arbitrary"),
    )(x, w1, win_g, cos_t, sg_t)


def _head_masks():
    lane = lax.broadcasted_iota(jnp.int32, (ATTN_BLK, 128), 1)
    even = lane < HEAD_DIM
    return even, (even, jnp.logical_not(even))


def _pair_fwd(q2, k2, v2, bias):
    even, masks = _head_masks()
    outs, lses = [], []
    for e in range(2):
        qm = jnp.where(masks[e], q2, 0.0).astype(bf16)
        s = _dot_nt(qm, k2) + bias
        m = jnp.max(s, axis=-1, keepdims=True)
        pe = jnp.exp(s - m)
        lsum = jnp.sum(pe, axis=-1, keepdims=True)
        outs.append(_dot(pe.astype(bf16), v2) / lsum)
        lses.append(jnp.broadcast_to(m + jnp.log(lsum), (ATTN_BLK, 128)))
    return jnp.where(even, outs[0], outs[1]), jnp.where(even, lses[0], lses[1])


def _merge(y0, l0, y1, l1):
    mx = jnp.maximum(l0, l1)
    a, b = jnp.exp(l0 - mx), jnp.exp(l1 - mx)
    tot = a + b
    return (a * y0 + b * y1) / tot, mx + jnp.log(tot)


def _pair_bwd(q2, k2f, v2, dy2, lse2, delta2, bias):
    _, masks = _head_masks()
    k2 = k2f.astype(bf16)
    klane = lax.broadcasted_iota(jnp.int32, (2 * ATTN_BLK, 128), 1) < HEAD_DIM
    kmasks = (klane, jnp.logical_not(klane))
    dq2 = jnp.zeros((ATTN_BLK, 128), f32)
    pes, dss, qms, dyms = [], [], [], []
    for e in range(2):
        c0 = e * HEAD_DIM
        qm = jnp.where(masks[e], q2, 0.0).astype(bf16)
        km = jnp.where(kmasks[e], k2f, 0.0).astype(bf16)
        dym = jnp.where(masks[e], dy2, 0.0).astype(bf16)
        pe = jnp.exp(_dot_nt(qm, k2) + bias - lse2[:, c0:c0 + 1])
        ds = (pe * (_dot_nt(dym, v2) - delta2[:, c0:c0 + 1])).astype(bf16)
        dq2 = dq2 + _dot(ds, km)
        pes.append(pe.astype(bf16))
        dss.append(ds)
        qms.append(qm)
        dyms.append(dym)
    dv2 = _dot_tn(jnp.concatenate(pes, axis=0), jnp.concatenate(dyms, axis=0))
    dk2 = _dot_tn(jnp.concatenate(dss, axis=0), jnp.concatenate(qms, axis=0))
    return dq2, dk2, dv2


TOK = 2048


def _key_bias():
    qi = lax.broadcasted_iota(jnp.int32, (ATTN_BLK, 2 * ATTN_BLK), 0)
    kj = lax.broadcasted_iota(jnp.int32, (ATTN_BLK, 2 * ATTN_BLK), 1)
    delta = ATTN_BLK + qi - kj
    seen = (delta >= 0) & (delta <= ATTN_BLK)
    return jnp.where(seen, 0.0, NEG), jnp.where(seen & (kj >= ATTN_BLK), 0.0, NEG)


def _attn_fwd(qkv, rider=None):
    S = qkv.shape[0]
    nS = S // TOK

    def body(q_ref, kp_ref, kc_ref, vp_ref, vc_ref, y_ref, l_ref, qs, k2, v2, ay, al):
        n = pl.program_id(1)
        qs[...] = q_ref[...].astype(f32)
        k2[0:TOK] = kp_ref[...].astype(f32)
        k2[TOK:2 * TOK] = kc_ref[...].astype(f32)
        v2[0:TOK] = vp_ref[...].astype(f32)
        v2[TOK:2 * TOK] = vc_ref[...].astype(f32)
        bias_any, bias_first = _key_bias()

        def block(dil, r, b, step, last):
            start = r + pl.multiple_of(step * b, step)
            rows = pl.ds(start, ATTN_BLK, stride=dil) if dil > 1 else pl.ds(start, ATTN_BLK)
            keys = (pl.ds(TOK + start - step, 2 * ATTN_BLK, stride=dil) if dil > 1
                    else pl.ds(TOK + start - step, 2 * ATTN_BLK))
            bias = jnp.where((n == 0) & (b == 0), bias_first, bias_any)
            out, lse = _pair_fwd(qs[rows, :], k2[keys, :].astype(bf16), v2[keys, :].astype(bf16), bias)
            if dil < DILATIONS[-1]:
                out, lse = _merge(ay[rows, :], al[rows, :], out, lse)
            if last:
                y_ref[rows, :] = out
                l_ref[rows, :] = lse
            else:
                ay[rows, :] = out
                al[rows, :] = lse

        for dil in reversed(DILATIONS):
            def loop(i, carry, dil=dil):
                block(dil, i % dil, i // dil, ATTN_BLK * dil, dil == 1)
                return carry
            lax.fori_loop(0, TOK // ATTN_BLK, loop, 0, unroll=4)

    blk = (TOK, 128)
    cur = lambda c: pl.BlockSpec(blk, lambda p, n: (n, 4 * c + p))
    prv = lambda c: pl.BlockSpec(blk, lambda p, n: (jnp.maximum(n - 1, 0), 4 * c + p))
    out = pl.BlockSpec(blk, lambda p, n: (n, p))
    call = dict(in_specs=[cur(0), prv(1), cur(1), prv(2), cur(2)], out_specs=[out, out],
                out_shape=[jax.ShapeDtypeStruct((S, ATTN_W), f32)] * 2,
                scratch_shapes=[pltpu.VMEM(blk, f32), pltpu.VMEM((2 * TOK, 128), f32), pltpu.VMEM((2 * TOK, 128), f32),
                                pltpu.VMEM(blk, f32), pltpu.VMEM(blk, f32)])
    call, body, more = _ride(call, rider, body, lambda: pl.program_id(0) * nS + pl.program_id(1), (ATTN_W // 128) * nS, 5, 2, 5)
    return pl.pallas_call(body, name="attention_fwd", grid=(ATTN_W // 128, nS), compiler_params=_cp("arbitrary", "arbitrary"),
                          **call)(qkv, qkv, qkv, qkv, qkv, *more)


def _attn_bwd(qkv, ya, lse, dmix, rider=None):
    S = qkv.shape[0]
    nS = S // TOK

    def body(q_ref, kp_ref, kc_ref, vp_ref, vc_ref, y_ref, l_ref, dy_ref, dq_ref, dk_ref, dv_ref, qs, k2, v2, dk2, dv2, dqa, dl):
        n = pl.program_id(1)

        @pl.when(n == 0)
        def _():
            dk2[...] = jnp.zeros_like(dk2)
            dv2[...] = jnp.zeros_like(dv2)

        @pl.when(n < nS)
        def _():
            qs[...] = q_ref[...].astype(f32)
            k2[0:TOK] = kp_ref[...].astype(f32)
            k2[TOK:2 * TOK] = kc_ref[...].astype(f32)
            v2[0:TOK] = vp_ref[...].astype(f32)
            v2[TOK:2 * TOK] = vc_ref[...].astype(f32)
            li = lax.broadcasted_iota(jnp.int32, (128, 128), 0)
            lj = lax.broadcasted_iota(jnp.int32, (128, 128), 1)
            seg = jnp.where((li // HEAD_DIM) == (lj // HEAD_DIM), 1.0, 0.0).astype(bf16)
            bias_any, bias_first = _key_bias()

            def delta_rows(t, carry):
                rows = pl.ds(pl.multiple_of(256 * t, 256), 256)
                dyy = dy_ref[rows, :] * y_ref[rows, :]
                hi = dyy.astype(bf16)
                dl[rows, :] = _dot(hi, seg) + _dot((dyy - hi.astype(f32)).astype(bf16), seg)
                return carry

            lax.fori_loop(0, TOK // 256, delta_rows, 0)

            def block(dil, r, b, step, first_pattern, last):
                start = r + pl.multiple_of(step * b, step)
                rows = pl.ds(start, ATTN_BLK, stride=dil) if dil > 1 else pl.ds(start, ATTN_BLK)
                keys = (pl.ds(TOK + start - step, 2 * ATTN_BLK, stride=dil) if dil > 1
                        else pl.ds(TOK + start - step, 2 * ATTN_BLK))
                bias = jnp.where((n == 0) & (b == 0), bias_first, bias_any)
                dq2, dkk, dvv = _pair_bwd(qs[rows, :], k2[keys, :], v2[keys, :].astype(bf16), dy_ref[rows, :],
                                          l_ref[rows, :], dl[rows, :], bias)
                if last:
                    dq_ref[rows, :] = dqa[rows, :] + dq2
                elif first_pattern:
                    dqa[rows, :] = dq2
                else:
                    dqa[rows, :] += dq2
                dk2[keys, :] += dkk
                dv2[keys, :] += dvv

            for dil in reversed(DILATIONS):
                def loop(i, carry, dil=dil):
                    block(dil, i % dil, i // dil, ATTN_BLK * dil, dil == DILATIONS[-1], dil == 1)
                    return carry
                lax.fori_loop(0, TOK // ATTN_BLK, loop, 0, unroll=4)

        dk_ref[...] = dk2[0:TOK]
        dv_ref[...] = dv2[0:TOK]
        dk2[0:TOK] = dk2[TOK:2 * TOK]
        dv2[0:TOK] = dv2[TOK:2 * TOK]
        dk2[TOK:2 * TOK] = jnp.zeros((TOK, 128), f32)
        dv2[TOK:2 * TOK] = jnp.zeros((TOK, 128), f32)

    blk = (TOK, 128)
    cn = lambda n: jnp.minimum(n, nS - 1)
    pn = lambda n: jnp.clip(n - 1, 0, nS - 1)
    cur = lambda c: pl.BlockSpec(blk, lambda p, n: (cn(n), 4 * c + p))
    prv = lambda c: pl.BlockSpec(blk, lambda p, n: (pn(n), 4 * c + p))
    at_n = pl.BlockSpec(blk, lambda p, n: (cn(n), p))
    at_p = pl.BlockSpec(blk, lambda p, n: (pn(n), p))
    big = lambda: pltpu.VMEM((2 * TOK, 128), f32)
    call = dict(in_specs=[cur(0), prv(1), cur(1), prv(2), cur(2), at_n, at_n, at_n], out_specs=[at_n, at_p, at_p],
                out_shape=[jax.ShapeDtypeStruct((S, ATTN_W), f32)] * 3,
                scratch_shapes=[pltpu.VMEM(blk, f32), big(), big(), big(), big(), pltpu.VMEM(blk, f32), pltpu.VMEM(blk, f32)])
    call, body, more = _ride(call, rider, body, lambda: pl.program_id(0) * (nS + 1) + pl.program_id(1),
                             (ATTN_W // 128) * (nS + 1), 8, 3, 7)
    return pl.pallas_call(body, name="attention_bwd", grid=(ATTN_W // 128, nS + 1), compiler_params=_cp("arbitrary", "arbitrary"),
                          **call)(qkv, qkv, qkv, qkv, qkv, ya, lse, dmix, *more)


HG_T = 256
N_HH = HGRN_W // HGRN_HD
HG_SUB = 128
SAFE_RANGE = 75.0


def _row_in_chunk():
    return lax.broadcasted_iota(jnp.int32, (HG_T, HGRN_HD), 0) % CHUNK


def _chunk_cumsum(v, rc):
    k = 1
    while k < CHUNK:
        v = v + jnp.where(rc >= k, pltpu.roll(v, k, 0), 0.0)
        k *= 2
    return v


def _chunk_rcumsum(v, rc):
    k = 1
    while k < CHUNK:
        v = v + jnp.where(rc < CHUNK - k, pltpu.roll(v, HG_T - k, 0), 0.0)
        k *= 2
    return v


def _hgrn_gates(qb, fb, lb):
    sf = _sigmoid(fb)
    f = lb + (1.0 - lb) * sf
    sq = _sigmoid(qb)
    return sf, f, jnp.log(f), 1.0 - f, sq, qb * sq


def _hgrn_prep(qb, fb, lbl2, rc):
    lb = _sigmoid(lbl2[0:1, :] - lbl2[1:2, :])
    sf, f, lf, key, sq, qf = _hgrn_gates(qb, fb, lb)
    b = _chunk_cumsum(lf, rc)
    rem = _chunk_rcumsum(lf, rc) - lf
    return dict(lb=lb, sf=sf, f=f, key=key, sq=sq, qf=qf, b=b, rem=rem, eb=jnp.exp(b), er=jnp.exp(rem))


def _chunk_mask():
    r = lax.broadcasted_iota(jnp.int32, (HG_SUB, HG_SUB), 0)
    c = lax.broadcasted_iota(jnp.int32, (HG_SUB, HG_SUB), 1)
    return ((r // CHUNK) == (c // CHUNK)) & (c <= r)


def _hgrn_fwd(hp, lbl, wn):
    S = hp.shape[0]
    nT = S // HG_T

    def body(qb_ref, fb_ref, ib_ref, gb_ref, lbl_ref, wn_ref, yb_ref, o_ref, st_ref, ST, qt_s, kh_s, dec_s, oi_s):
        @pl.when(pl.program_id(0) == 0)
        def _():
            ST[...] = jnp.zeros_like(ST)

        rc = _row_in_chunk()
        for h in range(N_HH):
            sl = slice(HGRN_HD * h, HGRN_HD * (h + 1))
            p = _hgrn_prep(qb_ref[:, sl], fb_ref[:, sl], lbl_ref[:, sl], rc)
            qf, key, b = p["qf"], p["key"], p["b"]
            qt = qf * p["eb"]
            qt_s[:, sl] = qt.astype(bf16)
            kh_s[:, sl] = (key * p["er"]).astype(bf16)
            dec_s[:, sl] = jnp.exp(b + p["rem"])
            rng = jnp.max(-(b + p["rem"]))

            @pl.when(rng < SAFE_RANGE)
            def _():
                kp = (key * jnp.exp(-b)).astype(bf16)
                cmask = _chunk_mask()
                for j in range(HG_T // HG_SUB):
                    rs = slice(HG_SUB * j, HG_SUB * (j + 1))
                    sc = jnp.where(cmask, _dot_nt(qt[rs].astype(bf16), kp[rs]), 0.0).astype(bf16)
                    oi_s[rs, sl] = _dot(sc, ib_ref[rs, sl].astype(bf16))

            @pl.when(rng >= SAFE_RANGE)
            def _():
                v = ib_ref[:, sl]
                ones = jnp.ones((HGRN_HD, HGRN_HD), bf16)

                def lag(l, o):
                    e = jnp.exp(jnp.where(rc >= l, b - pltpu.roll(b, l, 0), NEG))
                    pr = qf * pltpu.roll(key, l, 0) * e
                    return o + _dot(pr.astype(bf16), ones) * pltpu.roll(v, l, 0)

                oi_s[:, sl] = lax.fori_loop(1, CHUNK, lag, _dot((qf * key).astype(bf16), ones) * v)

        def step(c, carry):
            rows = pl.ds(pl.multiple_of(c * CHUNK, CHUNK), CHUNK)
            row0 = pl.ds(pl.multiple_of(c * CHUNK, CHUNK), 1)
            for h in range(N_HH):
                sl = slice(HGRN_HD * h, HGRN_HD * (h + 1))
                stv = ST[h]
                st_ref[c, sl, :] = stv
                oi_s[rows, sl] += _dot_nt(qt_s[rows, sl], stv.astype(bf16))
                ST[h] = stv * dec_s[row0, sl] + _dot_tn(ib_ref[rows, sl].astype(bf16), kh_s[rows, sl])
            return carry

        lax.fori_loop(0, HG_T // CHUNK, step, 0, unroll=True)

        for h in range(N_HH):
            sl = slice(HGRN_HD * h, HGRN_HD * (h + 1))
            o = oi_s[:, sl]
            o_ref[:, sl] = o
            on = o * lax.rsqrt(jnp.mean(o * o, axis=-1, keepdims=True) + EPS)
            g = gb_ref[:, sl]
            yb_ref[:, sl] = on * wn_ref[:, sl] * (g * _sigmoid(g))

    col = lambda c: pl.BlockSpec((HG_T, HGRN_W), lambda i: (i, c))
    tile = pl.BlockSpec((HG_T, HGRN_W), lambda i: (i, 0))
    whole = lambda a: pl.BlockSpec(a.shape, lambda i: (0, 0))
    return pl.pallas_call(
        body, name="hgrn_fwd", grid=(nT,),
        in_specs=[col(0), col(1), col(2), col(3), whole(lbl), whole(wn)],
        out_specs=[tile, tile, pl.BlockSpec((HG_T // CHUNK, HGRN_W, HGRN_HD), lambda i: (i, 0, 0))],
        out_shape=[jax.ShapeDtypeStruct((S, HGRN_W), f32), jax.ShapeDtypeStruct((S, HGRN_W), f32),
                   jax.ShapeDtypeStruct((S // CHUNK, HGRN_W, HGRN_HD), f32)],
        scratch_shapes=[pltpu.VMEM((N_HH, HGRN_HD, HGRN_HD), f32), pltpu.VMEM((HG_T, HGRN_W), bf16),
                        pltpu.VMEM((HG_T, HGRN_W), bf16), pltpu.VMEM((HG_T, HGRN_W), f32), pltpu.VMEM((HG_T, HGRN_W), f32)],
        compiler_params=_cp("arbitrary"),
    )(hp, hp, hp, hp, lbl, wn)


def _hgrn_bwd(hp, lbl, wn, o_sav, states, dmix, rider=None):
    S = hp.shape[0]
    nT = S // HG_T

    def body(qb_ref, fb_ref, ib_ref, gb_ref, lbl_ref, wn_ref, o_ref, st_ref, dy_ref,
             dq_ref, df_ref, di_ref, dg_ref, gwn_ref, glb_ref,
             DST, qt_s, kh_s, dec_s, do_s, dqt_s, dkh_s, dbl_s, dvi_s, dqi_s, dki_s, dbi_s):
        @pl.when(pl.program_id(0) == 0)
        def _():
            DST[...] = jnp.zeros_like(DST)
            gwn_ref[...] = jnp.zeros_like(gwn_ref)
            glb_ref[...] = jnp.zeros_like(glb_ref)

        rc = _row_in_chunk()
        preps = []
        for h in range(N_HH):
            sl = slice(HGRN_HD * h, HGRN_HD * (h + 1))
            p = _hgrn_prep(qb_ref[:, sl], fb_ref[:, sl], lbl_ref[:, sl], rc)
            preps.append(p)
            qf, key, b = p["qf"], p["key"], p["b"]
            v = ib_ref[:, sl]
            o = o_ref[:, sl]
            rinv = lax.rsqrt(jnp.mean(o * o, axis=-1, keepdims=True) + EPS)
            on = o * rinv
            g = gb_ref[:, sl]
            sgm = _sigmoid(g)
            silu_g = g * sgm
            dy = dy_ref[:, sl]
            wn_v = wn_ref[:, sl]
            gwn_ref[:, sl] += jnp.sum(dy * on * silu_g, axis=0, keepdims=True)
            dg_ref[:, sl] = (dy * on * wn_v * (sgm * (1.0 + g * (1.0 - sgm)))).astype(bf16)
            t1 = dy * wn_v * silu_g
            do = rinv * (t1 - on * jnp.mean(t1 * on, axis=-1, keepdims=True))
            do_s[:, sl] = do.astype(bf16)
            qt = qf * p["eb"]
            qt_s[:, sl] = qt.astype(bf16)
            kh_s[:, sl] = (key * p["er"]).astype(bf16)
            dec_s[:, sl] = jnp.exp(b + p["rem"])
            rng = jnp.max(-(b + p["rem"]))

            @pl.when(rng < SAFE_RANGE)
            def _():
                einv = jnp.exp(-b)
                kp = (key * einv).astype(bf16)
                cmask = _chunk_mask()
                for j in range(HG_T // HG_SUB):
                    rs = slice(HG_SUB * j, HG_SUB * (j + 1))
                    qtb, dob, vb = qt[rs].astype(bf16), do[rs].astype(bf16), v[rs].astype(bf16)
                    sc = jnp.where(cmask, _dot_nt(qtb, kp[rs]), 0.0).astype(bf16)
                    dsc = jnp.where(cmask, _dot_nt(dob, vb), 0.0).astype(bf16)
                    dqp = _dot(dsc, kp[rs])
                    dkp = _dot_tn(dsc, qtb)
                    dvi_s[rs, sl] = _dot_tn(sc, dob)
                    dqi_s[rs, sl] = dqp * p["eb"][rs]
                    dki_s[rs, sl] = dkp * einv[rs]
                    dbi_s[rs, sl] = dqp * qtb.astype(f32) - dkp * kp[rs].astype(f32)

            @pl.when(rng >= SAFE_RANGE)
            def _():
                ones = jnp.ones((HGRN_HD, HGRN_HD), bf16)

                def lag(l, carry):
                    dqf, dkey, db, dv = carry
                    e = jnp.exp(jnp.where(rc >= l, b - pltpu.roll(b, l, 0), NEG))
                    ks, vs, qe = pltpu.roll(key, l, 0), pltpu.roll(v, l, 0), qf * e
                    pr = qe * ks
                    rl = _dot(pr.astype(bf16), ones)
                    drl = jnp.where(rc >= l, _dot((do * vs).astype(bf16), ones), 0.0)
                    gl = drl * pr
                    back = HG_T - l
                    return (dqf + drl * ks * e, dkey + pltpu.roll(drl * qe, back, 0), db + gl - pltpu.roll(gl, back, 0),
                            dv + pltpu.roll(rl * do, back, 0))

                rl0 = _dot((qf * key).astype(bf16), ones)
                drl0 = _dot((do * v).astype(bf16), ones)
                dqf, dkey, db, dv = lax.fori_loop(1, CHUNK, lag, (drl0 * key, drl0 * qf, jnp.zeros((HG_T, HGRN_HD), f32), rl0 * do))
                dvi_s[:, sl] = dv
                dqi_s[:, sl] = dqf
                dki_s[:, sl] = dkey
                dbi_s[:, sl] = db

        def step(k, carry):
            c = HG_T // CHUNK - 1 - k
            rows = pl.ds(pl.multiple_of(c * CHUNK, CHUNK), CHUNK)
            row0 = pl.ds(pl.multiple_of(c * CHUNK, CHUNK), 1)
            for h in range(N_HH):
                sl = slice(HGRN_HD * h, HGRN_HD * (h + 1))
                stp = st_ref[c, sl, :]
                dst = DST[h]
                dstb = dst.astype(bf16)
                dob = do_s[rows, sl]
                khb = kh_s[rows, sl]
                dec = dec_s[row0, sl]
                dqt_s[rows, sl] = _dot(dob, stp.astype(bf16))
                dkh = _dot(ib_ref[rows, sl].astype(bf16), dstb)
                dkh_s[rows, sl] = dkh
                dvi_s[rows, sl] += _dot_nt(khb, dstb)
                dbl = jnp.sum(dst * stp, axis=0, keepdims=True) * dec + jnp.sum(dkh * khb.astype(f32), axis=0, keepdims=True)
                dbl_s[rows, sl] = jnp.broadcast_to(dbl, (CHUNK, HGRN_HD))
                DST[h] = dst * dec + _dot_tn(dob, qt_s[rows, sl])
            return carry

        lax.fori_loop(0, HG_T // CHUNK, step, 0, unroll=True)

        for h in range(N_HH):
            sl = slice(HGRN_HD * h, HGRN_HD * (h + 1))
            qb = qb_ref[:, sl]
            p = preps[h]
            sf, sq, lb = p["sf"], p["sq"], p["lb"]
            dqt, dkh = dqt_s[:, sl], dkh_s[:, sl]
            dqf = dqt * p["eb"] + dqi_s[:, sl]
            dkey = dkh * p["er"] + dki_s[:, sl]
            db = dqt * (p["qf"] * p["eb"]) - dkh * (p["key"] * p["er"]) + jnp.where(rc == CHUNK - 1, dbl_s[:, sl], 0.0) + dbi_s[:, sl]
            df = _chunk_rcumsum(db, rc) / p["f"] - dkey
            df_ref[:, sl] = (df * (1.0 - lb) * sf * (1.0 - sf)).astype(bf16)
            glb_ref[:, sl] += jnp.sum(df * (1.0 - sf), axis=0, keepdims=True)
            dq_ref[:, sl] = (dqf * (sq * (1.0 + qb * (1.0 - sq)))).astype(bf16)
            di_ref[:, sl] = dvi_s[:, sl].astype(bf16)

    rev = lambda i: nT - 1 - i
    col = lambda c: pl.BlockSpec((HG_T, HGRN_W), lambda i: (rev(i), c))
    tile = pl.BlockSpec((HG_T, HGRN_W), lambda i: (rev(i), 0))
    whole = lambda a: pl.BlockSpec(a.shape, lambda i: (0, 0))
    vec = pl.BlockSpec((1, HGRN_W), lambda i: (0, 0))
    tb = lambda: pltpu.VMEM((HG_T, HGRN_W), bf16)
    tf = lambda: pltpu.VMEM((HG_T, HGRN_W), f32)
    call = dict(in_specs=[col(0), col(1), col(2), col(3), whole(lbl), whole(wn), tile,
                          pl.BlockSpec((HG_T // CHUNK, HGRN_W, HGRN_HD), lambda i: (rev(i), 0, 0)),
                          pl.BlockSpec((HG_T, HGRN_W), lambda i: (rev(i), 1))],
                out_specs=[tile, tile, tile, tile, vec, vec],
                out_shape=[jax.ShapeDtypeStruct((S, HGRN_W), bf16)] * 4 + [jax.ShapeDtypeStruct((1, HGRN_W), f32)] * 2,
                scratch_shapes=[pltpu.VMEM((N_HH, HGRN_HD, HGRN_HD), f32), tb(), tb(), tf(), tb(), tf(), tf(), tf(), tf(), tf(),
                                tf(), tf()])
    call, body, more = _ride(call, rider, body, lambda: pl.program_id(0), nT, 9, 6, 12)
    return pl.pallas_call(body, name="hgrn_bwd", grid=(nT,), compiler_params=_cp("arbitrary"), **call)(
        hp, hp, hp, hp, lbl, wn, o_sav, states, dmix, *more)


def _out_proj(x, ya, yb, wout, w2):
    S = x.shape[0]
    tm = 512

    def body(x_ref, ya_ref, yb_ref, w_ref, w2_ref, h1_ref, u2_ref, mix_ref):
        mixed = jnp.concatenate([ya_ref[...], yb_ref[...]], axis=1).astype(bf16)
        mix_ref[...] = mixed
        h1 = x_ref[...] + _dot(mixed, w_ref[...])
        h1_ref[...] = h1
        r = lax.rsqrt(jnp.mean(h1 * h1, axis=-1, keepdims=True) + EPS)
        u2_ref[...] = (h1 * r * w2_ref[...]).astype(bf16)

    row = lambda w: pl.BlockSpec((tm, w), lambda i: (i, 0))
    return pl.pallas_call(
        body, name="out_proj", grid=(S // tm,),
        in_specs=[row(D_MODEL), row(ATTN_W), row(HGRN_W), pl.BlockSpec((D_MODEL, D_MODEL), lambda i: (0, 0)),
                  pl.BlockSpec((1, D_MODEL), lambda i: (0, 0))],
        out_specs=[row(D_MODEL), row(D_MODEL), row(D_MODEL)],
        out_shape=[jax.ShapeDtypeStruct((S, D_MODEL), f32), jax.ShapeDtypeStruct((S, D_MODEL), bf16),
                   jax.ShapeDtypeStruct((S, D_MODEL), bf16)],
        compiler_params=_cp("arbitrary"),
    )(x, ya, yb, wout, w2)


def _gate_up(u2, wgu_g):
    S = u2.shape[0]
    w = 2 * FFN // N_DEV
    tm, tn = 512, 2 * w
    nj = FFN // tn

    def body(u_ref, wgg_ref, wug_ref, g_ref, up_ref, a_ref, wg_ref, wu_ref):
        @pl.when(pl.program_id(1) == 0)
        def _():
            for k in range(2):
                wg_ref[:, w * k:w * (k + 1)] = wgg_ref[k]
                wu_ref[:, w * k:w * (k + 1)] = wug_ref[k]

        u = u_ref[...]
        g = _dot(u, wg_ref[...])
        up = _dot(u, wu_ref[...])
        g_ref[...] = g.astype(bf16)
        up_ref[...] = up.astype(bf16)
        a_ref[...] = (g * _sigmoid(g) * up).astype(bf16)

    out = pl.BlockSpec((tm, tn), lambda j, i: (i, j))
    wout = pl.BlockSpec((D_MODEL, tn), lambda j, i: (0, j))
    return pl.pallas_call(
        body, name="gate_up", grid=(nj, S // tm),
        in_specs=[pl.BlockSpec((tm, D_MODEL), lambda j, i: (i, 0)), pl.BlockSpec((2, D_MODEL, w), lambda j, i: (j, 0, 0)),
                  pl.BlockSpec((2, D_MODEL, w), lambda j, i: (j + nj, 0, 0))],
        out_specs=[out, out, out, wout, wout],
        out_shape=[jax.ShapeDtypeStruct((S, FFN), bf16)] * 3 + [jax.ShapeDtypeStruct((D_MODEL, FFN), bf16)] * 2,
        compiler_params=_cp("arbitrary", "arbitrary"),
    )(u2, wgu_g, wgu_g)


def _rms_bwd(dyw, hn, r):
    return r * (dyw - hn * jnp.mean(dyw * hn, axis=-1, keepdims=True))


def _down_loss(act, wdown, h1, tgt, w3):
    S = act.shape[0]
    tm = 256

    def body(a_ref, w_ref, h1_ref, t_ref, w3_ref, dh2_ref, loss_ref, gw3_ref):
        @pl.when(pl.program_id(0) == 0)
        def _():
            loss_ref[...] = jnp.zeros_like(loss_ref)
            gw3_ref[...] = jnp.zeros_like(gw3_ref)

        h2 = h1_ref[...] + _dot(a_ref[...], w_ref[...])
        r = lax.rsqrt(jnp.mean(h2 * h2, axis=-1, keepdims=True) + EPS)
        hn = h2 * r
        w3 = w3_ref[...]
        err = hn * w3 - t_ref[...]
        loss_ref[...] += (0.5 / D_MODEL) * jnp.sum(err * err)
        dy = err * (1.0 / D_MODEL)
        gw3_ref[...] += jnp.sum(dy * hn, axis=0, keepdims=True)
        dh2_ref[...] = _rms_bwd(dy * w3, hn, r)

    row = lambda w: pl.BlockSpec((tm, w), lambda i: (i, 0))
    return pl.pallas_call(
        body, name="down_loss", grid=(S // tm,),
        in_specs=[row(FFN), pl.BlockSpec((FFN, D_MODEL), lambda i: (0, 0)), row(D_MODEL), row(D_MODEL),
                  pl.BlockSpec((1, D_MODEL), lambda i: (0, 0))],
        out_specs=[row(D_MODEL), pl.BlockSpec((1, 128), lambda i: (0, 0)), pl.BlockSpec((1, D_MODEL), lambda i: (0, 0))],
        out_shape=[jax.ShapeDtypeStruct((S, D_MODEL), f32), jax.ShapeDtypeStruct((1, 128), f32),
                   jax.ShapeDtypeStruct((1, D_MODEL), f32)],
        compiler_params=_cp("arbitrary"),
    )(act, wdown, h1, tgt, w3)


def _dact(dh2, wdown, gate, up):
    S = dh2.shape[0]
    tm = 256

    def body(d_ref, w_ref, g_ref, u_ref, o_ref):
        da = _dot_nt(d_ref[...].astype(bf16), w_ref[...])
        g = g_ref[...].astype(f32)
        sg = _sigmoid(g)
        o_ref[1] = (da * g * sg).astype(bf16)
        o_ref[0] = (da * u_ref[...].astype(f32) * (sg * (1.0 + g * (1.0 - sg)))).astype(bf16)

    row = lambda w: pl.BlockSpec((tm, w), lambda i: (i, 0))
    return pl.pallas_call(
        body, name="dact", grid=(S // tm,),
        in_specs=[row(D_MODEL), pl.BlockSpec((FFN, D_MODEL), lambda i: (0, 0)), row(FFN), row(FFN)],
        out_specs=pl.BlockSpec((2, tm, FFN), lambda i: (0, i, 0)),
        out_shape=jax.ShapeDtypeStruct((2, S, FFN), bf16),
        compiler_params=_cp("arbitrary"),
    )(dh2, wdown, gate, up)


def _dgu(dgu2, wgate, wup, h1, w2, dh2, wout, rider=None):
    S = dgu2.shape[1]
    tm = 256

    def body(d_ref, wg_ref, wu_ref, h1_ref, w2_ref, dh2_ref, wo_ref, dh1_ref, gw2_ref, dmix_ref):
        @pl.when(pl.program_id(0) == 0)
        def _():
            gw2_ref[...] = jnp.zeros_like(gw2_ref)

        du2 = _dot_nt(d_ref[0], wg_ref[...]) + _dot_nt(d_ref[1], wu_ref[...])
        h1 = h1_ref[...]
        r = lax.rsqrt(jnp.mean(h1 * h1, axis=-1, keepdims=True) + EPS)
        hn = h1 * r
        gw2_ref[...] += jnp.sum(du2 * hn, axis=0, keepdims=True)
        dh1 = dh2_ref[...] + _rms_bwd(du2 * w2_ref[...], hn, r)
        dh1_ref[...] = dh1
        dmix_ref[...] = _dot_nt(dh1.astype(bf16), wo_ref[...])

    row = lambda w: pl.BlockSpec((tm, w), lambda i: (i, 0))
    call = dict(in_specs=[pl.BlockSpec((2, tm, FFN), lambda i: (0, i, 0)), pl.BlockSpec((D_MODEL, FFN), lambda i: (0, 0)),
                          pl.BlockSpec((D_MODEL, FFN), lambda i: (0, 0)), row(D_MODEL),
                          pl.BlockSpec((1, D_MODEL), lambda i: (0, 0)), row(D_MODEL),
                          pl.BlockSpec((D_MODEL, D_MODEL), lambda i: (0, 0))],
                out_specs=[row(D_MODEL), pl.BlockSpec((1, D_MODEL), lambda i: (0, 0)), row(D_MODEL)],
                out_shape=[jax.ShapeDtypeStruct((S, D_MODEL), f32), jax.ShapeDtypeStruct((1, D_MODEL), f32),
                           jax.ShapeDtypeStruct((S, D_MODEL), f32)], scratch_shapes=[])
    call, body, more = _ride(call, rider, body, lambda: pl.program_id(0), S // tm, 7, 3, 0)
    return pl.pallas_call(body, name="dgu", grid=(S // tm,), compiler_params=_cp("arbitrary"), **call)(
        dgu2, wgate, wup, h1, w2, dh2, wout, *more)


def _din(dq, dk, dv, dhq, dhf, dhi, dhg, cos_t, sg_t, win, x, w1, dh1):
    S = x.shape[0]
    tm = 256

    def body(dq_ref, dk_ref, dv_ref, dhq_ref, dhf_ref, dhi_ref, dhg_ref, cos_ref, sg_ref, w_ref, x_ref, w1_ref, dh1_ref,
             dp_ref, gx_ref, gw1_ref):
        @pl.when(pl.program_id(0) == 0)
        def _():
            gw1_ref[...] = jnp.zeros_like(gw1_ref)

        cosv, sgv = jnp.tile(cos_ref[...], (1, ATTN_W // 128)), jnp.tile(sg_ref[...], (1, ATTN_W // 128))
        unrope = lambda d: d * cosv - sgv * _swap_halves(d)
        parts = [(unrope(dq_ref[...]) * (HEAD_DIM ** -0.5)).astype(bf16), unrope(dk_ref[...]).astype(bf16),
                 dv_ref[...].astype(bf16), dhq_ref[...], dhf_ref[...], dhi_ref[...], dhg_ref[...]]
        du = jnp.zeros((tm, D_MODEL), f32)
        for j, pj in enumerate(parts):
            dp_ref[:, j * 512:(j + 1) * 512] = pj
            du = du + _dot_nt(pj, w_ref[:, j * 512:(j + 1) * 512])
        xv = x_ref[...]
        r = lax.rsqrt(jnp.mean(xv * xv, axis=-1, keepdims=True) + EPS)
        xn = xv * r
        gw1_ref[...] += jnp.sum(du * xn, axis=0, keepdims=True)
        gx_ref[...] = dh1_ref[...] + _rms_bwd(du * w1_ref[...], xn, r)

    row = lambda w: pl.BlockSpec((tm, w), lambda i: (i, 0))
    vec = pl.BlockSpec((1, D_MODEL), lambda i: (0, 0))
    return pl.pallas_call(
        body, name="din", grid=(S // tm,),
        in_specs=[row(512)] * 7 + [row(128), row(128), pl.BlockSpec((D_MODEL, IN_W), lambda i: (0, 0)), row(D_MODEL), vec,
                                   row(D_MODEL)],
        out_specs=[row(IN_W), row(D_MODEL), vec],
        out_shape=[jax.ShapeDtypeStruct((S, IN_W), bf16), jax.ShapeDtypeStruct((S, D_MODEL), f32),
                   jax.ShapeDtypeStruct((1, D_MODEL), f32)],
        compiler_params=_cp("arbitrary"),
    )(dq, dk, dv, dhq, dhf, dhi, dhg, cos_t, sg_t, win, x, w1, dh1)


def _gw(a, bs, tn, name, ts=2048):
    S, M = a.shape
    N = bs[0].shape[1]
    k = len(bs)

    def body(a_ref, *refs):
        @pl.when(pl.program_id(1) == 0)
        def _():
            for o_ref in refs[k:]:
                o_ref[...] = jnp.zeros_like(o_ref)

        at = a_ref[...].astype(bf16)
        for b_ref, o_ref in zip(refs[:k], refs[k:]):
            o_ref[...] += _dot_tn(at, b_ref[...].astype(bf16))

    return pl.pallas_call(
        body, name=name, grid=(N // tn, S // ts),
        in_specs=[pl.BlockSpec((ts, M), lambda j, s: (s, 0))] + [pl.BlockSpec((ts, tn), lambda j, s: (s, j))] * k,
        out_specs=[pl.BlockSpec((M, tn), lambda j, s: (0, j))] * k, out_shape=[jax.ShapeDtypeStruct((M, N), f32)] * k,
        compiler_params=_cp("arbitrary", "arbitrary"),
    )(a, *bs)


def _gw_by_owner(a, b3, w, name, ts):
    S, M = a.shape
    G, _, Ng = b3.shape
    tn = 2 * w
    per_group = Ng // tn
    n_s = S // ts

    def body(a_ref, b_ref, o_ref, acc):
        s = pl.program_id(1)

        @pl.when(s == 0)
        def _():
            acc[...] = jnp.zeros_like(acc)

        acc[...] += _dot_tn(a_ref[...].astype(bf16), b_ref[0].astype(bf16))

        @pl.when(s == n_s - 1)
        def _():
            o_ref[0] = acc[:, 0:w]
            o_ref[1] = acc[:, w:tn]

    return pl.pallas_call(
        body, name=name, grid=(G * per_group, n_s),
        in_specs=[pl.BlockSpec((ts, M), lambda j, s: (s, 0)),
                  pl.BlockSpec((1, ts, tn), lambda j, s: (j // per_group, s, j % per_group))],
        out_specs=pl.BlockSpec((2, M, w), lambda j, s: (j, 0, 0)), out_shape=jax.ShapeDtypeStruct((G * Ng // w, M, w), f32),
        scratch_shapes=[pltpu.VMEM((M, tn), f32)], compiler_params=_cp("arbitrary", "arbitrary"),
    )(a, b3)


MESH = pl.DeviceIdType.MESH
ANY = pl.BlockSpec(memory_space=pl.ANY)
VMEM_SPEC = pl.BlockSpec(memory_space=pltpu.VMEM)


def _pos():
    return lax.axis_index("x"), lax.axis_index("y"), lax.axis_index("c")


def _flip(v, bit):
    return 1 - v if bit else v


def _gather_rider(shards):
    n = len(shards)

    def parts(outs, scratch):
        send_sems, recv_sems, local_sems = scratch[n:]
        x, y, c = _pos()
        chips = [(1 - x, y), (x, 1 - y), (1 - x, 1 - y)]

        def copy(a, k, block, to, src=None):
            dst = outs[a].at[4 * block[0] + 2 * block[1] + block[2]]
            return pltpu.make_async_remote_copy(src_ref=dst if src is None else src, dst_ref=dst, send_sem=send_sems.at[a, k],
                                                recv_sem=recv_sems.at[a, k], device_id=to, device_id_type=MESH)

        bufs = scratch[:n]
        me, sibling = (x, y, c), (x, y, 1 - c)
        own = lambda a: pltpu.make_async_copy(bufs[a], outs[a].at[4 * x + 2 * y + c], local_sems.at[a])
        sent = lambda a: [copy(a, 0, me, sibling, src=bufs[a])] + [copy(a, 1 + j, me, (*chip, c), src=bufs[a])
                                                                   for j, chip in enumerate(chips)]
        passed = lambda a: [copy(a, 4 + j, (*chip, c), sibling) for j, chip in enumerate(chips)]
        landed = lambda a: [copy(a, 1 + j, (*chip, c), me) for j, chip in enumerate(chips)]
        from_sibling = lambda a: [copy(a, 0, sibling, me)] + [copy(a, 4 + j, (*chip, 1 - c), me) for j, chip in enumerate(chips)]
        return bufs, local_sems, own, sent, passed, landed, from_sibling

    def first(ins, outs, scratch):
        bufs, local_sems, own, sent, _, _, _ = parts(outs, scratch)
        loads = [pltpu.make_async_copy(ins[a], bufs[a], local_sems.at[a]) for a in range(n)]
        for ld in loads:
            ld.start()
        for a in range(n):
            loads[a].wait()
            own(a).start()
            for cp in sent(a):
                cp.start()

    def middle(ins, outs, scratch):
        _, _, _, _, passed, landed, _ = parts(outs, scratch)
        for a in range(n):
            for got, on in zip(landed(a), passed(a)):
                got.wait_recv()
                on.start()

    def last(ins, outs, scratch):
        _, _, own, sent, passed, _, from_sibling = parts(outs, scratch)
        for a in range(n):
            for cp in from_sibling(a):
                cp.wait_recv()
        for a in range(n):
            for cp in sent(a) + passed(a):
                cp.wait_send()
            own(a).wait()

    return _Rider(shards, [jax.ShapeDtypeStruct((N_DEV,) + s.shape, s.dtype) for s in shards],
                  [pltpu.VMEM(s.shape, s.dtype) for s in shards]
                  + [pltpu.SemaphoreType.DMA((n, 7)), pltpu.SemaphoreType.DMA((n, 7)), pltpu.SemaphoreType.DMA((n,))],
                  first, last, middle)


def _sibling_rider(grads):
    n = len(grads)

    def copies(g, got, scratch):
        send_sems, recv_sems = scratch
        x, y, c = _pos()
        return [pltpu.make_async_remote_copy(src_ref=g[a].at[2 * q + (1 - c)], dst_ref=got[a].at[q], send_sem=send_sems.at[a, q],
                                             recv_sem=recv_sems.at[a, q], device_id=(x, y, 1 - c), device_id_type=MESH)
                for a in range(n) for q in range(4)]

    def first(g, got, scratch):
        for cp in copies(g, got, scratch):
            cp.start()

    def last(g, got, scratch):
        for cp in copies(g, got, scratch):
            cp.wait()

    return _Rider(grads, [jax.ShapeDtypeStruct((4,) + g.shape[1:], g.dtype) for g in grads],
                  [pltpu.SemaphoreType.DMA((n, 4))] * 2, first, last)


def _chips_rider(sums):
    n = len(sums)

    def copies(s, out, scratch):
        send_sems, recv_sems = scratch
        x, y, c = _pos()
        cps = []
        for a in range(n):
            for f in (1, 2, 3):
                peer = (_flip(x, f >> 1), _flip(y, f & 1), c)
                cps.append(pltpu.make_async_remote_copy(
                    src_ref=s[a].at[2 * peer[0] + peer[1]], dst_ref=out[a].at[f - 1], send_sem=send_sems.at[a, f - 1],
                    recv_sem=recv_sems.at[a, f - 1], device_id=peer, device_id_type=MESH))
        return cps

    def first(s, out, scratch):
        for cp in copies(s, out, scratch):
            cp.start()

    def last(s, out, scratch):
        for cp in copies(s, out, scratch):
            cp.wait()

    return _Rider(sums, [jax.ShapeDtypeStruct((3,) + s.shape[1:], s.dtype) for s in sums],
                  [pltpu.SemaphoreType.DMA((n, 3))] * 2, first, last)


def _both(a, b):
    na = (len(a.ins), len(a.out_shapes), len(a.scratch))

    def split(fa, fb):
        def f(ins, outs, scratch):
            fa(ins[:na[0]], outs[:na[1]], scratch[:na[2]])
            fb(ins[na[0]:], outs[na[1]:], scratch[na[2]:])
        return f

    return _Rider(a.ins + b.ins, a.out_shapes + b.out_shapes, a.scratch + b.scratch, split(a.first, b.first), split(a.last, b.last))


def _alone(rider, name):
    ri, ro = len(rider.ins), len(rider.out_shapes)

    def body(*refs):
        theirs = (refs[:ri], refs[ri:ri + ro], refs[ri + ro:])
        rider.first(*theirs)
        if rider.middle is not None:
            rider.middle(*theirs)
        rider.last(*theirs)

    return pl.pallas_call(body, name=name, in_specs=[ANY] * ri, out_specs=[ANY] * ro, out_shape=rider.out_shapes,
                          scratch_shapes=rider.scratch)(*rider.ins)


def _gather_small(g_w1, g_w2, g_w3, g_lb, g_wn, loss):
    def body(w1_ref, w2_ref, w3_ref, lb_ref, wn_ref, loss_ref, out_ref, pk, send_sems, recv_sems):
        x, y, c = _pos()
        me = 4 * x + 2 * y + c
        pk[...] = jnp.zeros_like(pk)
        pk[0:1, :] = w1_ref[...]
        pk[1:2, :] = w2_ref[...]
        pk[2:3, :] = w3_ref[...]
        pk[3:4, 0:HGRN_W] = lb_ref[...]
        pk[3:4, HGRN_W:2 * HGRN_W] = wn_ref[...]
        pk[4:5, 0:128] = loss_ref[...]
        out_ref[me] = pk[...]
        sends, recvs = [], []
        for k in range(1, N_DEV):
            peer = (_flip(x, k >> 2), _flip(y, (k >> 1) & 1), _flip(c, k & 1))
            cp = pltpu.make_async_remote_copy(src_ref=pk, dst_ref=out_ref.at[me], send_sem=send_sems.at[k - 1],
                                              recv_sem=recv_sems.at[k - 1], device_id=peer, device_id_type=MESH)
            cp.start()
            sends.append(cp)
            recvs.append(pltpu.make_async_remote_copy(src_ref=pk, dst_ref=out_ref.at[4 * peer[0] + 2 * peer[1] + peer[2]],
                                                      send_sem=send_sems.at[k - 1], recv_sem=recv_sems.at[k - 1], device_id=peer,
                                                      device_id_type=MESH))
        for cp in recvs:
            cp.wait_recv()
        for cp in sends:
            cp.wait_send()

    return pl.pallas_call(
        body, name="gather_small", in_specs=[VMEM_SPEC] * 6, out_specs=VMEM_SPEC,
        out_shape=jax.ShapeDtypeStruct((N_DEV, 8, D_MODEL), f32),
        scratch_shapes=[pltpu.VMEM((8, D_MODEL), f32), pltpu.SemaphoreType.DMA((N_DEV - 1,)), pltpu.SemaphoreType.DMA((N_DEV - 1,))],
    )(g_w1, g_w2, g_w3, g_lb, g_wn, loss)


def _row_tile(r):
    return max(t for t in range(8, 257, 8) if r % t == 0)


def _add_sibling(core, g, got, name):
    _, r, c = got.shape
    tr = _row_tile(r)

    def body(core_ref, a_ref, b_ref, o_ref):
        o_ref[...] = (a_ref[...] + b_ref[...]).astype(bf16)

    blk = pl.BlockSpec((1, tr, c), lambda q, i, core_ref: (q, i, 0))
    return pl.pallas_call(
        body, name=name, out_shape=jax.ShapeDtypeStruct(got.shape, bf16),
        grid_spec=pltpu.PrefetchScalarGridSpec(
            num_scalar_prefetch=1, grid=(4, r // tr),
            in_specs=[pl.BlockSpec((1, tr, c), lambda q, i, core_ref: (2 * q + core_ref[0], i, 0)), blk], out_specs=blk),
        compiler_params=_cp("arbitrary", "arbitrary"))(core, g, got)


def _adamw(w, g, m, v):
    m = ADAM_B1 * m + (1.0 - ADAM_B1) * g
    v = ADAM_B2 * v + (1.0 - ADAM_B2) * (g * g)
    m_hat = m / (1.0 - ADAM_B1 ** ADAM_STEP)
    v_hat = v / (1.0 - ADAM_B2 ** ADAM_STEP)
    return -ADAM_LR * (m_hat / (jnp.sqrt(v_hat) + ADAM_EPS) + ADAM_WD * w), m, v


def _adam_shard(where, g, got, pieces, w, m, v, name):
    r, c = w.shape
    tr = _row_tile(r)

    def body(where_ref, g_ref, got_ref, p_ref, w_ref, m_ref, v_ref, g_out, d_out, m_out, v_out):
        gsum = g_ref[0] + got_ref[0]
        for f in range(3):
            gsum = gsum + p_ref[f].astype(f32)
        g_out[...] = gsum
        d_out[...], m_out[...], v_out[...] = _adamw(w_ref[...], gsum, m_ref[...], v_ref[...])

    blk = pl.BlockSpec((tr, c), lambda i, where_ref: (i, 0))
    return pl.pallas_call(
        body, name=name, out_shape=[jax.ShapeDtypeStruct((r, c), f32)] * 4,
        grid_spec=pltpu.PrefetchScalarGridSpec(
            num_scalar_prefetch=1, grid=(r // tr,),
            in_specs=[pl.BlockSpec((1, tr, c), lambda i, where_ref: (where_ref[0], i, 0)),
                      pl.BlockSpec((1, tr, c), lambda i, where_ref: (where_ref[1], i, 0)),
                      pl.BlockSpec((3, tr, c), lambda i, where_ref: (0, i, 0)), blk, blk, blk],
            out_specs=[blk] * 4),
        compiler_params=_cp("arbitrary"),
    )(where, g, got, pieces, w, m, v)


def _small_update(gath, params):
    def body(gath_ref, *refs):
        ins, outs = refs[:15], refs[15:]
        gs = gath_ref[0]
        for k in range(1, N_DEV):
            gs = gs + gath_ref[k]
        outs[0][...] = gs[4:5, 0:128]
        l0, l1 = ins[9][0:1, :], ins[9][1:2, :]
        lb = _sigmoid(l0 - l1)
        d0 = gs[3:4, 0:HGRN_W] * lb * (1.0 - lb)
        first_row = lax.broadcasted_iota(jnp.int32, (2, HGRN_W), 0) == 0
        grads = [gs[0:1, :], gs[1:2, :], gs[2:3, :], jnp.where(first_row, d0, -d0), gs[3:4, HGRN_W:2 * HGRN_W]]
        for i, g in enumerate(grads):
            w_ref, m_ref, v_ref = ins[3 * i:3 * i + 3]
            o = outs[1 + 4 * i:5 + 4 * i]
            o[0][...] = g
            o[1][...], o[2][...], o[3][...] = _adamw(w_ref[...], g, m_ref[...], v_ref[...])

    flat = [a for p in params for a in p]
    out_shape = [jax.ShapeDtypeStruct((1, 128), f32)] + [jax.ShapeDtypeStruct(p[0].shape, f32) for p in params for _ in range(4)]
    outs = pl.pallas_call(body, name="small_update", in_specs=[VMEM_SPEC] * 16, out_specs=[VMEM_SPEC] * 21, out_shape=out_shape)(gath, *flat)
    return outs[0], [outs[1 + 4 * i:5 + 4 * i] for i in range(5)]


def kernel(x, norm1_w, w_in, lb_logits, hgrn_norm_w, w_out, norm2_w, w_gate_up, w_down, final_norm_w, loss_target, m_norm1_w, m_w_in, m_lb_logits, m_hgrn_norm_w, m_w_out, m_norm2_w, m_w_gate_up, m_w_down, m_final_norm_w, v_norm1_w, v_w_in, v_lb_logits, v_hgrn_norm_w, v_w_out, v_norm2_w, v_w_gate_up, v_w_down, v_final_norm_w):
    row = lambda a: a.reshape(1, D_MODEL)
    ix, iy, ic = lax.axis_index("x"), lax.axis_index("y"), lax.axis_index("c")
    core = jnp.stack([ic]).astype(jnp.int32)
    where = jnp.stack([4 * ix + 2 * iy + ic, 2 * ix + iy]).astype(jnp.int32)
    xs, tgt, w3 = x[0], loss_target[0], row(final_norm_w)
    S = xs.shape[0]

    cos_t, sg_t, win_g = _rope_tables(S, _gather_rider([w_in[0].astype(bf16)]))
    u, qkv, hp, win = _in_proj(xs, norm1_w, win_g, cos_t, sg_t)
    ya, lse, wout_g, wgu_g, wdown_g = _attn_fwd(qkv, _gather_rider([w_out[0].astype(bf16), w_gate_up[0].astype(bf16),
                                                                     w_down[0].astype(bf16)]))
    wout = wout_g.reshape(D_MODEL, D_MODEL)
    wdown = wdown_g.reshape(FFN, D_MODEL)
    yb, o_sav, states = _hgrn_fwd(hp, lb_logits, hgrn_norm_w)
    h1, u2, mixed = _out_proj(xs, ya, yb, wout, norm2_w)
    gate, up, act, wgate, wup = _gate_up(u2, wgu_g)
    dh2, loss_p, g_w3 = _down_loss(act, wdown, h1, tgt, w3)

    (g_wdown,) = _gw(act, [dh2], 512, "gw_down")
    dgu2 = _dact(dh2, wdown, gate, up)
    early = [_gw_by_owner(u2, dgu2, 2 * FFN // N_DEV, "gw_gate_up", 2048), g_wdown.reshape(N_DEV, FFN // N_DEV, D_MODEL)]
    dh1, g_w2, dmix, *got_early = _dgu(dgu2, wgate, wup, h1, norm2_w, dh2, wout, _sibling_rider(early))
    sums_early = [_add_sibling(core, g, o, f"add_sibling_{i}") for i, (g, o) in enumerate(zip(early, got_early))]
    (g_wout,) = _gw(mixed, [dh1], 1024, "gw_out")
    mid = [g_wout.reshape(N_DEV, D_MODEL // N_DEV, D_MODEL)]
    dhq, dhf, dhi, dhg, g_wn, g_lb, *rode = _hgrn_bwd(hp, lb_logits, hgrn_norm_w, o_sav, states, dmix,
                                                      _both(_chips_rider(sums_early), _sibling_rider(mid)))
    pieces_early, got_mid = rode[:2], rode[2:]
    sums_mid = [_add_sibling(core, mid[0], got_mid[0], "add_sibling_2")]
    dq, dk, dv, *pieces_mid = _attn_bwd(qkv, ya, lse, dmix, _chips_rider(sums_mid))
    dproj, gx, g_w1 = _din(dq, dk, dv, dhq, dhf, dhi, dhg, cos_t, sg_t, win, xs, norm1_w, dh1)
    late = [_gw_by_owner(u, dproj[None], IN_W // N_DEV, "gw_in", 2048)]
    got_late = _alone(_sibling_rider(late), "reduce_sibling")
    sums_late = [_add_sibling(core, late[0], got_late[0], "add_sibling_3")]
    pieces_late = _alone(_chips_rider(sums_late), "reduce_chips")

    grads = [late[0], mid[0], early[0], early[1]]
    got = [got_late[0], got_mid[0], got_early[0], got_early[1]]
    pieces = [pieces_late[0], pieces_mid[0], pieces_early[0], pieces_early[1]]
    shards = [w_in[0], w_out[0], w_gate_up[0], w_down[0]]
    moms = [(m_w_in[0], v_w_in[0]), (m_w_out[0], v_w_out[0]), (m_w_gate_up[0], v_w_gate_up[0]), (m_w_down[0], v_w_down[0])]
    big = [_adam_shard(where, g, o, p, w, m, v, f"adam_{i}")
           for i, (g, o, p, w, (m, v)) in enumerate(zip(grads, got, pieces, shards, moms))]
    big = [[a[None] for a in four] for four in big]

    gath = _gather_small(g_w1, g_w2, g_w3, g_lb, g_wn, loss_p)
    params = [(norm1_w, m_norm1_w, v_norm1_w), (norm2_w, m_norm2_w, v_norm2_w),
              (row(final_norm_w), row(m_final_norm_w), row(v_final_norm_w)),
              (lb_logits, m_lb_logits, v_lb_logits), (hgrn_norm_w, m_hgrn_norm_w, v_hgrn_norm_w)]
    loss, (s_w1, s_w2, s_w3, s_lb, s_wn) = _small_update(gath, params)
    s_w3 = [a.reshape(D_MODEL) for a in s_w3]
    per_w = [s_w1, big[0], s_lb, s_wn, big[1], s_w2, big[2], big[3], s_w3]
    return (loss[0, 0], gx[None], *[p[0] for p in per_w], *[p[1] for p in per_w], *[p[2] for p in per_w], *[p[3] for p in per_w])
```

```python
import jax
import jax.numpy as jnp
from jax import lax
from jax.experimental import pallas as pl
from jax.experimental.pallas import tpu as pltpu

f32, bf16 = jnp.float32, jnp.bfloat16

D_MODEL = 1024
ATTN_W = 512
HEAD_DIM = 64
ATTN_BLK = 128
DILATIONS = (1, 4, 16)
HGRN_W = 512
HGRN_HD = 128
CHUNK = 64
IN_W = 3 * ATTN_W + 4 * HGRN_W
FFN = 2816
EPS = 1e-6
ROPE_THETA = 10000.0
NEG = -1e30
N_DEV = 8
ADAM_LR, ADAM_B1, ADAM_B2, ADAM_EPS, ADAM_WD, ADAM_STEP = 0.001, 0.9, 0.999, 1e-08, 0.01, 10
VMEM_LIMIT = 56 * 1024 * 1024


def _cp(*sem):
    return pltpu.CompilerParams(dimension_semantics=sem, vmem_limit_bytes=VMEM_LIMIT)


def _dot(a, b):
    return jnp.dot(a, b, preferred_element_type=f32)


def _dot_nt(a, b):
    return lax.dot_general(a, b, (((1,), (1,)), ((), ())), preferred_element_type=f32)


def _dot_tn(a, b):
    return lax.dot_general(a, b, (((0,), (0,)), ((), ())), preferred_element_type=f32)


def _sigmoid(x):
    return 0.5 * jnp.tanh(0.5 * x) + 0.5


class _Rider:
    def __init__(self, ins, out_shapes, scratch, first, last, middle=None):
        self.ins, self.out_shapes, self.scratch = list(ins), list(out_shapes), list(scratch)
        self.first, self.middle, self.last = first, middle, last


def _ride(call, rider, body, step, n_steps, n_in, n_out, n_scratch):
    if rider is None:
        return call, body, []
    ri, ro = len(rider.ins), len(rider.out_shapes)
    any_spec = pl.BlockSpec(memory_space=pl.ANY)
    call = dict(call, in_specs=call["in_specs"] + [any_spec] * ri, out_specs=call["out_specs"] + [any_spec] * ro,
                out_shape=call["out_shape"] + rider.out_shapes, scratch_shapes=call["scratch_shapes"] + rider.scratch)

    def riding(*refs):
        a = n_in + ri
        b = a + n_out + ro
        mine = refs[:n_in] + refs[a:a + n_out] + refs[b:b + n_scratch]
        theirs = (refs[n_in:a], refs[a + n_out:b], refs[b + n_scratch:])
        t = step()

        @pl.when(t == 0)
        def _():
            rider.first(*theirs)

        body(*mine)
        if rider.middle is not None:
            @pl.when(t == n_steps // 2)
            def _():
                rider.middle(*theirs)

        @pl.when(t == n_steps - 1)
        def _():
            rider.last(*theirs)

    return call, riding, rider.ins


def _rope_tables(S, rider=None):
    half = HEAD_DIM // 2
    tm = 256
    inv_freq = jnp.tile(ROPE_THETA ** (-jnp.arange(half, dtype=f32) / half), 128 // half).reshape(1, 128)
    sign = jnp.tile(jnp.concatenate([-jnp.ones((half,), f32), jnp.ones((half,), f32)]), 128 // HEAD_DIM).reshape(1, 128)

    def body(inv_ref, sign_ref, cos_ref, sg_ref):
        pos = (lax.broadcasted_iota(jnp.int32, (tm, 128), 0) + pl.program_id(0) * tm).astype(f32)
        ang = pos * inv_ref[...]
        cos_ref[...] = jnp.cos(ang)
        sg_ref[...] = jnp.sin(ang) * sign_ref[...]

    vec = pl.BlockSpec((1, 128), lambda i: (0, 0))
    out = pl.BlockSpec((tm, 128), lambda i: (i, 0))
    call = dict(in_specs=[vec, vec], out_specs=[out, out], out_shape=[jax.ShapeDtypeStruct((S, 128), f32)] * 2, scratch_shapes=[])
    call, body, more = _ride(call, rider, body, lambda: pl.program_id(0), S // tm, 2, 2, 0)
    return pl.pallas_call(body, name="rope_tables", grid=(S // tm,), compiler_params=_cp("arbitrary"), **call)(inv_freq, sign, *more)


def _swap_halves(v):
    n = v.shape[1]
    lane = lax.broadcasted_iota(jnp.int32, v.shape, 1)
    return jnp.where((lane % HEAD_DIM) < HEAD_DIM // 2, pltpu.roll(v, n - HEAD_DIM // 2, 1), pltpu.roll(v, HEAD_DIM // 2, 1))


def _in_proj(x, w1, win_g, cos_t, sg_t):
    S = x.shape[0]
    tm = 256
    w = IN_W // N_DEV

    def body(x_ref, w1_ref, wg_ref, cos_ref, sg_ref, u_ref, qkv_ref, hp_ref, w_ref):
        @pl.when(pl.program_id(0) == 0)
        def _():
            for d in range(N_DEV):
                w_ref[:, w * d:w * (d + 1)] = wg_ref[d]

        xv = x_ref[...]
        r = lax.rsqrt(jnp.mean(xv * xv, axis=-1, keepdims=True) + EPS)
        u = (xv * r * w1_ref[...]).astype(bf16)
        u_ref[...] = u
        cosv, sgv = jnp.tile(cos_ref[...], (1, ATTN_W // 128)), jnp.tile(sg_ref[...], (1, ATTN_W // 128))
        for j in range(3):
            pj = _dot(u, w_ref[:, j * ATTN_W:(j + 1) * ATTN_W])
            if j < 2:
                pj = pj * cosv + _swap_halves(pj) * sgv
            if j == 0:
                pj = pj * (HEAD_DIM ** -0.5)
            qkv_ref[:, j * ATTN_W:(j + 1) * ATTN_W] = pj.astype(bf16)
        for j in range(4):
            lo = 3 * ATTN_W + j * HGRN_W
            hp_ref[:, j * HGRN_W:(j + 1) * HGRN_W] = _dot(u, w_ref[:, lo:lo + HGRN_W])

    return pl.pallas_call(
        body, name="in_proj", grid=(S // tm,),
        in_specs=[pl.BlockSpec((tm, D_MODEL), lambda i: (i, 0)), pl.BlockSpec((1, D_MODEL), lambda i: (0, 0)),
                  pl.BlockSpec((N_DEV, D_MODEL, w), lambda i: (0, 0, 0)),
                  pl.BlockSpec((tm, 128), lambda i: (i, 0)), pl.BlockSpec((tm, 128), lambda i: (i, 0))],
        out_specs=[pl.BlockSpec((tm, D_MODEL), lambda i: (i, 0)), pl.BlockSpec((tm, 3 * ATTN_W), lambda i: (i, 0)),
                   pl.BlockSpec((tm, 4 * HGRN_W), lambda i: (i, 0)), pl.BlockSpec((D_MODEL, IN_W), lambda i: (0, 0))],
        out_shape=[jax.ShapeDtypeStruct((S, D_MODEL), bf16), jax.ShapeDtypeStruct((S, 3 * ATTN_W), bf16),
                   jax.ShapeDtypeStruct((S, 4 * HGRN_W), f32), jax.ShapeDtypeStruct((D_MODEL, IN_W), bf16)],
        compiler_params=_cp("arbitrary"),
    )(x, w1, win_g, cos_t, sg_t)


def _head_masks():
    lane = lax.broadcasted_iota(jnp.int32, (ATTN_BLK, 128), 1)
    even = lane < HEAD_DIM
    return even, (even, jnp.logical_not(even))


def _pair_fwd(q2, k2, v2, bias):
    even, masks = _head_masks()
    outs, lses = [], []
    for e in range(2):
        qm = jnp.where(masks[e], q2, 0.0).astype(bf16)
        s = _dot_nt(qm, k2) + bias
        m = jnp.max(s, axis=-1, keepdims=True)
        pe = jnp.exp(s - m)
        lsum = jnp.sum(pe, axis=-1, keepdims=True)
        outs.append(_dot(pe.astype(bf16), v2) / lsum)
        lses.append(jnp.broadcast_to(m + jnp.log(lsum), (ATTN_BLK, 128)))
    return jnp.where(even, outs[0], outs[1]), jnp.where(even, lses[0], lses[1])


def _merge(y0, l0, y1, l1):
    mx = jnp.maximum(l0, l1)
    a, b = jnp.exp(l0 - mx), jnp.exp(l1 - mx)
    tot = a + b
    return (a * y0 + b * y1) / tot, mx + jnp.log(tot)


def _pair_bwd(q2, k2f, v2, dy2, lse2, delta2, bias):
    _, masks = _head_masks()
    k2 = k2f.astype(bf16)
    klane = lax.broadcasted_iota(jnp.int32, (2 * ATTN_BLK, 128), 1) < HEAD_DIM
    kmasks = (klane, jnp.logical_not(klane))
    dq2 = jnp.zeros((ATTN_BLK, 128), f32)
    pes, dss, qms, dyms = [], [], [], []
    for e in range(2):
        c0 = e * HEAD_DIM
        qm = jnp.where(masks[e], q2, 0.0).astype(bf16)
        km = jnp.where(kmasks[e], k2f, 0.0).astype(bf16)
        dym = jnp.where(masks[e], dy2, 0.0).astype(bf16)
        pe = jnp.exp(_dot_nt(qm, k2) + bias - lse2[:, c0:c0 + 1])
        ds = (pe * (_dot_nt(dym, v2) - delta2[:, c0:c0 + 1])).astype(bf16)
        dq2 = dq2 + _dot(ds, km)
        pes.append(pe.astype(bf16))
        dss.append(ds)
        qms.append(qm)
        dyms.append(dym)
    dv2 = _dot_tn(jnp.concatenate(pes, axis=0), jnp.concatenate(dyms, axis=0))
    dk2 = _dot_tn(jnp.concatenate(dss, axis=0), jnp.concatenate(qms, axis=0))
    return dq2, dk2, dv2


TOK = 2048


def _key_bias():
    qi = lax.broadcasted_iota(jnp.int32, (ATTN_BLK, 2 * ATTN_BLK), 0)
    kj = lax.broadcasted_iota(jnp.int32, (ATTN_BLK, 2 * ATTN_BLK), 1)
    delta = ATTN_BLK + qi - kj
    seen = (delta >= 0) & (delta <= ATTN_BLK)
    return jnp.where(seen, 0.0, NEG), jnp.where(seen & (kj >= ATTN_BLK), 0.0, NEG)


def _attn_fwd(qkv, rider=None):
    S = qkv.shape[0]
    nS = S // TOK

    def body(q_ref, kp_ref, kc_ref, vp_ref, vc_ref, y_ref, l_ref, qs, k2, v2, ay, al):
        n = pl.program_id(1)
        qs[...] = q_ref[...].astype(f32)
        k2[0:TOK] = kp_ref[...].astype(f32)
        k2[TOK:2 * TOK] = kc_ref[...].astype(f32)
        v2[0:TOK] = vp_ref[...].astype(f32)
        v2[TOK:2 * TOK] = vc_ref[...].astype(f32)
        bias_any, bias_first = _key_bias()

        def block(dil, r, b, step, last):
            start = r + pl.multiple_of(step * b, step)
            rows = pl.ds(start, ATTN_BLK, stride=dil) if dil > 1 else pl.ds(start, ATTN_BLK)
            keys = (pl.ds(TOK + start - step, 2 * ATTN_BLK, stride=dil) if dil > 1
                    else pl.ds(TOK + start - step, 2 * ATTN_BLK))
            bias = jnp.where((n == 0) & (b == 0), bias_first, bias_any)
            out, lse = _pair_fwd(qs[rows, :], k2[keys, :].astype(bf16), v2[keys, :].astype(bf16), bias)
            if dil < DILATIONS[-1]:
                out, lse = _merge(ay[rows, :], al[rows, :], out, lse)
            if last:
                y_ref[rows, :] = out
                l_ref[rows, :] = lse
            else:
                ay[rows, :] = out
                al[rows, :] = lse

        for dil in reversed(DILATIONS):
            def loop(i, carry, dil=dil):
                block(dil, i % dil, i // dil, ATTN_BLK * dil, dil == 1)
                return carry
            lax.fori_loop(0, TOK // ATTN_BLK, loop, 0, unroll=8)

    blk = (TOK, 128)
    cur = lambda c: pl.BlockSpec(blk, lambda p, n: (n, 4 * c + p))
    prv = lambda c: pl.BlockSpec(blk, lambda p, n: (jnp.maximum(n - 1, 0), 4 * c + p))
    out = pl.BlockSpec(blk, lambda p, n: (n, p))
    call = dict(in_specs=[cur(0), prv(1), cur(1), prv(2), cur(2)], out_specs=[out, out],
                out_shape=[jax.ShapeDtypeStruct((S, ATTN_W), f32)] * 2,
                scratch_shapes=[pltpu.VMEM(blk, f32), pltpu.VMEM((2 * TOK, 128), f32), pltpu.VMEM((2 * TOK, 128), f32),
                                pltpu.VMEM(blk, f32), pltpu.VMEM(blk, f32)])
    call, body, more = _ride(call, rider, body, lambda: pl.program_id(0) * nS + pl.program_id(1), (ATTN_W // 128) * nS, 5, 2, 5)
    return pl.pallas_call(body, name="attention_fwd", grid=(ATTN_W // 128, nS), compiler_params=_cp("arbitrary", "arbitrary"),
                          **call)(qkv, qkv, qkv, qkv, qkv, *more)


def _attn_bwd(qkv, ya, lse, dmix, rider=None):
    S = qkv.shape[0]
    nS = S // TOK

    def body(q_ref, kp_ref, kc_ref, vp_ref, vc_ref, y_ref, l_ref, dy_ref, dq_ref, dk_ref, dv_ref, qs, k2, v2, dk2, dv2, dqa, dl):
        n = pl.program_id(1)

        @pl.when(n == 0)
        def _():
            dk2[...] = jnp.zeros_like(dk2)
            dv2[...] = jnp.zeros_like(dv2)

        @pl.when(n < nS)
        def _():
            qs[...] = q_ref[...].astype(f32)
            k2[0:TOK] = kp_ref[...].astype(f32)
            k2[TOK:2 * TOK] = kc_ref[...].astype(f32)
            v2[0:TOK] = vp_ref[...].astype(f32)
            v2[TOK:2 * TOK] = vc_ref[...].astype(f32)
            li = lax.broadcasted_iota(jnp.int32, (128, 128), 0)
            lj = lax.broadcasted_iota(jnp.int32, (128, 128), 1)
            seg = jnp.where((li // HEAD_DIM) == (lj // HEAD_DIM), 1.0, 0.0).astype(bf16)
            bias_any, bias_first = _key_bias()

            def delta_rows(t, carry):
                rows = pl.ds(pl.multiple_of(256 * t, 256), 256)
                dyy = dy_ref[rows, :] * y_ref[rows, :]
                hi = dyy.astype(bf16)
                dl[rows, :] = _dot(hi, seg) + _dot((dyy - hi.astype(f32)).astype(bf16), seg)
                return carry

            lax.fori_loop(0, TOK // 256, delta_rows, 0)

            def block(dil, r, b, step, first_pattern, last):
                start = r + pl.multiple_of(step * b, step)
                rows = pl.ds(start, ATTN_BLK, stride=dil) if dil > 1 else pl.ds(start, ATTN_BLK)
                keys = (pl.ds(TOK + start - step, 2 * ATTN_BLK, stride=dil) if dil > 1
                        else pl.ds(TOK + start - step, 2 * ATTN_BLK))
                bias = jnp.where((n == 0) & (b == 0), bias_first, bias_any)
                dq2, dkk, dvv = _pair_bwd(qs[rows, :], k2[keys, :], v2[keys, :].astype(bf16), dy_ref[rows, :],
                                          l_ref[rows, :], dl[rows, :], bias)
                if last:
                    dq_ref[rows, :] = dqa[rows, :] + dq2
                elif first_pattern:
                    dqa[rows, :] = dq2
                else:
                    dqa[rows, :] += dq2
                dk2[keys, :] += dkk
                dv2[keys, :] += dvv

            for dil in reversed(DILATIONS):
                def loop(i, carry, dil=dil):
                    block(dil, i % dil, i // dil, ATTN_BLK * dil, dil == DILATIONS[-1], dil == 1)
                    return carry
                lax.fori_loop(0, TOK // ATTN_BLK, loop, 0, unroll=8)

        dk_ref[...] = dk2[0:TOK]
        dv_ref[...] = dv2[0:TOK]
        dk2[0:TOK] = dk2[TOK:2 * TOK]
        dv2[0:TOK] = dv2[TOK:2 * TOK]
        dk2[TOK:2 * TOK] = jnp.zeros((TOK, 128), f32)
        dv2[TOK:2 * TOK] = jnp.zeros((TOK, 128), f32)

    blk = (TOK, 128)
    cn = lambda n: jnp.minimum(n, nS - 1)
    pn = lambda n: jnp.clip(n - 1, 0, nS - 1)
    cur = lambda c: pl.BlockSpec(blk, lambda p, n: (cn(n), 4 * c + p))
    prv = lambda c: pl.BlockSpec(blk, lambda p, n: (pn(n), 4 * c + p))
    at_n = pl.BlockSpec(blk, lambda p, n: (cn(n), p))
    at_p = pl.BlockSpec(blk, lambda p, n: (pn(n), p))
    big = lambda: pltpu.VMEM((2 * TOK, 128), f32)
    call = dict(in_specs=[cur(0), prv(1), cur(1), prv(2), cur(2), at_n, at_n, at_n], out_specs=[at_n, at_p, at_p],
                out_shape=[jax.ShapeDtypeStruct((S, ATTN_W), f32)] * 3,
                scratch_shapes=[pltpu.VMEM(blk, f32), big(), big(), big(), big(), pltpu.VMEM(blk, f32), pltpu.VMEM(blk, f32)])
    call, body, more = _ride(call, rider, body, lambda: pl.program_id(0) * (nS + 1) + pl.program_id(1),
                             (ATTN_W // 128) * (nS + 1), 8, 3, 7)
    return pl.pallas_call(body, name="attention_bwd", grid=(ATTN_W // 128, nS + 1), compiler_params=_cp("arbitrary", "arbitrary"),
                          **call)(qkv, qkv, qkv, qkv, qkv, ya, lse, dmix, *more)


HG_T = 256
N_HH = HGRN_W // HGRN_HD
HG_SUB = 128
SAFE_RANGE = 75.0


def _row_in_chunk():
    return lax.broadcasted_iota(jnp.int32, (HG_T, HGRN_HD), 0) % CHUNK


def _chunk_cumsum(v, rc):
    k = 1
    while k < CHUNK:
        v = v + jnp.where(rc >= k, pltpu.roll(v, k, 0), 0.0)
        k *= 2
    return v


def _chunk_rcumsum(v, rc):
    k = 1
    while k < CHUNK:
        v = v + jnp.where(rc < CHUNK - k, pltpu.roll(v, HG_T - k, 0), 0.0)
        k *= 2
    return v


def _hgrn_gates(qb, fb, lb):
    sf = _sigmoid(fb)
    f = lb + (1.0 - lb) * sf
    sq = _sigmoid(qb)
    return sf, f, jnp.log(f), 1.0 - f, sq, qb * sq


def _hgrn_prep(qb, fb, lbl2, rc):
    lb = _sigmoid(lbl2[0:1, :] - lbl2[1:2, :])
    sf, f, lf, key, sq, qf = _hgrn_gates(qb, fb, lb)
    b = _chunk_cumsum(lf, rc)
    rem = _chunk_rcumsum(lf, rc) - lf
    return dict(lb=lb, sf=sf, f=f, key=key, sq=sq, qf=qf, b=b, rem=rem, eb=jnp.exp(b), er=jnp.exp(rem))


def _chunk_mask():
    r = lax.broadcasted_iota(jnp.int32, (HG_SUB, HG_SUB), 0)
    c = lax.broadcasted_iota(jnp.int32, (HG_SUB, HG_SUB), 1)
    return ((r // CHUNK) == (c // CHUNK)) & (c <= r)


def _hgrn_fwd(hp, lbl, wn):
    S = hp.shape[0]
    nT = S // HG_T

    def body(qb_ref, fb_ref, ib_ref, gb_ref, lbl_ref, wn_ref, yb_ref, o_ref, st_ref, ST, qt_s, kh_s, dec_s, oi_s):
        @pl.when(pl.program_id(0) == 0)
        def _():
            ST[...] = jnp.zeros_like(ST)

        rc = _row_in_chunk()
        for h in range(N_HH):
            sl = slice(HGRN_HD * h, HGRN_HD * (h + 1))
            p = _hgrn_prep(qb_ref[:, sl], fb_ref[:, sl], lbl_ref[:, sl], rc)
            qf, key, b = p["qf"], p["key"], p["b"]
            qt = qf * p["eb"]
            qt_s[:, sl] = qt.astype(bf16)
            kh_s[:, sl] = (key * p["er"]).astype(bf16)
            dec_s[:, sl] = jnp.exp(b + p["rem"])
            rng = jnp.max(-(b + p["rem"]))

            @pl.when(rng < SAFE_RANGE)
            def _():
                kp = (key * jnp.exp(-b)).astype(bf16)
                cmask = _chunk_mask()
                for j in range(HG_T // HG_SUB):
                    rs = slice(HG_SUB * j, HG_SUB * (j + 1))
                    sc = jnp.where(cmask, _dot_nt(qt[rs].astype(bf16), kp[rs]), 0.0).astype(bf16)
                    oi_s[rs, sl] = _dot(sc, ib_ref[rs, sl].astype(bf16))

            @pl.when(rng >= SAFE_RANGE)
            def _():
                v = ib_ref[:, sl]
                ones = jnp.ones((HGRN_HD, HGRN_HD), bf16)

                def lag(l, o):
                    e = jnp.exp(jnp.where(rc >= l, b - pltpu.roll(b, l, 0), NEG))
                    pr = qf * pltpu.roll(key, l, 0) * e
                    return o + _dot(pr.astype(bf16), ones) * pltpu.roll(v, l, 0)

                oi_s[:, sl] = lax.fori_loop(1, CHUNK, lag, _dot((qf * key).astype(bf16), ones) * v)

        def step(c, carry):
            rows = pl.ds(pl.multiple_of(c * CHUNK, CHUNK), CHUNK)
            row0 = pl.ds(pl.multiple_of(c * CHUNK, CHUNK), 1)
            for h in range(N_HH):
                sl = slice(HGRN_HD * h, HGRN_HD * (h + 1))
                stv = ST[h]
                st_ref[c, sl, :] = stv
                oi_s[rows, sl] += _dot_nt(qt_s[rows, sl], stv.astype(bf16))
                ST[h] = stv * dec_s[row0, sl] + _dot_tn(ib_ref[rows, sl].astype(bf16), kh_s[rows, sl])
            return carry

        lax.fori_loop(0, HG_T // CHUNK, step, 0, unroll=True)

        for h in range(N_HH):
            sl = slice(HGRN_HD * h, HGRN_HD * (h + 1))
            o = oi_s[:, sl]
            o_ref[:, sl] = o
            on = o * lax.rsqrt(jnp.mean(o * o, axis=-1, keepdims=True) + EPS)
            g = gb_ref[:, sl]
            yb_ref[:, sl] = on * wn_ref[:, sl] * (g * _sigmoid(g))

    col = lambda c: pl.BlockSpec((HG_T, HGRN_W), lambda i: (i, c))
    tile = pl.BlockSpec((HG_T, HGRN_W), lambda i: (i, 0))
    whole = lambda a: pl.BlockSpec(a.shape, lambda i: (0, 0))
    return pl.pallas_call(
        body, name="hgrn_fwd", grid=(nT,),
        in_specs=[col(0), col(1), col(2), col(3), whole(lbl), whole(wn)],
        out_specs=[tile, tile, pl.BlockSpec((HG_T // CHUNK, HGRN_W, HGRN_HD), lambda i: (i, 0, 0))],
        out_shape=[jax.ShapeDtypeStruct((S, HGRN_W), f32), jax.ShapeDtypeStruct((S, HGRN_W), f32),
                   jax.ShapeDtypeStruct((S // CHUNK, HGRN_W, HGRN_HD), f32)],
        scratch_shapes=[pltpu.VMEM((N_HH, HGRN_HD, HGRN_HD), f32), pltpu.VMEM((HG_T, HGRN_W), bf16),
                        pltpu.VMEM((HG_T, HGRN_W), bf16), pltpu.VMEM((HG_T, HGRN_W), f32), pltpu.VMEM((HG_T, HGRN_W), f32)],
        compiler_params=_cp("arbitrary"),
    )(hp, hp, hp, hp, lbl, wn)


def _hgrn_bwd(hp, lbl, wn, o_sav, states, dmix, rider=None):
    S = hp.shape[0]
    nT = S // HG_T

    def body(qb_ref, fb_ref, ib_ref, gb_ref, lbl_ref, wn_ref, o_ref, st_ref, dy_ref,
             dq_ref, df_ref, di_ref, dg_ref, gwn_ref, glb_ref,
             DST, qt_s, kh_s, dec_s, do_s, dqt_s, dkh_s, dbl_s, dvi_s, dqi_s, dki_s, dbi_s):
        @pl.when(pl.program_id(0) == 0)
        def _():
            DST[...] = jnp.zeros_like(DST)
            gwn_ref[...] = jnp.zeros_like(gwn_ref)
            glb_ref[...] = jnp.zeros_like(glb_ref)

        rc = _row_in_chunk()
        preps = []
        for h in range(N_HH):
            sl = slice(HGRN_HD * h, HGRN_HD * (h + 1))
            p = _hgrn_prep(qb_ref[:, sl], fb_ref[:, sl], lbl_ref[:, sl], rc)
            preps.append(p)
            qf, key, b = p["qf"], p["key"], p["b"]
            v = ib_ref[:, sl]
            o = o_ref[:, sl]
            rinv = lax.rsqrt(jnp.mean(o * o, axis=-1, keepdims=True) + EPS)
            on = o * rinv
            g = gb_ref[:, sl]
            sgm = _sigmoid(g)
            silu_g = g * sgm
            dy = dy_ref[:, sl]
            wn_v = wn_ref[:, sl]
            gwn_ref[:, sl] += jnp.sum(dy * on * silu_g, axis=0, keepdims=True)
            dg_ref[:, sl] = (dy * on * wn_v * (sgm * (1.0 + g * (1.0 - sgm)))).astype(bf16)
            t1 = dy * wn_v * silu_g
            do = rinv * (t1 - on * jnp.mean(t1 * on, axis=-1, keepdims=True))
            do_s[:, sl] = do.astype(bf16)
            qt = qf * p["eb"]
            qt_s[:, sl] = qt.astype(bf16)
            kh_s[:, sl] = (key * p["er"]).astype(bf16)
            dec_s[:, sl] = jnp.exp(b + p["rem"])
            rng = jnp.max(-(b + p["rem"]))

            @pl.when(rng < SAFE_RANGE)
            def _():
                einv = jnp.exp(-b)
                kp = (key * einv).astype(bf16)
                cmask = _chunk_mask()
                for j in range(HG_T // HG_SUB):
                    rs = slice(HG_SUB * j, HG_SUB * (j + 1))
                    qtb, dob, vb = qt[rs].astype(bf16), do[rs].astype(bf16), v[rs].astype(bf16)
                    sc = jnp.where(cmask, _dot_nt(qtb, kp[rs]), 0.0).astype(bf16)
                    dsc = jnp.where(cmask, _dot_nt(dob, vb), 0.0).astype(bf16)
                    dqp = _dot(dsc, kp[rs])
                    dkp = _dot_tn(dsc, qtb)
                    dvi_s[rs, sl] = _dot_tn(sc, dob)
                    dqi_s[rs, sl] = dqp * p["eb"][rs]
                    dki_s[rs, sl] = dkp * einv[rs]
                    dbi_s[rs, sl] = dqp * qtb.astype(f32) - dkp * kp[rs].astype(f32)

            @pl.when(rng >= SAFE_RANGE)
            def _():
                ones = jnp.ones((HGRN_HD, HGRN_HD), bf16)

                def lag(l, carry):
                    dqf, dkey, db, dv = carry
                    e = jnp.exp(jnp.where(rc >= l, b - pltpu.roll(b, l, 0), NEG))
                    ks, vs, qe = pltpu.roll(key, l, 0), pltpu.roll(v, l, 0), qf * e
                    pr = qe * ks
                    rl = _dot(pr.astype(bf16), ones)
                    drl = jnp.where(rc >= l, _dot((do * vs).astype(bf16), ones), 0.0)
                    gl = drl * pr
                    back = HG_T - l
                    return (dqf + drl * ks * e, dkey + pltpu.roll(drl * qe, back, 0), db + gl - pltpu.roll(gl, back, 0),
                            dv + pltpu.roll(rl * do, back, 0))

                rl0 = _dot((qf * key).astype(bf16), ones)
                drl0 = _dot((do * v).astype(bf16), ones)
                dqf, dkey, db, dv = lax.fori_loop(1, CHUNK, lag, (drl0 * key, drl0 * qf, jnp.zeros((HG_T, HGRN_HD), f32), rl0 * do))
                dvi_s[:, sl] = dv
                dqi_s[:, sl] = dqf
                dki_s[:, sl] = dkey
                dbi_s[:, sl] = db

        def step(k, carry):
            c = HG_T // CHUNK - 1 - k
            rows = pl.ds(pl.multiple_of(c * CHUNK, CHUNK), CHUNK)
            row0 = pl.ds(pl.multiple_of(c * CHUNK, CHUNK), 1)
            for h in range(N_HH):
                sl = slice(HGRN_HD * h, HGRN_HD * (h + 1))
                stp = st_ref[c, sl, :]
                dst = DST[h]
                dstb = dst.astype(bf16)
                dob = do_s[rows, sl]
                khb = kh_s[rows, sl]
                dec = dec_s[row0, sl]
                dqt_s[rows, sl] = _dot(dob, stp.astype(bf16))
                dkh = _dot(ib_ref[rows, sl].astype(bf16), dstb)
                dkh_s[rows, sl] = dkh
                dvi_s[rows, sl] += _dot_nt(khb, dstb)
                dbl = jnp.sum(dst * stp, axis=0, keepdims=True) * dec + jnp.sum(dkh * khb.astype(f32), axis=0, keepdims=True)
                dbl_s[rows, sl] = jnp.broadcast_to(dbl, (CHUNK, HGRN_HD))
                DST[h] = dst * dec + _dot_tn(dob, qt_s[rows, sl])
            return carry

        lax.fori_loop(0, HG_T // CHUNK, step, 0, unroll=True)

        for h in range(N_HH):
            sl = slice(HGRN_HD * h, HGRN_HD * (h + 1))
            qb = qb_ref[:, sl]
            p = preps[h]
            sf, sq, lb = p["sf"], p["sq"], p["lb"]
            dqt, dkh = dqt_s[:, sl], dkh_s[:, sl]
            dqf = dqt * p["eb"] + dqi_s[:, sl]
            dkey = dkh * p["er"] + dki_s[:, sl]
            db = dqt * (p["qf"] * p["eb"]) - dkh * (p["key"] * p["er"]) + jnp.where(rc == CHUNK - 1, dbl_s[:, sl], 0.0) + dbi_s[:, sl]
            df = _chunk_rcumsum(db, rc) / p["f"] - dkey
            df_ref[:, sl] = (df * (1.0 - lb) * sf * (1.0 - sf)).astype(bf16)
            glb_ref[:, sl] += jnp.sum(df * (1.0 - sf), axis=0, keepdims=True)
            dq_ref[:, sl] = (dqf * (sq * (1.0 + qb * (1.0 - sq)))).astype(bf16)
            di_ref[:, sl] = dvi_s[:, sl].astype(bf16)

    rev = lambda i: nT - 1 - i
    col = lambda c: pl.BlockSpec((HG_T, HGRN_W), lambda i: (rev(i), c))
    tile = pl.BlockSpec((HG_T, HGRN_W), lambda i: (rev(i), 0))
    whole = lambda a: pl.BlockSpec(a.shape, lambda i: (0, 0))
    vec = pl.BlockSpec((1, HGRN_W), lambda i: (0, 0))
    tb = lambda: pltpu.VMEM((HG_T, HGRN_W), bf16)
    tf = lambda: pltpu.VMEM((HG_T, HGRN_W), f32)
    call = dict(in_specs=[col(0), col(1), col(2), col(3), whole(lbl), whole(wn), tile,
                          pl.BlockSpec((HG_T // CHUNK, HGRN_W, HGRN_HD), lambda i: (rev(i), 0, 0)),
                          pl.BlockSpec((HG_T, HGRN_W), lambda i: (rev(i), 1))],
                out_specs=[tile, tile, tile, tile, vec, vec],
                out_shape=[jax.ShapeDtypeStruct((S, HGRN_W), bf16)] * 4 + [jax.ShapeDtypeStruct((1, HGRN_W), f32)] * 2,
                scratch_shapes=[pltpu.VMEM((N_HH, HGRN_HD, HGRN_HD), f32), tb(), tb(), tf(), tb(), tf(), tf(), tf(), tf(), tf(),
                                tf(), tf()])
    call, body, more = _ride(call, rider, body, lambda: pl.program_id(0), nT, 9, 6, 12)
    return pl.pallas_call(body, name="hgrn_bwd", grid=(nT,), compiler_params=_cp("arbitrary"), **call)(
        hp, hp, hp, hp, lbl, wn, o_sav, states, dmix, *more)


def _out_proj(x, ya, yb, wout, w2):
    S = x.shape[0]
    tm = 512

    def body(x_ref, ya_ref, yb_ref, w_ref, w2_ref, h1_ref, u2_ref, mix_ref):
        mixed = jnp.concatenate([ya_ref[...], yb_ref[...]], axis=1).astype(bf16)
        mix_ref[...] = mixed
        h1 = x_ref[...] + _dot(mixed, w_ref[...])
        h1_ref[...] = h1
        r = lax.rsqrt(jnp.mean(h1 * h1, axis=-1, keepdims=True) + EPS)
        u2_ref[...] = (h1 * r * w2_ref[...]).astype(bf16)

    row = lambda w: pl.BlockSpec((tm, w), lambda i: (i, 0))
    return pl.pallas_call(
        body, name="out_proj", grid=(S // tm,),
        in_specs=[row(D_MODEL), row(ATTN_W), row(HGRN_W), pl.BlockSpec((D_MODEL, D_MODEL), lambda i: (0, 0)),
                  pl.BlockSpec((1, D_MODEL), lambda i: (0, 0))],
        out_specs=[row(D_MODEL), row(D_MODEL), row(D_MODEL)],
        out_shape=[jax.ShapeDtypeStruct((S, D_MODEL), f32), jax.ShapeDtypeStruct((S, D_MODEL), bf16),
                   jax.ShapeDtypeStruct((S, D_MODEL), bf16)],
        compiler_params=_cp("arbitrary"),
    )(x, ya, yb, wout, w2)


def _gate_up(u2, wgu_g):
    S = u2.shape[0]
    w = 2 * FFN // N_DEV
    tm, tn = 512, 2 * w
    nj = FFN // tn

    def body(u_ref, wgg_ref, wug_ref, g_ref, up_ref, a_ref, wg_ref, wu_ref):
        @pl.when(pl.program_id(1) == 0)
        def _():
            for k in range(2):
                wg_ref[:, w * k:w * (k + 1)] = wgg_ref[k]
                wu_ref[:, w * k:w * (k + 1)] = wug_ref[k]

        u = u_ref[...]
        g = _dot(u, wg_ref[...])
        up = _dot(u, wu_ref[...])
        g_ref[...] = g.astype(bf16)
        up_ref[...] = up.astype(bf16)
        a_ref[...] = (g * _sigmoid(g) * up).astype(bf16)

    out = pl.BlockSpec((tm, tn), lambda j, i: (i, j))
    wout = pl.BlockSpec((D_MODEL, tn), lambda j, i: (0, j))
    return pl.pallas_call(
        body, name="gate_up", grid=(nj, S // tm),
        in_specs=[pl.BlockSpec((tm, D_MODEL), lambda j, i: (i, 0)), pl.BlockSpec((2, D_MODEL, w), lambda j, i: (j, 0, 0)),
                  pl.BlockSpec((2, D_MODEL, w), lambda j, i: (j + nj, 0, 0))],
        out_specs=[out, out, out, wout, wout],
        out_shape=[jax.ShapeDtypeStruct((S, FFN), bf16)] * 3 + [jax.ShapeDtypeStruct((D_MODEL, FFN), bf16)] * 2,
        compiler_params=_cp("arbitrary", "arbitrary"),
    )(u2, wgu_g, wgu_g)


def _rms_bwd(dyw, hn, r):
    return r * (dyw - hn * jnp.mean(dyw * hn, axis=-1, keepdims=True))


def _down_loss(act, wdown, h1, tgt, w3):
    S = act.shape[0]
    tm = 256

    def body(a_ref, w_ref, h1_ref, t_ref, w3_ref, dh2_ref, loss_ref, gw3_ref):
        @pl.when(pl.program_id(0) == 0)
        def _():
            loss_ref[...] = jnp.zeros_like(loss_ref)
            gw3_ref[...] = jnp.zeros_like(gw3_ref)

        h2 = h1_ref[...] + _dot(a_ref[...], w_ref[...])
        r = lax.rsqrt(jnp.mean(h2 * h2, axis=-1, keepdims=True) + EPS)
        hn = h2 * r
        w3 = w3_ref[...]
        err = hn * w3 - t_ref[...]
        loss_ref[...] += (0.5 / D_MODEL) * jnp.sum(err * err)
        dy = err * (1.0 / D_MODEL)
        gw3_ref[...] += jnp.sum(dy * hn, axis=0, keepdims=True)
        dh2_ref[...] = _rms_bwd(dy * w3, hn, r)

    row = lambda w: pl.BlockSpec((tm, w), lambda i: (i, 0))
    return pl.pallas_call(
        body, name="down_loss", grid=(S // tm,),
        in_specs=[row(FFN), pl.BlockSpec((FFN, D_MODEL), lambda i: (0, 0)), row(D_MODEL), row(D_MODEL),
                  pl.BlockSpec((1, D_MODEL), lambda i: (0, 0))],
        out_specs=[row(D_MODEL), pl.BlockSpec((1, 128), lambda i: (0, 0)), pl.BlockSpec((1, D_MODEL), lambda i: (0, 0))],
        out_shape=[jax.ShapeDtypeStruct((S, D_MODEL), f32), jax.ShapeDtypeStruct((1, 128), f32),
                   jax.ShapeDtypeStruct((1, D_MODEL), f32)],
        compiler_params=_cp("arbitrary"),
    )(act, wdown, h1, tgt, w3)


def _dact(dh2, wdown, gate, up):
    S = dh2.shape[0]
    tm = 256

    def body(d_ref, w_ref, g_ref, u_ref, o_ref):
        da = _dot_nt(d_ref[...].astype(bf16), w_ref[...])
        g = g_ref[...].astype(f32)
        sg = _sigmoid(g)
        o_ref[1] = (da * g * sg).astype(bf16)
        o_ref[0] = (da * u_ref[...].astype(f32) * (sg * (1.0 + g * (1.0 - sg)))).astype(bf16)

    row = lambda w: pl.BlockSpec((tm, w), lambda i: (i, 0))
    return pl.pallas_call(
        body, name="dact", grid=(S // tm,),
        in_specs=[row(D_MODEL), pl.BlockSpec((FFN, D_MODEL), lambda i: (0, 0)), row(FFN), row(FFN)],
        out_specs=pl.BlockSpec((2, tm, FFN), lambda i: (0, i, 0)),
        out_shape=jax.ShapeDtypeStruct((2, S, FFN), bf16),
        compiler_params=_cp("arbitrary"),
    )(dh2, wdown, gate, up)


def _dgu(dgu2, wgate, wup, h1, w2, dh2, wout, rider=None):
    S = dgu2.shape[1]
    tm = 256

    def body(d_ref, wg_ref, wu_ref, h1_ref, w2_ref, dh2_ref, wo_ref, dh1_ref, gw2_ref, dmix_ref):
        @pl.when(pl.program_id(0) == 0)
        def _():
            gw2_ref[...] = jnp.zeros_like(gw2_ref)

        du2 = _dot_nt(d_ref[0], wg_ref[...]) + _dot_nt(d_ref[1], wu_ref[...])
        h1 = h1_ref[...]
        r = lax.rsqrt(jnp.mean(h1 * h1, axis=-1, keepdims=True) + EPS)
        hn = h1 * r
        gw2_ref[...] += jnp.sum(du2 * hn, axis=0, keepdims=True)
        dh1 = dh2_ref[...] + _rms_bwd(du2 * w2_ref[...], hn, r)
        dh1_ref[...] = dh1
        dmix_ref[...] = _dot_nt(dh1.astype(bf16), wo_ref[...])

    row = lambda w: pl.BlockSpec((tm, w), lambda i: (i, 0))
    call = dict(in_specs=[pl.BlockSpec((2, tm, FFN), lambda i: (0, i, 0)), pl.BlockSpec((D_MODEL, FFN), lambda i: (0, 0)),
                          pl.BlockSpec((D_MODEL, FFN), lambda i: (0, 0)), row(D_MODEL),
                          pl.BlockSpec((1, D_MODEL), lambda i: (0, 0)), row(D_MODEL),
                          pl.BlockSpec((D_MODEL, D_MODEL), lambda i: (0, 0))],
                out_specs=[row(D_MODEL), pl.BlockSpec((1, D_MODEL), lambda i: (0, 0)), row(D_MODEL)],
                out_shape=[jax.ShapeDtypeStruct((S, D_MODEL), f32), jax.ShapeDtypeStruct((1, D_MODEL), f32),
                           jax.ShapeDtypeStruct((S, D_MODEL), f32)], scratch_shapes=[])
    call, body, more = _ride(call, rider, body, lambda: pl.program_id(0), S // tm, 7, 3, 0)
    return pl.pallas_call(body, name="dgu", grid=(S // tm,), compiler_params=_cp("arbitrary"), **call)(
        dgu2, wgate, wup, h1, w2, dh2, wout, *more)


def _din(dq, dk, dv, dhq, dhf, dhi, dhg, cos_t, sg_t, win, x, w1, dh1):
    S = x.shape[0]
    tm = 256

    def body(dq_ref, dk_ref, dv_ref, dhq_ref, dhf_ref, dhi_ref, dhg_ref, cos_ref, sg_ref, w_ref, x_ref, w1_ref, dh1_ref,
             dp_ref, gx_ref, gw1_ref):
        @pl.when(pl.program_id(0) == 0)
        def _():
            gw1_ref[...] = jnp.zeros_like(gw1_ref)

        cosv, sgv = jnp.tile(cos_ref[...], (1, ATTN_W // 128)), jnp.tile(sg_ref[...], (1, ATTN_W // 128))
        unrope = lambda d: d * cosv - sgv * _swap_halves(d)
        parts = [(unrope(dq_ref[...]) * (HEAD_DIM ** -0.5)).astype(bf16), unrope(dk_ref[...]).astype(bf16),
                 dv_ref[...].astype(bf16), dhq_ref[...], dhf_ref[...], dhi_ref[...], dhg_ref[...]]
        du = jnp.zeros((tm, D_MODEL), f32)
        for j, pj in enumerate(parts):
            dp_ref[:, j * 512:(j + 1) * 512] = pj
            du = du + _dot_nt(pj, w_ref[:, j * 512:(j + 1) * 512])
        xv = x_ref[...]
        r = lax.rsqrt(jnp.mean(xv * xv, axis=-1, keepdims=True) + EPS)
        xn = xv * r
        gw1_ref[...] += jnp.sum(du * xn, axis=0, keepdims=True)
        gx_ref[...] = dh1_ref[...] + _rms_bwd(du * w1_ref[...], xn, r)

    row = lambda w: pl.BlockSpec((tm, w), lambda i: (i, 0))
    vec = pl.BlockSpec((1, D_MODEL), lambda i: (0, 0))
    return pl.pallas_call(
        body, name="din", grid=(S // tm,),
        in_specs=[row(512)] * 7 + [row(128), row(128), pl.BlockSpec((D_MODEL, IN_W), lambda i: (0, 0)), row(D_MODEL), vec,
                                   row(D_MODEL)],
        out_specs=[row(IN_W), row(D_MODEL), vec],
        out_shape=[jax.ShapeDtypeStruct((S, IN_W), bf16), jax.ShapeDtypeStruct((S, D_MODEL), f32),
                   jax.ShapeDtypeStruct((1, D_MODEL), f32)],
        compiler_params=_cp("arbitrary"),
    )(dq, dk, dv, dhq, dhf, dhi, dhg, cos_t, sg_t, win, x, w1, dh1)


def _gw(a, bs, tn, name, ts=2048):
    S, M = a.shape
    N = bs[0].shape[1]
    k = len(bs)

    def body(a_ref, *refs):
        @pl.when(pl.program_id(1) == 0)
        def _():
            for o_ref in refs[k:]:
                o_ref[...] = jnp.zeros_like(o_ref)

        at = a_ref[...].astype(bf16)
        for b_ref, o_ref in zip(refs[:k], refs[k:]):
            o_ref[...] += _dot_tn(at, b_ref[...].astype(bf16))

    return pl.pallas_call(
        body, name=name, grid=(N // tn, S // ts),
        in_specs=[pl.BlockSpec((ts, M), lambda j, s: (s, 0))] + [pl.BlockSpec((ts, tn), lambda j, s: (s, j))] * k,
        out_specs=[pl.BlockSpec((M, tn), lambda j, s: (0, j))] * k, out_shape=[jax.ShapeDtypeStruct((M, N), f32)] * k,
        compiler_params=_cp("arbitrary", "arbitrary"),
    )(a, *bs)


def _gw_by_owner(a, b3, w, name, ts):
    S, M = a.shape
    G, _, Ng = b3.shape
    tn = 2 * w
    per_group = Ng // tn
    n_s = S // ts

    def body(a_ref, b_ref, o_ref, acc):
        s = pl.program_id(1)

        @pl.when(s == 0)
        def _():
            acc[...] = jnp.zeros_like(acc)

        acc[...] += _dot_tn(a_ref[...].astype(bf16), b_ref[0].astype(bf16))

        @pl.when(s == n_s - 1)
        def _():
            o_ref[0] = acc[:, 0:w]
            o_ref[1] = acc[:, w:tn]

    return pl.pallas_call(
        body, name=name, grid=(G * per_group, n_s),
        in_specs=[pl.BlockSpec((ts, M), lambda j, s: (s, 0)),
                  pl.BlockSpec((1, ts, tn), lambda j, s: (j // per_group, s, j % per_group))],
        out_specs=pl.BlockSpec((2, M, w), lambda j, s: (j, 0, 0)), out_shape=jax.ShapeDtypeStruct((G * Ng // w, M, w), f32),
        scratch_shapes=[pltpu.VMEM((M, tn), f32)], compiler_params=_cp("arbitrary", "arbitrary"),
    )(a, b3)


MESH = pl.DeviceIdType.MESH
ANY = pl.BlockSpec(memory_space=pl.ANY)
VMEM_SPEC = pl.BlockSpec(memory_space=pltpu.VMEM)


def _pos():
    return lax.axis_index("x"), lax.axis_index("y"), lax.axis_index("c")


def _flip(v, bit):
    return 1 - v if bit else v


def _gather_rider(shards):
    n = len(shards)

    def parts(outs, scratch):
        send_sems, recv_sems, local_sems = scratch[n:]
        x, y, c = _pos()
        chips = [(1 - x, y), (x, 1 - y), (1 - x, 1 - y)]

        def copy(a, k, block, to, src=None):
            dst = outs[a].at[4 * block[0] + 2 * block[1] + block[2]]
            return pltpu.make_async_remote_copy(src_ref=dst if src is None else src, dst_ref=dst, send_sem=send_sems.at[a, k],
                                                recv_sem=recv_sems.at[a, k], device_id=to, device_id_type=MESH)

        bufs = scratch[:n]
        me, sibling = (x, y, c), (x, y, 1 - c)
        own = lambda a: pltpu.make_async_copy(bufs[a], outs[a].at[4 * x + 2 * y + c], local_sems.at[a])
        sent = lambda a: [copy(a, 0, me, sibling, src=bufs[a])] + [copy(a, 1 + j, me, (*chip, c), src=bufs[a])
                                                                   for j, chip in enumerate(chips)]
        passed = lambda a: [copy(a, 4 + j, (*chip, c), sibling) for j, chip in enumerate(chips)]
        landed = lambda a: [copy(a, 1 + j, (*chip, c), me) for j, chip in enumerate(chips)]
        from_sibling = lambda a: [copy(a, 0, sibling, me)] + [copy(a, 4 + j, (*chip, 1 - c), me) for j, chip in enumerate(chips)]
        return bufs, local_sems, own, sent, passed, landed, from_sibling

    def first(ins, outs, scratch):
        bufs, local_sems, own, sent, _, _, _ = parts(outs, scratch)
        loads = [pltpu.make_async_copy(ins[a], bufs[a], local_sems.at[a]) for a in range(n)]
        for ld in loads:
            ld.start()
        for a in range(n):
            loads[a].wait()
            own(a).start()
            for cp in sent(a):
                cp.start()

    def middle(ins, outs, scratch):
        _, _, _, _, passed, landed, _ = parts(outs, scratch)
        for a in range(n):
            for got, on in zip(landed(a), passed(a)):
                got.wait_recv()
                on.start()

    def last(ins, outs, scratch):
        _, _, own, sent, passed, _, from_sibling = parts(outs, scratch)
        for a in range(n):
            for cp in from_sibling(a):
                cp.wait_recv()
        for a in range(n):
            for cp in sent(a) + passed(a):
                cp.wait_send()
            own(a).wait()

    return _Rider(shards, [jax.ShapeDtypeStruct((N_DEV,) + s.shape, s.dtype) for s in shards],
                  [pltpu.VMEM(s.shape, s.dtype) for s in shards]
                  + [pltpu.SemaphoreType.DMA((n, 7)), pltpu.SemaphoreType.DMA((n, 7)), pltpu.SemaphoreType.DMA((n,))],
                  first, last, middle)


def _sibling_rider(grads):
    n = len(grads)

    def copies(g, got, scratch):
        send_sems, recv_sems = scratch
        x, y, c = _pos()
        return [pltpu.make_async_remote_copy(src_ref=g[a].at[2 * q + (1 - c)], dst_ref=got[a].at[q], send_sem=send_sems.at[a, q],
                                             recv_sem=recv_sems.at[a, q], device_id=(x, y, 1 - c), device_id_type=MESH)
                for a in range(n) for q in range(4)]

    def first(g, got, scratch):
        for cp in copies(g, got, scratch):
            cp.start()

    def last(g, got, scratch):
        for cp in copies(g, got, scratch):
            cp.wait()

    return _Rider(grads, [jax.ShapeDtypeStruct((4,) + g.shape[1:], g.dtype) for g in grads],
                  [pltpu.SemaphoreType.DMA((n, 4))] * 2, first, last)


def _chips_rider(sums):
    n = len(sums)

    def copies(s, out, scratch):
        send_sems, recv_sems = scratch
        x, y, c = _pos()
        cps = []
        for a in range(n):
            for f in (1, 2, 3):
                peer = (_flip(x, f >> 1), _flip(y, f & 1), c)
                cps.append(pltpu.make_async_remote_copy(
                    src_ref=s[a].at[2 * peer[0] + peer[1]], dst_ref=out[a].at[f - 1], send_sem=send_sems.at[a, f - 1],
                    recv_sem=recv_sems.at[a, f - 1], device_id=peer, device_id_type=MESH))
        return cps

    def first(s, out, scratch):
        for cp in copies(s, out, scratch):
            cp.start()

    def last(s, out, scratch):
        for cp in copies(s, out, scratch):
            cp.wait()

    return _Rider(sums, [jax.ShapeDtypeStruct((3,) + s.shape[1:], s.dtype) for s in sums],
                  [pltpu.SemaphoreType.DMA((n, 3))] * 2, first, last)


def _both(a, b):
    na = (len(a.ins), len(a.out_shapes), len(a.scratch))

    def split(fa, fb):
        def f(ins, outs, scratch):
            fa(ins[:na[0]], outs[:na[1]], scratch[:na[2]])
            fb(ins[na[0]:], outs[na[1]:], scratch[na[2]:])
        return f

    return _Rider(a.ins + b.ins, a.out_shapes + b.out_shapes, a.scratch + b.scratch, split(a.first, b.first), split(a.last, b.last))


def _alone(rider, name):
    ri, ro = len(rider.ins), len(rider.out_shapes)

    def body(*refs):
        theirs = (refs[:ri], refs[ri:ri + ro], refs[ri + ro:])
        rider.first(*theirs)
        if rider.middle is not None:
            rider.middle(*theirs)
        rider.last(*theirs)

    return pl.pallas_call(body, name=name, in_specs=[ANY] * ri, out_specs=[ANY] * ro, out_shape=rider.out_shapes,
                          scratch_shapes=rider.scratch)(*rider.ins)


def _gather_small(g_w1, g_w2, g_w3, g_lb, g_wn, loss):
    def body(w1_ref, w2_ref, w3_ref, lb_ref, wn_ref, loss_ref, out_ref, pk, send_sems, recv_sems):
        x, y, c = _pos()
        me = 4 * x + 2 * y + c
        pk[...] = jnp.zeros_like(pk)
        pk[0:1, :] = w1_ref[...]
        pk[1:2, :] = w2_ref[...]
        pk[2:3, :] = w3_ref[...]
        pk[3:4, 0:HGRN_W] = lb_ref[...]
        pk[3:4, HGRN_W:2 * HGRN_W] = wn_ref[...]
        pk[4:5, 0:128] = loss_ref[...]
        out_ref[me] = pk[...]
        sends, recvs = [], []
        for k in range(1, N_DEV):
            peer = (_flip(x, k >> 2), _flip(y, (k >> 1) & 1), _flip(c, k & 1))
            cp = pltpu.make_async_remote_copy(src_ref=pk, dst_ref=out_ref.at[me], send_sem=send_sems.at[k - 1],
                                              recv_sem=recv_sems.at[k - 1], device_id=peer, device_id_type=MESH)
            cp.start()
            sends.append(cp)
            recvs.append(pltpu.make_async_remote_copy(src_ref=pk, dst_ref=out_ref.at[4 * peer[0] + 2 * peer[1] + peer[2]],
                                                      send_sem=send_sems.at[k - 1], recv_sem=recv_sems.at[k - 1], device_id=peer,
                                                      device_id_type=MESH))
        for cp in recvs:
            cp.wait_recv()
        for cp in sends:
            cp.wait_send()

    return pl.pallas_call(
        body, name="gather_small", in_specs=[VMEM_SPEC] * 6, out_specs=VMEM_SPEC,
        out_shape=jax.ShapeDtypeStruct((N_DEV, 8, D_MODEL), f32),
        scratch_shapes=[pltpu.VMEM((8, D_MODEL), f32), pltpu.SemaphoreType.DMA((N_DEV - 1,)), pltpu.SemaphoreType.DMA((N_DEV - 1,))],
    )(g_w1, g_w2, g_w3, g_lb, g_wn, loss)


def _row_tile(r):
    return max(t for t in range(8, 257, 8) if r % t == 0)


def _add_sibling(core, g, got, name):
    _, r, c = got.shape
    tr = _row_tile(r)

    def body(core_ref, a_ref, b_ref, o_ref):
        o_ref[...] = (a_ref[...] + b_ref[...]).astype(bf16)

    blk = pl.BlockSpec((1, tr, c), lambda q, i, core_ref: (q, i, 0))
    return pl.pallas_call(
        body, name=name, out_shape=jax.ShapeDtypeStruct(got.shape, bf16),
        grid_spec=pltpu.PrefetchScalarGridSpec(
            num_scalar_prefetch=1, grid=(4, r // tr),
            in_specs=[pl.BlockSpec((1, tr, c), lambda q, i, core_ref: (2 * q + core_ref[0], i, 0)), blk], out_specs=blk),
        compiler_params=_cp("arbitrary", "arbitrary"))(core, g, got)


def _adamw(w, g, m, v):
    m = ADAM_B1 * m + (1.0 - ADAM_B1) * g
    v = ADAM_B2 * v + (1.0 - ADAM_B2) * (g * g)
    m_hat = m / (1.0 - ADAM_B1 ** ADAM_STEP)
    v_hat = v / (1.0 - ADAM_B2 ** ADAM_STEP)
    return -ADAM_LR * (m_hat / (jnp.sqrt(v_hat) + ADAM_EPS) + ADAM_WD * w), m, v


def _adam_shard(where, g, got, pieces, w, m, v, name):
    r, c = w.shape
    tr = _row_tile(r)

    def body(where_ref, g_ref, got_ref, p_ref, w_ref, m_ref, v_ref, g_out, d_out, m_out, v_out):
        gsum = g_ref[0] + got_ref[0]
        for f in range(3):
            gsum = gsum + p_ref[f].astype(f32)
        g_out[...] = gsum
        d_out[...], m_out[...], v_out[...] = _adamw(w_ref[...], gsum, m_ref[...], v_ref[...])

    blk = pl.BlockSpec((tr, c), lambda i, where_ref: (i, 0))
    return pl.pallas_call(
        body, name=name, out_shape=[jax.ShapeDtypeStruct((r, c), f32)] * 4,
        grid_spec=pltpu.PrefetchScalarGridSpec(
            num_scalar_prefetch=1, grid=(r // tr,),
            in_specs=[pl.BlockSpec((1, tr, c), lambda i, where_ref: (where_ref[0], i, 0)),
                      pl.BlockSpec((1, tr, c), lambda i, where_ref: (where_ref[1], i, 0)),
                      pl.BlockSpec((3, tr, c), lambda i, where_ref: (0, i, 0)), blk, blk, blk],
            out_specs=[blk] * 4),
        compiler_params=_cp("arbitrary"),
    )(where, g, got, pieces, w, m, v)


def _small_update(gath, params):
    def body(gath_ref, *refs):
        ins, outs = refs[:15], refs[15:]
        gs = gath_ref[0]
        for k in range(1, N_DEV):
            gs = gs + gath_ref[k]
        outs[0][...] = gs[4:5, 0:128]
        l0, l1 = ins[9][0:1, :], ins[9][1:2, :]
        lb = _sigmoid(l0 - l1)
        d0 = gs[3:4, 0:HGRN_W] * lb * (1.0 - lb)
        first_row = lax.broadcasted_iota(jnp.int32, (2, HGRN_W), 0) == 0
        grads = [gs[0:1, :], gs[1:2, :], gs[2:3, :], jnp.where(first_row, d0, -d0), gs[3:4, HGRN_W:2 * HGRN_W]]
        for i, g in enumerate(grads):
            w_ref, m_ref, v_ref = ins[3 * i:3 * i + 3]
            o = outs[1 + 4 * i:5 + 4 * i]
            o[0][...] = g
            o[1][...], o[2][...], o[3][...] = _adamw(w_ref[...], g, m_ref[...], v_ref[...])

    flat = [a for p in params for a in p]
    out_shape = [jax.ShapeDtypeStruct((1, 128), f32)] + [jax.ShapeDtypeStruct(p[0].shape, f32) for p in params for _ in range(4)]
    outs = pl.pallas_call(body, name="small_update", in_specs=[VMEM_SPEC] * 16, out_specs=[VMEM_SPEC] * 21, out_shape=out_shape)(gath, *flat)
    return outs[0], [outs[1 + 4 * i:5 + 4 * i] for i in range(5)]


def kernel(x, norm1_w, w_in, lb_logits, hgrn_norm_w, w_out, norm2_w, w_gate_up, w_down, final_norm_w, loss_target, m_norm1_w, m_w_in, m_lb_logits, m_hgrn_norm_w, m_w_out, m_norm2_w, m_w_gate_up, m_w_down, m_final_norm_w, v_norm1_w, v_w_in, v_lb_logits, v_hgrn_norm_w, v_w_out, v_norm2_w, v_w_gate_up, v_w_down, v_final_norm_w):
    row = lambda a: a.reshape(1, D_MODEL)
    ix, iy, ic = lax.axis_index("x"), lax.axis_index("y"), lax.axis_index("c")
    core = jnp.stack([ic]).astype(jnp.int32)
    where = jnp.stack([4 * ix + 2 * iy + ic, 2 * ix + iy]).astype(jnp.int32)
    xs, tgt, w3 = x[0], loss_target[0], row(final_norm_w)
    S = xs.shape[0]

    cos_t, sg_t, win_g = _rope_tables(S, _gather_rider([w_in[0].astype(bf16)]))
    u, qkv, hp, win = _in_proj(xs, norm1_w, win_g, cos_t, sg_t)
    ya, lse, wout_g, wgu_g, wdown_g = _attn_fwd(qkv, _gather_rider([w_out[0].astype(bf16), w_gate_up[0].astype(bf16),
                                                                     w_down[0].astype(bf16)]))
    wout = wout_g.reshape(D_MODEL, D_MODEL)
    wdown = wdown_g.reshape(FFN, D_MODEL)
    yb, o_sav, states = _hgrn_fwd(hp, lb_logits, hgrn_norm_w)
    h1, u2, mixed = _out_proj(xs, ya, yb, wout, norm2_w)
    gate, up, act, wgate, wup = _gate_up(u2, wgu_g)
    dh2, loss_p, g_w3 = _down_loss(act, wdown, h1, tgt, w3)

    (g_wdown,) = _gw(act, [dh2], 512, "gw_down")
    dgu2 = _dact(dh2, wdown, gate, up)
    early = [_gw_by_owner(u2, dgu2, 2 * FFN // N_DEV, "gw_gate_up", 2048), g_wdown.reshape(N_DEV, FFN // N_DEV, D_MODEL)]
    dh1, g_w2, dmix, *got_early = _dgu(dgu2, wgate, wup, h1, norm2_w, dh2, wout, _sibling_rider(early))
    sums_early = [_add_sibling(core, g, o, f"add_sibling_{i}") for i, (g, o) in enumerate(zip(early, got_early))]
    (g_wout,) = _gw(mixed, [dh1], 1024, "gw_out")
    mid = [g_wout.reshape(N_DEV, D_MODEL // N_DEV, D_MODEL)]
    dhq, dhf, dhi, dhg, g_wn, g_lb, *rode = _hgrn_bwd(hp, lb_logits, hgrn_norm_w, o_sav, states, dmix,
                                                      _both(_chips_rider(sums_early), _sibling_rider(mid)))
    pieces_early, got_mid = rode[:2], rode[2:]
    sums_mid = [_add_sibling(core, mid[0], got_mid[0], "add_sibling_2")]
    dq, dk, dv, *pieces_mid = _attn_bwd(qkv, ya, lse, dmix, _chips_rider(sums_mid))
    dproj, gx, g_w1 = _din(dq, dk, dv, dhq, dhf, dhi, dhg, cos_t, sg_t, win, xs, norm1_w, dh1)
    late = [_gw_by_owner(u, dproj[None], IN_W // N_DEV, "gw_in", 2048)]
    got_late = _alone(_sibling_rider(late), "reduce_sibling")
    sums_late = [_add_sibling(core, late[0], got_late[0], "add_sibling_3")]
    pieces_late = _alone(_chips_rider(sums_late), "reduce_chips")

    grads = [late[0], mid[0], early[0], early[1]]
    got = [got_late[0], got_mid[0], got_early[0], got_early[1]]
    pieces = [pieces_late[0], pieces_mid[0], pieces_early[0], pieces_early[1]]
    shards = [w_in[0], w_out[0], w_gate_up[0], w_down[0]]
    moms = [(m_w_in[0], v_w_in[0]), (m_w_out[0], v_w_out[0]), (m_w_gate_up[0], v_w_gate_up[0]), (m_w_down[0], v_w_down[0])]
    big = [_adam_shard(where, g, o, p, w, m, v, f"adam_{i}")
           for i, (g, o, p, w, (m, v)) in enumerate(zip(grads, got, pieces, shards, moms))]
    big = [[a[None] for a in four] for four in big]

    gath = _gather_small(g_w1, g_w2, g_w3, g_lb, g_wn, loss_p)
    params = [(norm1_w, m_norm1_w, v_norm1_w), (norm2_w, m_norm2_w, v_norm2_w),
              (row(final_norm_w), row(m_final_norm_w), row(v_final_norm_w)),
              (lb_logits, m_lb_logits, v_lb_logits), (hgrn_norm_w, m_hgrn_norm_w, v_hgrn_norm_w)]
    loss, (s_w1, s_w2, s_w3, s_lb, s_wn) = _small_update(gath, params)
    s_w3 = [a.reshape(D_MODEL) for a in s_w3]
    per_w = [s_w1, big[0], s_lb, s_wn, big[1], s_w2, big[2], big[3], s_w3]
    return (loss[0, 0], gx[None], *[p[0] for p in per_w], *[p[1] for p in per_w], *[p[2] for p in per_w], *[p[3] for p in per_w])
```

```python
import jax
import jax.numpy as jnp
from jax import lax
from jax.experimental import pallas as pl
from jax.experimental.pallas import tpu as pltpu

f32, bf16 = jnp.float32, jnp.bfloat16

D_MODEL = 1024
ATTN_W = 512
HEAD_DIM = 64
ATTN_BLK = 128
DILATIONS = (1, 4, 16)
HGRN_W = 512
HGRN_HD = 128
CHUNK = 64
IN_W = 3 * ATTN_W + 4 * HGRN_W
FFN = 2816
EPS = 1e-6
ROPE_THETA = 10000.0
NEG = -1e30
N_DEV = 8
ADAM_LR, ADAM_B1, ADAM_B2, ADAM_EPS, ADAM_WD, ADAM_STEP = 0.001, 0.9, 0.999, 1e-08, 0.01, 10
VMEM_LIMIT = 56 * 1024 * 1024


def _cp(*sem):
    return pltpu.CompilerParams(dimension_semantics=sem, vmem_limit_bytes=VMEM_LIMIT)


def _dot(a, b):
    return jnp.dot(a, b, preferred_element_type=f32)


def _dot_nt(a, b):
    return lax.dot_general(a, b, (((1,), (1,)), ((), ())), preferred_element_type=f32)


def _dot_tn(a, b):
    return lax.dot_general(a, b, (((0,), (0,)), ((), ())), preferred_element_type=f32)


def _sigmoid(x):
    return 0.5 * jnp.tanh(0.5 * x) + 0.5


class _Rider:
    def __init__(self, ins, out_shapes, scratch, first, last, middle=None):
        self.ins, self.out_shapes, self.scratch = list(ins), list(out_shapes), list(scratch)
        self.first, self.middle, self.last = first, middle, last


def _ride(call, rider, body, step, n_steps, n_in, n_out, n_scratch):
    if rider is None:
        return call, body, []
    ri, ro = len(rider.ins), len(rider.out_shapes)
    any_spec = pl.BlockSpec(memory_space=pl.ANY)
    call = dict(call, in_specs=call["in_specs"] + [any_spec] * ri, out_specs=call["out_specs"] + [any_spec] * ro,
                out_shape=call["out_shape"] + rider.out_shapes, scratch_shapes=call["scratch_shapes"] + rider.scratch)

    def riding(*refs):
        a = n_in + ri
        b = a + n_out + ro
        mine = refs[:n_in] + refs[a:a + n_out] + refs[b:b + n_scratch]
        theirs = (refs[n_in:a], refs[a + n_out:b], refs[b + n_scratch:])
        t = step()

        @pl.when(t == 0)
        def _():
            rider.first(*theirs)

        body(*mine)
        if rider.middle is not None:
            @pl.when(t == n_steps // 2)
            def _():
                rider.middle(*theirs)

        @pl.when(t == n_steps - 1)
        def _():
            rider.last(*theirs)

    return call, riding, rider.ins


def _prologue(x, w1, rider=None):
    S = x.shape[0]
    half = HEAD_DIM // 2
    tm = 256
    inv_freq = jnp.tile(ROPE_THETA ** (-jnp.arange(half, dtype=f32) / half), 128 // half).reshape(1, 128)
    sign = jnp.tile(jnp.concatenate([-jnp.ones((half,), f32), jnp.ones((half,), f32)]), 128 // HEAD_DIM).reshape(1, 128)

    def body(x_ref, w1_ref, inv_ref, sign_ref, u_ref, cos_ref, sg_ref):
        xv = x_ref[...]
        r = lax.rsqrt(jnp.mean(xv * xv, axis=-1, keepdims=True) + EPS)
        u_ref[...] = (xv * r * w1_ref[...]).astype(bf16)
        pos = (lax.broadcasted_iota(jnp.int32, (tm, 128), 0) + pl.program_id(0) * tm).astype(f32)
        ang = pos * inv_ref[...]
        cos_ref[...] = jnp.cos(ang)
        sg_ref[...] = jnp.sin(ang) * sign_ref[...]

    vec = pl.BlockSpec((1, 128), lambda i: (0, 0))
    row = lambda w: pl.BlockSpec((tm, w), lambda i: (i, 0))
    call = dict(in_specs=[row(D_MODEL), pl.BlockSpec((1, D_MODEL), lambda i: (0, 0)), vec, vec], out_specs=[row(D_MODEL), row(128), row(128)],
                out_shape=[jax.ShapeDtypeStruct((S, D_MODEL), bf16)] + [jax.ShapeDtypeStruct((S, 128), f32)] * 2, scratch_shapes=[])
    call, body, more = _ride(call, rider, body, lambda: pl.program_id(0), S // tm, 4, 3, 0)
    return pl.pallas_call(body, name="prologue", grid=(S // tm,), compiler_params=_cp("arbitrary"), **call)(x, w1, inv_freq, sign, *more)


def _swap_halves(v):
    n = v.shape[1]
    lane = lax.broadcasted_iota(jnp.int32, v.shape, 1)
    return jnp.where((lane % HEAD_DIM) < HEAD_DIM // 2, pltpu.roll(v, n - HEAD_DIM // 2, 1), pltpu.roll(v, HEAD_DIM // 2, 1))


def _in_proj(u, win_g, cos_t, sg_t):
    S = u.shape[0]
    tm = 256
    w = IN_W // N_DEV

    def body(u_in_ref, wg_ref, cos_ref, sg_ref, qkv_ref, hp_ref, w_ref):
        @pl.when(pl.program_id(0) == 0)
        def _():
            for d in range(N_DEV):
                w_ref[:, w * d:w * (d + 1)] = wg_ref[d]

        u = u_in_ref[...]
        cosv, sgv = jnp.tile(cos_ref[...], (1, ATTN_W // 128)), jnp.tile(sg_ref[...], (1, ATTN_W // 128))
        for j in range(3):
            pj = _dot(u, w_ref[:, j * ATTN_W:(j + 1) * ATTN_W])
            if j < 2:
                pj = pj * cosv + _swap_halves(pj) * sgv
            if j == 0:
                pj = pj * (HEAD_DIM ** -0.5)
            qkv_ref[:, j * ATTN_W:(j + 1) * ATTN_W] = pj.astype(bf16)
        for j in range(4):
            lo = 3 * ATTN_W + j * HGRN_W
            hp_ref[:, j * HGRN_W:(j + 1) * HGRN_W] = _dot(u, w_ref[:, lo:lo + HGRN_W])

    return pl.pallas_call(
        body, name="in_proj", grid=(S // tm,),
        in_specs=[pl.BlockSpec((tm, D_MODEL), lambda i: (i, 0)), pl.BlockSpec((N_DEV, D_MODEL, w), lambda i: (0, 0, 0)),
                  pl.BlockSpec((tm, 128), lambda i: (i, 0)), pl.BlockSpec((tm, 128), lambda i: (i, 0))],
        out_specs=[pl.BlockSpec((tm, 3 * ATTN_W), lambda i: (i, 0)), pl.BlockSpec((tm, 4 * HGRN_W), lambda i: (i, 0)),
                   pl.BlockSpec((D_MODEL, IN_W), lambda i: (0, 0))],
        out_shape=[jax.ShapeDtypeStruct((S, 3 * ATTN_W), bf16), jax.ShapeDtypeStruct((S, 4 * HGRN_W), f32),
                   jax.ShapeDtypeStruct((D_MODEL, IN_W), bf16)],
        compiler_params=_cp("arbitrary"),
    )(u, win_g, cos_t, sg_t)


def _head_masks():
    lane = lax.broadcasted_iota(jnp.int32, (ATTN_BLK, 128), 1)
    even = lane < HEAD_DIM
    return even, (even, jnp.logical_not(even))


def _pair_fwd(q2, k2, v2, bias):
    even, masks = _head_masks()
    outs, lses = [], []
    for e in range(2):
        qm = jnp.where(masks[e], q2, 0.0).astype(bf16)
        s = _dot_nt(qm, k2) + bias
        m = jnp.max(s, axis=-1, keepdims=True)
        pe = jnp.exp(s - m)
        lsum = jnp.sum(pe, axis=-1, keepdims=True)
        outs.append(_dot(pe.astype(bf16), v2) / lsum)
        lses.append(jnp.broadcast_to(m + jnp.log(lsum), (ATTN_BLK, 128)))
    return jnp.where(even, outs[0], outs[1]), jnp.where(even, lses[0], lses[1])


def _merge(y0, l0, y1, l1):
    mx = jnp.maximum(l0, l1)
    a, b = jnp.exp(l0 - mx), jnp.exp(l1 - mx)
    tot = a + b
    return (a * y0 + b * y1) / tot, mx + jnp.log(tot)


def _pair_bwd(q2, k2f, v2, dy2, lse2, delta2, bias):
    _, masks = _head_masks()
    k2 = k2f.astype(bf16)
    klane = lax.broadcasted_iota(jnp.int32, (2 * ATTN_BLK, 128), 1) < HEAD_DIM
    kmasks = (klane, jnp.logical_not(klane))
    dq2 = jnp.zeros((ATTN_BLK, 128), f32)
    pes, dss, qms, dyms = [], [], [], []
    for e in range(2):
        c0 = e * HEAD_DIM
        qm = jnp.where(masks[e], q2, 0.0).astype(bf16)
        km = jnp.where(kmasks[e], k2f, 0.0).astype(bf16)
        dym = jnp.where(masks[e], dy2, 0.0).astype(bf16)
        pe = jnp.exp(_dot_nt(qm, k2) + bias - lse2[:, c0:c0 + 1])
        ds = (pe * (_dot_nt(dym, v2) - delta2[:, c0:c0 + 1])).astype(bf16)
        dq2 = dq2 + _dot(ds, km)
        pes.append(pe.astype(bf16))
        dss.append(ds)
        qms.append(qm)
        dyms.append(dym)
    dv2 = _dot_tn(jnp.concatenate(pes, axis=0), jnp.concatenate(dyms, axis=0))
    dk2 = _dot_tn(jnp.concatenate(dss, axis=0), jnp.concatenate(qms, axis=0))
    return dq2, dk2, dv2


TOK = 2048


def _key_bias():
    qi = lax.broadcasted_iota(jnp.int32, (ATTN_BLK, 2 * ATTN_BLK), 0)
    kj = lax.broadcasted_iota(jnp.int32, (ATTN_BLK, 2 * ATTN_BLK), 1)
    delta = ATTN_BLK + qi - kj
    seen = (delta >= 0) & (delta <= ATTN_BLK)
    return jnp.where(seen, 0.0, NEG), jnp.where(seen & (kj >= ATTN_BLK), 0.0, NEG)


def _attn_fwd(qkv, rider=None):
    S = qkv.shape[0]
    nS = S // TOK

    def body(q_ref, kp_ref, kc_ref, vp_ref, vc_ref, y_ref, l_ref, qs, k2, v2, ay, al):
        n = pl.program_id(1)
        qs[...] = q_ref[...].astype(f32)
        k2[0:TOK] = kp_ref[...].astype(f32)
        k2[TOK:2 * TOK] = kc_ref[...].astype(f32)
        v2[0:TOK] = vp_ref[...].astype(f32)
        v2[TOK:2 * TOK] = vc_ref[...].astype(f32)
        bias_any, bias_first = _key_bias()

        def block(dil, r, b, step, last):
            start = r + pl.multiple_of(step * b, step)
            rows = pl.ds(start, ATTN_BLK, stride=dil) if dil > 1 else pl.ds(start, ATTN_BLK)
            keys = (pl.ds(TOK + start - step, 2 * ATTN_BLK, stride=dil) if dil > 1
                    else pl.ds(TOK + start - step, 2 * ATTN_BLK))
            bias = jnp.where((n == 0) & (b == 0), bias_first, bias_any)
            out, lse = _pair_fwd(qs[rows, :], k2[keys, :].astype(bf16), v2[keys, :].astype(bf16), bias)
            if dil < DILATIONS[-1]:
                out, lse = _merge(ay[rows, :], al[rows, :], out, lse)
            if last:
                y_ref[rows, :] = out
                l_ref[rows, :] = lse
            else:
                ay[rows, :] = out
                al[rows, :] = lse

        for dil in reversed(DILATIONS):
            def loop(i, carry, dil=dil):
                block(dil, i % dil, i // dil, ATTN_BLK * dil, dil == 1)
                return carry
            lax.fori_loop(0, TOK // ATTN_BLK, loop, 0, unroll=8)

    blk = (TOK, 128)
    cur = lambda c: pl.BlockSpec(blk, lambda p, n: (n, 4 * c + p))
    prv = lambda c: pl.BlockSpec(blk, lambda p, n: (jnp.maximum(n - 1, 0), 4 * c + p))
    out = pl.BlockSpec(blk, lambda p, n: (n, p))
    call = dict(in_specs=[cur(0), prv(1), cur(1), prv(2), cur(2)], out_specs=[out, out],
                out_shape=[jax.ShapeDtypeStruct((S, ATTN_W), f32)] * 2,
                scratch_shapes=[pltpu.VMEM(blk, f32), pltpu.VMEM((2 * TOK, 128), f32), pltpu.VMEM((2 * TOK, 128), f32),
                                pltpu.VMEM(blk, f32), pltpu.VMEM(blk, f32)])
    call, body, more = _ride(call, rider, body, lambda: pl.program_id(0) * nS + pl.program_id(1), (ATTN_W // 128) * nS, 5, 2, 5)
    return pl.pallas_call(body, name="attention_fwd", grid=(ATTN_W // 128, nS), compiler_params=_cp("arbitrary", "arbitrary"),
                          **call)(qkv, qkv, qkv, qkv, qkv, *more)


def _attn_bwd(qkv, ya, lse, dmix, rider=None):
    S = qkv.shape[0]
    nS = S // TOK

    def body(q_ref, kp_ref, kc_ref, vp_ref, vc_ref, y_ref, l_ref, dy_ref, dq_ref, dk_ref, dv_ref, qs, k2, v2, dk2, dv2, dqa, dl):
        n = pl.program_id(1)

        @pl.when(n == 0)
        def _():
            dk2[...] = jnp.zeros_like(dk2)
            dv2[...] = jnp.zeros_like(dv2)

        @pl.when(n < nS)
        def _():
            qs[...] = q_ref[...].astype(f32)
            k2[0:TOK] = kp_ref[...].astype(f32)
            k2[TOK:2 * TOK] = kc_ref[...].astype(f32)
            v2[0:TOK] = vp_ref[...].astype(f32)
            v2[TOK:2 * TOK] = vc_ref[...].astype(f32)
            li = lax.broadcasted_iota(jnp.int32, (128, 128), 0)
            lj = lax.broadcasted_iota(jnp.int32, (128, 128), 1)
            seg = jnp.where((li // HEAD_DIM) == (lj // HEAD_DIM), 1.0, 0.0).astype(bf16)
            bias_any, bias_first = _key_bias()

            def delta_rows(t, carry):
                rows = pl.ds(pl.multiple_of(256 * t, 256), 256)
                dyy = dy_ref[rows, :] * y_ref[rows, :]
                hi = dyy.astype(bf16)
                dl[rows, :] = _dot(hi, seg) + _dot((dyy - hi.astype(f32)).astype(bf16), seg)
                return carry

            lax.fori_loop(0, TOK // 256, delta_rows, 0)

            def block(dil, r, b, step, first_pattern, last):
                start = r + pl.multiple_of(step * b, step)
                rows = pl.ds(start, ATTN_BLK, stride=dil) if dil > 1 else pl.ds(start, ATTN_BLK)
                keys = (pl.ds(TOK + start - step, 2 * ATTN_BLK, stride=dil) if dil > 1
                        else pl.ds(TOK + start - step, 2 * ATTN_BLK))
                bias = jnp.where((n == 0) & (b == 0), bias_first, bias_any)
                dq2, dkk, dvv = _pair_bwd(qs[rows, :], k2[keys, :], v2[keys, :].astype(bf16), dy_ref[rows, :],
                                          l_ref[rows, :], dl[rows, :], bias)
                if last:
                    dq_ref[rows, :] = dqa[rows, :] + dq2
                elif first_pattern:
                    dqa[rows, :] = dq2
                else:
                    dqa[rows, :] += dq2
                dk2[keys, :] += dkk
                dv2[keys, :] += dvv

            for dil in reversed(DILATIONS):
                def loop(i, carry, dil=dil):
                    block(dil, i % dil, i // dil, ATTN_BLK * dil, dil == DILATIONS[-1], dil == 1)
                    return carry
                lax.fori_loop(0, TOK // ATTN_BLK, loop, 0, unroll=8)

        dk_ref[...] = dk2[0:TOK]
        dv_ref[...] = dv2[0:TOK]
        dk2[0:TOK] = dk2[TOK:2 * TOK]
        dv2[0:TOK] = dv2[TOK:2 * TOK]
        dk2[TOK:2 * TOK] = jnp.zeros((TOK, 128), f32)
        dv2[TOK:2 * TOK] = jnp.zeros((TOK, 128), f32)

    blk = (TOK, 128)
    cn = lambda n: jnp.minimum(n, nS - 1)
    pn = lambda n: jnp.clip(n - 1, 0, nS - 1)
    cur = lambda c: pl.BlockSpec(blk, lambda p, n: (cn(n), 4 * c + p))
    prv = lambda c: pl.BlockSpec(blk, lambda p, n: (pn(n), 4 * c + p))
    at_n = pl.BlockSpec(blk, lambda p, n: (cn(n), p))
    at_p = pl.BlockSpec(blk, lambda p, n: (pn(n), p))
    big = lambda: pltpu.VMEM((2 * TOK, 128), f32)
    call = dict(in_specs=[cur(0), prv(1), cur(1), prv(2), cur(2), at_n, at_n, at_n], out_specs=[at_n, at_p, at_p],
                out_shape=[jax.ShapeDtypeStruct((S, ATTN_W), f32)] * 3,
                scratch_shapes=[pltpu.VMEM(blk, f32), big(), big(), big(), big(), pltpu.VMEM(blk, f32), pltpu.VMEM(blk, f32)])
    call, body, more = _ride(call, rider, body, lambda: pl.program_id(0) * (nS + 1) + pl.program_id(1),
                             (ATTN_W // 128) * (nS + 1), 8, 3, 7)
    return pl.pallas_call(body, name="attention_bwd", grid=(ATTN_W // 128, nS + 1), compiler_params=_cp("arbitrary", "arbitrary"),
                          **call)(qkv, qkv, qkv, qkv, qkv, ya, lse, dmix, *more)


HG_T = 256
N_HH = HGRN_W // HGRN_HD
HG_SUB = 128
SAFE_RANGE = 75.0


def _row_in_chunk():
    return lax.broadcasted_iota(jnp.int32, (HG_T, HGRN_HD), 0) % CHUNK


def _chunk_cumsum(v, rc):
    k = 1
    while k < CHUNK:
        v = v + jnp.where(rc >= k, pltpu.roll(v, k, 0), 0.0)
        k *= 2
    return v


def _chunk_rcumsum(v, rc):
    k = 1
    while k < CHUNK:
        v = v + jnp.where(rc < CHUNK - k, pltpu.roll(v, HG_T - k, 0), 0.0)
        k *= 2
    return v


def _hgrn_gates(qb, fb, lb):
    sf = _sigmoid(fb)
    f = lb + (1.0 - lb) * sf
    sq = _sigmoid(qb)
    return sf, f, jnp.log(f), 1.0 - f, sq, qb * sq


def _hgrn_prep(qb, fb, lbl2, rc):
    lb = _sigmoid(lbl2[0:1, :] - lbl2[1:2, :])
    sf, f, lf, key, sq, qf = _hgrn_gates(qb, fb, lb)
    b = _chunk_cumsum(lf, rc)
    rem = _chunk_rcumsum(lf, rc) - lf
    return dict(lb=lb, sf=sf, f=f, key=key, sq=sq, qf=qf, b=b, rem=rem, eb=jnp.exp(b), er=jnp.exp(rem))


def _chunk_mask():
    r = lax.broadcasted_iota(jnp.int32, (HG_SUB, HG_SUB), 0)
    c = lax.broadcasted_iota(jnp.int32, (HG_SUB, HG_SUB), 1)
    return ((r // CHUNK) == (c // CHUNK)) & (c <= r)


def _hgrn_fwd(hp, lbl, wn):
    S = hp.shape[0]
    nT = S // HG_T

    def body(qb_ref, fb_ref, ib_ref, gb_ref, lbl_ref, wn_ref, yb_ref, o_ref, st_ref, ST, qt_s, kh_s, dec_s, oi_s):
        @pl.when(pl.program_id(0) == 0)
        def _():
            ST[...] = jnp.zeros_like(ST)

        rc = _row_in_chunk()
        for h in range(N_HH):
            sl = slice(HGRN_HD * h, HGRN_HD * (h + 1))
            p = _hgrn_prep(qb_ref[:, sl], fb_ref[:, sl], lbl_ref[:, sl], rc)
            qf, key, b = p["qf"], p["key"], p["b"]
            qt = qf * p["eb"]
            qt_s[:, sl] = qt.astype(bf16)
            kh_s[:, sl] = (key * p["er"]).astype(bf16)
            dec_s[:, sl] = jnp.exp(b + p["rem"])
            rng = jnp.max(-(b + p["rem"]))

            @pl.when(rng < SAFE_RANGE)
            def _():
                kp = (key * jnp.exp(-b)).astype(bf16)
                cmask = _chunk_mask()
                for j in range(HG_T // HG_SUB):
                    rs = slice(HG_SUB * j, HG_SUB * (j + 1))
                    sc = jnp.where(cmask, _dot_nt(qt[rs].astype(bf16), kp[rs]), 0.0).astype(bf16)
                    oi_s[rs, sl] = _dot(sc, ib_ref[rs, sl].astype(bf16))

            @pl.when(rng >= SAFE_RANGE)
            def _():
                v = ib_ref[:, sl]
                ones = jnp.ones((HGRN_HD, HGRN_HD), bf16)

                def lag(l, o):
                    e = jnp.exp(jnp.where(rc >= l, b - pltpu.roll(b, l, 0), NEG))
                    pr = qf * pltpu.roll(key, l, 0) * e
                    return o + _dot(pr.astype(bf16), ones) * pltpu.roll(v, l, 0)

                oi_s[:, sl] = lax.fori_loop(1, CHUNK, lag, _dot((qf * key).astype(bf16), ones) * v)

        def step(c, carry):
            rows = pl.ds(pl.multiple_of(c * CHUNK, CHUNK), CHUNK)
            row0 = pl.ds(pl.multiple_of(c * CHUNK, CHUNK), 1)
            for h in range(N_HH):
                sl = slice(HGRN_HD * h, HGRN_HD * (h + 1))
                stv = ST[h]
                st_ref[c, sl, :] = stv
                oi_s[rows, sl] += _dot_nt(qt_s[rows, sl], stv.astype(bf16))
                ST[h] = stv * dec_s[row0, sl] + _dot_tn(ib_ref[rows, sl].astype(bf16), kh_s[rows, sl])
            return carry

        lax.fori_loop(0, HG_T // CHUNK, step, 0, unroll=True)

        for h in range(N_HH):
            sl = slice(HGRN_HD * h, HGRN_HD * (h + 1))
            o = oi_s[:, sl]
            o_ref[:, sl] = o
            on = o * lax.rsqrt(jnp.mean(o * o, axis=-1, keepdims=True) + EPS)
            g = gb_ref[:, sl]
            yb_ref[:, sl] = on * wn_ref[:, sl] * (g * _sigmoid(g))

    col = lambda c: pl.BlockSpec((HG_T, HGRN_W), lambda i: (i, c))
    tile = pl.BlockSpec((HG_T, HGRN_W), lambda i: (i, 0))
    whole = lambda a: pl.BlockSpec(a.shape, lambda i: (0, 0))
    return pl.pallas_call(
        body, name="hgrn_fwd", grid=(nT,),
        in_specs=[col(0), col(1), col(2), col(3), whole(lbl), whole(wn)],
        out_specs=[tile, tile, pl.BlockSpec((HG_T // CHUNK, HGRN_W, HGRN_HD), lambda i: (i, 0, 0))],
        out_shape=[jax.ShapeDtypeStruct((S, HGRN_W), f32), jax.ShapeDtypeStruct((S, HGRN_W), f32),
                   jax.ShapeDtypeStruct((S // CHUNK, HGRN_W, HGRN_HD), f32)],
        scratch_shapes=[pltpu.VMEM((N_HH, HGRN_HD, HGRN_HD), f32), pltpu.VMEM((HG_T, HGRN_W), bf16),
                        pltpu.VMEM((HG_T, HGRN_W), bf16), pltpu.VMEM((HG_T, HGRN_W), f32), pltpu.VMEM((HG_T, HGRN_W), f32)],
        compiler_params=_cp("arbitrary"),
    )(hp, hp, hp, hp, lbl, wn)


def _hgrn_bwd(hp, lbl, wn, o_sav, states, dmix, rider=None):
    S = hp.shape[0]
    nT = S // HG_T

    def body(qb_ref, fb_ref, ib_ref, gb_ref, lbl_ref, wn_ref, o_ref, st_ref, dy_ref,
             dq_ref, df_ref, di_ref, dg_ref, gwn_ref, glb_ref,
             DST, qt_s, kh_s, dec_s, do_s, dqt_s, dkh_s, dbl_s, dvi_s, dqi_s, dki_s, dbi_s):
        @pl.when(pl.program_id(0) == 0)
        def _():
            DST[...] = jnp.zeros_like(DST)
            gwn_ref[...] = jnp.zeros_like(gwn_ref)
            glb_ref[...] = jnp.zeros_like(glb_ref)

        rc = _row_in_chunk()
        preps = []
        for h in range(N_HH):
            sl = slice(HGRN_HD * h, HGRN_HD * (h + 1))
            p = _hgrn_prep(qb_ref[:, sl], fb_ref[:, sl], lbl_ref[:, sl], rc)
            preps.append(p)
            qf, key, b = p["qf"], p["key"], p["b"]
            v = ib_ref[:, sl]
            o = o_ref[:, sl]
            rinv = lax.rsqrt(jnp.mean(o * o, axis=-1, keepdims=True) + EPS)
            on = o * rinv
            g = gb_ref[:, sl]
            sgm = _sigmoid(g)
            silu_g = g * sgm
            dy = dy_ref[:, sl]
            wn_v = wn_ref[:, sl]
            gwn_ref[:, sl] += jnp.sum(dy * on * silu_g, axis=0, keepdims=True)
            dg_ref[:, sl] = (dy * on * wn_v * (sgm * (1.0 + g * (1.0 - sgm)))).astype(bf16)
            t1 = dy * wn_v * silu_g
            do = rinv * (t1 - on * jnp.mean(t1 * on, axis=-1, keepdims=True))
            do_s[:, sl] = do.astype(bf16)
            qt = qf * p["eb"]
            qt_s[:, sl] = qt.astype(bf16)
            kh_s[:, sl] = (key * p["er"]).astype(bf16)
            dec_s[:, sl] = jnp.exp(b + p["rem"])
            rng = jnp.max(-(b + p["rem"]))

            @pl.when(rng < SAFE_RANGE)
            def _():
                einv = jnp.exp(-b)
                kp = (key * einv).astype(bf16)
                cmask = _chunk_mask()
                for j in range(HG_T // HG_SUB):
                    rs = slice(HG_SUB * j, HG_SUB * (j + 1))
                    qtb, dob, vb = qt[rs].astype(bf16), do[rs].astype(bf16), v[rs].astype(bf16)
                    sc = jnp.where(cmask, _dot_nt(qtb, kp[rs]), 0.0).astype(bf16)
                    dsc = jnp.where(cmask, _dot_nt(dob, vb), 0.0).astype(bf16)
                    dqp = _dot(dsc, kp[rs])
                    dkp = _dot_tn(dsc, qtb)
                    dvi_s[rs, sl] = _dot_tn(sc, dob)
                    dqi_s[rs, sl] = dqp * p["eb"][rs]
                    dki_s[rs, sl] = dkp * einv[rs]
                    dbi_s[rs, sl] = dqp * qtb.astype(f32) - dkp * kp[rs].astype(f32)

            @pl.when(rng >= SAFE_RANGE)
            def _():
                ones = jnp.ones((HGRN_HD, HGRN_HD), bf16)

                def lag(l, carry):
                    dqf, dkey, db, dv = carry
                    e = jnp.exp(jnp.where(rc >= l, b - pltpu.roll(b, l, 0), NEG))
                    ks, vs, qe = pltpu.roll(key, l, 0), pltpu.roll(v, l, 0), qf * e
                    pr = qe * ks
                    rl = _dot(pr.astype(bf16), ones)
                    drl = jnp.where(rc >= l, _dot((do * vs).astype(bf16), ones), 0.0)
                    gl = drl * pr
                    back = HG_T - l
                    return (dqf + drl * ks * e, dkey + pltpu.roll(drl * qe, back, 0), db + gl - pltpu.roll(gl, back, 0),
                            dv + pltpu.roll(rl * do, back, 0))

                rl0 = _dot((qf * key).astype(bf16), ones)
                drl0 = _dot((do * v).astype(bf16), ones)
                dqf, dkey, db, dv = lax.fori_loop(1, CHUNK, lag, (drl0 * key, drl0 * qf, jnp.zeros((HG_T, HGRN_HD), f32), rl0 * do))
                dvi_s[:, sl] = dv
                dqi_s[:, sl] = dqf
                dki_s[:, sl] = dkey
                dbi_s[:, sl] = db

        def step(k, carry):
            c = HG_T // CHUNK - 1 - k
            rows = pl.ds(pl.multiple_of(c * CHUNK, CHUNK), CHUNK)
            row0 = pl.ds(pl.multiple_of(c * CHUNK, CHUNK), 1)
            for h in range(N_HH):
                sl = slice(HGRN_HD * h, HGRN_HD * (h + 1))
                stp = st_ref[c, sl, :]
                dst = DST[h]
                dstb = dst.astype(bf16)
                dob = do_s[rows, sl]
                khb = kh_s[rows, sl]
                dec = dec_s[row0, sl]
                dqt_s[rows, sl] = _dot(dob, stp.astype(bf16))
                dkh = _dot(ib_ref[rows, sl].astype(bf16), dstb)
                dkh_s[rows, sl] = dkh
                dvi_s[rows, sl] += _dot_nt(khb, dstb)
                dbl = jnp.sum(dst * stp, axis=0, keepdims=True) * dec + jnp.sum(dkh * khb.astype(f32), axis=0, keepdims=True)
                dbl_s[rows, sl] = jnp.broadcast_to(dbl, (CHUNK, HGRN_HD))
                DST[h] = dst * dec + _dot_tn(dob, qt_s[rows, sl])
            return carry

        lax.fori_loop(0, HG_T // CHUNK, step, 0, unroll=True)

        for h in range(N_HH):
            sl = slice(HGRN_HD * h, HGRN_HD * (h + 1))
            qb = qb_ref[:, sl]
            p = preps[h]
            sf, sq, lb = p["sf"], p["sq"], p["lb"]
            dqt, dkh = dqt_s[:, sl], dkh_s[:, sl]
            dqf = dqt * p["eb"] + dqi_s[:, sl]
            dkey = dkh * p["er"] + dki_s[:, sl]
            db = dqt * (p["qf"] * p["eb"]) - dkh * (p["key"] * p["er"]) + jnp.where(rc == CHUNK - 1, dbl_s[:, sl], 0.0) + dbi_s[:, sl]
            df = _chunk_rcumsum(db, rc) / p["f"] - dkey
            df_ref[:, sl] = (df * (1.0 - lb) * sf * (1.0 - sf)).astype(bf16)
            glb_ref[:, sl] += jnp.sum(df * (1.0 - sf), axis=0, keepdims=True)
            dq_ref[:, sl] = (dqf * (sq * (1.0 + qb * (1.0 - sq)))).astype(bf16)
            di_ref[:, sl] = dvi_s[:, sl].astype(bf16)

    rev = lambda i: nT - 1 - i
    col = lambda c: pl.BlockSpec((HG_T, HGRN_W), lambda i: (rev(i), c))
    tile = pl.BlockSpec((HG_T, HGRN_W), lambda i: (rev(i), 0))
    whole = lambda a: pl.BlockSpec(a.shape, lambda i: (0, 0))
    vec = pl.BlockSpec((1, HGRN_W), lambda i: (0, 0))
    tb = lambda: pltpu.VMEM((HG_T, HGRN_W), bf16)
    tf = lambda: pltpu.VMEM((HG_T, HGRN_W), f32)
    call = dict(in_specs=[col(0), col(1), col(2), col(3), whole(lbl), whole(wn), tile,
                          pl.BlockSpec((HG_T // CHUNK, HGRN_W, HGRN_HD), lambda i: (rev(i), 0, 0)),
                          pl.BlockSpec((HG_T, HGRN_W), lambda i: (rev(i), 1))],
                out_specs=[tile, tile, tile, tile, vec, vec],
                out_shape=[jax.ShapeDtypeStruct((S, HGRN_W), bf16)] * 4 + [jax.ShapeDtypeStruct((1, HGRN_W), f32)] * 2,
                scratch_shapes=[pltpu.VMEM((N_HH, HGRN_HD, HGRN_HD), f32), tb(), tb(), tf(), tb(), tf(), tf(), tf(), tf(), tf(),
                                tf(), tf()])
    call, body, more = _ride(call, rider, body, lambda: pl.program_id(0), nT, 9, 6, 12)
    return pl.pallas_call(body, name="hgrn_bwd", grid=(nT,), compiler_params=_cp("arbitrary"), **call)(
        hp, hp, hp, hp, lbl, wn, o_sav, states, dmix, *more)


def _out_proj(x, ya, yb, wout, w2):
    S = x.shape[0]
    tm = 512

    def body(x_ref, ya_ref, yb_ref, w_ref, w2_ref, h1_ref, u2_ref, mix_ref):
        mixed = jnp.concatenate([ya_ref[...], yb_ref[...]], axis=1).astype(bf16)
        mix_ref[...] = mixed
        h1 = x_ref[...] + _dot(mixed, w_ref[...])
        h1_ref[...] = h1
        r = lax.rsqrt(jnp.mean(h1 * h1, axis=-1, keepdims=True) + EPS)
        u2_ref[...] = (h1 * r * w2_ref[...]).astype(bf16)

    row = lambda w: pl.BlockSpec((tm, w), lambda i: (i, 0))
    return pl.pallas_call(
        body, name="out_proj", grid=(S // tm,),
        in_specs=[row(D_MODEL), row(ATTN_W), row(HGRN_W), pl.BlockSpec((D_MODEL, D_MODEL), lambda i: (0, 0)),
                  pl.BlockSpec((1, D_MODEL), lambda i: (0, 0))],
        out_specs=[row(D_MODEL), row(D_MODEL), row(D_MODEL)],
        out_shape=[jax.ShapeDtypeStruct((S, D_MODEL), f32), jax.ShapeDtypeStruct((S, D_MODEL), bf16),
                   jax.ShapeDtypeStruct((S, D_MODEL), bf16)],
        compiler_params=_cp("arbitrary"),
    )(x, ya, yb, wout, w2)


def _gate_up(u2, wgu_g):
    S = u2.shape[0]
    w = 2 * FFN // N_DEV
    tm, tn = 512, 2 * w
    nj = FFN // tn

    def body(u_ref, wgg_ref, wug_ref, g_ref, up_ref, a_ref, wg_ref, wu_ref):
        @pl.when(pl.program_id(1) == 0)
        def _():
            for k in range(2):
                wg_ref[:, w * k:w * (k + 1)] = wgg_ref[k]
                wu_ref[:, w * k:w * (k + 1)] = wug_ref[k]

        u = u_ref[...]
        g = _dot(u, wg_ref[...])
        up = _dot(u, wu_ref[...])
        g_ref[...] = g.astype(bf16)
        up_ref[...] = up.astype(bf16)
        a_ref[...] = (g * _sigmoid(g) * up).astype(bf16)

    out = pl.BlockSpec((tm, tn), lambda j, i: (i, j))
    wout = pl.BlockSpec((D_MODEL, tn), lambda j, i: (0, j))
    return pl.pallas_call(
        body, name="gate_up", grid=(nj, S // tm),
        in_specs=[pl.BlockSpec((tm, D_MODEL), lambda j, i: (i, 0)), pl.BlockSpec((2, D_MODEL, w), lambda j, i: (j, 0, 0)),
                  pl.BlockSpec((2, D_MODEL, w), lambda j, i: (j + nj, 0, 0))],
        out_specs=[out, out, out, wout, wout],
        out_shape=[jax.ShapeDtypeStruct((S, FFN), bf16)] * 3 + [jax.ShapeDtypeStruct((D_MODEL, FFN), bf16)] * 2,
        compiler_params=_cp("arbitrary", "arbitrary"),
    )(u2, wgu_g, wgu_g)


def _rms_bwd(dyw, hn, r):
    return r * (dyw - hn * jnp.mean(dyw * hn, axis=-1, keepdims=True))


def _down_loss(act, wdown, h1, tgt, w3):
    S = act.shape[0]
    tm = 256

    def body(a_ref, w_ref, h1_ref, t_ref, w3_ref, dh2_ref, loss_ref, gw3_ref):
        @pl.when(pl.program_id(0) == 0)
        def _():
            loss_ref[...] = jnp.zeros_like(loss_ref)
            gw3_ref[...] = jnp.zeros_like(gw3_ref)

        h2 = h1_ref[...] + _dot(a_ref[...], w_ref[...])
        r = lax.rsqrt(jnp.mean(h2 * h2, axis=-1, keepdims=True) + EPS)
        hn = h2 * r
        w3 = w3_ref[...]
        err = hn * w3 - t_ref[...]
        loss_ref[...] += (0.5 / D_MODEL) * jnp.sum(err * err)
        dy = err * (1.0 / D_MODEL)
        gw3_ref[...] += jnp.sum(dy * hn, axis=0, keepdims=True)
        dh2_ref[...] = _rms_bwd(dy * w3, hn, r)

    row = lambda w: pl.BlockSpec((tm, w), lambda i: (i, 0))
    return pl.pallas_call(
        body, name="down_loss", grid=(S // tm,),
        in_specs=[row(FFN), pl.BlockSpec((FFN, D_MODEL), lambda i: (0, 0)), row(D_MODEL), row(D_MODEL),
                  pl.BlockSpec((1, D_MODEL), lambda i: (0, 0))],
        out_specs=[row(D_MODEL), pl.BlockSpec((1, 128), lambda i: (0, 0)), pl.BlockSpec((1, D_MODEL), lambda i: (0, 0))],
        out_shape=[jax.ShapeDtypeStruct((S, D_MODEL), f32), jax.ShapeDtypeStruct((1, 128), f32),
                   jax.ShapeDtypeStruct((1, D_MODEL), f32)],
        compiler_params=_cp("arbitrary"),
    )(act, wdown, h1, tgt, w3)


def _dact(dh2, wdown, gate, up):
    S = dh2.shape[0]
    tm = 256

    def body(d_ref, w_ref, g_ref, u_ref, o_ref):
        da = _dot_nt(d_ref[...].astype(bf16), w_ref[...])
        g = g_ref[...].astype(f32)
        sg = _sigmoid(g)
        silu = g * sg
        o_ref[1] = (da * silu).astype(bf16)
        o_ref[0] = (da * u_ref[...].astype(f32) * (sg + silu - silu * sg)).astype(bf16)

    row = lambda w: pl.BlockSpec((tm, w), lambda i: (i, 0))
    return pl.pallas_call(
        body, name="dact", grid=(S // tm,),
        in_specs=[row(D_MODEL), pl.BlockSpec((FFN, D_MODEL), lambda i: (0, 0)), row(FFN), row(FFN)],
        out_specs=pl.BlockSpec((2, tm, FFN), lambda i: (0, i, 0)),
        out_shape=jax.ShapeDtypeStruct((2, S, FFN), bf16),
        compiler_params=_cp("arbitrary"),
    )(dh2, wdown, gate, up)


def _dgu(dgu2, wgate, wup, h1, w2, dh2, wout, rider=None):
    S = dgu2.shape[1]
    tm = 256

    def body(d_ref, wg_ref, wu_ref, h1_ref, w2_ref, dh2_ref, wo_ref, dh1_ref, gw2_ref, dmix_ref):
        @pl.when(pl.program_id(0) == 0)
        def _():
            gw2_ref[...] = jnp.zeros_like(gw2_ref)

        du2 = _dot_nt(d_ref[0], wg_ref[...]) + _dot_nt(d_ref[1], wu_ref[...])
        h1 = h1_ref[...]
        r = lax.rsqrt(jnp.mean(h1 * h1, axis=-1, keepdims=True) + EPS)
        hn = h1 * r
        gw2_ref[...] += jnp.sum(du2 * hn, axis=0, keepdims=True)
        dh1 = dh2_ref[...] + _rms_bwd(du2 * w2_ref[...], hn, r)
        dh1_ref[...] = dh1
        dmix_ref[...] = _dot_nt(dh1.astype(bf16), wo_ref[...])

    row = lambda w: pl.BlockSpec((tm, w), lambda i: (i, 0))
    call = dict(in_specs=[pl.BlockSpec((2, tm, FFN), lambda i: (0, i, 0)), pl.BlockSpec((D_MODEL, FFN), lambda i: (0, 0)),
                          pl.BlockSpec((D_MODEL, FFN), lambda i: (0, 0)), row(D_MODEL),
                          pl.BlockSpec((1, D_MODEL), lambda i: (0, 0)), row(D_MODEL),
                          pl.BlockSpec((D_MODEL, D_MODEL), lambda i: (0, 0))],
                out_specs=[row(D_MODEL), pl.BlockSpec((1, D_MODEL), lambda i: (0, 0)), row(D_MODEL)],
                out_shape=[jax.ShapeDtypeStruct((S, D_MODEL), f32), jax.ShapeDtypeStruct((1, D_MODEL), f32),
                           jax.ShapeDtypeStruct((S, D_MODEL), f32)], scratch_shapes=[])
    call, body, more = _ride(call, rider, body, lambda: pl.program_id(0), S // tm, 7, 3, 0)
    return pl.pallas_call(body, name="dgu", grid=(S // tm,), compiler_params=_cp("arbitrary"), **call)(
        dgu2, wgate, wup, h1, w2, dh2, wout, *more)


def _din(dq, dk, dv, dhq, dhf, dhi, dhg, cos_t, sg_t, win, x, w1, dh1):
    S = x.shape[0]
    tm = 256

    def body(dq_ref, dk_ref, dv_ref, dhq_ref, dhf_ref, dhi_ref, dhg_ref, cos_ref, sg_ref, w_ref, x_ref, w1_ref, dh1_ref,
             dp_ref, gx_ref, gw1_ref):
        @pl.when(pl.program_id(0) == 0)
        def _():
            gw1_ref[...] = jnp.zeros_like(gw1_ref)

        cosv, sgv = jnp.tile(cos_ref[...], (1, ATTN_W // 128)), jnp.tile(sg_ref[...], (1, ATTN_W // 128))
        unrope = lambda d: d * cosv - sgv * _swap_halves(d)
        parts = [(unrope(dq_ref[...]) * (HEAD_DIM ** -0.5)).astype(bf16), unrope(dk_ref[...]).astype(bf16),
                 dv_ref[...].astype(bf16), dhq_ref[...], dhf_ref[...], dhi_ref[...], dhg_ref[...]]
        du = jnp.zeros((tm, D_MODEL), f32)
        for j, pj in enumerate(parts):
            dp_ref[:, j * 512:(j + 1) * 512] = pj
            du = du + _dot_nt(pj, w_ref[:, j * 512:(j + 1) * 512])
        xv = x_ref[...]
        r = lax.rsqrt(jnp.mean(xv * xv, axis=-1, keepdims=True) + EPS)
        xn = xv * r
        gw1_ref[...] += jnp.sum(du * xn, axis=0, keepdims=True)
        gx_ref[...] = dh1_ref[...] + _rms_bwd(du * w1_ref[...], xn, r)

    row = lambda w: pl.BlockSpec((tm, w), lambda i: (i, 0))
    vec = pl.BlockSpec((1, D_MODEL), lambda i: (0, 0))
    return pl.pallas_call(
        body, name="din", grid=(S // tm,),
        in_specs=[row(512)] * 7 + [row(128), row(128), pl.BlockSpec((D_MODEL, IN_W), lambda i: (0, 0)), row(D_MODEL), vec,
                                   row(D_MODEL)],
        out_specs=[row(IN_W), row(D_MODEL), vec],
        out_shape=[jax.ShapeDtypeStruct((S, IN_W), bf16), jax.ShapeDtypeStruct((S, D_MODEL), f32),
                   jax.ShapeDtypeStruct((1, D_MODEL), f32)],
        compiler_params=_cp("arbitrary"),
    )(dq, dk, dv, dhq, dhf, dhi, dhg, cos_t, sg_t, win, x, w1, dh1)


def _gw(a, bs, tn, name, ts=2048):
    S, M = a.shape
    N = bs[0].shape[1]
    k = len(bs)

    def body(a_ref, *refs):
        @pl.when(pl.program_id(1) == 0)
        def _():
            for o_ref in refs[k:]:
                o_ref[...] = jnp.zeros_like(o_ref)

        at = a_ref[...].astype(bf16)
        for b_ref, o_ref in zip(refs[:k], refs[k:]):
            o_ref[...] += _dot_tn(at, b_ref[...].astype(bf16))

    return pl.pallas_call(
        body, name=name, grid=(N // tn, S // ts),
        in_specs=[pl.BlockSpec((ts, M), lambda j, s: (s, 0))] + [pl.BlockSpec((ts, tn), lambda j, s: (s, j))] * k,
        out_specs=[pl.BlockSpec((M, tn), lambda j, s: (0, j))] * k, out_shape=[jax.ShapeDtypeStruct((M, N), f32)] * k,
        compiler_params=_cp("arbitrary", "arbitrary"),
    )(a, *bs)


def _gw_by_owner(a, b3, w, name, ts):
    S, M = a.shape
    G, _, Ng = b3.shape
    tn = 2 * w
    per_group = Ng // tn
    n_s = S // ts

    def body(a_ref, b_ref, o_ref, acc):
        s = pl.program_id(1)

        @pl.when(s == 0)
        def _():
            acc[...] = jnp.zeros_like(acc)

        acc[...] += _dot_tn(a_ref[...].astype(bf16), b_ref[0].astype(bf16))

        @pl.when(s == n_s - 1)
        def _():
            o_ref[0] = acc[:, 0:w]
            o_ref[1] = acc[:, w:tn]

    return pl.pallas_call(
        body, name=name, grid=(G * per_group, n_s),
        in_specs=[pl.BlockSpec((ts, M), lambda j, s: (s, 0)),
                  pl.BlockSpec((1, ts, tn), lambda j, s: (j // per_group, s, j % per_group))],
        out_specs=pl.BlockSpec((2, M, w), lambda j, s: (j, 0, 0)), out_shape=jax.ShapeDtypeStruct((G * Ng // w, M, w), f32),
        scratch_shapes=[pltpu.VMEM((M, tn), f32)], compiler_params=_cp("arbitrary", "arbitrary"),
    )(a, b3)


MESH = pl.DeviceIdType.MESH
ANY = pl.BlockSpec(memory_space=pl.ANY)
VMEM_SPEC = pl.BlockSpec(memory_space=pltpu.VMEM)


def _pos():
    return lax.axis_index("x"), lax.axis_index("y"), lax.axis_index("c")


def _flip(v, bit):
    return 1 - v if bit else v


def _gather_rider(shards):
    n = len(shards)

    def parts(outs, scratch):
        send_sems, recv_sems, local_sems = scratch[n:]
        x, y, c = _pos()
        chips = [(1 - x, y), (x, 1 - y), (1 - x, 1 - y)]

        def copy(a, k, block, to, src=None):
            dst = outs[a].at[4 * block[0] + 2 * block[1] + block[2]]
            return pltpu.make_async_remote_copy(src_ref=dst if src is None else src, dst_ref=dst, send_sem=send_sems.at[a, k],
                                                recv_sem=recv_sems.at[a, k], device_id=to, device_id_type=MESH)

        bufs = scratch[:n]
        me, sibling = (x, y, c), (x, y, 1 - c)
        own = lambda a: pltpu.make_async_copy(bufs[a], outs[a].at[4 * x + 2 * y + c], local_sems.at[a])
        sent = lambda a: [copy(a, 0, me, sibling, src=bufs[a])] + [copy(a, 1 + j, me, (*chip, c), src=bufs[a])
                                                                   for j, chip in enumerate(chips)]
        passed = lambda a: [copy(a, 4 + j, (*chip, c), sibling) for j, chip in enumerate(chips)]
        landed = lambda a: [copy(a, 1 + j, (*chip, c), me) for j, chip in enumerate(chips)]
        from_sibling = lambda a: [copy(a, 0, sibling, me)] + [copy(a, 4 + j, (*chip, 1 - c), me) for j, chip in enumerate(chips)]
        return bufs, local_sems, own, sent, passed, landed, from_sibling

    def first(ins, outs, scratch):
        bufs, local_sems, own, sent, _, _, _ = parts(outs, scratch)
        loads = [pltpu.make_async_copy(ins[a], bufs[a], local_sems.at[a]) for a in range(n)]
        for ld in loads:
            ld.start()
        for a in range(n):
            loads[a].wait()
            own(a).start()
            for cp in sent(a):
                cp.start()

    def middle(ins, outs, scratch):
        _, _, _, _, passed, landed, _ = parts(outs, scratch)
        for a in range(n):
            for got, on in zip(landed(a), passed(a)):
                got.wait_recv()
                on.start()

    def last(ins, outs, scratch):
        _, _, own, sent, passed, _, from_sibling = parts(outs, scratch)
        for a in range(n):
            for cp in from_sibling(a):
                cp.wait_recv()
        for a in range(n):
            for cp in sent(a) + passed(a):
                cp.wait_send()
            own(a).wait()

    return _Rider(shards, [jax.ShapeDtypeStruct((N_DEV,) + s.shape, s.dtype) for s in shards],
                  [pltpu.VMEM(s.shape, s.dtype) for s in shards]
                  + [pltpu.SemaphoreType.DMA((n, 7)), pltpu.SemaphoreType.DMA((n, 7)), pltpu.SemaphoreType.DMA((n,))],
                  first, last, middle)


def _sibling_rider(grads):
    n = len(grads)

    def copies(g, got, scratch):
        send_sems, recv_sems = scratch
        x, y, c = _pos()
        return [pltpu.make_async_remote_copy(src_ref=g[a].at[2 * q + (1 - c)], dst_ref=got[a].at[q], send_sem=send_sems.at[a, q],
                                             recv_sem=recv_sems.at[a, q], device_id=(x, y, 1 - c), device_id_type=MESH)
                for a in range(n) for q in range(4)]

    def first(g, got, scratch):
        for cp in copies(g, got, scratch):
            cp.start()

    def last(g, got, scratch):
        for cp in copies(g, got, scratch):
            cp.wait()

    return _Rider(grads, [jax.ShapeDtypeStruct((4,) + g.shape[1:], g.dtype) for g in grads],
                  [pltpu.SemaphoreType.DMA((n, 4))] * 2, first, last)


def _chips_rider(sums):
    n = len(sums)

    def copies(s, out, scratch):
        send_sems, recv_sems = scratch
        x, y, c = _pos()
        cps = []
        for a in range(n):
            for f in (1, 2, 3):
                peer = (_flip(x, f >> 1), _flip(y, f & 1), c)
                cps.append(pltpu.make_async_remote_copy(
                    src_ref=s[a].at[2 * peer[0] + peer[1]], dst_ref=out[a].at[f - 1], send_sem=send_sems.at[a, f - 1],
                    recv_sem=recv_sems.at[a, f - 1], device_id=peer, device_id_type=MESH))
        return cps

    def first(s, out, scratch):
        for cp in copies(s, out, scratch):
            cp.start()

    def last(s, out, scratch):
        for cp in copies(s, out, scratch):
            cp.wait()

    return _Rider(sums, [jax.ShapeDtypeStruct((3,) + s.shape[1:], s.dtype) for s in sums],
                  [pltpu.SemaphoreType.DMA((n, 3))] * 2, first, last)


def _both(a, b):
    na = (len(a.ins), len(a.out_shapes), len(a.scratch))

    def split(fa, fb):
        def f(ins, outs, scratch):
            fa(ins[:na[0]], outs[:na[1]], scratch[:na[2]])
            fb(ins[na[0]:], outs[na[1]:], scratch[na[2]:])
        return f

    return _Rider(a.ins + b.ins, a.out_shapes + b.out_shapes, a.scratch + b.scratch, split(a.first, b.first), split(a.last, b.last))


def _alone(rider, name):
    ri, ro = len(rider.ins), len(rider.out_shapes)

    def body(*refs):
        theirs = (refs[:ri], refs[ri:ri + ro], refs[ri + ro:])
        rider.first(*theirs)
        if rider.middle is not None:
            rider.middle(*theirs)
        rider.last(*theirs)

    return pl.pallas_call(body, name=name, in_specs=[ANY] * ri, out_specs=[ANY] * ro, out_shape=rider.out_shapes,
                          scratch_shapes=rider.scratch)(*rider.ins)


def _gather_small(g_w1, g_w2, g_w3, g_lb, g_wn, loss):
    def body(w1_ref, w2_ref, w3_ref, lb_ref, wn_ref, loss_ref, out_ref, pk, send_sems, recv_sems):
        x, y, c = _pos()
        me = 4 * x + 2 * y + c
        pk[...] = jnp.zeros_like(pk)
        pk[0:1, :] = w1_ref[...]
        pk[1:2, :] = w2_ref[...]
        pk[2:3, :] = w3_ref[...]
        pk[3:4, 0:HGRN_W] = lb_ref[...]
        pk[3:4, HGRN_W:2 * HGRN_W] = wn_ref[...]
        pk[4:5, 0:128] = loss_ref[...]
        out_ref[me] = pk[...]
        sends, recvs = [], []
        for k in range(1, N_DEV):
            peer = (_flip(x, k >> 2), _flip(y, (k >> 1) & 1), _flip(c, k & 1))
            cp = pltpu.make_async_remote_copy(src_ref=pk, dst_ref=out_ref.at[me], send_sem=send_sems.at[k - 1],
                                              recv_sem=recv_sems.at[k - 1], device_id=peer, device_id_type=MESH)
            cp.start()
            sends.append(cp)
            recvs.append(pltpu.make_async_remote_copy(src_ref=pk, dst_ref=out_ref.at[4 * peer[0] + 2 * peer[1] + peer[2]],
                                                      send_sem=send_sems.at[k - 1], recv_sem=recv_sems.at[k - 1], device_id=peer,
                                                      device_id_type=MESH))
        for cp in recvs:
            cp.wait_recv()
        for cp in sends:
            cp.wait_send()

    return pl.pallas_call(
        body, name="gather_small", in_specs=[VMEM_SPEC] * 6, out_specs=VMEM_SPEC,
        out_shape=jax.ShapeDtypeStruct((N_DEV, 8, D_MODEL), f32),
        scratch_shapes=[pltpu.VMEM((8, D_MODEL), f32), pltpu.SemaphoreType.DMA((N_DEV - 1,)), pltpu.SemaphoreType.DMA((N_DEV - 1,))],
    )(g_w1, g_w2, g_w3, g_lb, g_wn, loss)


def _row_tile(r):
    return max(t for t in range(8, 257, 8) if r % t == 0)


def _add_sibling(core, g, got, name):
    _, r, c = got.shape
    tr = _row_tile(r)

    def body(core_ref, a_ref, b_ref, o_ref):
        o_ref[...] = (a_ref[...] + b_ref[...]).astype(bf16)

    blk = pl.BlockSpec((1, tr, c), lambda q, i, core_ref: (q, i, 0))
    return pl.pallas_call(
        body, name=name, out_shape=jax.ShapeDtypeStruct(got.shape, bf16),
        grid_spec=pltpu.PrefetchScalarGridSpec(
            num_scalar_prefetch=1, grid=(4, r // tr),
            in_specs=[pl.BlockSpec((1, tr, c), lambda q, i, core_ref: (2 * q + core_ref[0], i, 0)), blk], out_specs=blk),
        compiler_params=_cp("arbitrary", "arbitrary"))(core, g, got)


def _adamw(w, g, m, v):
    m = ADAM_B1 * m + (1.0 - ADAM_B1) * g
    v = ADAM_B2 * v + (1.0 - ADAM_B2) * (g * g)
    m_hat = m / (1.0 - ADAM_B1 ** ADAM_STEP)
    v_hat = v / (1.0 - ADAM_B2 ** ADAM_STEP)
    return -ADAM_LR * (m_hat / (jnp.sqrt(v_hat) + ADAM_EPS) + ADAM_WD * w), m, v


def _adam_shard(where, g, got, pieces, w, m, v, name):
    r, c = w.shape
    tr = _row_tile(r)

    def body(where_ref, g_ref, got_ref, p_ref, w_ref, m_ref, v_ref, g_out, d_out, m_out, v_out):
        gsum = g_ref[0] + got_ref[0]
        for f in range(3):
            gsum = gsum + p_ref[f].astype(f32)
        g_out[...] = gsum
        d_out[...], m_out[...], v_out[...] = _adamw(w_ref[...], gsum, m_ref[...], v_ref[...])

    blk = pl.BlockSpec((tr, c), lambda i, where_ref: (i, 0))
    return pl.pallas_call(
        body, name=name, out_shape=[jax.ShapeDtypeStruct((r, c), f32)] * 4,
        grid_spec=pltpu.PrefetchScalarGridSpec(
            num_scalar_prefetch=1, grid=(r // tr,),
            in_specs=[pl.BlockSpec((1, tr, c), lambda i, where_ref: (where_ref[0], i, 0)),
                      pl.BlockSpec((1, tr, c), lambda i, where_ref: (where_ref[1], i, 0)),
                      pl.BlockSpec((3, tr, c), lambda i, where_ref: (0, i, 0)), blk, blk, blk],
            out_specs=[blk] * 4),
        compiler_params=_cp("arbitrary"),
    )(where, g, got, pieces, w, m, v)


def _small_update(gath, params):
    def body(gath_ref, *refs):
        ins, outs = refs[:15], refs[15:]
        gs = gath_ref[0]
        for k in range(1, N_DEV):
            gs = gs + gath_ref[k]
        outs[0][...] = gs[4:5, 0:128]
        l0, l1 = ins[9][0:1, :], ins[9][1:2, :]
        lb = _sigmoid(l0 - l1)
        d0 = gs[3:4, 0:HGRN_W] * lb * (1.0 - lb)
        first_row = lax.broadcasted_iota(jnp.int32, (2, HGRN_W), 0) == 0
        grads = [gs[0:1, :], gs[1:2, :], gs[2:3, :], jnp.where(first_row, d0, -d0), gs[3:4, HGRN_W:2 * HGRN_W]]
        for i, g in enumerate(grads):
            w_ref, m_ref, v_ref = ins[3 * i:3 * i + 3]
            o = outs[1 + 4 * i:5 + 4 * i]
            o[0][...] = g
            o[1][...], o[2][...], o[3][...] = _adamw(w_ref[...], g, m_ref[...], v_ref[...])

    flat = [a for p in params for a in p]
    out_shape = [jax.ShapeDtypeStruct((1, 128), f32)] + [jax.ShapeDtypeStruct(p[0].shape, f32) for p in params for _ in range(4)]
    outs = pl.pallas_call(body, name="small_update", in_specs=[VMEM_SPEC] * 16, out_specs=[VMEM_SPEC] * 21, out_shape=out_shape)(gath, *flat)
    return outs[0], [outs[1 + 4 * i:5 + 4 * i] for i in range(5)]


def kernel(x, norm1_w, w_in, lb_logits, hgrn_norm_w, w_out, norm2_w, w_gate_up, w_down, final_norm_w, loss_target, m_norm1_w, m_w_in, m_lb_logits, m_hgrn_norm_w, m_w_out, m_norm2_w, m_w_gate_up, m_w_down, m_final_norm_w, v_norm1_w, v_w_in, v_lb_logits, v_hgrn_norm_w, v_w_out, v_norm2_w, v_w_gate_up, v_w_down, v_final_norm_w):
    row = lambda a: a.reshape(1, D_MODEL)
    ix, iy, ic = lax.axis_index("x"), lax.axis_index("y"), lax.axis_index("c")
    core = jnp.stack([ic]).astype(jnp.int32)
    where = jnp.stack([4 * ix + 2 * iy + ic, 2 * ix + iy]).astype(jnp.int32)
    xs, tgt, w3 = x[0], loss_target[0], row(final_norm_w)
    S = xs.shape[0]

    u, cos_t, sg_t, win_g = _prologue(xs, norm1_w, _gather_rider([w_in[0].astype(bf16)]))
    qkv, hp, win = _in_proj(u, win_g, cos_t, sg_t)
    ya, lse, wout_g, wgu_g, wdown_g = _attn_fwd(qkv, _gather_rider([w_out[0].astype(bf16), w_gate_up[0].astype(bf16),
                                                                     w_down[0].astype(bf16)]))
    wout = wout_g.reshape(D_MODEL, D_MODEL)
    wdown = wdown_g.reshape(FFN, D_MODEL)
    yb, o_sav, states = _hgrn_fwd(hp, lb_logits, hgrn_norm_w)
    h1, u2, mixed = _out_proj(xs, ya, yb, wout, norm2_w)
    gate, up, act, wgate, wup = _gate_up(u2, wgu_g)
    dh2, loss_p, g_w3 = _down_loss(act, wdown, h1, tgt, w3)

    (g_wdown,) = _gw(act, [dh2], 512, "gw_down")
    dgu2 = _dact(dh2, wdown, gate, up)
    early = [_gw_by_owner(u2, dgu2, 2 * FFN // N_DEV, "gw_gate_up", 2048), g_wdown.reshape(N_DEV, FFN // N_DEV, D_MODEL)]
    dh1, g_w2, dmix, *got_early = _dgu(dgu2, wgate, wup, h1, norm2_w, dh2, wout, _sibling_rider(early))
    sums_early = [_add_sibling(core, g, o, f"add_sibling_{i}") for i, (g, o) in enumerate(zip(early, got_early))]
    (g_wout,) = _gw(mixed, [dh1], 1024, "gw_out")
    mid = [g_wout.reshape(N_DEV, D_MODEL // N_DEV, D_MODEL)]
    dhq, dhf, dhi, dhg, g_wn, g_lb, *rode = _hgrn_bwd(hp, lb_logits, hgrn_norm_w, o_sav, states, dmix,
                                                      _both(_chips_rider(sums_early), _sibling_rider(mid)))
    pieces_early, got_mid = rode[:2], rode[2:]
    sums_mid = [_add_sibling(core, mid[0], got_mid[0], "add_sibling_2")]
    dq, dk, dv, *pieces_mid = _attn_bwd(qkv, ya, lse, dmix, _chips_rider(sums_mid))
    dproj, gx, g_w1 = _din(dq, dk, dv, dhq, dhf, dhi, dhg, cos_t, sg_t, win, xs, norm1_w, dh1)
    late = [_gw_by_owner(u, dproj[None], IN_W // N_DEV, "gw_in", 2048)]
    got_late = _alone(_sibling_rider(late), "reduce_sibling")
    sums_late = [_add_sibling(core, late[0], got_late[0], "add_sibling_3")]
    pieces_late = _alone(_chips_rider(sums_late), "reduce_chips")

    grads = [late[0], mid[0], early[0], early[1]]
    got = [got_late[0], got_mid[0], got_early[0], got_early[1]]
    pieces = [pieces_late[0], pieces_mid[0], pieces_early[0], pieces_early[1]]
    shards = [w_in[0], w_out[0], w_gate_up[0], w_down[0]]
    moms = [(m_w_in[0], v_w_in[0]), (m_w_out[0], v_w_out[0]), (m_w_gate_up[0], v_w_gate_up[0]), (m_w_down[0], v_w_down[0])]
    big = [_adam_shard(where, g, o, p, w, m, v, f"adam_{i}")
           for i, (g, o, p, w, (m, v)) in enumerate(zip(grads, got, pieces, shards, moms))]
    big = [[a[None] for a in four] for four in big]

    gath = _gather_small(g_w1, g_w2, g_w3, g_lb, g_wn, loss_p)
    params = [(norm1_w, m_norm1_w, v_norm1_w), (norm2_w, m_norm2_w, v_norm2_w),
              (row(final_norm_w), row(m_final_norm_w), row(v_final_norm_w)),
              (lb_logits, m_lb_logits, v_lb_logits), (hgrn_norm_w, m_hgrn_norm_w, v_hgrn_norm_w)]
    loss, (s_w1, s_w2, s_w3, s_lb, s_wn) = _small_update(gath, params)
    s_w3 = [a.reshape(D_MODEL) for a in s_w3]
    per_w = [s_w1, big[0], s_lb, s_wn, big[1], s_w2, big[2], big[3], s_w3]
    return (loss[0, 0], gx[None], *[p[0] for p in per_w], *[p[1] for p in per_w], *[p[2] for p in per_w], *[p[3] for p in per_w])
```

```python
import jax
import jax.numpy as jnp
from jax import lax
from jax.experimental import pallas as pl
from jax.experimental.pallas import tpu as pltpu

f32, bf16 = jnp.float32, jnp.bfloat16

D_MODEL = 1024
ATTN_W = 512
HEAD_DIM = 64
ATTN_BLK = 128
DILATIONS = (1, 4, 16)
HGRN_W = 512
HGRN_HD = 128
CHUNK = 64
IN_W = 3 * ATTN_W + 4 * HGRN_W
FFN = 2816
EPS = 1e-6
ROPE_THETA = 10000.0
NEG = -1e30
N_DEV = 8
ADAM_LR, ADAM_B1, ADAM_B2, ADAM_EPS, ADAM_WD, ADAM_STEP = 0.001, 0.9, 0.999, 1e-08, 0.01, 10
VMEM_LIMIT = 56 * 1024 * 1024


def _cp(*sem):
    return pltpu.CompilerParams(dimension_semantics=sem, vmem_limit_bytes=VMEM_LIMIT)


def _dot(a, b):
    return jnp.dot(a, b, preferred_element_type=f32)


def _dot_nt(a, b):
    return lax.dot_general(a, b, (((1,), (1,)), ((), ())), preferred_element_type=f32)


def _dot_tn(a, b):
    return lax.dot_general(a, b, (((0,), (0,)), ((), ())), preferred_element_type=f32)


def _sigmoid(x):
    return 0.5 * jnp.tanh(0.5 * x) + 0.5


class _Rider:
    def __init__(self, ins, out_shapes, scratch, first, last, middle=None):
        self.ins, self.out_shapes, self.scratch = list(ins), list(out_shapes), list(scratch)
        self.first, self.middle, self.last = first, middle, last


def _ride(call, rider, body, step, n_steps, n_in, n_out, n_scratch):
    if rider is None:
        return call, body, []
    ri, ro = len(rider.ins), len(rider.out_shapes)
    any_spec = pl.BlockSpec(memory_space=pl.ANY)
    call = dict(call, in_specs=call["in_specs"] + [any_spec] * ri, out_specs=call["out_specs"] + [any_spec] * ro,
                out_shape=call["out_shape"] + rider.out_shapes, scratch_shapes=call["scratch_shapes"] + rider.scratch)

    def riding(*refs):
        a = n_in + ri
        b = a + n_out + ro
        mine = refs[:n_in] + refs[a:a + n_out] + refs[b:b + n_scratch]
        theirs = (refs[n_in:a], refs[a + n_out:b], refs[b + n_scratch:])
        t = step()

        @pl.when(t == 0)
        def _():
            rider.first(*theirs)

        body(*mine)
        if rider.middle is not None:
            @pl.when(t == n_steps // 2)
            def _():
                rider.middle(*theirs)

        @pl.when(t == n_steps - 1)
        def _():
            rider.last(*theirs)

    return call, riding, rider.ins


def _rope_tables(S, rider=None):
    half = HEAD_DIM // 2
    tm = 256
    inv_freq = jnp.tile(ROPE_THETA ** (-jnp.arange(half, dtype=f32) / half), 128 // half).reshape(1, 128)
    sign = jnp.tile(jnp.concatenate([-jnp.ones((half,), f32), jnp.ones((half,), f32)]), 128 // HEAD_DIM).reshape(1, 128)

    def body(inv_ref, sign_ref, cos_ref, sg_ref):
        pos = (lax.broadcasted_iota(jnp.int32, (tm, 128), 0) + pl.program_id(0) * tm).astype(f32)
        ang = pos * inv_ref[...]
        cos_ref[...] = jnp.cos(ang)
        sg_ref[...] = jnp.sin(ang) * sign_ref[...]

    vec = pl.BlockSpec((1, 128), lambda i: (0, 0))
    out = pl.BlockSpec((tm, 128), lambda i: (i, 0))
    call = dict(in_specs=[vec, vec], out_specs=[out, out], out_shape=[jax.ShapeDtypeStruct((S, 128), f32)] * 2, scratch_shapes=[])
    call, body, more = _ride(call, rider, body, lambda: pl.program_id(0), S // tm, 2, 2, 0)
    return pl.pallas_call(body, name="rope_tables", grid=(S // tm,), compiler_params=_cp("arbitrary"), **call)(inv_freq, sign, *more)


def _swap_halves(v):
    n = v.shape[1]
    lane = lax.broadcasted_iota(jnp.int32, v.shape, 1)
    return jnp.where((lane % HEAD_DIM) < HEAD_DIM // 2, pltpu.roll(v, n - HEAD_DIM // 2, 1), pltpu.roll(v, HEAD_DIM // 2, 1))


def _in_proj(x, w1, win_g, cos_t, sg_t):
    S = x.shape[0]
    tm = 256
    w = IN_W // N_DEV

    def body(x_ref, w1_ref, wg_ref, cos_ref, sg_ref, u_ref, qkv_ref, hp_ref, w_ref):
        @pl.when(pl.program_id(0) == 0)
        def _():
            for d in range(N_DEV):
                w_ref[:, w * d:w * (d + 1)] = wg_ref[d]

        xv = x_ref[...]
        r = lax.rsqrt(jnp.mean(xv * xv, axis=-1, keepdims=True) + EPS)
        u = (xv * r * w1_ref[...]).astype(bf16)
        u_ref[...] = u
        cosv, sgv = jnp.tile(cos_ref[...], (1, ATTN_W // 128)), jnp.tile(sg_ref[...], (1, ATTN_W // 128))
        for j in range(3):
            pj = _dot(u, w_ref[:, j * ATTN_W:(j + 1) * ATTN_W])
            if j < 2:
                pj = pj * cosv + _swap_halves(pj) * sgv
            if j == 0:
                pj = pj * (HEAD_DIM ** -0.5)
            qkv_ref[:, j * ATTN_W:(j + 1) * ATTN_W] = pj.astype(bf16)
        for j in range(4):
            lo = 3 * ATTN_W + j * HGRN_W
            hp_ref[:, j * HGRN_W:(j + 1) * HGRN_W] = _dot(u, w_ref[:, lo:lo + HGRN_W])

    return pl.pallas_call(
        body, name="in_proj", grid=(S // tm,),
        in_specs=[pl.BlockSpec((tm, D_MODEL), lambda i: (i, 0)), pl.BlockSpec((1, D_MODEL), lambda i: (0, 0)),
                  pl.BlockSpec((N_DEV, D_MODEL, w), lambda i: (0, 0, 0)),
                  pl.BlockSpec((tm, 128), lambda i: (i, 0)), pl.BlockSpec((tm, 128), lambda i: (i, 0))],
        out_specs=[pl.BlockSpec((tm, D_MODEL), lambda i: (i, 0)), pl.BlockSpec((tm, 3 * ATTN_W), lambda i: (i, 0)),
                   pl.BlockSpec((tm, 4 * HGRN_W), lambda i: (i, 0)), pl.BlockSpec((D_MODEL, IN_W), lambda i: (0, 0))],
        out_shape=[jax.ShapeDtypeStruct((S, D_MODEL), bf16), jax.ShapeDtypeStruct((S, 3 * ATTN_W), bf16),
                   jax.ShapeDtypeStruct((S, 4 * HGRN_W), f32), jax.ShapeDtypeStruct((D_MODEL, IN_W), bf16)],
        compiler_params=_cp("arbitrary"),
    )(x, w1, win_g, cos_t, sg_t)


def _head_masks():
    lane = lax.broadcasted_iota(jnp.int32, (ATTN_BLK, 128), 1)
    even = lane < HEAD_DIM
    return even, (even, jnp.logical_not(even))


def _pair_fwd(q2, k2, v2, bias):
    even, masks = _head_masks()
    outs, lses = [], []
    for e in range(2):
        qm = jnp.where(masks[e], q2, 0.0).astype(bf16)
        s = _dot_nt(qm, k2) + bias
        m = jnp.max(s, axis=-1, keepdims=True)
        pe = jnp.exp(s - m)
        lsum = jnp.sum(pe, axis=-1, keepdims=True)
        outs.append(_dot(pe.astype(bf16), v2) / lsum)
        lses.append(jnp.broadcast_to(m + jnp.log(lsum), (ATTN_BLK, 128)))
    return jnp.where(even, outs[0], outs[1]), jnp.where(even, lses[0], lses[1])


def _merge(y0, l0, y1, l1):
    mx = jnp.maximum(l0, l1)
    a, b = jnp.exp(l0 - mx), jnp.exp(l1 - mx)
    tot = a + b
    return (a * y0 + b * y1) / tot, mx + jnp.log(tot)


def _pair_bwd(q2, k2f, v2, dy2, lse2, delta2, bias):
    _, masks = _head_masks()
    k2 = k2f.astype(bf16)
    klane = lax.broadcasted_iota(jnp.int32, (2 * ATTN_BLK, 128), 1) < HEAD_DIM
    kmasks = (klane, jnp.logical_not(klane))
    dq2 = jnp.zeros((ATTN_BLK, 128), f32)
    pes, dss, qms, dyms = [], [], [], []
    for e in range(2):
        c0 = e * HEAD_DIM
        qm = jnp.where(masks[e], q2, 0.0).astype(bf16)
        km = jnp.where(kmasks[e], k2f, 0.0).astype(bf16)
        dym = jnp.where(masks[e], dy2, 0.0).astype(bf16)
        pe = jnp.exp(_dot_nt(qm, k2) + bias - lse2[:, c0:c0 + 1])
        ds = (pe * (_dot_nt(dym, v2) - delta2[:, c0:c0 + 1])).astype(bf16)
        dq2 = dq2 + _dot(ds, km)
        pes.append(pe.astype(bf16))
        dss.append(ds)
        qms.append(qm)
        dyms.append(dym)
    dv2 = _dot_tn(jnp.concatenate(pes, axis=0), jnp.concatenate(dyms, axis=0))
    dk2 = _dot_tn(jnp.concatenate(dss, axis=0), jnp.concatenate(qms, axis=0))
    return dq2, dk2, dv2


TOK = 2048


def _key_bias():
    qi = lax.broadcasted_iota(jnp.int32, (ATTN_BLK, 2 * ATTN_BLK), 0)
    kj = lax.broadcasted_iota(jnp.int32, (ATTN_BLK, 2 * ATTN_BLK), 1)
    delta = ATTN_BLK + qi - kj
    seen = (delta >= 0) & (delta <= ATTN_BLK)
    return jnp.where(seen, 0.0, NEG), jnp.where(seen & (kj >= ATTN_BLK), 0.0, NEG)


def _attn_fwd(qkv, rider=None):
    S = qkv.shape[0]
    nS = S // TOK

    def body(q_ref, kp_ref, kc_ref, vp_ref, vc_ref, y_ref, l_ref, qs, k2, v2, ay, al):
        n = pl.program_id(1)
        qs[...] = q_ref[...].astype(f32)
        k2[0:TOK] = kp_ref[...].astype(f32)
        k2[TOK:2 * TOK] = kc_ref[...].astype(f32)
        v2[0:TOK] = vp_ref[...].astype(f32)
        v2[TOK:2 * TOK] = vc_ref[...].astype(f32)
        bias_any, bias_first = _key_bias()

        def block(dil, r, b, step, last):
            start = r + pl.multiple_of(step * b, step)
            rows = pl.ds(start, ATTN_BLK, stride=dil) if dil > 1 else pl.ds(start, ATTN_BLK)
            keys = (pl.ds(TOK + start - step, 2 * ATTN_BLK, stride=dil) if dil > 1
                    else pl.ds(TOK + start - step, 2 * ATTN_BLK))
            bias = jnp.where((n == 0) & (b == 0), bias_first, bias_any)
            out, lse = _pair_fwd(qs[rows, :], k2[keys, :].astype(bf16), v2[keys, :].astype(bf16), bias)
            if dil < DILATIONS[-1]:
                out, lse = _merge(ay[rows, :], al[rows, :], out, lse)
            if last:
                y_ref[rows, :] = out
                l_ref[rows, :] = lse
            else:
                ay[rows, :] = out
                al[rows, :] = lse

        for dil in reversed(DILATIONS):
            def loop(i, carry, dil=dil):
                block(dil, i % dil, i // dil, ATTN_BLK * dil, dil == 1)
                return carry
            lax.fori_loop(0, TOK // ATTN_BLK, loop, 0, unroll=8)

    blk = (TOK, 128)
    cur = lambda c: pl.BlockSpec(blk, lambda p, n: (n, 4 * c + p))
    prv = lambda c: pl.BlockSpec(blk, lambda p, n: (jnp.maximum(n - 1, 0), 4 * c + p))
    out = pl.BlockSpec(blk, lambda p, n: (n, p))
    call = dict(in_specs=[cur(0), prv(1), cur(1), prv(2), cur(2)], out_specs=[out, out],
                out_shape=[jax.ShapeDtypeStruct((S, ATTN_W), f32)] * 2,
                scratch_shapes=[pltpu.VMEM(blk, f32), pltpu.VMEM((2 * TOK, 128), f32), pltpu.VMEM((2 * TOK, 128), f32),
                                pltpu.VMEM(blk, f32), pltpu.VMEM(blk, f32)])
    call, body, more = _ride(call, rider, body, lambda: pl.program_id(0) * nS + pl.program_id(1), (ATTN_W // 128) * nS, 5, 2, 5)
    return pl.pallas_call(body, name="attention_fwd", grid=(ATTN_W // 128, nS), compiler_params=_cp("arbitrary", "arbitrary"),
                          **call)(qkv, qkv, qkv, qkv, qkv, *more)


def _attn_bwd(qkv, ya, lse, dmix, rider=None):
    S = qkv.shape[0]
    nS = S // TOK

    def body(q_ref, kp_ref, kc_ref, vp_ref, vc_ref, y_ref, l_ref, dy_ref, dq_ref, dk_ref, dv_ref, qs, k2, v2, dk2, dv2, dqa, dl):
        n = pl.program_id(1)

        @pl.when(n == 0)
        def _():
            dk2[...] = jnp.zeros_like(dk2)
            dv2[...] = jnp.zeros_like(dv2)

        @pl.when(n < nS)
        def _():
            qs[...] = q_ref[...].astype(f32)
            k2[0:TOK] = kp_ref[...].astype(f32)
            k2[TOK:2 * TOK] = kc_ref[...].astype(f32)
            v2[0:TOK] = vp_ref[...].astype(f32)
            v2[TOK:2 * TOK] = vc_ref[...].astype(f32)
            li = lax.broadcasted_iota(jnp.int32, (128, 128), 0)
            lj = lax.broadcasted_iota(jnp.int32, (128, 128), 1)
            seg = jnp.where((li // HEAD_DIM) == (lj // HEAD_DIM), 1.0, 0.0).astype(bf16)
            bias_any, bias_first = _key_bias()

            def delta_rows(t, carry):
                rows = pl.ds(pl.multiple_of(256 * t, 256), 256)
                dyy = dy_ref[rows, :] * y_ref[rows, :]
                hi = dyy.astype(bf16)
                dl[rows, :] = _dot(hi, seg) + _dot((dyy - hi.astype(f32)).astype(bf16), seg)
                return carry

            lax.fori_loop(0, TOK // 256, delta_rows, 0)

            def block(dil, r, b, step, first_pattern, last):
                start = r + pl.multiple_of(step * b, step)
                rows = pl.ds(start, ATTN_BLK, stride=dil) if dil > 1 else pl.ds(start, ATTN_BLK)
                keys = (pl.ds(TOK + start - step, 2 * ATTN_BLK, stride=dil) if dil > 1
                        else pl.ds(TOK + start - step, 2 * ATTN_BLK))
                bias = jnp.where((n == 0) & (b == 0), bias_first, bias_any)
                dq2, dkk, dvv = _pair_bwd(qs[rows, :], k2[keys, :], v2[keys, :].astype(bf16), dy_ref[rows, :],
                                          l_ref[rows, :], dl[rows, :], bias)
                if last:
                    dq_ref[rows, :] = dqa[rows, :] + dq2
                elif first_pattern:
                    dqa[rows, :] = dq2
                else:
                    dqa[rows, :] += dq2
                dk2[keys, :] += dkk
                dv2[keys, :] += dvv

            for dil in reversed(DILATIONS):
                def loop(i, carry, dil=dil):
                    block(dil, i % dil, i // dil, ATTN_BLK * dil, dil == DILATIONS[-1], dil == 1)
                    return carry
                lax.fori_loop(0, TOK // ATTN_BLK, loop, 0, unroll=8)

        dk_ref[...] = dk2[0:TOK]
        dv_ref[...] = dv2[0:TOK]
        dk2[0:TOK] = dk2[TOK:2 * TOK]
        dv2[0:TOK] = dv2[TOK:2 * TOK]
        dk2[TOK:2 * TOK] = jnp.zeros((TOK, 128), f32)
        dv2[TOK:2 * TOK] = jnp.zeros((TOK, 128), f32)

    blk = (TOK, 128)
    cn = lambda n: jnp.minimum(n, nS - 1)
    pn = lambda n: jnp.clip(n - 1, 0, nS - 1)
    cur = lambda c: pl.BlockSpec(blk, lambda p, n: (cn(n), 4 * c + p))
    prv = lambda c: pl.BlockSpec(blk, lambda p, n: (pn(n), 4 * c + p))
    at_n = pl.BlockSpec(blk, lambda p, n: (cn(n), p))
    at_p = pl.BlockSpec(blk, lambda p, n: (pn(n), p))
    big = lambda: pltpu.VMEM((2 * TOK, 128), f32)
    call = dict(in_specs=[cur(0), prv(1), cur(1), prv(2), cur(2), at_n, at_n, at_n], out_specs=[at_n, at_p, at_p],
                out_shape=[jax.ShapeDtypeStruct((S, ATTN_W), f32)] * 3,
                scratch_shapes=[pltpu.VMEM(blk, f32), big(), big(), big(), big(), pltpu.VMEM(blk, f32), pltpu.VMEM(blk, f32)])
    call, body, more = _ride(call, rider, body, lambda: pl.program_id(0) * (nS + 1) + pl.program_id(1),
                             (ATTN_W // 128) * (nS + 1), 8, 3, 7)
    return pl.pallas_call(body, name="attention_bwd", grid=(ATTN_W // 128, nS + 1), compiler_params=_cp("arbitrary", "arbitrary"),
                          **call)(qkv, qkv, qkv, qkv, qkv, ya, lse, dmix, *more)


HG_T = 256
N_HH = HGRN_W // HGRN_HD
HG_SUB = 128
SAFE_RANGE = 75.0


def _row_in_chunk():
    return lax.broadcasted_iota(jnp.int32, (HG_T, HGRN_HD), 0) % CHUNK


def _chunk_cumsum(v, rc):
    k = 1
    while k < CHUNK:
        v = v + jnp.where(rc >= k, pltpu.roll(v, k, 0), 0.0)
        k *= 2
    return v


def _chunk_rcumsum(v, rc):
    k = 1
    while k < CHUNK:
        v = v + jnp.where(rc < CHUNK - k, pltpu.roll(v, HG_T - k, 0), 0.0)
        k *= 2
    return v


def _hgrn_gates(qb, fb, lb):
    sf = _sigmoid(fb)
    f = lb + (1.0 - lb) * sf
    sq = _sigmoid(qb)
    return sf, f, jnp.log(f), 1.0 - f, sq, qb * sq


def _hgrn_prep(qb, fb, lbl2, rc):
    lb = _sigmoid(lbl2[0:1, :] - lbl2[1:2, :])
    sf, f, lf, key, sq, qf = _hgrn_gates(qb, fb, lb)
    b = _chunk_cumsum(lf, rc)
    rem = _chunk_rcumsum(lf, rc) - lf
    return dict(lb=lb, sf=sf, f=f, key=key, sq=sq, qf=qf, b=b, rem=rem, eb=jnp.exp(b), er=jnp.exp(rem))


def _chunk_mask():
    r = lax.broadcasted_iota(jnp.int32, (HG_SUB, HG_SUB), 0)
    c = lax.broadcasted_iota(jnp.int32, (HG_SUB, HG_SUB), 1)
    return ((r // CHUNK) == (c // CHUNK)) & (c <= r)


def _hgrn_fwd(hp, lbl, wn):
    S = hp.shape[0]
    nT = S // HG_T

    def body(qb_ref, fb_ref, ib_ref, gb_ref, lbl_ref, wn_ref, yb_ref, o_ref, st_ref, ST, qt_s, kh_s, dec_s, oi_s):
        @pl.when(pl.program_id(0) == 0)
        def _():
            ST[...] = jnp.zeros_like(ST)

        rc = _row_in_chunk()
        for h in range(N_HH):
            sl = slice(HGRN_HD * h, HGRN_HD * (h + 1))
            p = _hgrn_prep(qb_ref[:, sl], fb_ref[:, sl], lbl_ref[:, sl], rc)
            qf, key, b = p["qf"], p["key"], p["b"]
            qt = qf * p["eb"]
            qt_s[:, sl] = qt.astype(bf16)
            kh_s[:, sl] = (key * p["er"]).astype(bf16)
            dec_s[:, sl] = jnp.exp(b + p["rem"])
            rng = jnp.max(-(b + p["rem"]))

            @pl.when(rng < SAFE_RANGE)
            def _():
                kp = (key * jnp.exp(-b)).astype(bf16)
                cmask = _chunk_mask()
                for j in range(HG_T // HG_SUB):
                    rs = slice(HG_SUB * j, HG_SUB * (j + 1))
                    sc = jnp.where(cmask, _dot_nt(qt[rs].astype(bf16), kp[rs]), 0.0).astype(bf16)
                    oi_s[rs, sl] = _dot(sc, ib_ref[rs, sl].astype(bf16))

            @pl.when(rng >= SAFE_RANGE)
            def _():
                v = ib_ref[:, sl]
                ones = jnp.ones((HGRN_HD, HGRN_HD), bf16)

                def lag(l, o):
                    e = jnp.exp(jnp.where(rc >= l, b - pltpu.roll(b, l, 0), NEG))
                    pr = qf * pltpu.roll(key, l, 0) * e
                    return o + _dot(pr.astype(bf16), ones) * pltpu.roll(v, l, 0)

                oi_s[:, sl] = lax.fori_loop(1, CHUNK, lag, _dot((qf * key).astype(bf16), ones) * v)

        def step(c, carry):
            rows = pl.ds(pl.multiple_of(c * CHUNK, CHUNK), CHUNK)
            row0 = pl.ds(pl.multiple_of(c * CHUNK, CHUNK), 1)
            for h in range(N_HH):
                sl = slice(HGRN_HD * h, HGRN_HD * (h + 1))
                stv = ST[h]
                st_ref[c, sl, :] = stv
                oi_s[rows, sl] += _dot_nt(qt_s[rows, sl], stv.astype(bf16))
                ST[h] = stv * dec_s[row0, sl] + _dot_tn(ib_ref[rows, sl].astype(bf16), kh_s[rows, sl])
            return carry

        lax.fori_loop(0, HG_T // CHUNK, step, 0, unroll=True)

        for h in range(N_HH):
            sl = slice(HGRN_HD * h, HGRN_HD * (h + 1))
            o = oi_s[:, sl]
            o_ref[:, sl] = o
            on = o * lax.rsqrt(jnp.mean(o * o, axis=-1, keepdims=True) + EPS)
            g = gb_ref[:, sl]
            yb_ref[:, sl] = on * wn_ref[:, sl] * (g * _sigmoid(g))

    col = lambda c: pl.BlockSpec((HG_T, HGRN_W), lambda i: (i, c))
    tile = pl.BlockSpec((HG_T, HGRN_W), lambda i: (i, 0))
    whole = lambda a: pl.BlockSpec(a.shape, lambda i: (0, 0))
    return pl.pallas_call(
        body, name="hgrn_fwd", grid=(nT,),
        in_specs=[col(0), col(1), col(2), col(3), whole(lbl), whole(wn)],
        out_specs=[tile, tile, pl.BlockSpec((HG_T // CHUNK, HGRN_W, HGRN_HD), lambda i: (i, 0, 0))],
        out_shape=[jax.ShapeDtypeStruct((S, HGRN_W), f32), jax.ShapeDtypeStruct((S, HGRN_W), f32),
                   jax.ShapeDtypeStruct((S // CHUNK, HGRN_W, HGRN_HD), f32)],
        scratch_shapes=[pltpu.VMEM((N_HH, HGRN_HD, HGRN_HD), f32), pltpu.VMEM((HG_T, HGRN_W), bf16),
                        pltpu.VMEM((HG_T, HGRN_W), bf16), pltpu.VMEM((HG_T, HGRN_W), f32), pltpu.VMEM((HG_T, HGRN_W), f32)],
        compiler_params=_cp("arbitrary"),
    )(hp, hp, hp, hp, lbl, wn)


def _hgrn_bwd(hp, lbl, wn, o_sav, states, dmix, rider=None):
    S = hp.shape[0]
    nT = S // HG_T

    def body(qb_ref, fb_ref, ib_ref, gb_ref, lbl_ref, wn_ref, o_ref, st_ref, dy_ref,
             dq_ref, df_ref, di_ref, dg_ref, gwn_ref, glb_ref,
             DST, qt_s, kh_s, dec_s, do_s, dqt_s, dkh_s, dbl_s, dvi_s, dqi_s, dki_s, dbi_s):
        @pl.when(pl.program_id(0) == 0)
        def _():
            DST[...] = jnp.zeros_like(DST)
            gwn_ref[...] = jnp.zeros_like(gwn_ref)
            glb_ref[...] = jnp.zeros_like(glb_ref)

        rc = _row_in_chunk()
        preps = []
        for h in range(N_HH):
            sl = slice(HGRN_HD * h, HGRN_HD * (h + 1))
            p = _hgrn_prep(qb_ref[:, sl], fb_ref[:, sl], lbl_ref[:, sl], rc)
            preps.append(p)
            qf, key, b = p["qf"], p["key"], p["b"]
            v = ib_ref[:, sl]
            o = o_ref[:, sl]
            rinv = lax.rsqrt(jnp.mean(o * o, axis=-1, keepdims=True) + EPS)
            on = o * rinv
            g = gb_ref[:, sl]
            sgm = _sigmoid(g)
            silu_g = g * sgm
            dy = dy_ref[:, sl]
            wn_v = wn_ref[:, sl]
            gwn_ref[:, sl] += jnp.sum(dy * on * silu_g, axis=0, keepdims=True)
            dg_ref[:, sl] = (dy * on * wn_v * (sgm * (1.0 + g * (1.0 - sgm)))).astype(bf16)
            t1 = dy * wn_v * silu_g
            do = rinv * (t1 - on * jnp.mean(t1 * on, axis=-1, keepdims=True))
            do_s[:, sl] = do.astype(bf16)
            qt = qf * p["eb"]
            qt_s[:, sl] = qt.astype(bf16)
            kh_s[:, sl] = (key * p["er"]).astype(bf16)
            dec_s[:, sl] = jnp.exp(b + p["rem"])
            rng = jnp.max(-(b + p["rem"]))

            @pl.when(rng < SAFE_RANGE)
            def _():
                einv = jnp.exp(-b)
                kp = (key * einv).astype(bf16)
                cmask = _chunk_mask()
                for j in range(HG_T // HG_SUB):
                    rs = slice(HG_SUB * j, HG_SUB * (j + 1))
                    qtb, dob, vb = qt[rs].astype(bf16), do[rs].astype(bf16), v[rs].astype(bf16)
                    sc = jnp.where(cmask, _dot_nt(qtb, kp[rs]), 0.0).astype(bf16)
                    dsc = jnp.where(cmask, _dot_nt(dob, vb), 0.0).astype(bf16)
                    dqp = _dot(dsc, kp[rs])
                    dkp = _dot_tn(dsc, qtb)
                    dvi_s[rs, sl] = _dot_tn(sc, dob)
                    dqi_s[rs, sl] = dqp * p["eb"][rs]
                    dki_s[rs, sl] = dkp * einv[rs]
                    dbi_s[rs, sl] = dqp * qtb.astype(f32) - dkp * kp[rs].astype(f32)

            @pl.when(rng >= SAFE_RANGE)
            def _():
                ones = jnp.ones((HGRN_HD, HGRN_HD), bf16)

                def lag(l, carry):
                    dqf, dkey, db, dv = carry
                    e = jnp.exp(jnp.where(rc >= l, b - pltpu.roll(b, l, 0), NEG))
                    ks, vs, qe = pltpu.roll(key, l, 0), pltpu.roll(v, l, 0), qf * e
                    pr = qe * ks
                    rl = _dot(pr.astype(bf16), ones)
                    drl = jnp.where(rc >= l, _dot((do * vs).astype(bf16), ones), 0.0)
                    gl = drl * pr
                    back = HG_T - l
                    return (dqf + drl * ks * e, dkey + pltpu.roll(drl * qe, back, 0), db + gl - pltpu.roll(gl, back, 0),
                            dv + pltpu.roll(rl * do, back, 0))

                rl0 = _dot((qf * key).astype(bf16), ones)
                drl0 = _dot((do * v).astype(bf16), ones)
                dqf, dkey, db, dv = lax.fori_loop(1, CHUNK, lag, (drl0 * key, drl0 * qf, jnp.zeros((HG_T, HGRN_HD), f32), rl0 * do))
                dvi_s[:, sl] = dv
                dqi_s[:, sl] = dqf
                dki_s[:, sl] = dkey
                dbi_s[:, sl] = db

        def step(k, carry):
            c = HG_T // CHUNK - 1 - k
            rows = pl.ds(pl.multiple_of(c * CHUNK, CHUNK), CHUNK)
            row0 = pl.ds(pl.multiple_of(c * CHUNK, CHUNK), 1)
            for h in range(N_HH):
                sl = slice(HGRN_HD * h, HGRN_HD * (h + 1))
                stp = st_ref[c, sl, :]
                dst = DST[h]
                dstb = dst.astype(bf16)
                dob = do_s[rows, sl]
                khb = kh_s[rows, sl]
                dec = dec_s[row0, sl]
                dqt_s[rows, sl] = _dot(dob, stp.astype(bf16))
                dkh = _dot(ib_ref[rows, sl].astype(bf16), dstb)
                dkh_s[rows, sl] = dkh
                dvi_s[rows, sl] += _dot_nt(khb, dstb)
                dbl = jnp.sum(dst * stp, axis=0, keepdims=True) * dec + jnp.sum(dkh * khb.astype(f32), axis=0, keepdims=True)
                dbl_s[rows, sl] = jnp.broadcast_to(dbl, (CHUNK, HGRN_HD))
                DST[h] = dst * dec + _dot_tn(dob, qt_s[rows, sl])
            return carry

        lax.fori_loop(0, HG_T // CHUNK, step, 0, unroll=True)

        for h in range(N_HH):
            sl = slice(HGRN_HD * h, HGRN_HD * (h + 1))
            qb = qb_ref[:, sl]
            p = preps[h]
            sf, sq, lb = p["sf"], p["sq"], p["lb"]
            dqt, dkh = dqt_s[:, sl], dkh_s[:, sl]
            dqf = dqt * p["eb"] + dqi_s[:, sl]
            dkey = dkh * p["er"] + dki_s[:, sl]
            db = dqt * (p["qf"] * p["eb"]) - dkh * (p["key"] * p["er"]) + jnp.where(rc == CHUNK - 1, dbl_s[:, sl], 0.0) + dbi_s[:, sl]
            df = _chunk_rcumsum(db, rc) / p["f"] - dkey
            df_ref[:, sl] = (df * (1.0 - lb) * sf * (1.0 - sf)).astype(bf16)
            glb_ref[:, sl] += jnp.sum(df * (1.0 - sf), axis=0, keepdims=True)
            dq_ref[:, sl] = (dqf * (sq * (1.0 + qb * (1.0 - sq)))).astype(bf16)
            di_ref[:, sl] = dvi_s[:, sl].astype(bf16)

    rev = lambda i: nT - 1 - i
    col = lambda c: pl.BlockSpec((HG_T, HGRN_W), lambda i: (rev(i), c))
    tile = pl.BlockSpec((HG_T, HGRN_W), lambda i: (rev(i), 0))
    whole = lambda a: pl.BlockSpec(a.shape, lambda i: (0, 0))
    vec = pl.BlockSpec((1, HGRN_W), lambda i: (0, 0))
    tb = lambda: pltpu.VMEM((HG_T, HGRN_W), bf16)
    tf = lambda: pltpu.VMEM((HG_T, HGRN_W), f32)
    call = dict(in_specs=[col(0), col(1), col(2), col(3), whole(lbl), whole(wn), tile,
                          pl.BlockSpec((HG_T // CHUNK, HGRN_W, HGRN_HD), lambda i: (rev(i), 0, 0)),
                          pl.BlockSpec((HG_T, HGRN_W), lambda i: (rev(i), 1))],
                out_specs=[tile, tile, tile, tile, vec, vec],
                out_shape=[jax.ShapeDtypeStruct((S, HGRN_W), bf16)] * 4 + [jax.ShapeDtypeStruct((1, HGRN_W), f32)] * 2,
                scratch_shapes=[pltpu.VMEM((N_HH, HGRN_HD, HGRN_HD), f32), tb(), tb(), tf(), tb(), tf(), tf(), tf(), tf(), tf(),
                                tf(), tf()])
    call, body, more = _ride(call, rider, body, lambda: pl.program_id(0), nT, 9, 6, 12)
    return pl.pallas_call(body, name="hgrn_bwd", grid=(nT,), compiler_params=_cp("arbitrary"), **call)(
        hp, hp, hp, hp, lbl, wn, o_sav, states, dmix, *more)


def _out_proj(x, ya, yb, wout, w2):
    S = x.shape[0]
    tm = 512

    def body(x_ref, ya_ref, yb_ref, w_ref, w2_ref, h1_ref, u2_ref, mix_ref):
        mixed = jnp.concatenate([ya_ref[...], yb_ref[...]], axis=1).astype(bf16)
        mix_ref[...] = mixed
        h1 = x_ref[...] + _dot(mixed, w_ref[...])
        h1_ref[...] = h1
        r = lax.rsqrt(jnp.mean(h1 * h1, axis=-1, keepdims=True) + EPS)
        u2_ref[...] = (h1 * r * w2_ref[...]).astype(bf16)

    row = lambda w: pl.BlockSpec((tm, w), lambda i: (i, 0))
    return pl.pallas_call(
        body, name="out_proj", grid=(S // tm,),
        in_specs=[row(D_MODEL), row(ATTN_W), row(HGRN_W), pl.BlockSpec((D_MODEL, D_MODEL), lambda i: (0, 0)),
                  pl.BlockSpec((1, D_MODEL), lambda i: (0, 0))],
        out_specs=[row(D_MODEL), row(D_MODEL), row(D_MODEL)],
        out_shape=[jax.ShapeDtypeStruct((S, D_MODEL), f32), jax.ShapeDtypeStruct((S, D_MODEL), bf16),
                   jax.ShapeDtypeStruct((S, D_MODEL), bf16)],
        compiler_params=_cp("arbitrary"),
    )(x, ya, yb, wout, w2)


def _gate_up(u2, wgu_g):
    S = u2.shape[0]
    w = 2 * FFN // N_DEV
    tm, tn = 512, 2 * w
    nj = FFN // tn

    def body(u_ref, wgg_ref, wug_ref, g_ref, up_ref, a_ref, wg_ref, wu_ref):
        @pl.when(pl.program_id(1) == 0)
        def _():
            for k in range(2):
                wg_ref[:, w * k:w * (k + 1)] = wgg_ref[k]
                wu_ref[:, w * k:w * (k + 1)] = wug_ref[k]

        u = u_ref[...]
        g = _dot(u, wg_ref[...])
        up = _dot(u, wu_ref[...])
        sg = _sigmoid(g)
        silu = g * sg
        g_ref[...] = silu.astype(bf16)
        up_ref[...] = (up * (sg + silu * (1.0 - sg))).astype(bf16)
        a_ref[...] = (silu * up).astype(bf16)

    out = pl.BlockSpec((tm, tn), lambda j, i: (i, j))
    wout = pl.BlockSpec((D_MODEL, tn), lambda j, i: (0, j))
    return pl.pallas_call(
        body, name="gate_up", grid=(nj, S // tm),
        in_specs=[pl.BlockSpec((tm, D_MODEL), lambda j, i: (i, 0)), pl.BlockSpec((2, D_MODEL, w), lambda j, i: (j, 0, 0)),
                  pl.BlockSpec((2, D_MODEL, w), lambda j, i: (j + nj, 0, 0))],
        out_specs=[out, out, out, wout, wout],
        out_shape=[jax.ShapeDtypeStruct((S, FFN), bf16)] * 3 + [jax.ShapeDtypeStruct((D_MODEL, FFN), bf16)] * 2,
        compiler_params=_cp("arbitrary", "arbitrary"),
    )(u2, wgu_g, wgu_g)


def _rms_bwd(dyw, hn, r):
    return r * (dyw - hn * jnp.mean(dyw * hn, axis=-1, keepdims=True))


def _down_loss(act, wdown, h1, tgt, w3):
    S = act.shape[0]
    tm = 256

    def body(a_ref, w_ref, h1_ref, t_ref, w3_ref, dh2_ref, loss_ref, gw3_ref):
        @pl.when(pl.program_id(0) == 0)
        def _():
            loss_ref[...] = jnp.zeros_like(loss_ref)
            gw3_ref[...] = jnp.zeros_like(gw3_ref)

        h2 = h1_ref[...] + _dot(a_ref[...], w_ref[...])
        r = lax.rsqrt(jnp.mean(h2 * h2, axis=-1, keepdims=True) + EPS)
        hn = h2 * r
        w3 = w3_ref[...]
        err = hn * w3 - t_ref[...]
        loss_ref[...] += (0.5 / D_MODEL) * jnp.sum(err * err)
        dy = err * (1.0 / D_MODEL)
        gw3_ref[...] += jnp.sum(dy * hn, axis=0, keepdims=True)
        dh2_ref[...] = _rms_bwd(dy * w3, hn, r)

    row = lambda w: pl.BlockSpec((tm, w), lambda i: (i, 0))
    return pl.pallas_call(
        body, name="down_loss", grid=(S // tm,),
        in_specs=[row(FFN), pl.BlockSpec((FFN, D_MODEL), lambda i: (0, 0)), row(D_MODEL), row(D_MODEL),
                  pl.BlockSpec((1, D_MODEL), lambda i: (0, 0))],
        out_specs=[row(D_MODEL), pl.BlockSpec((1, 128), lambda i: (0, 0)), pl.BlockSpec((1, D_MODEL), lambda i: (0, 0))],
        out_shape=[jax.ShapeDtypeStruct((S, D_MODEL), f32), jax.ShapeDtypeStruct((1, 128), f32),
                   jax.ShapeDtypeStruct((1, D_MODEL), f32)],
        compiler_params=_cp("arbitrary"),
    )(act, wdown, h1, tgt, w3)


def _dact(dh2, wdown, silu, up_dsilu):
    S = dh2.shape[0]
    tm = 256

    def body(d_ref, w_ref, s_ref, u_ref, o_ref):
        da = _dot_nt(d_ref[...].astype(bf16), w_ref[...])
        o_ref[1] = (da * s_ref[...].astype(f32)).astype(bf16)
        o_ref[0] = (da * u_ref[...].astype(f32)).astype(bf16)

    row = lambda w: pl.BlockSpec((tm, w), lambda i: (i, 0))
    return pl.pallas_call(
        body, name="dact", grid=(S // tm,),
        in_specs=[row(D_MODEL), pl.BlockSpec((FFN, D_MODEL), lambda i: (0, 0)), row(FFN), row(FFN)],
        out_specs=pl.BlockSpec((2, tm, FFN), lambda i: (0, i, 0)),
        out_shape=jax.ShapeDtypeStruct((2, S, FFN), bf16),
        compiler_params=_cp("arbitrary"),
    )(dh2, wdown, silu, up_dsilu)


def _dgu(dgu2, wgate, wup, h1, w2, dh2, wout, rider=None):
    S = dgu2.shape[1]
    tm = 256

    def body(d_ref, wg_ref, wu_ref, h1_ref, w2_ref, dh2_ref, wo_ref, dh1_ref, gw2_ref, dmix_ref):
        @pl.when(pl.program_id(0) == 0)
        def _():
            gw2_ref[...] = jnp.zeros_like(gw2_ref)

        du2 = _dot_nt(d_ref[0], wg_ref[...]) + _dot_nt(d_ref[1], wu_ref[...])
        h1 = h1_ref[...]
        r = lax.rsqrt(jnp.mean(h1 * h1, axis=-1, keepdims=True) + EPS)
        hn = h1 * r
        gw2_ref[...] += jnp.sum(du2 * hn, axis=0, keepdims=True)
        dh1 = dh2_ref[...] + _rms_bwd(du2 * w2_ref[...], hn, r)
        dh1_ref[...] = dh1
        dmix_ref[...] = _dot_nt(dh1.astype(bf16), wo_ref[...])

    row = lambda w: pl.BlockSpec((tm, w), lambda i: (i, 0))
    call = dict(in_specs=[pl.BlockSpec((2, tm, FFN), lambda i: (0, i, 0)), pl.BlockSpec((D_MODEL, FFN), lambda i: (0, 0)),
                          pl.BlockSpec((D_MODEL, FFN), lambda i: (0, 0)), row(D_MODEL),
                          pl.BlockSpec((1, D_MODEL), lambda i: (0, 0)), row(D_MODEL),
                          pl.BlockSpec((D_MODEL, D_MODEL), lambda i: (0, 0))],
                out_specs=[row(D_MODEL), pl.BlockSpec((1, D_MODEL), lambda i: (0, 0)), row(D_MODEL)],
                out_shape=[jax.ShapeDtypeStruct((S, D_MODEL), f32), jax.ShapeDtypeStruct((1, D_MODEL), f32),
                           jax.ShapeDtypeStruct((S, D_MODEL), f32)], scratch_shapes=[])
    call, body, more = _ride(call, rider, body, lambda: pl.program_id(0), S // tm, 7, 3, 0)
    return pl.pallas_call(body, name="dgu", grid=(S // tm,), compiler_params=_cp("arbitrary"), **call)(
        dgu2, wgate, wup, h1, w2, dh2, wout, *more)


def _din(dq, dk, dv, dhq, dhf, dhi, dhg, cos_t, sg_t, win, x, w1, dh1):
    S = x.shape[0]
    tm = 256

    def body(dq_ref, dk_ref, dv_ref, dhq_ref, dhf_ref, dhi_ref, dhg_ref, cos_ref, sg_ref, w_ref, x_ref, w1_ref, dh1_ref,
             dp_ref, gx_ref, gw1_ref):
        @pl.when(pl.program_id(0) == 0)
        def _():
            gw1_ref[...] = jnp.zeros_like(gw1_ref)

        cosv, sgv = jnp.tile(cos_ref[...], (1, ATTN_W // 128)), jnp.tile(sg_ref[...], (1, ATTN_W // 128))
        unrope = lambda d: d * cosv - sgv * _swap_halves(d)
        parts = [(unrope(dq_ref[...]) * (HEAD_DIM ** -0.5)).astype(bf16), unrope(dk_ref[...]).astype(bf16),
                 dv_ref[...].astype(bf16), dhq_ref[...], dhf_ref[...], dhi_ref[...], dhg_ref[...]]
        du = jnp.zeros((tm, D_MODEL), f32)
        for j, pj in enumerate(parts):
            dp_ref[:, j * 512:(j + 1) * 512] = pj
            du = du + _dot_nt(pj, w_ref[:, j * 512:(j + 1) * 512])
        xv = x_ref[...]
        r = lax.rsqrt(jnp.mean(xv * xv, axis=-1, keepdims=True) + EPS)
        xn = xv * r
        gw1_ref[...] += jnp.sum(du * xn, axis=0, keepdims=True)
        gx_ref[...] = dh1_ref[...] + _rms_bwd(du * w1_ref[...], xn, r)

    row = lambda w: pl.BlockSpec((tm, w), lambda i: (i, 0))
    vec = pl.BlockSpec((1, D_MODEL), lambda i: (0, 0))
    return pl.pallas_call(
        body, name="din", grid=(S // tm,),
        in_specs=[row(512)] * 7 + [row(128), row(128), pl.BlockSpec((D_MODEL, IN_W), lambda i: (0, 0)), row(D_MODEL), vec,
                                   row(D_MODEL)],
        out_specs=[row(IN_W), row(D_MODEL), vec],
        out_shape=[jax.ShapeDtypeStruct((S, IN_W), bf16), jax.ShapeDtypeStruct((S, D_MODEL), f32),
                   jax.ShapeDtypeStruct((1, D_MODEL), f32)],
        compiler_params=_cp("arbitrary"),
    )(dq, dk, dv, dhq, dhf, dhi, dhg, cos_t, sg_t, win, x, w1, dh1)


def _gw(a, bs, tn, name, ts=2048):
    S, M = a.shape
    N = bs[0].shape[1]
    k = len(bs)

    def body(a_ref, *refs):
        @pl.when(pl.program_id(1) == 0)
        def _():
            for o_ref in refs[k:]:
                o_ref[...] = jnp.zeros_like(o_ref)

        at = a_ref[...].astype(bf16)
        for b_ref, o_ref in zip(refs[:k], refs[k:]):
            o_ref[...] += _dot_tn(at, b_ref[...].astype(bf16))

    return pl.pallas_call(
        body, name=name, grid=(N // tn, S // ts),
        in_specs=[pl.BlockSpec((ts, M), lambda j, s: (s, 0))] + [pl.BlockSpec((ts, tn), lambda j, s: (s, j))] * k,
        out_specs=[pl.BlockSpec((M, tn), lambda j, s: (0, j))] * k, out_shape=[jax.ShapeDtypeStruct((M, N), f32)] * k,
        compiler_params=_cp("arbitrary", "arbitrary"),
    )(a, *bs)


def _gw_by_owner(a, b3, w, name, ts):
    S, M = a.shape
    G, _, Ng = b3.shape
    tn = 2 * w
    per_group = Ng // tn
    n_s = S // ts

    def body(a_ref, b_ref, o_ref, acc):
        s = pl.program_id(1)

        @pl.when(s == 0)
        def _():
            acc[...] = jnp.zeros_like(acc)

        acc[...] += _dot_tn(a_ref[...].astype(bf16), b_ref[0].astype(bf16))

        @pl.when(s == n_s - 1)
        def _():
            o_ref[0] = acc[:, 0:w]
            o_ref[1] = acc[:, w:tn]

    return pl.pallas_call(
        body, name=name, grid=(G * per_group, n_s),
        in_specs=[pl.BlockSpec((ts, M), lambda j, s: (s, 0)),
                  pl.BlockSpec((1, ts, tn), lambda j, s: (j // per_group, s, j % per_group))],
        out_specs=pl.BlockSpec((2, M, w), lambda j, s: (j, 0, 0)), out_shape=jax.ShapeDtypeStruct((G * Ng // w, M, w), f32),
        scratch_shapes=[pltpu.VMEM((M, tn), f32)], compiler_params=_cp("arbitrary", "arbitrary"),
    )(a, b3)


MESH = pl.DeviceIdType.MESH
ANY = pl.BlockSpec(memory_space=pl.ANY)
VMEM_SPEC = pl.BlockSpec(memory_space=pltpu.VMEM)


def _pos():
    return lax.axis_index("x"), lax.axis_index("y"), lax.axis_index("c")


def _flip(v, bit):
    return 1 - v if bit else v


def _gather_rider(shards):
    n = len(shards)

    def parts(outs, scratch):
        send_sems, recv_sems, local_sems = scratch[n:]
        x, y, c = _pos()
        chips = [(1 - x, y), (x, 1 - y), (1 - x, 1 - y)]

        def copy(a, k, block, to, src=None):
            dst = outs[a].at[4 * block[0] + 2 * block[1] + block[2]]
            return pltpu.make_async_remote_copy(src_ref=dst if src is None else src, dst_ref=dst, send_sem=send_sems.at[a, k],
                                                recv_sem=recv_sems.at[a, k], device_id=to, device_id_type=MESH)

        bufs = scratch[:n]
        me, sibling = (x, y, c), (x, y, 1 - c)
        own = lambda a: pltpu.make_async_copy(bufs[a], outs[a].at[4 * x + 2 * y + c], local_sems.at[a])
        sent = lambda a: [copy(a, 0, me, sibling, src=bufs[a])] + [copy(a, 1 + j, me, (*chip, c), src=bufs[a])
                                                                   for j, chip in enumerate(chips)]
        passed = lambda a: [copy(a, 4 + j, (*chip, c), sibling) for j, chip in enumerate(chips)]
        landed = lambda a: [copy(a, 1 + j, (*chip, c), me) for j, chip in enumerate(chips)]
        from_sibling = lambda a: [copy(a, 0, sibling, me)] + [copy(a, 4 + j, (*chip, 1 - c), me) for j, chip in enumerate(chips)]
        return bufs, local_sems, own, sent, passed, landed, from_sibling

    def first(ins, outs, scratch):
        bufs, local_sems, own, sent, _, _, _ = parts(outs, scratch)
        loads = [pltpu.make_async_copy(ins[a], bufs[a], local_sems.at[a]) for a in range(n)]
        for ld in loads:
            ld.start()
        for a in range(n):
            loads[a].wait()
            own(a).start()
            for cp in sent(a):
                cp.start()

    def middle(ins, outs, scratch):
        _, _, _, _, passed, landed, _ = parts(outs, scratch)
        for a in range(n):
            for got, on in zip(landed(a), passed(a)):
                got.wait_recv()
                on.start()

    def last(ins, outs, scratch):
        _, _, own, sent, passed, _, from_sibling = parts(outs, scratch)
        for a in range(n):
            for cp in from_sibling(a):
                cp.wait_recv()
        for a in range(n):
            for cp in sent(a) + passed(a):
                cp.wait_send()
            own(a).wait()

    return _Rider(shards, [jax.ShapeDtypeStruct((N_DEV,) + s.shape, s.dtype) for s in shards],
                  [pltpu.VMEM(s.shape, s.dtype) for s in shards]
                  + [pltpu.SemaphoreType.DMA((n, 7)), pltpu.SemaphoreType.DMA((n, 7)), pltpu.SemaphoreType.DMA((n,))],
                  first, last, middle)


def _sibling_rider(grads):
    n = len(grads)

    def copies(g, got, scratch):
        send_sems, recv_sems = scratch
        x, y, c = _pos()
        return [pltpu.make_async_remote_copy(src_ref=g[a].at[2 * q + (1 - c)], dst_ref=got[a].at[q], send_sem=send_sems.at[a, q],
                                             recv_sem=recv_sems.at[a, q], device_id=(x, y, 1 - c), device_id_type=MESH)
                for a in range(n) for q in range(4)]

    def first(g, got, scratch):
        for cp in copies(g, got, scratch):
            cp.start()

    def last(g, got, scratch):
        for cp in copies(g, got, scratch):
            cp.wait()

    return _Rider(grads, [jax.ShapeDtypeStruct((4,) + g.shape[1:], g.dtype) for g in grads],
                  [pltpu.SemaphoreType.DMA((n, 4))] * 2, first, last)


def _chips_rider(sums):
    n = len(sums)

    def copies(s, out, scratch):
        send_sems, recv_sems = scratch
        x, y, c = _pos()
        cps = []
        for a in range(n):
            for f in (1, 2, 3):
                peer = (_flip(x, f >> 1), _flip(y, f & 1), c)
                cps.append(pltpu.make_async_remote_copy(
                    src_ref=s[a].at[2 * peer[0] + peer[1]], dst_ref=out[a].at[f - 1], send_sem=send_sems.at[a, f - 1],
                    recv_sem=recv_sems.at[a, f - 1], device_id=peer, device_id_type=MESH))
        return cps

    def first(s, out, scratch):
        for cp in copies(s, out, scratch):
            cp.start()

    def last(s, out, scratch):
        for cp in copies(s, out, scratch):
            cp.wait()

    return _Rider(sums, [jax.ShapeDtypeStruct((3,) + s.shape[1:], s.dtype) for s in sums],
                  [pltpu.SemaphoreType.DMA((n, 3))] * 2, first, last)


def _both(a, b):
    na = (len(a.ins), len(a.out_shapes), len(a.scratch))

    def split(fa, fb):
        def f(ins, outs, scratch):
            fa(ins[:na[0]], outs[:na[1]], scratch[:na[2]])
            fb(ins[na[0]:], outs[na[1]:], scratch[na[2]:])
        return f

    return _Rider(a.ins + b.ins, a.out_shapes + b.out_shapes, a.scratch + b.scratch, split(a.first, b.first), split(a.last, b.last))


def _alone(rider, name):
    ri, ro = len(rider.ins), len(rider.out_shapes)

    def body(*refs):
        theirs = (refs[:ri], refs[ri:ri + ro], refs[ri + ro:])
        rider.first(*theirs)
        if rider.middle is not None:
            rider.middle(*theirs)
        rider.last(*theirs)

    return pl.pallas_call(body, name=name, in_specs=[ANY] * ri, out_specs=[ANY] * ro, out_shape=rider.out_shapes,
                          scratch_shapes=rider.scratch)(*rider.ins)


def _gather_small(g_w1, g_w2, g_w3, g_lb, g_wn, loss):
    def body(w1_ref, w2_ref, w3_ref, lb_ref, wn_ref, loss_ref, out_ref, pk, send_sems, recv_sems):
        x, y, c = _pos()
        me = 4 * x + 2 * y + c
        pk[...] = jnp.zeros_like(pk)
        pk[0:1, :] = w1_ref[...]
        pk[1:2, :] = w2_ref[...]
        pk[2:3, :] = w3_ref[...]
        pk[3:4, 0:HGRN_W] = lb_ref[...]
        pk[3:4, HGRN_W:2 * HGRN_W] = wn_ref[...]
        pk[4:5, 0:128] = loss_ref[...]
        out_ref[me] = pk[...]
        sends, recvs = [], []
        for k in range(1, N_DEV):
            peer = (_flip(x, k >> 2), _flip(y, (k >> 1) & 1), _flip(c, k & 1))
            cp = pltpu.make_async_remote_copy(src_ref=pk, dst_ref=out_ref.at[me], send_sem=send_sems.at[k - 1],
                                              recv_sem=recv_sems.at[k - 1], device_id=peer, device_id_type=MESH)
            cp.start()
            sends.append(cp)
            recvs.append(pltpu.make_async_remote_copy(src_ref=pk, dst_ref=out_ref.at[4 * peer[0] + 2 * peer[1] + peer[2]],
                                                      send_sem=send_sems.at[k - 1], recv_sem=recv_sems.at[k - 1], device_id=peer,
                                                      device_id_type=MESH))
        for cp in recvs:
            cp.wait_recv()
        for cp in sends:
            cp.wait_send()

    return pl.pallas_call(
        body, name="gather_small", in_specs=[VMEM_SPEC] * 6, out_specs=VMEM_SPEC,
        out_shape=jax.ShapeDtypeStruct((N_DEV, 8, D_MODEL), f32),
        scratch_shapes=[pltpu.VMEM((8, D_MODEL), f32), pltpu.SemaphoreType.DMA((N_DEV - 1,)), pltpu.SemaphoreType.DMA((N_DEV - 1,))],
    )(g_w1, g_w2, g_w3, g_lb, g_wn, loss)


def _row_tile(r):
    return max(t for t in range(8, 257, 8) if r % t == 0)


def _add_sibling(core, g, got, name):
    _, r, c = got.shape
    tr = _row_tile(r)

    def body(core_ref, a_ref, b_ref, o_ref):
        o_ref[...] = (a_ref[...] + b_ref[...]).astype(bf16)

    blk = pl.BlockSpec((1, tr, c), lambda q, i, core_ref: (q, i, 0))
    return pl.pallas_call(
        body, name=name, out_shape=jax.ShapeDtypeStruct(got.shape, bf16),
        grid_spec=pltpu.PrefetchScalarGridSpec(
            num_scalar_prefetch=1, grid=(4, r // tr),
            in_specs=[pl.BlockSpec((1, tr, c), lambda q, i, core_ref: (2 * q + core_ref[0], i, 0)), blk], out_specs=blk),
        compiler_params=_cp("arbitrary", "arbitrary"))(core, g, got)


def _adamw(w, g, m, v):
    m = ADAM_B1 * m + (1.0 - ADAM_B1) * g
    v = ADAM_B2 * v + (1.0 - ADAM_B2) * (g * g)
    m_hat = m / (1.0 - ADAM_B1 ** ADAM_STEP)
    v_hat = v / (1.0 - ADAM_B2 ** ADAM_STEP)
    return -ADAM_LR * (m_hat / (jnp.sqrt(v_hat) + ADAM_EPS) + ADAM_WD * w), m, v


def _adam_shard(where, g, got, pieces, w, m, v, name):
    r, c = w.shape
    tr = _row_tile(r)

    def body(where_ref, g_ref, got_ref, p_ref, w_ref, m_ref, v_ref, g_out, d_out, m_out, v_out):
        gsum = g_ref[0] + got_ref[0]
        for f in range(3):
            gsum = gsum + p_ref[f].astype(f32)
        g_out[...] = gsum
        d_out[...], m_out[...], v_out[...] = _adamw(w_ref[...], gsum, m_ref[...], v_ref[...])

    blk = pl.BlockSpec((tr, c), lambda i, where_ref: (i, 0))
    return pl.pallas_call(
        body, name=name, out_shape=[jax.ShapeDtypeStruct((r, c), f32)] * 4,
        grid_spec=pltpu.PrefetchScalarGridSpec(
            num_scalar_prefetch=1, grid=(r // tr,),
            in_specs=[pl.BlockSpec((1, tr, c), lambda i, where_ref: (where_ref[0], i, 0)),
                      pl.BlockSpec((1, tr, c), lambda i, where_ref: (where_ref[1], i, 0)),
                      pl.BlockSpec((3, tr, c), lambda i, where_ref: (0, i, 0)), blk, blk, blk],
            out_specs=[blk] * 4),
        compiler_params=_cp("arbitrary"),
    )(where, g, got, pieces, w, m, v)


def _small_update(gath, params):
    def body(gath_ref, *refs):
        ins, outs = refs[:15], refs[15:]
        gs = gath_ref[0]
        for k in range(1, N_DEV):
            gs = gs + gath_ref[k]
        outs[0][...] = gs[4:5, 0:128]
        l0, l1 = ins[9][0:1, :], ins[9][1:2, :]
        lb = _sigmoid(l0 - l1)
        d0 = gs[3:4, 0:HGRN_W] * lb * (1.0 - lb)
        first_row = lax.broadcasted_iota(jnp.int32, (2, HGRN_W), 0) == 0
        grads = [gs[0:1, :], gs[1:2, :], gs[2:3, :], jnp.where(first_row, d0, -d0), gs[3:4, HGRN_W:2 * HGRN_W]]
        for i, g in enumerate(grads):
            w_ref, m_ref, v_ref = ins[3 * i:3 * i + 3]
            o = outs[1 + 4 * i:5 + 4 * i]
            o[0][...] = g
            o[1][...], o[2][...], o[3][...] = _adamw(w_ref[...], g, m_ref[...], v_ref[...])

    flat = [a for p in params for a in p]
    out_shape = [jax.ShapeDtypeStruct((1, 128), f32)] + [jax.ShapeDtypeStruct(p[0].shape, f32) for p in params for _ in range(4)]
    outs = pl.pallas_call(body, name="small_update", in_specs=[VMEM_SPEC] * 16, out_specs=[VMEM_SPEC] * 21, out_shape=out_shape)(gath, *flat)
    return outs[0], [outs[1 + 4 * i:5 + 4 * i] for i in range(5)]


def kernel(x, norm1_w, w_in, lb_logits, hgrn_norm_w, w_out, norm2_w, w_gate_up, w_down, final_norm_w, loss_target, m_norm1_w, m_w_in, m_lb_logits, m_hgrn_norm_w, m_w_out, m_norm2_w, m_w_gate_up, m_w_down, m_final_norm_w, v_norm1_w, v_w_in, v_lb_logits, v_hgrn_norm_w, v_w_out, v_norm2_w, v_w_gate_up, v_w_down, v_final_norm_w):
    row = lambda a: a.reshape(1, D_MODEL)
    ix, iy, ic = lax.axis_index("x"), lax.axis_index("y"), lax.axis_index("c")
    core = jnp.stack([ic]).astype(jnp.int32)
    where = jnp.stack([4 * ix + 2 * iy + ic, 2 * ix + iy]).astype(jnp.int32)
    xs, tgt, w3 = x[0], loss_target[0], row(final_norm_w)
    S = xs.shape[0]

    cos_t, sg_t, win_g = _rope_tables(S, _gather_rider([w_in[0].astype(bf16)]))
    u, qkv, hp, win = _in_proj(xs, norm1_w, win_g, cos_t, sg_t)
    ya, lse, wout_g, wgu_g, wdown_g = _attn_fwd(qkv, _gather_rider([w_out[0].astype(bf16), w_gate_up[0].astype(bf16),
                                                                     w_down[0].astype(bf16)]))
    wout = wout_g.reshape(D_MODEL, D_MODEL)
    wdown = wdown_g.reshape(FFN, D_MODEL)
    yb, o_sav, states = _hgrn_fwd(hp, lb_logits, hgrn_norm_w)
    h1, u2, mixed = _out_proj(xs, ya, yb, wout, norm2_w)
    silu, up_dsilu, act, wgate, wup = _gate_up(u2, wgu_g)
    dh2, loss_p, g_w3 = _down_loss(act, wdown, h1, tgt, w3)

    (g_wdown,) = _gw(act, [dh2], 512, "gw_down")
    dgu2 = _dact(dh2, wdown, silu, up_dsilu)
    early = [_gw_by_owner(u2, dgu2, 2 * FFN // N_DEV, "gw_gate_up", 2048), g_wdown.reshape(N_DEV, FFN // N_DEV, D_MODEL)]
    dh1, g_w2, dmix, *got_early = _dgu(dgu2, wgate, wup, h1, norm2_w, dh2, wout, _sibling_rider(early))
    sums_early = [_add_sibling(core, g, o, f"add_sibling_{i}") for i, (g, o) in enumerate(zip(early, got_early))]
    (g_wout,) = _gw(mixed, [dh1], 1024, "gw_out")
    mid = [g_wout.reshape(N_DEV, D_MODEL // N_DEV, D_MODEL)]
    dhq, dhf, dhi, dhg, g_wn, g_lb, *rode = _hgrn_bwd(hp, lb_logits, hgrn_norm_w, o_sav, states, dmix,
                                                      _both(_chips_rider(sums_early), _sibling_rider(mid)))
    pieces_early, got_mid = rode[:2], rode[2:]
    sums_mid = [_add_sibling(core, mid[0], got_mid[0], "add_sibling_2")]
    dq, dk, dv, *pieces_mid = _attn_bwd(qkv, ya, lse, dmix, _chips_rider(sums_mid))
    dproj, gx, g_w1 = _din(dq, dk, dv, dhq, dhf, dhi, dhg, cos_t, sg_t, win, xs, norm1_w, dh1)
    late = [_gw_by_owner(u, dproj[None], IN_W // N_DEV, "gw_in", 2048)]
    got_late = _alone(_sibling_rider(late), "reduce_sibling")
    sums_late = [_add_sibling(core, late[0], got_late[0], "add_sibling_3")]
    pieces_late = _alone(_chips_rider(sums_late), "reduce_chips")

    grads = [late[0], mid[0], early[0], early[1]]
    got = [got_late[0], got_mid[0], got_early[0], got_early[1]]
    pieces = [pieces_late[0], pieces_mid[0], pieces_early[0], pieces_early[1]]
    shards = [w_in[0], w_out[0], w_gate_up[0], w_down[0]]
    moms = [(m_w_in[0], v_w_in[0]), (m_w_out[0], v_w_out[0]), (m_w_gate_up[0], v_w_gate_up[0]), (m_w_down[0], v_w_down[0])]
    big = [_adam_shard(where, g, o, p, w, m, v, f"adam_{i}")
           for i, (g, o, p, w, (m, v)) in enumerate(zip(grads, got, pieces, shards, moms))]
    big = [[a[None] for a in four] for four in big]

    gath = _gather_small(g_w1, g_w2, g_w3, g_lb, g_wn, loss_p)
    params = [(norm1_w, m_norm1_w, v_norm1_w), (norm2_w, m_norm2_w, v_norm2_w),
              (row(final_norm_w), row(m_final_norm_w), row(v_final_norm_w)),
              (lb_logits, m_lb_logits, v_lb_logits), (hgrn_norm_w, m_hgrn_norm_w, v_hgrn_norm_w)]
    loss, (s_w1, s_w2, s_w3, s_lb, s_wn) = _small_update(gath, params)
    s_w3 = [a.reshape(D_MODEL) for a in s_w3]
    per_w = [s_w1, big[0], s_lb, s_wn, big[1], s_w2, big[2], big[3], s_w3]
    return (loss[0, 0], gx[None], *[p[0] for p in per_w], *[p[1] for p in per_w], *[p[2] for p in per_w], *[p[3] for p in per_w])
```

```python
import jax
import jax.numpy as jnp
from jax import lax
from jax.experimental import pallas as pl
from jax.experimental.pallas import tpu as pltpu

f32, bf16 = jnp.float32, jnp.bfloat16

D_MODEL = 1024
ATTN_W = 512
HEAD_DIM = 64
ATTN_BLK = 128
DILATIONS = (1, 4, 16)
HGRN_W = 512
HGRN_HD = 128
CHUNK = 64
IN_W = 3 * ATTN_W + 4 * HGRN_W
FFN = 2816
EPS = 1e-6
ROPE_THETA = 10000.0
NEG = -1e30
N_DEV = 8
ADAM_LR, ADAM_B1, ADAM_B2, ADAM_EPS, ADAM_WD, ADAM_STEP = 0.001, 0.9, 0.999, 1e-08, 0.01, 10
VMEM_LIMIT = 56 * 1024 * 1024


def _cp(*sem):
    return pltpu.CompilerParams(dimension_semantics=sem, vmem_limit_bytes=VMEM_LIMIT)


def _dot(a, b):
    return jnp.dot(a, b, preferred_element_type=f32)


def _dot_nt(a, b):
    return lax.dot_general(a, b, (((1,), (1,)), ((), ())), preferred_element_type=f32)


def _dot_tn(a, b):
    return lax.dot_general(a, b, (((0,), (0,)), ((), ())), preferred_element_type=f32)


def _sigmoid(x):
    return 0.5 * jnp.tanh(0.5 * x) + 0.5


class _Rider:
    def __init__(self, ins, out_shapes, scratch, first, last, middle=None):
        self.ins, self.out_shapes, self.scratch = list(ins), list(out_shapes), list(scratch)
        self.first, self.middle, self.last = first, middle, last


def _ride(call, rider, body, step, n_steps, n_in, n_out, n_scratch):
    if rider is None:
        return call, body, []
    ri, ro = len(rider.ins), len(rider.out_shapes)
    any_spec = pl.BlockSpec(memory_space=pl.ANY)
    call = dict(call, in_specs=call["in_specs"] + [any_spec] * ri, out_specs=call["out_specs"] + [any_spec] * ro,
                out_shape=call["out_shape"] + rider.out_shapes, scratch_shapes=call["scratch_shapes"] + rider.scratch)

    def riding(*refs):
        a = n_in + ri
        b = a + n_out + ro
        mine = refs[:n_in] + refs[a:a + n_out] + refs[b:b + n_scratch]
        theirs = (refs[n_in:a], refs[a + n_out:b], refs[b + n_scratch:])
        t = step()

        @pl.when(t == 0)
        def _():
            rider.first(*theirs)

        body(*mine)
        if rider.middle is not None:
            @pl.when(t == n_steps // 2)
            def _():
                rider.middle(*theirs)

        @pl.when(t == n_steps - 1)
        def _():
            rider.last(*theirs)

    return call, riding, rider.ins


def _rope_tables(S, rider=None):
    half = HEAD_DIM // 2
    tm = 256
    inv_freq = jnp.tile(ROPE_THETA ** (-jnp.arange(half, dtype=f32) / half), 128 // half).reshape(1, 128)
    sign = jnp.tile(jnp.concatenate([-jnp.ones((half,), f32), jnp.ones((half,), f32)]), 128 // HEAD_DIM).reshape(1, 128)

    def body(inv_ref, sign_ref, cos_ref, sg_ref):
        pos = (lax.broadcasted_iota(jnp.int32, (tm, 128), 0) + pl.program_id(0) * tm).astype(f32)
        ang = pos * inv_ref[...]
        cos_ref[...] = jnp.cos(ang)
        sg_ref[...] = jnp.sin(ang) * sign_ref[...]

    vec = pl.BlockSpec((1, 128), lambda i: (0, 0))
    out = pl.BlockSpec((tm, 128), lambda i: (i, 0))
    call = dict(in_specs=[vec, vec], out_specs=[out, out], out_shape=[jax.ShapeDtypeStruct((S, 128), f32)] * 2, scratch_shapes=[])
    call, body, more = _ride(call, rider, body, lambda: pl.program_id(0), S // tm, 2, 2, 0)
    return pl.pallas_call(body, name="rope_tables", grid=(S // tm,), compiler_params=_cp("arbitrary"), **call)(inv_freq, sign, *more)


def _swap_halves(v):
    n = v.shape[1]
    lane = lax.broadcasted_iota(jnp.int32, v.shape, 1)
    return jnp.where((lane % HEAD_DIM) < HEAD_DIM // 2, pltpu.roll(v, n - HEAD_DIM // 2, 1), pltpu.roll(v, HEAD_DIM // 2, 1))


def _in_proj(x, w1, win_g, cos_t, sg_t):
    S = x.shape[0]
    tm = 256
    w = IN_W // N_DEV

    def body(x_ref, w1_ref, wg_ref, cos_ref, sg_ref, u_ref, qkv_ref, hp_ref, w_ref):
        @pl.when(pl.program_id(0) == 0)
        def _():
            for d in range(N_DEV):
                w_ref[:, w * d:w * (d + 1)] = wg_ref[d]

        xv = x_ref[...]
        r = lax.rsqrt(jnp.mean(xv * xv, axis=-1, keepdims=True) + EPS)
        u = (xv * r * w1_ref[...]).astype(bf16)
        u_ref[...] = u
        cosv, sgv = jnp.tile(cos_ref[...], (1, ATTN_W // 128)), jnp.tile(sg_ref[...], (1, ATTN_W // 128))
        for j in range(3):
            pj = _dot(u, w_ref[:, j * ATTN_W:(j + 1) * ATTN_W])
            if j < 2:
                pj = pj * cosv + _swap_halves(pj) * sgv
            if j == 0:
                pj = pj * (HEAD_DIM ** -0.5)
            qkv_ref[:, j * ATTN_W:(j + 1) * ATTN_W] = pj.astype(bf16)
        for j in range(4):
            lo = 3 * ATTN_W + j * HGRN_W
            hp_ref[:, j * HGRN_W:(j + 1) * HGRN_W] = _dot(u, w_ref[:, lo:lo + HGRN_W])

    return pl.pallas_call(
        body, name="in_proj", grid=(S // tm,),
        in_specs=[pl.BlockSpec((tm, D_MODEL), lambda i: (i, 0)), pl.BlockSpec((1, D_MODEL), lambda i: (0, 0)),
                  pl.BlockSpec((N_DEV, D_MODEL, w), lambda i: (0, 0, 0)),
                  pl.BlockSpec((tm, 128), lambda i: (i, 0)), pl.BlockSpec((tm, 128), lambda i: (i, 0))],
        out_specs=[pl.BlockSpec((tm, D_MODEL), lambda i: (i, 0)), pl.BlockSpec((tm, 3 * ATTN_W), lambda i: (i, 0)),
                   pl.BlockSpec((tm, 4 * HGRN_W), lambda i: (i, 0)), pl.BlockSpec((D_MODEL, IN_W), lambda i: (0, 0))],
        out_shape=[jax.ShapeDtypeStruct((S, D_MODEL), bf16), jax.ShapeDtypeStruct((S, 3 * ATTN_W), bf16),
                   jax.ShapeDtypeStruct((S, 4 * HGRN_W), f32), jax.ShapeDtypeStruct((D_MODEL, IN_W), bf16)],
        compiler_params=_cp("arbitrary"),
    )(x, w1, win_g, cos_t, sg_t)


def _head_masks():
    lane = lax.broadcasted_iota(jnp.int32, (ATTN_BLK, 128), 1)
    even = lane < HEAD_DIM
    return even, (even, jnp.logical_not(even))


def _pair_fwd(q2, k2, v2, bias):
    even, masks = _head_masks()
    outs, lses = [], []
    for e in range(2):
        qm = jnp.where(masks[e], q2, 0.0).astype(bf16)
        s = _dot_nt(qm, k2) + bias
        m = jnp.max(s, axis=-1, keepdims=True)
        pe = jnp.exp(s - m)
        lsum = jnp.sum(pe, axis=-1, keepdims=True)
        outs.append(_dot(pe.astype(bf16), v2) / lsum)
        lses.append(jnp.broadcast_to(m + jnp.log(lsum), (ATTN_BLK, 128)))
    return jnp.where(even, outs[0], outs[1]), jnp.where(even, lses[0], lses[1])


def _merge(y0, l0, y1, l1):
    mx = jnp.maximum(l0, l1)
    a, b = jnp.exp(l0 - mx), jnp.exp(l1 - mx)
    tot = a + b
    return (a * y0 + b * y1) / tot, mx + jnp.log(tot)


def _pair_bwd(q2, k2f, v2, dy2, lse2, delta2, bias):
    _, masks = _head_masks()
    k2 = k2f.astype(bf16)
    klane = lax.broadcasted_iota(jnp.int32, (2 * ATTN_BLK, 128), 1) < HEAD_DIM
    kmasks = (klane, jnp.logical_not(klane))
    dq2 = jnp.zeros((ATTN_BLK, 128), f32)
    pes, dss, qms, dyms = [], [], [], []
    for e in range(2):
        c0 = e * HEAD_DIM
        qm = jnp.where(masks[e], q2, 0.0).astype(bf16)
        km = jnp.where(kmasks[e], k2f, 0.0).astype(bf16)
        dym = jnp.where(masks[e], dy2, 0.0).astype(bf16)
        pe = jnp.exp(_dot_nt(qm, k2) + bias - lse2[:, c0:c0 + 1])
        ds = (pe * (_dot_nt(dym, v2) - delta2[:, c0:c0 + 1])).astype(bf16)
        dq2 = dq2 + _dot(ds, km)
        pes.append(pe.astype(bf16))
        dss.append(ds)
        qms.append(qm)
        dyms.append(dym)
    dv2 = _dot_tn(jnp.concatenate(pes, axis=0), jnp.concatenate(dyms, axis=0))
    dk2 = _dot_tn(jnp.concatenate(dss, axis=0), jnp.concatenate(qms, axis=0))
    return dq2, dk2, dv2


TOK = 2048


def _key_bias():
    qi = lax.broadcasted_iota(jnp.int32, (ATTN_BLK, 2 * ATTN_BLK), 0)
    kj = lax.broadcasted_iota(jnp.int32, (ATTN_BLK, 2 * ATTN_BLK), 1)
    delta = ATTN_BLK + qi - kj
    seen = (delta >= 0) & (delta <= ATTN_BLK)
    return jnp.where(seen, 0.0, NEG), jnp.where(seen & (kj >= ATTN_BLK), 0.0, NEG)


def _attn_fwd(qkv, rider=None):
    S = qkv.shape[0]
    nS = S // TOK

    def body(q_ref, kp_ref, kc_ref, vp_ref, vc_ref, y_ref, l_ref, qs, k2, v2, ay, al):
        n = pl.program_id(1)
        qs[...] = q_ref[...].astype(f32)
        k2[0:TOK] = kp_ref[...].astype(f32)
        k2[TOK:2 * TOK] = kc_ref[...].astype(f32)
        v2[0:TOK] = vp_ref[...].astype(f32)
        v2[TOK:2 * TOK] = vc_ref[...].astype(f32)
        bias_any, bias_first = _key_bias()

        def block(dil, r, b, step, last):
            start = r + pl.multiple_of(step * b, step)
            rows = pl.ds(start, ATTN_BLK, stride=dil) if dil > 1 else pl.ds(start, ATTN_BLK)
            keys = (pl.ds(TOK + start - step, 2 * ATTN_BLK, stride=dil) if dil > 1
                    else pl.ds(TOK + start - step, 2 * ATTN_BLK))
            bias = jnp.where((n == 0) & (b == 0), bias_first, bias_any)
            out, lse = _pair_fwd(qs[rows, :], k2[keys, :].astype(bf16), v2[keys, :].astype(bf16), bias)
            if dil < DILATIONS[-1]:
                out, lse = _merge(ay[rows, :], al[rows, :], out, lse)
            if last:
                y_ref[rows, :] = out
                l_ref[rows, :] = lse
            else:
                ay[rows, :] = out
                al[rows, :] = lse

        for dil in reversed(DILATIONS):
            def loop(i, carry, dil=dil):
                block(dil, i % dil, i // dil, ATTN_BLK * dil, dil == 1)
                return carry
            lax.fori_loop(0, TOK // ATTN_BLK, loop, 0, unroll=8)

    blk = (TOK, 128)
    cur = lambda c: pl.BlockSpec(blk, lambda p, n: (n, 4 * c + p))
    prv = lambda c: pl.BlockSpec(blk, lambda p, n: (jnp.maximum(n - 1, 0), 4 * c + p))
    out = pl.BlockSpec(blk, lambda p, n: (n, p))
    call = dict(in_specs=[cur(0), prv(1), cur(1), prv(2), cur(2)], out_specs=[out, out],
                out_shape=[jax.ShapeDtypeStruct((S, ATTN_W), f32)] * 2,
                scratch_shapes=[pltpu.VMEM(blk, f32), pltpu.VMEM((2 * TOK, 128), f32), pltpu.VMEM((2 * TOK, 128), f32),
                                pltpu.VMEM(blk, f32), pltpu.VMEM(blk, f32)])
    call, body, more = _ride(call, rider, body, lambda: pl.program_id(0) * nS + pl.program_id(1), (ATTN_W // 128) * nS, 5, 2, 5)
    return pl.pallas_call(body, name="attention_fwd", grid=(ATTN_W // 128, nS), compiler_params=_cp("arbitrary", "arbitrary"),
                          **call)(qkv, qkv, qkv, qkv, qkv, *more)


def _attn_bwd(qkv, ya, lse, dmix, rider=None):
    S = qkv.shape[0]
    nS = S // TOK

    def body(q_ref, kp_ref, kc_ref, vp_ref, vc_ref, y_ref, l_ref, dy_ref, dq_ref, dk_ref, dv_ref, qs, k2, v2, dk2, dv2, dqa, dl):
        n = pl.program_id(1)

        @pl.when(n == 0)
        def _():
            dk2[...] = jnp.zeros_like(dk2)
            dv2[...] = jnp.zeros_like(dv2)

        @pl.when(n < nS)
        def _():
            qs[...] = q_ref[...].astype(f32)
            k2[0:TOK] = kp_ref[...].astype(f32)
            k2[TOK:2 * TOK] = kc_ref[...].astype(f32)
            v2[0:TOK] = vp_ref[...].astype(f32)
            v2[TOK:2 * TOK] = vc_ref[...].astype(f32)
            li = lax.broadcasted_iota(jnp.int32, (128, 128), 0)
            lj = lax.broadcasted_iota(jnp.int32, (128, 128), 1)
            seg = jnp.where((li // HEAD_DIM) == (lj // HEAD_DIM), 1.0, 0.0).astype(bf16)
            bias_any, bias_first = _key_bias()

            def delta_rows(t, carry):
                rows = pl.ds(pl.multiple_of(256 * t, 256), 256)
                dyy = dy_ref[rows, :] * y_ref[rows, :]
                hi = dyy.astype(bf16)
                dl[rows, :] = _dot(hi, seg) + _dot((dyy - hi.astype(f32)).astype(bf16), seg)
                return carry

            lax.fori_loop(0, TOK // 256, delta_rows, 0)

            def block(dil, r, b, step, first_pattern, last):
                start = r + pl.multiple_of(step * b, step)
                rows = pl.ds(start, ATTN_BLK, stride=dil) if dil > 1 else pl.ds(start, ATTN_BLK)
                keys = (pl.ds(TOK + start - step, 2 * ATTN_BLK, stride=dil) if dil > 1
                        else pl.ds(TOK + start - step, 2 * ATTN_BLK))
                bias = jnp.where((n == 0) & (b == 0), bias_first, bias_any)
                dq2, dkk, dvv = _pair_bwd(qs[rows, :], k2[keys, :], v2[keys, :].astype(bf16), dy_ref[rows, :],
                                          l_ref[rows, :], dl[rows, :], bias)
                if last:
                    dq_ref[rows, :] = dqa[rows, :] + dq2
                elif first_pattern:
                    dqa[rows, :] = dq2
                else:
                    dqa[rows, :] += dq2
                dk2[keys, :] += dkk
                dv2[keys, :] += dvv

            for dil in reversed(DILATIONS):
                def loop(i, carry, dil=dil):
                    block(dil, i % dil, i // dil, ATTN_BLK * dil, dil == DILATIONS[-1], dil == 1)
                    return carry
                lax.fori_loop(0, TOK // ATTN_BLK, loop, 0, unroll=8)

        dk_ref[...] = dk2[0:TOK]
        dv_ref[...] = dv2[0:TOK]
        dk2[0:TOK] = dk2[TOK:2 * TOK]
        dv2[0:TOK] = dv2[TOK:2 * TOK]
        dk2[TOK:2 * TOK] = jnp.zeros((TOK, 128), f32)
        dv2[TOK:2 * TOK] = jnp.zeros((TOK, 128), f32)

    blk = (TOK, 128)
    cn = lambda n: jnp.minimum(n, nS - 1)
    pn = lambda n: jnp.clip(n - 1, 0, nS - 1)
    cur = lambda c: pl.BlockSpec(blk, lambda p, n: (cn(n), 4 * c + p))
    prv = lambda c: pl.BlockSpec(blk, lambda p, n: (pn(n), 4 * c + p))
    at_n = pl.BlockSpec(blk, lambda p, n: (cn(n), p))
    at_p = pl.BlockSpec(blk, lambda p, n: (pn(n), p))
    big = lambda: pltpu.VMEM((2 * TOK, 128), f32)
    call = dict(in_specs=[cur(0), prv(1), cur(1), prv(2), cur(2), at_n, at_n, at_n], out_specs=[at_n, at_p, at_p],
                out_shape=[jax.ShapeDtypeStruct((S, ATTN_W), f32)] * 3,
                scratch_shapes=[pltpu.VMEM(blk, f32), big(), big(), big(), big(), pltpu.VMEM(blk, f32), pltpu.VMEM(blk, f32)])
    call, body, more = _ride(call, rider, body, lambda: pl.program_id(0) * (nS + 1) + pl.program_id(1),
                             (ATTN_W // 128) * (nS + 1), 8, 3, 7)
    return pl.pallas_call(body, name="attention_bwd", grid=(ATTN_W // 128, nS + 1), compiler_params=_cp("arbitrary", "arbitrary"),
                          **call)(qkv, qkv, qkv, qkv, qkv, ya, lse, dmix, *more)


HG_T = 256
N_HH = HGRN_W // HGRN_HD
HG_SUB = 128
SAFE_RANGE = -1.0


def _row_in_chunk():
    return lax.broadcasted_iota(jnp.int32, (HG_T, HGRN_HD), 0) % CHUNK


def _chunk_cumsum(v, rc):
    k = 1
    while k < CHUNK:
        v = v + jnp.where(rc >= k, pltpu.roll(v, k, 0), 0.0)
        k *= 2
    return v


def _chunk_rcumsum(v, rc):
    k = 1
    while k < CHUNK:
        v = v + jnp.where(rc < CHUNK - k, pltpu.roll(v, HG_T - k, 0), 0.0)
        k *= 2
    return v


def _hgrn_gates(qb, fb, lb):
    sf = _sigmoid(fb)
    f = lb + (1.0 - lb) * sf
    sq = _sigmoid(qb)
    return sf, f, jnp.log(f), 1.0 - f, sq, qb * sq


def _hgrn_prep(qb, fb, lbl2, rc):
    lb = _sigmoid(lbl2[0:1, :] - lbl2[1:2, :])
    sf, f, lf, key, sq, qf = _hgrn_gates(qb, fb, lb)
    b = _chunk_cumsum(lf, rc)
    rem = _chunk_rcumsum(lf, rc) - lf
    return dict(lb=lb, sf=sf, f=f, key=key, sq=sq, qf=qf, b=b, rem=rem, eb=jnp.exp(b), er=jnp.exp(rem))


def _chunk_mask():
    r = lax.broadcasted_iota(jnp.int32, (HG_SUB, HG_SUB), 0)
    c = lax.broadcasted_iota(jnp.int32, (HG_SUB, HG_SUB), 1)
    return ((r // CHUNK) == (c // CHUNK)) & (c <= r)


def _hgrn_fwd(hp, lbl, wn):
    S = hp.shape[0]
    nT = S // HG_T

    def body(qb_ref, fb_ref, ib_ref, gb_ref, lbl_ref, wn_ref, yb_ref, o_ref, st_ref, ST, qt_s, kh_s, dec_s, oi_s):
        @pl.when(pl.program_id(0) == 0)
        def _():
            ST[...] = jnp.zeros_like(ST)

        rc = _row_in_chunk()
        for h in range(N_HH):
            sl = slice(HGRN_HD * h, HGRN_HD * (h + 1))
            p = _hgrn_prep(qb_ref[:, sl], fb_ref[:, sl], lbl_ref[:, sl], rc)
            qf, key, b = p["qf"], p["key"], p["b"]
            qt = qf * p["eb"]
            qt_s[:, sl] = qt.astype(bf16)
            kh_s[:, sl] = (key * p["er"]).astype(bf16)
            dec_s[:, sl] = jnp.exp(b + p["rem"])
            rng = jnp.max(-(b + p["rem"]))

            @pl.when(rng < SAFE_RANGE)
            def _():
                kp = (key * jnp.exp(-b)).astype(bf16)
                cmask = _chunk_mask()
                for j in range(HG_T // HG_SUB):
                    rs = slice(HG_SUB * j, HG_SUB * (j + 1))
                    sc = jnp.where(cmask, _dot_nt(qt[rs].astype(bf16), kp[rs]), 0.0).astype(bf16)
                    oi_s[rs, sl] = _dot(sc, ib_ref[rs, sl].astype(bf16))

            @pl.when(rng >= SAFE_RANGE)
            def _():
                v = ib_ref[:, sl]
                ones = jnp.ones((HGRN_HD, HGRN_HD), bf16)

                def lag(l, o):
                    e = jnp.exp(jnp.where(rc >= l, b - pltpu.roll(b, l, 0), NEG))
                    pr = qf * pltpu.roll(key, l, 0) * e
                    return o + _dot(pr.astype(bf16), ones) * pltpu.roll(v, l, 0)

                oi_s[:, sl] = lax.fori_loop(1, CHUNK, lag, _dot((qf * key).astype(bf16), ones) * v)

        def step(c, carry):
            rows = pl.ds(pl.multiple_of(c * CHUNK, CHUNK), CHUNK)
            row0 = pl.ds(pl.multiple_of(c * CHUNK, CHUNK), 1)
            for h in range(N_HH):
                sl = slice(HGRN_HD * h, HGRN_HD * (h + 1))
                stv = ST[h]
                st_ref[c, sl, :] = stv
                oi_s[rows, sl] += _dot_nt(qt_s[rows, sl], stv.astype(bf16))
                ST[h] = stv * dec_s[row0, sl] + _dot_tn(ib_ref[rows, sl].astype(bf16), kh_s[rows, sl])
            return carry

        lax.fori_loop(0, HG_T // CHUNK, step, 0, unroll=True)

        for h in range(N_HH):
            sl = slice(HGRN_HD * h, HGRN_HD * (h + 1))
            o = oi_s[:, sl]
            o_ref[:, sl] = o
            on = o * lax.rsqrt(jnp.mean(o * o, axis=-1, keepdims=True) + EPS)
            g = gb_ref[:, sl]
            yb_ref[:, sl] = on * wn_ref[:, sl] * (g * _sigmoid(g))

    col = lambda c: pl.BlockSpec((HG_T, HGRN_W), lambda i: (i, c))
    tile = pl.BlockSpec((HG_T, HGRN_W), lambda i: (i, 0))
    whole = lambda a: pl.BlockSpec(a.shape, lambda i: (0, 0))
    return pl.pallas_call(
        body, name="hgrn_fwd", grid=(nT,),
        in_specs=[col(0), col(1), col(2), col(3), whole(lbl), whole(wn)],
        out_specs=[tile, tile, pl.BlockSpec((HG_T // CHUNK, HGRN_W, HGRN_HD), lambda i: (i, 0, 0))],
        out_shape=[jax.ShapeDtypeStruct((S, HGRN_W), f32), jax.ShapeDtypeStruct((S, HGRN_W), f32),
                   jax.ShapeDtypeStruct((S // CHUNK, HGRN_W, HGRN_HD), f32)],
        scratch_shapes=[pltpu.VMEM((N_HH, HGRN_HD, HGRN_HD), f32), pltpu.VMEM((HG_T, HGRN_W), bf16),
                        pltpu.VMEM((HG_T, HGRN_W), bf16), pltpu.VMEM((HG_T, HGRN_W), f32), pltpu.VMEM((HG_T, HGRN_W), f32)],
        compiler_params=_cp("arbitrary"),
    )(hp, hp, hp, hp, lbl, wn)


def _hgrn_bwd(hp, lbl, wn, o_sav, states, dmix, rider=None):
    S = hp.shape[0]
    nT = S // HG_T

    def body(qb_ref, fb_ref, ib_ref, gb_ref, lbl_ref, wn_ref, o_ref, st_ref, dy_ref,
             dq_ref, df_ref, di_ref, dg_ref, gwn_ref, glb_ref,
             DST, qt_s, kh_s, dec_s, do_s, dqt_s, dkh_s, dbl_s, dvi_s, dqi_s, dki_s, dbi_s):
        @pl.when(pl.program_id(0) == 0)
        def _():
            DST[...] = jnp.zeros_like(DST)
            gwn_ref[...] = jnp.zeros_like(gwn_ref)
            glb_ref[...] = jnp.zeros_like(glb_ref)

        rc = _row_in_chunk()
        preps = []
        for h in range(N_HH):
            sl = slice(HGRN_HD * h, HGRN_HD * (h + 1))
            p = _hgrn_prep(qb_ref[:, sl], fb_ref[:, sl], lbl_ref[:, sl], rc)
            preps.append(p)
            qf, key, b = p["qf"], p["key"], p["b"]
            v = ib_ref[:, sl]
            o = o_ref[:, sl]
            rinv = lax.rsqrt(jnp.mean(o * o, axis=-1, keepdims=True) + EPS)
            on = o * rinv
            g = gb_ref[:, sl]
            sgm = _sigmoid(g)
            silu_g = g * sgm
            dy = dy_ref[:, sl]
            wn_v = wn_ref[:, sl]
            gwn_ref[:, sl] += jnp.sum(dy * on * silu_g, axis=0, keepdims=True)
            dg_ref[:, sl] = (dy * on * wn_v * (sgm * (1.0 + g * (1.0 - sgm)))).astype(bf16)
            t1 = dy * wn_v * silu_g
            do = rinv * (t1 - on * jnp.mean(t1 * on, axis=-1, keepdims=True))
            do_s[:, sl] = do.astype(bf16)
            qt = qf * p["eb"]
            qt_s[:, sl] = qt.astype(bf16)
            kh_s[:, sl] = (key * p["er"]).astype(bf16)
            dec_s[:, sl] = jnp.exp(b + p["rem"])
            rng = jnp.max(-(b + p["rem"]))

            @pl.when(rng < SAFE_RANGE)
            def _():
                einv = jnp.exp(-b)
                kp = (key * einv).astype(bf16)
                cmask = _chunk_mask()
                for j in range(HG_T // HG_SUB):
                    rs = slice(HG_SUB * j, HG_SUB * (j + 1))
                    qtb, dob, vb = qt[rs].astype(bf16), do[rs].astype(bf16), v[rs].astype(bf16)
                    sc = jnp.where(cmask, _dot_nt(qtb, kp[rs]), 0.0).astype(bf16)
                    dsc = jnp.where(cmask, _dot_nt(dob, vb), 0.0).astype(bf16)
                    dqp = _dot(dsc, kp[rs])
                    dkp = _dot_tn(dsc, qtb)
                    dvi_s[rs, sl] = _dot_tn(sc, dob)
                    dqi_s[rs, sl] = dqp * p["eb"][rs]
                    dki_s[rs, sl] = dkp * einv[rs]
                    dbi_s[rs, sl] = dqp * qtb.astype(f32) - dkp * kp[rs].astype(f32)

            @pl.when(rng >= SAFE_RANGE)
            def _():
                ones = jnp.ones((HGRN_HD, HGRN_HD), bf16)

                def lag(l, carry):
                    dqf, dkey, db, dv = carry
                    e = jnp.exp(jnp.where(rc >= l, b - pltpu.roll(b, l, 0), NEG))
                    ks, vs, qe = pltpu.roll(key, l, 0), pltpu.roll(v, l, 0), qf * e
                    pr = qe * ks
                    rl = _dot(pr.astype(bf16), ones)
                    drl = jnp.where(rc >= l, _dot((do * vs).astype(bf16), ones), 0.0)
                    gl = drl * pr
                    back = HG_T - l
                    return (dqf + drl * ks * e, dkey + pltpu.roll(drl * qe, back, 0), db + gl - pltpu.roll(gl, back, 0),
                            dv + pltpu.roll(rl * do, back, 0))

                rl0 = _dot((qf * key).astype(bf16), ones)
                drl0 = _dot((do * v).astype(bf16), ones)
                dqf, dkey, db, dv = lax.fori_loop(1, CHUNK, lag, (drl0 * key, drl0 * qf, jnp.zeros((HG_T, HGRN_HD), f32), rl0 * do))
                dvi_s[:, sl] = dv
                dqi_s[:, sl] = dqf
                dki_s[:, sl] = dkey
                dbi_s[:, sl] = db

        def step(k, carry):
            c = HG_T // CHUNK - 1 - k
            rows = pl.ds(pl.multiple_of(c * CHUNK, CHUNK), CHUNK)
            row0 = pl.ds(pl.multiple_of(c * CHUNK, CHUNK), 1)
            for h in range(N_HH):
                sl = slice(HGRN_HD * h, HGRN_HD * (h + 1))
                stp = st_ref[c, sl, :]
                dst = DST[h]
                dstb = dst.astype(bf16)
                dob = do_s[rows, sl]
                khb = kh_s[rows, sl]
                dec = dec_s[row0, sl]
                dqt_s[rows, sl] = _dot(dob, stp.astype(bf16))
                dkh = _dot(ib_ref[rows, sl].astype(bf16), dstb)
                dkh_s[rows, sl] = dkh
                dvi_s[rows, sl] += _dot_nt(khb, dstb)
                dbl = jnp.sum(dst * stp, axis=0, keepdims=True) * dec + jnp.sum(dkh * khb.astype(f32), axis=0, keepdims=True)
                dbl_s[rows, sl] = jnp.broadcast_to(dbl, (CHUNK, HGRN_HD))
                DST[h] = dst * dec + _dot_tn(dob, qt_s[rows, sl])
            return carry

        lax.fori_loop(0, HG_T // CHUNK, step, 0, unroll=True)

        for h in range(N_HH):
            sl = slice(HGRN_HD * h, HGRN_HD * (h + 1))
            qb = qb_ref[:, sl]
            p = preps[h]
            sf, sq, lb = p["sf"], p["sq"], p["lb"]
            dqt, dkh = dqt_s[:, sl], dkh_s[:, sl]
            dqf = dqt * p["eb"] + dqi_s[:, sl]
            dkey = dkh * p["er"] + dki_s[:, sl]
            db = dqt * (p["qf"] * p["eb"]) - dkh * (p["key"] * p["er"]) + jnp.where(rc == CHUNK - 1, dbl_s[:, sl], 0.0) + dbi_s[:, sl]
            df = _chunk_rcumsum(db, rc) / p["f"] - dkey
            df_ref[:, sl] = (df * (1.0 - lb) * sf * (1.0 - sf)).astype(bf16)
            glb_ref[:, sl] += jnp.sum(df * (1.0 - sf), axis=0, keepdims=True)
            dq_ref[:, sl] = (dqf * (sq * (1.0 + qb * (1.0 - sq)))).astype(bf16)
            di_ref[:, sl] = dvi_s[:, sl].astype(bf16)

    rev = lambda i: nT - 1 - i
    col = lambda c: pl.BlockSpec((HG_T, HGRN_W), lambda i: (rev(i), c))
    tile = pl.BlockSpec((HG_T, HGRN_W), lambda i: (rev(i), 0))
    whole = lambda a: pl.BlockSpec(a.shape, lambda i: (0, 0))
    vec = pl.BlockSpec((1, HGRN_W), lambda i: (0, 0))
    tb = lambda: pltpu.VMEM((HG_T, HGRN_W), bf16)
    tf = lambda: pltpu.VMEM((HG_T, HGRN_W), f32)
    call = dict(in_specs=[col(0), col(1), col(2), col(3), whole(lbl), whole(wn), tile,
                          pl.BlockSpec((HG_T // CHUNK, HGRN_W, HGRN_HD), lambda i: (rev(i), 0, 0)),
                          pl.BlockSpec((HG_T, HGRN_W), lambda i: (rev(i), 1))],
                out_specs=[tile, tile, tile, tile, vec, vec],
                out_shape=[jax.ShapeDtypeStruct((S, HGRN_W), bf16)] * 4 + [jax.ShapeDtypeStruct((1, HGRN_W), f32)] * 2,
                scratch_shapes=[pltpu.VMEM((N_HH, HGRN_HD, HGRN_HD), f32), tb(), tb(), tf(), tb(), tf(), tf(), tf(), tf(), tf(),
                                tf(), tf()])
    call, body, more = _ride(call, rider, body, lambda: pl.program_id(0), nT, 9, 6, 12)
    return pl.pallas_call(body, name="hgrn_bwd", grid=(nT,), compiler_params=_cp("arbitrary"), **call)(
        hp, hp, hp, hp, lbl, wn, o_sav, states, dmix, *more)


def _out_proj(x, ya, yb, wout, w2):
    S = x.shape[0]
    tm = 512

    def body(x_ref, ya_ref, yb_ref, w_ref, w2_ref, h1_ref, u2_ref, mix_ref):
        mixed = jnp.concatenate([ya_ref[...], yb_ref[...]], axis=1).astype(bf16)
        mix_ref[...] = mixed
        h1 = x_ref[...] + _dot(mixed, w_ref[...])
        h1_ref[...] = h1
        r = lax.rsqrt(jnp.mean(h1 * h1, axis=-1, keepdims=True) + EPS)
        u2_ref[...] = (h1 * r * w2_ref[...]).astype(bf16)

    row = lambda w: pl.BlockSpec((tm, w), lambda i: (i, 0))
    return pl.pallas_call(
        body, name="out_proj", grid=(S // tm,),
        in_specs=[row(D_MODEL), row(ATTN_W), row(HGRN_W), pl.BlockSpec((D_MODEL, D_MODEL), lambda i: (0, 0)),
                  pl.BlockSpec((1, D_MODEL), lambda i: (0, 0))],
        out_specs=[row(D_MODEL), row(D_MODEL), row(D_MODEL)],
        out_shape=[jax.ShapeDtypeStruct((S, D_MODEL), f32), jax.ShapeDtypeStruct((S, D_MODEL), bf16),
                   jax.ShapeDtypeStruct((S, D_MODEL), bf16)],
        compiler_params=_cp("arbitrary"),
    )(x, ya, yb, wout, w2)


def _gate_up(u2, wgu_g):
    S = u2.shape[0]
    w = 2 * FFN // N_DEV
    tm, tn = 512, 2 * w
    nj = FFN // tn

    def body(u_ref, wgg_ref, wug_ref, g_ref, up_ref, a_ref, wg_ref, wu_ref):
        @pl.when(pl.program_id(1) == 0)
        def _():
            for k in range(2):
                wg_ref[:, w * k:w * (k + 1)] = wgg_ref[k]
                wu_ref[:, w * k:w * (k + 1)] = wug_ref[k]

        u = u_ref[...]
        g = _dot(u, wg_ref[...])
        up = _dot(u, wu_ref[...])
        sg = _sigmoid(g)
        silu = g * sg
        g_ref[...] = silu.astype(bf16)
        up_ref[...] = (up * (sg + silu * (1.0 - sg))).astype(bf16)
        a_ref[...] = (silu * up).astype(bf16)

    out = pl.BlockSpec((tm, tn), lambda j, i: (i, j))
    wout = pl.BlockSpec((D_MODEL, tn), lambda j, i: (0, j))
    return pl.pallas_call(
        body, name="gate_up", grid=(nj, S // tm),
        in_specs=[pl.BlockSpec((tm, D_MODEL), lambda j, i: (i, 0)), pl.BlockSpec((2, D_MODEL, w), lambda j, i: (j, 0, 0)),
                  pl.BlockSpec((2, D_MODEL, w), lambda j, i: (j + nj, 0, 0))],
        out_specs=[out, out, out, wout, wout],
        out_shape=[jax.ShapeDtypeStruct((S, FFN), bf16)] * 3 + [jax.ShapeDtypeStruct((D_MODEL, FFN), bf16)] * 2,
        compiler_params=_cp("arbitrary", "arbitrary"),
    )(u2, wgu_g, wgu_g)


def _rms_bwd(dyw, hn, r):
    return r * (dyw - hn * jnp.mean(dyw * hn, axis=-1, keepdims=True))


def _down_loss(act, wdown, h1, tgt, w3):
    S = act.shape[0]
    tm = 256

    def body(a_ref, w_ref, h1_ref, t_ref, w3_ref, dh2_ref, loss_ref, gw3_ref):
        @pl.when(pl.program_id(0) == 0)
        def _():
            loss_ref[...] = jnp.zeros_like(loss_ref)
            gw3_ref[...] = jnp.zeros_like(gw3_ref)

        h2 = h1_ref[...] + _dot(a_ref[...], w_ref[...])
        r = lax.rsqrt(jnp.mean(h2 * h2, axis=-1, keepdims=True) + EPS)
        hn = h2 * r
        w3 = w3_ref[...]
        err = hn * w3 - t_ref[...]
        loss_ref[...] += (0.5 / D_MODEL) * jnp.sum(err * err)
        dy = err * (1.0 / D_MODEL)
        gw3_ref[...] += jnp.sum(dy * hn, axis=0, keepdims=True)
        dh2_ref[...] = _rms_bwd(dy * w3, hn, r)

    row = lambda w: pl.BlockSpec((tm, w), lambda i: (i, 0))
    return pl.pallas_call(
        body, name="down_loss", grid=(S // tm,),
        in_specs=[row(FFN), pl.BlockSpec((FFN, D_MODEL), lambda i: (0, 0)), row(D_MODEL), row(D_MODEL),
                  pl.BlockSpec((1, D_MODEL), lambda i: (0, 0))],
        out_specs=[row(D_MODEL), pl.BlockSpec((1, 128), lambda i: (0, 0)), pl.BlockSpec((1, D_MODEL), lambda i: (0, 0))],
        out_shape=[jax.ShapeDtypeStruct((S, D_MODEL), f32), jax.ShapeDtypeStruct((1, 128), f32),
                   jax.ShapeDtypeStruct((1, D_MODEL), f32)],
        compiler_params=_cp("arbitrary"),
    )(act, wdown, h1, tgt, w3)


def _dact(dh2, wdown, silu, up_dsilu):
    S = dh2.shape[0]
    tm = 256

    def body(d_ref, w_ref, s_ref, u_ref, o_ref):
        da = _dot_nt(d_ref[...].astype(bf16), w_ref[...])
        o_ref[1] = (da * s_ref[...].astype(f32)).astype(bf16)
        o_ref[0] = (da * u_ref[...].astype(f32)).astype(bf16)

    row = lambda w: pl.BlockSpec((tm, w), lambda i: (i, 0))
    return pl.pallas_call(
        body, name="dact", grid=(S // tm,),
        in_specs=[row(D_MODEL), pl.BlockSpec((FFN, D_MODEL), lambda i: (0, 0)), row(FFN), row(FFN)],
        out_specs=pl.BlockSpec((2, tm, FFN), lambda i: (0, i, 0)),
        out_shape=jax.ShapeDtypeStruct((2, S, FFN), bf16),
        compiler_params=_cp("arbitrary"),
    )(dh2, wdown, silu, up_dsilu)


def _dgu(dgu2, wgate, wup, h1, w2, dh2, wout, rider=None):
    S = dgu2.shape[1]
    tm = 256

    def body(d_ref, wg_ref, wu_ref, h1_ref, w2_ref, dh2_ref, wo_ref, dh1_ref, gw2_ref, dmix_ref):
        @pl.when(pl.program_id(0) == 0)
        def _():
            gw2_ref[...] = jnp.zeros_like(gw2_ref)

        du2 = _dot_nt(d_ref[0], wg_ref[...]) + _dot_nt(d_ref[1], wu_ref[...])
        h1 = h1_ref[...]
        r = lax.rsqrt(jnp.mean(h1 * h1, axis=-1, keepdims=True) + EPS)
        hn = h1 * r
        gw2_ref[...] += jnp.sum(du2 * hn, axis=0, keepdims=True)
        dh1 = dh2_ref[...] + _rms_bwd(du2 * w2_ref[...], hn, r)
        dh1_ref[...] = dh1
        dmix_ref[...] = _dot_nt(dh1.astype(bf16), wo_ref[...])

    row = lambda w: pl.BlockSpec((tm, w), lambda i: (i, 0))
    call = dict(in_specs=[pl.BlockSpec((2, tm, FFN), lambda i: (0, i, 0)), pl.BlockSpec((D_MODEL, FFN), lambda i: (0, 0)),
                          pl.BlockSpec((D_MODEL, FFN), lambda i: (0, 0)), row(D_MODEL),
                          pl.BlockSpec((1, D_MODEL), lambda i: (0, 0)), row(D_MODEL),
                          pl.BlockSpec((D_MODEL, D_MODEL), lambda i: (0, 0))],
                out_specs=[row(D_MODEL), pl.BlockSpec((1, D_MODEL), lambda i: (0, 0)), row(D_MODEL)],
                out_shape=[jax.ShapeDtypeStruct((S, D_MODEL), f32), jax.ShapeDtypeStruct((1, D_MODEL), f32),
                           jax.ShapeDtypeStruct((S, D_MODEL), f32)], scratch_shapes=[])
    call, body, more = _ride(call, rider, body, lambda: pl.program_id(0), S // tm, 7, 3, 0)
    return pl.pallas_call(body, name="dgu", grid=(S // tm,), compiler_params=_cp("arbitrary"), **call)(
        dgu2, wgate, wup, h1, w2, dh2, wout, *more)


def _din(dq, dk, dv, dhq, dhf, dhi, dhg, cos_t, sg_t, win, x, w1, dh1):
    S = x.shape[0]
    tm = 256

    def body(dq_ref, dk_ref, dv_ref, dhq_ref, dhf_ref, dhi_ref, dhg_ref, cos_ref, sg_ref, w_ref, x_ref, w1_ref, dh1_ref,
             dp_ref, gx_ref, gw1_ref):
        @pl.when(pl.program_id(0) == 0)
        def _():
            gw1_ref[...] = jnp.zeros_like(gw1_ref)

        cosv, sgv = jnp.tile(cos_ref[...], (1, ATTN_W // 128)), jnp.tile(sg_ref[...], (1, ATTN_W // 128))
        unrope = lambda d: d * cosv - sgv * _swap_halves(d)
        parts = [(unrope(dq_ref[...]) * (HEAD_DIM ** -0.5)).astype(bf16), unrope(dk_ref[...]).astype(bf16),
                 dv_ref[...].astype(bf16), dhq_ref[...], dhf_ref[...], dhi_ref[...], dhg_ref[...]]
        du = jnp.zeros((tm, D_MODEL), f32)
        for j, pj in enumerate(parts):
            dp_ref[:, j * 512:(j + 1) * 512] = pj
            du = du + _dot_nt(pj, w_ref[:, j * 512:(j + 1) * 512])
        xv = x_ref[...]
        r = lax.rsqrt(jnp.mean(xv * xv, axis=-1, keepdims=True) + EPS)
        xn = xv * r
        gw1_ref[...] += jnp.sum(du * xn, axis=0, keepdims=True)
        gx_ref[...] = dh1_ref[...] + _rms_bwd(du * w1_ref[...], xn, r)

    row = lambda w: pl.BlockSpec((tm, w), lambda i: (i, 0))
    vec = pl.BlockSpec((1, D_MODEL), lambda i: (0, 0))
    return pl.pallas_call(
        body, name="din", grid=(S // tm,),
        in_specs=[row(512)] * 7 + [row(128), row(128), pl.BlockSpec((D_MODEL, IN_W), lambda i: (0, 0)), row(D_MODEL), vec,
                                   row(D_MODEL)],
        out_specs=[row(IN_W), row(D_MODEL), vec],
        out_shape=[jax.ShapeDtypeStruct((S, IN_W), bf16), jax.ShapeDtypeStruct((S, D_MODEL), f32),
                   jax.ShapeDtypeStruct((1, D_MODEL), f32)],
        compiler_params=_cp("arbitrary"),
    )(dq, dk, dv, dhq, dhf, dhi, dhg, cos_t, sg_t, win, x, w1, dh1)


def _gw(a, bs, tn, name, ts=2048):
    S, M = a.shape
    N = bs[0].shape[1]
    k = len(bs)

    def body(a_ref, *refs):
        @pl.when(pl.program_id(1) == 0)
        def _():
            for o_ref in refs[k:]:
                o_ref[...] = jnp.zeros_like(o_ref)

        at = a_ref[...].astype(bf16)
        for b_ref, o_ref in zip(refs[:k], refs[k:]):
            o_ref[...] += _dot_tn(at, b_ref[...].astype(bf16))

    return pl.pallas_call(
        body, name=name, grid=(N // tn, S // ts),
        in_specs=[pl.BlockSpec((ts, M), lambda j, s: (s, 0))] + [pl.BlockSpec((ts, tn), lambda j, s: (s, j))] * k,
        out_specs=[pl.BlockSpec((M, tn), lambda j, s: (0, j))] * k, out_shape=[jax.ShapeDtypeStruct((M, N), f32)] * k,
        compiler_params=_cp("arbitrary", "arbitrary"),
    )(a, *bs)


def _gw_by_owner(a, b3, w, name, ts):
    S, M = a.shape
    G, _, Ng = b3.shape
    tn = 2 * w
    per_group = Ng // tn
    n_s = S // ts

    def body(a_ref, b_ref, o_ref, acc):
        s = pl.program_id(1)

        @pl.when(s == 0)
        def _():
            acc[...] = jnp.zeros_like(acc)

        acc[...] += _dot_tn(a_ref[...].astype(bf16), b_ref[0].astype(bf16))

        @pl.when(s == n_s - 1)
        def _():
            o_ref[0] = acc[:, 0:w]
            o_ref[1] = acc[:, w:tn]

    return pl.pallas_call(
        body, name=name, grid=(G * per_group, n_s),
        in_specs=[pl.BlockSpec((ts, M), lambda j, s: (s, 0)),
                  pl.BlockSpec((1, ts, tn), lambda j, s: (j // per_group, s, j % per_group))],
        out_specs=pl.BlockSpec((2, M, w), lambda j, s: (j, 0, 0)), out_shape=jax.ShapeDtypeStruct((G * Ng // w, M, w), f32),
        scratch_shapes=[pltpu.VMEM((M, tn), f32)], compiler_params=_cp("arbitrary", "arbitrary"),
    )(a, b3)


MESH = pl.DeviceIdType.MESH
ANY = pl.BlockSpec(memory_space=pl.ANY)
VMEM_SPEC = pl.BlockSpec(memory_space=pltpu.VMEM)


def _pos():
    return lax.axis_index("x"), lax.axis_index("y"), lax.axis_index("c")


def _flip(v, bit):
    return 1 - v if bit else v


def _gather_rider(shards):
    n = len(shards)

    def parts(outs, scratch):
        send_sems, recv_sems, local_sems = scratch[n:]
        x, y, c = _pos()
        chips = [(1 - x, y), (x, 1 - y), (1 - x, 1 - y)]

        def copy(a, k, block, to, src=None):
            dst = outs[a].at[4 * block[0] + 2 * block[1] + block[2]]
            return pltpu.make_async_remote_copy(src_ref=dst if src is None else src, dst_ref=dst, send_sem=send_sems.at[a, k],
                                                recv_sem=recv_sems.at[a, k], device_id=to, device_id_type=MESH)

        bufs = scratch[:n]
        me, sibling = (x, y, c), (x, y, 1 - c)
        own = lambda a: pltpu.make_async_copy(bufs[a], outs[a].at[4 * x + 2 * y + c], local_sems.at[a])
        sent = lambda a: [copy(a, 0, me, sibling, src=bufs[a])] + [copy(a, 1 + j, me, (*chip, c), src=bufs[a])
                                                                   for j, chip in enumerate(chips)]
        passed = lambda a: [copy(a, 4 + j, (*chip, c), sibling) for j, chip in enumerate(chips)]
        landed = lambda a: [copy(a, 1 + j, (*chip, c), me) for j, chip in enumerate(chips)]
        from_sibling = lambda a: [copy(a, 0, sibling, me)] + [copy(a, 4 + j, (*chip, 1 - c), me) for j, chip in enumerate(chips)]
        return bufs, local_sems, own, sent, passed, landed, from_sibling

    def first(ins, outs, scratch):
        bufs, local_sems, own, sent, _, _, _ = parts(outs, scratch)
        loads = [pltpu.make_async_copy(ins[a], bufs[a], local_sems.at[a]) for a in range(n)]
        for ld in loads:
            ld.start()
        for a in range(n):
            loads[a].wait()
            own(a).start()
            for cp in sent(a):
                cp.start()

    def middle(ins, outs, scratch):
        _, _, _, _, passed, landed, _ = parts(outs, scratch)
        for a in range(n):
            for got, on in zip(landed(a), passed(a)):
                got.wait_recv()
                on.start()

    def last(ins, outs, scratch):
        _, _, own, sent, passed, _, from_sibling = parts(outs, scratch)
        for a in range(n):
            for cp in from_sibling(a):
                cp.wait_recv()
        for a in range(n):
            for cp in sent(a) + passed(a):
                cp.wait_send()
            own(a).wait()

    return _Rider(shards, [jax.ShapeDtypeStruct((N_DEV,) + s.shape, s.dtype) for s in shards],
                  [pltpu.VMEM(s.shape, s.dtype) for s in shards]
                  + [pltpu.SemaphoreType.DMA((n, 7)), pltpu.SemaphoreType.DMA((n, 7)), pltpu.SemaphoreType.DMA((n,))],
                  first, last, middle)


def _sibling_rider(grads):
    n = len(grads)

    def copies(g, got, scratch):
        send_sems, recv_sems = scratch
        x, y, c = _pos()
        return [pltpu.make_async_remote_copy(src_ref=g[a].at[2 * q + (1 - c)], dst_ref=got[a].at[q], send_sem=send_sems.at[a, q],
                                             recv_sem=recv_sems.at[a, q], device_id=(x, y, 1 - c), device_id_type=MESH)
                for a in range(n) for q in range(4)]

    def first(g, got, scratch):
        for cp in copies(g, got, scratch):
            cp.start()

    def last(g, got, scratch):
        for cp in copies(g, got, scratch):
            cp.wait()

    return _Rider(grads, [jax.ShapeDtypeStruct((4,) + g.shape[1:], g.dtype) for g in grads],
                  [pltpu.SemaphoreType.DMA((n, 4))] * 2, first, last)


def _chips_rider(sums):
    n = len(sums)

    def copies(s, out, scratch):
        send_sems, recv_sems = scratch
        x, y, c = _pos()
        cps = []
        for a in range(n):
            for f in (1, 2, 3):
                peer = (_flip(x, f >> 1), _flip(y, f & 1), c)
                cps.append(pltpu.make_async_remote_copy(
                    src_ref=s[a].at[2 * peer[0] + peer[1]], dst_ref=out[a].at[f - 1], send_sem=send_sems.at[a, f - 1],
                    recv_sem=recv_sems.at[a, f - 1], device_id=peer, device_id_type=MESH))
        return cps

    def first(s, out, scratch):
        for cp in copies(s, out, scratch):
            cp.start()

    def last(s, out, scratch):
        for cp in copies(s, out, scratch):
            cp.wait()

    return _Rider(sums, [jax.ShapeDtypeStruct((3,) + s.shape[1:], s.dtype) for s in sums],
                  [pltpu.SemaphoreType.DMA((n, 3))] * 2, first, last)


def _both(a, b):
    na = (len(a.ins), len(a.out_shapes), len(a.scratch))

    def split(fa, fb):
        def f(ins, outs, scratch):
            fa(ins[:na[0]], outs[:na[1]], scratch[:na[2]])
            fb(ins[na[0]:], outs[na[1]:], scratch[na[2]:])
        return f

    return _Rider(a.ins + b.ins, a.out_shapes + b.out_shapes, a.scratch + b.scratch, split(a.first, b.first), split(a.last, b.last))


def _alone(rider, name):
    ri, ro = len(rider.ins), len(rider.out_shapes)

    def body(*refs):
        theirs = (refs[:ri], refs[ri:ri + ro], refs[ri + ro:])
        rider.first(*theirs)
        if rider.middle is not None:
            rider.middle(*theirs)
        rider.last(*theirs)

    return pl.pallas_call(body, name=name, in_specs=[ANY] * ri, out_specs=[ANY] * ro, out_shape=rider.out_shapes,
                          scratch_shapes=rider.scratch)(*rider.ins)


def _gather_small(g_w1, g_w2, g_w3, g_lb, g_wn, loss):
    def body(w1_ref, w2_ref, w3_ref, lb_ref, wn_ref, loss_ref, out_ref, pk, send_sems, recv_sems):
        x, y, c = _pos()
        me = 4 * x + 2 * y + c
        pk[...] = jnp.zeros_like(pk)
        pk[0:1, :] = w1_ref[...]
        pk[1:2, :] = w2_ref[...]
        pk[2:3, :] = w3_ref[...]
        pk[3:4, 0:HGRN_W] = lb_ref[...]
        pk[3:4, HGRN_W:2 * HGRN_W] = wn_ref[...]
        pk[4:5, 0:128] = loss_ref[...]
        out_ref[me] = pk[...]
        sends, recvs = [], []
        for k in range(1, N_DEV):
            peer = (_flip(x, k >> 2), _flip(y, (k >> 1) & 1), _flip(c, k & 1))
            cp = pltpu.make_async_remote_copy(src_ref=pk, dst_ref=out_ref.at[me], send_sem=send_sems.at[k - 1],
                                              recv_sem=recv_sems.at[k - 1], device_id=peer, device_id_type=MESH)
            cp.start()
            sends.append(cp)
            recvs.append(pltpu.make_async_remote_copy(src_ref=pk, dst_ref=out_ref.at[4 * peer[0] + 2 * peer[1] + peer[2]],
                                                      send_sem=send_sems.at[k - 1], recv_sem=recv_sems.at[k - 1], device_id=peer,
                                                      device_id_type=MESH))
        for cp in recvs:
            cp.wait_recv()
        for cp in sends:
            cp.wait_send()

    return pl.pallas_call(
        body, name="gather_small", in_specs=[VMEM_SPEC] * 6, out_specs=VMEM_SPEC,
        out_shape=jax.ShapeDtypeStruct((N_DEV, 8, D_MODEL), f32),
        scratch_shapes=[pltpu.VMEM((8, D_MODEL), f32), pltpu.SemaphoreType.DMA((N_DEV - 1,)), pltpu.SemaphoreType.DMA((N_DEV - 1,))],
    )(g_w1, g_w2, g_w3, g_lb, g_wn, loss)


def _row_tile(r):
    return max(t for t in range(8, 257, 8) if r % t == 0)


def _add_sibling(core, g, got, name):
    _, r, c = got.shape
    tr = _row_tile(r)

    def body(core_ref, a_ref, b_ref, o_ref):
        o_ref[...] = (a_ref[...] + b_ref[...]).astype(bf16)

    blk = pl.BlockSpec((1, tr, c), lambda q, i, core_ref: (q, i, 0))
    return pl.pallas_call(
        body, name=name, out_shape=jax.ShapeDtypeStruct(got.shape, bf16),
        grid_spec=pltpu.PrefetchScalarGridSpec(
            num_scalar_prefetch=1, grid=(4, r // tr),
            in_specs=[pl.BlockSpec((1, tr, c), lambda q, i, core_ref: (2 * q + core_ref[0], i, 0)), blk], out_specs=blk),
        compiler_params=_cp("arbitrary", "arbitrary"))(core, g, got)


def _adamw(w, g, m, v):
    m = ADAM_B1 * m + (1.0 - ADAM_B1) * g
    v = ADAM_B2 * v + (1.0 - ADAM_B2) * (g * g)
    m_hat = m / (1.0 - ADAM_B1 ** ADAM_STEP)
    v_hat = v / (1.0 - ADAM_B2 ** ADAM_STEP)
    return -ADAM_LR * (m_hat / (jnp.sqrt(v_hat) + ADAM_EPS) + ADAM_WD * w), m, v


def _adam_shard(where, g, got, pieces, w, m, v, name):
    r, c = w.shape
    tr = _row_tile(r)

    def body(where_ref, g_ref, got_ref, p_ref, w_ref, m_ref, v_ref, g_out, d_out, m_out, v_out):
        gsum = g_ref[0] + got_ref[0]
        for f in range(3):
            gsum = gsum + p_ref[f].astype(f32)
        g_out[...] = gsum
        d_out[...], m_out[...], v_out[...] = _adamw(w_ref[...], gsum, m_ref[...], v_ref[...])

    blk = pl.BlockSpec((tr, c), lambda i, where_ref: (i, 0))
    return pl.pallas_call(
        body, name=name, out_shape=[jax.ShapeDtypeStruct((r, c), f32)] * 4,
        grid_spec=pltpu.PrefetchScalarGridSpec(
            num_scalar_prefetch=1, grid=(r // tr,),
            in_specs=[pl.BlockSpec((1, tr, c), lambda i, where_ref: (where_ref[0], i, 0)),
                      pl.BlockSpec((1, tr, c), lambda i, where_ref: (where_ref[1], i, 0)),
                      pl.BlockSpec((3, tr, c), lambda i, where_ref: (0, i, 0)), blk, blk, blk],
            out_specs=[blk] * 4),
        compiler_params=_cp("arbitrary"),
    )(where, g, got, pieces, w, m, v)


def _small_update(gath, params):
    def body(gath_ref, *refs):
        ins, outs = refs[:15], refs[15:]
        gs = gath_ref[0]
        for k in range(1, N_DEV):
            gs = gs + gath_ref[k]
        outs[0][...] = gs[4:5, 0:128]
        l0, l1 = ins[9][0:1, :], ins[9][1:2, :]
        lb = _sigmoid(l0 - l1)
        d0 = gs[3:4, 0:HGRN_W] * lb * (1.0 - lb)
        first_row = lax.broadcasted_iota(jnp.int32, (2, HGRN_W), 0) == 0
        grads = [gs[0:1, :], gs[1:2, :], gs[2:3, :], jnp.where(first_row, d0, -d0), gs[3:4, HGRN_W:2 * HGRN_W]]
        for i, g in enumerate(grads):
            w_ref, m_ref, v_ref = ins[3 * i:3 * i + 3]
            o = outs[1 + 4 * i:5 + 4 * i]
            o[0][...] = g
            o[1][...], o[2][...], o[3][...] = _adamw(w_ref[...], g, m_ref[...], v_ref[...])

    flat = [a for p in params for a in p]
    out_shape = [jax.ShapeDtypeStruct((1, 128), f32)] + [jax.ShapeDtypeStruct(p[0].shape, f32) for p in params for _ in range(4)]
    outs = pl.pallas_call(body, name="small_update", in_specs=[VMEM_SPEC] * 16, out_specs=[VMEM_SPEC] * 21, out_shape=out_shape)(gath, *flat)
    return outs[0], [outs[1 + 4 * i:5 + 4 * i] for i in range(5)]


def kernel(x, norm1_w, w_in, lb_logits, hgrn_norm_w, w_out, norm2_w, w_gate_up, w_down, final_norm_w, loss_target, m_norm1_w, m_w_in, m_lb_logits, m_hgrn_norm_w, m_w_out, m_norm2_w, m_w_gate_up, m_w_down, m_final_norm_w, v_norm1_w, v_w_in, v_lb_logits, v_hgrn_norm_w, v_w_out, v_norm2_w, v_w_gate_up, v_w_down, v_final_norm_w):
    row = lambda a: a.reshape(1, D_MODEL)
    ix, iy, ic = lax.axis_index("x"), lax.axis_index("y"), lax.axis_index("c")
    core = jnp.stack([ic]).astype(jnp.int32)
    where = jnp.stack([4 * ix + 2 * iy + ic, 2 * ix + iy]).astype(jnp.int32)
    xs, tgt, w3 = x[0], loss_target[0], row(final_norm_w)
    S = xs.shape[0]

    cos_t, sg_t, win_g = _rope_tables(S, _gather_rider([w_in[0].astype(bf16)]))
    u, qkv, hp, win = _in_proj(xs, norm1_w, win_g, cos_t, sg_t)
    ya, lse, wout_g, wgu_g, wdown_g = _attn_fwd(qkv, _gather_rider([w_out[0].astype(bf16), w_gate_up[0].astype(bf16),
                                                                     w_down[0].astype(bf16)]))
    wout = wout_g.reshape(D_MODEL, D_MODEL)
    wdown = wdown_g.reshape(FFN, D_MODEL)
    yb, o_sav, states = _hgrn_fwd(hp, lb_logits, hgrn_norm_w)
    h1, u2, mixed = _out_proj(xs, ya, yb, wout, norm2_w)
    silu, up_dsilu, act, wgate, wup = _gate_up(u2, wgu_g)
    dh2, loss_p, g_w3 = _down_loss(act, wdown, h1, tgt, w3)

    (g_wdown,) = _gw(act, [dh2], 512, "gw_down")
    dgu2 = _dact(dh2, wdown, silu, up_dsilu)
    early = [_gw_by_owner(u2, dgu2, 2 * FFN // N_DEV, "gw_gate_up", 2048), g_wdown.reshape(N_DEV, FFN // N_DEV, D_MODEL)]
    dh1, g_w2, dmix, *got_early = _dgu(dgu2, wgate, wup, h1, norm2_w, dh2, wout, _sibling_rider(early))
    sums_early = [_add_sibling(core, g, o, f"add_sibling_{i}") for i, (g, o) in enumerate(zip(early, got_early))]
    (g_wout,) = _gw(mixed, [dh1], 1024, "gw_out")
    mid = [g_wout.reshape(N_DEV, D_MODEL // N_DEV, D_MODEL)]
    dhq, dhf, dhi, dhg, g_wn, g_lb, *rode = _hgrn_bwd(hp, lb_logits, hgrn_norm_w, o_sav, states, dmix,
                                                      _both(_chips_rider(sums_early), _sibling_rider(mid)))
    pieces_early, got_mid = rode[:2], rode[2:]
    sums_mid = [_add_sibling(core, mid[0], got_mid[0], "add_sibling_2")]
    dq, dk, dv, *pieces_mid = _attn_bwd(qkv, ya, lse, dmix, _chips_rider(sums_mid))
    dproj, gx, g_w1 = _din(dq, dk, dv, dhq, dhf, dhi, dhg, cos_t, sg_t, win, xs, norm1_w, dh1)
    late = [_gw_by_owner(u, dproj[None], IN_W // N_DEV, "gw_in", 2048)]
    got_late = _alone(_sibling_rider(late), "reduce_sibling")
    sums_late = [_add_sibling(core, late[0], got_late[0], "add_sibling_3")]
    pieces_late = _alone(_chips_rider(sums_late), "reduce_chips")

    grads = [late[0], mid[0], early[0], early[1]]
    got = [got_late[0], got_mid[0], got_early[0], got_early[1]]
    pieces = [pieces_late[0], pieces_mid[0], pieces_early[0], pieces_early[1]]
    shards = [w_in[0], w_out[0], w_gate_up[0], w_down[0]]
    moms = [(m_w_in[0], v_w_in[0]), (m_w_out[0], v_w_out[0]), (m_w_gate_up[0], v_w_gate_up[0]), (m_w_down[0], v_w_down[0])]
    big = [_adam_shard(where, g, o, p, w, m, v, f"adam_{i}")
           for i, (g, o, p, w, (m, v)) in enumerate(zip(grads, got, pieces, shards, moms))]
    big = [[a[None] for a in four] for four in big]

    gath = _gather_small(g_w1, g_w2, g_w3, g_lb, g_wn, loss_p)
    params = [(norm1_w, m_norm1_w, v_norm1_w), (norm2_w, m_norm2_w, v_norm2_w),
              (row(final_norm_w), row(m_final_norm_w), row(v_final_norm_w)),
              (lb_logits, m_lb_logits, v_lb_logits), (hgrn_norm_w, m_hgrn_norm_w, v_hgrn_norm_w)]
    loss, (s_w1, s_w2, s_w3, s_lb, s_wn) = _small_update(gath, params)
    s_w3 = [a.reshape(D_MODEL) for a in s_w3]
    per_w = [s_w1, big[0], s_lb, s_wn, big[1], s_w2, big[2], big[3], s_w3]
    return (loss[0, 0], gx[None], *[p[0] for p in per_w], *[p[1] for p in per_w], *[p[2] for p in per_w], *[p[3] for p in per_w])
```

```python
import jax
import jax.numpy as jnp
from jax import lax
from jax.experimental import pallas as pl
from jax.experimental.pallas import tpu as pltpu

f32, bf16 = jnp.float32, jnp.bfloat16

D_MODEL = 1024
ATTN_W = 512
HEAD_DIM = 64
ATTN_BLK = 128
DILATIONS = (1, 4, 16)
HGRN_W = 512
HGRN_HD = 128
CHUNK = 64
IN_W = 3 * ATTN_W + 4 * HGRN_W
FFN = 2816
EPS = 1e-6
ROPE_THETA = 10000.0
NEG = -1e30
N_DEV = 8
ADAM_LR, ADAM_B1, ADAM_B2, ADAM_EPS, ADAM_WD, ADAM_STEP = 0.001, 0.9, 0.999, 1e-08, 0.01, 10
VMEM_LIMIT = 56 * 1024 * 1024


def _cp(*sem):
    return pltpu.CompilerParams(dimension_semantics=sem, vmem_limit_bytes=VMEM_LIMIT)


def _dot(a, b):
    return jnp.dot(a, b, preferred_element_type=f32)


def _dot_nt(a, b):
    return lax.dot_general(a, b, (((1,), (1,)), ((), ())), preferred_element_type=f32)


def _dot_tn(a, b):
    return lax.dot_general(a, b, (((0,), (0,)), ((), ())), preferred_element_type=f32)


def _sigmoid(x):
    return 0.5 * jnp.tanh(0.5 * x) + 0.5


class _Rider:
    def __init__(self, ins, out_shapes, scratch, first, last, middle=None):
        self.ins, self.out_shapes, self.scratch = list(ins), list(out_shapes), list(scratch)
        self.first, self.middle, self.last = first, middle, last


def _ride(call, rider, body, step, n_steps, n_in, n_out, n_scratch):
    if rider is None:
        return call, body, []
    ri, ro = len(rider.ins), len(rider.out_shapes)
    any_spec = pl.BlockSpec(memory_space=pl.ANY)
    call = dict(call, in_specs=call["in_specs"] + [any_spec] * ri, out_specs=call["out_specs"] + [any_spec] * ro,
                out_shape=call["out_shape"] + rider.out_shapes, scratch_shapes=call["scratch_shapes"] + rider.scratch)

    def riding(*refs):
        a = n_in + ri
        b = a + n_out + ro
        mine = refs[:n_in] + refs[a:a + n_out] + refs[b:b + n_scratch]
        theirs = (refs[n_in:a], refs[a + n_out:b], refs[b + n_scratch:])
        t = step()

        @pl.when(t == 0)
        def _():
            rider.first(*theirs)

        body(*mine)
        if rider.middle is not None:
            @pl.when(t == n_steps // 2)
            def _():
                rider.middle(*theirs)

        @pl.when(t == n_steps - 1)
        def _():
            rider.last(*theirs)

    return call, riding, rider.ins


def _rope_tables(S, rider=None):
    half = HEAD_DIM // 2
    tm = 256
    inv_freq = jnp.tile(ROPE_THETA ** (-jnp.arange(half, dtype=f32) / half), 128 // half).reshape(1, 128)
    sign = jnp.tile(jnp.concatenate([-jnp.ones((half,), f32), jnp.ones((half,), f32)]), 128 // HEAD_DIM).reshape(1, 128)

    def body(inv_ref, sign_ref, cos_ref, sg_ref):
        pos = (lax.broadcasted_iota(jnp.int32, (tm, 128), 0) + pl.program_id(0) * tm).astype(f32)
        ang = pos * inv_ref[...]
        cos_ref[...] = jnp.cos(ang)
        sg_ref[...] = jnp.sin(ang) * sign_ref[...]

    vec = pl.BlockSpec((1, 128), lambda i: (0, 0))
    out = pl.BlockSpec((tm, 128), lambda i: (i, 0))
    call = dict(in_specs=[vec, vec], out_specs=[out, out], out_shape=[jax.ShapeDtypeStruct((S, 128), f32)] * 2, scratch_shapes=[])
    call, body, more = _ride(call, rider, body, lambda: pl.program_id(0), S // tm, 2, 2, 0)
    return pl.pallas_call(body, name="rope_tables", grid=(S // tm,), compiler_params=_cp("arbitrary"), **call)(inv_freq, sign, *more)


def _swap_halves(v):
    n = v.shape[1]
    lane = lax.broadcasted_iota(jnp.int32, v.shape, 1)
    return jnp.where((lane % HEAD_DIM) < HEAD_DIM // 2, pltpu.roll(v, n - HEAD_DIM // 2, 1), pltpu.roll(v, HEAD_DIM // 2, 1))


def _in_proj(x, w1, win_g, cos_t, sg_t):
    S = x.shape[0]
    tm = 256
    w = IN_W // N_DEV

    def body(x_ref, w1_ref, wg_ref, cos_ref, sg_ref, u_ref, qkv_ref, hp_ref, w_ref):
        @pl.when(pl.program_id(0) == 0)
        def _():
            for d in range(N_DEV):
                w_ref[:, w * d:w * (d + 1)] = wg_ref[d]

        xv = x_ref[...]
        r = lax.rsqrt(jnp.mean(xv * xv, axis=-1, keepdims=True) + EPS)
        u = (xv * r * w1_ref[...]).astype(bf16)
        u_ref[...] = u
        cosv, sgv = jnp.tile(cos_ref[...], (1, ATTN_W // 128)), jnp.tile(sg_ref[...], (1, ATTN_W // 128))
        for j in range(3):
            pj = _dot(u, w_ref[:, j * ATTN_W:(j + 1) * ATTN_W])
            if j < 2:
                pj = pj * cosv + _swap_halves(pj) * sgv
            if j == 0:
                pj = pj * (HEAD_DIM ** -0.5)
            qkv_ref[:, j * ATTN_W:(j + 1) * ATTN_W] = pj.astype(bf16)
        for j in range(4):
            lo = 3 * ATTN_W + j * HGRN_W
            hp_ref[:, j * HGRN_W:(j + 1) * HGRN_W] = _dot(u, w_ref[:, lo:lo + HGRN_W])

    return pl.pallas_call(
        body, name="in_proj", grid=(S // tm,),
        in_specs=[pl.BlockSpec((tm, D_MODEL), lambda i: (i, 0)), pl.BlockSpec((1, D_MODEL), lambda i: (0, 0)),
                  pl.BlockSpec((N_DEV, D_MODEL, w), lambda i: (0, 0, 0)),
                  pl.BlockSpec((tm, 128), lambda i: (i, 0)), pl.BlockSpec((tm, 128), lambda i: (i, 0))],
        out_specs=[pl.BlockSpec((tm, D_MODEL), lambda i: (i, 0)), pl.BlockSpec((tm, 3 * ATTN_W), lambda i: (i, 0)),
                   pl.BlockSpec((tm, 4 * HGRN_W), lambda i: (i, 0)), pl.BlockSpec((D_MODEL, IN_W), lambda i: (0, 0))],
        out_shape=[jax.ShapeDtypeStruct((S, D_MODEL), bf16), jax.ShapeDtypeStruct((S, 3 * ATTN_W), bf16),
                   jax.ShapeDtypeStruct((S, 4 * HGRN_W), f32), jax.ShapeDtypeStruct((D_MODEL, IN_W), bf16)],
        compiler_params=_cp("arbitrary"),
    )(x, w1, win_g, cos_t, sg_t)


def _head_masks():
    lane = lax.broadcasted_iota(jnp.int32, (ATTN_BLK, 128), 1)
    even = lane < HEAD_DIM
    return even, (even, jnp.logical_not(even))


def _pair_fwd(q2, k2, v2, bias):
    even, masks = _head_masks()
    outs, lses = [], []
    for e in range(2):
        qm = jnp.where(masks[e], q2, 0.0).astype(bf16)
        s = _dot_nt(qm, k2) + bias
        m = jnp.max(s, axis=-1, keepdims=True)
        pe = jnp.exp(s - m)
        lsum = jnp.sum(pe, axis=-1, keepdims=True)
        outs.append(_dot(pe.astype(bf16), v2) / lsum)
        lses.append(jnp.broadcast_to(m + jnp.log(lsum), (ATTN_BLK, 128)))
    return jnp.where(even, outs[0], outs[1]), jnp.where(even, lses[0], lses[1])


def _merge(y0, l0, y1, l1):
    mx = jnp.maximum(l0, l1)
    a, b = jnp.exp(l0 - mx), jnp.exp(l1 - mx)
    tot = a + b
    return (a * y0 + b * y1) / tot, mx + jnp.log(tot)


def _pair_bwd(q2, k2f, v2, dy2, lse2, delta2, bias):
    _, masks = _head_masks()
    k2 = k2f.astype(bf16)
    klane = lax.broadcasted_iota(jnp.int32, (2 * ATTN_BLK, 128), 1) < HEAD_DIM
    kmasks = (klane, jnp.logical_not(klane))
    dq2 = jnp.zeros((ATTN_BLK, 128), f32)
    pes, dss, qms, dyms = [], [], [], []
    for e in range(2):
        c0 = e * HEAD_DIM
        qm = jnp.where(masks[e], q2, 0.0).astype(bf16)
        km = jnp.where(kmasks[e], k2f, 0.0).astype(bf16)
        dym = jnp.where(masks[e], dy2, 0.0).astype(bf16)
        pe = jnp.exp(_dot_nt(qm, k2) + bias - lse2[:, c0:c0 + 1])
        ds = (pe * (_dot_nt(dym, v2) - delta2[:, c0:c0 + 1])).astype(bf16)
        dq2 = dq2 + _dot(ds, km)
        pes.append(pe.astype(bf16))
        dss.append(ds)
        qms.append(qm)
        dyms.append(dym)
    dv2 = _dot_tn(jnp.concatenate(pes, axis=0), jnp.concatenate(dyms, axis=0))
    dk2 = _dot_tn(jnp.concatenate(dss, axis=0), jnp.concatenate(qms, axis=0))
    return dq2, dk2, dv2


TOK = 2048


def _key_bias():
    qi = lax.broadcasted_iota(jnp.int32, (ATTN_BLK, 2 * ATTN_BLK), 0)
    kj = lax.broadcasted_iota(jnp.int32, (ATTN_BLK, 2 * ATTN_BLK), 1)
    delta = ATTN_BLK + qi - kj
    seen = (delta >= 0) & (delta <= ATTN_BLK)
    return jnp.where(seen, 0.0, NEG), jnp.where(seen & (kj >= ATTN_BLK), 0.0, NEG)


def _attn_fwd(qkv, rider=None):
    S = qkv.shape[0]
    nS = S // TOK

    def body(q_ref, kp_ref, kc_ref, vp_ref, vc_ref, y_ref, l_ref, qs, k2, v2, ay, al):
        n = pl.program_id(1)
        qs[...] = q_ref[...].astype(f32)
        k2[0:TOK] = kp_ref[...].astype(f32)
        k2[TOK:2 * TOK] = kc_ref[...].astype(f32)
        v2[0:TOK] = vp_ref[...].astype(f32)
        v2[TOK:2 * TOK] = vc_ref[...].astype(f32)
        bias_any, bias_first = _key_bias()

        def block(dil, r, b, step, last):
            start = r + pl.multiple_of(step * b, step)
            rows = pl.ds(start, ATTN_BLK, stride=dil) if dil > 1 else pl.ds(start, ATTN_BLK)
            keys = (pl.ds(TOK + start - step, 2 * ATTN_BLK, stride=dil) if dil > 1
                    else pl.ds(TOK + start - step, 2 * ATTN_BLK))
            bias = jnp.where((n == 0) & (b == 0), bias_first, bias_any)
            out, lse = _pair_fwd(qs[rows, :], k2[keys, :].astype(bf16), v2[keys, :].astype(bf16), bias)
            if dil < DILATIONS[-1]:
                out, lse = _merge(ay[rows, :], al[rows, :], out, lse)
            if last:
                y_ref[rows, :] = out
                l_ref[rows, :] = lse
            else:
                ay[rows, :] = out
                al[rows, :] = lse

        for dil in reversed(DILATIONS):
            def loop(i, carry, dil=dil):
                block(dil, i % dil, i // dil, ATTN_BLK * dil, dil == 1)
                return carry
            lax.fori_loop(0, TOK // ATTN_BLK, loop, 0, unroll=True)

    blk = (TOK, 128)
    cur = lambda c: pl.BlockSpec(blk, lambda p, n: (n, 4 * c + p))
    prv = lambda c: pl.BlockSpec(blk, lambda p, n: (jnp.maximum(n - 1, 0), 4 * c + p))
    out = pl.BlockSpec(blk, lambda p, n: (n, p))
    call = dict(in_specs=[cur(0), prv(1), cur(1), prv(2), cur(2)], out_specs=[out, out],
                out_shape=[jax.ShapeDtypeStruct((S, ATTN_W), f32)] * 2,
                scratch_shapes=[pltpu.VMEM(blk, f32), pltpu.VMEM((2 * TOK, 128), f32), pltpu.VMEM((2 * TOK, 128), f32),
                                pltpu.VMEM(blk, f32), pltpu.VMEM(blk, f32)])
    call, body, more = _ride(call, rider, body, lambda: pl.program_id(0) * nS + pl.program_id(1), (ATTN_W // 128) * nS, 5, 2, 5)
    return pl.pallas_call(body, name="attention_fwd", grid=(ATTN_W // 128, nS), compiler_params=_cp("arbitrary", "arbitrary"),
                          **call)(qkv, qkv, qkv, qkv, qkv, *more)


def _attn_bwd(qkv, ya, lse, dmix, rider=None):
    S = qkv.shape[0]
    nS = S // TOK

    def body(q_ref, kp_ref, kc_ref, vp_ref, vc_ref, y_ref, l_ref, dy_ref, dq_ref, dk_ref, dv_ref, qs, k2, v2, dk2, dv2, dqa, dl):
        n = pl.program_id(1)

        @pl.when(n == 0)
        def _():
            dk2[...] = jnp.zeros_like(dk2)
            dv2[...] = jnp.zeros_like(dv2)

        @pl.when(n < nS)
        def _():
            qs[...] = q_ref[...].astype(f32)
            k2[0:TOK] = kp_ref[...].astype(f32)
            k2[TOK:2 * TOK] = kc_ref[...].astype(f32)
            v2[0:TOK] = vp_ref[...].astype(f32)
            v2[TOK:2 * TOK] = vc_ref[...].astype(f32)
            li = lax.broadcasted_iota(jnp.int32, (128, 128), 0)
            lj = lax.broadcasted_iota(jnp.int32, (128, 128), 1)
            seg = jnp.where((li // HEAD_DIM) == (lj // HEAD_DIM), 1.0, 0.0).astype(bf16)
            bias_any, bias_first = _key_bias()

            def delta_rows(t, carry):
                rows = pl.ds(pl.multiple_of(256 * t, 256), 256)
                dyy = dy_ref[rows, :] * y_ref[rows, :]
                hi = dyy.astype(bf16)
                dl[rows, :] = _dot(hi, seg) + _dot((dyy - hi.astype(f32)).astype(bf16), seg)
                return carry

            lax.fori_loop(0, TOK // 256, delta_rows, 0)

            def block(dil, r, b, step, first_pattern, last):
                start = r + pl.multiple_of(step * b, step)
                rows = pl.ds(start, ATTN_BLK, stride=dil) if dil > 1 else pl.ds(start, ATTN_BLK)
                keys = (pl.ds(TOK + start - step, 2 * ATTN_BLK, stride=dil) if dil > 1
                        else pl.ds(TOK + start - step, 2 * ATTN_BLK))
                bias = jnp.where((n == 0) & (b == 0), bias_first, bias_any)
                dq2, dkk, dvv = _pair_bwd(qs[rows, :], k2[keys, :], v2[keys, :].astype(bf16), dy_ref[rows, :],
                                          l_ref[rows, :], dl[rows, :], bias)
                if last:
                    dq_ref[rows, :] = dqa[rows, :] + dq2
                elif first_pattern:
                    dqa[rows, :] = dq2
                else:
                    dqa[rows, :] += dq2
                dk2[keys, :] += dkk
                dv2[keys, :] += dvv

            for dil in reversed(DILATIONS):
                def loop(i, carry, dil=dil):
                    block(dil, i % dil, i // dil, ATTN_BLK * dil, dil == DILATIONS[-1], dil == 1)
                    return carry
                lax.fori_loop(0, TOK // ATTN_BLK, loop, 0, unroll=True)

        dk_ref[...] = dk2[0:TOK]
        dv_ref[...] = dv2[0:TOK]
        dk2[0:TOK] = dk2[TOK:2 * TOK]
        dv2[0:TOK] = dv2[TOK:2 * TOK]
        dk2[TOK:2 * TOK] = jnp.zeros((TOK, 128), f32)
        dv2[TOK:2 * TOK] = jnp.zeros((TOK, 128), f32)

    blk = (TOK, 128)
    cn = lambda n: jnp.minimum(n, nS - 1)
    pn = lambda n: jnp.clip(n - 1, 0, nS - 1)
    cur = lambda c: pl.BlockSpec(blk, lambda p, n: (cn(n), 4 * c + p))
    prv = lambda c: pl.BlockSpec(blk, lambda p, n: (pn(n), 4 * c + p))
    at_n = pl.BlockSpec(blk, lambda p, n: (cn(n), p))
    at_p = pl.BlockSpec(blk, lambda p, n: (pn(n), p))
    big = lambda: pltpu.VMEM((2 * TOK, 128), f32)
    call = dict(in_specs=[cur(0), prv(1), cur(1), prv(2), cur(2), at_n, at_n, at_n], out_specs=[at_n, at_p, at_p],
                out_shape=[jax.ShapeDtypeStruct((S, ATTN_W), f32)] * 3,
                scratch_shapes=[pltpu.VMEM(blk, f32), big(), big(), big(), big(), pltpu.VMEM(blk, f32), pltpu.VMEM(blk, f32)])
    call, body, more = _ride(call, rider, body, lambda: pl.program_id(0) * (nS + 1) + pl.program_id(1),
                             (ATTN_W // 128) * (nS + 1), 8, 3, 7)
    return pl.pallas_call(body, name="attention_bwd", grid=(ATTN_W // 128, nS + 1), compiler_params=_cp("arbitrary", "arbitrary"),
                          **call)(qkv, qkv, qkv, qkv, qkv, ya, lse, dmix, *more)


HG_T = 256
N_HH = HGRN_W // HGRN_HD
HG_SUB = 128
SAFE_RANGE = 75.0


def _row_in_chunk():
    return lax.broadcasted_iota(jnp.int32, (HG_T, HGRN_HD), 0) % CHUNK


def _chunk_cumsum(v, rc):
    k = 1
    while k < CHUNK:
        v = v + jnp.where(rc >= k, pltpu.roll(v, k, 0), 0.0)
        k *= 2
    return v


def _chunk_rcumsum(v, rc):
    k = 1
    while k < CHUNK:
        v = v + jnp.where(rc < CHUNK - k, pltpu.roll(v, HG_T - k, 0), 0.0)
        k *= 2
    return v


def _hgrn_gates(qb, fb, lb):
    sf = _sigmoid(fb)
    f = lb + (1.0 - lb) * sf
    sq = _sigmoid(qb)
    return sf, f, jnp.log(f), 1.0 - f, sq, qb * sq


def _hgrn_prep(qb, fb, lbl2, rc):
    lb = _sigmoid(lbl2[0:1, :] - lbl2[1:2, :])
    sf, f, lf, key, sq, qf = _hgrn_gates(qb, fb, lb)
    b = _chunk_cumsum(lf, rc)
    rem = _chunk_rcumsum(lf, rc) - lf
    return dict(lb=lb, sf=sf, f=f, key=key, sq=sq, qf=qf, b=b, rem=rem, eb=jnp.exp(b), er=jnp.exp(rem))


def _chunk_mask():
    r = lax.broadcasted_iota(jnp.int32, (HG_SUB, HG_SUB), 0)
    c = lax.broadcasted_iota(jnp.int32, (HG_SUB, HG_SUB), 1)
    return ((r // CHUNK) == (c // CHUNK)) & (c <= r)


def _hgrn_fwd(hp, lbl, wn):
    S = hp.shape[0]
    nT = S // HG_T

    def body(qb_ref, fb_ref, ib_ref, gb_ref, lbl_ref, wn_ref, yb_ref, o_ref, st_ref, ST, qt_s, kh_s, dec_s, oi_s):
        @pl.when(pl.program_id(0) == 0)
        def _():
            ST[...] = jnp.zeros_like(ST)

        rc = _row_in_chunk()
        for h in range(N_HH):
            sl = slice(HGRN_HD * h, HGRN_HD * (h + 1))
            p = _hgrn_prep(qb_ref[:, sl], fb_ref[:, sl], lbl_ref[:, sl], rc)
            qf, key, b = p["qf"], p["key"], p["b"]
            qt = qf * p["eb"]
            qt_s[:, sl] = qt.astype(bf16)
            kh_s[:, sl] = (key * p["er"]).astype(bf16)
            dec_s[:, sl] = jnp.exp(b + p["rem"])
            rng = jnp.max(-(b + p["rem"]))

            @pl.when(rng < SAFE_RANGE)
            def _():
                kp = (key * jnp.exp(-b)).astype(bf16)
                cmask = _chunk_mask()
                for j in range(HG_T // HG_SUB):
                    rs = slice(HG_SUB * j, HG_SUB * (j + 1))
                    sc = jnp.where(cmask, _dot_nt(qt[rs].astype(bf16), kp[rs]), 0.0).astype(bf16)
                    oi_s[rs, sl] = _dot(sc, ib_ref[rs, sl].astype(bf16))

            @pl.when(rng >= SAFE_RANGE)
            def _():
                v = ib_ref[:, sl]
                ones = jnp.ones((HGRN_HD, HGRN_HD), bf16)

                def lag(l, o):
                    e = jnp.exp(jnp.where(rc >= l, b - pltpu.roll(b, l, 0), NEG))
                    pr = qf * pltpu.roll(key, l, 0) * e
                    return o + _dot(pr.astype(bf16), ones) * pltpu.roll(v, l, 0)

                oi_s[:, sl] = lax.fori_loop(1, CHUNK, lag, _dot((qf * key).astype(bf16), ones) * v)

        def step(c, carry):
            rows = pl.ds(pl.multiple_of(c * CHUNK, CHUNK), CHUNK)
            row0 = pl.ds(pl.multiple_of(c * CHUNK, CHUNK), 1)
            for h in range(N_HH):
                sl = slice(HGRN_HD * h, HGRN_HD * (h + 1))
                stv = ST[h]
                st_ref[c, sl, :] = stv
                oi_s[rows, sl] += _dot_nt(qt_s[rows, sl], stv.astype(bf16))
                ST[h] = stv * dec_s[row0, sl] + _dot_tn(ib_ref[rows, sl].astype(bf16), kh_s[rows, sl])
            return carry

        lax.fori_loop(0, HG_T // CHUNK, step, 0, unroll=True)

        for h in range(N_HH):
            sl = slice(HGRN_HD * h, HGRN_HD * (h + 1))
            o = oi_s[:, sl]
            o_ref[:, sl] = o
            on = o * lax.rsqrt(jnp.mean(o * o, axis=-1, keepdims=True) + EPS)
            g = gb_ref[:, sl]
            yb_ref[:, sl] = on * wn_ref[:, sl] * (g * _sigmoid(g))

    col = lambda c: pl.BlockSpec((HG_T, HGRN_W), lambda i: (i, c))
    tile = pl.BlockSpec((HG_T, HGRN_W), lambda i: (i, 0))
    whole = lambda a: pl.BlockSpec(a.shape, lambda i: (0, 0))
    return pl.pallas_call(
        body, name="hgrn_fwd", grid=(nT,),
        in_specs=[col(0), col(1), col(2), col(3), whole(lbl), whole(wn)],
        out_specs=[tile, tile, pl.BlockSpec((HG_T // CHUNK, HGRN_W, HGRN_HD), lambda i: (i, 0, 0))],
        out_shape=[jax.ShapeDtypeStruct((S, HGRN_W), f32), jax.ShapeDtypeStruct((S, HGRN_W), f32),
                   jax.ShapeDtypeStruct((S // CHUNK, HGRN_W, HGRN_HD), f32)],
        scratch_shapes=[pltpu.VMEM((N_HH, HGRN_HD, HGRN_HD), f32), pltpu.VMEM((HG_T, HGRN_W), bf16),
                        pltpu.VMEM((HG_T, HGRN_W), bf16), pltpu.VMEM((HG_T, HGRN_W), f32), pltpu.VMEM((HG_T, HGRN_W), f32)],
        compiler_params=_cp("arbitrary"),
    )(hp, hp, hp, hp, lbl, wn)


def _hgrn_bwd(hp, lbl, wn, o_sav, states, dmix, rider=None):
    S = hp.shape[0]
    nT = S // HG_T

    def body(qb_ref, fb_ref, ib_ref, gb_ref, lbl_ref, wn_ref, o_ref, st_ref, dy_ref,
             dq_ref, df_ref, di_ref, dg_ref, gwn_ref, glb_ref,
             DST, qt_s, kh_s, dec_s, do_s, dqt_s, dkh_s, dbl_s, dvi_s, dqi_s, dki_s, dbi_s):
        @pl.when(pl.program_id(0) == 0)
        def _():
            DST[...] = jnp.zeros_like(DST)
            gwn_ref[...] = jnp.zeros_like(gwn_ref)
            glb_ref[...] = jnp.zeros_like(glb_ref)

        rc = _row_in_chunk()
        preps = []
        for h in range(N_HH):
            sl = slice(HGRN_HD * h, HGRN_HD * (h + 1))
            p = _hgrn_prep(qb_ref[:, sl], fb_ref[:, sl], lbl_ref[:, sl], rc)
            preps.append(p)
            qf, key, b = p["qf"], p["key"], p["b"]
            v = ib_ref[:, sl]
            o = o_ref[:, sl]
            rinv = lax.rsqrt(jnp.mean(o * o, axis=-1, keepdims=True) + EPS)
            on = o * rinv
            g = gb_ref[:, sl]
            sgm = _sigmoid(g)
            silu_g = g * sgm
            dy = dy_ref[:, sl]
            wn_v = wn_ref[:, sl]
            gwn_ref[:, sl] += jnp.sum(dy * on * silu_g, axis=0, keepdims=True)
            dg_ref[:, sl] = (dy * on * wn_v * (sgm * (1.0 + g * (1.0 - sgm)))).astype(bf16)
            t1 = dy * wn_v * silu_g
            do = rinv * (t1 - on * jnp.mean(t1 * on, axis=-1, keepdims=True))
            do_s[:, sl] = do.astype(bf16)
            qt = qf * p["eb"]
            qt_s[:, sl] = qt.astype(bf16)
            kh_s[:, sl] = (key * p["er"]).astype(bf16)
            dec_s[:, sl] = jnp.exp(b + p["rem"])
            rng = jnp.max(-(b + p["rem"]))

            @pl.when(rng < SAFE_RANGE)
            def _():
                einv = jnp.exp(-b)
                kp = (key * einv).astype(bf16)
                cmask = _chunk_mask()
                for j in range(HG_T // HG_SUB):
                    rs = slice(HG_SUB * j, HG_SUB * (j + 1))
                    qtb, dob, vb = qt[rs].astype(bf16), do[rs].astype(bf16), v[rs].astype(bf16)
                    sc = jnp.where(cmask, _dot_nt(qtb, kp[rs]), 0.0).astype(bf16)
                    dsc = jnp.where(cmask, _dot_nt(dob, vb), 0.0).astype(bf16)
                    dqp = _dot(dsc, kp[rs])
                    dkp = _dot_tn(dsc, qtb)
                    dvi_s[rs, sl] = _dot_tn(sc, dob)
                    dqi_s[rs, sl] = dqp * p["eb"][rs]
                    dki_s[rs, sl] = dkp * einv[rs]
                    dbi_s[rs, sl] = dqp * qtb.astype(f32) - dkp * kp[rs].astype(f32)

            @pl.when(rng >= SAFE_RANGE)
            def _():
                ones = jnp.ones((HGRN_HD, HGRN_HD), bf16)

                def lag(l, carry):
                    dqf, dkey, db, dv = carry
                    e = jnp.exp(jnp.where(rc >= l, b - pltpu.roll(b, l, 0), NEG))
                    ks, vs, qe = pltpu.roll(key, l, 0), pltpu.roll(v, l, 0), qf * e
                    pr = qe * ks
                    rl = _dot(pr.astype(bf16), ones)
                    drl = jnp.where(rc >= l, _dot((do * vs).astype(bf16), ones), 0.0)
                    gl = drl * pr
                    back = HG_T - l
                    return (dqf + drl * ks * e, dkey + pltpu.roll(drl * qe, back, 0), db + gl - pltpu.roll(gl, back, 0),
                            dv + pltpu.roll(rl * do, back, 0))

                rl0 = _dot((qf * key).astype(bf16), ones)
                drl0 = _dot((do * v).astype(bf16), ones)
                dqf, dkey, db, dv = lax.fori_loop(1, CHUNK, lag, (drl0 * key, drl0 * qf, jnp.zeros((HG_T, HGRN_HD), f32), rl0 * do))
                dvi_s[:, sl] = dv
                dqi_s[:, sl] = dqf
                dki_s[:, sl] = dkey
                dbi_s[:, sl] = db

        def step(k, carry):
            c = HG_T // CHUNK - 1 - k
            rows = pl.ds(pl.multiple_of(c * CHUNK, CHUNK), CHUNK)
            row0 = pl.ds(pl.multiple_of(c * CHUNK, CHUNK), 1)
            for h in range(N_HH):
                sl = slice(HGRN_HD * h, HGRN_HD * (h + 1))
                stp = st_ref[c, sl, :]
                dst = DST[h]
                dstb = dst.astype(bf16)
                dob = do_s[rows, sl]
                khb = kh_s[rows, sl]
                dec = dec_s[row0, sl]
                dqt_s[rows, sl] = _dot(dob, stp.astype(bf16))
                dkh = _dot(ib_ref[rows, sl].astype(bf16), dstb)
                dkh_s[rows, sl] = dkh
                dvi_s[rows, sl] += _dot_nt(khb, dstb)
                dbl = jnp.sum(dst * stp, axis=0, keepdims=True) * dec + jnp.sum(dkh * khb.astype(f32), axis=0, keepdims=True)
                dbl_s[rows, sl] = jnp.broadcast_to(dbl, (CHUNK, HGRN_HD))
                DST[h] = dst * dec + _dot_tn(dob, qt_s[rows, sl])
            return carry

        lax.fori_loop(0, HG_T // CHUNK, step, 0, unroll=True)

        for h in range(N_HH):
            sl = slice(HGRN_HD * h, HGRN_HD * (h + 1))
            qb = qb_ref[:, sl]
            p = preps[h]
            sf, sq, lb = p["sf"], p["sq"], p["lb"]
            dqt, dkh = dqt_s[:, sl], dkh_s[:, sl]
            dqf = dqt * p["eb"] + dqi_s[:, sl]
            dkey = dkh * p["er"] + dki_s[:, sl]
            db = dqt * (p["qf"] * p["eb"]) - dkh * (p["key"] * p["er"]) + jnp.where(rc == CHUNK - 1, dbl_s[:, sl], 0.0) + dbi_s[:, sl]
            df = _chunk_rcumsum(db, rc) / p["f"] - dkey
            df_ref[:, sl] = (df * (1.0 - lb) * sf * (1.0 - sf)).astype(bf16)
            glb_ref[:, sl] += jnp.sum(df * (1.0 - sf), axis=0, keepdims=True)
            dq_ref[:, sl] = (dqf * (sq * (1.0 + qb * (1.0 - sq)))).astype(bf16)
            di_ref[:, sl] = dvi_s[:, sl].astype(bf16)

    rev = lambda i: nT - 1 - i
    col = lambda c: pl.BlockSpec((HG_T, HGRN_W), lambda i: (rev(i), c))
    tile = pl.BlockSpec((HG_T, HGRN_W), lambda i: (rev(i), 0))
    whole = lambda a: pl.BlockSpec(a.shape, lambda i: (0, 0))
    vec = pl.BlockSpec((1, HGRN_W), lambda i: (0, 0))
    tb = lambda: pltpu.VMEM((HG_T, HGRN_W), bf16)
    tf = lambda: pltpu.VMEM((HG_T, HGRN_W), f32)
    call = dict(in_specs=[col(0), col(1), col(2), col(3), whole(lbl), whole(wn), tile,
                          pl.BlockSpec((HG_T // CHUNK, HGRN_W, HGRN_HD), lambda i: (rev(i), 0, 0)),
                          pl.BlockSpec((HG_T, HGRN_W), lambda i: (rev(i), 1))],
                out_specs=[tile, tile, tile, tile, vec, vec],
                out_shape=[jax.ShapeDtypeStruct((S, HGRN_W), bf16)] * 4 + [jax.ShapeDtypeStruct((1, HGRN_W), f32)] * 2,
                scratch_shapes=[pltpu.VMEM((N_HH, HGRN_HD, HGRN_HD), f32), tb(), tb(), tf(), tb(), tf(), tf(), tf(), tf(), tf(),
                                tf(), tf()])
    call, body, more = _ride(call, rider, body, lambda: pl.program_id(0), nT, 9, 6, 12)
    return pl.pallas_call(body, name="hgrn_bwd", grid=(nT,), compiler_params=_cp("arbitrary"), **call)(
        hp, hp, hp, hp, lbl, wn, o_sav, states, dmix, *more)


def _out_proj(x, ya, yb, wout, w2):
    S = x.shape[0]
    tm = 512

    def body(x_ref, ya_ref, yb_ref, w_ref, w2_ref, h1_ref, u2_ref, mix_ref):
        mixed = jnp.concatenate([ya_ref[...], yb_ref[...]], axis=1).astype(bf16)
        mix_ref[...] = mixed
        h1 = x_ref[...] + _dot(mixed, w_ref[...])
        h1_ref[...] = h1
        r = lax.rsqrt(jnp.mean(h1 * h1, axis=-1, keepdims=True) + EPS)
        u2_ref[...] = (h1 * r * w2_ref[...]).astype(bf16)

    row = lambda w: pl.BlockSpec((tm, w), lambda i: (i, 0))
    return pl.pallas_call(
        body, name="out_proj", grid=(S // tm,),
        in_specs=[row(D_MODEL), row(ATTN_W), row(HGRN_W), pl.BlockSpec((D_MODEL, D_MODEL), lambda i: (0, 0)),
                  pl.BlockSpec((1, D_MODEL), lambda i: (0, 0))],
        out_specs=[row(D_MODEL), row(D_MODEL), row(D_MODEL)],
        out_shape=[jax.ShapeDtypeStruct((S, D_MODEL), f32), jax.ShapeDtypeStruct((S, D_MODEL), bf16),
                   jax.ShapeDtypeStruct((S, D_MODEL), bf16)],
        compiler_params=_cp("arbitrary"),
    )(x, ya, yb, wout, w2)


def _gate_up(u2, wgu_g):
    S = u2.shape[0]
    w = 2 * FFN // N_DEV
    tm, tn = 512, 2 * w
    nj = FFN // tn

    def body(u_ref, wgg_ref, wug_ref, g_ref, up_ref, a_ref, wg_ref, wu_ref):
        @pl.when(pl.program_id(1) == 0)
        def _():
            for k in range(2):
                wg_ref[:, w * k:w * (k + 1)] = wgg_ref[k]
                wu_ref[:, w * k:w * (k + 1)] = wug_ref[k]

        u = u_ref[...]
        g = _dot(u, wg_ref[...])
        up = _dot(u, wu_ref[...])
        sg = _sigmoid(g)
        silu = g * sg
        g_ref[...] = silu.astype(bf16)
        up_ref[...] = (up * (sg + silu * (1.0 - sg))).astype(bf16)
        a_ref[...] = (silu * up).astype(bf16)

    out = pl.BlockSpec((tm, tn), lambda j, i: (i, j))
    wout = pl.BlockSpec((D_MODEL, tn), lambda j, i: (0, j))
    return pl.pallas_call(
        body, name="gate_up", grid=(nj, S // tm),
        in_specs=[pl.BlockSpec((tm, D_MODEL), lambda j, i: (i, 0)), pl.BlockSpec((2, D_MODEL, w), lambda j, i: (j, 0, 0)),
                  pl.BlockSpec((2, D_MODEL, w), lambda j, i: (j + nj, 0, 0))],
        out_specs=[out, out, out, wout, wout],
        out_shape=[jax.ShapeDtypeStruct((S, FFN), bf16)] * 3 + [jax.ShapeDtypeStruct((D_MODEL, FFN), bf16)] * 2,
        compiler_params=_cp("arbitrary", "arbitrary"),
    )(u2, wgu_g, wgu_g)


def _rms_bwd(dyw, hn, r):
    return r * (dyw - hn * jnp.mean(dyw * hn, axis=-1, keepdims=True))


def _down_loss(act, wdown, h1, tgt, w3):
    S = act.shape[0]
    tm = 256

    def body(a_ref, w_ref, h1_ref, t_ref, w3_ref, dh2_ref, loss_ref, gw3_ref):
        @pl.when(pl.program_id(0) == 0)
        def _():
            loss_ref[...] = jnp.zeros_like(loss_ref)
            gw3_ref[...] = jnp.zeros_like(gw3_ref)

        h2 = h1_ref[...] + _dot(a_ref[...], w_ref[...])
        r = lax.rsqrt(jnp.mean(h2 * h2, axis=-1, keepdims=True) + EPS)
        hn = h2 * r
        w3 = w3_ref[...]
        err = hn * w3 - t_ref[...]
        loss_ref[...] += (0.5 / D_MODEL) * jnp.sum(err * err)
        dy = err * (1.0 / D_MODEL)
        gw3_ref[...] += jnp.sum(dy * hn, axis=0, keepdims=True)
        dh2_ref[...] = _rms_bwd(dy * w3, hn, r)

    row = lambda w: pl.BlockSpec((tm, w), lambda i: (i, 0))
    return pl.pallas_call(
        body, name="down_loss", grid=(S // tm,),
        in_specs=[row(FFN), pl.BlockSpec((FFN, D_MODEL), lambda i: (0, 0)), row(D_MODEL), row(D_MODEL),
                  pl.BlockSpec((1, D_MODEL), lambda i: (0, 0))],
        out_specs=[row(D_MODEL), pl.BlockSpec((1, 128), lambda i: (0, 0)), pl.BlockSpec((1, D_MODEL), lambda i: (0, 0))],
        out_shape=[jax.ShapeDtypeStruct((S, D_MODEL), f32), jax.ShapeDtypeStruct((1, 128), f32),
                   jax.ShapeDtypeStruct((1, D_MODEL), f32)],
        compiler_params=_cp("arbitrary"),
    )(act, wdown, h1, tgt, w3)


def _dact(dh2, wdown, silu, up_dsilu):
    S = dh2.shape[0]
    tm = 256

    def body(d_ref, w_ref, s_ref, u_ref, o_ref):
        da = _dot_nt(d_ref[...].astype(bf16), w_ref[...])
        o_ref[1] = (da * s_ref[...].astype(f32)).astype(bf16)
        o_ref[0] = (da * u_ref[...].astype(f32)).astype(bf16)

    row = lambda w: pl.BlockSpec((tm, w), lambda i: (i, 0))
    return pl.pallas_call(
        body, name="dact", grid=(S // tm,),
        in_specs=[row(D_MODEL), pl.BlockSpec((FFN, D_MODEL), lambda i: (0, 0)), row(FFN), row(FFN)],
        out_specs=pl.BlockSpec((2, tm, FFN), lambda i: (0, i, 0)),
        out_shape=jax.ShapeDtypeStruct((2, S, FFN), bf16),
        compiler_params=_cp("arbitrary"),
    )(dh2, wdown, silu, up_dsilu)


def _dgu(dgu2, wgate, wup, h1, w2, dh2, wout, rider=None):
    S = dgu2.shape[1]
    tm = 256

    def body(d_ref, wg_ref, wu_ref, h1_ref, w2_ref, dh2_ref, wo_ref, dh1_ref, gw2_ref, dmix_ref):
        @pl.when(pl.program_id(0) == 0)
        def _():
            gw2_ref[...] = jnp.zeros_like(gw2_ref)

        du2 = _dot_nt(d_ref[0], wg_ref[...]) + _dot_nt(d_ref[1], wu_ref[...])
        h1 = h1_ref[...]
        r = lax.rsqrt(jnp.mean(h1 * h1, axis=-1, keepdims=True) + EPS)
        hn = h1 * r
        gw2_ref[...] += jnp.sum(du2 * hn, axis=0, keepdims=True)
        dh1 = dh2_ref[...] + _rms_bwd(du2 * w2_ref[...], hn, r)
        dh1_ref[...] = dh1
        dmix_ref[...] = _dot_nt(dh1.astype(bf16), wo_ref[...])

    row = lambda w: pl.BlockSpec((tm, w), lambda i: (i, 0))
    call = dict(in_specs=[pl.BlockSpec((2, tm, FFN), lambda i: (0, i, 0)), pl.BlockSpec((D_MODEL, FFN), lambda i: (0, 0)),
                          pl.BlockSpec((D_MODEL, FFN), lambda i: (0, 0)), row(D_MODEL),
                          pl.BlockSpec((1, D_MODEL), lambda i: (0, 0)), row(D_MODEL),
                          pl.BlockSpec((D_MODEL, D_MODEL), lambda i: (0, 0))],
                out_specs=[row(D_MODEL), pl.BlockSpec((1, D_MODEL), lambda i: (0, 0)), row(D_MODEL)],
                out_shape=[jax.ShapeDtypeStruct((S, D_MODEL), f32), jax.ShapeDtypeStruct((1, D_MODEL), f32),
                           jax.ShapeDtypeStruct((S, D_MODEL), f32)], scratch_shapes=[])
    call, body, more = _ride(call, rider, body, lambda: pl.program_id(0), S // tm, 7, 3, 0)
    return pl.pallas_call(body, name="dgu", grid=(S // tm,), compiler_params=_cp("arbitrary"), **call)(
        dgu2, wgate, wup, h1, w2, dh2, wout, *more)


def _din(dq, dk, dv, dhq, dhf, dhi, dhg, cos_t, sg_t, win, x, w1, dh1):
    S = x.shape[0]
    tm = 256

    def body(dq_ref, dk_ref, dv_ref, dhq_ref, dhf_ref, dhi_ref, dhg_ref, cos_ref, sg_ref, w_ref, x_ref, w1_ref, dh1_ref,
             dp_ref, gx_ref, gw1_ref):
        @pl.when(pl.program_id(0) == 0)
        def _():
            gw1_ref[...] = jnp.zeros_like(gw1_ref)

        cosv, sgv = jnp.tile(cos_ref[...], (1, ATTN_W // 128)), jnp.tile(sg_ref[...], (1, ATTN_W // 128))
        unrope = lambda d: d * cosv - sgv * _swap_halves(d)
        parts = [(unrope(dq_ref[...]) * (HEAD_DIM ** -0.5)).astype(bf16), unrope(dk_ref[...]).astype(bf16),
                 dv_ref[...].astype(bf16), dhq_ref[...], dhf_ref[...], dhi_ref[...], dhg_ref[...]]
        du = jnp.zeros((tm, D_MODEL), f32)
        for j, pj in enumerate(parts):
            dp_ref[:, j * 512:(j + 1) * 512] = pj
            du = du + _dot_nt(pj, w_ref[:, j * 512:(j + 1) * 512])
        xv = x_ref[...]
        r = lax.rsqrt(jnp.mean(xv * xv, axis=-1, keepdims=True) + EPS)
        xn = xv * r
        gw1_ref[...] += jnp.sum(du * xn, axis=0, keepdims=True)
        gx_ref[...] = dh1_ref[...] + _rms_bwd(du * w1_ref[...], xn, r)

    row = lambda w: pl.BlockSpec((tm, w), lambda i: (i, 0))
    vec = pl.BlockSpec((1, D_MODEL), lambda i: (0, 0))
    return pl.pallas_call(
        body, name="din", grid=(S // tm,),
        in_specs=[row(512)] * 7 + [row(128), row(128), pl.BlockSpec((D_MODEL, IN_W), lambda i: (0, 0)), row(D_MODEL), vec,
                                   row(D_MODEL)],
        out_specs=[row(IN_W), row(D_MODEL), vec],
        out_shape=[jax.ShapeDtypeStruct((S, IN_W), bf16), jax.ShapeDtypeStruct((S, D_MODEL), f32),
                   jax.ShapeDtypeStruct((1, D_MODEL), f32)],
        compiler_params=_cp("arbitrary"),
    )(dq, dk, dv, dhq, dhf, dhi, dhg, cos_t, sg_t, win, x, w1, dh1)


def _gw(a, bs, tn, name, ts=2048):
    S, M = a.shape
    N = bs[0].shape[1]
    k = len(bs)

    def body(a_ref, *refs):
        @pl.when(pl.program_id(1) == 0)
        def _():
            for o_ref in refs[k:]:
                o_ref[...] = jnp.zeros_like(o_ref)

        at = a_ref[...].astype(bf16)
        for b_ref, o_ref in zip(refs[:k], refs[k:]):
            o_ref[...] += _dot_tn(at, b_ref[...].astype(bf16))

    return pl.pallas_call(
        body, name=name, grid=(N // tn, S // ts),
        in_specs=[pl.BlockSpec((ts, M), lambda j, s: (s, 0))] + [pl.BlockSpec((ts, tn), lambda j, s: (s, j))] * k,
        out_specs=[pl.BlockSpec((M, tn), lambda j, s: (0, j))] * k, out_shape=[jax.ShapeDtypeStruct((M, N), f32)] * k,
        compiler_params=_cp("arbitrary", "arbitrary"),
    )(a, *bs)


def _gw_by_owner(a, b3, w, name, ts):
    S, M = a.shape
    G, _, Ng = b3.shape
    tn = 2 * w
    per_group = Ng // tn
    n_s = S // ts

    def body(a_ref, b_ref, o_ref, acc):
        s = pl.program_id(1)

        @pl.when(s == 0)
        def _():
            acc[...] = jnp.zeros_like(acc)

        acc[...] += _dot_tn(a_ref[...].astype(bf16), b_ref[0].astype(bf16))

        @pl.when(s == n_s - 1)
        def _():
            o_ref[0] = acc[:, 0:w]
            o_ref[1] = acc[:, w:tn]

    return pl.pallas_call(
        body, name=name, grid=(G * per_group, n_s),
        in_specs=[pl.BlockSpec((ts, M), lambda j, s: (s, 0)),
                  pl.BlockSpec((1, ts, tn), lambda j, s: (j // per_group, s, j % per_group))],
        out_specs=pl.BlockSpec((2, M, w), lambda j, s: (j, 0, 0)), out_shape=jax.ShapeDtypeStruct((G * Ng // w, M, w), f32),
        scratch_shapes=[pltpu.VMEM((M, tn), f32)], compiler_params=_cp("arbitrary", "arbitrary"),
    )(a, b3)


MESH = pl.DeviceIdType.MESH
ANY = pl.BlockSpec(memory_space=pl.ANY)
VMEM_SPEC = pl.BlockSpec(memory_space=pltpu.VMEM)


def _pos():
    return lax.axis_index("x"), lax.axis_index("y"), lax.axis_index("c")


def _flip(v, bit):
    return 1 - v if bit else v


def _gather_rider(shards):
    n = len(shards)

    def parts(outs, scratch):
        send_sems, recv_sems, local_sems = scratch[n:]
        x, y, c = _pos()
        chips = [(1 - x, y), (x, 1 - y), (1 - x, 1 - y)]

        def copy(a, k, block, to, src=None):
            dst = outs[a].at[4 * block[0] + 2 * block[1] + block[2]]
            return pltpu.make_async_remote_copy(src_ref=dst if src is None else src, dst_ref=dst, send_sem=send_sems.at[a, k],
                                                recv_sem=recv_sems.at[a, k], device_id=to, device_id_type=MESH)

        bufs = scratch[:n]
        me, sibling = (x, y, c), (x, y, 1 - c)
        own = lambda a: pltpu.make_async_copy(bufs[a], outs[a].at[4 * x + 2 * y + c], local_sems.at[a])
        sent = lambda a: [copy(a, 0, me, sibling, src=bufs[a])] + [copy(a, 1 + j, me, (*chip, c), src=bufs[a])
                                                                   for j, chip in enumerate(chips)]
        passed = lambda a: [copy(a, 4 + j, (*chip, c), sibling) for j, chip in enumerate(chips)]
        landed = lambda a: [copy(a, 1 + j, (*chip, c), me) for j, chip in enumerate(chips)]
        from_sibling = lambda a: [copy(a, 0, sibling, me)] + [copy(a, 4 + j, (*chip, 1 - c), me) for j, chip in enumerate(chips)]
        return bufs, local_sems, own, sent, passed, landed, from_sibling

    def first(ins, outs, scratch):
        bufs, local_sems, own, sent, _, _, _ = parts(outs, scratch)
        loads = [pltpu.make_async_copy(ins[a], bufs[a], local_sems.at[a]) for a in range(n)]
        for ld in loads:
            ld.start()
        for a in range(n):
            loads[a].wait()
            own(a).start()
            for cp in sent(a):
                cp.start()

    def middle(ins, outs, scratch):
        _, _, _, _, passed, landed, _ = parts(outs, scratch)
        for a in range(n):
            for got, on in zip(landed(a), passed(a)):
                got.wait_recv()
                on.start()

    def last(ins, outs, scratch):
        _, _, own, sent, passed, _, from_sibling = parts(outs, scratch)
        for a in range(n):
            for cp in from_sibling(a):
                cp.wait_recv()
        for a in range(n):
            for cp in sent(a) + passed(a):
                cp.wait_send()
            own(a).wait()

    return _Rider(shards, [jax.ShapeDtypeStruct((N_DEV,) + s.shape, s.dtype) for s in shards],
                  [pltpu.VMEM(s.shape, s.dtype) for s in shards]
                  + [pltpu.SemaphoreType.DMA((n, 7)), pltpu.SemaphoreType.DMA((n, 7)), pltpu.SemaphoreType.DMA((n,))],
                  first, last, middle)


def _sibling_rider(grads):
    n = len(grads)

    def copies(g, got, scratch):
        send_sems, recv_sems = scratch
        x, y, c = _pos()
        return [pltpu.make_async_remote_copy(src_ref=g[a].at[2 * q + (1 - c)], dst_ref=got[a].at[q], send_sem=send_sems.at[a, q],
                                             recv_sem=recv_sems.at[a, q], device_id=(x, y, 1 - c), device_id_type=MESH)
                for a in range(n) for q in range(4)]

    def first(g, got, scratch):
        for cp in copies(g, got, scratch):
            cp.start()

    def last(g, got, scratch):
        for cp in copies(g, got, scratch):
            cp.wait()

    return _Rider(grads, [jax.ShapeDtypeStruct((4,) + g.shape[1:], g.dtype) for g in grads],
                  [pltpu.SemaphoreType.DMA((n, 4))] * 2, first, last)


def _chips_rider(sums):
    n = len(sums)

    def copies(s, out, scratch):
        send_sems, recv_sems = scratch
        x, y, c = _pos()
        cps = []
        for a in range(n):
            for f in (1, 2, 3):
                peer = (_flip(x, f >> 1), _flip(y, f & 1), c)
                cps.append(pltpu.make_async_remote_copy(
                    src_ref=s[a].at[2 * peer[0] + peer[1]], dst_ref=out[a].at[f - 1], send_sem=send_sems.at[a, f - 1],
                    recv_sem=recv_sems.at[a, f - 1], device_id=peer, device_id_type=MESH))
        return cps

    def first(s, out, scratch):
        for cp in copies(s, out, scratch):
            cp.start()

    def last(s, out, scratch):
        for cp in copies(s, out, scratch):
            cp.wait()

    return _Rider(sums, [jax.ShapeDtypeStruct((3,) + s.shape[1:], s.dtype) for s in sums],
                  [pltpu.SemaphoreType.DMA((n, 3))] * 2, first, last)


def _both(a, b):
    na = (len(a.ins), len(a.out_shapes), len(a.scratch))

    def split(fa, fb):
        def f(ins, outs, scratch):
            fa(ins[:na[0]], outs[:na[1]], scratch[:na[2]])
            fb(ins[na[0]:], outs[na[1]:], scratch[na[2]:])
        return f

    return _Rider(a.ins + b.ins, a.out_shapes + b.out_shapes, a.scratch + b.scratch, split(a.first, b.first), split(a.last, b.last))


def _alone(rider, name):
    ri, ro = len(rider.ins), len(rider.out_shapes)

    def body(*refs):
        theirs = (refs[:ri], refs[ri:ri + ro], refs[ri + ro:])
        rider.first(*theirs)
        if rider.middle is not None:
            rider.middle(*theirs)
        rider.last(*theirs)

    return pl.pallas_call(body, name=name, in_specs=[ANY] * ri, out_specs=[ANY] * ro, out_shape=rider.out_shapes,
                          scratch_shapes=rider.scratch)(*rider.ins)


def _gather_small(g_w1, g_w2, g_w3, g_lb, g_wn, loss):
    def body(w1_ref, w2_ref, w3_ref, lb_ref, wn_ref, loss_ref, out_ref, pk, send_sems, recv_sems):
        x, y, c = _pos()
        me = 4 * x + 2 * y + c
        pk[...] = jnp.zeros_like(pk)
        pk[0:1, :] = w1_ref[...]
        pk[1:2, :] = w2_ref[...]
        pk[2:3, :] = w3_ref[...]
        pk[3:4, 0:HGRN_W] = lb_ref[...]
        pk[3:4, HGRN_W:2 * HGRN_W] = wn_ref[...]
        pk[4:5, 0:128] = loss_ref[...]
        out_ref[me] = pk[...]
        sends, recvs = [], []
        for k in range(1, N_DEV):
            peer = (_flip(x, k >> 2), _flip(y, (k >> 1) & 1), _flip(c, k & 1))
            cp = pltpu.make_async_remote_copy(src_ref=pk, dst_ref=out_ref.at[me], send_sem=send_sems.at[k - 1],
                                              recv_sem=recv_sems.at[k - 1], device_id=peer, device_id_type=MESH)
            cp.start()
            sends.append(cp)
            recvs.append(pltpu.make_async_remote_copy(src_ref=pk, dst_ref=out_ref.at[4 * peer[0] + 2 * peer[1] + peer[2]],
                                                      send_sem=send_sems.at[k - 1], recv_sem=recv_sems.at[k - 1], device_id=peer,
                                                      device_id_type=MESH))
        for cp in recvs:
            cp.wait_recv()
        for cp in sends:
            cp.wait_send()

    return pl.pallas_call(
        body, name="gather_small", in_specs=[VMEM_SPEC] * 6, out_specs=VMEM_SPEC,
        out_shape=jax.ShapeDtypeStruct((N_DEV, 8, D_MODEL), f32),
        scratch_shapes=[pltpu.VMEM((8, D_MODEL), f32), pltpu.SemaphoreType.DMA((N_DEV - 1,)), pltpu.SemaphoreType.DMA((N_DEV - 1,))],
    )(g_w1, g_w2, g_w3, g_lb, g_wn, loss)


def _row_tile(r):
    return max(t for t in range(8, 257, 8) if r % t == 0)


def _add_sibling(core, g, got, name):
    _, r, c = got.shape
    tr = _row_tile(r)

    def body(core_ref, a_ref, b_ref, o_ref):
        o_ref[...] = (a_ref[...] + b_ref[...]).astype(bf16)

    blk = pl.BlockSpec((1, tr, c), lambda q, i, core_ref: (q, i, 0))
    return pl.pallas_call(
        body, name=name, out_shape=jax.ShapeDtypeStruct(got.shape, bf16),
        grid_spec=pltpu.PrefetchScalarGridSpec(
            num_scalar_prefetch=1, grid=(4, r // tr),
            in_specs=[pl.BlockSpec((1, tr, c), lambda q, i, core_ref: (2 * q + core_ref[0], i, 0)), blk], out_specs=blk),
        compiler_params=_cp("arbitrary", "arbitrary"))(core, g, got)


def _adamw(w, g, m, v):
    m = ADAM_B1 * m + (1.0 - ADAM_B1) * g
    v = ADAM_B2 * v + (1.0 - ADAM_B2) * (g * g)
    m_hat = m / (1.0 - ADAM_B1 ** ADAM_STEP)
    v_hat = v / (1.0 - ADAM_B2 ** ADAM_STEP)
    return -ADAM_LR * (m_hat / (jnp.sqrt(v_hat) + ADAM_EPS) + ADAM_WD * w), m, v


def _adam_shard(where, g, got, pieces, w, m, v, name):
    r, c = w.shape
    tr = _row_tile(r)

    def body(where_ref, g_ref, got_ref, p_ref, w_ref, m_ref, v_ref, g_out, d_out, m_out, v_out):
        gsum = g_ref[0] + got_ref[0]
        for f in range(3):
            gsum = gsum + p_ref[f].astype(f32)
        g_out[...] = gsum
        d_out[...], m_out[...], v_out[...] = _adamw(w_ref[...], gsum, m_ref[...], v_ref[...])

    blk = pl.BlockSpec((tr, c), lambda i, where_ref: (i, 0))
    return pl.pallas_call(
        body, name=name, out_shape=[jax.ShapeDtypeStruct((r, c), f32)] * 4,
        grid_spec=pltpu.PrefetchScalarGridSpec(
            num_scalar_prefetch=1, grid=(r // tr,),
            in_specs=[pl.BlockSpec((1, tr, c), lambda i, where_ref: (where_ref[0], i, 0)),
                      pl.BlockSpec((1, tr, c), lambda i, where_ref: (where_ref[1], i, 0)),
                      pl.BlockSpec((3, tr, c), lambda i, where_ref: (0, i, 0)), blk, blk, blk],
            out_specs=[blk] * 4),
        compiler_params=_cp("arbitrary"),
    )(where, g, got, pieces, w, m, v)


def _small_update(gath, params):
    def body(gath_ref, *refs):
        ins, outs = refs[:15], refs[15:]
        gs = gath_ref[0]
        for k in range(1, N_DEV):
            gs = gs + gath_ref[k]
        outs[0][...] = gs[4:5, 0:128]
        l0, l1 = ins[9][0:1, :], ins[9][1:2, :]
        lb = _sigmoid(l0 - l1)
        d0 = gs[3:4, 0:HGRN_W] * lb * (1.0 - lb)
        first_row = lax.broadcasted_iota(jnp.int32, (2, HGRN_W), 0) == 0
        grads = [gs[0:1, :], gs[1:2, :], gs[2:3, :], jnp.where(first_row, d0, -d0), gs[3:4, HGRN_W:2 * HGRN_W]]
        for i, g in enumerate(grads):
            w_ref, m_ref, v_ref = ins[3 * i:3 * i + 3]
            o = outs[1 + 4 * i:5 + 4 * i]
            o[0][...] = g
            o[1][...], o[2][...], o[3][...] = _adamw(w_ref[...], g, m_ref[...], v_ref[...])

    flat = [a for p in params for a in p]
    out_shape = [jax.ShapeDtypeStruct((1, 128), f32)] + [jax.ShapeDtypeStruct(p[0].shape, f32) for p in params for _ in range(4)]
    outs = pl.pallas_call(body, name="small_update", in_specs=[VMEM_SPEC] * 16, out_specs=[VMEM_SPEC] * 21, out_shape=out_shape)(gath, *flat)
    return outs[0], [outs[1 + 4 * i:5 + 4 * i] for i in range(5)]


def kernel(x, norm1_w, w_in, lb_logits, hgrn_norm_w, w_out, norm2_w, w_gate_up, w_down, final_norm_w, loss_target, m_norm1_w, m_w_in, m_lb_logits, m_hgrn_norm_w, m_w_out, m_norm2_w, m_w_gate_up, m_w_down, m_final_norm_w, v_norm1_w, v_w_in, v_lb_logits, v_hgrn_norm_w, v_w_out, v_norm2_w, v_w_gate_up, v_w_down, v_final_norm_w):
    row = lambda a: a.reshape(1, D_MODEL)
    ix, iy, ic = lax.axis_index("x"), lax.axis_index("y"), lax.axis_index("c")
    core = jnp.stack([ic]).astype(jnp.int32)
    where = jnp.stack([4 * ix + 2 * iy + ic, 2 * ix + iy]).astype(jnp.int32)
    xs, tgt, w3 = x[0], loss_target[0], row(final_norm_w)
    S = xs.shape[0]

    cos_t, sg_t, win_g = _rope_tables(S, _gather_rider([w_in[0].astype(bf16)]))
    u, qkv, hp, win = _in_proj(xs, norm1_w, win_g, cos_t, sg_t)
    ya, lse, wout_g, wgu_g, wdown_g = _attn_fwd(qkv, _gather_rider([w_out[0].astype(bf16), w_gate_up[0].astype(bf16),
                                                                     w_down[0].astype(bf16)]))
    wout = wout_g.reshape(D_MODEL, D_MODEL)
    wdown = wdown_g.reshape(FFN, D_MODEL)
    yb, o_sav, states = _hgrn_fwd(hp, lb_logits, hgrn_norm_w)
    h1, u2, mixed = _out_proj(xs, ya, yb, wout, norm2_w)
    silu, up_dsilu, act, wgate, wup = _gate_up(u2, wgu_g)
    dh2, loss_p, g_w3 = _down_loss(act, wdown, h1, tgt, w3)

    (g_wdown,) = _gw(act, [dh2], 512, "gw_down")
    dgu2 = _dact(dh2, wdown, silu, up_dsilu)
    early = [_gw_by_owner(u2, dgu2, 2 * FFN // N_DEV, "gw_gate_up", 2048), g_wdown.reshape(N_DEV, FFN // N_DEV, D_MODEL)]
    dh1, g_w2, dmix, *got_early = _dgu(dgu2, wgate, wup, h1, norm2_w, dh2, wout, _sibling_rider(early))
    sums_early = [_add_sibling(core, g, o, f"add_sibling_{i}") for i, (g, o) in enumerate(zip(early, got_early))]
    (g_wout,) = _gw(mixed, [dh1], 1024, "gw_out")
    mid = [g_wout.reshape(N_DEV, D_MODEL // N_DEV, D_MODEL)]
    dhq, dhf, dhi, dhg, g_wn, g_lb, *rode = _hgrn_bwd(hp, lb_logits, hgrn_norm_w, o_sav, states, dmix,
                                                      _both(_chips_rider(sums_early), _sibling_rider(mid)))
    pieces_early, got_mid = rode[:2], rode[2:]
    sums_mid = [_add_sibling(core, mid[0], got_mid[0], "add_sibling_2")]
    dq, dk, dv, *pieces_mid = _attn_bwd(qkv, ya, lse, dmix, _chips_rider(sums_mid))
    dproj, gx, g_w1 = _din(dq, dk, dv, dhq, dhf, dhi, dhg, cos_t, sg_t, win, xs, norm1_w, dh1)
    late = [_gw_by_owner(u, dproj[None], IN_W // N_DEV, "gw_in", 2048)]
    got_late = _alone(_sibling_rider(late), "reduce_sibling")
    sums_late = [_add_sibling(core, late[0], got_late[0], "add_sibling_3")]
    pieces_late = _alone(_chips_rider(sums_late), "reduce_chips")

    grads = [late[0], mid[0], early[0], early[1]]
    got = [got_late[0], got_mid[0], got_early[0], got_early[1]]
    pieces = [pieces_late[0], pieces_mid[0], pieces_early[0], pieces_early[1]]
    shards = [w_in[0], w_out[0], w_gate_up[0], w_down[0]]
    moms = [(m_w_in[0], v_w_in[0]), (m_w_out[0], v_w_out[0]), (m_w_gate_up[0], v_w_gate_up[0]), (m_w_down[0], v_w_down[0])]
    big = [_adam_shard(where, g, o, p, w, m, v, f"adam_{i}")
           for i, (g, o, p, w, (m, v)) in enumerate(zip(grads, got, pieces, shards, moms))]
    big = [[a[None] for a in four] for four in big]

    gath = _gather_small(g_w1, g_w2, g_w3, g_lb, g_wn, loss_p)
    params = [(norm1_w, m_norm1_w, v_norm1_w), (norm2_w, m_norm2_w, v_norm2_w),
              (row(final_norm_w), row(m_final_norm_w), row(v_final_norm_w)),
              (lb_logits, m_lb_logits, v_lb_logits), (hgrn_norm_w, m_hgrn_norm_w, v_hgrn_norm_w)]
    loss, (s_w1, s_w2, s_w3, s_lb, s_wn) = _small_update(gath, params)
    s_w3 = [a.reshape(D_MODEL) for a in s_w3]
    per_w = [s_w1, big[0], s_lb, s_wn, big[1], s_w2, big[2], big[3], s_w3]
    return (loss[0, 0], gx[None], *[p[0] for p in per_w], *[p[1] for p in per_w], *[p[2] for p in per_w], *[p[3] for p in per_w])
```

```python
import jax
import jax.numpy as jnp
from jax import lax
from jax.experimental import pallas as pl
from jax.experimental.pallas import tpu as pltpu

f32, bf16 = jnp.float32, jnp.bfloat16

D_MODEL = 1024
ATTN_W = 512
HEAD_DIM = 64
ATTN_BLK = 128
DILATIONS = (1, 4, 16)
HGRN_W = 512
HGRN_HD = 128
CHUNK = 64
IN_W = 3 * ATTN_W + 4 * HGRN_W
FFN = 2816
EPS = 1e-6
ROPE_THETA = 10000.0
NEG = -1e30
N_DEV = 8
ADAM_LR, ADAM_B1, ADAM_B2, ADAM_EPS, ADAM_WD, ADAM_STEP = 0.001, 0.9, 0.999, 1e-08, 0.01, 10
VMEM_LIMIT = 56 * 1024 * 1024


def _cp(*sem):
    return pltpu.CompilerParams(dimension_semantics=sem, vmem_limit_bytes=VMEM_LIMIT)


def _dot(a, b):
    return jnp.dot(a, b, preferred_element_type=f32)


def _dot_nt(a, b):
    return lax.dot_general(a, b, (((1,), (1,)), ((), ())), preferred_element_type=f32)


def _dot_tn(a, b):
    return lax.dot_general(a, b, (((0,), (0,)), ((), ())), preferred_element_type=f32)


def _sigmoid(x):
    return 0.5 * jnp.tanh(0.5 * x) + 0.5


class _Rider:
    def __init__(self, ins, out_shapes, scratch, first, last, middle=None):
        self.ins, self.out_shapes, self.scratch = list(ins), list(out_shapes), list(scratch)
        self.first, self.middle, self.last = first, middle, last


def _ride(call, rider, body, step, n_steps, n_in, n_out, n_scratch):
    if rider is None:
        return call, body, []
    ri, ro = len(rider.ins), len(rider.out_shapes)
    any_spec = pl.BlockSpec(memory_space=pl.ANY)
    call = dict(call, in_specs=call["in_specs"] + [any_spec] * ri, out_specs=call["out_specs"] + [any_spec] * ro,
                out_shape=call["out_shape"] + rider.out_shapes, scratch_shapes=call["scratch_shapes"] + rider.scratch)

    def riding(*refs):
        a = n_in + ri
        b = a + n_out + ro
        mine = refs[:n_in] + refs[a:a + n_out] + refs[b:b + n_scratch]
        theirs = (refs[n_in:a], refs[a + n_out:b], refs[b + n_scratch:])
        t = step()

        @pl.when(t == 0)
        def _():
            rider.first(*theirs)

        body(*mine)
        if rider.middle is not None:
            @pl.when(t == n_steps // 2)
            def _():
                rider.middle(*theirs)

        @pl.when(t == n_steps - 1)
        def _():
            rider.last(*theirs)

    return call, riding, rider.ins


def _rope_tables(S, rider=None):
    half = HEAD_DIM // 2
    tm = 256
    inv_freq = jnp.tile(ROPE_THETA ** (-jnp.arange(half, dtype=f32) / half), 128 // half).reshape(1, 128)
    sign = jnp.tile(jnp.concatenate([-jnp.ones((half,), f32), jnp.ones((half,), f32)]), 128 // HEAD_DIM).reshape(1, 128)

    def body(inv_ref, sign_ref, cos_ref, sg_ref):
        pos = (lax.broadcasted_iota(jnp.int32, (tm, 128), 0) + pl.program_id(0) * tm).astype(f32)
        ang = pos * inv_ref[...]
        cos_ref[...] = jnp.cos(ang)
        sg_ref[...] = jnp.sin(ang) * sign_ref[...]

    vec = pl.BlockSpec((1, 128), lambda i: (0, 0))
    out = pl.BlockSpec((tm, 128), lambda i: (i, 0))
    call = dict(in_specs=[vec, vec], out_specs=[out, out], out_shape=[jax.ShapeDtypeStruct((S, 128), f32)] * 2, scratch_shapes=[])
    call, body, more = _ride(call, rider, body, lambda: pl.program_id(0), S // tm, 2, 2, 0)
    return pl.pallas_call(body, name="rope_tables", grid=(S // tm,), compiler_params=_cp("arbitrary"), **call)(inv_freq, sign, *more)


def _swap_halves(v):
    n = v.shape[1]
    lane = lax.broadcasted_iota(jnp.int32, v.shape, 1)
    return jnp.where((lane % HEAD_DIM) < HEAD_DIM // 2, pltpu.roll(v, n - HEAD_DIM // 2, 1), pltpu.roll(v, HEAD_DIM // 2, 1))


def _in_proj(x, w1, win_g, cos_t, sg_t):
    S = x.shape[0]
    tm = 256
    w = IN_W // N_DEV

    def body(x_ref, w1_ref, wg_ref, cos_ref, sg_ref, u_ref, qkv_ref, hp_ref, w_ref):
        @pl.when(pl.program_id(0) == 0)
        def _():
            for d in range(N_DEV):
                w_ref[:, w * d:w * (d + 1)] = wg_ref[d]

        xv = x_ref[...]
        r = lax.rsqrt(jnp.mean(xv * xv, axis=-1, keepdims=True) + EPS)
        u = (xv * r * w1_ref[...]).astype(bf16)
        u_ref[...] = u
        cosv, sgv = jnp.tile(cos_ref[...], (1, ATTN_W // 128)), jnp.tile(sg_ref[...], (1, ATTN_W // 128))
        for j in range(3):
            pj = _dot(u, w_ref[:, j * ATTN_W:(j + 1) * ATTN_W])
            if j < 2:
                pj = pj * cosv + _swap_halves(pj) * sgv
            if j == 0:
                pj = pj * (HEAD_DIM ** -0.5)
            qkv_ref[:, j * ATTN_W:(j + 1) * ATTN_W] = pj.astype(bf16)
        for j in range(4):
            lo = 3 * ATTN_W + j * HGRN_W
            hp_ref[:, j * HGRN_W:(j + 1) * HGRN_W] = _dot(u, w_ref[:, lo:lo + HGRN_W])

    return pl.pallas_call(
        body, name="in_proj", grid=(S // tm,),
        in_specs=[pl.BlockSpec((tm, D_MODEL), lambda i: (i, 0)), pl.BlockSpec((1, D_MODEL), lambda i: (0, 0)),
                  pl.BlockSpec((N_DEV, D_MODEL, w), lambda i: (0, 0, 0)),
                  pl.BlockSpec((tm, 128), lambda i: (i, 0)), pl.BlockSpec((tm, 128), lambda i: (i, 0))],
        out_specs=[pl.BlockSpec((tm, D_MODEL), lambda i: (i, 0)), pl.BlockSpec((tm, 3 * ATTN_W), lambda i: (i, 0)),
                   pl.BlockSpec((tm, 4 * HGRN_W), lambda i: (i, 0)), pl.BlockSpec((D_MODEL, IN_W), lambda i: (0, 0))],
        out_shape=[jax.ShapeDtypeStruct((S, D_MODEL), bf16), jax.ShapeDtypeStruct((S, 3 * ATTN_W), bf16),
                   jax.ShapeDtypeStruct((S, 4 * HGRN_W), f32), jax.ShapeDtypeStruct((D_MODEL, IN_W), bf16)],
        compiler_params=_cp("arbitrary"),
    )(x, w1, win_g, cos_t, sg_t)


def _head_masks():
    lane = lax.broadcasted_iota(jnp.int32, (ATTN_BLK, 128), 1)
    even = lane < HEAD_DIM
    return even, (even, jnp.logical_not(even))


def _pair_fwd(q2, k2, v2, bias):
    even, masks = _head_masks()
    outs, lses = [], []
    for e in range(2):
        qm = jnp.where(masks[e], q2, 0.0).astype(bf16)
        s = _dot_nt(qm, k2) + bias
        m = jnp.max(s, axis=-1, keepdims=True)
        pe = jnp.exp(s - m)
        lsum = jnp.sum(pe, axis=-1, keepdims=True)
        outs.append(_dot(pe.astype(bf16), v2) / lsum)
        lses.append(jnp.broadcast_to(m + jnp.log(lsum), (ATTN_BLK, 128)))
    return jnp.where(even, outs[0], outs[1]), jnp.where(even, lses[0], lses[1])


def _merge(y0, l0, y1, l1):
    mx = jnp.maximum(l0, l1)
    a, b = jnp.exp(l0 - mx), jnp.exp(l1 - mx)
    tot = a + b
    return (a * y0 + b * y1) / tot, mx + jnp.log(tot)


def _pair_bwd(q2, k2f, v2, dy2, lse2, delta2, bias):
    _, masks = _head_masks()
    k2 = k2f.astype(bf16)
    klane = lax.broadcasted_iota(jnp.int32, (2 * ATTN_BLK, 128), 1) < HEAD_DIM
    kmasks = (klane, jnp.logical_not(klane))
    dq2 = jnp.zeros((ATTN_BLK, 128), f32)
    pes, dss, qms, dyms = [], [], [], []
    for e in range(2):
        c0 = e * HEAD_DIM
        qm = jnp.where(masks[e], q2, 0.0).astype(bf16)
        km = jnp.where(kmasks[e], k2f, 0.0).astype(bf16)
        dym = jnp.where(masks[e], dy2, 0.0).astype(bf16)
        pe = jnp.exp(_dot_nt(qm, k2) + bias - lse2[:, c0:c0 + 1])
        ds = (pe * (_dot_nt(dym, v2) - delta2[:, c0:c0 + 1])).astype(bf16)
        dq2 = dq2 + _dot(ds, km)
        pes.append(pe.astype(bf16))
        dss.append(ds)
        qms.append(qm)
        dyms.append(dym)
    dv2 = _dot_tn(jnp.concatenate(pes, axis=0), jnp.concatenate(dyms, axis=0))
    dk2 = _dot_tn(jnp.concatenate(dss, axis=0), jnp.concatenate(qms, axis=0))
    return dq2, dk2, dv2


TOK = 2048


def _key_bias():
    qi = lax.broadcasted_iota(jnp.int32, (ATTN_BLK, 2 * ATTN_BLK), 0)
    kj = lax.broadcasted_iota(jnp.int32, (ATTN_BLK, 2 * ATTN_BLK), 1)
    delta = ATTN_BLK + qi - kj
    seen = (delta >= 0) & (delta <= ATTN_BLK)
    return jnp.where(seen, 0.0, NEG), jnp.where(seen & (kj >= ATTN_BLK), 0.0, NEG)


def _attn_fwd(qkv, rider=None):
    S = qkv.shape[0]
    nS = S // TOK

    def body(q_ref, kp_ref, kc_ref, vp_ref, vc_ref, y_ref, l_ref, qs, k2, v2, ay, al):
        n = pl.program_id(1)
        qs[...] = q_ref[...].astype(f32)
        k2[0:TOK] = kp_ref[...].astype(f32)
        k2[TOK:2 * TOK] = kc_ref[...].astype(f32)
        v2[0:TOK] = vp_ref[...].astype(f32)
        v2[TOK:2 * TOK] = vc_ref[...].astype(f32)
        bias_any, bias_first = _key_bias()

        def block(dil, r, b, step, last):
            start = r + pl.multiple_of(step * b, step)
            rows = pl.ds(start, ATTN_BLK, stride=dil) if dil > 1 else pl.ds(start, ATTN_BLK)
            keys = (pl.ds(TOK + start - step, 2 * ATTN_BLK, stride=dil) if dil > 1
                    else pl.ds(TOK + start - step, 2 * ATTN_BLK))
            bias = jnp.where((n == 0) & (b == 0), bias_first, bias_any)
            out, lse = _pair_fwd(qs[rows, :], k2[keys, :].astype(bf16), v2[keys, :].astype(bf16), bias)
            if dil < DILATIONS[-1]:
                out, lse = _merge(ay[rows, :], al[rows, :], out, lse)
            if last:
                y_ref[rows, :] = out
                l_ref[rows, :] = lse
            else:
                ay[rows, :] = out
                al[rows, :] = lse

        for dil in reversed(DILATIONS):
            def loop(i, carry, dil=dil):
                block(dil, i % dil, i // dil, ATTN_BLK * dil, dil == 1)
                return carry
            lax.fori_loop(0, TOK // ATTN_BLK, loop, 0, unroll=True)

    blk = (TOK, 128)
    cur = lambda c: pl.BlockSpec(blk, lambda p, n: (n, 4 * c + p))
    prv = lambda c: pl.BlockSpec(blk, lambda p, n: (jnp.maximum(n - 1, 0), 4 * c + p))
    out = pl.BlockSpec(blk, lambda p, n: (n, p))
    call = dict(in_specs=[cur(0), prv(1), cur(1), prv(2), cur(2)], out_specs=[out, out],
                out_shape=[jax.ShapeDtypeStruct((S, ATTN_W), f32)] * 2,
                scratch_shapes=[pltpu.VMEM(blk, f32), pltpu.VMEM((2 * TOK, 128), f32), pltpu.VMEM((2 * TOK, 128), f32),
                                pltpu.VMEM(blk, f32), pltpu.VMEM(blk, f32)])
    call, body, more = _ride(call, rider, body, lambda: pl.program_id(0) * nS + pl.program_id(1), (ATTN_W // 128) * nS, 5, 2, 5)
    return pl.pallas_call(body, name="attention_fwd", grid=(ATTN_W // 128, nS), compiler_params=_cp("arbitrary", "arbitrary"),
                          **call)(qkv, qkv, qkv, qkv, qkv, *more)


def _attn_bwd(qkv, ya, lse, dmix, rider=None):
    S = qkv.shape[0]
    nS = S // TOK

    def body(q_ref, kp_ref, kc_ref, vp_ref, vc_ref, y_ref, l_ref, dy_ref, dq_ref, dk_ref, dv_ref, qs, k2, v2, dk2, dv2, dqa, dl):
        n = pl.program_id(1)

        @pl.when(n == 0)
        def _():
            dk2[...] = jnp.zeros_like(dk2)
            dv2[...] = jnp.zeros_like(dv2)

        @pl.when(n < nS)
        def _():
            qs[...] = q_ref[...].astype(f32)
            k2[0:TOK] = kp_ref[...].astype(f32)
            k2[TOK:2 * TOK] = kc_ref[...].astype(f32)
            v2[0:TOK] = vp_ref[...].astype(f32)
            v2[TOK:2 * TOK] = vc_ref[...].astype(f32)
            li = lax.broadcasted_iota(jnp.int32, (128, 128), 0)
            lj = lax.broadcasted_iota(jnp.int32, (128, 128), 1)
            seg = jnp.where((li // HEAD_DIM) == (lj // HEAD_DIM), 1.0, 0.0).astype(bf16)
            bias_any, bias_first = _key_bias()

            def delta_rows(t, carry):
                rows = pl.ds(pl.multiple_of(256 * t, 256), 256)
                dyy = dy_ref[rows, :] * y_ref[rows, :]
                hi = dyy.astype(bf16)
                dl[rows, :] = _dot(hi, seg) + _dot((dyy - hi.astype(f32)).astype(bf16), seg)
                return carry

            lax.fori_loop(0, TOK // 256, delta_rows, 0)

            def block(dil, r, b, step, first_pattern, last):
                start = r + pl.multiple_of(step * b, step)
                rows = pl.ds(start, ATTN_BLK, stride=dil) if dil > 1 else pl.ds(start, ATTN_BLK)
                keys = (pl.ds(TOK + start - step, 2 * ATTN_BLK, stride=dil) if dil > 1
                        else pl.ds(TOK + start - step, 2 * ATTN_BLK))
                bias = jnp.where((n == 0) & (b == 0), bias_first, bias_any)
                dq2, dkk, dvv = _pair_bwd(qs[rows, :], k2[keys, :], v2[keys, :].astype(bf16), dy_ref[rows, :],
                                          l_ref[rows, :], dl[rows, :], bias)
                if last:
                    dq_ref[rows, :] = dqa[rows, :] + dq2
                elif first_pattern:
                    dqa[rows, :] = dq2
                else:
                    dqa[rows, :] += dq2
                dk2[keys, :] += dkk
                dv2[keys, :] += dvv

            for dil in reversed(DILATIONS):
                def loop(i, carry, dil=dil):
                    block(dil, i % dil, i // dil, ATTN_BLK * dil, dil == DILATIONS[-1], dil == 1)
                    return carry
                lax.fori_loop(0, TOK // ATTN_BLK, loop, 0, unroll=True)

        dk_ref[...] = dk2[0:TOK]
        dv_ref[...] = dv2[0:TOK]
        dk2[0:TOK] = dk2[TOK:2 * TOK]
        dv2[0:TOK] = dv2[TOK:2 * TOK]
        dk2[TOK:2 * TOK] = jnp.zeros((TOK, 128), f32)
        dv2[TOK:2 * TOK] = jnp.zeros((TOK, 128), f32)

    blk = (TOK, 128)
    cn = lambda n: jnp.minimum(n, nS - 1)
    pn = lambda n: jnp.clip(n - 1, 0, nS - 1)
    cur = lambda c: pl.BlockSpec(blk, lambda p, n: (cn(n), 4 * c + p))
    prv = lambda c: pl.BlockSpec(blk, lambda p, n: (pn(n), 4 * c + p))
    at_n = pl.BlockSpec(blk, lambda p, n: (cn(n), p))
    at_p = pl.BlockSpec(blk, lambda p, n: (pn(n), p))
    big = lambda: pltpu.VMEM((2 * TOK, 128), f32)
    call = dict(in_specs=[cur(0), prv(1), cur(1), prv(2), cur(2), at_n, at_n, at_n], out_specs=[at_n, at_p, at_p],
                out_shape=[jax.ShapeDtypeStruct((S, ATTN_W), f32)] * 3,
                scratch_shapes=[pltpu.VMEM(blk, f32), big(), big(), big(), big(), pltpu.VMEM(blk, f32), pltpu.VMEM(blk, f32)])
    call, body, more = _ride(call, rider, body, lambda: pl.program_id(0) * (nS + 1) + pl.program_id(1),
                             (ATTN_W // 128) * (nS + 1), 8, 3, 7)
    return pl.pallas_call(body, name="attention_bwd", grid=(ATTN_W // 128, nS + 1), compiler_params=_cp("arbitrary", "arbitrary"),
                          **call)(qkv, qkv, qkv, qkv, qkv, ya, lse, dmix, *more)


HG_T = 512
N_HH = HGRN_W // HGRN_HD
HG_SUB = 128
SAFE_RANGE = 75.0


def _row_in_chunk():
    return lax.broadcasted_iota(jnp.int32, (HG_T, HGRN_HD), 0) % CHUNK


def _chunk_cumsum(v, rc):
    k = 1
    while k < CHUNK:
        v = v + jnp.where(rc >= k, pltpu.roll(v, k, 0), 0.0)
        k *= 2
    return v


def _chunk_rcumsum(v, rc):
    k = 1
    while k < CHUNK:
        v = v + jnp.where(rc < CHUNK - k, pltpu.roll(v, HG_T - k, 0), 0.0)
        k *= 2
    return v


def _hgrn_gates(qb, fb, lb):
    sf = _sigmoid(fb)
    f = lb + (1.0 - lb) * sf
    sq = _sigmoid(qb)
    return sf, f, jnp.log(f), 1.0 - f, sq, qb * sq


def _hgrn_prep(qb, fb, lbl2, rc):
    lb = _sigmoid(lbl2[0:1, :] - lbl2[1:2, :])
    sf, f, lf, key, sq, qf = _hgrn_gates(qb, fb, lb)
    b = _chunk_cumsum(lf, rc)
    rem = _chunk_rcumsum(lf, rc) - lf
    return dict(lb=lb, sf=sf, f=f, key=key, sq=sq, qf=qf, b=b, rem=rem, eb=jnp.exp(b), er=jnp.exp(rem))


def _chunk_mask():
    r = lax.broadcasted_iota(jnp.int32, (HG_SUB, HG_SUB), 0)
    c = lax.broadcasted_iota(jnp.int32, (HG_SUB, HG_SUB), 1)
    return ((r // CHUNK) == (c // CHUNK)) & (c <= r)


def _hgrn_fwd(hp, lbl, wn):
    S = hp.shape[0]
    nT = S // HG_T

    def body(qb_ref, fb_ref, ib_ref, gb_ref, lbl_ref, wn_ref, yb_ref, o_ref, st_ref, ST, qt_s, kh_s, dec_s, oi_s):
        @pl.when(pl.program_id(0) == 0)
        def _():
            ST[...] = jnp.zeros_like(ST)

        rc = _row_in_chunk()
        for h in range(N_HH):
            sl = slice(HGRN_HD * h, HGRN_HD * (h + 1))
            p = _hgrn_prep(qb_ref[:, sl], fb_ref[:, sl], lbl_ref[:, sl], rc)
            qf, key, b = p["qf"], p["key"], p["b"]
            qt = qf * p["eb"]
            qt_s[:, sl] = qt.astype(bf16)
            kh_s[:, sl] = (key * p["er"]).astype(bf16)
            dec_s[:, sl] = jnp.exp(b + p["rem"])
            rng = jnp.max(-(b + p["rem"]))

            @pl.when(rng < SAFE_RANGE)
            def _():
                kp = (key * jnp.exp(-b)).astype(bf16)
                cmask = _chunk_mask()
                for j in range(HG_T // HG_SUB):
                    rs = slice(HG_SUB * j, HG_SUB * (j + 1))
                    sc = jnp.where(cmask, _dot_nt(qt[rs].astype(bf16), kp[rs]), 0.0).astype(bf16)
                    oi_s[rs, sl] = _dot(sc, ib_ref[rs, sl].astype(bf16))

            @pl.when(rng >= SAFE_RANGE)
            def _():
                v = ib_ref[:, sl]
                ones = jnp.ones((HGRN_HD, HGRN_HD), bf16)

                def lag(l, o):
                    e = jnp.exp(jnp.where(rc >= l, b - pltpu.roll(b, l, 0), NEG))
                    pr = qf * pltpu.roll(key, l, 0) * e
                    return o + _dot(pr.astype(bf16), ones) * pltpu.roll(v, l, 0)

                oi_s[:, sl] = lax.fori_loop(1, CHUNK, lag, _dot((qf * key).astype(bf16), ones) * v)

        def step(c, carry):
            rows = pl.ds(pl.multiple_of(c * CHUNK, CHUNK), CHUNK)
            row0 = pl.ds(pl.multiple_of(c * CHUNK, CHUNK), 1)
            for h in range(N_HH):
                sl = slice(HGRN_HD * h, HGRN_HD * (h + 1))
                stv = ST[h]
                st_ref[c, sl, :] = stv
                oi_s[rows, sl] += _dot_nt(qt_s[rows, sl], stv.astype(bf16))
                ST[h] = stv * dec_s[row0, sl] + _dot_tn(ib_ref[rows, sl].astype(bf16), kh_s[rows, sl])
            return carry

        lax.fori_loop(0, HG_T // CHUNK, step, 0, unroll=True)

        for h in range(N_HH):
            sl = slice(HGRN_HD * h, HGRN_HD * (h + 1))
            o = oi_s[:, sl]
            o_ref[:, sl] = o
            on = o * lax.rsqrt(jnp.mean(o * o, axis=-1, keepdims=True) + EPS)
            g = gb_ref[:, sl]
            yb_ref[:, sl] = on * wn_ref[:, sl] * (g * _sigmoid(g))

    col = lambda c: pl.BlockSpec((HG_T, HGRN_W), lambda i: (i, c))
    tile = pl.BlockSpec((HG_T, HGRN_W), lambda i: (i, 0))
    whole = lambda a: pl.BlockSpec(a.shape, lambda i: (0, 0))
    return pl.pallas_call(
        body, name="hgrn_fwd", grid=(nT,),
        in_specs=[col(0), col(1), col(2), col(3), whole(lbl), whole(wn)],
        out_specs=[tile, tile, pl.BlockSpec((HG_T // CHUNK, HGRN_W, HGRN_HD), lambda i: (i, 0, 0))],
        out_shape=[jax.ShapeDtypeStruct((S, HGRN_W), f32), jax.ShapeDtypeStruct((S, HGRN_W), f32),
                   jax.ShapeDtypeStruct((S // CHUNK, HGRN_W, HGRN_HD), f32)],
        scratch_shapes=[pltpu.VMEM((N_HH, HGRN_HD, HGRN_HD), f32), pltpu.VMEM((HG_T, HGRN_W), bf16),
                        pltpu.VMEM((HG_T, HGRN_W), bf16), pltpu.VMEM((HG_T, HGRN_W), f32), pltpu.VMEM((HG_T, HGRN_W), f32)],
        compiler_params=_cp("arbitrary"),
    )(hp, hp, hp, hp, lbl, wn)


def _hgrn_bwd(hp, lbl, wn, o_sav, states, dmix, rider=None):
    S = hp.shape[0]
    nT = S // HG_T

    def body(qb_ref, fb_ref, ib_ref, gb_ref, lbl_ref, wn_ref, o_ref, st_ref, dy_ref,
             dq_ref, df_ref, di_ref, dg_ref, gwn_ref, glb_ref,
             DST, qt_s, kh_s, dec_s, do_s, dqt_s, dkh_s, dbl_s, dvi_s, dqi_s, dki_s, dbi_s):
        @pl.when(pl.program_id(0) == 0)
        def _():
            DST[...] = jnp.zeros_like(DST)
            gwn_ref[...] = jnp.zeros_like(gwn_ref)
            glb_ref[...] = jnp.zeros_like(glb_ref)

        rc = _row_in_chunk()
        preps = []
        for h in range(N_HH):
            sl = slice(HGRN_HD * h, HGRN_HD * (h + 1))
            p = _hgrn_prep(qb_ref[:, sl], fb_ref[:, sl], lbl_ref[:, sl], rc)
            preps.append(p)
            qf, key, b = p["qf"], p["key"], p["b"]
            v = ib_ref[:, sl]
            o = o_ref[:, sl]
            rinv = lax.rsqrt(jnp.mean(o * o, axis=-1, keepdims=True) + EPS)
            on = o * rinv
            g = gb_ref[:, sl]
            sgm = _sigmoid(g)
            silu_g = g * sgm
            dy = dy_ref[:, sl]
            wn_v = wn_ref[:, sl]
            gwn_ref[:, sl] += jnp.sum(dy * on * silu_g, axis=0, keepdims=True)
            dg_ref[:, sl] = (dy * on * wn_v * (sgm * (1.0 + g * (1.0 - sgm)))).astype(bf16)
            t1 = dy * wn_v * silu_g
            do = rinv * (t1 - on * jnp.mean(t1 * on, axis=-1, keepdims=True))
            do_s[:, sl] = do.astype(bf16)
            qt = qf * p["eb"]
            qt_s[:, sl] = qt.astype(bf16)
            kh_s[:, sl] = (key * p["er"]).astype(bf16)
            dec_s[:, sl] = jnp.exp(b + p["rem"])
            rng = jnp.max(-(b + p["rem"]))

            @pl.when(rng < SAFE_RANGE)
            def _():
                einv = jnp.exp(-b)
                kp = (key * einv).astype(bf16)
                cmask = _chunk_mask()
                for j in range(HG_T // HG_SUB):
                    rs = slice(HG_SUB * j, HG_SUB * (j + 1))
                    qtb, dob, vb = qt[rs].astype(bf16), do[rs].astype(bf16), v[rs].astype(bf16)
                    sc = jnp.where(cmask, _dot_nt(qtb, kp[rs]), 0.0).astype(bf16)
                    dsc = jnp.where(cmask, _dot_nt(dob, vb), 0.0).astype(bf16)
                    dqp = _dot(dsc, kp[rs])
                    dkp = _dot_tn(dsc, qtb)
                    dvi_s[rs, sl] = _dot_tn(sc, dob)
                    dqi_s[rs, sl] = dqp * p["eb"][rs]
                    dki_s[rs, sl] = dkp * einv[rs]
                    dbi_s[rs, sl] = dqp * qtb.astype(f32) - dkp * kp[rs].astype(f32)

            @pl.when(rng >= SAFE_RANGE)
            def _():
                ones = jnp.ones((HGRN_HD, HGRN_HD), bf16)

                def lag(l, carry):
                    dqf, dkey, db, dv = carry
                    e = jnp.exp(jnp.where(rc >= l, b - pltpu.roll(b, l, 0), NEG))
                    ks, vs, qe = pltpu.roll(key, l, 0), pltpu.roll(v, l, 0), qf * e
                    pr = qe * ks
                    rl = _dot(pr.astype(bf16), ones)
                    drl = jnp.where(rc >= l, _dot((do * vs).astype(bf16), ones), 0.0)
                    gl = drl * pr
                    back = HG_T - l
                    return (dqf + drl * ks * e, dkey + pltpu.roll(drl * qe, back, 0), db + gl - pltpu.roll(gl, back, 0),
                            dv + pltpu.roll(rl * do, back, 0))

                rl0 = _dot((qf * key).astype(bf16), ones)
                drl0 = _dot((do * v).astype(bf16), ones)
                dqf, dkey, db, dv = lax.fori_loop(1, CHUNK, lag, (drl0 * key, drl0 * qf, jnp.zeros((HG_T, HGRN_HD), f32), rl0 * do))
                dvi_s[:, sl] = dv
                dqi_s[:, sl] = dqf
                dki_s[:, sl] = dkey
                dbi_s[:, sl] = db

        def step(k, carry):
            c = HG_T // CHUNK - 1 - k
            rows = pl.ds(pl.multiple_of(c * CHUNK, CHUNK), CHUNK)
            row0 = pl.ds(pl.multiple_of(c * CHUNK, CHUNK), 1)
            for h in range(N_HH):
                sl = slice(HGRN_HD * h, HGRN_HD * (h + 1))
                stp = st_ref[c, sl, :]
                dst = DST[h]
                dstb = dst.astype(bf16)
                dob = do_s[rows, sl]
                khb = kh_s[rows, sl]
                dec = dec_s[row0, sl]
                dqt_s[rows, sl] = _dot(dob, stp.astype(bf16))
                dkh = _dot(ib_ref[rows, sl].astype(bf16), dstb)
                dkh_s[rows, sl] = dkh
                dvi_s[rows, sl] += _dot_nt(khb, dstb)
                dbl = jnp.sum(dst * stp, axis=0, keepdims=True) * dec + jnp.sum(dkh * khb.astype(f32), axis=0, keepdims=True)
                dbl_s[rows, sl] = jnp.broadcast_to(dbl, (CHUNK, HGRN_HD))
                DST[h] = dst * dec + _dot_tn(dob, qt_s[rows, sl])
            return carry

        lax.fori_loop(0, HG_T // CHUNK, step, 0, unroll=True)

        for h in range(N_HH):
            sl = slice(HGRN_HD * h, HGRN_HD * (h + 1))
            qb = qb_ref[:, sl]
            p = preps[h]
            sf, sq, lb = p["sf"], p["sq"], p["lb"]
            dqt, dkh = dqt_s[:, sl], dkh_s[:, sl]
            dqf = dqt * p["eb"] + dqi_s[:, sl]
            dkey = dkh * p["er"] + dki_s[:, sl]
            db = dqt * (p["qf"] * p["eb"]) - dkh * (p["key"] * p["er"]) + jnp.where(rc == CHUNK - 1, dbl_s[:, sl], 0.0) + dbi_s[:, sl]
            df = _chunk_rcumsum(db, rc) / p["f"] - dkey
            df_ref[:, sl] = (df * (1.0 - lb) * sf * (1.0 - sf)).astype(bf16)
            glb_ref[:, sl] += jnp.sum(df * (1.0 - sf), axis=0, keepdims=True)
            dq_ref[:, sl] = (dqf * (sq * (1.0 + qb * (1.0 - sq)))).astype(bf16)
            di_ref[:, sl] = dvi_s[:, sl].astype(bf16)

    rev = lambda i: nT - 1 - i
    col = lambda c: pl.BlockSpec((HG_T, HGRN_W), lambda i: (rev(i), c))
    tile = pl.BlockSpec((HG_T, HGRN_W), lambda i: (rev(i), 0))
    whole = lambda a: pl.BlockSpec(a.shape, lambda i: (0, 0))
    vec = pl.BlockSpec((1, HGRN_W), lambda i: (0, 0))
    tb = lambda: pltpu.VMEM((HG_T, HGRN_W), bf16)
    tf = lambda: pltpu.VMEM((HG_T, HGRN_W), f32)
    call = dict(in_specs=[col(0), col(1), col(2), col(3), whole(lbl), whole(wn), tile,
                          pl.BlockSpec((HG_T // CHUNK, HGRN_W, HGRN_HD), lambda i: (rev(i), 0, 0)),
                          pl.BlockSpec((HG_T, HGRN_W), lambda i: (rev(i), 1))],
                out_specs=[tile, tile, tile, tile, vec, vec],
                out_shape=[jax.ShapeDtypeStruct((S, HGRN_W), bf16)] * 4 + [jax.ShapeDtypeStruct((1, HGRN_W), f32)] * 2,
                scratch_shapes=[pltpu.VMEM((N_HH, HGRN_HD, HGRN_HD), f32), tb(), tb(), tf(), tb(), tf(), tf(), tf(), tf(), tf(),
                                tf(), tf()])
    call, body, more = _ride(call, rider, body, lambda: pl.program_id(0), nT, 9, 6, 12)
    return pl.pallas_call(body, name="hgrn_bwd", grid=(nT,), compiler_params=_cp("arbitrary"), **call)(
        hp, hp, hp, hp, lbl, wn, o_sav, states, dmix, *more)


def _out_proj(x, ya, yb, wout, w2):
    S = x.shape[0]
    tm = 512

    def body(x_ref, ya_ref, yb_ref, w_ref, w2_ref, h1_ref, u2_ref, mix_ref):
        mixed = jnp.concatenate([ya_ref[...], yb_ref[...]], axis=1).astype(bf16)
        mix_ref[...] = mixed
        h1 = x_ref[...] + _dot(mixed, w_ref[...])
        h1_ref[...] = h1
        r = lax.rsqrt(jnp.mean(h1 * h1, axis=-1, keepdims=True) + EPS)
        u2_ref[...] = (h1 * r * w2_ref[...]).astype(bf16)

    row = lambda w: pl.BlockSpec((tm, w), lambda i: (i, 0))
    return pl.pallas_call(
        body, name="out_proj", grid=(S // tm,),
        in_specs=[row(D_MODEL), row(ATTN_W), row(HGRN_W), pl.BlockSpec((D_MODEL, D_MODEL), lambda i: (0, 0)),
                  pl.BlockSpec((1, D_MODEL), lambda i: (0, 0))],
        out_specs=[row(D_MODEL), row(D_MODEL), row(D_MODEL)],
        out_shape=[jax.ShapeDtypeStruct((S, D_MODEL), f32), jax.ShapeDtypeStruct((S, D_MODEL), bf16),
                   jax.ShapeDtypeStruct((S, D_MODEL), bf16)],
        compiler_params=_cp("arbitrary"),
    )(x, ya, yb, wout, w2)


def _gate_up(u2, wgu_g):
    S = u2.shape[0]
    w = 2 * FFN // N_DEV
    tm, tn = 512, 2 * w
    nj = FFN // tn

    def body(u_ref, wgg_ref, wug_ref, g_ref, up_ref, a_ref, wg_ref, wu_ref):
        @pl.when(pl.program_id(1) == 0)
        def _():
            for k in range(2):
                wg_ref[:, w * k:w * (k + 1)] = wgg_ref[k]
                wu_ref[:, w * k:w * (k + 1)] = wug_ref[k]

        u = u_ref[...]
        g = _dot(u, wg_ref[...])
        up = _dot(u, wu_ref[...])
        sg = _sigmoid(g)
        silu = g * sg
        g_ref[...] = silu.astype(bf16)
        up_ref[...] = (up * (sg + silu * (1.0 - sg))).astype(bf16)
        a_ref[...] = (silu * up).astype(bf16)

    out = pl.BlockSpec((tm, tn), lambda j, i: (i, j))
    wout = pl.BlockSpec((D_MODEL, tn), lambda j, i: (0, j))
    return pl.pallas_call(
        body, name="gate_up", grid=(nj, S // tm),
        in_specs=[pl.BlockSpec((tm, D_MODEL), lambda j, i: (i, 0)), pl.BlockSpec((2, D_MODEL, w), lambda j, i: (j, 0, 0)),
                  pl.BlockSpec((2, D_MODEL, w), lambda j, i: (j + nj, 0, 0))],
        out_specs=[out, out, out, wout, wout],
        out_shape=[jax.ShapeDtypeStruct((S, FFN), bf16)] * 3 + [jax.ShapeDtypeStruct((D_MODEL, FFN), bf16)] * 2,
        compiler_params=_cp("arbitrary", "arbitrary"),
    )(u2, wgu_g, wgu_g)


def _rms_bwd(dyw, hn, r):
    return r * (dyw - hn * jnp.mean(dyw * hn, axis=-1, keepdims=True))


def _down_loss(act, wdown, h1, tgt, w3):
    S = act.shape[0]
    tm = 256

    def body(a_ref, w_ref, h1_ref, t_ref, w3_ref, dh2_ref, loss_ref, gw3_ref):
        @pl.when(pl.program_id(0) == 0)
        def _():
            loss_ref[...] = jnp.zeros_like(loss_ref)
            gw3_ref[...] = jnp.zeros_like(gw3_ref)

        h2 = h1_ref[...] + _dot(a_ref[...], w_ref[...])
        r = lax.rsqrt(jnp.mean(h2 * h2, axis=-1, keepdims=True) + EPS)
        hn = h2 * r
        w3 = w3_ref[...]
        err = hn * w3 - t_ref[...]
        loss_ref[...] += (0.5 / D_MODEL) * jnp.sum(err * err)
        dy = err * (1.0 / D_MODEL)
        gw3_ref[...] += jnp.sum(dy * hn, axis=0, keepdims=True)
        dh2_ref[...] = _rms_bwd(dy * w3, hn, r)

    row = lambda w: pl.BlockSpec((tm, w), lambda i: (i, 0))
    return pl.pallas_call(
        body, name="down_loss", grid=(S // tm,),
        in_specs=[row(FFN), pl.BlockSpec((FFN, D_MODEL), lambda i: (0, 0)), row(D_MODEL), row(D_MODEL),
                  pl.BlockSpec((1, D_MODEL), lambda i: (0, 0))],
        out_specs=[row(D_MODEL), pl.BlockSpec((1, 128), lambda i: (0, 0)), pl.BlockSpec((1, D_MODEL), lambda i: (0, 0))],
        out_shape=[jax.ShapeDtypeStruct((S, D_MODEL), f32), jax.ShapeDtypeStruct((1, 128), f32),
                   jax.ShapeDtypeStruct((1, D_MODEL), f32)],
        compiler_params=_cp("arbitrary"),
    )(act, wdown, h1, tgt, w3)


def _dact(dh2, wdown, silu, up_dsilu):
    S = dh2.shape[0]
    tm = 256

    def body(d_ref, w_ref, s_ref, u_ref, o_ref):
        da = _dot_nt(d_ref[...].astype(bf16), w_ref[...])
        o_ref[1] = (da * s_ref[...].astype(f32)).astype(bf16)
        o_ref[0] = (da * u_ref[...].astype(f32)).astype(bf16)

    row = lambda w: pl.BlockSpec((tm, w), lambda i: (i, 0))
    return pl.pallas_call(
        body, name="dact", grid=(S // tm,),
        in_specs=[row(D_MODEL), pl.BlockSpec((FFN, D_MODEL), lambda i: (0, 0)), row(FFN), row(FFN)],
        out_specs=pl.BlockSpec((2, tm, FFN), lambda i: (0, i, 0)),
        out_shape=jax.ShapeDtypeStruct((2, S, FFN), bf16),
        compiler_params=_cp("arbitrary"),
    )(dh2, wdown, silu, up_dsilu)


def _dgu(dgu2, wgate, wup, h1, w2, dh2, wout, rider=None):
    S = dgu2.shape[1]
    tm = 256

    def body(d_ref, wg_ref, wu_ref, h1_ref, w2_ref, dh2_ref, wo_ref, dh1_ref, gw2_ref, dmix_ref):
        @pl.when(pl.program_id(0) == 0)
        def _():
            gw2_ref[...] = jnp.zeros_like(gw2_ref)

        du2 = _dot_nt(d_ref[0], wg_ref[...]) + _dot_nt(d_ref[1], wu_ref[...])
        h1 = h1_ref[...]
        r = lax.rsqrt(jnp.mean(h1 * h1, axis=-1, keepdims=True) + EPS)
        hn = h1 * r
        gw2_ref[...] += jnp.sum(du2 * hn, axis=0, keepdims=True)
        dh1 = dh2_ref[...] + _rms_bwd(du2 * w2_ref[...], hn, r)
        dh1_ref[...] = dh1
        dmix_ref[...] = _dot_nt(dh1.astype(bf16), wo_ref[...])

    row = lambda w: pl.BlockSpec((tm, w), lambda i: (i, 0))
    call = dict(in_specs=[pl.BlockSpec((2, tm, FFN), lambda i: (0, i, 0)), pl.BlockSpec((D_MODEL, FFN), lambda i: (0, 0)),
                          pl.BlockSpec((D_MODEL, FFN), lambda i: (0, 0)), row(D_MODEL),
                          pl.BlockSpec((1, D_MODEL), lambda i: (0, 0)), row(D_MODEL),
                          pl.BlockSpec((D_MODEL, D_MODEL), lambda i: (0, 0))],
                out_specs=[row(D_MODEL), pl.BlockSpec((1, D_MODEL), lambda i: (0, 0)), row(D_MODEL)],
                out_shape=[jax.ShapeDtypeStruct((S, D_MODEL), f32), jax.ShapeDtypeStruct((1, D_MODEL), f32),
                           jax.ShapeDtypeStruct((S, D_MODEL), f32)], scratch_shapes=[])
    call, body, more = _ride(call, rider, body, lambda: pl.program_id(0), S // tm, 7, 3, 0)
    return pl.pallas_call(body, name="dgu", grid=(S // tm,), compiler_params=_cp("arbitrary"), **call)(
        dgu2, wgate, wup, h1, w2, dh2, wout, *more)


def _din(dq, dk, dv, dhq, dhf, dhi, dhg, cos_t, sg_t, win, x, w1, dh1):
    S = x.shape[0]
    tm = 256

    def body(dq_ref, dk_ref, dv_ref, dhq_ref, dhf_ref, dhi_ref, dhg_ref, cos_ref, sg_ref, w_ref, x_ref, w1_ref, dh1_ref,
             dp_ref, gx_ref, gw1_ref):
        @pl.when(pl.program_id(0) == 0)
        def _():
            gw1_ref[...] = jnp.zeros_like(gw1_ref)

        cosv, sgv = jnp.tile(cos_ref[...], (1, ATTN_W // 128)), jnp.tile(sg_ref[...], (1, ATTN_W // 128))
        unrope = lambda d: d * cosv - sgv * _swap_halves(d)
        parts = [(unrope(dq_ref[...]) * (HEAD_DIM ** -0.5)).astype(bf16), unrope(dk_ref[...]).astype(bf16),
                 dv_ref[...].astype(bf16), dhq_ref[...], dhf_ref[...], dhi_ref[...], dhg_ref[...]]
        du = jnp.zeros((tm, D_MODEL), f32)
        for j, pj in enumerate(parts):
            dp_ref[:, j * 512:(j + 1) * 512] = pj
            du = du + _dot_nt(pj, w_ref[:, j * 512:(j + 1) * 512])
        xv = x_ref[...]
        r = lax.rsqrt(jnp.mean(xv * xv, axis=-1, keepdims=True) + EPS)
        xn = xv * r
        gw1_ref[...] += jnp.sum(du * xn, axis=0, keepdims=True)
        gx_ref[...] = dh1_ref[...] + _rms_bwd(du * w1_ref[...], xn, r)

    row = lambda w: pl.BlockSpec((tm, w), lambda i: (i, 0))
    vec = pl.BlockSpec((1, D_MODEL), lambda i: (0, 0))
    return pl.pallas_call(
        body, name="din", grid=(S // tm,),
        in_specs=[row(512)] * 7 + [row(128), row(128), pl.BlockSpec((D_MODEL, IN_W), lambda i: (0, 0)), row(D_MODEL), vec,
                                   row(D_MODEL)],
        out_specs=[row(IN_W), row(D_MODEL), vec],
        out_shape=[jax.ShapeDtypeStruct((S, IN_W), bf16), jax.ShapeDtypeStruct((S, D_MODEL), f32),
                   jax.ShapeDtypeStruct((1, D_MODEL), f32)],
        compiler_params=_cp("arbitrary"),
    )(dq, dk, dv, dhq, dhf, dhi, dhg, cos_t, sg_t, win, x, w1, dh1)


def _gw(a, bs, tn, name, ts=2048):
    S, M = a.shape
    N = bs[0].shape[1]
    k = len(bs)

    def body(a_ref, *refs):
        @pl.when(pl.program_id(1) == 0)
        def _():
            for o_ref in refs[k:]:
                o_ref[...] = jnp.zeros_like(o_ref)

        at = a_ref[...].astype(bf16)
        for b_ref, o_ref in zip(refs[:k], refs[k:]):
            o_ref[...] += _dot_tn(at, b_ref[...].astype(bf16))

    return pl.pallas_call(
        body, name=name, grid=(N // tn, S // ts),
        in_specs=[pl.BlockSpec((ts, M), lambda j, s: (s, 0))] + [pl.BlockSpec((ts, tn), lambda j, s: (s, j))] * k,
        out_specs=[pl.BlockSpec((M, tn), lambda j, s: (0, j))] * k, out_shape=[jax.ShapeDtypeStruct((M, N), f32)] * k,
        compiler_params=_cp("arbitrary", "arbitrary"),
    )(a, *bs)


def _gw_by_owner(a, b3, w, name, ts):
    S, M = a.shape
    G, _, Ng = b3.shape
    tn = 2 * w
    per_group = Ng // tn
    n_s = S // ts

    def body(a_ref, b_ref, o_ref, acc):
        s = pl.program_id(1)

        @pl.when(s == 0)
        def _():
            acc[...] = jnp.zeros_like(acc)

        acc[...] += _dot_tn(a_ref[...].astype(bf16), b_ref[0].astype(bf16))

        @pl.when(s == n_s - 1)
        def _():
            o_ref[0] = acc[:, 0:w]
            o_ref[1] = acc[:, w:tn]

    return pl.pallas_call(
        body, name=name, grid=(G * per_group, n_s),
        in_specs=[pl.BlockSpec((ts, M), lambda j, s: (s, 0)),
                  pl.BlockSpec((1, ts, tn), lambda j, s: (j // per_group, s, j % per_group))],
        out_specs=pl.BlockSpec((2, M, w), lambda j, s: (j, 0, 0)), out_shape=jax.ShapeDtypeStruct((G * Ng // w, M, w), f32),
        scratch_shapes=[pltpu.VMEM((M, tn), f32)], compiler_params=_cp("arbitrary", "arbitrary"),
    )(a, b3)


MESH = pl.DeviceIdType.MESH
ANY = pl.BlockSpec(memory_space=pl.ANY)
VMEM_SPEC = pl.BlockSpec(memory_space=pltpu.VMEM)


def _pos():
    return lax.axis_index("x"), lax.axis_index("y"), lax.axis_index("c")


def _flip(v, bit):
    return 1 - v if bit else v


def _gather_rider(shards):
    n = len(shards)

    def parts(outs, scratch):
        send_sems, recv_sems, local_sems = scratch[n:]
        x, y, c = _pos()
        chips = [(1 - x, y), (x, 1 - y), (1 - x, 1 - y)]

        def copy(a, k, block, to, src=None):
            dst = outs[a].at[4 * block[0] + 2 * block[1] + block[2]]
            return pltpu.make_async_remote_copy(src_ref=dst if src is None else src, dst_ref=dst, send_sem=send_sems.at[a, k],
                                                recv_sem=recv_sems.at[a, k], device_id=to, device_id_type=MESH)

        bufs = scratch[:n]
        me, sibling = (x, y, c), (x, y, 1 - c)
        own = lambda a: pltpu.make_async_copy(bufs[a], outs[a].at[4 * x + 2 * y + c], local_sems.at[a])
        sent = lambda a: [copy(a, 0, me, sibling, src=bufs[a])] + [copy(a, 1 + j, me, (*chip, c), src=bufs[a])
                                                                   for j, chip in enumerate(chips)]
        passed = lambda a: [copy(a, 4 + j, (*chip, c), sibling) for j, chip in enumerate(chips)]
        landed = lambda a: [copy(a, 1 + j, (*chip, c), me) for j, chip in enumerate(chips)]
        from_sibling = lambda a: [copy(a, 0, sibling, me)] + [copy(a, 4 + j, (*chip, 1 - c), me) for j, chip in enumerate(chips)]
        return bufs, local_sems, own, sent, passed, landed, from_sibling

    def first(ins, outs, scratch):
        bufs, local_sems, own, sent, _, _, _ = parts(outs, scratch)
        loads = [pltpu.make_async_copy(ins[a], bufs[a], local_sems.at[a]) for a in range(n)]
        for ld in loads:
            ld.start()
        for a in range(n):
            loads[a].wait()
            own(a).start()
            for cp in sent(a):
                cp.start()

    def middle(ins, outs, scratch):
        _, _, _, _, passed, landed, _ = parts(outs, scratch)
        for a in range(n):
            for got, on in zip(landed(a), passed(a)):
                got.wait_recv()
                on.start()

    def last(ins, outs, scratch):
        _, _, own, sent, passed, _, from_sibling = parts(outs, scratch)
        for a in range(n):
            for cp in from_sibling(a):
                cp.wait_recv()
        for a in range(n):
            for cp in sent(a) + passed(a):
                cp.wait_send()
            own(a).wait()

    return _Rider(shards, [jax.ShapeDtypeStruct((N_DEV,) + s.shape, s.dtype) for s in shards],
                  [pltpu.VMEM(s.shape, s.dtype) for s in shards]
                  + [pltpu.SemaphoreType.DMA((n, 7)), pltpu.SemaphoreType.DMA((n, 7)), pltpu.SemaphoreType.DMA((n,))],
                  first, last, middle)


def _sibling_rider(grads):
    n = len(grads)

    def copies(g, got, scratch):
        send_sems, recv_sems = scratch
        x, y, c = _pos()
        return [pltpu.make_async_remote_copy(src_ref=g[a].at[2 * q + (1 - c)], dst_ref=got[a].at[q], send_sem=send_sems.at[a, q],
                                             recv_sem=recv_sems.at[a, q], device_id=(x, y, 1 - c), device_id_type=MESH)
                for a in range(n) for q in range(4)]

    def first(g, got, scratch):
        for cp in copies(g, got, scratch):
            cp.start()

    def last(g, got, scratch):
        for cp in copies(g, got, scratch):
            cp.wait()

    return _Rider(grads, [jax.ShapeDtypeStruct((4,) + g.shape[1:], g.dtype) for g in grads],
                  [pltpu.SemaphoreType.DMA((n, 4))] * 2, first, last)


def _chips_rider(sums):
    n = len(sums)

    def copies(s, out, scratch):
        send_sems, recv_sems = scratch
        x, y, c = _pos()
        cps = []
        for a in range(n):
            for f in (1, 2, 3):
                peer = (_flip(x, f >> 1), _flip(y, f & 1), c)
                cps.append(pltpu.make_async_remote_copy(
                    src_ref=s[a].at[2 * peer[0] + peer[1]], dst_ref=out[a].at[f - 1], send_sem=send_sems.at[a, f - 1],
                    recv_sem=recv_sems.at[a, f - 1], device_id=peer, device_id_type=MESH))
        return cps

    def first(s, out, scratch):
        for cp in copies(s, out, scratch):
            cp.start()

    def last(s, out, scratch):
        for cp in copies(s, out, scratch):
            cp.wait()

    return _Rider(sums, [jax.ShapeDtypeStruct((3,) + s.shape[1:], s.dtype) for s in sums],
                  [pltpu.SemaphoreType.DMA((n, 3))] * 2, first, last)


def _both(a, b):
    na = (len(a.ins), len(a.out_shapes), len(a.scratch))

    def split(fa, fb):
        def f(ins, outs, scratch):
            fa(ins[:na[0]], outs[:na[1]], scratch[:na[2]])
            fb(ins[na[0]:], outs[na[1]:], scratch[na[2]:])
        return f

    return _Rider(a.ins + b.ins, a.out_shapes + b.out_shapes, a.scratch + b.scratch, split(a.first, b.first), split(a.last, b.last))


def _alone(rider, name):
    ri, ro = len(rider.ins), len(rider.out_shapes)

    def body(*refs):
        theirs = (refs[:ri], refs[ri:ri + ro], refs[ri + ro:])
        rider.first(*theirs)
        if rider.middle is not None:
            rider.middle(*theirs)
        rider.last(*theirs)

    return pl.pallas_call(body, name=name, in_specs=[ANY] * ri, out_specs=[ANY] * ro, out_shape=rider.out_shapes,
                          scratch_shapes=rider.scratch)(*rider.ins)


def _gather_small(g_w1, g_w2, g_w3, g_lb, g_wn, loss):
    def body(w1_ref, w2_ref, w3_ref, lb_ref, wn_ref, loss_ref, out_ref, pk, send_sems, recv_sems):
        x, y, c = _pos()
        me = 4 * x + 2 * y + c
        pk[...] = jnp.zeros_like(pk)
        pk[0:1, :] = w1_ref[...]
        pk[1:2, :] = w2_ref[...]
        pk[2:3, :] = w3_ref[...]
        pk[3:4, 0:HGRN_W] = lb_ref[...]
        pk[3:4, HGRN_W:2 * HGRN_W] = wn_ref[...]
        pk[4:5, 0:128] = loss_ref[...]
        out_ref[me] = pk[...]
        sends, recvs = [], []
        for k in range(1, N_DEV):
            peer = (_flip(x, k >> 2), _flip(y, (k >> 1) & 1), _flip(c, k & 1))
            cp = pltpu.make_async_remote_copy(src_ref=pk, dst_ref=out_ref.at[me], send_sem=send_sems.at[k - 1],
                                              recv_sem=recv_sems.at[k - 1], device_id=peer, device_id_type=MESH)
            cp.start()
            sends.append(cp)
            recvs.append(pltpu.make_async_remote_copy(src_ref=pk, dst_ref=out_ref.at[4 * peer[0] + 2 * peer[1] + peer[2]],
                                                      send_sem=send_sems.at[k - 1], recv_sem=recv_sems.at[k - 1], device_id=peer,
                                                      device_id_type=MESH))
        for cp in recvs:
            cp.wait_recv()
        for cp in sends:
            cp.wait_send()

    return pl.pallas_call(
        body, name="gather_small", in_specs=[VMEM_SPEC] * 6, out_specs=VMEM_SPEC,
        out_shape=jax.ShapeDtypeStruct((N_DEV, 8, D_MODEL), f32),
        scratch_shapes=[pltpu.VMEM((8, D_MODEL), f32), pltpu.SemaphoreType.DMA((N_DEV - 1,)), pltpu.SemaphoreType.DMA((N_DEV - 1,))],
    )(g_w1, g_w2, g_w3, g_lb, g_wn, loss)


def _row_tile(r):
    return max(t for t in range(8, 257, 8) if r % t == 0)


def _add_sibling(core, g, got, name):
    _, r, c = got.shape
    tr = _row_tile(r)

    def body(core_ref, a_ref, b_ref, o_ref):
        o_ref[...] = (a_ref[...] + b_ref[...]).astype(bf16)

    blk = pl.BlockSpec((1, tr, c), lambda q, i, core_ref: (q, i, 0))
    return pl.pallas_call(
        body, name=name, out_shape=jax.ShapeDtypeStruct(got.shape, bf16),
        grid_spec=pltpu.PrefetchScalarGridSpec(
            num_scalar_prefetch=1, grid=(4, r // tr),
            in_specs=[pl.BlockSpec((1, tr, c), lambda q, i, core_ref: (2 * q + core_ref[0], i, 0)), blk], out_specs=blk),
        compiler_params=_cp("arbitrary", "arbitrary"))(core, g, got)


def _adamw(w, g, m, v):
    m = ADAM_B1 * m + (1.0 - ADAM_B1) * g
    v = ADAM_B2 * v + (1.0 - ADAM_B2) * (g * g)
    m_hat = m / (1.0 - ADAM_B1 ** ADAM_STEP)
    v_hat = v / (1.0 - ADAM_B2 ** ADAM_STEP)
    return -ADAM_LR * (m_hat / (jnp.sqrt(v_hat) + ADAM_EPS) + ADAM_WD * w), m, v


def _adam_shard(where, g, got, pieces, w, m, v, name):
    r, c = w.shape
    tr = _row_tile(r)

    def body(where_ref, g_ref, got_ref, p_ref, w_ref, m_ref, v_ref, g_out, d_out, m_out, v_out):
        gsum = g_ref[0] + got_ref[0]
        for f in range(3):
            gsum = gsum + p_ref[f].astype(f32)
        g_out[...] = gsum
        d_out[...], m_out[...], v_out[...] = _adamw(w_ref[...], gsum, m_ref[...], v_ref[...])

    blk = pl.BlockSpec((tr, c), lambda i, where_ref: (i, 0))
    return pl.pallas_call(
        body, name=name, out_shape=[jax.ShapeDtypeStruct((r, c), f32)] * 4,
        grid_spec=pltpu.PrefetchScalarGridSpec(
            num_scalar_prefetch=1, grid=(r // tr,),
            in_specs=[pl.BlockSpec((1, tr, c), lambda i, where_ref: (where_ref[0], i, 0)),
                      pl.BlockSpec((1, tr, c), lambda i, where_ref: (where_ref[1], i, 0)),
                      pl.BlockSpec((3, tr, c), lambda i, where_ref: (0, i, 0)), blk, blk, blk],
            out_specs=[blk] * 4),
        compiler_params=_cp("arbitrary"),
    )(where, g, got, pieces, w, m, v)


def _small_update(gath, params):
    def body(gath_ref, *refs):
        ins, outs = refs[:15], refs[15:]
        gs = gath_ref[0]
        for k in range(1, N_DEV):
            gs = gs + gath_ref[k]
        outs[0][...] = gs[4:5, 0:128]
        l0, l1 = ins[9][0:1, :], ins[9][1:2, :]
        lb = _sigmoid(l0 - l1)
        d0 = gs[3:4, 0:HGRN_W] * lb * (1.0 - lb)
        first_row = lax.broadcasted_iota(jnp.int32, (2, HGRN_W), 0) == 0
        grads = [gs[0:1, :], gs[1:2, :], gs[2:3, :], jnp.where(first_row, d0, -d0), gs[3:4, HGRN_W:2 * HGRN_W]]
        for i, g in enumerate(grads):
            w_ref, m_ref, v_ref = ins[3 * i:3 * i + 3]
            o = outs[1 + 4 * i:5 + 4 * i]
            o[0][...] = g
            o[1][...], o[2][...], o[3][...] = _adamw(w_ref[...], g, m_ref[...], v_ref[...])

    flat = [a for p in params for a in p]
    out_shape = [jax.ShapeDtypeStruct((1, 128), f32)] + [jax.ShapeDtypeStruct(p[0].shape, f32) for p in params for _ in range(4)]
    outs = pl.pallas_call(body, name="small_update", in_specs=[VMEM_SPEC] * 16, out_specs=[VMEM_SPEC] * 21, out_shape=out_shape)(gath, *flat)
    return outs[0], [outs[1 + 4 * i:5 + 4 * i] for i in range(5)]


def kernel(x, norm1_w, w_in, lb_logits, hgrn_norm_w, w_out, norm2_w, w_gate_up, w_down, final_norm_w, loss_target, m_norm1_w, m_w_in, m_lb_logits, m_hgrn_norm_w, m_w_out, m_norm2_w, m_w_gate_up, m_w_down, m_final_norm_w, v_norm1_w, v_w_in, v_lb_logits, v_hgrn_norm_w, v_w_out, v_norm2_w, v_w_gate_up, v_w_down, v_final_norm_w):
    row = lambda a: a.reshape(1, D_MODEL)
    ix, iy, ic = lax.axis_index("x"), lax.axis_index("y"), lax.axis_index("c")
    core = jnp.stack([ic]).astype(jnp.int32)
    where = jnp.stack([4 * ix + 2 * iy + ic, 2 * ix + iy]).astype(jnp.int32)
    xs, tgt, w3 = x[0], loss_target[0], row(final_norm_w)
    S = xs.shape[0]

    cos_t, sg_t, win_g = _rope_tables(S, _gather_rider([w_in[0].astype(bf16)]))
    u, qkv, hp, win = _in_proj(xs, norm1_w, win_g, cos_t, sg_t)
    ya, lse, wout_g, wgu_g, wdown_g = _attn_fwd(qkv, _gather_rider([w_out[0].astype(bf16), w_gate_up[0].astype(bf16),
                                                                     w_down[0].astype(bf16)]))
    wout = wout_g.reshape(D_MODEL, D_MODEL)
    wdown = wdown_g.reshape(FFN, D_MODEL)
    yb, o_sav, states = _hgrn_fwd(hp, lb_logits, hgrn_norm_w)
    h1, u2, mixed = _out_proj(xs, ya, yb, wout, norm2_w)
    silu, up_dsilu, act, wgate, wup = _gate_up(u2, wgu_g)
    dh2, loss_p, g_w3 = _down_loss(act, wdown, h1, tgt, w3)

    (g_wdown,) = _gw(act, [dh2], 512, "gw_down")
    dgu2 = _dact(dh2, wdown, silu, up_dsilu)
    early = [_gw_by_owner(u2, dgu2, 2 * FFN // N_DEV, "gw_gate_up", 2048), g_wdown.reshape(N_DEV, FFN // N_DEV, D_MODEL)]
    dh1, g_w2, dmix, *got_early = _dgu(dgu2, wgate, wup, h1, norm2_w, dh2, wout, _sibling_rider(early))
    sums_early = [_add_sibling(core, g, o, f"add_sibling_{i}") for i, (g, o) in enumerate(zip(early, got_early))]
    (g_wout,) = _gw(mixed, [dh1], 1024, "gw_out")
    mid = [g_wout.reshape(N_DEV, D_MODEL // N_DEV, D_MODEL)]
    dhq, dhf, dhi, dhg, g_wn, g_lb, *rode = _hgrn_bwd(hp, lb_logits, hgrn_norm_w, o_sav, states, dmix,
                                                      _both(_chips_rider(sums_early), _sibling_rider(mid)))
    pieces_early, got_mid = rode[:2], rode[2:]
    sums_mid = [_add_sibling(core, mid[0], got_mid[0], "add_sibling_2")]
    dq, dk, dv, *pieces_mid = _attn_bwd(qkv, ya, lse, dmix, _chips_rider(sums_mid))
    dproj, gx, g_w1 = _din(dq, dk, dv, dhq, dhf, dhi, dhg, cos_t, sg_t, win, xs, norm1_w, dh1)
    late = [_gw_by_owner(u, dproj[None], IN_W // N_DEV, "gw_in", 2048)]
    got_late = _alone(_sibling_rider(late), "reduce_sibling")
    sums_late = [_add_sibling(core, late[0], got_late[0], "add_sibling_3")]
    pieces_late = _alone(_chips_rider(sums_late), "reduce_chips")

    grads = [late[0], mid[0], early[0], early[1]]
    got = [got_late[0], got_mid[0], got_early[0], got_early[1]]
    pieces = [pieces_late[0], pieces_mid[0], pieces_early[0], pieces_early[1]]
    shards = [w_in[0], w_out[0], w_gate_up[0], w_down[0]]
    moms = [(m_w_in[0], v_w_in[0]), (m_w_out[0], v_w_out[0]), (m_w_gate_up[0], v_w_gate_up[0]), (m_w_down[0], v_w_down[0])]
    big = [_adam_shard(where, g, o, p, w, m, v, f"adam_{i}")
           for i, (g, o, p, w, (m, v)) in enumerate(zip(grads, got, pieces, shards, moms))]
    big = [[a[None] for a in four] for four in big]

    gath = _gather_small(g_w1, g_w2, g_w3, g_lb, g_wn, loss_p)
    params = [(norm1_w, m_norm1_w, v_norm1_w), (norm2_w, m_norm2_w, v_norm2_w),
              (row(final_norm_w), row(m_final_norm_w), row(v_final_norm_w)),
              (lb_logits, m_lb_logits, v_lb_logits), (hgrn_norm_w, m_hgrn_norm_w, v_hgrn_norm_w)]
    loss, (s_w1, s_w2, s_w3, s_lb, s_wn) = _small_update(gath, params)
    s_w3 = [a.reshape(D_MODEL) for a in s_w3]
    per_w = [s_w1, big[0], s_lb, s_wn, big[1], s_w2, big[2], big[3], s_w3]
    return (loss[0, 0], gx[None], *[p[0] for p in per_w], *[p[1] for p in per_w], *[p[2] for p in per_w], *[p[3] for p in per_w])
```

```python
import jax
import jax.numpy as jnp
from jax import lax
from jax.experimental import pallas as pl
from jax.experimental.pallas import tpu as pltpu

f32, bf16 = jnp.float32, jnp.bfloat16

D_MODEL = 1024
ATTN_W = 512
HEAD_DIM = 64
ATTN_BLK = 128
DILATIONS = (1, 4, 16)
HGRN_W = 512
HGRN_HD = 128
CHUNK = 64
IN_W = 3 * ATTN_W + 4 * HGRN_W
FFN = 2816
EPS = 1e-6
ROPE_THETA = 10000.0
NEG = -1e30
N_DEV = 8
ADAM_LR, ADAM_B1, ADAM_B2, ADAM_EPS, ADAM_WD, ADAM_STEP = 0.001, 0.9, 0.999, 1e-08, 0.01, 10
VMEM_LIMIT = 56 * 1024 * 1024


def _cp(*sem):
    return pltpu.CompilerParams(dimension_semantics=sem, vmem_limit_bytes=VMEM_LIMIT)


def _dot(a, b):
    return jnp.dot(a, b, preferred_element_type=f32)


def _dot_nt(a, b):
    return lax.dot_general(a, b, (((1,), (1,)), ((), ())), preferred_element_type=f32)


def _dot_tn(a, b):
    return lax.dot_general(a, b, (((0,), (0,)), ((), ())), preferred_element_type=f32)


def _sigmoid(x):
    return 0.5 * jnp.tanh(0.5 * x) + 0.5


class _Rider:
    def __init__(self, ins, out_shapes, scratch, first, last, middle=None):
        self.ins, self.out_shapes, self.scratch = list(ins), list(out_shapes), list(scratch)
        self.first, self.middle, self.last = first, middle, last


def _ride(call, rider, body, step, n_steps, n_in, n_out, n_scratch):
    if rider is None:
        return call, body, []
    ri, ro = len(rider.ins), len(rider.out_shapes)
    any_spec = pl.BlockSpec(memory_space=pl.ANY)
    call = dict(call, in_specs=call["in_specs"] + [any_spec] * ri, out_specs=call["out_specs"] + [any_spec] * ro,
                out_shape=call["out_shape"] + rider.out_shapes, scratch_shapes=call["scratch_shapes"] + rider.scratch)

    def riding(*refs):
        a = n_in + ri
        b = a + n_out + ro
        mine = refs[:n_in] + refs[a:a + n_out] + refs[b:b + n_scratch]
        theirs = (refs[n_in:a], refs[a + n_out:b], refs[b + n_scratch:])
        t = step()

        @pl.when(t == 0)
        def _():
            rider.first(*theirs)

        body(*mine)
        if rider.middle is not None:
            @pl.when(t == n_steps // 2)
            def _():
                rider.middle(*theirs)

        @pl.when(t == n_steps - 1)
        def _():
            rider.last(*theirs)

    return call, riding, rider.ins


def _rope_tables(S, rider=None):
    half = HEAD_DIM // 2
    tm = 256
    inv_freq = jnp.tile(ROPE_THETA ** (-jnp.arange(half, dtype=f32) / half), 128 // half).reshape(1, 128)
    sign = jnp.tile(jnp.concatenate([-jnp.ones((half,), f32), jnp.ones((half,), f32)]), 128 // HEAD_DIM).reshape(1, 128)

    def body(inv_ref, sign_ref, cos_ref, sg_ref):
        pos = (lax.broadcasted_iota(jnp.int32, (tm, 128), 0) + pl.program_id(0) * tm).astype(f32)
        ang = pos * inv_ref[...]
        cos_ref[...] = jnp.cos(ang)
        sg_ref[...] = jnp.sin(ang) * sign_ref[...]

    vec = pl.BlockSpec((1, 128), lambda i: (0, 0))
    out = pl.BlockSpec((tm, 128), lambda i: (i, 0))
    call = dict(in_specs=[vec, vec], out_specs=[out, out], out_shape=[jax.ShapeDtypeStruct((S, 128), f32)] * 2, scratch_shapes=[])
    call, body, more = _ride(call, rider, body, lambda: pl.program_id(0), S // tm, 2, 2, 0)
    return pl.pallas_call(body, name="rope_tables", grid=(S // tm,), compiler_params=_cp("arbitrary"), **call)(inv_freq, sign, *more)


def _swap_halves(v):
    n = v.shape[1]
    lane = lax.broadcasted_iota(jnp.int32, v.shape, 1)
    return jnp.where((lane % HEAD_DIM) < HEAD_DIM // 2, pltpu.roll(v, n - HEAD_DIM // 2, 1), pltpu.roll(v, HEAD_DIM // 2, 1))


def _in_proj(x, w1, win_g, cos_t, sg_t):
    S = x.shape[0]
    tm = 256
    w = IN_W // N_DEV

    def body(x_ref, w1_ref, wg_ref, cos_ref, sg_ref, u_ref, qkv_ref, hp_ref, w_ref):
        @pl.when(pl.program_id(0) == 0)
        def _():
            for d in range(N_DEV):
                w_ref[:, w * d:w * (d + 1)] = wg_ref[d]

        xv = x_ref[...]
        r = lax.rsqrt(jnp.mean(xv * xv, axis=-1, keepdims=True) + EPS)
        u = (xv * r * w1_ref[...]).astype(bf16)
        u_ref[...] = u
        cosv, sgv = jnp.tile(cos_ref[...], (1, ATTN_W // 128)), jnp.tile(sg_ref[...], (1, ATTN_W // 128))
        for j in range(3):
            pj = _dot(u, w_ref[:, j * ATTN_W:(j + 1) * ATTN_W])
            if j < 2:
                pj = pj * cosv + _swap_halves(pj) * sgv
            if j == 0:
                pj = pj * (HEAD_DIM ** -0.5)
            qkv_ref[:, j * ATTN_W:(j + 1) * ATTN_W] = pj.astype(bf16)
        for j in range(4):
            lo = 3 * ATTN_W + j * HGRN_W
            hp_ref[:, j * HGRN_W:(j + 1) * HGRN_W] = _dot(u, w_ref[:, lo:lo + HGRN_W])

    return pl.pallas_call(
        body, name="in_proj", grid=(S // tm,),
        in_specs=[pl.BlockSpec((tm, D_MODEL), lambda i: (i, 0)), pl.BlockSpec((1, D_MODEL), lambda i: (0, 0)),
                  pl.BlockSpec((N_DEV, D_MODEL, w), lambda i: (0, 0, 0)),
                  pl.BlockSpec((tm, 128), lambda i: (i, 0)), pl.BlockSpec((tm, 128), lambda i: (i, 0))],
        out_specs=[pl.BlockSpec((tm, D_MODEL), lambda i: (i, 0)), pl.BlockSpec((tm, 3 * ATTN_W), lambda i: (i, 0)),
                   pl.BlockSpec((tm, 4 * HGRN_W), lambda i: (i, 0)), pl.BlockSpec((D_MODEL, IN_W), lambda i: (0, 0))],
        out_shape=[jax.ShapeDtypeStruct((S, D_MODEL), bf16), jax.ShapeDtypeStruct((S, 3 * ATTN_W), bf16),
                   jax.ShapeDtypeStruct((S, 4 * HGRN_W), f32), jax.ShapeDtypeStruct((D_MODEL, IN_W), bf16)],
        compiler_params=_cp("arbitrary"),
    )(x, w1, win_g, cos_t, sg_t)


def _head_masks():
    lane = lax.broadcasted_iota(jnp.int32, (ATTN_BLK, 128), 1)
    even = lane < HEAD_DIM
    return even, (even, jnp.logical_not(even))


def _pair_fwd(q2, k2, v2, bias):
    even, masks = _head_masks()
    outs, lses = [], []
    for e in range(2):
        qm = jnp.where(masks[e], q2, 0.0).astype(bf16)
        s = _dot_nt(qm, k2) + bias
        m = jnp.max(s, axis=-1, keepdims=True)
        pe = jnp.exp(s - m)
        lsum = jnp.sum(pe, axis=-1, keepdims=True)
        outs.append(_dot(pe.astype(bf16), v2) / lsum)
        lses.append(jnp.broadcast_to(m + jnp.log(lsum), (ATTN_BLK, 128)))
    return jnp.where(even, outs[0], outs[1]), jnp.where(even, lses[0], lses[1])


def _merge(y0, l0, y1, l1):
    mx = jnp.maximum(l0, l1)
    a, b = jnp.exp(l0 - mx), jnp.exp(l1 - mx)
    tot = a + b
    return (a * y0 + b * y1) / tot, mx + jnp.log(tot)


def _pair_bwd(q2, k2f, v2, dy2, lse2, delta2, bias):
    _, masks = _head_masks()
    k2 = k2f.astype(bf16)
    klane = lax.broadcasted_iota(jnp.int32, (2 * ATTN_BLK, 128), 1) < HEAD_DIM
    kmasks = (klane, jnp.logical_not(klane))
    dq2 = jnp.zeros((ATTN_BLK, 128), f32)
    pes, dss, qms, dyms = [], [], [], []
    for e in range(2):
        c0 = e * HEAD_DIM
        qm = jnp.where(masks[e], q2, 0.0).astype(bf16)
        km = jnp.where(kmasks[e], k2f, 0.0).astype(bf16)
        dym = jnp.where(masks[e], dy2, 0.0).astype(bf16)
        pe = jnp.exp(_dot_nt(qm, k2) + bias - lse2[:, c0:c0 + 1])
        ds = (pe * (_dot_nt(dym, v2) - delta2[:, c0:c0 + 1])).astype(bf16)
        dq2 = dq2 + _dot(ds, km)
        pes.append(pe.astype(bf16))
        dss.append(ds)
        qms.append(qm)
        dyms.append(dym)
    dv2 = _dot_tn(jnp.concatenate(pes, axis=0), jnp.concatenate(dyms, axis=0))
    dk2 = _dot_tn(jnp.concatenate(dss, axis=0), jnp.concatenate(qms, axis=0))
    return dq2, dk2, dv2


TOK = 2048


def _key_bias():
    qi = lax.broadcasted_iota(jnp.int32, (ATTN_BLK, 2 * ATTN_BLK), 0)
    kj = lax.broadcasted_iota(jnp.int32, (ATTN_BLK, 2 * ATTN_BLK), 1)
    delta = ATTN_BLK + qi - kj
    seen = (delta >= 0) & (delta <= ATTN_BLK)
    return jnp.where(seen, 0.0, NEG), jnp.where(seen & (kj >= ATTN_BLK), 0.0, NEG)


def _attn_fwd(qkv, rider=None):
    S = qkv.shape[0]
    nS = S // TOK

    def body(q_ref, kp_ref, kc_ref, vp_ref, vc_ref, y_ref, l_ref, qs, k2, v2, ay, al):
        n = pl.program_id(1)
        qs[...] = q_ref[...].astype(f32)
        k2[0:TOK] = kp_ref[...].astype(f32)
        k2[TOK:2 * TOK] = kc_ref[...].astype(f32)
        v2[0:TOK] = vp_ref[...].astype(f32)
        v2[TOK:2 * TOK] = vc_ref[...].astype(f32)
        bias_any, bias_first = _key_bias()

        def block(dil, r, b, step, last):
            start = r + pl.multiple_of(step * b, step)
            rows = pl.ds(start, ATTN_BLK, stride=dil) if dil > 1 else pl.ds(start, ATTN_BLK)
            keys = (pl.ds(TOK + start - step, 2 * ATTN_BLK, stride=dil) if dil > 1
                    else pl.ds(TOK + start - step, 2 * ATTN_BLK))
            bias = jnp.where((n == 0) & (b == 0), bias_first, bias_any)
            out, lse = _pair_fwd(qs[rows, :], k2[keys, :].astype(bf16), v2[keys, :].astype(bf16), bias)
            if dil < DILATIONS[-1]:
                out, lse = _merge(ay[rows, :], al[rows, :], out, lse)
            if last:
                y_ref[rows, :] = out
                l_ref[rows, :] = lse
            else:
                ay[rows, :] = out
                al[rows, :] = lse

        for dil in reversed(DILATIONS):
            def loop(i, carry, dil=dil):
                block(dil, i % dil, i // dil, ATTN_BLK * dil, dil == 1)
                return carry
            lax.fori_loop(0, TOK // ATTN_BLK, loop, 0, unroll=True)

    blk = (TOK, 128)
    cur = lambda c: pl.BlockSpec(blk, lambda p, n: (n, 4 * c + p))
    prv = lambda c: pl.BlockSpec(blk, lambda p, n: (jnp.maximum(n - 1, 0), 4 * c + p))
    out = pl.BlockSpec(blk, lambda p, n: (n, p))
    call = dict(in_specs=[cur(0), prv(1), cur(1), prv(2), cur(2)], out_specs=[out, out],
                out_shape=[jax.ShapeDtypeStruct((S, ATTN_W), f32)] * 2,
                scratch_shapes=[pltpu.VMEM(blk, f32), pltpu.VMEM((2 * TOK, 128), f32), pltpu.VMEM((2 * TOK, 128), f32),
                                pltpu.VMEM(blk, f32), pltpu.VMEM(blk, f32)])
    call, body, more = _ride(call, rider, body, lambda: pl.program_id(0) * nS + pl.program_id(1), (ATTN_W // 128) * nS, 5, 2, 5)
    return pl.pallas_call(body, name="attention_fwd", grid=(ATTN_W // 128, nS), compiler_params=_cp("arbitrary", "arbitrary"),
                          **call)(qkv, qkv, qkv, qkv, qkv, *more)


def _attn_bwd(qkv, ya, lse, dmix, rider=None):
    S = qkv.shape[0]
    nS = S // TOK

    def body(q_ref, kp_ref, kc_ref, vp_ref, vc_ref, y_ref, l_ref, dy_ref, dq_ref, dk_ref, dv_ref, qs, k2, v2, dk2, dv2, dqa, dl):
        n = pl.program_id(1)

        @pl.when(n == 0)
        def _():
            dk2[...] = jnp.zeros_like(dk2)
            dv2[...] = jnp.zeros_like(dv2)

        @pl.when(n < nS)
        def _():
            qs[...] = q_ref[...].astype(f32)
            k2[0:TOK] = kp_ref[...].astype(f32)
            k2[TOK:2 * TOK] = kc_ref[...].astype(f32)
            v2[0:TOK] = vp_ref[...].astype(f32)
            v2[TOK:2 * TOK] = vc_ref[...].astype(f32)
            li = lax.broadcasted_iota(jnp.int32, (128, 128), 0)
            lj = lax.broadcasted_iota(jnp.int32, (128, 128), 1)
            seg = jnp.where((li // HEAD_DIM) == (lj // HEAD_DIM), 1.0, 0.0).astype(bf16)
            bias_any, bias_first = _key_bias()

            def delta_rows(t, carry):
                rows = pl.ds(pl.multiple_of(256 * t, 256), 256)
                dyy = dy_ref[rows, :] * y_ref[rows, :]
                hi = dyy.astype(bf16)
                dl[rows, :] = _dot(hi, seg) + _dot((dyy - hi.astype(f32)).astype(bf16), seg)
                return carry

            lax.fori_loop(0, TOK // 256, delta_rows, 0)

            def block(dil, r, b, step, first_pattern, last):
                start = r + pl.multiple_of(step * b, step)
                rows = pl.ds(start, ATTN_BLK, stride=dil) if dil > 1 else pl.ds(start, ATTN_BLK)
                keys = (pl.ds(TOK + start - step, 2 * ATTN_BLK, stride=dil) if dil > 1
                        else pl.ds(TOK + start - step, 2 * ATTN_BLK))
                bias = jnp.where((n == 0) & (b == 0), bias_first, bias_any)
                dq2, dkk, dvv = _pair_bwd(qs[rows, :], k2[keys, :], v2[keys, :].astype(bf16), dy_ref[rows, :],
                                          l_ref[rows, :], dl[rows, :], bias)
                if last:
                    dq_ref[rows, :] = dqa[rows, :] + dq2
                elif first_pattern:
                    dqa[rows, :] = dq2
                else:
                    dqa[rows, :] += dq2
                dk2[keys, :] += dkk
                dv2[keys, :] += dvv

            for dil in reversed(DILATIONS):
                def loop(i, carry, dil=dil):
                    block(dil, i % dil, i // dil, ATTN_BLK * dil, dil == DILATIONS[-1], dil == 1)
                    return carry
                lax.fori_loop(0, TOK // ATTN_BLK, loop, 0, unroll=True)

        dk_ref[...] = dk2[0:TOK]
        dv_ref[...] = dv2[0:TOK]
        dk2[0:TOK] = dk2[TOK:2 * TOK]
        dv2[0:TOK] = dv2[TOK:2 * TOK]
        dk2[TOK:2 * TOK] = jnp.zeros((TOK, 128), f32)
        dv2[TOK:2 * TOK] = jnp.zeros((TOK, 128), f32)

    blk = (TOK, 128)
    cn = lambda n: jnp.minimum(n, nS - 1)
    pn = lambda n: jnp.clip(n - 1, 0, nS - 1)
    cur = lambda c: pl.BlockSpec(blk, lambda p, n: (cn(n), 4 * c + p))
    prv = lambda c: pl.BlockSpec(blk, lambda p, n: (pn(n), 4 * c + p))
    at_n = pl.BlockSpec(blk, lambda p, n: (cn(n), p))
    at_p = pl.BlockSpec(blk, lambda p, n: (pn(n), p))
    big = lambda: pltpu.VMEM((2 * TOK, 128), f32)
    call = dict(in_specs=[cur(0), prv(1), cur(1), prv(2), cur(2), at_n, at_n, at_n], out_specs=[at_n, at_p, at_p],
                out_shape=[jax.ShapeDtypeStruct((S, ATTN_W), f32)] * 3,
                scratch_shapes=[pltpu.VMEM(blk, f32), big(), big(), big(), big(), pltpu.VMEM(blk, f32), pltpu.VMEM(blk, f32)])
    call, body, more = _ride(call, rider, body, lambda: pl.program_id(0) * (nS + 1) + pl.program_id(1),
                             (ATTN_W // 128) * (nS + 1), 8, 3, 7)
    return pl.pallas_call(body, name="attention_bwd", grid=(ATTN_W // 128, nS + 1), compiler_params=_cp("arbitrary", "arbitrary"),
                          **call)(qkv, qkv, qkv, qkv, qkv, ya, lse, dmix, *more)


HG_T = 512
N_HH = HGRN_W // HGRN_HD
HG_SUB = 128
SAFE_RANGE = 75.0


def _row_in_chunk():
    return lax.broadcasted_iota(jnp.int32, (HG_T, HGRN_HD), 0) % CHUNK


def _chunk_cumsum(v, rc):
    k = 1
    while k < CHUNK:
        v = v + jnp.where(rc >= k, pltpu.roll(v, k, 0), 0.0)
        k *= 2
    return v


def _chunk_rcumsum(v, rc):
    k = 1
    while k < CHUNK:
        v = v + jnp.where(rc < CHUNK - k, pltpu.roll(v, HG_T - k, 0), 0.0)
        k *= 2
    return v


def _hgrn_gates(qb, fb, lb):
    sf = _sigmoid(fb)
    f = lb + (1.0 - lb) * sf
    sq = _sigmoid(qb)
    return sf, f, jnp.log(f), 1.0 - f, sq, qb * sq


def _hgrn_prep(qb, fb, lbl2, rc):
    lb = _sigmoid(lbl2[0:1, :] - lbl2[1:2, :])
    sf, f, lf, key, sq, qf = _hgrn_gates(qb, fb, lb)
    b = _chunk_cumsum(lf, rc)
    rem = _chunk_rcumsum(lf, rc) - lf
    return dict(lb=lb, sf=sf, f=f, key=key, sq=sq, qf=qf, b=b, rem=rem, eb=jnp.exp(b), er=jnp.exp(rem))


def _chunk_mask():
    r = lax.broadcasted_iota(jnp.int32, (HG_SUB, HG_SUB), 0)
    c = lax.broadcasted_iota(jnp.int32, (HG_SUB, HG_SUB), 1)
    return ((r // CHUNK) == (c // CHUNK)) & (c <= r)


def _hgrn_fwd(hp, lbl, wn):
    S = hp.shape[0]
    nT = S // HG_T

    def body(qb_ref, fb_ref, ib_ref, gb_ref, lbl_ref, wn_ref, yb_ref, o_ref, st_ref, ST, qt_s, kh_s, dec_s, oi_s):
        @pl.when(pl.program_id(0) == 0)
        def _():
            ST[...] = jnp.zeros_like(ST)

        rc = _row_in_chunk()
        for h in range(N_HH):
            sl = slice(HGRN_HD * h, HGRN_HD * (h + 1))
            p = _hgrn_prep(qb_ref[:, sl], fb_ref[:, sl], lbl_ref[:, sl], rc)
            qf, key, b = p["qf"], p["key"], p["b"]
            qt = qf * p["eb"]
            qt_s[:, sl] = qt.astype(bf16)
            kh_s[:, sl] = (key * p["er"]).astype(bf16)
            dec_s[:, sl] = jnp.exp(b + p["rem"])
            rng = jnp.max(-(b + p["rem"]))

            @pl.when(rng < SAFE_RANGE)
            def _():
                kp = (key * jnp.exp(-b)).astype(bf16)
                cmask = _chunk_mask()
                for j in range(HG_T // HG_SUB):
                    rs = slice(HG_SUB * j, HG_SUB * (j + 1))
                    sc = jnp.where(cmask, _dot_nt(qt[rs].astype(bf16), kp[rs]), 0.0).astype(bf16)
                    oi_s[rs, sl] = _dot(sc, ib_ref[rs, sl].astype(bf16))

            @pl.when(rng >= SAFE_RANGE)
            def _():
                v = ib_ref[:, sl]
                ones = jnp.ones((HGRN_HD, HGRN_HD), bf16)

                def lag(l, o):
                    e = jnp.exp(jnp.where(rc >= l, b - pltpu.roll(b, l, 0), NEG))
                    pr = qf * pltpu.roll(key, l, 0) * e
                    return o + _dot(pr.astype(bf16), ones) * pltpu.roll(v, l, 0)

                oi_s[:, sl] = lax.fori_loop(1, CHUNK, lag, _dot((qf * key).astype(bf16), ones) * v)

        def step(c, carry):
            rows = pl.ds(pl.multiple_of(c * CHUNK, CHUNK), CHUNK)
            row0 = pl.ds(pl.multiple_of(c * CHUNK, CHUNK), 1)
            for h in range(N_HH):
                sl = slice(HGRN_HD * h, HGRN_HD * (h + 1))
                stv = ST[h]
                st_ref[c, sl, :] = stv
                oi_s[rows, sl] += _dot_nt(qt_s[rows, sl], stv.astype(bf16))
                ST[h] = stv * dec_s[row0, sl] + _dot_tn(ib_ref[rows, sl].astype(bf16), kh_s[rows, sl])
            return carry

        lax.fori_loop(0, HG_T // CHUNK, step, 0, unroll=True)

        for h in range(N_HH):
            sl = slice(HGRN_HD * h, HGRN_HD * (h + 1))
            o = oi_s[:, sl]
            o_ref[:, sl] = o
            on = o * lax.rsqrt(jnp.mean(o * o, axis=-1, keepdims=True) + EPS)
            g = gb_ref[:, sl]
            yb_ref[:, sl] = on * wn_ref[:, sl] * (g * _sigmoid(g))

    col = lambda c: pl.BlockSpec((HG_T, HGRN_W), lambda i: (i, c))
    tile = pl.BlockSpec((HG_T, HGRN_W), lambda i: (i, 0))
    whole = lambda a: pl.BlockSpec(a.shape, lambda i: (0, 0))
    return pl.pallas_call(
        body, name="hgrn_fwd", grid=(nT,),
        in_specs=[col(0), col(1), col(2), col(3), whole(lbl), whole(wn)],
        out_specs=[tile, tile, pl.BlockSpec((HG_T // CHUNK, HGRN_W, HGRN_HD), lambda i: (i, 0, 0))],
        out_shape=[jax.ShapeDtypeStruct((S, HGRN_W), f32), jax.ShapeDtypeStruct((S, HGRN_W), f32),
                   jax.ShapeDtypeStruct((S // CHUNK, HGRN_W, HGRN_HD), f32)],
        scratch_shapes=[pltpu.VMEM((N_HH, HGRN_HD, HGRN_HD), f32), pltpu.VMEM((HG_T, HGRN_W), bf16),
                        pltpu.VMEM((HG_T, HGRN_W), bf16), pltpu.VMEM((HG_T, HGRN_W), f32), pltpu.VMEM((HG_T, HGRN_W), f32)],
        compiler_params=_cp("arbitrary"),
    )(hp, hp, hp, hp, lbl, wn)


def _hgrn_bwd(hp, lbl, wn, o_sav, states, dmix, rider=None):
    S = hp.shape[0]
    nT = S // HG_T

    def body(qb_ref, fb_ref, ib_ref, gb_ref, lbl_ref, wn_ref, o_ref, st_ref, dy_ref,
             dq_ref, df_ref, di_ref, dg_ref, gwn_ref, glb_ref,
             DST, qt_s, kh_s, dec_s, do_s, dqt_s, dkh_s, dbl_s, dvi_s, dqi_s, dki_s, dbi_s):
        @pl.when(pl.program_id(0) == 0)
        def _():
            DST[...] = jnp.zeros_like(DST)
            gwn_ref[...] = jnp.zeros_like(gwn_ref)
            glb_ref[...] = jnp.zeros_like(glb_ref)

        rc = _row_in_chunk()
        preps = []
        for h in range(N_HH):
            sl = slice(HGRN_HD * h, HGRN_HD * (h + 1))
            p = _hgrn_prep(qb_ref[:, sl], fb_ref[:, sl], lbl_ref[:, sl], rc)
            preps.append(p)
            qf, key, b = p["qf"], p["key"], p["b"]
            v = ib_ref[:, sl]
            o = o_ref[:, sl]
            rinv = lax.rsqrt(jnp.mean(o * o, axis=-1, keepdims=True) + EPS)
            on = o * rinv
            g = gb_ref[:, sl]
            sgm = _sigmoid(g)
            silu_g = g * sgm
            dy = dy_ref[:, sl]
            wn_v = wn_ref[:, sl]
            gwn_ref[:, sl] += jnp.sum(dy * on * silu_g, axis=0, keepdims=True)
            dg_ref[:, sl] = (dy * on * wn_v * (sgm * (1.0 + g * (1.0 - sgm)))).astype(bf16)
            t1 = dy * wn_v * silu_g
            do = rinv * (t1 - on * jnp.mean(t1 * on, axis=-1, keepdims=True))
            do_s[:, sl] = do.astype(bf16)
            qt = qf * p["eb"]
            qt_s[:, sl] = qt.astype(bf16)
            kh_s[:, sl] = (key * p["er"]).astype(bf16)
            dec_s[:, sl] = jnp.exp(b + p["rem"])
            rng = jnp.max(-(b + p["rem"]))

            @pl.when(rng < SAFE_RANGE)
            def _():
                einv = jnp.exp(-b)
                kp = (key * einv).astype(bf16)
                cmask = _chunk_mask()
                for j in range(HG_T // HG_SUB):
                    rs = slice(HG_SUB * j, HG_SUB * (j + 1))
                    qtb, dob, vb = qt[rs].astype(bf16), do[rs].astype(bf16), v[rs].astype(bf16)
                    sc = jnp.where(cmask, _dot_nt(qtb, kp[rs]), 0.0).astype(bf16)
                    dsc = jnp.where(cmask, _dot_nt(dob, vb), 0.0).astype(bf16)
                    dqp = _dot(dsc, kp[rs])
                    dkp = _dot_tn(dsc, qtb)
                    dvi_s[rs, sl] = _dot_tn(sc, dob)
                    dqi_s[rs, sl] = dqp * p["eb"][rs]
                    dki_s[rs, sl] = dkp * einv[rs]
                    dbi_s[rs, sl] = dqp * qtb.astype(f32) - dkp * kp[rs].astype(f32)

            @pl.when(rng >= SAFE_RANGE)
            def _():
                ones = jnp.ones((HGRN_HD, HGRN_HD), bf16)

                def lag(l, carry):
                    dqf, dkey, db, dv = carry
                    e = jnp.exp(jnp.where(rc >= l, b - pltpu.roll(b, l, 0), NEG))
                    ks, vs, qe = pltpu.roll(key, l, 0), pltpu.roll(v, l, 0), qf * e
                    pr = qe * ks
                    rl = _dot(pr.astype(bf16), ones)
                    drl = jnp.where(rc >= l, _dot((do * vs).astype(bf16), ones), 0.0)
                    gl = drl * pr
                    back = HG_T - l
                    return (dqf + drl * ks * e, dkey + pltpu.roll(drl * qe, back, 0), db + gl - pltpu.roll(gl, back, 0),
                            dv + pltpu.roll(rl * do, back, 0))

                rl0 = _dot((qf * key).astype(bf16), ones)
                drl0 = _dot((do * v).astype(bf16), ones)
                dqf, dkey, db, dv = lax.fori_loop(1, CHUNK, lag, (drl0 * key, drl0 * qf, jnp.zeros((HG_T, HGRN_HD), f32), rl0 * do))
                dvi_s[:, sl] = dv
                dqi_s[:, sl] = dqf
                dki_s[:, sl] = dkey
                dbi_s[:, sl] = db

        def step(k, carry):
            c = HG_T // CHUNK - 1 - k
            rows = pl.ds(pl.multiple_of(c * CHUNK, CHUNK), CHUNK)
            row0 = pl.ds(pl.multiple_of(c * CHUNK, CHUNK), 1)
            for h in range(N_HH):
                sl = slice(HGRN_HD * h, HGRN_HD * (h + 1))
                stp = st_ref[c, sl, :]
                dst = DST[h]
                dstb = dst.astype(bf16)
                dob = do_s[rows, sl]
                khb = kh_s[rows, sl]
                dec = dec_s[row0, sl]
                dqt_s[rows, sl] = _dot(dob, stp.astype(bf16))
                dkh = _dot(ib_ref[rows, sl].astype(bf16), dstb)
                dkh_s[rows, sl] = dkh
                dvi_s[rows, sl] += _dot_nt(khb, dstb)
                dbl = jnp.sum(dst * stp, axis=0, keepdims=True) * dec + jnp.sum(dkh * khb.astype(f32), axis=0, keepdims=True)
                dbl_s[rows, sl] = jnp.broadcast_to(dbl, (CHUNK, HGRN_HD))
                DST[h] = dst * dec + _dot_tn(dob, qt_s[rows, sl])
            return carry

        lax.fori_loop(0, HG_T // CHUNK, step, 0, unroll=True)

        for h in range(N_HH):
            sl = slice(HGRN_HD * h, HGRN_HD * (h + 1))
            qb = qb_ref[:, sl]
            p = preps[h]
            sf, sq, lb = p["sf"], p["sq"], p["lb"]
            dqt, dkh = dqt_s[:, sl], dkh_s[:, sl]
            dqf = dqt * p["eb"] + dqi_s[:, sl]
            dkey = dkh * p["er"] + dki_s[:, sl]
            db = dqt * (p["qf"] * p["eb"]) - dkh * (p["key"] * p["er"]) + jnp.where(rc == CHUNK - 1, dbl_s[:, sl], 0.0) + dbi_s[:, sl]
            df = _chunk_rcumsum(db, rc) / p["f"] - dkey
            df_ref[:, sl] = (df * (1.0 - lb) * sf * (1.0 - sf)).astype(bf16)
            glb_ref[:, sl] += jnp.sum(df * (1.0 - sf), axis=0, keepdims=True)
            dq_ref[:, sl] = (dqf * (sq * (1.0 + qb * (1.0 - sq)))).astype(bf16)
            di_ref[:, sl] = dvi_s[:, sl].astype(bf16)

    rev = lambda i: nT - 1 - i
    col = lambda c: pl.BlockSpec((HG_T, HGRN_W), lambda i: (rev(i), c))
    tile = pl.BlockSpec((HG_T, HGRN_W), lambda i: (rev(i), 0))
    whole = lambda a: pl.BlockSpec(a.shape, lambda i: (0, 0))
    vec = pl.BlockSpec((1, HGRN_W), lambda i: (0, 0))
    tb = lambda: pltpu.VMEM((HG_T, HGRN_W), bf16)
    tf = lambda: pltpu.VMEM((HG_T, HGRN_W), f32)
    call = dict(in_specs=[col(0), col(1), col(2), col(3), whole(lbl), whole(wn), tile,
                          pl.BlockSpec((HG_T // CHUNK, HGRN_W, HGRN_HD), lambda i: (rev(i), 0, 0)),
                          pl.BlockSpec((HG_T, HGRN_W), lambda i: (rev(i), 1))],
                out_specs=[tile, tile, tile, tile, vec, vec],
                out_shape=[jax.ShapeDtypeStruct((S, HGRN_W), bf16)] * 4 + [jax.ShapeDtypeStruct((1, HGRN_W), f32)] * 2,
                scratch_shapes=[pltpu.VMEM((N_HH, HGRN_HD, HGRN_HD), f32), tb(), tb(), tf(), tb(), tf(), tf(), tf(), tf(), tf(),
                                tf(), tf()])
    call, body, more = _ride(call, rider, body, lambda: pl.program_id(0), nT, 9, 6, 12)
    return pl.pallas_call(body, name="hgrn_bwd", grid=(nT,), compiler_params=_cp("arbitrary"), **call)(
        hp, hp, hp, hp, lbl, wn, o_sav, states, dmix, *more)


def _out_proj(x, ya, yb, wout, w2):
    S = x.shape[0]
    tm = 512

    def body(x_ref, ya_ref, yb_ref, w_ref, w2_ref, h1_ref, u2_ref, mix_ref):
        mixed = jnp.concatenate([ya_ref[...], yb_ref[...]], axis=1).astype(bf16)
        mix_ref[...] = mixed
        h1 = x_ref[...] + _dot(mixed, w_ref[...])
        h1_ref[...] = h1
        r = lax.rsqrt(jnp.mean(h1 * h1, axis=-1, keepdims=True) + EPS)
        u2_ref[...] = (h1 * r * w2_ref[...]).astype(bf16)

    row = lambda w: pl.BlockSpec((tm, w), lambda i: (i, 0))
    return pl.pallas_call(
        body, name="out_proj", grid=(S // tm,),
        in_specs=[row(D_MODEL), row(ATTN_W), row(HGRN_W), pl.BlockSpec((D_MODEL, D_MODEL), lambda i: (0, 0)),
                  pl.BlockSpec((1, D_MODEL), lambda i: (0, 0))],
        out_specs=[row(D_MODEL), row(D_MODEL), row(D_MODEL)],
        out_shape=[jax.ShapeDtypeStruct((S, D_MODEL), f32), jax.ShapeDtypeStruct((S, D_MODEL), bf16),
                   jax.ShapeDtypeStruct((S, D_MODEL), bf16)],
        compiler_params=_cp("arbitrary"),
    )(x, ya, yb, wout, w2)


def _gate_up(u2, wgu_g):
    S = u2.shape[0]
    w = 2 * FFN // N_DEV
    tm, tn = 512, 2 * w
    nj = FFN // tn

    def body(u_ref, wgg_ref, wug_ref, g_ref, up_ref, a_ref, wg_ref, wu_ref):
        @pl.when(pl.program_id(1) == 0)
        def _():
            for k in range(2):
                wg_ref[:, w * k:w * (k + 1)] = wgg_ref[k]
                wu_ref[:, w * k:w * (k + 1)] = wug_ref[k]

        u = u_ref[...]
        g = _dot(u, wg_ref[...])
        up = _dot(u, wu_ref[...])
        sg = _sigmoid(g)
        silu = g * sg
        g_ref[...] = silu.astype(bf16)
        up_ref[...] = (up * (sg + silu * (1.0 - sg))).astype(bf16)
        a_ref[...] = (silu * up).astype(bf16)

    out = pl.BlockSpec((tm, tn), lambda j, i: (i, j))
    wout = pl.BlockSpec((D_MODEL, tn), lambda j, i: (0, j))
    return pl.pallas_call(
        body, name="gate_up", grid=(nj, S // tm),
        in_specs=[pl.BlockSpec((tm, D_MODEL), lambda j, i: (i, 0)), pl.BlockSpec((2, D_MODEL, w), lambda j, i: (j, 0, 0)),
                  pl.BlockSpec((2, D_MODEL, w), lambda j, i: (j + nj, 0, 0))],
        out_specs=[out, out, out, wout, wout],
        out_shape=[jax.ShapeDtypeStruct((S, FFN), bf16)] * 3 + [jax.ShapeDtypeStruct((D_MODEL, FFN), bf16)] * 2,
        compiler_params=_cp("arbitrary", "arbitrary"),
    )(u2, wgu_g, wgu_g)


def _rms_bwd(dyw, hn, r):
    return r * (dyw - hn * jnp.mean(dyw * hn, axis=-1, keepdims=True))


def _down_loss(act, wdown, h1, tgt, w3):
    S = act.shape[0]
    tm = 256

    def body(a_ref, w_ref, h1_ref, t_ref, w3_ref, dh2_ref, loss_ref, gw3_ref):
        @pl.when(pl.program_id(0) == 0)
        def _():
            loss_ref[...] = jnp.zeros_like(loss_ref)
            gw3_ref[...] = jnp.zeros_like(gw3_ref)

        h2 = h1_ref[...] + _dot(a_ref[...], w_ref[...])
        r = lax.rsqrt(jnp.mean(h2 * h2, axis=-1, keepdims=True) + EPS)
        hn = h2 * r
        w3 = w3_ref[...]
        err = hn * w3 - t_ref[...]
        loss_ref[...] += (0.5 / D_MODEL) * jnp.sum(err * err)
        dy = err * (1.0 / D_MODEL)
        gw3_ref[...] += jnp.sum(dy * hn, axis=0, keepdims=True)
        dh2_ref[...] = _rms_bwd(dy * w3, hn, r)

    row = lambda w: pl.BlockSpec((tm, w), lambda i: (i, 0))
    return pl.pallas_call(
        body, name="down_loss", grid=(S // tm,),
        in_specs=[row(FFN), pl.BlockSpec((FFN, D_MODEL), lambda i: (0, 0)), row(D_MODEL), row(D_MODEL),
                  pl.BlockSpec((1, D_MODEL), lambda i: (0, 0))],
        out_specs=[row(D_MODEL), pl.BlockSpec((1, 128), lambda i: (0, 0)), pl.BlockSpec((1, D_MODEL), lambda i: (0, 0))],
        out_shape=[jax.ShapeDtypeStruct((S, D_MODEL), f32), jax.ShapeDtypeStruct((1, 128), f32),
                   jax.ShapeDtypeStruct((1, D_MODEL), f32)],
        compiler_params=_cp("arbitrary"),
    )(act, wdown, h1, tgt, w3)


def _dact(dh2, wdown, silu, up_dsilu):
    S = dh2.shape[0]
    tm = 256

    def body(d_ref, w_ref, s_ref, u_ref, o_ref):
        da = _dot_nt(d_ref[...].astype(bf16), w_ref[...])
        o_ref[1] = (da * s_ref[...].astype(f32)).astype(bf16)
        o_ref[0] = (da * u_ref[...].astype(f32)).astype(bf16)

    row = lambda w: pl.BlockSpec((tm, w), lambda i: (i, 0))
    return pl.pallas_call(
        body, name="dact", grid=(S // tm,),
        in_specs=[row(D_MODEL), pl.BlockSpec((FFN, D_MODEL), lambda i: (0, 0)), row(FFN), row(FFN)],
        out_specs=pl.BlockSpec((2, tm, FFN), lambda i: (0, i, 0)),
        out_shape=jax.ShapeDtypeStruct((2, S, FFN), bf16),
        compiler_params=_cp("arbitrary"),
    )(dh2, wdown, silu, up_dsilu)


def _dgu(dgu2, wgate, wup, h1, w2, dh2, wout, rider=None):
    S = dgu2.shape[1]
    tm = 256

    def body(d_ref, wg_ref, wu_ref, h1_ref, w2_ref, dh2_ref, wo_ref, dh1_ref, gw2_ref, dmix_ref):
        @pl.when(pl.program_id(0) == 0)
        def _():
            gw2_ref[...] = jnp.zeros_like(gw2_ref)

        du2 = _dot_nt(d_ref[0], wg_ref[...]) + _dot_nt(d_ref[1], wu_ref[...])
        h1 = h1_ref[...]
        r = lax.rsqrt(jnp.mean(h1 * h1, axis=-1, keepdims=True) + EPS)
        hn = h1 * r
        gw2_ref[...] += jnp.sum(du2 * hn, axis=0, keepdims=True)
        dh1 = dh2_ref[...] + _rms_bwd(du2 * w2_ref[...], hn, r)
        dh1_ref[...] = dh1
        dmix_ref[...] = _dot_nt(dh1.astype(bf16), wo_ref[...])

    row = lambda w: pl.BlockSpec((tm, w), lambda i: (i, 0))
    call = dict(in_specs=[pl.BlockSpec((2, tm, FFN), lambda i: (0, i, 0)), pl.BlockSpec((D_MODEL, FFN), lambda i: (0, 0)),
                          pl.BlockSpec((D_MODEL, FFN), lambda i: (0, 0)), row(D_MODEL),
                          pl.BlockSpec((1, D_MODEL), lambda i: (0, 0)), row(D_MODEL),
                          pl.BlockSpec((D_MODEL, D_MODEL), lambda i: (0, 0))],
                out_specs=[row(D_MODEL), pl.BlockSpec((1, D_MODEL), lambda i: (0, 0)), row(D_MODEL)],
                out_shape=[jax.ShapeDtypeStruct((S, D_MODEL), f32), jax.ShapeDtypeStruct((1, D_MODEL), f32),
                           jax.ShapeDtypeStruct((S, D_MODEL), f32)], scratch_shapes=[])
    call, body, more = _ride(call, rider, body, lambda: pl.program_id(0), S // tm, 7, 3, 0)
    return pl.pallas_call(body, name="dgu", grid=(S // tm,), compiler_params=_cp("arbitrary"), **call)(
        dgu2, wgate, wup, h1, w2, dh2, wout, *more)


def _din(dq, dk, dv, dhq, dhf, dhi, dhg, cos_t, sg_t, win, x, w1, dh1):
    S = x.shape[0]
    tm = 256

    def body(dq_ref, dk_ref, dv_ref, dhq_ref, dhf_ref, dhi_ref, dhg_ref, cos_ref, sg_ref, w_ref, x_ref, w1_ref, dh1_ref,
             dp_ref, gx_ref, gw1_ref):
        @pl.when(pl.program_id(0) == 0)
        def _():
            gw1_ref[...] = jnp.zeros_like(gw1_ref)

        cosv, sgv = jnp.tile(cos_ref[...], (1, ATTN_W // 128)), jnp.tile(sg_ref[...], (1, ATTN_W // 128))
        unrope = lambda d: d * cosv - sgv * _swap_halves(d)
        parts = [(unrope(dq_ref[...]) * (HEAD_DIM ** -0.5)).astype(bf16), unrope(dk_ref[...]).astype(bf16),
                 dv_ref[...].astype(bf16), dhq_ref[...], dhf_ref[...], dhi_ref[...], dhg_ref[...]]
        du = jnp.zeros((tm, D_MODEL), f32)
        for j, pj in enumerate(parts):
            dp_ref[:, j * 512:(j + 1) * 512] = pj
            du = du + _dot_nt(pj, w_ref[:, j * 512:(j + 1) * 512])
        xv = x_ref[...]
        r = lax.rsqrt(jnp.mean(xv * xv, axis=-1, keepdims=True) + EPS)
        xn = xv * r
        gw1_ref[...] += jnp.sum(du * xn, axis=0, keepdims=True)
        gx_ref[...] = dh1_ref[...] + _rms_bwd(du * w1_ref[...], xn, r)

    row = lambda w: pl.BlockSpec((tm, w), lambda i: (i, 0))
    vec = pl.BlockSpec((1, D_MODEL), lambda i: (0, 0))
    return pl.pallas_call(
        body, name="din", grid=(S // tm,),
        in_specs=[row(512)] * 7 + [row(128), row(128), pl.BlockSpec((D_MODEL, IN_W), lambda i: (0, 0)), row(D_MODEL), vec,
                                   row(D_MODEL)],
        out_specs=[row(IN_W), row(D_MODEL), vec],
        out_shape=[jax.ShapeDtypeStruct((S, IN_W), bf16), jax.ShapeDtypeStruct((S, D_MODEL), f32),
                   jax.ShapeDtypeStruct((1, D_MODEL), f32)],
        compiler_params=_cp("arbitrary"),
    )(dq, dk, dv, dhq, dhf, dhi, dhg, cos_t, sg_t, win, x, w1, dh1)


def _gw(a, bs, tn, name, ts=2048):
    S, M = a.shape
    N = bs[0].shape[1]
    k = len(bs)

    def body(a_ref, *refs):
        @pl.when(pl.program_id(1) == 0)
        def _():
            for o_ref in refs[k:]:
                o_ref[...] = jnp.zeros_like(o_ref)

        at = a_ref[...].astype(bf16)
        for b_ref, o_ref in zip(refs[:k], refs[k:]):
            o_ref[...] += _dot_tn(at, b_ref[...].astype(bf16))

    return pl.pallas_call(
        body, name=name, grid=(N // tn, S // ts),
        in_specs=[pl.BlockSpec((ts, M), lambda j, s: (s, 0))] + [pl.BlockSpec((ts, tn), lambda j, s: (s, j))] * k,
        out_specs=[pl.BlockSpec((M, tn), lambda j, s: (0, j))] * k, out_shape=[jax.ShapeDtypeStruct((M, N), f32)] * k,
        compiler_params=_cp("arbitrary", "arbitrary"),
    )(a, *bs)


def _gw_by_owner(a, b3, w, name, ts):
    S, M = a.shape
    G, _, Ng = b3.shape
    tn = 2 * w
    per_group = Ng // tn
    n_s = S // ts

    def body(a_ref, b_ref, o_ref, acc):
        s = pl.program_id(1)

        @pl.when(s == 0)
        def _():
            acc[...] = jnp.zeros_like(acc)

        acc[...] += _dot_tn(a_ref[...].astype(bf16), b_ref[0].astype(bf16))

        @pl.when(s == n_s - 1)
        def _():
            o_ref[0] = acc[:, 0:w]
            o_ref[1] = acc[:, w:tn]

    return pl.pallas_call(
        body, name=name, grid=(G * per_group, n_s),
        in_specs=[pl.BlockSpec((ts, M), lambda j, s: (s, 0)),
                  pl.BlockSpec((1, ts, tn), lambda j, s: (j // per_group, s, j % per_group))],
        out_specs=pl.BlockSpec((2, M, w), lambda j, s: (j, 0, 0)), out_shape=jax.ShapeDtypeStruct((G * Ng // w, M, w), f32),
        scratch_shapes=[pltpu.VMEM((M, tn), f32)], compiler_params=_cp("arbitrary", "arbitrary"),
    )(a, b3)


MESH = pl.DeviceIdType.MESH
ANY = pl.BlockSpec(memory_space=pl.ANY)
VMEM_SPEC = pl.BlockSpec(memory_space=pltpu.VMEM)


def _pos():
    return lax.axis_index("x"), lax.axis_index("y"), lax.axis_index("c")


def _flip(v, bit):
    return 1 - v if bit else v


def _gather_rider(shards):
    n = len(shards)

    def parts(outs, scratch):
        send_sems, recv_sems, local_sems = scratch[n:]
        x, y, c = _pos()
        chips = [(1 - x, y), (x, 1 - y), (1 - x, 1 - y)]

        def copy(a, k, block, to, src=None):
            dst = outs[a].at[4 * block[0] + 2 * block[1] + block[2]]
            return pltpu.make_async_remote_copy(src_ref=dst if src is None else src, dst_ref=dst, send_sem=send_sems.at[a, k],
                                                recv_sem=recv_sems.at[a, k], device_id=to, device_id_type=MESH)

        bufs = scratch[:n]
        me, sibling = (x, y, c), (x, y, 1 - c)
        own = lambda a: pltpu.make_async_copy(bufs[a], outs[a].at[4 * x + 2 * y + c], local_sems.at[a])
        sent = lambda a: [copy(a, 0, me, sibling, src=bufs[a])] + [copy(a, 1 + j, me, (*chip, c), src=bufs[a])
                                                                   for j, chip in enumerate(chips)]
        passed = lambda a: [copy(a, 4 + j, (*chip, c), sibling) for j, chip in enumerate(chips)]
        landed = lambda a: [copy(a, 1 + j, (*chip, c), me) for j, chip in enumerate(chips)]
        from_sibling = lambda a: [copy(a, 0, sibling, me)] + [copy(a, 4 + j, (*chip, 1 - c), me) for j, chip in enumerate(chips)]
        return bufs, local_sems, own, sent, passed, landed, from_sibling

    def first(ins, outs, scratch):
        bufs, local_sems, own, sent, _, _, _ = parts(outs, scratch)
        loads = [pltpu.make_async_copy(ins[a], bufs[a], local_sems.at[a]) for a in range(n)]
        for ld in loads:
            ld.start()
        for a in range(n):
            loads[a].wait()
            own(a).start()
            for cp in sent(a):
                cp.start()

    def middle(ins, outs, scratch):
        _, _, _, _, passed, landed, _ = parts(outs, scratch)
        for a in range(n):
            for got, on in zip(landed(a), passed(a)):
                got.wait_recv()
                on.start()

    def last(ins, outs, scratch):
        _, _, own, sent, passed, _, from_sibling = parts(outs, scratch)
        for a in range(n):
            for cp in from_sibling(a):
                cp.wait_recv()
        for a in range(n):
            for cp in sent(a) + passed(a):
                cp.wait_send()
            own(a).wait()

    return _Rider(shards, [jax.ShapeDtypeStruct((N_DEV,) + s.shape, s.dtype) for s in shards],
                  [pltpu.VMEM(s.shape, s.dtype) for s in shards]
                  + [pltpu.SemaphoreType.DMA((n, 7)), pltpu.SemaphoreType.DMA((n, 7)), pltpu.SemaphoreType.DMA((n,))],
                  first, last, middle)


def _sibling_rider(grads):
    n = len(grads)

    def copies(g, got, scratch):
        send_sems, recv_sems = scratch
        x, y, c = _pos()
        return [pltpu.make_async_remote_copy(src_ref=g[a].at[2 * q + (1 - c)], dst_ref=got[a].at[q], send_sem=send_sems.at[a, q],
                                             recv_sem=recv_sems.at[a, q], device_id=(x, y, 1 - c), device_id_type=MESH)
                for a in range(n) for q in range(4)]

    def first(g, got, scratch):
        for cp in copies(g, got, scratch):
            cp.start()

    def last(g, got, scratch):
        for cp in copies(g, got, scratch):
            cp.wait()

    return _Rider(grads, [jax.ShapeDtypeStruct((4,) + g.shape[1:], g.dtype) for g in grads],
                  [pltpu.SemaphoreType.DMA((n, 4))] * 2, first, last)


def _chips_rider(sums):
    n = len(sums)

    def copies(s, out, scratch):
        send_sems, recv_sems = scratch
        x, y, c = _pos()
        cps = []
        for a in range(n):
            for f in (1, 2, 3):
                peer = (_flip(x, f >> 1), _flip(y, f & 1), c)
                cps.append(pltpu.make_async_remote_copy(
                    src_ref=s[a].at[2 * peer[0] + peer[1]], dst_ref=out[a].at[f - 1], send_sem=send_sems.at[a, f - 1],
                    recv_sem=recv_sems.at[a, f - 1], device_id=peer, device_id_type=MESH))
        return cps

    def first(s, out, scratch):
        for cp in copies(s, out, scratch):
            cp.start()

    def last(s, out, scratch):
        for cp in copies(s, out, scratch):
            cp.wait()

    return _Rider(sums, [jax.ShapeDtypeStruct((3,) + s.shape[1:], s.dtype) for s in sums],
                  [pltpu.SemaphoreType.DMA((n, 3))] * 2, first, last)


def _add_and_send(g, got, name):
    _, r, c = got.shape

    def body(g_ref, got_ref, out_ref, a_buf, b_buf, s_buf, load_sems, send_sems, recv_sems):
        x, y, cc = _pos()
        copies = []
        for f in (1, 2, 3):
            peer = (_flip(x, f >> 1), _flip(y, f & 1), cc)
            qd = 2 * peer[0] + peer[1]
            mine = pltpu.make_async_copy(g_ref.at[2 * qd + cc], a_buf, load_sems.at[0])
            theirs = pltpu.make_async_copy(got_ref.at[qd], b_buf, load_sems.at[1])
            mine.start()
            theirs.start()
            mine.wait()
            theirs.wait()
            s_buf[f - 1] = (a_buf[...] + b_buf[...]).astype(bf16)
            cp = pltpu.make_async_remote_copy(src_ref=s_buf.at[f - 1], dst_ref=out_ref.at[f - 1], send_sem=send_sems.at[f - 1],
                                              recv_sem=recv_sems.at[f - 1], device_id=peer, device_id_type=MESH)
            cp.start()
            copies.append(cp)
        for cp in copies:
            cp.wait()

    return pl.pallas_call(
        body, name=name, in_specs=[ANY, ANY], out_specs=ANY, out_shape=jax.ShapeDtypeStruct((3, r, c), bf16),
        scratch_shapes=[pltpu.VMEM((r, c), f32), pltpu.VMEM((r, c), f32), pltpu.VMEM((3, r, c), bf16),
                        pltpu.SemaphoreType.DMA((2,)), pltpu.SemaphoreType.DMA((3,)), pltpu.SemaphoreType.DMA((3,))],
    )(g, got)


def _both(a, b):
    na = (len(a.ins), len(a.out_shapes), len(a.scratch))

    def split(fa, fb):
        def f(ins, outs, scratch):
            fa(ins[:na[0]], outs[:na[1]], scratch[:na[2]])
            fb(ins[na[0]:], outs[na[1]:], scratch[na[2]:])
        return f

    return _Rider(a.ins + b.ins, a.out_shapes + b.out_shapes, a.scratch + b.scratch, split(a.first, b.first), split(a.last, b.last))


def _alone(rider, name):
    ri, ro = len(rider.ins), len(rider.out_shapes)

    def body(*refs):
        theirs = (refs[:ri], refs[ri:ri + ro], refs[ri + ro:])
        rider.first(*theirs)
        if rider.middle is not None:
            rider.middle(*theirs)
        rider.last(*theirs)

    return pl.pallas_call(body, name=name, in_specs=[ANY] * ri, out_specs=[ANY] * ro, out_shape=rider.out_shapes,
                          scratch_shapes=rider.scratch)(*rider.ins)


def _gather_small(g_w1, g_w2, g_w3, g_lb, g_wn, loss):
    def body(w1_ref, w2_ref, w3_ref, lb_ref, wn_ref, loss_ref, out_ref, pk, send_sems, recv_sems):
        x, y, c = _pos()
        me = 4 * x + 2 * y + c
        pk[...] = jnp.zeros_like(pk)
        pk[0:1, :] = w1_ref[...]
        pk[1:2, :] = w2_ref[...]
        pk[2:3, :] = w3_ref[...]
        pk[3:4, 0:HGRN_W] = lb_ref[...]
        pk[3:4, HGRN_W:2 * HGRN_W] = wn_ref[...]
        pk[4:5, 0:128] = loss_ref[...]
        out_ref[me] = pk[...]
        sends, recvs = [], []
        for k in range(1, N_DEV):
            peer = (_flip(x, k >> 2), _flip(y, (k >> 1) & 1), _flip(c, k & 1))
            cp = pltpu.make_async_remote_copy(src_ref=pk, dst_ref=out_ref.at[me], send_sem=send_sems.at[k - 1],
                                              recv_sem=recv_sems.at[k - 1], device_id=peer, device_id_type=MESH)
            cp.start()
            sends.append(cp)
            recvs.append(pltpu.make_async_remote_copy(src_ref=pk, dst_ref=out_ref.at[4 * peer[0] + 2 * peer[1] + peer[2]],
                                                      send_sem=send_sems.at[k - 1], recv_sem=recv_sems.at[k - 1], device_id=peer,
                                                      device_id_type=MESH))
        for cp in recvs:
            cp.wait_recv()
        for cp in sends:
            cp.wait_send()

    return pl.pallas_call(
        body, name="gather_small", in_specs=[VMEM_SPEC] * 6, out_specs=VMEM_SPEC,
        out_shape=jax.ShapeDtypeStruct((N_DEV, 8, D_MODEL), f32),
        scratch_shapes=[pltpu.VMEM((8, D_MODEL), f32), pltpu.SemaphoreType.DMA((N_DEV - 1,)), pltpu.SemaphoreType.DMA((N_DEV - 1,))],
    )(g_w1, g_w2, g_w3, g_lb, g_wn, loss)


def _row_tile(r):
    return max(t for t in range(8, 257, 8) if r % t == 0)


def _add_sibling(core, g, got, name):
    _, r, c = got.shape
    tr = _row_tile(r)

    def body(core_ref, a_ref, b_ref, o_ref):
        o_ref[...] = (a_ref[...] + b_ref[...]).astype(bf16)

    blk = pl.BlockSpec((1, tr, c), lambda q, i, core_ref: (q, i, 0))
    return pl.pallas_call(
        body, name=name, out_shape=jax.ShapeDtypeStruct(got.shape, bf16),
        grid_spec=pltpu.PrefetchScalarGridSpec(
            num_scalar_prefetch=1, grid=(4, r // tr),
            in_specs=[pl.BlockSpec((1, tr, c), lambda q, i, core_ref: (2 * q + core_ref[0], i, 0)), blk], out_specs=blk),
        compiler_params=_cp("arbitrary", "arbitrary"))(core, g, got)


def _adamw(w, g, m, v):
    m = ADAM_B1 * m + (1.0 - ADAM_B1) * g
    v = ADAM_B2 * v + (1.0 - ADAM_B2) * (g * g)
    m_hat = m / (1.0 - ADAM_B1 ** ADAM_STEP)
    v_hat = v / (1.0 - ADAM_B2 ** ADAM_STEP)
    return -ADAM_LR * (m_hat / (jnp.sqrt(v_hat) + ADAM_EPS) + ADAM_WD * w), m, v


def _adam_shard(where, g, got, pieces, w, m, v, name):
    r, c = w.shape
    tr = _row_tile(r)

    def body(where_ref, g_ref, got_ref, p_ref, w_ref, m_ref, v_ref, g_out, d_out, m_out, v_out):
        gsum = g_ref[0] + got_ref[0]
        for f in range(3):
            gsum = gsum + p_ref[f].astype(f32)
        g_out[...] = gsum
        d_out[...], m_out[...], v_out[...] = _adamw(w_ref[...], gsum, m_ref[...], v_ref[...])

    blk = pl.BlockSpec((tr, c), lambda i, where_ref: (i, 0))
    return pl.pallas_call(
        body, name=name, out_shape=[jax.ShapeDtypeStruct((r, c), f32)] * 4,
        grid_spec=pltpu.PrefetchScalarGridSpec(
            num_scalar_prefetch=1, grid=(r // tr,),
            in_specs=[pl.BlockSpec((1, tr, c), lambda i, where_ref: (where_ref[0], i, 0)),
                      pl.BlockSpec((1, tr, c), lambda i, where_ref: (where_ref[1], i, 0)),
                      pl.BlockSpec((3, tr, c), lambda i, where_ref: (0, i, 0)), blk, blk, blk],
            out_specs=[blk] * 4),
        compiler_params=_cp("arbitrary"),
    )(where, g, got, pieces, w, m, v)


def _small_update(gath, params):
    def body(gath_ref, *refs):
        ins, outs = refs[:15], refs[15:]
        gs = gath_ref[0]
        for k in range(1, N_DEV):
            gs = gs + gath_ref[k]
        outs[0][...] = gs[4:5, 0:128]
        l0, l1 = ins[9][0:1, :], ins[9][1:2, :]
        lb = _sigmoid(l0 - l1)
        d0 = gs[3:4, 0:HGRN_W] * lb * (1.0 - lb)
        first_row = lax.broadcasted_iota(jnp.int32, (2, HGRN_W), 0) == 0
        grads = [gs[0:1, :], gs[1:2, :], gs[2:3, :], jnp.where(first_row, d0, -d0), gs[3:4, HGRN_W:2 * HGRN_W]]
        for i, g in enumerate(grads):
            w_ref, m_ref, v_ref = ins[3 * i:3 * i + 3]
            o = outs[1 + 4 * i:5 + 4 * i]
            o[0][...] = g
            o[1][...], o[2][...], o[3][...] = _adamw(w_ref[...], g, m_ref[...], v_ref[...])

    flat = [a for p in params for a in p]
    out_shape = [jax.ShapeDtypeStruct((1, 128), f32)] + [jax.ShapeDtypeStruct(p[0].shape, f32) for p in params for _ in range(4)]
    outs = pl.pallas_call(body, name="small_update", in_specs=[VMEM_SPEC] * 16, out_specs=[VMEM_SPEC] * 21, out_shape=out_shape)(gath, *flat)
    return outs[0], [outs[1 + 4 * i:5 + 4 * i] for i in range(5)]


def kernel(x, norm1_w, w_in, lb_logits, hgrn_norm_w, w_out, norm2_w, w_gate_up, w_down, final_norm_w, loss_target, m_norm1_w, m_w_in, m_lb_logits, m_hgrn_norm_w, m_w_out, m_norm2_w, m_w_gate_up, m_w_down, m_final_norm_w, v_norm1_w, v_w_in, v_lb_logits, v_hgrn_norm_w, v_w_out, v_norm2_w, v_w_gate_up, v_w_down, v_final_norm_w):
    row = lambda a: a.reshape(1, D_MODEL)
    ix, iy, ic = lax.axis_index("x"), lax.axis_index("y"), lax.axis_index("c")
    core = jnp.stack([ic]).astype(jnp.int32)
    where = jnp.stack([4 * ix + 2 * iy + ic, 2 * ix + iy]).astype(jnp.int32)
    xs, tgt, w3 = x[0], loss_target[0], row(final_norm_w)
    S = xs.shape[0]

    cos_t, sg_t, win_g = _rope_tables(S, _gather_rider([w_in[0].astype(bf16)]))
    u, qkv, hp, win = _in_proj(xs, norm1_w, win_g, cos_t, sg_t)
    ya, lse, wout_g, wgu_g, wdown_g = _attn_fwd(qkv, _gather_rider([w_out[0].astype(bf16), w_gate_up[0].astype(bf16),
                                                                     w_down[0].astype(bf16)]))
    wout = wout_g.reshape(D_MODEL, D_MODEL)
    wdown = wdown_g.reshape(FFN, D_MODEL)
    yb, o_sav, states = _hgrn_fwd(hp, lb_logits, hgrn_norm_w)
    h1, u2, mixed = _out_proj(xs, ya, yb, wout, norm2_w)
    silu, up_dsilu, act, wgate, wup = _gate_up(u2, wgu_g)
    dh2, loss_p, g_w3 = _down_loss(act, wdown, h1, tgt, w3)

    (g_wdown,) = _gw(act, [dh2], 512, "gw_down")
    dgu2 = _dact(dh2, wdown, silu, up_dsilu)
    early = [_gw_by_owner(u2, dgu2, 2 * FFN // N_DEV, "gw_gate_up", 2048), g_wdown.reshape(N_DEV, FFN // N_DEV, D_MODEL)]
    dh1, g_w2, dmix, *got_early = _dgu(dgu2, wgate, wup, h1, norm2_w, dh2, wout, _sibling_rider(early))
    sums_early = [_add_sibling(core, g, o, f"add_sibling_{i}") for i, (g, o) in enumerate(zip(early, got_early))]
    (g_wout,) = _gw(mixed, [dh1], 1024, "gw_out")
    mid = [g_wout.reshape(N_DEV, D_MODEL // N_DEV, D_MODEL)]
    dhq, dhf, dhi, dhg, g_wn, g_lb, *rode = _hgrn_bwd(hp, lb_logits, hgrn_norm_w, o_sav, states, dmix,
                                                      _both(_chips_rider(sums_early), _sibling_rider(mid)))
    pieces_early, got_mid = rode[:2], rode[2:]
    sums_mid = [_add_sibling(core, mid[0], got_mid[0], "add_sibling_2")]
    dq, dk, dv, *pieces_mid = _attn_bwd(qkv, ya, lse, dmix, _chips_rider(sums_mid))
    dproj, gx, g_w1 = _din(dq, dk, dv, dhq, dhf, dhi, dhg, cos_t, sg_t, win, xs, norm1_w, dh1)
    late = [_gw_by_owner(u, dproj[None], IN_W // N_DEV, "gw_in", 2048)]
    got_late = _alone(_sibling_rider(late), "reduce_sibling")
    pieces_late = [_add_and_send(late[0], got_late[0], "reduce_chips")]

    grads = [late[0], mid[0], early[0], early[1]]
    got = [got_late[0], got_mid[0], got_early[0], got_early[1]]
    pieces = [pieces_late[0], pieces_mid[0], pieces_early[0], pieces_early[1]]
    shards = [w_in[0], w_out[0], w_gate_up[0], w_down[0]]
    moms = [(m_w_in[0], v_w_in[0]), (m_w_out[0], v_w_out[0]), (m_w_gate_up[0], v_w_gate_up[0]), (m_w_down[0], v_w_down[0])]
    big = [_adam_shard(where, g, o, p, w, m, v, f"adam_{i}")
           for i, (g, o, p, w, (m, v)) in enumerate(zip(grads, got, pieces, shards, moms))]
    big = [[a[None] for a in four] for four in big]

    gath = _gather_small(g_w1, g_w2, g_w3, g_lb, g_wn, loss_p)
    params = [(norm1_w, m_norm1_w, v_norm1_w), (norm2_w, m_norm2_w, v_norm2_w),
              (row(final_norm_w), row(m_final_norm_w), row(v_final_norm_w)),
              (lb_logits, m_lb_logits, v_lb_logits), (hgrn_norm_w, m_hgrn_norm_w, v_hgrn_norm_w)]
    loss, (s_w1, s_w2, s_w3, s_lb, s_wn) = _small_update(gath, params)
    s_w3 = [a.reshape(D_MODEL) for a in s_w3]
    per_w = [s_w1, big[0], s_lb, s_wn, big[1], s_w2, big[2], big[3], s_w3]
    return (loss[0, 0], gx[None], *[p[0] for p in per_w], *[p[1] for p in per_w], *[p[2] for p in per_w], *[p[3] for p in per_w])
```

```python
import jax
import jax.numpy as jnp
from jax import lax
from jax.experimental import pallas as pl
from jax.experimental.pallas import tpu as pltpu

f32, bf16 = jnp.float32, jnp.bfloat16

D_MODEL = 1024
ATTN_W = 512
HEAD_DIM = 64
ATTN_BLK = 128
DILATIONS = (1, 4, 16)
HGRN_W = 512
HGRN_HD = 128
CHUNK = 64
IN_W = 3 * ATTN_W + 4 * HGRN_W
FFN = 2816
EPS = 1e-6
ROPE_THETA = 10000.0
NEG = -1e30
N_DEV = 8
ADAM_LR, ADAM_B1, ADAM_B2, ADAM_EPS, ADAM_WD, ADAM_STEP = 0.001, 0.9, 0.999, 1e-08, 0.01, 10
VMEM_LIMIT = 56 * 1024 * 1024


def _cp(*sem):
    return pltpu.CompilerParams(dimension_semantics=sem, vmem_limit_bytes=VMEM_LIMIT)


def _dot(a, b):
    return jnp.dot(a, b, preferred_element_type=f32)


def _dot_nt(a, b):
    return lax.dot_general(a, b, (((1,), (1,)), ((), ())), preferred_element_type=f32)


def _dot_tn(a, b):
    return lax.dot_general(a, b, (((0,), (0,)), ((), ())), preferred_element_type=f32)


def _sigmoid(x):
    return 0.5 * jnp.tanh(0.5 * x) + 0.5


class _Rider:
    def __init__(self, ins, out_shapes, scratch, first, last, middle=None):
        self.ins, self.out_shapes, self.scratch = list(ins), list(out_shapes), list(scratch)
        self.first, self.middle, self.last = first, middle, last


def _ride(call, rider, body, step, n_steps, n_in, n_out, n_scratch):
    if rider is None:
        return call, body, []
    ri, ro = len(rider.ins), len(rider.out_shapes)
    any_spec = pl.BlockSpec(memory_space=pl.ANY)
    call = dict(call, in_specs=call["in_specs"] + [any_spec] * ri, out_specs=call["out_specs"] + [any_spec] * ro,
                out_shape=call["out_shape"] + rider.out_shapes, scratch_shapes=call["scratch_shapes"] + rider.scratch)

    def riding(*refs):
        a = n_in + ri
        b = a + n_out + ro
        mine = refs[:n_in] + refs[a:a + n_out] + refs[b:b + n_scratch]
        theirs = (refs[n_in:a], refs[a + n_out:b], refs[b + n_scratch:])
        t = step()

        @pl.when(t == 0)
        def _():
            rider.first(*theirs)

        body(*mine)
        if rider.middle is not None:
            @pl.when(t == n_steps // 2)
            def _():
                rider.middle(*theirs)

        @pl.when(t == n_steps - 1)
        def _():
            rider.last(*theirs)

    return call, riding, rider.ins


def _rope_tables(S, rider=None):
    half = HEAD_DIM // 2
    tm = 256
    inv_freq = jnp.tile(ROPE_THETA ** (-jnp.arange(half, dtype=f32) / half), 128 // half).reshape(1, 128)
    sign = jnp.tile(jnp.concatenate([-jnp.ones((half,), f32), jnp.ones((half,), f32)]), 128 // HEAD_DIM).reshape(1, 128)

    def body(inv_ref, sign_ref, cos_ref, sg_ref):
        pos = (lax.broadcasted_iota(jnp.int32, (tm, 128), 0) + pl.program_id(0) * tm).astype(f32)
        ang = pos * inv_ref[...]
        cos_ref[...] = jnp.cos(ang)
        sg_ref[...] = jnp.sin(ang) * sign_ref[...]

    vec = pl.BlockSpec((1, 128), lambda i: (0, 0))
    out = pl.BlockSpec((tm, 128), lambda i: (i, 0))
    call = dict(in_specs=[vec, vec], out_specs=[out, out], out_shape=[jax.ShapeDtypeStruct((S, 128), f32)] * 2, scratch_shapes=[])
    call, body, more = _ride(call, rider, body, lambda: pl.program_id(0), S // tm, 2, 2, 0)
    return pl.pallas_call(body, name="rope_tables", grid=(S // tm,), compiler_params=_cp("arbitrary"), **call)(inv_freq, sign, *more)


def _swap_halves(v):
    n = v.shape[1]
    lane = lax.broadcasted_iota(jnp.int32, v.shape, 1)
    return jnp.where((lane % HEAD_DIM) < HEAD_DIM // 2, pltpu.roll(v, n - HEAD_DIM // 2, 1), pltpu.roll(v, HEAD_DIM // 2, 1))


def _in_proj(x, w1, win_g, cos_t, sg_t):
    S = x.shape[0]
    tm = 256
    w = IN_W // N_DEV

    def body(x_ref, w1_ref, wg_ref, cos_ref, sg_ref, u_ref, qkv_ref, hp_ref, w_ref):
        @pl.when(pl.program_id(0) == 0)
        def _():
            for d in range(N_DEV):
                w_ref[:, w * d:w * (d + 1)] = wg_ref[d]

        xv = x_ref[...]
        r = lax.rsqrt(jnp.mean(xv * xv, axis=-1, keepdims=True) + EPS)
        u = (xv * r * w1_ref[...]).astype(bf16)
        u_ref[...] = u
        cosv, sgv = jnp.tile(cos_ref[...], (1, ATTN_W // 128)), jnp.tile(sg_ref[...], (1, ATTN_W // 128))
        for j in range(3):
            pj = _dot(u, w_ref[:, j * ATTN_W:(j + 1) * ATTN_W])
            if j < 2:
                pj = pj * cosv + _swap_halves(pj) * sgv
            if j == 0:
                pj = pj * (HEAD_DIM ** -0.5)
            qkv_ref[:, j * ATTN_W:(j + 1) * ATTN_W] = pj.astype(bf16)
        for j in range(4):
            lo = 3 * ATTN_W + j * HGRN_W
            hp_ref[:, j * HGRN_W:(j + 1) * HGRN_W] = _dot(u, w_ref[:, lo:lo + HGRN_W])

    return pl.pallas_call(
        body, name="in_proj", grid=(S // tm,),
        in_specs=[pl.BlockSpec((tm, D_MODEL), lambda i: (i, 0)), pl.BlockSpec((1, D_MODEL), lambda i: (0, 0)),
                  pl.BlockSpec((N_DEV, D_MODEL, w), lambda i: (0, 0, 0)),
                  pl.BlockSpec((tm, 128), lambda i: (i, 0)), pl.BlockSpec((tm, 128), lambda i: (i, 0))],
        out_specs=[pl.BlockSpec((tm, D_MODEL), lambda i: (i, 0)), pl.BlockSpec((tm, 3 * ATTN_W), lambda i: (i, 0)),
                   pl.BlockSpec((tm, 4 * HGRN_W), lambda i: (i, 0)), pl.BlockSpec((D_MODEL, IN_W), lambda i: (0, 0))],
        out_shape=[jax.ShapeDtypeStruct((S, D_MODEL), bf16), jax.ShapeDtypeStruct((S, 3 * ATTN_W), bf16),
                   jax.ShapeDtypeStruct((S, 4 * HGRN_W), f32), jax.ShapeDtypeStruct((D_MODEL, IN_W), bf16)],
        compiler_params=_cp("arbitrary"),
    )(x, w1, win_g, cos_t, sg_t)


def _head_masks():
    lane = lax.broadcasted_iota(jnp.int32, (ATTN_BLK, 128), 1)
    even = lane < HEAD_DIM
    return even, (even, jnp.logical_not(even))


def _pair_fwd(q2, k2, v2, bias):
    even, masks = _head_masks()
    outs, lses = [], []
    for e in range(2):
        qm = jnp.where(masks[e], q2, 0.0).astype(bf16)
        s = _dot_nt(qm, k2) + bias
        m = jnp.max(s, axis=-1, keepdims=True)
        pe = jnp.exp(s - m)
        lsum = jnp.sum(pe, axis=-1, keepdims=True)
        outs.append(_dot(pe.astype(bf16), v2) / lsum)
        lses.append(jnp.broadcast_to(m + jnp.log(lsum), (ATTN_BLK, 128)))
    return jnp.where(even, outs[0], outs[1]), jnp.where(even, lses[0], lses[1])


def _merge(y0, l0, y1, l1):
    mx = jnp.maximum(l0, l1)
    a, b = jnp.exp(l0 - mx), jnp.exp(l1 - mx)
    tot = a + b
    return (a * y0 + b * y1) / tot, mx + jnp.log(tot)


def _pair_bwd(q2, k2f, v2, dy2, lse2, delta2, bias):
    _, masks = _head_masks()
    k2 = k2f.astype(bf16)
    klane = lax.broadcasted_iota(jnp.int32, (2 * ATTN_BLK, 128), 1) < HEAD_DIM
    kmasks = (klane, jnp.logical_not(klane))
    dq2 = jnp.zeros((ATTN_BLK, 128), f32)
    pes, dss, qms, dyms = [], [], [], []
    for e in range(2):
        c0 = e * HEAD_DIM
        qm = jnp.where(masks[e], q2, 0.0).astype(bf16)
        km = jnp.where(kmasks[e], k2f, 0.0).astype(bf16)
        dym = jnp.where(masks[e], dy2, 0.0).astype(bf16)
        pe = jnp.exp(_dot_nt(qm, k2) + bias - lse2[:, c0:c0 + 1])
        ds = (pe * (_dot_nt(dym, v2) - delta2[:, c0:c0 + 1])).astype(bf16)
        dq2 = dq2 + _dot(ds, km)
        pes.append(pe.astype(bf16))
        dss.append(ds)
        qms.append(qm)
        dyms.append(dym)
    dv2 = _dot_tn(jnp.concatenate(pes, axis=0), jnp.concatenate(dyms, axis=0))
    dk2 = _dot_tn(jnp.concatenate(dss, axis=0), jnp.concatenate(qms, axis=0))
    return dq2, dk2, dv2


TOK = 2048


def _key_bias():
    qi = lax.broadcasted_iota(jnp.int32, (ATTN_BLK, 2 * ATTN_BLK), 0)
    kj = lax.broadcasted_iota(jnp.int32, (ATTN_BLK, 2 * ATTN_BLK), 1)
    delta = ATTN_BLK + qi - kj
    seen = (delta >= 0) & (delta <= ATTN_BLK)
    return jnp.where(seen, 0.0, NEG), jnp.where(seen & (kj >= ATTN_BLK), 0.0, NEG)


def _attn_fwd(qkv, rider=None):
    S = qkv.shape[0]
    nS = S // TOK

    def body(q_ref, kp_ref, kc_ref, vp_ref, vc_ref, y_ref, l_ref, qs, k2, v2, ay, al):
        n = pl.program_id(1)
        qs[...] = q_ref[...].astype(f32)
        k2[0:TOK] = kp_ref[...].astype(f32)
        k2[TOK:2 * TOK] = kc_ref[...].astype(f32)
        v2[0:TOK] = vp_ref[...].astype(f32)
        v2[TOK:2 * TOK] = vc_ref[...].astype(f32)
        bias_any, bias_first = _key_bias()

        def block(dil, r, b, step, last):
            start = r + pl.multiple_of(step * b, step)
            rows = pl.ds(start, ATTN_BLK, stride=dil) if dil > 1 else pl.ds(start, ATTN_BLK)
            keys = (pl.ds(TOK + start - step, 2 * ATTN_BLK, stride=dil) if dil > 1
                    else pl.ds(TOK + start - step, 2 * ATTN_BLK))
            bias = jnp.where((n == 0) & (b == 0), bias_first, bias_any)
            out, lse = _pair_fwd(qs[rows, :], k2[keys, :].astype(bf16), v2[keys, :].astype(bf16), bias)
            if dil < DILATIONS[-1]:
                out, lse = _merge(ay[rows, :], al[rows, :], out, lse)
            if last:
                y_ref[rows, :] = out
                l_ref[rows, :] = lse
            else:
                ay[rows, :] = out
                al[rows, :] = lse

        for dil in reversed(DILATIONS):
            def loop(i, carry, dil=dil):
                block(dil, i % dil, i // dil, ATTN_BLK * dil, dil == 1)
                return carry
            lax.fori_loop(0, TOK // ATTN_BLK, loop, 0, unroll=True)

    blk = (TOK, 128)
    cur = lambda c: pl.BlockSpec(blk, lambda p, n: (n, 4 * c + p))
    prv = lambda c: pl.BlockSpec(blk, lambda p, n: (jnp.maximum(n - 1, 0), 4 * c + p))
    out = pl.BlockSpec(blk, lambda p, n: (n, p))
    call = dict(in_specs=[cur(0), prv(1), cur(1), prv(2), cur(2)], out_specs=[out, out],
                out_shape=[jax.ShapeDtypeStruct((S, ATTN_W), f32)] * 2,
                scratch_shapes=[pltpu.VMEM(blk, f32), pltpu.VMEM((2 * TOK, 128), f32), pltpu.VMEM((2 * TOK, 128), f32),
                                pltpu.VMEM(blk, f32), pltpu.VMEM(blk, f32)])
    call, body, more = _ride(call, rider, body, lambda: pl.program_id(0) * nS + pl.program_id(1), (ATTN_W // 128) * nS, 5, 2, 5)
    return pl.pallas_call(body, name="attention_fwd", grid=(ATTN_W // 128, nS), compiler_params=_cp("arbitrary", "arbitrary"),
                          **call)(qkv, qkv, qkv, qkv, qkv, *more)


def _attn_bwd(qkv, ya, lse, dmix, rider=None):
    S = qkv.shape[0]
    nS = S // TOK

    def body(q_ref, kp_ref, kc_ref, vp_ref, vc_ref, y_ref, l_ref, dy_ref, dq_ref, dk_ref, dv_ref, qs, k2, v2, dk2, dv2, dqa, dl):
        n = pl.program_id(1)

        @pl.when(n == 0)
        def _():
            dk2[...] = jnp.zeros_like(dk2)
            dv2[...] = jnp.zeros_like(dv2)

        @pl.when(n < nS)
        def _():
            qs[...] = q_ref[...].astype(f32)
            k2[0:TOK] = kp_ref[...].astype(f32)
            k2[TOK:2 * TOK] = kc_ref[...].astype(f32)
            v2[0:TOK] = vp_ref[...].astype(f32)
            v2[TOK:2 * TOK] = vc_ref[...].astype(f32)
            li = lax.broadcasted_iota(jnp.int32, (128, 128), 0)
            lj = lax.broadcasted_iota(jnp.int32, (128, 128), 1)
            seg = jnp.where((li // HEAD_DIM) == (lj // HEAD_DIM), 1.0, 0.0).astype(bf16)
            bias_any, bias_first = _key_bias()

            def delta_rows(t, carry):
                rows = pl.ds(pl.multiple_of(256 * t, 256), 256)
                dyy = dy_ref[rows, :] * y_ref[rows, :]
                hi = dyy.astype(bf16)
                dl[rows, :] = _dot(hi, seg) + _dot((dyy - hi.astype(f32)).astype(bf16), seg)
                return carry

            lax.fori_loop(0, TOK // 256, delta_rows, 0)

            def block(dil, r, b, step, first_pattern, last):
                start = r + pl.multiple_of(step * b, step)
                rows = pl.ds(start, ATTN_BLK, stride=dil) if dil > 1 else pl.ds(start, ATTN_BLK)
                keys = (pl.ds(TOK + start - step, 2 * ATTN_BLK, stride=dil) if dil > 1
                        else pl.ds(TOK + start - step, 2 * ATTN_BLK))
                bias = jnp.where((n == 0) & (b == 0), bias_first, bias_any)
                dq2, dkk, dvv = _pair_bwd(qs[rows, :], k2[keys, :], v2[keys, :].astype(bf16), dy_ref[rows, :],
                                          l_ref[rows, :], dl[rows, :], bias)
                if last:
                    dq_ref[rows, :] = dqa[rows, :] + dq2
                elif first_pattern:
                    dqa[rows, :] = dq2
                else:
                    dqa[rows, :] += dq2
                dk2[keys, :] += dkk
                dv2[keys, :] += dvv

            for dil in reversed(DILATIONS):
                def loop(i, carry, dil=dil):
                    block(dil, i % dil, i // dil, ATTN_BLK * dil, dil == DILATIONS[-1], dil == 1)
                    return carry
                lax.fori_loop(0, TOK // ATTN_BLK, loop, 0, unroll=True)

        dk_ref[...] = dk2[0:TOK]
        dv_ref[...] = dv2[0:TOK]
        dk2[0:TOK] = dk2[TOK:2 * TOK]
        dv2[0:TOK] = dv2[TOK:2 * TOK]
        dk2[TOK:2 * TOK] = jnp.zeros((TOK, 128), f32)
        dv2[TOK:2 * TOK] = jnp.zeros((TOK, 128), f32)

    blk = (TOK, 128)
    cn = lambda n: jnp.minimum(n, nS - 1)
    pn = lambda n: jnp.clip(n - 1, 0, nS - 1)
    cur = lambda c: pl.BlockSpec(blk, lambda p, n: (cn(n), 4 * c + p))
    prv = lambda c: pl.BlockSpec(blk, lambda p, n: (pn(n), 4 * c + p))
    at_n = pl.BlockSpec(blk, lambda p, n: (cn(n), p))
    at_p = pl.BlockSpec(blk, lambda p, n: (pn(n), p))
    big = lambda: pltpu.VMEM((2 * TOK, 128), f32)
    call = dict(in_specs=[cur(0), prv(1), cur(1), prv(2), cur(2), at_n, at_n, at_n], out_specs=[at_n, at_p, at_p],
                out_shape=[jax.ShapeDtypeStruct((S, ATTN_W), f32)] * 3,
                scratch_shapes=[pltpu.VMEM(blk, f32), big(), big(), big(), big(), pltpu.VMEM(blk, f32), pltpu.VMEM(blk, f32)])
    call, body, more = _ride(call, rider, body, lambda: pl.program_id(0) * (nS + 1) + pl.program_id(1),
                             (ATTN_W // 128) * (nS + 1), 8, 3, 7)
    return pl.pallas_call(body, name="attention_bwd", grid=(ATTN_W // 128, nS + 1), compiler_params=_cp("arbitrary", "arbitrary"),
                          **call)(qkv, qkv, qkv, qkv, qkv, ya, lse, dmix, *more)


HG_T = 512
N_HH = HGRN_W // HGRN_HD
HG_SUB = 128
SAFE_RANGE = 75.0


def _row_in_chunk():
    return lax.broadcasted_iota(jnp.int32, (HG_T, HGRN_HD), 0) % CHUNK


def _chunk_cumsum(v, rc):
    k = 1
    while k < CHUNK:
        v = v + jnp.where(rc >= k, pltpu.roll(v, k, 0), 0.0)
        k *= 2
    return v


def _chunk_rcumsum(v, rc):
    k = 1
    while k < CHUNK:
        v = v + jnp.where(rc < CHUNK - k, pltpu.roll(v, HG_T - k, 0), 0.0)
        k *= 2
    return v


def _hgrn_gates(qb, fb, lb):
    sf = _sigmoid(fb)
    f = lb + (1.0 - lb) * sf
    sq = _sigmoid(qb)
    return sf, f, jnp.log(f), 1.0 - f, sq, qb * sq


def _hgrn_prep(qb, fb, lbl2, rc):
    lb = _sigmoid(lbl2[0:1, :] - lbl2[1:2, :])
    sf, f, lf, key, sq, qf = _hgrn_gates(qb, fb, lb)
    b = _chunk_cumsum(lf, rc)
    rem = _chunk_rcumsum(lf, rc) - lf
    return dict(lb=lb, sf=sf, f=f, key=key, sq=sq, qf=qf, b=b, rem=rem, eb=jnp.exp(b), er=jnp.exp(rem))


def _chunk_mask():
    r = lax.broadcasted_iota(jnp.int32, (HG_SUB, HG_SUB), 0)
    c = lax.broadcasted_iota(jnp.int32, (HG_SUB, HG_SUB), 1)
    return ((r // CHUNK) == (c // CHUNK)) & (c <= r)


def _hgrn_fwd(hp, lbl, wn):
    S = hp.shape[0]
    nT = S // HG_T

    def body(qb_ref, fb_ref, ib_ref, gb_ref, lbl_ref, wn_ref, yb_ref, o_ref, st_ref, ST, qt_s, kh_s, dec_s, oi_s):
        @pl.when(pl.program_id(0) == 0)
        def _():
            ST[...] = jnp.zeros_like(ST)

        rc = _row_in_chunk()
        for h in range(N_HH):
            sl = slice(HGRN_HD * h, HGRN_HD * (h + 1))
            p = _hgrn_prep(qb_ref[:, sl], fb_ref[:, sl], lbl_ref[:, sl], rc)
            qf, key, b = p["qf"], p["key"], p["b"]
            qt = qf * p["eb"]
            qt_s[:, sl] = qt.astype(bf16)
            kh_s[:, sl] = (key * p["er"]).astype(bf16)
            dec_s[:, sl] = jnp.exp(b + p["rem"])
            rng = jnp.max(-(b + p["rem"]))

            @pl.when(rng < SAFE_RANGE)
            def _():
                kp = (key * jnp.exp(-b)).astype(bf16)
                cmask = _chunk_mask()
                for j in range(HG_T // HG_SUB):
                    rs = slice(HG_SUB * j, HG_SUB * (j + 1))
                    sc = jnp.where(cmask, _dot_nt(qt[rs].astype(bf16), kp[rs]), 0.0).astype(bf16)
                    oi_s[rs, sl] = _dot(sc, ib_ref[rs, sl].astype(bf16))

            @pl.when(rng >= SAFE_RANGE)
            def _():
                v = ib_ref[:, sl]
                ones = jnp.ones((HGRN_HD, HGRN_HD), bf16)

                def lag(l, o):
                    e = jnp.exp(jnp.where(rc >= l, b - pltpu.roll(b, l, 0), NEG))
                    pr = qf * pltpu.roll(key, l, 0) * e
                    return o + _dot(pr.astype(bf16), ones) * pltpu.roll(v, l, 0)

                oi_s[:, sl] = lax.fori_loop(1, CHUNK, lag, _dot((qf * key).astype(bf16), ones) * v)

        def step(c, carry):
            rows = pl.ds(pl.multiple_of(c * CHUNK, CHUNK), CHUNK)
            row0 = pl.ds(pl.multiple_of(c * CHUNK, CHUNK), 1)
            for h in range(N_HH):
                sl = slice(HGRN_HD * h, HGRN_HD * (h + 1))
                stv = ST[h]
                st_ref[c, sl, :] = stv
                oi_s[rows, sl] += _dot_nt(qt_s[rows, sl], stv.astype(bf16))
                ST[h] = stv * dec_s[row0, sl] + _dot_tn(ib_ref[rows, sl].astype(bf16), kh_s[rows, sl])
            return carry

        lax.fori_loop(0, HG_T // CHUNK, step, 0, unroll=True)

        for h in range(N_HH):
            sl = slice(HGRN_HD * h, HGRN_HD * (h + 1))
            o = oi_s[:, sl]
            o_ref[:, sl] = o
            on = o * lax.rsqrt(jnp.mean(o * o, axis=-1, keepdims=True) + EPS)
            g = gb_ref[:, sl]
            yb_ref[:, sl] = on * wn_ref[:, sl] * (g * _sigmoid(g))

    col = lambda c: pl.BlockSpec((HG_T, HGRN_W), lambda i: (i, c))
    tile = pl.BlockSpec((HG_T, HGRN_W), lambda i: (i, 0))
    whole = lambda a: pl.BlockSpec(a.shape, lambda i: (0, 0))
    return pl.pallas_call(
        body, name="hgrn_fwd", grid=(nT,),
        in_specs=[col(0), col(1), col(2), col(3), whole(lbl), whole(wn)],
        out_specs=[tile, tile, pl.BlockSpec((HG_T // CHUNK, HGRN_W, HGRN_HD), lambda i: (i, 0, 0))],
        out_shape=[jax.ShapeDtypeStruct((S, HGRN_W), f32), jax.ShapeDtypeStruct((S, HGRN_W), f32),
                   jax.ShapeDtypeStruct((S // CHUNK, HGRN_W, HGRN_HD), f32)],
        scratch_shapes=[pltpu.VMEM((N_HH, HGRN_HD, HGRN_HD), f32), pltpu.VMEM((HG_T, HGRN_W), bf16),
                        pltpu.VMEM((HG_T, HGRN_W), bf16), pltpu.VMEM((HG_T, HGRN_W), f32), pltpu.VMEM((HG_T, HGRN_W), f32)],
        compiler_params=_cp("arbitrary"),
    )(hp, hp, hp, hp, lbl, wn)


def _hgrn_bwd(hp, lbl, wn, o_sav, states, dmix, rider=None):
    S = hp.shape[0]
    nT = S // HG_T

    def body(qb_ref, fb_ref, ib_ref, gb_ref, lbl_ref, wn_ref, o_ref, st_ref, dy_ref,
             dq_ref, df_ref, di_ref, dg_ref, gwn_ref, glb_ref,
             DST, qt_s, kh_s, dec_s, do_s, dqt_s, dkh_s, dbl_s, dvi_s, dqi_s, dki_s, dbi_s):
        @pl.when(pl.program_id(0) == 0)
        def _():
            DST[...] = jnp.zeros_like(DST)
            gwn_ref[...] = jnp.zeros_like(gwn_ref)
            glb_ref[...] = jnp.zeros_like(glb_ref)

        rc = _row_in_chunk()
        preps = []
        for h in range(N_HH):
            sl = slice(HGRN_HD * h, HGRN_HD * (h + 1))
            p = _hgrn_prep(qb_ref[:, sl], fb_ref[:, sl], lbl_ref[:, sl], rc)
            preps.append(p)
            qf, key, b = p["qf"], p["key"], p["b"]
            v = ib_ref[:, sl]
            o = o_ref[:, sl]
            rinv = lax.rsqrt(jnp.mean(o * o, axis=-1, keepdims=True) + EPS)
            on = o * rinv
            g = gb_ref[:, sl]
            sgm = _sigmoid(g)
            silu_g = g * sgm
            dy = dy_ref[:, sl]
            wn_v = wn_ref[:, sl]
            gwn_ref[:, sl] += jnp.sum(dy * on * silu_g, axis=0, keepdims=True)
            dg_ref[:, sl] = (dy * on * wn_v * (sgm * (1.0 + g * (1.0 - sgm)))).astype(bf16)
            t1 = dy * wn_v * silu_g
            do = rinv * (t1 - on * jnp.mean(t1 * on, axis=-1, keepdims=True))
            do_s[:, sl] = do.astype(bf16)
            qt = qf * p["eb"]
            qt_s[:, sl] = qt.astype(bf16)
            kh_s[:, sl] = (key * p["er"]).astype(bf16)
            dec_s[:, sl] = jnp.exp(b + p["rem"])
            rng = jnp.max(-(b + p["rem"]))

            @pl.when(rng < SAFE_RANGE)
            def _():
                einv = jnp.exp(-b)
                kp = (key * einv).astype(bf16)
                cmask = _chunk_mask()
                for j in range(HG_T // HG_SUB):
                    rs = slice(HG_SUB * j, HG_SUB * (j + 1))
                    qtb, dob, vb = qt[rs].astype(bf16), do[rs].astype(bf16), v[rs].astype(bf16)
                    sc = jnp.where(cmask, _dot_nt(qtb, kp[rs]), 0.0).astype(bf16)
                    dsc = jnp.where(cmask, _dot_nt(dob, vb), 0.0).astype(bf16)
                    dqp = _dot(dsc, kp[rs])
                    dkp = _dot_tn(dsc, qtb)
                    dvi_s[rs, sl] = _dot_tn(sc, dob)
                    dqi_s[rs, sl] = dqp * p["eb"][rs]
                    dki_s[rs, sl] = dkp * einv[rs]
                    dbi_s[rs, sl] = dqp * qtb.astype(f32) - dkp * kp[rs].astype(f32)

            @pl.when(rng >= SAFE_RANGE)
            def _():
                ones = jnp.ones((HGRN_HD, HGRN_HD), bf16)

                def lag(l, carry):
                    dqf, dkey, db, dv = carry
                    e = jnp.exp(jnp.where(rc >= l, b - pltpu.roll(b, l, 0), NEG))
                    ks, vs, qe = pltpu.roll(key, l, 0), pltpu.roll(v, l, 0), qf * e
                    pr = qe * ks
                    rl = _dot(pr.astype(bf16), ones)
                    drl = jnp.where(rc >= l, _dot((do * vs).astype(bf16), ones), 0.0)
                    gl = drl * pr
                    back = HG_T - l
                    return (dqf + drl * ks * e, dkey + pltpu.roll(drl * qe, back, 0), db + gl - pltpu.roll(gl, back, 0),
                            dv + pltpu.roll(rl * do, back, 0))

                rl0 = _dot((qf * key).astype(bf16), ones)
                drl0 = _dot((do * v).astype(bf16), ones)
                dqf, dkey, db, dv = lax.fori_loop(1, CHUNK, lag, (drl0 * key, drl0 * qf, jnp.zeros((HG_T, HGRN_HD), f32), rl0 * do))
                dvi_s[:, sl] = dv
                dqi_s[:, sl] = dqf
                dki_s[:, sl] = dkey
                dbi_s[:, sl] = db

        def step(k, carry):
            c = HG_T // CHUNK - 1 - k
            rows = pl.ds(pl.multiple_of(c * CHUNK, CHUNK), CHUNK)
            row0 = pl.ds(pl.multiple_of(c * CHUNK, CHUNK), 1)
            for h in range(N_HH):
                sl = slice(HGRN_HD * h, HGRN_HD * (h + 1))
                stp = st_ref[c, sl, :]
                dst = DST[h]
                dstb = dst.astype(bf16)
                dob = do_s[rows, sl]
                khb = kh_s[rows, sl]
                dec = dec_s[row0, sl]
                dqt_s[rows, sl] = _dot(dob, stp.astype(bf16))
                dkh = _dot(ib_ref[rows, sl].astype(bf16), dstb)
                dkh_s[rows, sl] = dkh
                dvi_s[rows, sl] += _dot_nt(khb, dstb)
                dbl = jnp.sum(dst * stp, axis=0, keepdims=True) * dec + jnp.sum(dkh * khb.astype(f32), axis=0, keepdims=True)
                dbl_s[rows, sl] = jnp.broadcast_to(dbl, (CHUNK, HGRN_HD))
                DST[h] = dst * dec + _dot_tn(dob, qt_s[rows, sl])
            return carry

        lax.fori_loop(0, HG_T // CHUNK, step, 0, unroll=True)

        for h in range(N_HH):
            sl = slice(HGRN_HD * h, HGRN_HD * (h + 1))
            qb = qb_ref[:, sl]
            p = preps[h]
            sf, sq, lb = p["sf"], p["sq"], p["lb"]
            dqt, dkh = dqt_s[:, sl], dkh_s[:, sl]
            dqf = dqt * p["eb"] + dqi_s[:, sl]
            dkey = dkh * p["er"] + dki_s[:, sl]
            db = dqt * (p["qf"] * p["eb"]) - dkh * (p["key"] * p["er"]) + jnp.where(rc == CHUNK - 1, dbl_s[:, sl], 0.0) + dbi_s[:, sl]
            df = _chunk_rcumsum(db, rc) / p["f"] - dkey
            df_ref[:, sl] = (df * (1.0 - lb) * sf * (1.0 - sf)).astype(bf16)
            glb_ref[:, sl] += jnp.sum(df * (1.0 - sf), axis=0, keepdims=True)
            dq_ref[:, sl] = (dqf * (sq * (1.0 + qb * (1.0 - sq)))).astype(bf16)
            di_ref[:, sl] = dvi_s[:, sl].astype(bf16)

    rev = lambda i: nT - 1 - i
    col = lambda c: pl.BlockSpec((HG_T, HGRN_W), lambda i: (rev(i), c))
    tile = pl.BlockSpec((HG_T, HGRN_W), lambda i: (rev(i), 0))
    whole = lambda a: pl.BlockSpec(a.shape, lambda i: (0, 0))
    vec = pl.BlockSpec((1, HGRN_W), lambda i: (0, 0))
    tb = lambda: pltpu.VMEM((HG_T, HGRN_W), bf16)
    tf = lambda: pltpu.VMEM((HG_T, HGRN_W), f32)
    call = dict(in_specs=[col(0), col(1), col(2), col(3), whole(lbl), whole(wn), tile,
                          pl.BlockSpec((HG_T // CHUNK, HGRN_W, HGRN_HD), lambda i: (rev(i), 0, 0)),
                          pl.BlockSpec((HG_T, HGRN_W), lambda i: (rev(i), 1))],
                out_specs=[tile, tile, tile, tile, vec, vec],
                out_shape=[jax.ShapeDtypeStruct((S, HGRN_W), bf16)] * 4 + [jax.ShapeDtypeStruct((1, HGRN_W), f32)] * 2,
                scratch_shapes=[pltpu.VMEM((N_HH, HGRN_HD, HGRN_HD), f32), tb(), tb(), tf(), tb(), tf(), tf(), tf(), tf(), tf(),
                                tf(), tf()])
    call, body, more = _ride(call, rider, body, lambda: pl.program_id(0), nT, 9, 6, 12)
    return pl.pallas_call(body, name="hgrn_bwd", grid=(nT,), compiler_params=_cp("arbitrary"), **call)(
        hp, hp, hp, hp, lbl, wn, o_sav, states, dmix, *more)


def _out_proj(x, ya, yb, wout, w2):
    S = x.shape[0]
    tm = 512

    def body(x_ref, ya_ref, yb_ref, w_ref, w2_ref, h1_ref, u2_ref, mix_ref):
        mixed = jnp.concatenate([ya_ref[...], yb_ref[...]], axis=1).astype(bf16)
        mix_ref[...] = mixed
        h1 = x_ref[...] + _dot(mixed, w_ref[...])
        h1_ref[...] = h1
        r = lax.rsqrt(jnp.mean(h1 * h1, axis=-1, keepdims=True) + EPS)
        u2_ref[...] = (h1 * r * w2_ref[...]).astype(bf16)

    row = lambda w: pl.BlockSpec((tm, w), lambda i: (i, 0))
    return pl.pallas_call(
        body, name="out_proj", grid=(S // tm,),
        in_specs=[row(D_MODEL), row(ATTN_W), row(HGRN_W), pl.BlockSpec((D_MODEL, D_MODEL), lambda i: (0, 0)),
                  pl.BlockSpec((1, D_MODEL), lambda i: (0, 0))],
        out_specs=[row(D_MODEL), row(D_MODEL), row(D_MODEL)],
        out_shape=[jax.ShapeDtypeStruct((S, D_MODEL), f32), jax.ShapeDtypeStruct((S, D_MODEL), bf16),
                   jax.ShapeDtypeStruct((S, D_MODEL), bf16)],
        compiler_params=_cp("arbitrary"),
    )(x, ya, yb, wout, w2)


def _gate_up(u2, wgu_g):
    S = u2.shape[0]
    w = 2 * FFN // N_DEV
    tm, tn = 512, 2 * w
    nj = FFN // tn

    def body(u_ref, wgg_ref, wug_ref, g_ref, up_ref, a_ref, wg_ref, wu_ref):
        @pl.when(pl.program_id(1) == 0)
        def _():
            for k in range(2):
                wg_ref[:, w * k:w * (k + 1)] = wgg_ref[k]
                wu_ref[:, w * k:w * (k + 1)] = wug_ref[k]

        u = u_ref[...]
        g = _dot(u, wg_ref[...])
        up = _dot(u, wu_ref[...])
        sg = _sigmoid(g)
        silu = g * sg
        g_ref[...] = silu.astype(bf16)
        up_ref[...] = (up * (sg + silu * (1.0 - sg))).astype(bf16)
        a_ref[...] = (silu * up).astype(bf16)

    out = pl.BlockSpec((tm, tn), lambda j, i: (i, j))
    wout = pl.BlockSpec((D_MODEL, tn), lambda j, i: (0, j))
    return pl.pallas_call(
        body, name="gate_up", grid=(nj, S // tm),
        in_specs=[pl.BlockSpec((tm, D_MODEL), lambda j, i: (i, 0)), pl.BlockSpec((2, D_MODEL, w), lambda j, i: (j, 0, 0)),
                  pl.BlockSpec((2, D_MODEL, w), lambda j, i: (j + nj, 0, 0))],
        out_specs=[out, out, out, wout, wout],
        out_shape=[jax.ShapeDtypeStruct((S, FFN), bf16)] * 3 + [jax.ShapeDtypeStruct((D_MODEL, FFN), bf16)] * 2,
        compiler_params=_cp("arbitrary", "arbitrary"),
    )(u2, wgu_g, wgu_g)


def _rms_bwd(dyw, hn, r):
    return r * (dyw - hn * jnp.mean(dyw * hn, axis=-1, keepdims=True))


def _down_loss(act, wdown, h1, tgt, w3):
    S = act.shape[0]
    tm = 256

    def body(a_ref, w_ref, h1_ref, t_ref, w3_ref, dh2_ref, loss_ref, gw3_ref):
        @pl.when(pl.program_id(0) == 0)
        def _():
            loss_ref[...] = jnp.zeros_like(loss_ref)
            gw3_ref[...] = jnp.zeros_like(gw3_ref)

        h2 = h1_ref[...] + _dot(a_ref[...], w_ref[...])
        r = lax.rsqrt(jnp.mean(h2 * h2, axis=-1, keepdims=True) + EPS)
        hn = h2 * r
        w3 = w3_ref[...]
        err = hn * w3 - t_ref[...]
        loss_ref[...] += (0.5 / D_MODEL) * jnp.sum(err * err)
        dy = err * (1.0 / D_MODEL)
        gw3_ref[...] += jnp.sum(dy * hn, axis=0, keepdims=True)
        dh2_ref[...] = _rms_bwd(dy * w3, hn, r)

    row = lambda w: pl.BlockSpec((tm, w), lambda i: (i, 0))
    return pl.pallas_call(
        body, name="down_loss", grid=(S // tm,),
        in_specs=[row(FFN), pl.BlockSpec((FFN, D_MODEL), lambda i: (0, 0)), row(D_MODEL), row(D_MODEL),
                  pl.BlockSpec((1, D_MODEL), lambda i: (0, 0))],
        out_specs=[row(D_MODEL), pl.BlockSpec((1, 128), lambda i: (0, 0)), pl.BlockSpec((1, D_MODEL), lambda i: (0, 0))],
        out_shape=[jax.ShapeDtypeStruct((S, D_MODEL), f32), jax.ShapeDtypeStruct((1, 128), f32),
                   jax.ShapeDtypeStruct((1, D_MODEL), f32)],
        compiler_params=_cp("arbitrary"),
    )(act, wdown, h1, tgt, w3)


def _dact(dh2, wdown, silu, up_dsilu):
    S = dh2.shape[0]
    tm = 256

    def body(d_ref, w_ref, s_ref, u_ref, o_ref):
        da = _dot_nt(d_ref[...].astype(bf16), w_ref[...])
        o_ref[1] = (da * s_ref[...].astype(f32)).astype(bf16)
        o_ref[0] = (da * u_ref[...].astype(f32)).astype(bf16)

    row = lambda w: pl.BlockSpec((tm, w), lambda i: (i, 0))
    return pl.pallas_call(
        body, name="dact", grid=(S // tm,),
        in_specs=[row(D_MODEL), pl.BlockSpec((FFN, D_MODEL), lambda i: (0, 0)), row(FFN), row(FFN)],
        out_specs=pl.BlockSpec((2, tm, FFN), lambda i: (0, i, 0)),
        out_shape=jax.ShapeDtypeStruct((2, S, FFN), bf16),
        compiler_params=_cp("arbitrary"),
    )(dh2, wdown, silu, up_dsilu)


def _dgu(dgu2, wgate, wup, h1, w2, dh2, wout, rider=None):
    S = dgu2.shape[1]
    tm = 256

    def body(d_ref, wg_ref, wu_ref, h1_ref, w2_ref, dh2_ref, wo_ref, dh1_ref, gw2_ref, dmix_ref):
        @pl.when(pl.program_id(0) == 0)
        def _():
            gw2_ref[...] = jnp.zeros_like(gw2_ref)

        du2 = _dot_nt(d_ref[0], wg_ref[...]) + _dot_nt(d_ref[1], wu_ref[...])
        h1 = h1_ref[...]
        r = lax.rsqrt(jnp.mean(h1 * h1, axis=-1, keepdims=True) + EPS)
        hn = h1 * r
        gw2_ref[...] += jnp.sum(du2 * hn, axis=0, keepdims=True)
        dh1 = dh2_ref[...] + _rms_bwd(du2 * w2_ref[...], hn, r)
        dh1_ref[...] = dh1
        dmix_ref[...] = _dot_nt(dh1.astype(bf16), wo_ref[...])

    row = lambda w: pl.BlockSpec((tm, w), lambda i: (i, 0))
    call = dict(in_specs=[pl.BlockSpec((2, tm, FFN), lambda i: (0, i, 0)), pl.BlockSpec((D_MODEL, FFN), lambda i: (0, 0)),
                          pl.BlockSpec((D_MODEL, FFN), lambda i: (0, 0)), row(D_MODEL),
                          pl.BlockSpec((1, D_MODEL), lambda i: (0, 0)), row(D_MODEL),
                          pl.BlockSpec((D_MODEL, D_MODEL), lambda i: (0, 0))],
                out_specs=[row(D_MODEL), pl.BlockSpec((1, D_MODEL), lambda i: (0, 0)), row(D_MODEL)],
                out_shape=[jax.ShapeDtypeStruct((S, D_MODEL), f32), jax.ShapeDtypeStruct((1, D_MODEL), f32),
                           jax.ShapeDtypeStruct((S, D_MODEL), f32)], scratch_shapes=[])
    call, body, more = _ride(call, rider, body, lambda: pl.program_id(0), S // tm, 7, 3, 0)
    return pl.pallas_call(body, name="dgu", grid=(S // tm,), compiler_params=_cp("arbitrary"), **call)(
        dgu2, wgate, wup, h1, w2, dh2, wout, *more)


def _din(dq, dk, dv, dhq, dhf, dhi, dhg, cos_t, sg_t, win, x, w1, dh1):
    S = x.shape[0]
    tm = 256

    def body(dq_ref, dk_ref, dv_ref, dhq_ref, dhf_ref, dhi_ref, dhg_ref, cos_ref, sg_ref, w_ref, x_ref, w1_ref, dh1_ref,
             dp_ref, gx_ref, gw1_ref):
        @pl.when(pl.program_id(0) == 0)
        def _():
            gw1_ref[...] = jnp.zeros_like(gw1_ref)

        cosv, sgv = jnp.tile(cos_ref[...], (1, ATTN_W // 128)), jnp.tile(sg_ref[...], (1, ATTN_W // 128))
        unrope = lambda d: d * cosv - sgv * _swap_halves(d)
        parts = [(unrope(dq_ref[...]) * (HEAD_DIM ** -0.5)).astype(bf16), unrope(dk_ref[...]).astype(bf16),
                 dv_ref[...].astype(bf16), dhq_ref[...], dhf_ref[...], dhi_ref[...], dhg_ref[...]]
        du = jnp.zeros((tm, D_MODEL), f32)
        for j, pj in enumerate(parts):
            dp_ref[:, j * 512:(j + 1) * 512] = pj
            du = du + _dot_nt(pj, w_ref[:, j * 512:(j + 1) * 512])
        xv = x_ref[...]
        r = lax.rsqrt(jnp.mean(xv * xv, axis=-1, keepdims=True) + EPS)
        xn = xv * r
        gw1_ref[...] += jnp.sum(du * xn, axis=0, keepdims=True)
        gx_ref[...] = dh1_ref[...] + _rms_bwd(du * w1_ref[...], xn, r)

    row = lambda w: pl.BlockSpec((tm, w), lambda i: (i, 0))
    vec = pl.BlockSpec((1, D_MODEL), lambda i: (0, 0))
    return pl.pallas_call(
        body, name="din", grid=(S // tm,),
        in_specs=[row(512)] * 7 + [row(128), row(128), pl.BlockSpec((D_MODEL, IN_W), lambda i: (0, 0)), row(D_MODEL), vec,
                                   row(D_MODEL)],
        out_specs=[row(IN_W), row(D_MODEL), vec],
        out_shape=[jax.ShapeDtypeStruct((S, IN_W), bf16), jax.ShapeDtypeStruct((S, D_MODEL), f32),
                   jax.ShapeDtypeStruct((1, D_MODEL), f32)],
        compiler_params=_cp("arbitrary"),
    )(dq, dk, dv, dhq, dhf, dhi, dhg, cos_t, sg_t, win, x, w1, dh1)


def _gw(a, bs, tn, name, ts=2048):
    S, M = a.shape
    N = bs[0].shape[1]
    k = len(bs)

    def body(a_ref, *refs):
        @pl.when(pl.program_id(1) == 0)
        def _():
            for o_ref in refs[k:]:
                o_ref[...] = jnp.zeros_like(o_ref)

        at = a_ref[...].astype(bf16)
        for b_ref, o_ref in zip(refs[:k], refs[k:]):
            o_ref[...] += _dot_tn(at, b_ref[...].astype(bf16))

    return pl.pallas_call(
        body, name=name, grid=(N // tn, S // ts),
        in_specs=[pl.BlockSpec((ts, M), lambda j, s: (s, 0))] + [pl.BlockSpec((ts, tn), lambda j, s: (s, j))] * k,
        out_specs=[pl.BlockSpec((M, tn), lambda j, s: (0, j))] * k, out_shape=[jax.ShapeDtypeStruct((M, N), f32)] * k,
        compiler_params=_cp("arbitrary", "arbitrary"),
    )(a, *bs)


def _gw_by_owner(a, b3, w, name, ts):
    S, M = a.shape
    G, _, Ng = b3.shape
    tn = 2 * w
    per_group = Ng // tn
    n_s = S // ts

    def body(a_ref, b_ref, o_ref, acc):
        s = pl.program_id(1)

        @pl.when(s == 0)
        def _():
            acc[...] = jnp.zeros_like(acc)

        acc[...] += _dot_tn(a_ref[...].astype(bf16), b_ref[0].astype(bf16))

        @pl.when(s == n_s - 1)
        def _():
            o_ref[0] = acc[:, 0:w]
            o_ref[1] = acc[:, w:tn]

    return pl.pallas_call(
        body, name=name, grid=(G * per_group, n_s),
        in_specs=[pl.BlockSpec((ts, M), lambda j, s: (s, 0)),
                  pl.BlockSpec((1, ts, tn), lambda j, s: (j // per_group, s, j % per_group))],
        out_specs=pl.BlockSpec((2, M, w), lambda j, s: (j, 0, 0)), out_shape=jax.ShapeDtypeStruct((G * Ng // w, M, w), f32),
        scratch_shapes=[pltpu.VMEM((M, tn), f32)], compiler_params=_cp("arbitrary", "arbitrary"),
    )(a, b3)


MESH = pl.DeviceIdType.MESH
ANY = pl.BlockSpec(memory_space=pl.ANY)
VMEM_SPEC = pl.BlockSpec(memory_space=pltpu.VMEM)


def _pos():
    return lax.axis_index("x"), lax.axis_index("y"), lax.axis_index("c")


def _flip(v, bit):
    return 1 - v if bit else v


def _gather_rider(shards):
    n = len(shards)

    def parts(outs, scratch):
        send_sems, recv_sems, local_sems = scratch[n:]
        x, y, c = _pos()
        chips = [(1 - x, y), (x, 1 - y), (1 - x, 1 - y)]

        def copy(a, k, block, to, src=None):
            dst = outs[a].at[4 * block[0] + 2 * block[1] + block[2]]
            return pltpu.make_async_remote_copy(src_ref=dst if src is None else src, dst_ref=dst, send_sem=send_sems.at[a, k],
                                                recv_sem=recv_sems.at[a, k], device_id=to, device_id_type=MESH)

        bufs = scratch[:n]
        me, sibling = (x, y, c), (x, y, 1 - c)
        own = lambda a: pltpu.make_async_copy(bufs[a], outs[a].at[4 * x + 2 * y + c], local_sems.at[a])
        sent = lambda a: [copy(a, 0, me, sibling, src=bufs[a])] + [copy(a, 1 + j, me, (*chip, c), src=bufs[a])
                                                                   for j, chip in enumerate(chips)]
        passed = lambda a: [copy(a, 4 + j, (*chip, c), sibling) for j, chip in enumerate(chips)]
        landed = lambda a: [copy(a, 1 + j, (*chip, c), me) for j, chip in enumerate(chips)]
        from_sibling = lambda a: [copy(a, 0, sibling, me)] + [copy(a, 4 + j, (*chip, 1 - c), me) for j, chip in enumerate(chips)]
        return bufs, local_sems, own, sent, passed, landed, from_sibling

    def first(ins, outs, scratch):
        bufs, local_sems, own, sent, _, _, _ = parts(outs, scratch)
        loads = [pltpu.make_async_copy(ins[a], bufs[a], local_sems.at[a]) for a in range(n)]
        for ld in loads:
            ld.start()
        for a in range(n):
            loads[a].wait()
            own(a).start()
            for cp in sent(a):
                cp.start()

    def middle(ins, outs, scratch):
        _, _, _, _, passed, landed, _ = parts(outs, scratch)
        for a in range(n):
            for got, on in zip(landed(a), passed(a)):
                got.wait_recv()
                on.start()

    def last(ins, outs, scratch):
        _, _, own, sent, passed, _, from_sibling = parts(outs, scratch)
        for a in range(n):
            for cp in from_sibling(a):
                cp.wait_recv()
        for a in range(n):
            for cp in sent(a) + passed(a):
                cp.wait_send()
            own(a).wait()

    return _Rider(shards, [jax.ShapeDtypeStruct((N_DEV,) + s.shape, s.dtype) for s in shards],
                  [pltpu.VMEM(s.shape, s.dtype) for s in shards]
                  + [pltpu.SemaphoreType.DMA((n, 7)), pltpu.SemaphoreType.DMA((n, 7)), pltpu.SemaphoreType.DMA((n,))],
                  first, last, middle)


def _sibling_rider(grads):
    n = len(grads)

    def copies(g, got, scratch):
        send_sems, recv_sems = scratch
        x, y, c = _pos()
        return [pltpu.make_async_remote_copy(src_ref=g[a].at[2 * q + (1 - c)], dst_ref=got[a].at[q], send_sem=send_sems.at[a, q],
                                             recv_sem=recv_sems.at[a, q], device_id=(x, y, 1 - c), device_id_type=MESH)
                for a in range(n) for q in range(4)]

    def first(g, got, scratch):
        for cp in copies(g, got, scratch):
            cp.start()

    def last(g, got, scratch):
        for cp in copies(g, got, scratch):
            cp.wait()

    return _Rider(grads, [jax.ShapeDtypeStruct((4,) + g.shape[1:], g.dtype) for g in grads],
                  [pltpu.SemaphoreType.DMA((n, 4))] * 2, first, last)


def _chips_rider(sums):
    n = len(sums)

    def copies(s, out, scratch):
        send_sems, recv_sems = scratch
        x, y, c = _pos()
        cps = []
        for a in range(n):
            for f in (1, 2, 3):
                peer = (_flip(x, f >> 1), _flip(y, f & 1), c)
                cps.append(pltpu.make_async_remote_copy(
                    src_ref=s[a].at[2 * peer[0] + peer[1]], dst_ref=out[a].at[f - 1], send_sem=send_sems.at[a, f - 1],
                    recv_sem=recv_sems.at[a, f - 1], device_id=peer, device_id_type=MESH))
        return cps

    def first(s, out, scratch):
        for cp in copies(s, out, scratch):
            cp.start()

    def last(s, out, scratch):
        for cp in copies(s, out, scratch):
            cp.wait()

    return _Rider(sums, [jax.ShapeDtypeStruct((3,) + s.shape[1:], s.dtype) for s in sums],
                  [pltpu.SemaphoreType.DMA((n, 3))] * 2, first, last)


def _last_exchange(g, name):
    _, r, c = g.shape

    def body(g_ref, got_ref, out_ref, a_buf, b_buf, s_buf, load_sems, sib_send, sib_recv, send_sems, recv_sems):
        x, y, cc = _pos()

        def swap(q):
            return pltpu.make_async_remote_copy(src_ref=g_ref.at[2 * q + (1 - cc)], dst_ref=got_ref.at[q], send_sem=sib_send.at[q],
                                                recv_sem=sib_recv.at[q], device_id=(x, y, 1 - cc), device_id_type=MESH)

        for q in range(4):
            swap(q).start()
        copies = []
        for f in (1, 2, 3):
            peer = (_flip(x, f >> 1), _flip(y, f & 1), cc)
            qd = 2 * peer[0] + peer[1]
            mine = pltpu.make_async_copy(g_ref.at[2 * qd + cc], a_buf, load_sems.at[0])
            mine.start()
            swap(qd).wait_recv()
            theirs = pltpu.make_async_copy(got_ref.at[qd], b_buf, load_sems.at[1])
            theirs.start()
            mine.wait()
            theirs.wait()
            s_buf[f - 1] = (a_buf[...] + b_buf[...]).astype(bf16)
            cp = pltpu.make_async_remote_copy(src_ref=s_buf.at[f - 1], dst_ref=out_ref.at[f - 1], send_sem=send_sems.at[f - 1],
                                              recv_sem=recv_sems.at[f - 1], device_id=peer, device_id_type=MESH)
            cp.start()
            copies.append(cp)
        swap(2 * x + y).wait_recv()
        for q in range(4):
            swap(q).wait_send()
        for cp in copies:
            cp.wait()

    return pl.pallas_call(
        body, name=name, in_specs=[ANY], out_specs=[ANY, ANY],
        out_shape=[jax.ShapeDtypeStruct((4, r, c), f32), jax.ShapeDtypeStruct((3, r, c), bf16)],
        scratch_shapes=[pltpu.VMEM((r, c), f32), pltpu.VMEM((r, c), f32), pltpu.VMEM((3, r, c), bf16), pltpu.SemaphoreType.DMA((2,)),
                        pltpu.SemaphoreType.DMA((4,)), pltpu.SemaphoreType.DMA((4,)), pltpu.SemaphoreType.DMA((3,)),
                        pltpu.SemaphoreType.DMA((3,))],
    )(g)


def _both(a, b):
    na = (len(a.ins), len(a.out_shapes), len(a.scratch))

    def split(fa, fb):
        def f(ins, outs, scratch):
            fa(ins[:na[0]], outs[:na[1]], scratch[:na[2]])
            fb(ins[na[0]:], outs[na[1]:], scratch[na[2]:])
        return f

    return _Rider(a.ins + b.ins, a.out_shapes + b.out_shapes, a.scratch + b.scratch, split(a.first, b.first), split(a.last, b.last))


def _alone(rider, name):
    ri, ro = len(rider.ins), len(rider.out_shapes)

    def body(*refs):
        theirs = (refs[:ri], refs[ri:ri + ro], refs[ri + ro:])
        rider.first(*theirs)
        if rider.middle is not None:
            rider.middle(*theirs)
        rider.last(*theirs)

    return pl.pallas_call(body, name=name, in_specs=[ANY] * ri, out_specs=[ANY] * ro, out_shape=rider.out_shapes,
                          scratch_shapes=rider.scratch)(*rider.ins)


def _gather_small(g_w1, g_w2, g_w3, g_lb, g_wn, loss):
    def body(w1_ref, w2_ref, w3_ref, lb_ref, wn_ref, loss_ref, out_ref, pk, send_sems, recv_sems):
        x, y, c = _pos()
        me = 4 * x + 2 * y + c
        pk[...] = jnp.zeros_like(pk)
        pk[0:1, :] = w1_ref[...]
        pk[1:2, :] = w2_ref[...]
        pk[2:3, :] = w3_ref[...]
        pk[3:4, 0:HGRN_W] = lb_ref[...]
        pk[3:4, HGRN_W:2 * HGRN_W] = wn_ref[...]
        pk[4:5, 0:128] = loss_ref[...]
        out_ref[me] = pk[...]
        sends, recvs = [], []
        for k in range(1, N_DEV):
            peer = (_flip(x, k >> 2), _flip(y, (k >> 1) & 1), _flip(c, k & 1))
            cp = pltpu.make_async_remote_copy(src_ref=pk, dst_ref=out_ref.at[me], send_sem=send_sems.at[k - 1],
                                              recv_sem=recv_sems.at[k - 1], device_id=peer, device_id_type=MESH)
            cp.start()
            sends.append(cp)
            recvs.append(pltpu.make_async_remote_copy(src_ref=pk, dst_ref=out_ref.at[4 * peer[0] + 2 * peer[1] + peer[2]],
                                                      send_sem=send_sems.at[k - 1], recv_sem=recv_sems.at[k - 1], device_id=peer,
                                                      device_id_type=MESH))
        for cp in recvs:
            cp.wait_recv()
        for cp in sends:
            cp.wait_send()

    return pl.pallas_call(
        body, name="gather_small", in_specs=[VMEM_SPEC] * 6, out_specs=VMEM_SPEC,
        out_shape=jax.ShapeDtypeStruct((N_DEV, 8, D_MODEL), f32),
        scratch_shapes=[pltpu.VMEM((8, D_MODEL), f32), pltpu.SemaphoreType.DMA((N_DEV - 1,)), pltpu.SemaphoreType.DMA((N_DEV - 1,))],
    )(g_w1, g_w2, g_w3, g_lb, g_wn, loss)


def _row_tile(r):
    return max(t for t in range(8, 257, 8) if r % t == 0)


def _add_sibling(core, g, got, name):
    _, r, c = got.shape
    tr = _row_tile(r)

    def body(core_ref, a_ref, b_ref, o_ref):
        o_ref[...] = (a_ref[...] + b_ref[...]).astype(bf16)

    blk = pl.BlockSpec((1, tr, c), lambda q, i, core_ref: (q, i, 0))
    return pl.pallas_call(
        body, name=name, out_shape=jax.ShapeDtypeStruct(got.shape, bf16),
        grid_spec=pltpu.PrefetchScalarGridSpec(
            num_scalar_prefetch=1, grid=(4, r // tr),
            in_specs=[pl.BlockSpec((1, tr, c), lambda q, i, core_ref: (2 * q + core_ref[0], i, 0)), blk], out_specs=blk),
        compiler_params=_cp("arbitrary", "arbitrary"))(core, g, got)


def _adamw(w, g, m, v):
    m = ADAM_B1 * m + (1.0 - ADAM_B1) * g
    v = ADAM_B2 * v + (1.0 - ADAM_B2) * (g * g)
    m_hat = m / (1.0 - ADAM_B1 ** ADAM_STEP)
    v_hat = v / (1.0 - ADAM_B2 ** ADAM_STEP)
    return -ADAM_LR * (m_hat / (jnp.sqrt(v_hat) + ADAM_EPS) + ADAM_WD * w), m, v


def _adam_shard(where, g, got, pieces, w, m, v, name):
    r, c = w.shape
    tr = _row_tile(r)

    def body(where_ref, g_ref, got_ref, p_ref, w_ref, m_ref, v_ref, g_out, d_out, m_out, v_out):
        gsum = g_ref[0] + got_ref[0]
        for f in range(3):
            gsum = gsum + p_ref[f].astype(f32)
        g_out[...] = gsum
        d_out[...], m_out[...], v_out[...] = _adamw(w_ref[...], gsum, m_ref[...], v_ref[...])

    blk = pl.BlockSpec((tr, c), lambda i, where_ref: (i, 0))
    return pl.pallas_call(
        body, name=name, out_shape=[jax.ShapeDtypeStruct((r, c), f32)] * 4,
        grid_spec=pltpu.PrefetchScalarGridSpec(
            num_scalar_prefetch=1, grid=(r // tr,),
            in_specs=[pl.BlockSpec((1, tr, c), lambda i, where_ref: (where_ref[0], i, 0)),
                      pl.BlockSpec((1, tr, c), lambda i, where_ref: (where_ref[1], i, 0)),
                      pl.BlockSpec((3, tr, c), lambda i, where_ref: (0, i, 0)), blk, blk, blk],
            out_specs=[blk] * 4),
        compiler_params=_cp("arbitrary"),
    )(where, g, got, pieces, w, m, v)


def _small_update(gath, params):
    def body(gath_ref, *refs):
        ins, outs = refs[:15], refs[15:]
        gs = gath_ref[0]
        for k in range(1, N_DEV):
            gs = gs + gath_ref[k]
        outs[0][...] = gs[4:5, 0:128]
        l0, l1 = ins[9][0:1, :], ins[9][1:2, :]
        lb = _sigmoid(l0 - l1)
        d0 = gs[3:4, 0:HGRN_W] * lb * (1.0 - lb)
        first_row = lax.broadcasted_iota(jnp.int32, (2, HGRN_W), 0) == 0
        grads = [gs[0:1, :], gs[1:2, :], gs[2:3, :], jnp.where(first_row, d0, -d0), gs[3:4, HGRN_W:2 * HGRN_W]]
        for i, g in enumerate(grads):
            w_ref, m_ref, v_ref = ins[3 * i:3 * i + 3]
            o = outs[1 + 4 * i:5 + 4 * i]
            o[0][...] = g
            o[1][...], o[2][...], o[3][...] = _adamw(w_ref[...], g, m_ref[...], v_ref[...])

    flat = [a for p in params for a in p]
    out_shape = [jax.ShapeDtypeStruct((1, 128), f32)] + [jax.ShapeDtypeStruct(p[0].shape, f32) for p in params for _ in range(4)]
    outs = pl.pallas_call(body, name="small_update", in_specs=[VMEM_SPEC] * 16, out_specs=[VMEM_SPEC] * 21, out_shape=out_shape)(gath, *flat)
    return outs[0], [outs[1 + 4 * i:5 + 4 * i] for i in range(5)]


def kernel(x, norm1_w, w_in, lb_logits, hgrn_norm_w, w_out, norm2_w, w_gate_up, w_down, final_norm_w, loss_target, m_norm1_w, m_w_in, m_lb_logits, m_hgrn_norm_w, m_w_out, m_norm2_w, m_w_gate_up, m_w_down, m_final_norm_w, v_norm1_w, v_w_in, v_lb_logits, v_hgrn_norm_w, v_w_out, v_norm2_w, v_w_gate_up, v_w_down, v_final_norm_w):
    row = lambda a: a.reshape(1, D_MODEL)
    ix, iy, ic = lax.axis_index("x"), lax.axis_index("y"), lax.axis_index("c")
    core = jnp.stack([ic]).astype(jnp.int32)
    where = jnp.stack([4 * ix + 2 * iy + ic, 2 * ix + iy]).astype(jnp.int32)
    xs, tgt, w3 = x[0], loss_target[0], row(final_norm_w)
    S = xs.shape[0]

    cos_t, sg_t, win_g = _rope_tables(S, _gather_rider([w_in[0].astype(bf16)]))
    u, qkv, hp, win = _in_proj(xs, norm1_w, win_g, cos_t, sg_t)
    ya, lse, wout_g, wgu_g, wdown_g = _attn_fwd(qkv, _gather_rider([w_out[0].astype(bf16), w_gate_up[0].astype(bf16),
                                                                     w_down[0].astype(bf16)]))
    wout = wout_g.reshape(D_MODEL, D_MODEL)
    wdown = wdown_g.reshape(FFN, D_MODEL)
    yb, o_sav, states = _hgrn_fwd(hp, lb_logits, hgrn_norm_w)
    h1, u2, mixed = _out_proj(xs, ya, yb, wout, norm2_w)
    silu, up_dsilu, act, wgate, wup = _gate_up(u2, wgu_g)
    dh2, loss_p, g_w3 = _down_loss(act, wdown, h1, tgt, w3)

    (g_wdown,) = _gw(act, [dh2], 512, "gw_down")
    dgu2 = _dact(dh2, wdown, silu, up_dsilu)
    early = [_gw_by_owner(u2, dgu2, 2 * FFN // N_DEV, "gw_gate_up", 2048), g_wdown.reshape(N_DEV, FFN // N_DEV, D_MODEL)]
    dh1, g_w2, dmix, *got_early = _dgu(dgu2, wgate, wup, h1, norm2_w, dh2, wout, _sibling_rider(early))
    sums_early = [_add_sibling(core, g, o, f"add_sibling_{i}") for i, (g, o) in enumerate(zip(early, got_early))]
    (g_wout,) = _gw(mixed, [dh1], 1024, "gw_out")
    mid = [g_wout.reshape(N_DEV, D_MODEL // N_DEV, D_MODEL)]
    dhq, dhf, dhi, dhg, g_wn, g_lb, *rode = _hgrn_bwd(hp, lb_logits, hgrn_norm_w, o_sav, states, dmix,
                                                      _both(_chips_rider(sums_early), _sibling_rider(mid)))
    pieces_early, got_mid = rode[:2], rode[2:]
    sums_mid = [_add_sibling(core, mid[0], got_mid[0], "add_sibling_2")]
    dq, dk, dv, *pieces_mid = _attn_bwd(qkv, ya, lse, dmix, _chips_rider(sums_mid))
    dproj, gx, g_w1 = _din(dq, dk, dv, dhq, dhf, dhi, dhg, cos_t, sg_t, win, xs, norm1_w, dh1)
    late = [_gw_by_owner(u, dproj[None], IN_W // N_DEV, "gw_in", 2048)]
    got_late, pieces_late = ([a] for a in _last_exchange(late[0], "last_exchange"))

    grads = [late[0], mid[0], early[0], early[1]]
    got = [got_late[0], got_mid[0], got_early[0], got_early[1]]
    pieces = [pieces_late[0], pieces_mid[0], pieces_early[0], pieces_early[1]]
    shards = [w_in[0], w_out[0], w_gate_up[0], w_down[0]]
    moms = [(m_w_in[0], v_w_in[0]), (m_w_out[0], v_w_out[0]), (m_w_gate_up[0], v_w_gate_up[0]), (m_w_down[0], v_w_down[0])]
    big = [_adam_shard(where, g, o, p, w, m, v, f"adam_{i}")
           for i, (g, o, p, w, (m, v)) in enumerate(zip(grads, got, pieces, shards, moms))]
    big = [[a[None] for a in four] for four in big]

    gath = _gather_small(g_w1, g_w2, g_w3, g_lb, g_wn, loss_p)
    params = [(norm1_w, m_norm1_w, v_norm1_w), (norm2_w, m_norm2_w, v_norm2_w),
              (row(final_norm_w), row(m_final_norm_w), row(v_final_norm_w)),
              (lb_logits, m_lb_logits, v_lb_logits), (hgrn_norm_w, m_hgrn_norm_w, v_hgrn_norm_w)]
    loss, (s_w1, s_w2, s_w3, s_lb, s_wn) = _small_update(gath, params)
    s_w3 = [a.reshape(D_MODEL) for a in s_w3]
    per_w = [s_w1, big[0], s_lb, s_wn, big[1], s_w2, big[2], big[3], s_w3]
    return (loss[0, 0], gx[None], *[p[0] for p in per_w], *[p[1] for p in per_w], *[p[2] for p in per_w], *[p[3] for p in per_w])
```

```python
import jax
import jax.numpy as jnp
from jax import lax
from jax.experimental import pallas as pl
from jax.experimental.pallas import tpu as pltpu

f32, bf16 = jnp.float32, jnp.bfloat16

D_MODEL = 1024
ATTN_W = 512
HEAD_DIM = 64
ATTN_BLK = 128
DILATIONS = (1, 4, 16)
HGRN_W = 512
HGRN_HD = 128
CHUNK = 64
IN_W = 3 * ATTN_W + 4 * HGRN_W
FFN = 2816
EPS = 1e-6
ROPE_THETA = 10000.0
NEG = -1e30
N_DEV = 8
ADAM_LR, ADAM_B1, ADAM_B2, ADAM_EPS, ADAM_WD, ADAM_STEP = 0.001, 0.9, 0.999, 1e-08, 0.01, 10
VMEM_LIMIT = 56 * 1024 * 1024


def _cp(*sem):
    return pltpu.CompilerParams(dimension_semantics=sem, vmem_limit_bytes=VMEM_LIMIT)


def _dot(a, b):
    return jnp.dot(a, b, preferred_element_type=f32)


def _dot_nt(a, b):
    return lax.dot_general(a, b, (((1,), (1,)), ((), ())), preferred_element_type=f32)


def _dot_tn(a, b):
    return lax.dot_general(a, b, (((0,), (0,)), ((), ())), preferred_element_type=f32)


def _sigmoid(x):
    return 0.5 * jnp.tanh(0.5 * x) + 0.5


class _Rider:
    def __init__(self, ins, out_shapes, scratch, first, last, middle=None):
        self.ins, self.out_shapes, self.scratch = list(ins), list(out_shapes), list(scratch)
        self.first, self.middle, self.last = first, middle, last


def _ride(call, rider, body, step, n_steps, n_in, n_out, n_scratch):
    if rider is None:
        return call, body, []
    ri, ro = len(rider.ins), len(rider.out_shapes)
    any_spec = pl.BlockSpec(memory_space=pl.ANY)
    call = dict(call, in_specs=call["in_specs"] + [any_spec] * ri, out_specs=call["out_specs"] + [any_spec] * ro,
                out_shape=call["out_shape"] + rider.out_shapes, scratch_shapes=call["scratch_shapes"] + rider.scratch)

    def riding(*refs):
        a = n_in + ri
        b = a + n_out + ro
        mine = refs[:n_in] + refs[a:a + n_out] + refs[b:b + n_scratch]
        theirs = (refs[n_in:a], refs[a + n_out:b], refs[b + n_scratch:])
        t = step()

        @pl.when(t == 0)
        def _():
            rider.first(*theirs)

        body(*mine)
        if rider.middle is not None:
            @pl.when(t == n_steps // 2)
            def _():
                rider.middle(*theirs)

        @pl.when(t == n_steps - 1)
        def _():
            rider.last(*theirs)

    return call, riding, rider.ins


def _rope_tables(S, rider=None):
    half = HEAD_DIM // 2
    tm = 256
    inv_freq = jnp.tile(ROPE_THETA ** (-jnp.arange(half, dtype=f32) / half), 128 // half).reshape(1, 128)
    sign = jnp.tile(jnp.concatenate([-jnp.ones((half,), f32), jnp.ones((half,), f32)]), 128 // HEAD_DIM).reshape(1, 128)

    def body(inv_ref, sign_ref, cos_ref, sg_ref):
        pos = (lax.broadcasted_iota(jnp.int32, (tm, 128), 0) + pl.program_id(0) * tm).astype(f32)
        ang = pos * inv_ref[...]
        cos_ref[...] = jnp.cos(ang)
        sg_ref[...] = jnp.sin(ang) * sign_ref[...]

    vec = pl.BlockSpec((1, 128), lambda i: (0, 0))
    out = pl.BlockSpec((tm, 128), lambda i: (i, 0))
    call = dict(in_specs=[vec, vec], out_specs=[out, out], out_shape=[jax.ShapeDtypeStruct((S, 128), f32)] * 2, scratch_shapes=[])
    call, body, more = _ride(call, rider, body, lambda: pl.program_id(0), S // tm, 2, 2, 0)
    return pl.pallas_call(body, name="rope_tables", grid=(S // tm,), compiler_params=_cp("arbitrary"), **call)(inv_freq, sign, *more)


def _swap_halves(v):
    n = v.shape[1]
    lane = lax.broadcasted_iota(jnp.int32, v.shape, 1)
    return jnp.where((lane % HEAD_DIM) < HEAD_DIM // 2, pltpu.roll(v, n - HEAD_DIM // 2, 1), pltpu.roll(v, HEAD_DIM // 2, 1))


def _in_proj(x, w1, win_g, cos_t, sg_t):
    S = x.shape[0]
    tm = 512
    w = IN_W // N_DEV

    def body(x_ref, w1_ref, wg_ref, cos_ref, sg_ref, u_ref, qkv_ref, hp_ref, w_ref):
        @pl.when(pl.program_id(0) == 0)
        def _():
            for d in range(N_DEV):
                w_ref[:, w * d:w * (d + 1)] = wg_ref[d]

        xv = x_ref[...]
        r = lax.rsqrt(jnp.mean(xv * xv, axis=-1, keepdims=True) + EPS)
        u = (xv * r * w1_ref[...]).astype(bf16)
        u_ref[...] = u
        cosv, sgv = jnp.tile(cos_ref[...], (1, ATTN_W // 128)), jnp.tile(sg_ref[...], (1, ATTN_W // 128))
        for j in range(3):
            pj = _dot(u, w_ref[:, j * ATTN_W:(j + 1) * ATTN_W])
            if j < 2:
                pj = pj * cosv + _swap_halves(pj) * sgv
            if j == 0:
                pj = pj * (HEAD_DIM ** -0.5)
            qkv_ref[:, j * ATTN_W:(j + 1) * ATTN_W] = pj.astype(bf16)
        for j in range(4):
            lo = 3 * ATTN_W + j * HGRN_W
            hp_ref[:, j * HGRN_W:(j + 1) * HGRN_W] = _dot(u, w_ref[:, lo:lo + HGRN_W])

    return pl.pallas_call(
        body, name="in_proj", grid=(S // tm,),
        in_specs=[pl.BlockSpec((tm, D_MODEL), lambda i: (i, 0)), pl.BlockSpec((1, D_MODEL), lambda i: (0, 0)),
                  pl.BlockSpec((N_DEV, D_MODEL, w), lambda i: (0, 0, 0)),
                  pl.BlockSpec((tm, 128), lambda i: (i, 0)), pl.BlockSpec((tm, 128), lambda i: (i, 0))],
        out_specs=[pl.BlockSpec((tm, D_MODEL), lambda i: (i, 0)), pl.BlockSpec((tm, 3 * ATTN_W), lambda i: (i, 0)),
                   pl.BlockSpec((tm, 4 * HGRN_W), lambda i: (i, 0)), pl.BlockSpec((D_MODEL, IN_W), lambda i: (0, 0))],
        out_shape=[jax.ShapeDtypeStruct((S, D_MODEL), bf16), jax.ShapeDtypeStruct((S, 3 * ATTN_W), bf16),
                   jax.ShapeDtypeStruct((S, 4 * HGRN_W), f32), jax.ShapeDtypeStruct((D_MODEL, IN_W), bf16)],
        compiler_params=_cp("arbitrary"),
    )(x, w1, win_g, cos_t, sg_t)


def _head_masks():
    lane = lax.broadcasted_iota(jnp.int32, (ATTN_BLK, 128), 1)
    even = lane < HEAD_DIM
    return even, (even, jnp.logical_not(even))


def _pair_fwd(q2, k2, v2, bias):
    even, masks = _head_masks()
    outs, lses = [], []
    for e in range(2):
        qm = jnp.where(masks[e], q2, 0.0).astype(bf16)
        s = _dot_nt(qm, k2) + bias
        m = jnp.max(s, axis=-1, keepdims=True)
        pe = jnp.exp(s - m)
        lsum = jnp.sum(pe, axis=-1, keepdims=True)
        outs.append(_dot(pe.astype(bf16), v2) / lsum)
        lses.append(jnp.broadcast_to(m + jnp.log(lsum), (ATTN_BLK, 128)))
    return jnp.where(even, outs[0], outs[1]), jnp.where(even, lses[0], lses[1])


def _merge(y0, l0, y1, l1):
    mx = jnp.maximum(l0, l1)
    a, b = jnp.exp(l0 - mx), jnp.exp(l1 - mx)
    tot = a + b
    return (a * y0 + b * y1) / tot, mx + jnp.log(tot)


def _pair_bwd(q2, k2f, v2, dy2, lse2, delta2, bias):
    _, masks = _head_masks()
    k2 = k2f.astype(bf16)
    klane = lax.broadcasted_iota(jnp.int32, (2 * ATTN_BLK, 128), 1) < HEAD_DIM
    kmasks = (klane, jnp.logical_not(klane))
    dq2 = jnp.zeros((ATTN_BLK, 128), f32)
    pes, dss, qms, dyms = [], [], [], []
    for e in range(2):
        c0 = e * HEAD_DIM
        qm = jnp.where(masks[e], q2, 0.0).astype(bf16)
        km = jnp.where(kmasks[e], k2f, 0.0).astype(bf16)
        dym = jnp.where(masks[e], dy2, 0.0).astype(bf16)
        pe = jnp.exp(_dot_nt(qm, k2) + bias - lse2[:, c0:c0 + 1])
        ds = (pe * (_dot_nt(dym, v2) - delta2[:, c0:c0 + 1])).astype(bf16)
        dq2 = dq2 + _dot(ds, km)
        pes.append(pe.astype(bf16))
        dss.append(ds)
        qms.append(qm)
        dyms.append(dym)
    dv2 = _dot_tn(jnp.concatenate(pes, axis=0), jnp.concatenate(dyms, axis=0))
    dk2 = _dot_tn(jnp.concatenate(dss, axis=0), jnp.concatenate(qms, axis=0))
    return dq2, dk2, dv2


TOK = 2048


def _key_bias():
    qi = lax.broadcasted_iota(jnp.int32, (ATTN_BLK, 2 * ATTN_BLK), 0)
    kj = lax.broadcasted_iota(jnp.int32, (ATTN_BLK, 2 * ATTN_BLK), 1)
    delta = ATTN_BLK + qi - kj
    seen = (delta >= 0) & (delta <= ATTN_BLK)
    return jnp.where(seen, 0.0, NEG), jnp.where(seen & (kj >= ATTN_BLK), 0.0, NEG)


def _attn_fwd(qkv, rider=None):
    S = qkv.shape[0]
    nS = S // TOK

    def body(q_ref, kp_ref, kc_ref, vp_ref, vc_ref, y_ref, l_ref, qs, k2, v2, ay, al):
        n = pl.program_id(1)
        qs[...] = q_ref[...].astype(f32)
        k2[0:TOK] = kp_ref[...].astype(f32)
        k2[TOK:2 * TOK] = kc_ref[...].astype(f32)
        v2[0:TOK] = vp_ref[...].astype(f32)
        v2[TOK:2 * TOK] = vc_ref[...].astype(f32)
        bias_any, bias_first = _key_bias()

        def block(dil, r, b, step, last):
            start = r + pl.multiple_of(step * b, step)
            rows = pl.ds(start, ATTN_BLK, stride=dil) if dil > 1 else pl.ds(start, ATTN_BLK)
            keys = (pl.ds(TOK + start - step, 2 * ATTN_BLK, stride=dil) if dil > 1
                    else pl.ds(TOK + start - step, 2 * ATTN_BLK))
            bias = jnp.where((n == 0) & (b == 0), bias_first, bias_any)
            out, lse = _pair_fwd(qs[rows, :], k2[keys, :].astype(bf16), v2[keys, :].astype(bf16), bias)
            if dil < DILATIONS[-1]:
                out, lse = _merge(ay[rows, :], al[rows, :], out, lse)
            if last:
                y_ref[rows, :] = out
                l_ref[rows, :] = lse
            else:
                ay[rows, :] = out
                al[rows, :] = lse

        for dil in reversed(DILATIONS):
            def loop(i, carry, dil=dil):
                block(dil, i % dil, i // dil, ATTN_BLK * dil, dil == 1)
                return carry
            lax.fori_loop(0, TOK // ATTN_BLK, loop, 0, unroll=True)

    blk = (TOK, 128)
    cur = lambda c: pl.BlockSpec(blk, lambda p, n: (n, 4 * c + p))
    prv = lambda c: pl.BlockSpec(blk, lambda p, n: (jnp.maximum(n - 1, 0), 4 * c + p))
    out = pl.BlockSpec(blk, lambda p, n: (n, p))
    call = dict(in_specs=[cur(0), prv(1), cur(1), prv(2), cur(2)], out_specs=[out, out],
                out_shape=[jax.ShapeDtypeStruct((S, ATTN_W), f32)] * 2,
                scratch_shapes=[pltpu.VMEM(blk, f32), pltpu.VMEM((2 * TOK, 128), f32), pltpu.VMEM((2 * TOK, 128), f32),
                                pltpu.VMEM(blk, f32), pltpu.VMEM(blk, f32)])
    call, body, more = _ride(call, rider, body, lambda: pl.program_id(0) * nS + pl.program_id(1), (ATTN_W // 128) * nS, 5, 2, 5)
    return pl.pallas_call(body, name="attention_fwd", grid=(ATTN_W // 128, nS), compiler_params=_cp("arbitrary", "arbitrary"),
                          **call)(qkv, qkv, qkv, qkv, qkv, *more)


def _attn_bwd(qkv, ya, lse, dmix, rider=None):
    S = qkv.shape[0]
    nS = S // TOK

    def body(q_ref, kp_ref, kc_ref, vp_ref, vc_ref, y_ref, l_ref, dy_ref, dq_ref, dk_ref, dv_ref, qs, k2, v2, dk2, dv2, dqa, dl):
        n = pl.program_id(1)

        @pl.when(n == 0)
        def _():
            dk2[...] = jnp.zeros_like(dk2)
            dv2[...] = jnp.zeros_like(dv2)

        @pl.when(n < nS)
        def _():
            qs[...] = q_ref[...].astype(f32)
            k2[0:TOK] = kp_ref[...].astype(f32)
            k2[TOK:2 * TOK] = kc_ref[...].astype(f32)
            v2[0:TOK] = vp_ref[...].astype(f32)
            v2[TOK:2 * TOK] = vc_ref[...].astype(f32)
            li = lax.broadcasted_iota(jnp.int32, (128, 128), 0)
            lj = lax.broadcasted_iota(jnp.int32, (128, 128), 1)
            seg = jnp.where((li // HEAD_DIM) == (lj // HEAD_DIM), 1.0, 0.0).astype(bf16)
            bias_any, bias_first = _key_bias()

            def delta_rows(t, carry):
                rows = pl.ds(pl.multiple_of(256 * t, 256), 256)
                dyy = dy_ref[rows, :] * y_ref[rows, :]
                hi = dyy.astype(bf16)
                dl[rows, :] = _dot(hi, seg) + _dot((dyy - hi.astype(f32)).astype(bf16), seg)
                return carry

            lax.fori_loop(0, TOK // 256, delta_rows, 0)

            def block(dil, r, b, step, first_pattern, last):
                start = r + pl.multiple_of(step * b, step)
                rows = pl.ds(start, ATTN_BLK, stride=dil) if dil > 1 else pl.ds(start, ATTN_BLK)
                keys = (pl.ds(TOK + start - step, 2 * ATTN_BLK, stride=dil) if dil > 1
                        else pl.ds(TOK + start - step, 2 * ATTN_BLK))
                bias = jnp.where((n == 0) & (b == 0), bias_first, bias_any)
                dq2, dkk, dvv = _pair_bwd(qs[rows, :], k2[keys, :], v2[keys, :].astype(bf16), dy_ref[rows, :],
                                          l_ref[rows, :], dl[rows, :], bias)
                if last:
                    dq_ref[rows, :] = dqa[rows, :] + dq2
                elif first_pattern:
                    dqa[rows, :] = dq2
                else:
                    dqa[rows, :] += dq2
                dk2[keys, :] += dkk
                dv2[keys, :] += dvv

            for dil in reversed(DILATIONS):
                def loop(i, carry, dil=dil):
                    block(dil, i % dil, i // dil, ATTN_BLK * dil, dil == DILATIONS[-1], dil == 1)
                    return carry
                lax.fori_loop(0, TOK // ATTN_BLK, loop, 0, unroll=True)

        dk_ref[...] = dk2[0:TOK]
        dv_ref[...] = dv2[0:TOK]
        dk2[0:TOK] = dk2[TOK:2 * TOK]
        dv2[0:TOK] = dv2[TOK:2 * TOK]
        dk2[TOK:2 * TOK] = jnp.zeros((TOK, 128), f32)
        dv2[TOK:2 * TOK] = jnp.zeros((TOK, 128), f32)

    blk = (TOK, 128)
    cn = lambda n: jnp.minimum(n, nS - 1)
    pn = lambda n: jnp.clip(n - 1, 0, nS - 1)
    cur = lambda c: pl.BlockSpec(blk, lambda p, n: (cn(n), 4 * c + p))
    prv = lambda c: pl.BlockSpec(blk, lambda p, n: (pn(n), 4 * c + p))
    at_n = pl.BlockSpec(blk, lambda p, n: (cn(n), p))
    at_p = pl.BlockSpec(blk, lambda p, n: (pn(n), p))
    big = lambda: pltpu.VMEM((2 * TOK, 128), f32)
    call = dict(in_specs=[cur(0), prv(1), cur(1), prv(2), cur(2), at_n, at_n, at_n], out_specs=[at_n, at_p, at_p],
                out_shape=[jax.ShapeDtypeStruct((S, ATTN_W), f32)] * 3,
                scratch_shapes=[pltpu.VMEM(blk, f32), big(), big(), big(), big(), pltpu.VMEM(blk, f32), pltpu.VMEM(blk, f32)])
    call, body, more = _ride(call, rider, body, lambda: pl.program_id(0) * (nS + 1) + pl.program_id(1),
                             (ATTN_W // 128) * (nS + 1), 8, 3, 7)
    return pl.pallas_call(body, name="attention_bwd", grid=(ATTN_W // 128, nS + 1), compiler_params=_cp("arbitrary", "arbitrary"),
                          **call)(qkv, qkv, qkv, qkv, qkv, ya, lse, dmix, *more)


HG_T = 512
N_HH = HGRN_W // HGRN_HD
HG_SUB = 128
SAFE_RANGE = 75.0


def _row_in_chunk():
    return lax.broadcasted_iota(jnp.int32, (HG_T, HGRN_HD), 0) % CHUNK


def _chunk_cumsum(v, rc):
    k = 1
    while k < CHUNK:
        v = v + jnp.where(rc >= k, pltpu.roll(v, k, 0), 0.0)
        k *= 2
    return v


def _chunk_rcumsum(v, rc):
    k = 1
    while k < CHUNK:
        v = v + jnp.where(rc < CHUNK - k, pltpu.roll(v, HG_T - k, 0), 0.0)
        k *= 2
    return v


def _hgrn_gates(qb, fb, lb):
    sf = _sigmoid(fb)
    f = lb + (1.0 - lb) * sf
    sq = _sigmoid(qb)
    return sf, f, jnp.log(f), 1.0 - f, sq, qb * sq


def _hgrn_prep(qb, fb, lbl2, rc):
    lb = _sigmoid(lbl2[0:1, :] - lbl2[1:2, :])
    sf, f, lf, key, sq, qf = _hgrn_gates(qb, fb, lb)
    b = _chunk_cumsum(lf, rc)
    rem = _chunk_rcumsum(lf, rc) - lf
    return dict(lb=lb, sf=sf, f=f, key=key, sq=sq, qf=qf, b=b, rem=rem, eb=jnp.exp(b), er=jnp.exp(rem))


def _chunk_mask():
    r = lax.broadcasted_iota(jnp.int32, (HG_SUB, HG_SUB), 0)
    c = lax.broadcasted_iota(jnp.int32, (HG_SUB, HG_SUB), 1)
    return ((r // CHUNK) == (c // CHUNK)) & (c <= r)


def _hgrn_fwd(hp, lbl, wn):
    S = hp.shape[0]
    nT = S // HG_T

    def body(qb_ref, fb_ref, ib_ref, gb_ref, lbl_ref, wn_ref, yb_ref, o_ref, st_ref, ST, qt_s, kh_s, dec_s, oi_s):
        @pl.when(pl.program_id(0) == 0)
        def _():
            ST[...] = jnp.zeros_like(ST)

        rc = _row_in_chunk()
        for h in range(N_HH):
            sl = slice(HGRN_HD * h, HGRN_HD * (h + 1))
            p = _hgrn_prep(qb_ref[:, sl], fb_ref[:, sl], lbl_ref[:, sl], rc)
            qf, key, b = p["qf"], p["key"], p["b"]
            qt = qf * p["eb"]
            qt_s[:, sl] = qt.astype(bf16)
            kh_s[:, sl] = (key * p["er"]).astype(bf16)
            dec_s[:, sl] = jnp.exp(b + p["rem"])
            rng = jnp.max(-(b + p["rem"]))

            @pl.when(rng < SAFE_RANGE)
            def _():
                kp = (key * jnp.exp(-b)).astype(bf16)
                cmask = _chunk_mask()
                for j in range(HG_T // HG_SUB):
                    rs = slice(HG_SUB * j, HG_SUB * (j + 1))
                    sc = jnp.where(cmask, _dot_nt(qt[rs].astype(bf16), kp[rs]), 0.0).astype(bf16)
                    oi_s[rs, sl] = _dot(sc, ib_ref[rs, sl].astype(bf16))

            @pl.when(rng >= SAFE_RANGE)
            def _():
                v = ib_ref[:, sl]
                ones = jnp.ones((HGRN_HD, HGRN_HD), bf16)

                def lag(l, o):
                    e = jnp.exp(jnp.where(rc >= l, b - pltpu.roll(b, l, 0), NEG))
                    pr = qf * pltpu.roll(key, l, 0) * e
                    return o + _dot(pr.astype(bf16), ones) * pltpu.roll(v, l, 0)

                oi_s[:, sl] = lax.fori_loop(1, CHUNK, lag, _dot((qf * key).astype(bf16), ones) * v)

        def step(c, carry):
            rows = pl.ds(pl.multiple_of(c * CHUNK, CHUNK), CHUNK)
            row0 = pl.ds(pl.multiple_of(c * CHUNK, CHUNK), 1)
            for h in range(N_HH):
                sl = slice(HGRN_HD * h, HGRN_HD * (h + 1))
                stv = ST[h]
                st_ref[c, sl, :] = stv
                oi_s[rows, sl] += _dot_nt(qt_s[rows, sl], stv.astype(bf16))
                ST[h] = stv * dec_s[row0, sl] + _dot_tn(ib_ref[rows, sl].astype(bf16), kh_s[rows, sl])
            return carry

        lax.fori_loop(0, HG_T // CHUNK, step, 0, unroll=True)

        for h in range(N_HH):
            sl = slice(HGRN_HD * h, HGRN_HD * (h + 1))
            o = oi_s[:, sl]
            o_ref[:, sl] = o
            on = o * lax.rsqrt(jnp.mean(o * o, axis=-1, keepdims=True) + EPS)
            g = gb_ref[:, sl]
            yb_ref[:, sl] = on * wn_ref[:, sl] * (g * _sigmoid(g))

    col = lambda c: pl.BlockSpec((HG_T, HGRN_W), lambda i: (i, c))
    tile = pl.BlockSpec((HG_T, HGRN_W), lambda i: (i, 0))
    whole = lambda a: pl.BlockSpec(a.shape, lambda i: (0, 0))
    return pl.pallas_call(
        body, name="hgrn_fwd", grid=(nT,),
        in_specs=[col(0), col(1), col(2), col(3), whole(lbl), whole(wn)],
        out_specs=[tile, tile, pl.BlockSpec((HG_T // CHUNK, HGRN_W, HGRN_HD), lambda i: (i, 0, 0))],
        out_shape=[jax.ShapeDtypeStruct((S, HGRN_W), f32), jax.ShapeDtypeStruct((S, HGRN_W), f32),
                   jax.ShapeDtypeStruct((S // CHUNK, HGRN_W, HGRN_HD), f32)],
        scratch_shapes=[pltpu.VMEM((N_HH, HGRN_HD, HGRN_HD), f32), pltpu.VMEM((HG_T, HGRN_W), bf16),
                        pltpu.VMEM((HG_T, HGRN_W), bf16), pltpu.VMEM((HG_T, HGRN_W), f32), pltpu.VMEM((HG_T, HGRN_W), f32)],
        compiler_params=_cp("arbitrary"),
    )(hp, hp, hp, hp, lbl, wn)


def _hgrn_bwd(hp, lbl, wn, o_sav, states, dmix, rider=None):
    S = hp.shape[0]
    nT = S // HG_T

    def body(qb_ref, fb_ref, ib_ref, gb_ref, lbl_ref, wn_ref, o_ref, st_ref, dy_ref,
             dq_ref, df_ref, di_ref, dg_ref, gwn_ref, glb_ref,
             DST, qt_s, kh_s, dec_s, do_s, dqt_s, dkh_s, dbl_s, dvi_s, dqi_s, dki_s, dbi_s):
        @pl.when(pl.program_id(0) == 0)
        def _():
            DST[...] = jnp.zeros_like(DST)
            gwn_ref[...] = jnp.zeros_like(gwn_ref)
            glb_ref[...] = jnp.zeros_like(glb_ref)

        rc = _row_in_chunk()
        preps = []
        for h in range(N_HH):
            sl = slice(HGRN_HD * h, HGRN_HD * (h + 1))
            p = _hgrn_prep(qb_ref[:, sl], fb_ref[:, sl], lbl_ref[:, sl], rc)
            preps.append(p)
            qf, key, b = p["qf"], p["key"], p["b"]
            v = ib_ref[:, sl]
            o = o_ref[:, sl]
            rinv = lax.rsqrt(jnp.mean(o * o, axis=-1, keepdims=True) + EPS)
            on = o * rinv
            g = gb_ref[:, sl]
            sgm = _sigmoid(g)
            silu_g = g * sgm
            dy = dy_ref[:, sl]
            wn_v = wn_ref[:, sl]
            gwn_ref[:, sl] += jnp.sum(dy * on * silu_g, axis=0, keepdims=True)
            dg_ref[:, sl] = (dy * on * wn_v * (sgm * (1.0 + g * (1.0 - sgm)))).astype(bf16)
            t1 = dy * wn_v * silu_g
            do = rinv * (t1 - on * jnp.mean(t1 * on, axis=-1, keepdims=True))
            do_s[:, sl] = do.astype(bf16)
            qt = qf * p["eb"]
            qt_s[:, sl] = qt.astype(bf16)
            kh_s[:, sl] = (key * p["er"]).astype(bf16)
            dec_s[:, sl] = jnp.exp(b + p["rem"])
            rng = jnp.max(-(b + p["rem"]))

            @pl.when(rng < SAFE_RANGE)
            def _():
                einv = jnp.exp(-b)
                kp = (key * einv).astype(bf16)
                cmask = _chunk_mask()
                for j in range(HG_T // HG_SUB):
                    rs = slice(HG_SUB * j, HG_SUB * (j + 1))
                    qtb, dob, vb = qt[rs].astype(bf16), do[rs].astype(bf16), v[rs].astype(bf16)
                    sc = jnp.where(cmask, _dot_nt(qtb, kp[rs]), 0.0).astype(bf16)
                    dsc = jnp.where(cmask, _dot_nt(dob, vb), 0.0).astype(bf16)
                    dqp = _dot(dsc, kp[rs])
                    dkp = _dot_tn(dsc, qtb)
                    dvi_s[rs, sl] = _dot_tn(sc, dob)
                    dqi_s[rs, sl] = dqp * p["eb"][rs]
                    dki_s[rs, sl] = dkp * einv[rs]
                    dbi_s[rs, sl] = dqp * qtb.astype(f32) - dkp * kp[rs].astype(f32)

            @pl.when(rng >= SAFE_RANGE)
            def _():
                ones = jnp.ones((HGRN_HD, HGRN_HD), bf16)

                def lag(l, carry):
                    dqf, dkey, db, dv = carry
                    e = jnp.exp(jnp.where(rc >= l, b - pltpu.roll(b, l, 0), NEG))
                    ks, vs, qe = pltpu.roll(key, l, 0), pltpu.roll(v, l, 0), qf * e
                    pr = qe * ks
                    rl = _dot(pr.astype(bf16), ones)
                    drl = jnp.where(rc >= l, _dot((do * vs).astype(bf16), ones), 0.0)
                    gl = drl * pr
                    back = HG_T - l
                    return (dqf + drl * ks * e, dkey + pltpu.roll(drl * qe, back, 0), db + gl - pltpu.roll(gl, back, 0),
                            dv + pltpu.roll(rl * do, back, 0))

                rl0 = _dot((qf * key).astype(bf16), ones)
                drl0 = _dot((do * v).astype(bf16), ones)
                dqf, dkey, db, dv = lax.fori_loop(1, CHUNK, lag, (drl0 * key, drl0 * qf, jnp.zeros((HG_T, HGRN_HD), f32), rl0 * do))
                dvi_s[:, sl] = dv
                dqi_s[:, sl] = dqf
                dki_s[:, sl] = dkey
                dbi_s[:, sl] = db

        def step(k, carry):
            c = HG_T // CHUNK - 1 - k
            rows = pl.ds(pl.multiple_of(c * CHUNK, CHUNK), CHUNK)
            row0 = pl.ds(pl.multiple_of(c * CHUNK, CHUNK), 1)
            for h in range(N_HH):
                sl = slice(HGRN_HD * h, HGRN_HD * (h + 1))
                stp = st_ref[c, sl, :]
                dst = DST[h]
                dstb = dst.astype(bf16)
                dob = do_s[rows, sl]
                khb = kh_s[rows, sl]
                dec = dec_s[row0, sl]
                dqt_s[rows, sl] = _dot(dob, stp.astype(bf16))
                dkh = _dot(ib_ref[rows, sl].astype(bf16), dstb)
                dkh_s[rows, sl] = dkh
                dvi_s[rows, sl] += _dot_nt(khb, dstb)
                dbl = jnp.sum(dst * stp, axis=0, keepdims=True) * dec + jnp.sum(dkh * khb.astype(f32), axis=0, keepdims=True)
                dbl_s[rows, sl] = jnp.broadcast_to(dbl, (CHUNK, HGRN_HD))
                DST[h] = dst * dec + _dot_tn(dob, qt_s[rows, sl])
            return carry

        lax.fori_loop(0, HG_T // CHUNK, step, 0, unroll=True)

        for h in range(N_HH):
            sl = slice(HGRN_HD * h, HGRN_HD * (h + 1))
            qb = qb_ref[:, sl]
            p = preps[h]
            sf, sq, lb = p["sf"], p["sq"], p["lb"]
            dqt, dkh = dqt_s[:, sl], dkh_s[:, sl]
            dqf = dqt * p["eb"] + dqi_s[:, sl]
            dkey = dkh * p["er"] + dki_s[:, sl]
            db = dqt * (p["qf"] * p["eb"]) - dkh * (p["key"] * p["er"]) + jnp.where(rc == CHUNK - 1, dbl_s[:, sl], 0.0) + dbi_s[:, sl]
            df = _chunk_rcumsum(db, rc) / p["f"] - dkey
            df_ref[:, sl] = (df * (1.0 - lb) * sf * (1.0 - sf)).astype(bf16)
            glb_ref[:, sl] += jnp.sum(df * (1.0 - sf), axis=0, keepdims=True)
            dq_ref[:, sl] = (dqf * (sq * (1.0 + qb * (1.0 - sq)))).astype(bf16)
            di_ref[:, sl] = dvi_s[:, sl].astype(bf16)

    rev = lambda i: nT - 1 - i
    col = lambda c: pl.BlockSpec((HG_T, HGRN_W), lambda i: (rev(i), c))
    tile = pl.BlockSpec((HG_T, HGRN_W), lambda i: (rev(i), 0))
    whole = lambda a: pl.BlockSpec(a.shape, lambda i: (0, 0))
    vec = pl.BlockSpec((1, HGRN_W), lambda i: (0, 0))
    tb = lambda: pltpu.VMEM((HG_T, HGRN_W), bf16)
    tf = lambda: pltpu.VMEM((HG_T, HGRN_W), f32)
    call = dict(in_specs=[col(0), col(1), col(2), col(3), whole(lbl), whole(wn), tile,
                          pl.BlockSpec((HG_T // CHUNK, HGRN_W, HGRN_HD), lambda i: (rev(i), 0, 0)),
                          pl.BlockSpec((HG_T, HGRN_W), lambda i: (rev(i), 1))],
                out_specs=[tile, tile, tile, tile, vec, vec],
                out_shape=[jax.ShapeDtypeStruct((S, HGRN_W), bf16)] * 4 + [jax.ShapeDtypeStruct((1, HGRN_W), f32)] * 2,
                scratch_shapes=[pltpu.VMEM((N_HH, HGRN_HD, HGRN_HD), f32), tb(), tb(), tf(), tb(), tf(), tf(), tf(), tf(), tf(),
                                tf(), tf()])
    call, body, more = _ride(call, rider, body, lambda: pl.program_id(0), nT, 9, 6, 12)
    return pl.pallas_call(body, name="hgrn_bwd", grid=(nT,), compiler_params=_cp("arbitrary"), **call)(
        hp, hp, hp, hp, lbl, wn, o_sav, states, dmix, *more)


def _out_proj(x, ya, yb, wout, w2):
    S = x.shape[0]
    tm = 512

    def body(x_ref, ya_ref, yb_ref, w_ref, w2_ref, h1_ref, u2_ref, mix_ref):
        mixed = jnp.concatenate([ya_ref[...], yb_ref[...]], axis=1).astype(bf16)
        mix_ref[...] = mixed
        h1 = x_ref[...] + _dot(mixed, w_ref[...])
        h1_ref[...] = h1
        r = lax.rsqrt(jnp.mean(h1 * h1, axis=-1, keepdims=True) + EPS)
        u2_ref[...] = (h1 * r * w2_ref[...]).astype(bf16)

    row = lambda w: pl.BlockSpec((tm, w), lambda i: (i, 0))
    return pl.pallas_call(
        body, name="out_proj", grid=(S // tm,),
        in_specs=[row(D_MODEL), row(ATTN_W), row(HGRN_W), pl.BlockSpec((D_MODEL, D_MODEL), lambda i: (0, 0)),
                  pl.BlockSpec((1, D_MODEL), lambda i: (0, 0))],
        out_specs=[row(D_MODEL), row(D_MODEL), row(D_MODEL)],
        out_shape=[jax.ShapeDtypeStruct((S, D_MODEL), f32), jax.ShapeDtypeStruct((S, D_MODEL), bf16),
                   jax.ShapeDtypeStruct((S, D_MODEL), bf16)],
        compiler_params=_cp("arbitrary"),
    )(x, ya, yb, wout, w2)


def _gate_up(u2, wgu_g):
    S = u2.shape[0]
    w = 2 * FFN // N_DEV
    tm, tn = 512, 2 * w
    nj = FFN // tn

    def body(u_ref, wgg_ref, wug_ref, g_ref, up_ref, a_ref, wg_ref, wu_ref):
        @pl.when(pl.program_id(1) == 0)
        def _():
            for k in range(2):
                wg_ref[:, w * k:w * (k + 1)] = wgg_ref[k]
                wu_ref[:, w * k:w * (k + 1)] = wug_ref[k]

        u = u_ref[...]
        g = _dot(u, wg_ref[...])
        up = _dot(u, wu_ref[...])
        sg = _sigmoid(g)
        silu = g * sg
        g_ref[...] = silu.astype(bf16)
        up_ref[...] = (up * (sg + silu * (1.0 - sg))).astype(bf16)
        a_ref[...] = (silu * up).astype(bf16)

    out = pl.BlockSpec((tm, tn), lambda j, i: (i, j))
    wout = pl.BlockSpec((D_MODEL, tn), lambda j, i: (0, j))
    return pl.pallas_call(
        body, name="gate_up", grid=(nj, S // tm),
        in_specs=[pl.BlockSpec((tm, D_MODEL), lambda j, i: (i, 0)), pl.BlockSpec((2, D_MODEL, w), lambda j, i: (j, 0, 0)),
                  pl.BlockSpec((2, D_MODEL, w), lambda j, i: (j + nj, 0, 0))],
        out_specs=[out, out, out, wout, wout],
        out_shape=[jax.ShapeDtypeStruct((S, FFN), bf16)] * 3 + [jax.ShapeDtypeStruct((D_MODEL, FFN), bf16)] * 2,
        compiler_params=_cp("arbitrary", "arbitrary"),
    )(u2, wgu_g, wgu_g)


def _rms_bwd(dyw, hn, r):
    return r * (dyw - hn * jnp.mean(dyw * hn, axis=-1, keepdims=True))


def _down_loss(act, wdown, h1, tgt, w3):
    S = act.shape[0]
    tm = 512

    def body(a_ref, w_ref, h1_ref, t_ref, w3_ref, dh2_ref, loss_ref, gw3_ref):
        @pl.when(pl.program_id(0) == 0)
        def _():
            loss_ref[...] = jnp.zeros_like(loss_ref)
            gw3_ref[...] = jnp.zeros_like(gw3_ref)

        h2 = h1_ref[...] + _dot(a_ref[...], w_ref[...])
        r = lax.rsqrt(jnp.mean(h2 * h2, axis=-1, keepdims=True) + EPS)
        hn = h2 * r
        w3 = w3_ref[...]
        err = hn * w3 - t_ref[...]
        loss_ref[...] += (0.5 / D_MODEL) * jnp.sum(err * err)
        dy = err * (1.0 / D_MODEL)
        gw3_ref[...] += jnp.sum(dy * hn, axis=0, keepdims=True)
        dh2_ref[...] = _rms_bwd(dy * w3, hn, r)

    row = lambda w: pl.BlockSpec((tm, w), lambda i: (i, 0))
    return pl.pallas_call(
        body, name="down_loss", grid=(S // tm,),
        in_specs=[row(FFN), pl.BlockSpec((FFN, D_MODEL), lambda i: (0, 0)), row(D_MODEL), row(D_MODEL),
                  pl.BlockSpec((1, D_MODEL), lambda i: (0, 0))],
        out_specs=[row(D_MODEL), pl.BlockSpec((1, 128), lambda i: (0, 0)), pl.BlockSpec((1, D_MODEL), lambda i: (0, 0))],
        out_shape=[jax.ShapeDtypeStruct((S, D_MODEL), f32), jax.ShapeDtypeStruct((1, 128), f32),
                   jax.ShapeDtypeStruct((1, D_MODEL), f32)],
        compiler_params=_cp("arbitrary"),
    )(act, wdown, h1, tgt, w3)


def _dact(dh2, wdown, silu, up_dsilu):
    S = dh2.shape[0]
    tm = 256

    def body(d_ref, w_ref, s_ref, u_ref, o_ref):
        da = _dot_nt(d_ref[...].astype(bf16), w_ref[...])
        o_ref[1] = (da * s_ref[...].astype(f32)).astype(bf16)
        o_ref[0] = (da * u_ref[...].astype(f32)).astype(bf16)

    row = lambda w: pl.BlockSpec((tm, w), lambda i: (i, 0))
    return pl.pallas_call(
        body, name="dact", grid=(S // tm,),
        in_specs=[row(D_MODEL), pl.BlockSpec((FFN, D_MODEL), lambda i: (0, 0)), row(FFN), row(FFN)],
        out_specs=pl.BlockSpec((2, tm, FFN), lambda i: (0, i, 0)),
        out_shape=jax.ShapeDtypeStruct((2, S, FFN), bf16),
        compiler_params=_cp("arbitrary"),
    )(dh2, wdown, silu, up_dsilu)


def _dgu(dgu2, wgate, wup, h1, w2, dh2, wout, rider=None):
    S = dgu2.shape[1]
    tm = 256

    def body(d_ref, wg_ref, wu_ref, h1_ref, w2_ref, dh2_ref, wo_ref, dh1_ref, gw2_ref, dmix_ref):
        @pl.when(pl.program_id(0) == 0)
        def _():
            gw2_ref[...] = jnp.zeros_like(gw2_ref)

        du2 = _dot_nt(d_ref[0], wg_ref[...]) + _dot_nt(d_ref[1], wu_ref[...])
        h1 = h1_ref[...]
        r = lax.rsqrt(jnp.mean(h1 * h1, axis=-1, keepdims=True) + EPS)
        hn = h1 * r
        gw2_ref[...] += jnp.sum(du2 * hn, axis=0, keepdims=True)
        dh1 = dh2_ref[...] + _rms_bwd(du2 * w2_ref[...], hn, r)
        dh1_ref[...] = dh1
        dmix_ref[...] = _dot_nt(dh1.astype(bf16), wo_ref[...])

    row = lambda w: pl.BlockSpec((tm, w), lambda i: (i, 0))
    call = dict(in_specs=[pl.BlockSpec((2, tm, FFN), lambda i: (0, i, 0)), pl.BlockSpec((D_MODEL, FFN), lambda i: (0, 0)),
                          pl.BlockSpec((D_MODEL, FFN), lambda i: (0, 0)), row(D_MODEL),
                          pl.BlockSpec((1, D_MODEL), lambda i: (0, 0)), row(D_MODEL),
                          pl.BlockSpec((D_MODEL, D_MODEL), lambda i: (0, 0))],
                out_specs=[row(D_MODEL), pl.BlockSpec((1, D_MODEL), lambda i: (0, 0)), row(D_MODEL)],
                out_shape=[jax.ShapeDtypeStruct((S, D_MODEL), f32), jax.ShapeDtypeStruct((1, D_MODEL), f32),
                           jax.ShapeDtypeStruct((S, D_MODEL), f32)], scratch_shapes=[])
    call, body, more = _ride(call, rider, body, lambda: pl.program_id(0), S // tm, 7, 3, 0)
    return pl.pallas_call(body, name="dgu", grid=(S // tm,), compiler_params=_cp("arbitrary"), **call)(
        dgu2, wgate, wup, h1, w2, dh2, wout, *more)


def _din(dq, dk, dv, dhq, dhf, dhi, dhg, cos_t, sg_t, win, x, w1, dh1):
    S = x.shape[0]
    tm = 256

    def body(dq_ref, dk_ref, dv_ref, dhq_ref, dhf_ref, dhi_ref, dhg_ref, cos_ref, sg_ref, w_ref, x_ref, w1_ref, dh1_ref,
             dp_ref, gx_ref, gw1_ref):
        @pl.when(pl.program_id(0) == 0)
        def _():
            gw1_ref[...] = jnp.zeros_like(gw1_ref)

        cosv, sgv = jnp.tile(cos_ref[...], (1, ATTN_W // 128)), jnp.tile(sg_ref[...], (1, ATTN_W // 128))
        unrope = lambda d: d * cosv - sgv * _swap_halves(d)
        parts = [(unrope(dq_ref[...]) * (HEAD_DIM ** -0.5)).astype(bf16), unrope(dk_ref[...]).astype(bf16),
                 dv_ref[...].astype(bf16), dhq_ref[...], dhf_ref[...], dhi_ref[...], dhg_ref[...]]
        du = jnp.zeros((tm, D_MODEL), f32)
        for j, pj in enumerate(parts):
            dp_ref[:, j * 512:(j + 1) * 512] = pj
            du = du + _dot_nt(pj, w_ref[:, j * 512:(j + 1) * 512])
        xv = x_ref[...]
        r = lax.rsqrt(jnp.mean(xv * xv, axis=-1, keepdims=True) + EPS)
        xn = xv * r
        gw1_ref[...] += jnp.sum(du * xn, axis=0, keepdims=True)
        gx_ref[...] = dh1_ref[...] + _rms_bwd(du * w1_ref[...], xn, r)

    row = lambda w: pl.BlockSpec((tm, w), lambda i: (i, 0))
    vec = pl.BlockSpec((1, D_MODEL), lambda i: (0, 0))
    return pl.pallas_call(
        body, name="din", grid=(S // tm,),
        in_specs=[row(512)] * 7 + [row(128), row(128), pl.BlockSpec((D_MODEL, IN_W), lambda i: (0, 0)), row(D_MODEL), vec,
                                   row(D_MODEL)],
        out_specs=[row(IN_W), row(D_MODEL), vec],
        out_shape=[jax.ShapeDtypeStruct((S, IN_W), bf16), jax.ShapeDtypeStruct((S, D_MODEL), f32),
                   jax.ShapeDtypeStruct((1, D_MODEL), f32)],
        compiler_params=_cp("arbitrary"),
    )(dq, dk, dv, dhq, dhf, dhi, dhg, cos_t, sg_t, win, x, w1, dh1)


def _gw(a, bs, tn, name, ts=2048):
    S, M = a.shape
    N = bs[0].shape[1]
    k = len(bs)

    def body(a_ref, *refs):
        @pl.when(pl.program_id(1) == 0)
        def _():
            for o_ref in refs[k:]:
                o_ref[...] = jnp.zeros_like(o_ref)

        at = a_ref[...].astype(bf16)
        for b_ref, o_ref in zip(refs[:k], refs[k:]):
            o_ref[...] += _dot_tn(at, b_ref[...].astype(bf16))

    return pl.pallas_call(
        body, name=name, grid=(N // tn, S // ts),
        in_specs=[pl.BlockSpec((ts, M), lambda j, s: (s, 0))] + [pl.BlockSpec((ts, tn), lambda j, s: (s, j))] * k,
        out_specs=[pl.BlockSpec((M, tn), lambda j, s: (0, j))] * k, out_shape=[jax.ShapeDtypeStruct((M, N), f32)] * k,
        compiler_params=_cp("arbitrary", "arbitrary"),
    )(a, *bs)


def _gw_by_owner(a, b3, w, name, ts):
    S, M = a.shape
    G, _, Ng = b3.shape
    tn = 2 * w
    per_group = Ng // tn
    n_s = S // ts

    def body(a_ref, b_ref, o_ref, acc):
        s = pl.program_id(1)

        @pl.when(s == 0)
        def _():
            acc[...] = jnp.zeros_like(acc)

        acc[...] += _dot_tn(a_ref[...].astype(bf16), b_ref[0].astype(bf16))

        @pl.when(s == n_s - 1)
        def _():
            o_ref[0] = acc[:, 0:w]
            o_ref[1] = acc[:, w:tn]

    return pl.pallas_call(
        body, name=name, grid=(G * per_group, n_s),
        in_specs=[pl.BlockSpec((ts, M), lambda j, s: (s, 0)),
                  pl.BlockSpec((1, ts, tn), lambda j, s: (j // per_group, s, j % per_group))],
        out_specs=pl.BlockSpec((2, M, w), lambda j, s: (j, 0, 0)), out_shape=jax.ShapeDtypeStruct((G * Ng // w, M, w), f32),
        scratch_shapes=[pltpu.VMEM((M, tn), f32)], compiler_params=_cp("arbitrary", "arbitrary"),
    )(a, b3)


MESH = pl.DeviceIdType.MESH
ANY = pl.BlockSpec(memory_space=pl.ANY)
VMEM_SPEC = pl.BlockSpec(memory_space=pltpu.VMEM)


def _pos():
    return lax.axis_index("x"), lax.axis_index("y"), lax.axis_index("c")


def _flip(v, bit):
    return 1 - v if bit else v


def _gather_rider(shards):
    n = len(shards)

    def parts(outs, scratch):
        send_sems, recv_sems, local_sems = scratch[n:]
        x, y, c = _pos()
        chips = [(1 - x, y), (x, 1 - y), (1 - x, 1 - y)]

        def copy(a, k, block, to, src=None):
            dst = outs[a].at[4 * block[0] + 2 * block[1] + block[2]]
            return pltpu.make_async_remote_copy(src_ref=dst if src is None else src, dst_ref=dst, send_sem=send_sems.at[a, k],
                                                recv_sem=recv_sems.at[a, k], device_id=to, device_id_type=MESH)

        bufs = scratch[:n]
        me, sibling = (x, y, c), (x, y, 1 - c)
        own = lambda a: pltpu.make_async_copy(bufs[a], outs[a].at[4 * x + 2 * y + c], local_sems.at[a])
        sent = lambda a: [copy(a, 0, me, sibling, src=bufs[a])] + [copy(a, 1 + j, me, (*chip, c), src=bufs[a])
                                                                   for j, chip in enumerate(chips)]
        passed = lambda a: [copy(a, 4 + j, (*chip, c), sibling) for j, chip in enumerate(chips)]
        landed = lambda a: [copy(a, 1 + j, (*chip, c), me) for j, chip in enumerate(chips)]
        from_sibling = lambda a: [copy(a, 0, sibling, me)] + [copy(a, 4 + j, (*chip, 1 - c), me) for j, chip in enumerate(chips)]
        return bufs, local_sems, own, sent, passed, landed, from_sibling

    def first(ins, outs, scratch):
        bufs, local_sems, own, sent, _, _, _ = parts(outs, scratch)
        loads = [pltpu.make_async_copy(ins[a], bufs[a], local_sems.at[a]) for a in range(n)]
        for ld in loads:
            ld.start()
        for a in range(n):
            loads[a].wait()
            own(a).start()
            for cp in sent(a):
                cp.start()

    def middle(ins, outs, scratch):
        _, _, _, _, passed, landed, _ = parts(outs, scratch)
        for a in range(n):
            for got, on in zip(landed(a), passed(a)):
                got.wait_recv()
                on.start()

    def last(ins, outs, scratch):
        _, _, own, sent, passed, _, from_sibling = parts(outs, scratch)
        for a in range(n):
            for cp in from_sibling(a):
                cp.wait_recv()
        for a in range(n):
            for cp in sent(a) + passed(a):
                cp.wait_send()
            own(a).wait()

    return _Rider(shards, [jax.ShapeDtypeStruct((N_DEV,) + s.shape, s.dtype) for s in shards],
                  [pltpu.VMEM(s.shape, s.dtype) for s in shards]
                  + [pltpu.SemaphoreType.DMA((n, 7)), pltpu.SemaphoreType.DMA((n, 7)), pltpu.SemaphoreType.DMA((n,))],
                  first, last, middle)


def _sibling_rider(grads):
    n = len(grads)

    def copies(g, got, scratch):
        send_sems, recv_sems = scratch
        x, y, c = _pos()
        return [pltpu.make_async_remote_copy(src_ref=g[a].at[2 * q + (1 - c)], dst_ref=got[a].at[q], send_sem=send_sems.at[a, q],
                                             recv_sem=recv_sems.at[a, q], device_id=(x, y, 1 - c), device_id_type=MESH)
                for a in range(n) for q in range(4)]

    def first(g, got, scratch):
        for cp in copies(g, got, scratch):
            cp.start()

    def last(g, got, scratch):
        for cp in copies(g, got, scratch):
            cp.wait()

    return _Rider(grads, [jax.ShapeDtypeStruct((4,) + g.shape[1:], g.dtype) for g in grads],
                  [pltpu.SemaphoreType.DMA((n, 4))] * 2, first, last)


def _chips_rider(sums):
    n = len(sums)

    def copies(s, out, scratch):
        send_sems, recv_sems = scratch
        x, y, c = _pos()
        cps = []
        for a in range(n):
            for f in (1, 2, 3):
                peer = (_flip(x, f >> 1), _flip(y, f & 1), c)
                cps.append(pltpu.make_async_remote_copy(
                    src_ref=s[a].at[2 * peer[0] + peer[1]], dst_ref=out[a].at[f - 1], send_sem=send_sems.at[a, f - 1],
                    recv_sem=recv_sems.at[a, f - 1], device_id=peer, device_id_type=MESH))
        return cps

    def first(s, out, scratch):
        for cp in copies(s, out, scratch):
            cp.start()

    def last(s, out, scratch):
        for cp in copies(s, out, scratch):
            cp.wait()

    return _Rider(sums, [jax.ShapeDtypeStruct((3,) + s.shape[1:], s.dtype) for s in sums],
                  [pltpu.SemaphoreType.DMA((n, 3))] * 2, first, last)


def _add_and_send(g, got, name):
    _, r, c = got.shape

    def body(g_ref, got_ref, out_ref, a_buf, b_buf, s_buf, load_sems, send_sems, recv_sems):
        x, y, cc = _pos()
        copies = []
        for f in (1, 2, 3):
            peer = (_flip(x, f >> 1), _flip(y, f & 1), cc)
            qd = 2 * peer[0] + peer[1]
            mine = pltpu.make_async_copy(g_ref.at[2 * qd + cc], a_buf, load_sems.at[0])
            theirs = pltpu.make_async_copy(got_ref.at[qd], b_buf, load_sems.at[1])
            mine.start()
            theirs.start()
            mine.wait()
            theirs.wait()
            s_buf[f - 1] = (a_buf[...] + b_buf[...]).astype(bf16)
            cp = pltpu.make_async_remote_copy(src_ref=s_buf.at[f - 1], dst_ref=out_ref.at[f - 1], send_sem=send_sems.at[f - 1],
                                              recv_sem=recv_sems.at[f - 1], device_id=peer, device_id_type=MESH)
            cp.start()
            copies.append(cp)
        for cp in copies:
            cp.wait()

    return pl.pallas_call(
        body, name=name, in_specs=[ANY, ANY], out_specs=ANY, out_shape=jax.ShapeDtypeStruct((3, r, c), bf16),
        scratch_shapes=[pltpu.VMEM((r, c), f32), pltpu.VMEM((r, c), f32), pltpu.VMEM((3, r, c), bf16),
                        pltpu.SemaphoreType.DMA((2,)), pltpu.SemaphoreType.DMA((3,)), pltpu.SemaphoreType.DMA((3,))],
    )(g, got)


def _both(a, b):
    na = (len(a.ins), len(a.out_shapes), len(a.scratch))

    def split(fa, fb):
        def f(ins, outs, scratch):
            fa(ins[:na[0]], outs[:na[1]], scratch[:na[2]])
            fb(ins[na[0]:], outs[na[1]:], scratch[na[2]:])
        return f

    return _Rider(a.ins + b.ins, a.out_shapes + b.out_shapes, a.scratch + b.scratch, split(a.first, b.first), split(a.last, b.last))


def _alone(rider, name):
    ri, ro = len(rider.ins), len(rider.out_shapes)

    def body(*refs):
        theirs = (refs[:ri], refs[ri:ri + ro], refs[ri + ro:])
        rider.first(*theirs)
        if rider.middle is not None:
            rider.middle(*theirs)
        rider.last(*theirs)

    return pl.pallas_call(body, name=name, in_specs=[ANY] * ri, out_specs=[ANY] * ro, out_shape=rider.out_shapes,
                          scratch_shapes=rider.scratch)(*rider.ins)


def _gather_small(g_w1, g_w2, g_w3, g_lb, g_wn, loss):
    def body(w1_ref, w2_ref, w3_ref, lb_ref, wn_ref, loss_ref, out_ref, pk, send_sems, recv_sems):
        x, y, c = _pos()
        me = 4 * x + 2 * y + c
        pk[...] = jnp.zeros_like(pk)
        pk[0:1, :] = w1_ref[...]
        pk[1:2, :] = w2_ref[...]
        pk[2:3, :] = w3_ref[...]
        pk[3:4, 0:HGRN_W] = lb_ref[...]
        pk[3:4, HGRN_W:2 * HGRN_W] = wn_ref[...]
        pk[4:5, 0:128] = loss_ref[...]
        out_ref[me] = pk[...]
        sends, recvs = [], []
        for k in range(1, N_DEV):
            peer = (_flip(x, k >> 2), _flip(y, (k >> 1) & 1), _flip(c, k & 1))
            cp = pltpu.make_async_remote_copy(src_ref=pk, dst_ref=out_ref.at[me], send_sem=send_sems.at[k - 1],
                                              recv_sem=recv_sems.at[k - 1], device_id=peer, device_id_type=MESH)
            cp.start()
            sends.append(cp)
            recvs.append(pltpu.make_async_remote_copy(src_ref=pk, dst_ref=out_ref.at[4 * peer[0] + 2 * peer[1] + peer[2]],
                                                      send_sem=send_sems.at[k - 1], recv_sem=recv_sems.at[k - 1], device_id=peer,
                                                      device_id_type=MESH))
        for cp in recvs:
            cp.wait_recv()
        for cp in sends:
            cp.wait_send()

    return pl.pallas_call(
        body, name="gather_small", in_specs=[VMEM_SPEC] * 6, out_specs=VMEM_SPEC,
        out_shape=jax.ShapeDtypeStruct((N_DEV, 8, D_MODEL), f32),
        scratch_shapes=[pltpu.VMEM((8, D_MODEL), f32), pltpu.SemaphoreType.DMA((N_DEV - 1,)), pltpu.SemaphoreType.DMA((N_DEV - 1,))],
    )(g_w1, g_w2, g_w3, g_lb, g_wn, loss)


def _row_tile(r):
    return max(t for t in range(8, 257, 8) if r % t == 0)


def _add_sibling(core, g, got, name):
    _, r, c = got.shape
    tr = _row_tile(r)

    def body(core_ref, a_ref, b_ref, o_ref):
        o_ref[...] = (a_ref[...] + b_ref[...]).astype(bf16)

    blk = pl.BlockSpec((1, tr, c), lambda q, i, core_ref: (q, i, 0))
    return pl.pallas_call(
        body, name=name, out_shape=jax.ShapeDtypeStruct(got.shape, bf16),
        grid_spec=pltpu.PrefetchScalarGridSpec(
            num_scalar_prefetch=1, grid=(4, r // tr),
            in_specs=[pl.BlockSpec((1, tr, c), lambda q, i, core_ref: (2 * q + core_ref[0], i, 0)), blk], out_specs=blk),
        compiler_params=_cp("arbitrary", "arbitrary"))(core, g, got)


def _adamw(w, g, m, v):
    m = ADAM_B1 * m + (1.0 - ADAM_B1) * g
    v = ADAM_B2 * v + (1.0 - ADAM_B2) * (g * g)
    m_hat = m / (1.0 - ADAM_B1 ** ADAM_STEP)
    v_hat = v / (1.0 - ADAM_B2 ** ADAM_STEP)
    return -ADAM_LR * (m_hat / (jnp.sqrt(v_hat) + ADAM_EPS) + ADAM_WD * w), m, v


def _adam_shard(where, g, got, pieces, w, m, v, name):
    r, c = w.shape
    tr = _row_tile(r)

    def body(where_ref, g_ref, got_ref, p_ref, w_ref, m_ref, v_ref, g_out, d_out, m_out, v_out):
        gsum = g_ref[0] + got_ref[0]
        for f in range(3):
            gsum = gsum + p_ref[f].astype(f32)
        g_out[...] = gsum
        d_out[...], m_out[...], v_out[...] = _adamw(w_ref[...], gsum, m_ref[...], v_ref[...])

    blk = pl.BlockSpec((tr, c), lambda i, where_ref: (i, 0))
    return pl.pallas_call(
        body, name=name, out_shape=[jax.ShapeDtypeStruct((r, c), f32)] * 4,
        grid_spec=pltpu.PrefetchScalarGridSpec(
            num_scalar_prefetch=1, grid=(r // tr,),
            in_specs=[pl.BlockSpec((1, tr, c), lambda i, where_ref: (where_ref[0], i, 0)),
                      pl.BlockSpec((1, tr, c), lambda i, where_ref: (where_ref[1], i, 0)),
                      pl.BlockSpec((3, tr, c), lambda i, where_ref: (0, i, 0)), blk, blk, blk],
            out_specs=[blk] * 4),
        compiler_params=_cp("arbitrary"),
    )(where, g, got, pieces, w, m, v)


def _small_update(gath, params):
    def body(gath_ref, *refs):
        ins, outs = refs[:15], refs[15:]
        gs = gath_ref[0]
        for k in range(1, N_DEV):
            gs = gs + gath_ref[k]
        outs[0][...] = gs[4:5, 0:128]
        l0, l1 = ins[9][0:1, :], ins[9][1:2, :]
        lb = _sigmoid(l0 - l1)
        d0 = gs[3:4, 0:HGRN_W] * lb * (1.0 - lb)
        first_row = lax.broadcasted_iota(jnp.int32, (2, HGRN_W), 0) == 0
        grads = [gs[0:1, :], gs[1:2, :], gs[2:3, :], jnp.where(first_row, d0, -d0), gs[3:4, HGRN_W:2 * HGRN_W]]
        for i, g in enumerate(grads):
            w_ref, m_ref, v_ref = ins[3 * i:3 * i + 3]
            o = outs[1 + 4 * i:5 + 4 * i]
            o[0][...] = g
            o[1][...], o[2][...], o[3][...] = _adamw(w_ref[...], g, m_ref[...], v_ref[...])

    flat = [a for p in params for a in p]
    out_shape = [jax.ShapeDtypeStruct((1, 128), f32)] + [jax.ShapeDtypeStruct(p[0].shape, f32) for p in params for _ in range(4)]
    outs = pl.pallas_call(body, name="small_update", in_specs=[VMEM_SPEC] * 16, out_specs=[VMEM_SPEC] * 21, out_shape=out_shape)(gath, *flat)
    return outs[0], [outs[1 + 4 * i:5 + 4 * i] for i in range(5)]


def kernel(x, norm1_w, w_in, lb_logits, hgrn_norm_w, w_out, norm2_w, w_gate_up, w_down, final_norm_w, loss_target, m_norm1_w, m_w_in, m_lb_logits, m_hgrn_norm_w, m_w_out, m_norm2_w, m_w_gate_up, m_w_down, m_final_norm_w, v_norm1_w, v_w_in, v_lb_logits, v_hgrn_norm_w, v_w_out, v_norm2_w, v_w_gate_up, v_w_down, v_final_norm_w):
    row = lambda a: a.reshape(1, D_MODEL)
    ix, iy, ic = lax.axis_index("x"), lax.axis_index("y"), lax.axis_index("c")
    core = jnp.stack([ic]).astype(jnp.int32)
    where = jnp.stack([4 * ix + 2 * iy + ic, 2 * ix + iy]).astype(jnp.int32)
    xs, tgt, w3 = x[0], loss_target[0], row(final_norm_w)
    S = xs.shape[0]

    cos_t, sg_t, win_g = _rope_tables(S, _gather_rider([w_in[0].astype(bf16)]))
    u, qkv, hp, win = _in_proj(xs, norm1_w, win_g, cos_t, sg_t)
    ya, lse, wout_g, wgu_g, wdown_g = _attn_fwd(qkv, _gather_rider([w_out[0].astype(bf16), w_gate_up[0].astype(bf16),
                                                                     w_down[0].astype(bf16)]))
    wout = wout_g.reshape(D_MODEL, D_MODEL)
    wdown = wdown_g.reshape(FFN, D_MODEL)
    yb, o_sav, states = _hgrn_fwd(hp, lb_logits, hgrn_norm_w)
    h1, u2, mixed = _out_proj(xs, ya, yb, wout, norm2_w)
    silu, up_dsilu, act, wgate, wup = _gate_up(u2, wgu_g)
    dh2, loss_p, g_w3 = _down_loss(act, wdown, h1, tgt, w3)

    (g_wdown,) = _gw(act, [dh2], 512, "gw_down")
    dgu2 = _dact(dh2, wdown, silu, up_dsilu)
    early = [_gw_by_owner(u2, dgu2, 2 * FFN // N_DEV, "gw_gate_up", 2048), g_wdown.reshape(N_DEV, FFN // N_DEV, D_MODEL)]
    dh1, g_w2, dmix, *got_early = _dgu(dgu2, wgate, wup, h1, norm2_w, dh2, wout, _sibling_rider(early))
    sums_early = [_add_sibling(core, g, o, f"add_sibling_{i}") for i, (g, o) in enumerate(zip(early, got_early))]
    (g_wout,) = _gw(mixed, [dh1], 1024, "gw_out")
    mid = [g_wout.reshape(N_DEV, D_MODEL // N_DEV, D_MODEL)]
    dhq, dhf, dhi, dhg, g_wn, g_lb, *rode = _hgrn_bwd(hp, lb_logits, hgrn_norm_w, o_sav, states, dmix,
                                                      _both(_chips_rider(sums_early), _sibling_rider(mid)))
    pieces_early, got_mid = rode[:2], rode[2:]
    sums_mid = [_add_sibling(core, mid[0], got_mid[0], "add_sibling_2")]
    dq, dk, dv, *pieces_mid = _attn_bwd(qkv, ya, lse, dmix, _chips_rider(sums_mid))
    dproj, gx, g_w1 = _din(dq, dk, dv, dhq, dhf, dhi, dhg, cos_t, sg_t, win, xs, norm1_w, dh1)
    late = [_gw_by_owner(u, dproj[None], IN_W // N_DEV, "gw_in", 2048)]
    got_late = _alone(_sibling_rider(late), "reduce_sibling")
    pieces_late = [_add_and_send(late[0], got_late[0], "reduce_chips")]

    grads = [late[0], mid[0], early[0], early[1]]
    got = [got_late[0], got_mid[0], got_early[0], got_early[1]]
    pieces = [pieces_late[0], pieces_mid[0], pieces_early[0], pieces_early[1]]
    shards = [w_in[0], w_out[0], w_gate_up[0], w_down[0]]
    moms = [(m_w_in[0], v_w_in[0]), (m_w_out[0], v_w_out[0]), (m_w_gate_up[0], v_w_gate_up[0]), (m_w_down[0], v_w_down[0])]
    big = [_adam_shard(where, g, o, p, w, m, v, f"adam_{i}")
           for i, (g, o, p, w, (m, v)) in enumerate(zip(grads, got, pieces, shards, moms))]
    big = [[a[None] for a in four] for four in big]

    gath = _gather_small(g_w1, g_w2, g_w3, g_lb, g_wn, loss_p)
    params = [(norm1_w, m_norm1_w, v_norm1_w), (norm2_w, m_norm2_w, v_norm2_w),
              (row(final_norm_w), row(m_final_norm_w), row(v_final_norm_w)),
              (lb_logits, m_lb_logits, v_lb_logits), (hgrn_norm_w, m_hgrn_norm_w, v_hgrn_norm_w)]
    loss, (s_w1, s_w2, s_w3, s_lb, s_wn) = _small_update(gath, params)
    s_w3 = [a.reshape(D_MODEL) for a in s_w3]
    per_w = [s_w1, big[0], s_lb, s_wn, big[1], s_w2, big[2], big[3], s_w3]
    return (loss[0, 0], gx[None], *[p[0] for p in per_w], *[p[1] for p in per_w], *[p[2] for p in per_w], *[p[3] for p in per_w])
```

```python
import jax
import jax.numpy as jnp
from jax import lax
from jax.experimental import pallas as pl
from jax.experimental.pallas import tpu as pltpu

f32, bf16 = jnp.float32, jnp.bfloat16

D_MODEL = 1024
ATTN_W = 512
HEAD_DIM = 64
ATTN_BLK = 128
DILATIONS = (1, 4, 16)
HGRN_W = 512
HGRN_HD = 128
CHUNK = 64
IN_W = 3 * ATTN_W + 4 * HGRN_W
FFN = 2816
EPS = 1e-6
ROPE_THETA = 10000.0
NEG = -1e30
N_DEV = 8
ADAM_LR, ADAM_B1, ADAM_B2, ADAM_EPS, ADAM_WD, ADAM_STEP = 0.001, 0.9, 0.999, 1e-08, 0.01, 10
VMEM_LIMIT = 56 * 1024 * 1024


def _cp(*sem):
    return pltpu.CompilerParams(dimension_semantics=sem, vmem_limit_bytes=VMEM_LIMIT)


def _dot(a, b):
    return jnp.dot(a, b, preferred_element_type=f32)


def _dot_nt(a, b):
    return lax.dot_general(a, b, (((1,), (1,)), ((), ())), preferred_element_type=f32)


def _dot_tn(a, b):
    return lax.dot_general(a, b, (((0,), (0,)), ((), ())), preferred_element_type=f32)


def _sigmoid(x):
    return 0.5 * jnp.tanh(0.5 * x) + 0.5


class _Rider:
    def __init__(self, ins, out_shapes, scratch, first, last, middle=None):
        self.ins, self.out_shapes, self.scratch = list(ins), list(out_shapes), list(scratch)
        self.first, self.middle, self.last = first, middle, last


def _ride(call, rider, body, step, n_steps, n_in, n_out, n_scratch):
    if rider is None:
        return call, body, []
    ri, ro = len(rider.ins), len(rider.out_shapes)
    any_spec = pl.BlockSpec(memory_space=pl.ANY)
    call = dict(call, in_specs=call["in_specs"] + [any_spec] * ri, out_specs=call["out_specs"] + [any_spec] * ro,
                out_shape=call["out_shape"] + rider.out_shapes, scratch_shapes=call["scratch_shapes"] + rider.scratch)

    def riding(*refs):
        a = n_in + ri
        b = a + n_out + ro
        mine = refs[:n_in] + refs[a:a + n_out] + refs[b:b + n_scratch]
        theirs = (refs[n_in:a], refs[a + n_out:b], refs[b + n_scratch:])
        t = step()

        @pl.when(t == 0)
        def _():
            rider.first(*theirs)

        body(*mine)
        if rider.middle is not None:
            @pl.when(t == n_steps // 2)
            def _():
                rider.middle(*theirs)

        @pl.when(t == n_steps - 1)
        def _():
            rider.last(*theirs)

    return call, riding, rider.ins


def _rope_tables(S, rider=None):
    half = HEAD_DIM // 2
    tm = 256
    inv_freq = jnp.tile(ROPE_THETA ** (-jnp.arange(half, dtype=f32) / half), 128 // half).reshape(1, 128)
    sign = jnp.tile(jnp.concatenate([-jnp.ones((half,), f32), jnp.ones((half,), f32)]), 128 // HEAD_DIM).reshape(1, 128)

    def body(inv_ref, sign_ref, cos_ref, sg_ref):
        pos = (lax.broadcasted_iota(jnp.int32, (tm, 128), 0) + pl.program_id(0) * tm).astype(f32)
        ang = pos * inv_ref[...]
        cos_ref[...] = jnp.cos(ang)
        sg_ref[...] = jnp.sin(ang) * sign_ref[...]

    vec = pl.BlockSpec((1, 128), lambda i: (0, 0))
    out = pl.BlockSpec((tm, 128), lambda i: (i, 0))
    call = dict(in_specs=[vec, vec], out_specs=[out, out], out_shape=[jax.ShapeDtypeStruct((S, 128), f32)] * 2, scratch_shapes=[])
    call, body, more = _ride(call, rider, body, lambda: pl.program_id(0), S // tm, 2, 2, 0)
    return pl.pallas_call(body, name="rope_tables", grid=(S // tm,), compiler_params=_cp("arbitrary"), **call)(inv_freq, sign, *more)


def _swap_halves(v):
    n = v.shape[1]
    lane = lax.broadcasted_iota(jnp.int32, v.shape, 1)
    return jnp.where((lane % HEAD_DIM) < HEAD_DIM // 2, pltpu.roll(v, n - HEAD_DIM // 2, 1), pltpu.roll(v, HEAD_DIM // 2, 1))


def _in_proj(x, w1, win_g, cos_t, sg_t):
    S = x.shape[0]
    tm = 512
    w = IN_W // N_DEV

    def body(x_ref, w1_ref, wg_ref, cos_ref, sg_ref, u_ref, qkv_ref, hp_ref, w_ref):
        @pl.when(pl.program_id(0) == 0)
        def _():
            for d in range(N_DEV):
                w_ref[:, w * d:w * (d + 1)] = wg_ref[d]

        xv = x_ref[...]
        r = lax.rsqrt(jnp.mean(xv * xv, axis=-1, keepdims=True) + EPS)
        u = (xv * r * w1_ref[...]).astype(bf16)
        u_ref[...] = u
        cosv, sgv = jnp.tile(cos_ref[...], (1, ATTN_W // 128)), jnp.tile(sg_ref[...], (1, ATTN_W // 128))
        for j in range(3):
            pj = _dot(u, w_ref[:, j * ATTN_W:(j + 1) * ATTN_W])
            if j < 2:
                pj = pj * cosv + _swap_halves(pj) * sgv
            if j == 0:
                pj = pj * (HEAD_DIM ** -0.5)
            qkv_ref[:, j * ATTN_W:(j + 1) * ATTN_W] = pj.astype(bf16)
        for j in range(4):
            lo = 3 * ATTN_W + j * HGRN_W
            hp_ref[:, j * HGRN_W:(j + 1) * HGRN_W] = _dot(u, w_ref[:, lo:lo + HGRN_W])

    return pl.pallas_call(
        body, name="in_proj", grid=(S // tm,),
        in_specs=[pl.BlockSpec((tm, D_MODEL), lambda i: (i, 0)), pl.BlockSpec((1, D_MODEL), lambda i: (0, 0)),
                  pl.BlockSpec((N_DEV, D_MODEL, w), lambda i: (0, 0, 0)),
                  pl.BlockSpec((tm, 128), lambda i: (i, 0)), pl.BlockSpec((tm, 128), lambda i: (i, 0))],
        out_specs=[pl.BlockSpec((tm, D_MODEL), lambda i: (i, 0)), pl.BlockSpec((tm, 3 * ATTN_W), lambda i: (i, 0)),
                   pl.BlockSpec((tm, 4 * HGRN_W), lambda i: (i, 0)), pl.BlockSpec((D_MODEL, IN_W), lambda i: (0, 0))],
        out_shape=[jax.ShapeDtypeStruct((S, D_MODEL), bf16), jax.ShapeDtypeStruct((S, 3 * ATTN_W), bf16),
                   jax.ShapeDtypeStruct((S, 4 * HGRN_W), f32), jax.ShapeDtypeStruct((D_MODEL, IN_W), bf16)],
        compiler_params=_cp("arbitrary"),
    )(x, w1, win_g, cos_t, sg_t)


def _head_masks():
    lane = lax.broadcasted_iota(jnp.int32, (ATTN_BLK, 128), 1)
    even = lane < HEAD_DIM
    return even, (even, jnp.logical_not(even))


def _pair_fwd(q2, k2, v2, bias):
    even, masks = _head_masks()
    outs, lses = [], []
    for e in range(2):
        qm = jnp.where(masks[e], q2, 0.0).astype(bf16)
        s = _dot_nt(qm, k2) + bias
        m = jnp.max(s, axis=-1, keepdims=True)
        pe = jnp.exp(s - m)
        lsum = jnp.sum(pe, axis=-1, keepdims=True)
        outs.append(_dot(pe.astype(bf16), v2) / lsum)
        lses.append(jnp.broadcast_to(m + jnp.log(lsum), (ATTN_BLK, 128)))
    return jnp.where(even, outs[0], outs[1]), jnp.where(even, lses[0], lses[1])


def _merge(y0, l0, y1, l1):
    mx = jnp.maximum(l0, l1)
    a, b = jnp.exp(l0 - mx), jnp.exp(l1 - mx)
    tot = a + b
    return (a * y0 + b * y1) / tot, mx + jnp.log(tot)


def _pair_bwd(q2, k2f, v2, dy2, lse2, delta2, bias):
    _, masks = _head_masks()
    k2 = k2f.astype(bf16)
    klane = lax.broadcasted_iota(jnp.int32, (2 * ATTN_BLK, 128), 1) < HEAD_DIM
    kmasks = (klane, jnp.logical_not(klane))
    dq2 = jnp.zeros((ATTN_BLK, 128), f32)
    pes, dss, qms, dyms = [], [], [], []
    for e in range(2):
        c0 = e * HEAD_DIM
        qm = jnp.where(masks[e], q2, 0.0).astype(bf16)
        km = jnp.where(kmasks[e], k2f, 0.0).astype(bf16)
        dym = jnp.where(masks[e], dy2, 0.0).astype(bf16)
        pe = jnp.exp(_dot_nt(qm, k2) + bias - lse2[:, c0:c0 + 1])
        ds = (pe * (_dot_nt(dym, v2) - delta2[:, c0:c0 + 1])).astype(bf16)
        dq2 = dq2 + _dot(ds, km)
        pes.append(pe.astype(bf16))
        dss.append(ds)
        qms.append(qm)
        dyms.append(dym)
    dv2 = _dot_tn(jnp.concatenate(pes, axis=0), jnp.concatenate(dyms, axis=0))
    dk2 = _dot_tn(jnp.concatenate(dss, axis=0), jnp.concatenate(qms, axis=0))
    return dq2, dk2, dv2


TOK = 2048


def _key_bias():
    qi = lax.broadcasted_iota(jnp.int32, (ATTN_BLK, 2 * ATTN_BLK), 0)
    kj = lax.broadcasted_iota(jnp.int32, (ATTN_BLK, 2 * ATTN_BLK), 1)
    delta = ATTN_BLK + qi - kj
    seen = (delta >= 0) & (delta <= ATTN_BLK)
    return jnp.where(seen, 0.0, NEG), jnp.where(seen & (kj >= ATTN_BLK), 0.0, NEG)


def _attn_fwd(qkv, rider=None):
    S = qkv.shape[0]
    nS = S // TOK

    def body(q_ref, kp_ref, kc_ref, vp_ref, vc_ref, y_ref, l_ref, qs, k2, v2, ay, al):
        n = pl.program_id(1)
        qs[...] = q_ref[...].astype(f32)
        k2[0:TOK] = kp_ref[...].astype(f32)
        k2[TOK:2 * TOK] = kc_ref[...].astype(f32)
        v2[0:TOK] = vp_ref[...].astype(f32)
        v2[TOK:2 * TOK] = vc_ref[...].astype(f32)
        bias_any, bias_first = _key_bias()

        def block(dil, r, b, step, last):
            start = r + pl.multiple_of(step * b, step)
            rows = pl.ds(start, ATTN_BLK, stride=dil) if dil > 1 else pl.ds(start, ATTN_BLK)
            keys = (pl.ds(TOK + start - step, 2 * ATTN_BLK, stride=dil) if dil > 1
                    else pl.ds(TOK + start - step, 2 * ATTN_BLK))
            bias = jnp.where((n == 0) & (b == 0), bias_first, bias_any)
            out, lse = _pair_fwd(qs[rows, :], k2[keys, :].astype(bf16), v2[keys, :].astype(bf16), bias)
            if dil < DILATIONS[-1]:
                out, lse = _merge(ay[rows, :], al[rows, :], out, lse)
            if last:
                y_ref[rows, :] = out
                l_ref[rows, :] = lse
            else:
                ay[rows, :] = out
                al[rows, :] = lse

        for dil in reversed(DILATIONS):
            def loop(i, carry, dil=dil):
                block(dil, i % dil, i // dil, ATTN_BLK * dil, dil == 1)
                return carry
            lax.fori_loop(0, TOK // ATTN_BLK, loop, 0, unroll=True)

    blk = (TOK, 128)
    cur = lambda c: pl.BlockSpec(blk, lambda p, n: (n, 4 * c + p))
    prv = lambda c: pl.BlockSpec(blk, lambda p, n: (jnp.maximum(n - 1, 0), 4 * c + p))
    out = pl.BlockSpec(blk, lambda p, n: (n, p))
    call = dict(in_specs=[cur(0), prv(1), cur(1), prv(2), cur(2)], out_specs=[out, out],
                out_shape=[jax.ShapeDtypeStruct((S, ATTN_W), f32)] * 2,
                scratch_shapes=[pltpu.VMEM(blk, f32), pltpu.VMEM((2 * TOK, 128), f32), pltpu.VMEM((2 * TOK, 128), f32),
                                pltpu.VMEM(blk, f32), pltpu.VMEM(blk, f32)])
    call, body, more = _ride(call, rider, body, lambda: pl.program_id(0) * nS + pl.program_id(1), (ATTN_W // 128) * nS, 5, 2, 5)
    return pl.pallas_call(body, name="attention_fwd", grid=(ATTN_W // 128, nS), compiler_params=_cp("arbitrary", "arbitrary"),
                          **call)(qkv, qkv, qkv, qkv, qkv, *more)


def _attn_bwd(qkv, ya, lse, dmix, rider=None):
    S = qkv.shape[0]
    nS = S // TOK

    def body(q_ref, kp_ref, kc_ref, vp_ref, vc_ref, y_ref, l_ref, dy_ref, dq_ref, dk_ref, dv_ref, qs, k2, v2, dk2, dv2, dqa, dl):
        n = pl.program_id(1)

        @pl.when(n == 0)
        def _():
            dk2[...] = jnp.zeros_like(dk2)
            dv2[...] = jnp.zeros_like(dv2)

        @pl.when(n < nS)
        def _():
            qs[...] = q_ref[...].astype(f32)
            k2[0:TOK] = kp_ref[...].astype(f32)
            k2[TOK:2 * TOK] = kc_ref[...].astype(f32)
            v2[0:TOK] = vp_ref[...].astype(f32)
            v2[TOK:2 * TOK] = vc_ref[...].astype(f32)
            li = lax.broadcasted_iota(jnp.int32, (128, 128), 0)
            lj = lax.broadcasted_iota(jnp.int32, (128, 128), 1)
            seg = jnp.where((li // HEAD_DIM) == (lj // HEAD_DIM), 1.0, 0.0).astype(bf16)
            bias_any, bias_first = _key_bias()

            def delta_rows(t, carry):
                rows = pl.ds(pl.multiple_of(256 * t, 256), 256)
                dyy = dy_ref[rows, :] * y_ref[rows, :]
                hi = dyy.astype(bf16)
                dl[rows, :] = _dot(hi, seg) + _dot((dyy - hi.astype(f32)).astype(bf16), seg)
                return carry

            lax.fori_loop(0, TOK // 256, delta_rows, 0)

            def block(dil, r, b, step, first_pattern, last):
                start = r + pl.multiple_of(step * b, step)
                rows = pl.ds(start, ATTN_BLK, stride=dil) if dil > 1 else pl.ds(start, ATTN_BLK)
                keys = (pl.ds(TOK + start - step, 2 * ATTN_BLK, stride=dil) if dil > 1
                        else pl.ds(TOK + start - step, 2 * ATTN_BLK))
                bias = jnp.where((n == 0) & (b == 0), bias_first, bias_any)
                dq2, dkk, dvv = _pair_bwd(qs[rows, :], k2[keys, :], v2[keys, :].astype(bf16), dy_ref[rows, :],
                                          l_ref[rows, :], dl[rows, :], bias)
                if last:
                    dq_ref[rows, :] = dqa[rows, :] + dq2
                elif first_pattern:
                    dqa[rows, :] = dq2
                else:
                    dqa[rows, :] += dq2
                dk2[keys, :] += dkk
                dv2[keys, :] += dvv

            for dil in reversed(DILATIONS):
                def loop(i, carry, dil=dil):
                    block(dil, i % dil, i // dil, ATTN_BLK * dil, dil == DILATIONS[-1], dil == 1)
                    return carry
                lax.fori_loop(0, TOK // ATTN_BLK, loop, 0, unroll=True)

        dk_ref[...] = dk2[0:TOK]
        dv_ref[...] = dv2[0:TOK]
        dk2[0:TOK] = dk2[TOK:2 * TOK]
        dv2[0:TOK] = dv2[TOK:2 * TOK]
        dk2[TOK:2 * TOK] = jnp.zeros((TOK, 128), f32)
        dv2[TOK:2 * TOK] = jnp.zeros((TOK, 128), f32)

    blk = (TOK, 128)
    cn = lambda n: jnp.minimum(n, nS - 1)
    pn = lambda n: jnp.clip(n - 1, 0, nS - 1)
    cur = lambda c: pl.BlockSpec(blk, lambda p, n: (cn(n), 4 * c + p))
    prv = lambda c: pl.BlockSpec(blk, lambda p, n: (pn(n), 4 * c + p))
    at_n = pl.BlockSpec(blk, lambda p, n: (cn(n), p))
    at_p = pl.BlockSpec(blk, lambda p, n: (pn(n), p))
    big = lambda: pltpu.VMEM((2 * TOK, 128), f32)
    call = dict(in_specs=[cur(0), prv(1), cur(1), prv(2), cur(2), at_n, at_n, at_n], out_specs=[at_n, at_p, at_p],
                out_shape=[jax.ShapeDtypeStruct((S, ATTN_W), f32)] * 3,
                scratch_shapes=[pltpu.VMEM(blk, f32), big(), big(), big(), big(), pltpu.VMEM(blk, f32), pltpu.VMEM(blk, f32)])
    call, body, more = _ride(call, rider, body, lambda: pl.program_id(0) * (nS + 1) + pl.program_id(1),
                             (ATTN_W // 128) * (nS + 1), 8, 3, 7)
    return pl.pallas_call(body, name="attention_bwd", grid=(ATTN_W // 128, nS + 1), compiler_params=_cp("arbitrary", "arbitrary"),
                          **call)(qkv, qkv, qkv, qkv, qkv, ya, lse, dmix, *more)


HG_T = 512
N_HH = HGRN_W // HGRN_HD
HG_SUB = 128
SAFE_RANGE = 75.0


def _row_in_chunk():
    return lax.broadcasted_iota(jnp.int32, (HG_T, HGRN_HD), 0) % CHUNK


def _chunk_cumsum(v, rc):
    k = 1
    while k < CHUNK:
        v = v + jnp.where(rc >= k, pltpu.roll(v, k, 0), 0.0)
        k *= 2
    return v


def _chunk_rcumsum(v, rc):
    k = 1
    while k < CHUNK:
        v = v + jnp.where(rc < CHUNK - k, pltpu.roll(v, HG_T - k, 0), 0.0)
        k *= 2
    return v


def _hgrn_gates(qb, fb, lb):
    sf = _sigmoid(fb)
    f = lb + (1.0 - lb) * sf
    sq = _sigmoid(qb)
    return sf, f, jnp.log(f), 1.0 - f, sq, qb * sq


def _hgrn_prep(qb, fb, lbl2, rc):
    lb = _sigmoid(lbl2[0:1, :] - lbl2[1:2, :])
    sf, f, lf, key, sq, qf = _hgrn_gates(qb, fb, lb)
    b = _chunk_cumsum(lf, rc)
    rem = _chunk_rcumsum(lf, rc) - lf
    return dict(lb=lb, sf=sf, f=f, key=key, sq=sq, qf=qf, b=b, rem=rem, eb=jnp.exp(b), er=jnp.exp(rem))


def _chunk_mask():
    r = lax.broadcasted_iota(jnp.int32, (HG_SUB, HG_SUB), 0)
    c = lax.broadcasted_iota(jnp.int32, (HG_SUB, HG_SUB), 1)
    return ((r // CHUNK) == (c // CHUNK)) & (c <= r)


def _hgrn_fwd(hp, lbl, wn):
    S = hp.shape[0]
    nT = S // HG_T

    def body(qb_ref, fb_ref, ib_ref, gb_ref, lbl_ref, wn_ref, yb_ref, o_ref, st_ref, ST, qt_s, kh_s, dec_s, oi_s):
        @pl.when(pl.program_id(0) == 0)
        def _():
            ST[...] = jnp.zeros_like(ST)

        rc = _row_in_chunk()
        for h in range(N_HH):
            sl = slice(HGRN_HD * h, HGRN_HD * (h + 1))
            p = _hgrn_prep(qb_ref[:, sl], fb_ref[:, sl], lbl_ref[:, sl], rc)
            qf, key, b = p["qf"], p["key"], p["b"]
            qt = qf * p["eb"]
            qt_s[:, sl] = qt.astype(bf16)
            kh_s[:, sl] = (key * p["er"]).astype(bf16)
            dec_s[:, sl] = jnp.exp(b + p["rem"])
            rng = jnp.max(-(b + p["rem"]))

            @pl.when(rng < SAFE_RANGE)
            def _():
                kp = (key * jnp.exp(-b)).astype(bf16)
                cmask = _chunk_mask()
                for j in range(HG_T // HG_SUB):
                    rs = slice(HG_SUB * j, HG_SUB * (j + 1))
                    sc = jnp.where(cmask, _dot_nt(qt[rs].astype(bf16), kp[rs]), 0.0).astype(bf16)
                    oi_s[rs, sl] = _dot(sc, ib_ref[rs, sl].astype(bf16))

            @pl.when(rng >= SAFE_RANGE)
            def _():
                v = ib_ref[:, sl]
                ones = jnp.ones((HGRN_HD, HGRN_HD), bf16)

                def lag(l, o):
                    e = jnp.exp(jnp.where(rc >= l, b - pltpu.roll(b, l, 0), NEG))
                    pr = qf * pltpu.roll(key, l, 0) * e
                    return o + _dot(pr.astype(bf16), ones) * pltpu.roll(v, l, 0)

                oi_s[:, sl] = lax.fori_loop(1, CHUNK, lag, _dot((qf * key).astype(bf16), ones) * v)

        def step(c, carry):
            rows = pl.ds(pl.multiple_of(c * CHUNK, CHUNK), CHUNK)
            row0 = pl.ds(pl.multiple_of(c * CHUNK, CHUNK), 1)
            for h in range(N_HH):
                sl = slice(HGRN_HD * h, HGRN_HD * (h + 1))
                stv = ST[h]
                st_ref[c, sl, :] = stv
                oi_s[rows, sl] += _dot_nt(qt_s[rows, sl], stv.astype(bf16))
                ST[h] = stv * dec_s[row0, sl] + _dot_tn(ib_ref[rows, sl].astype(bf16), kh_s[rows, sl])
            return carry

        lax.fori_loop(0, HG_T // CHUNK, step, 0, unroll=True)

        for h in range(N_HH):
            sl = slice(HGRN_HD * h, HGRN_HD * (h + 1))
            o = oi_s[:, sl]
            o_ref[:, sl] = o
            on = o * lax.rsqrt(jnp.mean(o * o, axis=-1, keepdims=True) + EPS)
            g = gb_ref[:, sl]
            yb_ref[:, sl] = on * wn_ref[:, sl] * (g * _sigmoid(g))

    col = lambda c: pl.BlockSpec((HG_T, HGRN_W), lambda i: (i, c))
    tile = pl.BlockSpec((HG_T, HGRN_W), lambda i: (i, 0))
    whole = lambda a: pl.BlockSpec(a.shape, lambda i: (0, 0))
    return pl.pallas_call(
        body, name="hgrn_fwd", grid=(nT,),
        in_specs=[col(0), col(1), col(2), col(3), whole(lbl), whole(wn)],
        out_specs=[tile, tile, pl.BlockSpec((HG_T // CHUNK, HGRN_W, HGRN_HD), lambda i: (i, 0, 0))],
        out_shape=[jax.ShapeDtypeStruct((S, HGRN_W), f32), jax.ShapeDtypeStruct((S, HGRN_W), f32),
                   jax.ShapeDtypeStruct((S // CHUNK, HGRN_W, HGRN_HD), f32)],
        scratch_shapes=[pltpu.VMEM((N_HH, HGRN_HD, HGRN_HD), f32), pltpu.VMEM((HG_T, HGRN_W), bf16),
                        pltpu.VMEM((HG_T, HGRN_W), bf16), pltpu.VMEM((HG_T, HGRN_W), f32), pltpu.VMEM((HG_T, HGRN_W), f32)],
        compiler_params=_cp("arbitrary"),
    )(hp, hp, hp, hp, lbl, wn)


def _hgrn_bwd(hp, lbl, wn, o_sav, states, dmix, rider=None):
    S = hp.shape[0]
    nT = S // HG_T

    def body(qb_ref, fb_ref, ib_ref, gb_ref, lbl_ref, wn_ref, o_ref, st_ref, dy_ref,
             dq_ref, df_ref, di_ref, dg_ref, gwn_ref, glb_ref,
             DST, qt_s, kh_s, dec_s, do_s, dqt_s, dkh_s, dbl_s, dvi_s, dqi_s, dki_s, dbi_s):
        @pl.when(pl.program_id(0) == 0)
        def _():
            DST[...] = jnp.zeros_like(DST)
            gwn_ref[...] = jnp.zeros_like(gwn_ref)
            glb_ref[...] = jnp.zeros_like(glb_ref)

        rc = _row_in_chunk()
        preps = []
        for h in range(N_HH):
            sl = slice(HGRN_HD * h, HGRN_HD * (h + 1))
            p = _hgrn_prep(qb_ref[:, sl], fb_ref[:, sl], lbl_ref[:, sl], rc)
            preps.append(p)
            qf, key, b = p["qf"], p["key"], p["b"]
            v = ib_ref[:, sl]
            o = o_ref[:, sl]
            rinv = lax.rsqrt(jnp.mean(o * o, axis=-1, keepdims=True) + EPS)
            on = o * rinv
            g = gb_ref[:, sl]
            sgm = _sigmoid(g)
            silu_g = g * sgm
            dy = dy_ref[:, sl]
            wn_v = wn_ref[:, sl]
            gwn_ref[:, sl] += jnp.sum(dy * on * silu_g, axis=0, keepdims=True)
            dg_ref[:, sl] = (dy * on * wn_v * (sgm * (1.0 + g * (1.0 - sgm)))).astype(bf16)
            t1 = dy * wn_v * silu_g
            do = rinv * (t1 - on * jnp.mean(t1 * on, axis=-1, keepdims=True))
            do_s[:, sl] = do.astype(bf16)
            qt = qf * p["eb"]
            qt_s[:, sl] = qt.astype(bf16)
            kh_s[:, sl] = (key * p["er"]).astype(bf16)
            dec_s[:, sl] = jnp.exp(b + p["rem"])
            rng = jnp.max(-(b + p["rem"]))

            @pl.when(rng < SAFE_RANGE)
            def _():
                einv = jnp.exp(-b)
                kp = (key * einv).astype(bf16)
                cmask = _chunk_mask()
                for j in range(HG_T // HG_SUB):
                    rs = slice(HG_SUB * j, HG_SUB * (j + 1))
                    qtb, dob, vb = qt[rs].astype(bf16), do[rs].astype(bf16), v[rs].astype(bf16)
                    sc = jnp.where(cmask, _dot_nt(qtb, kp[rs]), 0.0).astype(bf16)
                    dsc = jnp.where(cmask, _dot_nt(dob, vb), 0.0).astype(bf16)
                    dqp = _dot(dsc, kp[rs])
                    dkp = _dot_tn(dsc, qtb)
                    dvi_s[rs, sl] = _dot_tn(sc, dob)
                    dqi_s[rs, sl] = dqp * p["eb"][rs]
                    dki_s[rs, sl] = dkp * einv[rs]
                    dbi_s[rs, sl] = dqp * qtb.astype(f32) - dkp * kp[rs].astype(f32)

            @pl.when(rng >= SAFE_RANGE)
            def _():
                ones = jnp.ones((HGRN_HD, HGRN_HD), bf16)

                def lag(l, carry):
                    dqf, dkey, db, dv = carry
                    e = jnp.exp(jnp.where(rc >= l, b - pltpu.roll(b, l, 0), NEG))
                    ks, vs, qe = pltpu.roll(key, l, 0), pltpu.roll(v, l, 0), qf * e
                    pr = qe * ks
                    rl = _dot(pr.astype(bf16), ones)
                    drl = jnp.where(rc >= l, _dot((do * vs).astype(bf16), ones), 0.0)
                    gl = drl * pr
                    back = HG_T - l
                    return (dqf + drl * ks * e, dkey + pltpu.roll(drl * qe, back, 0), db + gl - pltpu.roll(gl, back, 0),
                            dv + pltpu.roll(rl * do, back, 0))

                rl0 = _dot((qf * key).astype(bf16), ones)
                drl0 = _dot((do * v).astype(bf16), ones)
                dqf, dkey, db, dv = lax.fori_loop(1, CHUNK, lag, (drl0 * key, drl0 * qf, jnp.zeros((HG_T, HGRN_HD), f32), rl0 * do))
                dvi_s[:, sl] = dv
                dqi_s[:, sl] = dqf
                dki_s[:, sl] = dkey
                dbi_s[:, sl] = db

        def step(k, carry):
            c = HG_T // CHUNK - 1 - k
            rows = pl.ds(pl.multiple_of(c * CHUNK, CHUNK), CHUNK)
            row0 = pl.ds(pl.multiple_of(c * CHUNK, CHUNK), 1)
            for h in range(N_HH):
                sl = slice(HGRN_HD * h, HGRN_HD * (h + 1))
                stp = st_ref[c, sl, :]
                dst = DST[h]
                dstb = dst.astype(bf16)
                dob = do_s[rows, sl]
                khb = kh_s[rows, sl]
                dec = dec_s[row0, sl]
                dqt_s[rows, sl] = _dot(dob, stp.astype(bf16))
                dkh = _dot(ib_ref[rows, sl].astype(bf16), dstb)
                dkh_s[rows, sl] = dkh
                dvi_s[rows, sl] += _dot_nt(khb, dstb)
                dbl = jnp.sum(dst * stp, axis=0, keepdims=True) * dec + jnp.sum(dkh * khb.astype(f32), axis=0, keepdims=True)
                dbl_s[rows, sl] = jnp.broadcast_to(dbl, (CHUNK, HGRN_HD))
                DST[h] = dst * dec + _dot_tn(dob, qt_s[rows, sl])
            return carry

        lax.fori_loop(0, HG_T // CHUNK, step, 0, unroll=True)

        for h in range(N_HH):
            sl = slice(HGRN_HD * h, HGRN_HD * (h + 1))
            qb = qb_ref[:, sl]
            p = preps[h]
            sf, sq, lb = p["sf"], p["sq"], p["lb"]
            dqt, dkh = dqt_s[:, sl], dkh_s[:, sl]
            dqf = dqt * p["eb"] + dqi_s[:, sl]
            dkey = dkh * p["er"] + dki_s[:, sl]
            db = dqt * (p["qf"] * p["eb"]) - dkh * (p["key"] * p["er"]) + jnp.where(rc == CHUNK - 1, dbl_s[:, sl], 0.0) + dbi_s[:, sl]
            df = _chunk_rcumsum(db, rc) / p["f"] - dkey
            df_ref[:, sl] = (df * (1.0 - lb) * sf * (1.0 - sf)).astype(bf16)
            glb_ref[:, sl] += jnp.sum(df * (1.0 - sf), axis=0, keepdims=True)
            dq_ref[:, sl] = (dqf * (sq * (1.0 + qb * (1.0 - sq)))).astype(bf16)
            di_ref[:, sl] = dvi_s[:, sl].astype(bf16)

    rev = lambda i: nT - 1 - i
    col = lambda c: pl.BlockSpec((HG_T, HGRN_W), lambda i: (rev(i), c))
    tile = pl.BlockSpec((HG_T, HGRN_W), lambda i: (rev(i), 0))
    whole = lambda a: pl.BlockSpec(a.shape, lambda i: (0, 0))
    vec = pl.BlockSpec((1, HGRN_W), lambda i: (0, 0))
    tb = lambda: pltpu.VMEM((HG_T, HGRN_W), bf16)
    tf = lambda: pltpu.VMEM((HG_T, HGRN_W), f32)
    call = dict(in_specs=[col(0), col(1), col(2), col(3), whole(lbl), whole(wn), tile,
                          pl.BlockSpec((HG_T // CHUNK, HGRN_W, HGRN_HD), lambda i: (rev(i), 0, 0)),
                          pl.BlockSpec((HG_T, HGRN_W), lambda i: (rev(i), 1))],
                out_specs=[tile, tile, tile, tile, vec, vec],
                out_shape=[jax.ShapeDtypeStruct((S, HGRN_W), bf16)] * 4 + [jax.ShapeDtypeStruct((1, HGRN_W), f32)] * 2,
                scratch_shapes=[pltpu.VMEM((N_HH, HGRN_HD, HGRN_HD), f32), tb(), tb(), tf(), tb(), tf(), tf(), tf(), tf(), tf(),
                                tf(), tf()])
    call, body, more = _ride(call, rider, body, lambda: pl.program_id(0), nT, 9, 6, 12)
    return pl.pallas_call(body, name="hgrn_bwd", grid=(nT,), compiler_params=_cp("arbitrary"), **call)(
        hp, hp, hp, hp, lbl, wn, o_sav, states, dmix, *more)


def _out_proj(x, ya, yb, wout, w2):
    S = x.shape[0]
    tm = 512

    def body(x_ref, ya_ref, yb_ref, w_ref, w2_ref, h1_ref, u2_ref, mix_ref):
        mixed = jnp.concatenate([ya_ref[...], yb_ref[...]], axis=1).astype(bf16)
        mix_ref[...] = mixed
        h1 = x_ref[...] + _dot(mixed, w_ref[...])
        h1_ref[...] = h1
        r = lax.rsqrt(jnp.mean(h1 * h1, axis=-1, keepdims=True) + EPS)
        u2_ref[...] = (h1 * r * w2_ref[...]).astype(bf16)

    row = lambda w: pl.BlockSpec((tm, w), lambda i: (i, 0))
    return pl.pallas_call(
        body, name="out_proj", grid=(S // tm,),
        in_specs=[row(D_MODEL), row(ATTN_W), row(HGRN_W), pl.BlockSpec((D_MODEL, D_MODEL), lambda i: (0, 0)),
                  pl.BlockSpec((1, D_MODEL), lambda i: (0, 0))],
        out_specs=[row(D_MODEL), row(D_MODEL), row(D_MODEL)],
        out_shape=[jax.ShapeDtypeStruct((S, D_MODEL), f32), jax.ShapeDtypeStruct((S, D_MODEL), bf16),
                   jax.ShapeDtypeStruct((S, D_MODEL), bf16)],
        compiler_params=_cp("arbitrary"),
    )(x, ya, yb, wout, w2)


def _gate_up(u2, wgu_g):
    S = u2.shape[0]
    w = 2 * FFN // N_DEV
    tm, tn = 512, 2 * w
    nj = FFN // tn

    def body(u_ref, wgg_ref, wug_ref, g_ref, up_ref, a_ref, wg_ref, wu_ref):
        @pl.when(pl.program_id(1) == 0)
        def _():
            for k in range(2):
                wg_ref[:, w * k:w * (k + 1)] = wgg_ref[k]
                wu_ref[:, w * k:w * (k + 1)] = wug_ref[k]

        u = u_ref[...]
        g = _dot(u, wg_ref[...])
        up = _dot(u, wu_ref[...])
        sg = _sigmoid(g)
        silu = g * sg
        g_ref[...] = silu.astype(bf16)
        up_ref[...] = (up * (sg + silu * (1.0 - sg))).astype(bf16)
        a_ref[...] = (silu * up).astype(bf16)

    out = pl.BlockSpec((tm, tn), lambda j, i: (i, j))
    wout = pl.BlockSpec((D_MODEL, tn), lambda j, i: (0, j))
    return pl.pallas_call(
        body, name="gate_up", grid=(nj, S // tm),
        in_specs=[pl.BlockSpec((tm, D_MODEL), lambda j, i: (i, 0)), pl.BlockSpec((2, D_MODEL, w), lambda j, i: (j, 0, 0)),
                  pl.BlockSpec((2, D_MODEL, w), lambda j, i: (j + nj, 0, 0))],
        out_specs=[out, out, out, wout, wout],
        out_shape=[jax.ShapeDtypeStruct((S, FFN), bf16)] * 3 + [jax.ShapeDtypeStruct((D_MODEL, FFN), bf16)] * 2,
        compiler_params=_cp("arbitrary", "arbitrary"),
    )(u2, wgu_g, wgu_g)


def _rms_bwd(dyw, hn, r):
    return r * (dyw - hn * jnp.mean(dyw * hn, axis=-1, keepdims=True))


def _down_loss(act, wdown, h1, tgt, w3):
    S = act.shape[0]
    tm = 512

    def body(a_ref, w_ref, h1_ref, t_ref, w3_ref, dh2_ref, loss_ref, gw3_ref):
        @pl.when(pl.program_id(0) == 0)
        def _():
            loss_ref[...] = jnp.zeros_like(loss_ref)
            gw3_ref[...] = jnp.zeros_like(gw3_ref)

        h2 = h1_ref[...] + _dot(a_ref[...], w_ref[...])
        r = lax.rsqrt(jnp.mean(h2 * h2, axis=-1, keepdims=True) + EPS)
        hn = h2 * r
        w3 = w3_ref[...]
        err = hn * w3 - t_ref[...]
        loss_ref[...] += (0.5 / D_MODEL) * jnp.sum(err * err)
        dy = err * (1.0 / D_MODEL)
        gw3_ref[...] += jnp.sum(dy * hn, axis=0, keepdims=True)
        dh2_ref[...] = _rms_bwd(dy * w3, hn, r)

    row = lambda w: pl.BlockSpec((tm, w), lambda i: (i, 0))
    return pl.pallas_call(
        body, name="down_loss", grid=(S // tm,),
        in_specs=[row(FFN), pl.BlockSpec((FFN, D_MODEL), lambda i: (0, 0)), row(D_MODEL), row(D_MODEL),
                  pl.BlockSpec((1, D_MODEL), lambda i: (0, 0))],
        out_specs=[row(D_MODEL), pl.BlockSpec((1, 128), lambda i: (0, 0)), pl.BlockSpec((1, D_MODEL), lambda i: (0, 0))],
        out_shape=[jax.ShapeDtypeStruct((S, D_MODEL), f32), jax.ShapeDtypeStruct((1, 128), f32),
                   jax.ShapeDtypeStruct((1, D_MODEL), f32)],
        compiler_params=_cp("arbitrary"),
    )(act, wdown, h1, tgt, w3)


def _dact(dh2, wdown, silu, up_dsilu):
    S = dh2.shape[0]
    tm = 256

    def body(d_ref, w_ref, s_ref, u_ref, o_ref):
        da = _dot_nt(d_ref[...].astype(bf16), w_ref[...])
        o_ref[1] = (da * s_ref[...].astype(f32)).astype(bf16)
        o_ref[0] = (da * u_ref[...].astype(f32)).astype(bf16)

    row = lambda w: pl.BlockSpec((tm, w), lambda i: (i, 0))
    return pl.pallas_call(
        body, name="dact", grid=(S // tm,),
        in_specs=[row(D_MODEL), pl.BlockSpec((FFN, D_MODEL), lambda i: (0, 0)), row(FFN), row(FFN)],
        out_specs=pl.BlockSpec((2, tm, FFN), lambda i: (0, i, 0)),
        out_shape=jax.ShapeDtypeStruct((2, S, FFN), bf16),
        compiler_params=_cp("arbitrary"),
    )(dh2, wdown, silu, up_dsilu)


def _dgu(dgu2, wgate, wup, h1, w2, dh2, wout, rider=None):
    S = dgu2.shape[1]
    tm = 256

    def body(d_ref, wg_ref, wu_ref, h1_ref, w2_ref, dh2_ref, wo_ref, dh1_ref, gw2_ref, dmix_ref):
        @pl.when(pl.program_id(0) == 0)
        def _():
            gw2_ref[...] = jnp.zeros_like(gw2_ref)

        du2 = _dot_nt(d_ref[0], wg_ref[...]) + _dot_nt(d_ref[1], wu_ref[...])
        h1 = h1_ref[...]
        r = lax.rsqrt(jnp.mean(h1 * h1, axis=-1, keepdims=True) + EPS)
        hn = h1 * r
        gw2_ref[...] += jnp.sum(du2 * hn, axis=0, keepdims=True)
        dh1 = dh2_ref[...] + _rms_bwd(du2 * w2_ref[...], hn, r)
        dh1_ref[...] = dh1
        dmix_ref[...] = _dot_nt(dh1.astype(bf16), wo_ref[...])

    row = lambda w: pl.BlockSpec((tm, w), lambda i: (i, 0))
    call = dict(in_specs=[pl.BlockSpec((2, tm, FFN), lambda i: (0, i, 0)), pl.BlockSpec((D_MODEL, FFN), lambda i: (0, 0)),
                          pl.BlockSpec((D_MODEL, FFN), lambda i: (0, 0)), row(D_MODEL),
                          pl.BlockSpec((1, D_MODEL), lambda i: (0, 0)), row(D_MODEL),
                          pl.BlockSpec((D_MODEL, D_MODEL), lambda i: (0, 0))],
                out_specs=[row(D_MODEL), pl.BlockSpec((1, D_MODEL), lambda i: (0, 0)), row(D_MODEL)],
                out_shape=[jax.ShapeDtypeStruct((S, D_MODEL), f32), jax.ShapeDtypeStruct((1, D_MODEL), f32),
                           jax.ShapeDtypeStruct((S, D_MODEL), f32)], scratch_shapes=[])
    call, body, more = _ride(call, rider, body, lambda: pl.program_id(0), S // tm, 7, 3, 0)
    return pl.pallas_call(body, name="dgu", grid=(S // tm,), compiler_params=_cp("arbitrary"), **call)(
        dgu2, wgate, wup, h1, w2, dh2, wout, *more)


def _din(dq, dk, dv, dhq, dhf, dhi, dhg, cos_t, sg_t, win, x, w1, dh1):
    S = x.shape[0]
    tm = 512

    def body(dq_ref, dk_ref, dv_ref, dhq_ref, dhf_ref, dhi_ref, dhg_ref, cos_ref, sg_ref, w_ref, x_ref, w1_ref, dh1_ref,
             dp_ref, gx_ref, gw1_ref):
        @pl.when(pl.program_id(0) == 0)
        def _():
            gw1_ref[...] = jnp.zeros_like(gw1_ref)

        cosv, sgv = jnp.tile(cos_ref[...], (1, ATTN_W // 128)), jnp.tile(sg_ref[...], (1, ATTN_W // 128))
        unrope = lambda d: d * cosv - sgv * _swap_halves(d)
        parts = [(unrope(dq_ref[...]) * (HEAD_DIM ** -0.5)).astype(bf16), unrope(dk_ref[...]).astype(bf16),
                 dv_ref[...].astype(bf16), dhq_ref[...], dhf_ref[...], dhi_ref[...], dhg_ref[...]]
        du = jnp.zeros((tm, D_MODEL), f32)
        for j, pj in enumerate(parts):
            dp_ref[:, j * 512:(j + 1) * 512] = pj
            du = du + _dot_nt(pj, w_ref[:, j * 512:(j + 1) * 512])
        xv = x_ref[...]
        r = lax.rsqrt(jnp.mean(xv * xv, axis=-1, keepdims=True) + EPS)
        xn = xv * r
        gw1_ref[...] += jnp.sum(du * xn, axis=0, keepdims=True)
        gx_ref[...] = dh1_ref[...] + _rms_bwd(du * w1_ref[...], xn, r)

    row = lambda w: pl.BlockSpec((tm, w), lambda i: (i, 0))
    vec = pl.BlockSpec((1, D_MODEL), lambda i: (0, 0))
    return pl.pallas_call(
        body, name="din", grid=(S // tm,),
        in_specs=[row(512)] * 7 + [row(128), row(128), pl.BlockSpec((D_MODEL, IN_W), lambda i: (0, 0)), row(D_MODEL), vec,
                                   row(D_MODEL)],
        out_specs=[row(IN_W), row(D_MODEL), vec],
        out_shape=[jax.ShapeDtypeStruct((S, IN_W), bf16), jax.ShapeDtypeStruct((S, D_MODEL), f32),
                   jax.ShapeDtypeStruct((1, D_MODEL), f32)],
        compiler_params=_cp("arbitrary"),
    )(dq, dk, dv, dhq, dhf, dhi, dhg, cos_t, sg_t, win, x, w1, dh1)


def _gw(a, bs, tn, name, ts=2048):
    S, M = a.shape
    N = bs[0].shape[1]
    k = len(bs)

    def body(a_ref, *refs):
        @pl.when(pl.program_id(1) == 0)
        def _():
            for o_ref in refs[k:]:
                o_ref[...] = jnp.zeros_like(o_ref)

        at = a_ref[...].astype(bf16)
        for b_ref, o_ref in zip(refs[:k], refs[k:]):
            o_ref[...] += _dot_tn(at, b_ref[...].astype(bf16))

    return pl.pallas_call(
        body, name=name, grid=(N // tn, S // ts),
        in_specs=[pl.BlockSpec((ts, M), lambda j, s: (s, 0))] + [pl.BlockSpec((ts, tn), lambda j, s: (s, j))] * k,
        out_specs=[pl.BlockSpec((M, tn), lambda j, s: (0, j))] * k, out_shape=[jax.ShapeDtypeStruct((M, N), f32)] * k,
        compiler_params=_cp("arbitrary", "arbitrary"),
    )(a, *bs)


def _gw_by_owner(a, b3, w, name, ts):
    S, M = a.shape
    G, _, Ng = b3.shape
    tn = 2 * w
    per_group = Ng // tn
    n_s = S // ts

    def body(a_ref, b_ref, o_ref, acc):
        s = pl.program_id(1)

        @pl.when(s == 0)
        def _():
            acc[...] = jnp.zeros_like(acc)

        acc[...] += _dot_tn(a_ref[...].astype(bf16), b_ref[0].astype(bf16))

        @pl.when(s == n_s - 1)
        def _():
            o_ref[0] = acc[:, 0:w]
            o_ref[1] = acc[:, w:tn]

    return pl.pallas_call(
        body, name=name, grid=(G * per_group, n_s),
        in_specs=[pl.BlockSpec((ts, M), lambda j, s: (s, 0)),
                  pl.BlockSpec((1, ts, tn), lambda j, s: (j // per_group, s, j % per_group))],
        out_specs=pl.BlockSpec((2, M, w), lambda j, s: (j, 0, 0)), out_shape=jax.ShapeDtypeStruct((G * Ng // w, M, w), f32),
        scratch_shapes=[pltpu.VMEM((M, tn), f32)], compiler_params=_cp("arbitrary", "arbitrary"),
    )(a, b3)


MESH = pl.DeviceIdType.MESH
ANY = pl.BlockSpec(memory_space=pl.ANY)
VMEM_SPEC = pl.BlockSpec(memory_space=pltpu.VMEM)


def _pos():
    return lax.axis_index("x"), lax.axis_index("y"), lax.axis_index("c")


def _flip(v, bit):
    return 1 - v if bit else v


def _gather_rider(shards):
    n = len(shards)

    def parts(outs, scratch):
        send_sems, recv_sems, local_sems = scratch[n:]
        x, y, c = _pos()
        chips = [(1 - x, y), (x, 1 - y), (1 - x, 1 - y)]

        def copy(a, k, block, to, src=None):
            dst = outs[a].at[4 * block[0] + 2 * block[1] + block[2]]
            return pltpu.make_async_remote_copy(src_ref=dst if src is None else src, dst_ref=dst, send_sem=send_sems.at[a, k],
                                                recv_sem=recv_sems.at[a, k], device_id=to, device_id_type=MESH)

        bufs = scratch[:n]
        me, sibling = (x, y, c), (x, y, 1 - c)
        own = lambda a: pltpu.make_async_copy(bufs[a], outs[a].at[4 * x + 2 * y + c], local_sems.at[a])
        sent = lambda a: [copy(a, 0, me, sibling, src=bufs[a])] + [copy(a, 1 + j, me, (*chip, c), src=bufs[a])
                                                                   for j, chip in enumerate(chips)]
        passed = lambda a: [copy(a, 4 + j, (*chip, c), sibling) for j, chip in enumerate(chips)]
        landed = lambda a: [copy(a, 1 + j, (*chip, c), me) for j, chip in enumerate(chips)]
        from_sibling = lambda a: [copy(a, 0, sibling, me)] + [copy(a, 4 + j, (*chip, 1 - c), me) for j, chip in enumerate(chips)]
        return bufs, local_sems, own, sent, passed, landed, from_sibling

    def first(ins, outs, scratch):
        bufs, local_sems, own, sent, _, _, _ = parts(outs, scratch)
        loads = [pltpu.make_async_copy(ins[a], bufs[a], local_sems.at[a]) for a in range(n)]
        for ld in loads:
            ld.start()
        for a in range(n):
            loads[a].wait()
            own(a).start()
            for cp in sent(a):
                cp.start()

    def middle(ins, outs, scratch):
        _, _, _, _, passed, landed, _ = parts(outs, scratch)
        for a in range(n):
            for got, on in zip(landed(a), passed(a)):
                got.wait_recv()
                on.start()

    def last(ins, outs, scratch):
        _, _, own, sent, passed, _, from_sibling = parts(outs, scratch)
        for a in range(n):
            for cp in from_sibling(a):
                cp.wait_recv()
        for a in range(n):
            for cp in sent(a) + passed(a):
                cp.wait_send()
            own(a).wait()

    return _Rider(shards, [jax.ShapeDtypeStruct((N_DEV,) + s.shape, s.dtype) for s in shards],
                  [pltpu.VMEM(s.shape, s.dtype) for s in shards]
                  + [pltpu.SemaphoreType.DMA((n, 7)), pltpu.SemaphoreType.DMA((n, 7)), pltpu.SemaphoreType.DMA((n,))],
                  first, last, middle)


def _sibling_rider(grads):
    n = len(grads)

    def copies(g, got, scratch):
        send_sems, recv_sems = scratch
        x, y, c = _pos()
        return [pltpu.make_async_remote_copy(src_ref=g[a].at[2 * q + (1 - c)], dst_ref=got[a].at[q], send_sem=send_sems.at[a, q],
                                             recv_sem=recv_sems.at[a, q], device_id=(x, y, 1 - c), device_id_type=MESH)
                for a in range(n) for q in range(4)]

    def first(g, got, scratch):
        for cp in copies(g, got, scratch):
            cp.start()

    def last(g, got, scratch):
        for cp in copies(g, got, scratch):
            cp.wait()

    return _Rider(grads, [jax.ShapeDtypeStruct((4,) + g.shape[1:], g.dtype) for g in grads],
                  [pltpu.SemaphoreType.DMA((n, 4))] * 2, first, last)


def _chips_rider(sums):
    n = len(sums)

    def copies(s, out, scratch):
        send_sems, recv_sems = scratch
        x, y, c = _pos()
        cps = []
        for a in range(n):
            for f in (1, 2, 3):
                peer = (_flip(x, f >> 1), _flip(y, f & 1), c)
                cps.append(pltpu.make_async_remote_copy(
                    src_ref=s[a].at[2 * peer[0] + peer[1]], dst_ref=out[a].at[f - 1], send_sem=send_sems.at[a, f - 1],
                    recv_sem=recv_sems.at[a, f - 1], device_id=peer, device_id_type=MESH))
        return cps

    def first(s, out, scratch):
        for cp in copies(s, out, scratch):
            cp.start()

    def last(s, out, scratch):
        for cp in copies(s, out, scratch):
            cp.wait()

    return _Rider(sums, [jax.ShapeDtypeStruct((3,) + s.shape[1:], s.dtype) for s in sums],
                  [pltpu.SemaphoreType.DMA((n, 3))] * 2, first, last)


def _add_and_send(g, got, name):
    _, r, c = got.shape

    def body(g_ref, got_ref, out_ref, a_buf, b_buf, s_buf, load_sems, send_sems, recv_sems):
        x, y, cc = _pos()
        copies = []
        for f in (1, 2, 3):
            peer = (_flip(x, f >> 1), _flip(y, f & 1), cc)
            qd = 2 * peer[0] + peer[1]
            mine = pltpu.make_async_copy(g_ref.at[2 * qd + cc], a_buf, load_sems.at[0])
            theirs = pltpu.make_async_copy(got_ref.at[qd], b_buf, load_sems.at[1])
            mine.start()
            theirs.start()
            mine.wait()
            theirs.wait()
            s_buf[f - 1] = (a_buf[...] + b_buf[...]).astype(bf16)
            cp = pltpu.make_async_remote_copy(src_ref=s_buf.at[f - 1], dst_ref=out_ref.at[f - 1], send_sem=send_sems.at[f - 1],
                                              recv_sem=recv_sems.at[f - 1], device_id=peer, device_id_type=MESH)
            cp.start()
            copies.append(cp)
        for cp in copies:
            cp.wait()

    return pl.pallas_call(
        body, name=name, in_specs=[ANY, ANY], out_specs=ANY, out_shape=jax.ShapeDtypeStruct((3, r, c), bf16),
        scratch_shapes=[pltpu.VMEM((r, c), f32), pltpu.VMEM((r, c), f32), pltpu.VMEM((3, r, c), bf16),
                        pltpu.SemaphoreType.DMA((2,)), pltpu.SemaphoreType.DMA((3,)), pltpu.SemaphoreType.DMA((3,))],
    )(g, got)


def _both(a, b):
    na = (len(a.ins), len(a.out_shapes), len(a.scratch))

    def split(fa, fb):
        def f(ins, outs, scratch):
            fa(ins[:na[0]], outs[:na[1]], scratch[:na[2]])
            fb(ins[na[0]:], outs[na[1]:], scratch[na[2]:])
        return f

    return _Rider(a.ins + b.ins, a.out_shapes + b.out_shapes, a.scratch + b.scratch, split(a.first, b.first), split(a.last, b.last))


def _alone(rider, name):
    ri, ro = len(rider.ins), len(rider.out_shapes)

    def body(*refs):
        theirs = (refs[:ri], refs[ri:ri + ro], refs[ri + ro:])
        rider.first(*theirs)
        if rider.middle is not None:
            rider.middle(*theirs)
        rider.last(*theirs)

    return pl.pallas_call(body, name=name, in_specs=[ANY] * ri, out_specs=[ANY] * ro, out_shape=rider.out_shapes,
                          scratch_shapes=rider.scratch)(*rider.ins)


def _gather_small(g_w1, g_w2, g_w3, g_lb, g_wn, loss):
    def body(w1_ref, w2_ref, w3_ref, lb_ref, wn_ref, loss_ref, out_ref, pk, send_sems, recv_sems):
        x, y, c = _pos()
        me = 4 * x + 2 * y + c
        pk[...] = jnp.zeros_like(pk)
        pk[0:1, :] = w1_ref[...]
        pk[1:2, :] = w2_ref[...]
        pk[2:3, :] = w3_ref[...]
        pk[3:4, 0:HGRN_W] = lb_ref[...]
        pk[3:4, HGRN_W:2 * HGRN_W] = wn_ref[...]
        pk[4:5, 0:128] = loss_ref[...]
        out_ref[me] = pk[...]
        sends, recvs = [], []
        for k in range(1, N_DEV):
            peer = (_flip(x, k >> 2), _flip(y, (k >> 1) & 1), _flip(c, k & 1))
            cp = pltpu.make_async_remote_copy(src_ref=pk, dst_ref=out_ref.at[me], send_sem=send_sems.at[k - 1],
                                              recv_sem=recv_sems.at[k - 1], device_id=peer, device_id_type=MESH)
            cp.start()
            sends.append(cp)
            recvs.append(pltpu.make_async_remote_copy(src_ref=pk, dst_ref=out_ref.at[4 * peer[0] + 2 * peer[1] + peer[2]],
                                                      send_sem=send_sems.at[k - 1], recv_sem=recv_sems.at[k - 1], device_id=peer,
                                                      device_id_type=MESH))
        for cp in recvs:
            cp.wait_recv()
        for cp in sends:
            cp.wait_send()

    return pl.pallas_call(
        body, name="gather_small", in_specs=[VMEM_SPEC] * 6, out_specs=VMEM_SPEC,
        out_shape=jax.ShapeDtypeStruct((N_DEV, 8, D_MODEL), f32),
        scratch_shapes=[pltpu.VMEM((8, D_MODEL), f32), pltpu.SemaphoreType.DMA((N_DEV - 1,)), pltpu.SemaphoreType.DMA((N_DEV - 1,))],
    )(g_w1, g_w2, g_w3, g_lb, g_wn, loss)


def _row_tile(r):
    return max(t for t in range(8, 257, 8) if r % t == 0)


def _add_sibling(core, g, got, name):
    _, r, c = got.shape
    tr = _row_tile(r)

    def body(core_ref, a_ref, b_ref, o_ref):
        o_ref[...] = (a_ref[...] + b_ref[...]).astype(bf16)

    blk = pl.BlockSpec((1, tr, c), lambda q, i, core_ref: (q, i, 0))
    return pl.pallas_call(
        body, name=name, out_shape=jax.ShapeDtypeStruct(got.shape, bf16),
        grid_spec=pltpu.PrefetchScalarGridSpec(
            num_scalar_prefetch=1, grid=(4, r // tr),
            in_specs=[pl.BlockSpec((1, tr, c), lambda q, i, core_ref: (2 * q + core_ref[0], i, 0)), blk], out_specs=blk),
        compiler_params=_cp("arbitrary", "arbitrary"))(core, g, got)


def _adamw(w, g, m, v):
    m = ADAM_B1 * m + (1.0 - ADAM_B1) * g
    v = ADAM_B2 * v + (1.0 - ADAM_B2) * (g * g)
    m_hat = m / (1.0 - ADAM_B1 ** ADAM_STEP)
    v_hat = v / (1.0 - ADAM_B2 ** ADAM_STEP)
    return -ADAM_LR * (m_hat / (jnp.sqrt(v_hat) + ADAM_EPS) + ADAM_WD * w), m, v


def _adam_shard(where, g, got, pieces, w, m, v, name):
    r, c = w.shape
    tr = _row_tile(r)

    def body(where_ref, g_ref, got_ref, p_ref, w_ref, m_ref, v_ref, g_out, d_out, m_out, v_out):
        gsum = g_ref[0] + got_ref[0]
        for f in range(3):
            gsum = gsum + p_ref[f].astype(f32)
        g_out[...] = gsum
        d_out[...], m_out[...], v_out[...] = _adamw(w_ref[...], gsum, m_ref[...], v_ref[...])

    blk = pl.BlockSpec((tr, c), lambda i, where_ref: (i, 0))
    return pl.pallas_call(
        body, name=name, out_shape=[jax.ShapeDtypeStruct((r, c), f32)] * 4,
        grid_spec=pltpu.PrefetchScalarGridSpec(
            num_scalar_prefetch=1, grid=(r // tr,),
            in_specs=[pl.BlockSpec((1, tr, c), lambda i, where_ref: (where_ref[0], i, 0)),
                      pl.BlockSpec((1, tr, c), lambda i, where_ref: (where_ref[1], i, 0)),
                      pl.BlockSpec((3, tr, c), lambda i, where_ref: (0, i, 0)), blk, blk, blk],
            out_specs=[blk] * 4),
        compiler_params=_cp("arbitrary"),
    )(where, g, got, pieces, w, m, v)


def _small_update(gath, params):
    def body(gath_ref, *refs):
        ins, outs = refs[:15], refs[15:]
        gs = gath_ref[0]
        for k in range(1, N_DEV):
            gs = gs + gath_ref[k]
        outs[0][...] = gs[4:5, 0:128]
        l0, l1 = ins[9][0:1, :], ins[9][1:2, :]
        lb = _sigmoid(l0 - l1)
        d0 = gs[3:4, 0:HGRN_W] * lb * (1.0 - lb)
        first_row = lax.broadcasted_iota(jnp.int32, (2, HGRN_W), 0) == 0
        grads = [gs[0:1, :], gs[1:2, :], gs[2:3, :], jnp.where(first_row, d0, -d0), gs[3:4, HGRN_W:2 * HGRN_W]]
        for i, g in enumerate(grads):
            w_ref, m_ref, v_ref = ins[3 * i:3 * i + 3]
            o = outs[1 + 4 * i:5 + 4 * i]
            o[0][...] = g
            o[1][...], o[2][...], o[3][...] = _adamw(w_ref[...], g, m_ref[...], v_ref[...])

    flat = [a for p in params for a in p]
    out_shape = [jax.ShapeDtypeStruct((1, 128), f32)] + [jax.ShapeDtypeStruct(p[0].shape, f32) for p in params for _ in range(4)]
    outs = pl.pallas_call(body, name="small_update", in_specs=[VMEM_SPEC] * 16, out_specs=[VMEM_SPEC] * 21, out_shape=out_shape)(gath, *flat)
    return outs[0], [outs[1 + 4 * i:5 + 4 * i] for i in range(5)]


def kernel(x, norm1_w, w_in, lb_logits, hgrn_norm_w, w_out, norm2_w, w_gate_up, w_down, final_norm_w, loss_target, m_norm1_w, m_w_in, m_lb_logits, m_hgrn_norm_w, m_w_out, m_norm2_w, m_w_gate_up, m_w_down, m_final_norm_w, v_norm1_w, v_w_in, v_lb_logits, v_hgrn_norm_w, v_w_out, v_norm2_w, v_w_gate_up, v_w_down, v_final_norm_w):
    row = lambda a: a.reshape(1, D_MODEL)
    ix, iy, ic = lax.axis_index("x"), lax.axis_index("y"), lax.axis_index("c")
    core = jnp.stack([ic]).astype(jnp.int32)
    where = jnp.stack([4 * ix + 2 * iy + ic, 2 * ix + iy]).astype(jnp.int32)
    xs, tgt, w3 = x[0], loss_target[0], row(final_norm_w)
    S = xs.shape[0]

    cos_t, sg_t, win_g = _rope_tables(S, _gather_rider([w_in[0].astype(bf16)]))
    u, qkv, hp, win = _in_proj(xs, norm1_w, win_g, cos_t, sg_t)
    ya, lse, wout_g, wgu_g, wdown_g = _attn_fwd(qkv, _gather_rider([w_out[0].astype(bf16), w_gate_up[0].astype(bf16),
                                                                     w_down[0].astype(bf16)]))
    wout = wout_g.reshape(D_MODEL, D_MODEL)
    wdown = wdown_g.reshape(FFN, D_MODEL)
    yb, o_sav, states = _hgrn_fwd(hp, lb_logits, hgrn_norm_w)
    h1, u2, mixed = _out_proj(xs, ya, yb, wout, norm2_w)
    silu, up_dsilu, act, wgate, wup = _gate_up(u2, wgu_g)
    dh2, loss_p, g_w3 = _down_loss(act, wdown, h1, tgt, w3)

    (g_wdown,) = _gw(act, [dh2], 512, "gw_down")
    dgu2 = _dact(dh2, wdown, silu, up_dsilu)
    early = [_gw_by_owner(u2, dgu2, 2 * FFN // N_DEV, "gw_gate_up", 2048), g_wdown.reshape(N_DEV, FFN // N_DEV, D_MODEL)]
    dh1, g_w2, dmix, *got_early = _dgu(dgu2, wgate, wup, h1, norm2_w, dh2, wout, _sibling_rider(early))
    sums_early = [_add_sibling(core, g, o, f"add_sibling_{i}") for i, (g, o) in enumerate(zip(early, got_early))]
    (g_wout,) = _gw(mixed, [dh1], 1024, "gw_out")
    mid = [g_wout.reshape(N_DEV, D_MODEL // N_DEV, D_MODEL)]
    dhq, dhf, dhi, dhg, g_wn, g_lb, *rode = _hgrn_bwd(hp, lb_logits, hgrn_norm_w, o_sav, states, dmix,
                                                      _both(_chips_rider(sums_early), _sibling_rider(mid)))
    pieces_early, got_mid = rode[:2], rode[2:]
    sums_mid = [_add_sibling(core, mid[0], got_mid[0], "add_sibling_2")]
    dq, dk, dv, *pieces_mid = _attn_bwd(qkv, ya, lse, dmix, _chips_rider(sums_mid))
    dproj, gx, g_w1 = _din(dq, dk, dv, dhq, dhf, dhi, dhg, cos_t, sg_t, win, xs, norm1_w, dh1)
    late = [_gw_by_owner(u, dproj[None], IN_W // N_DEV, "gw_in", 2048)]
    got_late = _alone(_sibling_rider(late), "reduce_sibling")
    pieces_late = [_add_and_send(late[0], got_late[0], "reduce_chips")]

    grads = [late[0], mid[0], early[0], early[1]]
    got = [got_late[0], got_mid[0], got_early[0], got_early[1]]
    pieces = [pieces_late[0], pieces_mid[0], pieces_early[0], pieces_early[1]]
    shards = [w_in[0], w_out[0], w_gate_up[0], w_down[0]]
    moms = [(m_w_in[0], v_w_in[0]), (m_w_out[0], v_w_out[0]), (m_w_gate_up[0], v_w_gate_up[0]), (m_w_down[0], v_w_down[0])]
    big = [_adam_shard(where, g, o, p, w, m, v, f"adam_{i}")
           for i, (g, o, p, w, (m, v)) in enumerate(zip(grads, got, pieces, shards, moms))]
    big = [[a[None] for a in four] for four in big]

    gath = _gather_small(g_w1, g_w2, g_w3, g_lb, g_wn, loss_p)
    params = [(norm1_w, m_norm1_w, v_norm1_w), (norm2_w, m_norm2_w, v_norm2_w),
              (row(final_norm_w), row(m_final_norm_w), row(v_final_norm_w)),
              (lb_logits, m_lb_logits, v_lb_logits), (hgrn_norm_w, m_hgrn_norm_w, v_hgrn_norm_w)]
    loss, (s_w1, s_w2, s_w3, s_lb, s_wn) = _small_update(gath, params)
    s_w3 = [a.reshape(D_MODEL) for a in s_w3]
    per_w = [s_w1, big[0], s_lb, s_wn, big[1], s_w2, big[2], big[3], s_w3]
    return (loss[0, 0], gx[None], *[p[0] for p in per_w], *[p[1] for p in per_w], *[p[2] for p in per_w], *[p[3] for p in per_w])
```

```python
import jax
import jax.numpy as jnp
from jax import lax
from jax.experimental import pallas as pl
from jax.experimental.pallas import tpu as pltpu

f32, bf16 = jnp.float32, jnp.bfloat16

D_MODEL = 1024
ATTN_W = 512
HEAD_DIM = 64
ATTN_BLK = 128
DILATIONS = (1, 4, 16)
HGRN_W = 512
HGRN_HD = 128
CHUNK = 64
IN_W = 3 * ATTN_W + 4 * HGRN_W
FFN = 2816
EPS = 1e-6
ROPE_THETA = 10000.0
NEG = -1e30
N_DEV = 8
ADAM_LR, ADAM_B1, ADAM_B2, ADAM_EPS, ADAM_WD, ADAM_STEP = 0.001, 0.9, 0.999, 1e-08, 0.01, 10
VMEM_LIMIT = 56 * 1024 * 1024


def _cp(*sem):
    return pltpu.CompilerParams(dimension_semantics=sem, vmem_limit_bytes=VMEM_LIMIT)


def _dot(a, b):
    return jnp.dot(a, b, preferred_element_type=f32)


def _dot_nt(a, b):
    return lax.dot_general(a, b, (((1,), (1,)), ((), ())), preferred_element_type=f32)


def _dot_tn(a, b):
    return lax.dot_general(a, b, (((0,), (0,)), ((), ())), preferred_element_type=f32)


def _sigmoid(x):
    return 0.5 * jnp.tanh(0.5 * x) + 0.5


class _Rider:
    def __init__(self, ins, out_shapes, scratch, first, last, middle=None):
        self.ins, self.out_shapes, self.scratch = list(ins), list(out_shapes), list(scratch)
        self.first, self.middle, self.last = first, middle, last


def _ride(call, rider, body, step, n_steps, n_in, n_out, n_scratch):
    if rider is None:
        return call, body, []
    ri, ro = len(rider.ins), len(rider.out_shapes)
    any_spec = pl.BlockSpec(memory_space=pl.ANY)
    call = dict(call, in_specs=call["in_specs"] + [any_spec] * ri, out_specs=call["out_specs"] + [any_spec] * ro,
                out_shape=call["out_shape"] + rider.out_shapes, scratch_shapes=call["scratch_shapes"] + rider.scratch)

    def riding(*refs):
        a = n_in + ri
        b = a + n_out + ro
        mine = refs[:n_in] + refs[a:a + n_out] + refs[b:b + n_scratch]
        theirs = (refs[n_in:a], refs[a + n_out:b], refs[b + n_scratch:])
        t = step()

        @pl.when(t == 0)
        def _():
            rider.first(*theirs)

        body(*mine)
        if rider.middle is not None:
            @pl.when(t == n_steps // 2)
            def _():
                rider.middle(*theirs)

        @pl.when(t == n_steps - 1)
        def _():
            rider.last(*theirs)

    return call, riding, rider.ins


def _rope_tables(S, rider=None):
    half = HEAD_DIM // 2
    tm = 256
    inv_freq = jnp.tile(ROPE_THETA ** (-jnp.arange(half, dtype=f32) / half), 128 // half).reshape(1, 128)
    sign = jnp.tile(jnp.concatenate([-jnp.ones((half,), f32), jnp.ones((half,), f32)]), 128 // HEAD_DIM).reshape(1, 128)

    def body(inv_ref, sign_ref, cos_ref, sg_ref):
        pos = (lax.broadcasted_iota(jnp.int32, (tm, 128), 0) + pl.program_id(0) * tm).astype(f32)
        ang = pos * inv_ref[...]
        cos_ref[...] = jnp.cos(ang)
        sg_ref[...] = jnp.sin(ang) * sign_ref[...]

    vec = pl.BlockSpec((1, 128), lambda i: (0, 0))
    out = pl.BlockSpec((tm, 128), lambda i: (i, 0))
    call = dict(in_specs=[vec, vec], out_specs=[out, out], out_shape=[jax.ShapeDtypeStruct((S, 128), f32)] * 2, scratch_shapes=[])
    call, body, more = _ride(call, rider, body, lambda: pl.program_id(0), S // tm, 2, 2, 0)
    return pl.pallas_call(body, name="rope_tables", grid=(S // tm,), compiler_params=_cp("arbitrary"), **call)(inv_freq, sign, *more)


def _swap_halves(v):
    n = v.shape[1]
    lane = lax.broadcasted_iota(jnp.int32, v.shape, 1)
    return jnp.where((lane % HEAD_DIM) < HEAD_DIM // 2, pltpu.roll(v, n - HEAD_DIM // 2, 1), pltpu.roll(v, HEAD_DIM // 2, 1))


def _in_proj(x, w1, win_g, cos_t, sg_t):
    S = x.shape[0]
    tm = 512
    w = IN_W // N_DEV

    def body(x_ref, w1_ref, wg_ref, cos_ref, sg_ref, u_ref, qkv_ref, hp_ref, w_ref):
        @pl.when(pl.program_id(0) == 0)
        def _():
            for d in range(N_DEV):
                w_ref[:, w * d:w * (d + 1)] = wg_ref[d]

        xv = x_ref[...]
        r = lax.rsqrt(jnp.mean(xv * xv, axis=-1, keepdims=True) + EPS)
        u = (xv * r * w1_ref[...]).astype(bf16)
        u_ref[...] = u
        cosv, sgv = jnp.tile(cos_ref[...], (1, ATTN_W // 128)), jnp.tile(sg_ref[...], (1, ATTN_W // 128))
        for j in range(3):
            pj = _dot(u, w_ref[:, j * ATTN_W:(j + 1) * ATTN_W])
            if j < 2:
                pj = pj * cosv + _swap_halves(pj) * sgv
            if j == 0:
                pj = pj * (HEAD_DIM ** -0.5)
            qkv_ref[:, j * ATTN_W:(j + 1) * ATTN_W] = pj.astype(bf16)
        for j in range(4):
            lo = 3 * ATTN_W + j * HGRN_W
            hp_ref[:, j * HGRN_W:(j + 1) * HGRN_W] = _dot(u, w_ref[:, lo:lo + HGRN_W])

    return pl.pallas_call(
        body, name="in_proj", grid=(S // tm,),
        in_specs=[pl.BlockSpec((tm, D_MODEL), lambda i: (i, 0)), pl.BlockSpec((1, D_MODEL), lambda i: (0, 0)),
                  pl.BlockSpec((N_DEV, D_MODEL, w), lambda i: (0, 0, 0)),
                  pl.BlockSpec((tm, 128), lambda i: (i, 0)), pl.BlockSpec((tm, 128), lambda i: (i, 0))],
        out_specs=[pl.BlockSpec((tm, D_MODEL), lambda i: (i, 0)), pl.BlockSpec((tm, 3 * ATTN_W), lambda i: (i, 0)),
                   pl.BlockSpec((tm, 4 * HGRN_W), lambda i: (i, 0)), pl.BlockSpec((D_MODEL, IN_W), lambda i: (0, 0))],
        out_shape=[jax.ShapeDtypeStruct((S, D_MODEL), bf16), jax.ShapeDtypeStruct((S, 3 * ATTN_W), bf16),
                   jax.ShapeDtypeStruct((S, 4 * HGRN_W), f32), jax.ShapeDtypeStruct((D_MODEL, IN_W), bf16)],
        compiler_params=_cp("arbitrary"),
    )(x, w1, win_g, cos_t, sg_t)


def _head_masks():
    lane = lax.broadcasted_iota(jnp.int32, (ATTN_BLK, 128), 1)
    even = lane < HEAD_DIM
    return even, (even, jnp.logical_not(even))


def _pair_fwd(q2, k2, v2, bias):
    even, masks = _head_masks()
    outs, lses = [], []
    for e in range(2):
        qm = jnp.where(masks[e], q2, 0.0).astype(bf16)
        s = _dot_nt(qm, k2) + bias
        m = jnp.max(s, axis=-1, keepdims=True)
        pe = jnp.exp(s - m)
        lsum = jnp.sum(pe, axis=-1, keepdims=True)
        outs.append(_dot(pe.astype(bf16), v2) / lsum)
        lses.append(jnp.broadcast_to(m + jnp.log(lsum), (ATTN_BLK, 128)))
    return jnp.where(even, outs[0], outs[1]), jnp.where(even, lses[0], lses[1])


def _merge(y0, l0, y1, l1):
    mx = jnp.maximum(l0, l1)
    a, b = jnp.exp(l0 - mx), jnp.exp(l1 - mx)
    tot = a + b
    return (a * y0 + b * y1) / tot, mx + jnp.log(tot)


def _pair_bwd(q2, k2f, v2, dy2, lse2, delta2, bias):
    _, masks = _head_masks()
    k2 = k2f.astype(bf16)
    klane = lax.broadcasted_iota(jnp.int32, (2 * ATTN_BLK, 128), 1) < HEAD_DIM
    kmasks = (klane, jnp.logical_not(klane))
    dq2 = jnp.zeros((ATTN_BLK, 128), f32)
    pes, dss, qms, dyms = [], [], [], []
    for e in range(2):
        c0 = e * HEAD_DIM
        qm = jnp.where(masks[e], q2, 0.0).astype(bf16)
        km = jnp.where(kmasks[e], k2f, 0.0).astype(bf16)
        dym = jnp.where(masks[e], dy2, 0.0).astype(bf16)
        pe = jnp.exp(_dot_nt(qm, k2) + bias - lse2[:, c0:c0 + 1])
        ds = (pe * (_dot_nt(dym, v2) - delta2[:, c0:c0 + 1])).astype(bf16)
        dq2 = dq2 + _dot(ds, km)
        pes.append(pe.astype(bf16))
        dss.append(ds)
        qms.append(qm)
        dyms.append(dym)
    dv2 = _dot_tn(jnp.concatenate(pes, axis=0), jnp.concatenate(dyms, axis=0))
    dk2 = _dot_tn(jnp.concatenate(dss, axis=0), jnp.concatenate(qms, axis=0))
    return dq2, dk2, dv2


TOK = 2048


def _key_bias():
    qi = lax.broadcasted_iota(jnp.int32, (ATTN_BLK, 2 * ATTN_BLK), 0)
    kj = lax.broadcasted_iota(jnp.int32, (ATTN_BLK, 2 * ATTN_BLK), 1)
    delta = ATTN_BLK + qi - kj
    seen = (delta >= 0) & (delta <= ATTN_BLK)
    return jnp.where(seen, 0.0, NEG), jnp.where(seen & (kj >= ATTN_BLK), 0.0, NEG)


def _attn_fwd(qkv, rider=None):
    S = qkv.shape[0]
    nS = S // TOK

    def body(q_ref, kp_ref, kc_ref, vp_ref, vc_ref, y_ref, l_ref, qs, k2, v2, ay, al):
        n = pl.program_id(1)
        qs[...] = q_ref[...].astype(f32)
        k2[0:TOK] = kp_ref[...].astype(f32)
        k2[TOK:2 * TOK] = kc_ref[...].astype(f32)
        v2[0:TOK] = vp_ref[...].astype(f32)
        v2[TOK:2 * TOK] = vc_ref[...].astype(f32)
        bias_any, bias_first = _key_bias()

        def block(dil, r, b, step, last):
            start = r + pl.multiple_of(step * b, step)
            rows = pl.ds(start, ATTN_BLK, stride=dil) if dil > 1 else pl.ds(start, ATTN_BLK)
            keys = (pl.ds(TOK + start - step, 2 * ATTN_BLK, stride=dil) if dil > 1
                    else pl.ds(TOK + start - step, 2 * ATTN_BLK))
            bias = jnp.where((n == 0) & (b == 0), bias_first, bias_any)
            out, lse = _pair_fwd(qs[rows, :], k2[keys, :].astype(bf16), v2[keys, :].astype(bf16), bias)
            if dil < DILATIONS[-1]:
                out, lse = _merge(ay[rows, :], al[rows, :], out, lse)
            if last:
                y_ref[rows, :] = out
                l_ref[rows, :] = lse
            else:
                ay[rows, :] = out
                al[rows, :] = lse

        for dil in reversed(DILATIONS):
            def loop(i, carry, dil=dil):
                block(dil, i % dil, i // dil, ATTN_BLK * dil, dil == 1)
                return carry
            lax.fori_loop(0, TOK // ATTN_BLK, loop, 0, unroll=True)

    blk = (TOK, 128)
    cur = lambda c: pl.BlockSpec(blk, lambda p, n: (n, 4 * c + p))
    prv = lambda c: pl.BlockSpec(blk, lambda p, n: (jnp.maximum(n - 1, 0), 4 * c + p))
    out = pl.BlockSpec(blk, lambda p, n: (n, p))
    call = dict(in_specs=[cur(0), prv(1), cur(1), prv(2), cur(2)], out_specs=[out, out],
                out_shape=[jax.ShapeDtypeStruct((S, ATTN_W), f32)] * 2,
                scratch_shapes=[pltpu.VMEM(blk, f32), pltpu.VMEM((2 * TOK, 128), f32), pltpu.VMEM((2 * TOK, 128), f32),
                                pltpu.VMEM(blk, f32), pltpu.VMEM(blk, f32)])
    call, body, more = _ride(call, rider, body, lambda: pl.program_id(0) * nS + pl.program_id(1), (ATTN_W // 128) * nS, 5, 2, 5)
    return pl.pallas_call(body, name="attention_fwd", grid=(ATTN_W // 128, nS), compiler_params=_cp("arbitrary", "arbitrary"),
                          **call)(qkv, qkv, qkv, qkv, qkv, *more)


def _attn_bwd(qkv, ya, lse, dmix, rider=None):
    S = qkv.shape[0]
    nS = S // TOK

    def body(q_ref, kp_ref, kc_ref, vp_ref, vc_ref, y_ref, l_ref, dy_ref, dq_ref, dk_ref, dv_ref, qs, k2, v2, dk2, dv2, dqa, dl):
        n = pl.program_id(1)

        @pl.when(n == 0)
        def _():
            dk2[...] = jnp.zeros_like(dk2)
            dv2[...] = jnp.zeros_like(dv2)

        @pl.when(n < nS)
        def _():
            qs[...] = q_ref[...].astype(f32)
            k2[0:TOK] = kp_ref[...].astype(f32)
            k2[TOK:2 * TOK] = kc_ref[...].astype(f32)
            v2[0:TOK] = vp_ref[...].astype(f32)
            v2[TOK:2 * TOK] = vc_ref[...].astype(f32)
            li = lax.broadcasted_iota(jnp.int32, (128, 128), 0)
            lj = lax.broadcasted_iota(jnp.int32, (128, 128), 1)
            seg = jnp.where((li // HEAD_DIM) == (lj // HEAD_DIM), 1.0, 0.0).astype(bf16)
            bias_any, bias_first = _key_bias()

            def delta_rows(t, carry):
                rows = pl.ds(pl.multiple_of(256 * t, 256), 256)
                dyy = dy_ref[rows, :] * y_ref[rows, :]
                hi = dyy.astype(bf16)
                dl[rows, :] = _dot(hi, seg) + _dot((dyy - hi.astype(f32)).astype(bf16), seg)
                return carry

            lax.fori_loop(0, TOK // 256, delta_rows, 0)

            def block(dil, r, b, step, first_pattern, last):
                start = r + pl.multiple_of(step * b, step)
                rows = pl.ds(start, ATTN_BLK, stride=dil) if dil > 1 else pl.ds(start, ATTN_BLK)
                keys = (pl.ds(TOK + start - step, 2 * ATTN_BLK, stride=dil) if dil > 1
                        else pl.ds(TOK + start - step, 2 * ATTN_BLK))
                bias = jnp.where((n == 0) & (b == 0), bias_first, bias_any)
                dq2, dkk, dvv = _pair_bwd(qs[rows, :], k2[keys, :], v2[keys, :].astype(bf16), dy_ref[rows, :],
                                          l_ref[rows, :], dl[rows, :], bias)
                if last:
                    dq_ref[rows, :] = dqa[rows, :] + dq2
                elif first_pattern:
                    dqa[rows, :] = dq2
                else:
                    dqa[rows, :] += dq2
                dk2[keys, :] += dkk
                dv2[keys, :] += dvv

            for dil in reversed(DILATIONS):
                def loop(i, carry, dil=dil):
                    block(dil, i % dil, i // dil, ATTN_BLK * dil, dil == DILATIONS[-1], dil == 1)
                    return carry
                lax.fori_loop(0, TOK // ATTN_BLK, loop, 0, unroll=True)

        dk_ref[...] = dk2[0:TOK]
        dv_ref[...] = dv2[0:TOK]
        dk2[0:TOK] = dk2[TOK:2 * TOK]
        dv2[0:TOK] = dv2[TOK:2 * TOK]
        dk2[TOK:2 * TOK] = jnp.zeros((TOK, 128), f32)
        dv2[TOK:2 * TOK] = jnp.zeros((TOK, 128), f32)

    blk = (TOK, 128)
    cn = lambda n: jnp.minimum(n, nS - 1)
    pn = lambda n: jnp.clip(n - 1, 0, nS - 1)
    cur = lambda c: pl.BlockSpec(blk, lambda p, n: (cn(n), 4 * c + p))
    prv = lambda c: pl.BlockSpec(blk, lambda p, n: (pn(n), 4 * c + p))
    at_n = pl.BlockSpec(blk, lambda p, n: (cn(n), p))
    at_p = pl.BlockSpec(blk, lambda p, n: (pn(n), p))
    big = lambda: pltpu.VMEM((2 * TOK, 128), f32)
    call = dict(in_specs=[cur(0), prv(1), cur(1), prv(2), cur(2), at_n, at_n, at_n], out_specs=[at_n, at_p, at_p],
                out_shape=[jax.ShapeDtypeStruct((S, ATTN_W), f32)] * 3,
                scratch_shapes=[pltpu.VMEM(blk, f32), big(), big(), big(), big(), pltpu.VMEM(blk, f32), pltpu.VMEM(blk, f32)])
    call, body, more = _ride(call, rider, body, lambda: pl.program_id(0) * (nS + 1) + pl.program_id(1),
                             (ATTN_W // 128) * (nS + 1), 8, 3, 7)
    return pl.pallas_call(body, name="attention_bwd", grid=(ATTN_W // 128, nS + 1), compiler_params=_cp("arbitrary", "arbitrary"),
                          **call)(qkv, qkv, qkv, qkv, qkv, ya, lse, dmix, *more)


HG_T = 512
N_HH = HGRN_W // HGRN_HD
HG_SUB = 128
SAFE_RANGE = 75.0


def _row_in_chunk():
    return lax.broadcasted_iota(jnp.int32, (HG_T, HGRN_HD), 0) % CHUNK


def _chunk_cumsum(v, rc):
    k = 1
    while k < CHUNK:
        v = v + jnp.where(rc >= k, pltpu.roll(v, k, 0), 0.0)
        k *= 2
    return v


def _chunk_rcumsum(v, rc):
    k = 1
    while k < CHUNK:
        v = v + jnp.where(rc < CHUNK - k, pltpu.roll(v, HG_T - k, 0), 0.0)
        k *= 2
    return v


def _hgrn_gates(qb, fb, lb):
    sf = _sigmoid(fb)
    f = lb + (1.0 - lb) * sf
    sq = _sigmoid(qb)
    return sf, f, jnp.log(f), 1.0 - f, sq, qb * sq


def _hgrn_prep(qb, fb, lbl2, rc):
    lb = _sigmoid(lbl2[0:1, :] - lbl2[1:2, :])
    sf, f, lf, key, sq, qf = _hgrn_gates(qb, fb, lb)
    b = _chunk_cumsum(lf, rc)
    rem = _chunk_rcumsum(lf, rc) - lf
    return dict(lb=lb, sf=sf, f=f, key=key, sq=sq, qf=qf, b=b, rem=rem, eb=jnp.exp(b), er=jnp.exp(rem))


def _chunk_mask():
    r = lax.broadcasted_iota(jnp.int32, (HG_SUB, HG_SUB), 0)
    c = lax.broadcasted_iota(jnp.int32, (HG_SUB, HG_SUB), 1)
    return ((r // CHUNK) == (c // CHUNK)) & (c <= r)


def _hgrn_fwd(hp, lbl, wn):
    S = hp.shape[0]
    nT = S // HG_T

    def body(qb_ref, fb_ref, ib_ref, gb_ref, lbl_ref, wn_ref, yb_ref, o_ref, st_ref, ST, qt_s, kh_s, dec_s, oi_s):
        @pl.when(pl.program_id(0) == 0)
        def _():
            ST[...] = jnp.zeros_like(ST)

        rc = _row_in_chunk()
        for h in range(N_HH):
            sl = slice(HGRN_HD * h, HGRN_HD * (h + 1))
            p = _hgrn_prep(qb_ref[:, sl], fb_ref[:, sl], lbl_ref[:, sl], rc)
            qf, key, b = p["qf"], p["key"], p["b"]
            qt = qf * p["eb"]
            qt_s[:, sl] = qt.astype(bf16)
            kh_s[:, sl] = (key * p["er"]).astype(bf16)
            dec_s[:, sl] = jnp.exp(b + p["rem"])
            rng = jnp.max(-(b + p["rem"]))

            @pl.when(rng < SAFE_RANGE)
            def _():
                kp = (key * jnp.exp(-b)).astype(bf16)
                cmask = _chunk_mask()
                for j in range(HG_T // HG_SUB):
                    rs = slice(HG_SUB * j, HG_SUB * (j + 1))
                    sc = jnp.where(cmask, _dot_nt(qt[rs].astype(bf16), kp[rs]), 0.0).astype(bf16)
                    oi_s[rs, sl] = _dot(sc, ib_ref[rs, sl].astype(bf16))

            @pl.when(rng >= SAFE_RANGE)
            def _():
                v = ib_ref[:, sl]
                ones = jnp.ones((HGRN_HD, HGRN_HD), bf16)

                def lag(l, o):
                    e = jnp.exp(jnp.where(rc >= l, b - pltpu.roll(b, l, 0), NEG))
                    pr = qf * pltpu.roll(key, l, 0) * e
                    return o + _dot(pr.astype(bf16), ones) * pltpu.roll(v, l, 0)

                oi_s[:, sl] = lax.fori_loop(1, CHUNK, lag, _dot((qf * key).astype(bf16), ones) * v)

        def step(c, carry):
            rows = pl.ds(pl.multiple_of(c * CHUNK, CHUNK), CHUNK)
            row0 = pl.ds(pl.multiple_of(c * CHUNK, CHUNK), 1)
            for h in range(N_HH):
                sl = slice(HGRN_HD * h, HGRN_HD * (h + 1))
                stv = ST[h]
                st_ref[c, sl, :] = stv
                oi_s[rows, sl] += _dot_nt(qt_s[rows, sl], stv.astype(bf16))
                ST[h] = stv * dec_s[row0, sl] + _dot_tn(ib_ref[rows, sl].astype(bf16), kh_s[rows, sl])
            return carry

        lax.fori_loop(0, HG_T // CHUNK, step, 0, unroll=True)

        for h in range(N_HH):
            sl = slice(HGRN_HD * h, HGRN_HD * (h + 1))
            o = oi_s[:, sl]
            o_ref[:, sl] = o
            on = o * lax.rsqrt(jnp.mean(o * o, axis=-1, keepdims=True) + EPS)
            g = gb_ref[:, sl]
            yb_ref[:, sl] = on * wn_ref[:, sl] * (g * _sigmoid(g))

    col = lambda c: pl.BlockSpec((HG_T, HGRN_W), lambda i: (i, c))
    tile = pl.BlockSpec((HG_T, HGRN_W), lambda i: (i, 0))
    whole = lambda a: pl.BlockSpec(a.shape, lambda i: (0, 0))
    return pl.pallas_call(
        body, name="hgrn_fwd", grid=(nT,),
        in_specs=[col(0), col(1), col(2), col(3), whole(lbl), whole(wn)],
        out_specs=[tile, tile, pl.BlockSpec((HG_T // CHUNK, HGRN_W, HGRN_HD), lambda i: (i, 0, 0))],
        out_shape=[jax.ShapeDtypeStruct((S, HGRN_W), f32), jax.ShapeDtypeStruct((S, HGRN_W), f32),
                   jax.ShapeDtypeStruct((S // CHUNK, HGRN_W, HGRN_HD), f32)],
        scratch_shapes=[pltpu.VMEM((N_HH, HGRN_HD, HGRN_HD), f32), pltpu.VMEM((HG_T, HGRN_W), bf16),
                        pltpu.VMEM((HG_T, HGRN_W), bf16), pltpu.VMEM((HG_T, HGRN_W), f32), pltpu.VMEM((HG_T, HGRN_W), f32)],
        compiler_params=_cp("arbitrary"),
    )(hp, hp, hp, hp, lbl, wn)


def _hgrn_bwd(hp, lbl, wn, o_sav, states, dmix, rider=None):
    S = hp.shape[0]
    nT = S // HG_T

    def body(qb_ref, fb_ref, ib_ref, gb_ref, lbl_ref, wn_ref, o_ref, st_ref, dy_ref,
             dq_ref, df_ref, di_ref, dg_ref, gwn_ref, glb_ref,
             DST, qt_s, kh_s, dec_s, do_s, dqt_s, dkh_s, dbl_s, dvi_s, dqi_s, dki_s, dbi_s):
        @pl.when(pl.program_id(0) == 0)
        def _():
            DST[...] = jnp.zeros_like(DST)
            gwn_ref[...] = jnp.zeros_like(gwn_ref)
            glb_ref[...] = jnp.zeros_like(glb_ref)

        rc = _row_in_chunk()
        preps = []
        for h in range(N_HH):
            sl = slice(HGRN_HD * h, HGRN_HD * (h + 1))
            p = _hgrn_prep(qb_ref[:, sl], fb_ref[:, sl], lbl_ref[:, sl], rc)
            preps.append(p)
            qf, key, b = p["qf"], p["key"], p["b"]
            v = ib_ref[:, sl]
            o = o_ref[:, sl]
            rinv = lax.rsqrt(jnp.mean(o * o, axis=-1, keepdims=True) + EPS)
            on = o * rinv
            g = gb_ref[:, sl]
            sgm = _sigmoid(g)
            silu_g = g * sgm
            dy = dy_ref[:, sl]
            wn_v = wn_ref[:, sl]
            gwn_ref[:, sl] += jnp.sum(dy * on * silu_g, axis=0, keepdims=True)
            dg_ref[:, sl] = (dy * on * wn_v * (sgm * (1.0 + g * (1.0 - sgm)))).astype(bf16)
            t1 = dy * wn_v * silu_g
            do = rinv * (t1 - on * jnp.mean(t1 * on, axis=-1, keepdims=True))
            do_s[:, sl] = do.astype(bf16)
            qt = qf * p["eb"]
            qt_s[:, sl] = qt.astype(bf16)
            kh_s[:, sl] = (key * p["er"]).astype(bf16)
            dec_s[:, sl] = jnp.exp(b + p["rem"])
            rng = jnp.max(-(b + p["rem"]))

            @pl.when(rng < SAFE_RANGE)
            def _():
                einv = jnp.exp(-b)
                kp = (key * einv).astype(bf16)
                cmask = _chunk_mask()
                for j in range(HG_T // HG_SUB):
                    rs = slice(HG_SUB * j, HG_SUB * (j + 1))
                    qtb, dob, vb = qt[rs].astype(bf16), do[rs].astype(bf16), v[rs].astype(bf16)
                    sc = jnp.where(cmask, _dot_nt(qtb, kp[rs]), 0.0).astype(bf16)
                    dsc = jnp.where(cmask, _dot_nt(dob, vb), 0.0).astype(bf16)
                    dqp = _dot(dsc, kp[rs])
                    dkp = _dot_tn(dsc, qtb)
                    dvi_s[rs, sl] = _dot_tn(sc, dob)
                    dqi_s[rs, sl] = dqp * p["eb"][rs]
                    dki_s[rs, sl] = dkp * einv[rs]
                    dbi_s[rs, sl] = dqp * qtb.astype(f32) - dkp * kp[rs].astype(f32)

            @pl.when(rng >= SAFE_RANGE)
            def _():
                ones = jnp.ones((HGRN_HD, HGRN_HD), bf16)

                def lag(l, carry):
                    dqf, dkey, db, dv = carry
                    e = jnp.exp(jnp.where(rc >= l, b - pltpu.roll(b, l, 0), NEG))
                    ks, vs, qe = pltpu.roll(key, l, 0), pltpu.roll(v, l, 0), qf * e
                    pr = qe * ks
                    rl = _dot(pr.astype(bf16), ones)
                    drl = jnp.where(rc >= l, _dot((do * vs).astype(bf16), ones), 0.0)
                    gl = drl * pr
                    back = HG_T - l
                    return (dqf + drl * ks * e, dkey + pltpu.roll(drl * qe, back, 0), db + gl - pltpu.roll(gl, back, 0),
                            dv + pltpu.roll(rl * do, back, 0))

                rl0 = _dot((qf * key).astype(bf16), ones)
                drl0 = _dot((do * v).astype(bf16), ones)
                dqf, dkey, db, dv = lax.fori_loop(1, CHUNK, lag, (drl0 * key, drl0 * qf, jnp.zeros((HG_T, HGRN_HD), f32), rl0 * do))
                dvi_s[:, sl] = dv
                dqi_s[:, sl] = dqf
                dki_s[:, sl] = dkey
                dbi_s[:, sl] = db

        def step(k, carry):
            c = HG_T // CHUNK - 1 - k
            rows = pl.ds(pl.multiple_of(c * CHUNK, CHUNK), CHUNK)
            row0 = pl.ds(pl.multiple_of(c * CHUNK, CHUNK), 1)
            for h in range(N_HH):
                sl = slice(HGRN_HD * h, HGRN_HD * (h + 1))
                stp = st_ref[c, sl, :]
                dst = DST[h]
                dstb = dst.astype(bf16)
                dob = do_s[rows, sl]
                khb = kh_s[rows, sl]
                dec = dec_s[row0, sl]
                dqt_s[rows, sl] = _dot(dob, stp.astype(bf16))
                dkh = _dot(ib_ref[rows, sl].astype(bf16), dstb)
                dkh_s[rows, sl] = dkh
                dvi_s[rows, sl] += _dot_nt(khb, dstb)
                dbl = jnp.sum(dst * stp, axis=0, keepdims=True) * dec + jnp.sum(dkh * khb.astype(f32), axis=0, keepdims=True)
                dbl_s[rows, sl] = jnp.broadcast_to(dbl, (CHUNK, HGRN_HD))
                DST[h] = dst * dec + _dot_tn(dob, qt_s[rows, sl])
            return carry

        lax.fori_loop(0, HG_T // CHUNK, step, 0, unroll=True)

        for h in range(N_HH):
            sl = slice(HGRN_HD * h, HGRN_HD * (h + 1))
            qb = qb_ref[:, sl]
            p = preps[h]
            sf, sq, lb = p["sf"], p["sq"], p["lb"]
            dqt, dkh = dqt_s[:, sl], dkh_s[:, sl]
            dqf = dqt * p["eb"] + dqi_s[:, sl]
            dkey = dkh * p["er"] + dki_s[:, sl]
            db = dqt * (p["qf"] * p["eb"]) - dkh * (p["key"] * p["er"]) + jnp.where(rc == CHUNK - 1, dbl_s[:, sl], 0.0) + dbi_s[:, sl]
            df = _chunk_rcumsum(db, rc) / p["f"] - dkey
            df_ref[:, sl] = (df * (1.0 - lb) * sf * (1.0 - sf)).astype(bf16)
            glb_ref[:, sl] += jnp.sum(df * (1.0 - sf), axis=0, keepdims=True)
            dq_ref[:, sl] = (dqf * (sq * (1.0 + qb * (1.0 - sq)))).astype(bf16)
            di_ref[:, sl] = dvi_s[:, sl].astype(bf16)

    rev = lambda i: nT - 1 - i
    col = lambda c: pl.BlockSpec((HG_T, HGRN_W), lambda i: (rev(i), c))
    tile = pl.BlockSpec((HG_T, HGRN_W), lambda i: (rev(i), 0))
    whole = lambda a: pl.BlockSpec(a.shape, lambda i: (0, 0))
    vec = pl.BlockSpec((1, HGRN_W), lambda i: (0, 0))
    tb = lambda: pltpu.VMEM((HG_T, HGRN_W), bf16)
    tf = lambda: pltpu.VMEM((HG_T, HGRN_W), f32)
    call = dict(in_specs=[col(0), col(1), col(2), col(3), whole(lbl), whole(wn), tile,
                          pl.BlockSpec((HG_T // CHUNK, HGRN_W, HGRN_HD), lambda i: (rev(i), 0, 0)),
                          pl.BlockSpec((HG_T, HGRN_W), lambda i: (rev(i), 1))],
                out_specs=[tile, tile, tile, tile, vec, vec],
                out_shape=[jax.ShapeDtypeStruct((S, HGRN_W), bf16)] * 4 + [jax.ShapeDtypeStruct((1, HGRN_W), f32)] * 2,
                scratch_shapes=[pltpu.VMEM((N_HH, HGRN_HD, HGRN_HD), f32), tb(), tb(), tf(), tb(), tf(), tf(), tf(), tf(), tf(),
                                tf(), tf()])
    call, body, more = _ride(call, rider, body, lambda: pl.program_id(0), nT, 9, 6, 12)
    return pl.pallas_call(body, name="hgrn_bwd", grid=(nT,), compiler_params=_cp("arbitrary"), **call)(
        hp, hp, hp, hp, lbl, wn, o_sav, states, dmix, *more)


def _out_proj(x, ya, yb, wout, w2):
    S = x.shape[0]
    tm = 512

    def body(x_ref, ya_ref, yb_ref, w_ref, w2_ref, h1_ref, u2_ref, mix_ref):
        mixed = jnp.concatenate([ya_ref[...], yb_ref[...]], axis=1).astype(bf16)
        mix_ref[...] = mixed
        h1 = x_ref[...] + _dot(mixed, w_ref[...])
        h1_ref[...] = h1
        r = lax.rsqrt(jnp.mean(h1 * h1, axis=-1, keepdims=True) + EPS)
        u2_ref[...] = (h1 * r * w2_ref[...]).astype(bf16)

    row = lambda w: pl.BlockSpec((tm, w), lambda i: (i, 0))
    return pl.pallas_call(
        body, name="out_proj", grid=(S // tm,),
        in_specs=[row(D_MODEL), row(ATTN_W), row(HGRN_W), pl.BlockSpec((D_MODEL, D_MODEL), lambda i: (0, 0)),
                  pl.BlockSpec((1, D_MODEL), lambda i: (0, 0))],
        out_specs=[row(D_MODEL), row(D_MODEL), row(D_MODEL)],
        out_shape=[jax.ShapeDtypeStruct((S, D_MODEL), f32), jax.ShapeDtypeStruct((S, D_MODEL), bf16),
                   jax.ShapeDtypeStruct((S, D_MODEL), bf16)],
        compiler_params=_cp("arbitrary"),
    )(x, ya, yb, wout, w2)


def _gate_up(u2, wgu_g):
    S = u2.shape[0]
    w = 2 * FFN // N_DEV
    tm, tn = 512, 2 * w
    nj = FFN // tn

    def body(u_ref, wgg_ref, wug_ref, g_ref, up_ref, a_ref, wg_ref, wu_ref):
        @pl.when(pl.program_id(1) == 0)
        def _():
            for k in range(2):
                wg_ref[:, w * k:w * (k + 1)] = wgg_ref[k]
                wu_ref[:, w * k:w * (k + 1)] = wug_ref[k]

        u = u_ref[...]
        g = _dot(u, wg_ref[...])
        up = _dot(u, wu_ref[...])
        sg = _sigmoid(g)
        silu = g * sg
        g_ref[...] = silu.astype(bf16)
        up_ref[...] = (up * (sg + silu * (1.0 - sg))).astype(bf16)
        a_ref[...] = (silu * up).astype(bf16)

    out = pl.BlockSpec((tm, tn), lambda j, i: (i, j))
    wout = pl.BlockSpec((D_MODEL, tn), lambda j, i: (0, j))
    return pl.pallas_call(
        body, name="gate_up", grid=(nj, S // tm),
        in_specs=[pl.BlockSpec((tm, D_MODEL), lambda j, i: (i, 0)), pl.BlockSpec((2, D_MODEL, w), lambda j, i: (j, 0, 0)),
                  pl.BlockSpec((2, D_MODEL, w), lambda j, i: (j + nj, 0, 0))],
        out_specs=[out, out, out, wout, wout],
        out_shape=[jax.ShapeDtypeStruct((S, FFN), bf16)] * 3 + [jax.ShapeDtypeStruct((D_MODEL, FFN), bf16)] * 2,
        compiler_params=_cp("arbitrary", "arbitrary"),
    )(u2, wgu_g, wgu_g)


def _rms_bwd(dyw, hn, r):
    return r * (dyw - hn * jnp.mean(dyw * hn, axis=-1, keepdims=True))


def _down_loss(act, wdown, h1, tgt, w3):
    S = act.shape[0]
    tm = 512

    def body(a_ref, w_ref, h1_ref, t_ref, w3_ref, dh2_ref, loss_ref, gw3_ref):
        @pl.when(pl.program_id(0) == 0)
        def _():
            loss_ref[...] = jnp.zeros_like(loss_ref)
            gw3_ref[...] = jnp.zeros_like(gw3_ref)

        h2 = h1_ref[...] + _dot(a_ref[...], w_ref[...])
        r = lax.rsqrt(jnp.mean(h2 * h2, axis=-1, keepdims=True) + EPS)
        hn = h2 * r
        w3 = w3_ref[...]
        err = hn * w3 - t_ref[...]
        loss_ref[...] += (0.5 / D_MODEL) * jnp.sum(err * err)
        dy = err * (1.0 / D_MODEL)
        gw3_ref[...] += jnp.sum(dy * hn, axis=0, keepdims=True)
        dh2_ref[...] = _rms_bwd(dy * w3, hn, r)

    row = lambda w: pl.BlockSpec((tm, w), lambda i: (i, 0))
    return pl.pallas_call(
        body, name="down_loss", grid=(S // tm,),
        in_specs=[row(FFN), pl.BlockSpec((FFN, D_MODEL), lambda i: (0, 0)), row(D_MODEL), row(D_MODEL),
                  pl.BlockSpec((1, D_MODEL), lambda i: (0, 0))],
        out_specs=[row(D_MODEL), pl.BlockSpec((1, 128), lambda i: (0, 0)), pl.BlockSpec((1, D_MODEL), lambda i: (0, 0))],
        out_shape=[jax.ShapeDtypeStruct((S, D_MODEL), f32), jax.ShapeDtypeStruct((1, 128), f32),
                   jax.ShapeDtypeStruct((1, D_MODEL), f32)],
        compiler_params=_cp("arbitrary"),
    )(act, wdown, h1, tgt, w3)


def _dact(dh2, wdown, silu, up_dsilu):
    S = dh2.shape[0]
    tm = 256

    def body(d_ref, w_ref, s_ref, u_ref, o_ref):
        da = _dot_nt(d_ref[...].astype(bf16), w_ref[...])
        o_ref[1] = (da * s_ref[...].astype(f32)).astype(bf16)
        o_ref[0] = (da * u_ref[...].astype(f32)).astype(bf16)

    row = lambda w: pl.BlockSpec((tm, w), lambda i: (i, 0))
    return pl.pallas_call(
        body, name="dact", grid=(S // tm,),
        in_specs=[row(D_MODEL), pl.BlockSpec((FFN, D_MODEL), lambda i: (0, 0)), row(FFN), row(FFN)],
        out_specs=pl.BlockSpec((2, tm, FFN), lambda i: (0, i, 0)),
        out_shape=jax.ShapeDtypeStruct((2, S, FFN), bf16),
        compiler_params=_cp("arbitrary"),
    )(dh2, wdown, silu, up_dsilu)


def _dgu(dgu2, wgate, wup, h1, w2, dh2, wout, rider=None):
    S = dgu2.shape[1]
    tm = 256

    def body(d_ref, wg_ref, wu_ref, h1_ref, w2_ref, dh2_ref, wo_ref, dh1_ref, gw2_ref, dmix_ref):
        @pl.when(pl.program_id(0) == 0)
        def _():
            gw2_ref[...] = jnp.zeros_like(gw2_ref)

        du2 = _dot_nt(d_ref[0], wg_ref[...]) + _dot_nt(d_ref[1], wu_ref[...])
        h1 = h1_ref[...]
        r = lax.rsqrt(jnp.mean(h1 * h1, axis=-1, keepdims=True) + EPS)
        hn = h1 * r
        gw2_ref[...] += jnp.sum(du2 * hn, axis=0, keepdims=True)
        dh1 = dh2_ref[...] + _rms_bwd(du2 * w2_ref[...], hn, r)
        dh1_ref[...] = dh1
        dmix_ref[...] = _dot_nt(dh1.astype(bf16), wo_ref[...])

    row = lambda w: pl.BlockSpec((tm, w), lambda i: (i, 0))
    call = dict(in_specs=[pl.BlockSpec((2, tm, FFN), lambda i: (0, i, 0)), pl.BlockSpec((D_MODEL, FFN), lambda i: (0, 0)),
                          pl.BlockSpec((D_MODEL, FFN), lambda i: (0, 0)), row(D_MODEL),
                          pl.BlockSpec((1, D_MODEL), lambda i: (0, 0)), row(D_MODEL),
                          pl.BlockSpec((D_MODEL, D_MODEL), lambda i: (0, 0))],
                out_specs=[row(D_MODEL), pl.BlockSpec((1, D_MODEL), lambda i: (0, 0)), row(D_MODEL)],
                out_shape=[jax.ShapeDtypeStruct((S, D_MODEL), f32), jax.ShapeDtypeStruct((1, D_MODEL), f32),
                           jax.ShapeDtypeStruct((S, D_MODEL), f32)], scratch_shapes=[])
    call, body, more = _ride(call, rider, body, lambda: pl.program_id(0), S // tm, 7, 3, 0)
    return pl.pallas_call(body, name="dgu", grid=(S // tm,), compiler_params=_cp("arbitrary"), **call)(
        dgu2, wgate, wup, h1, w2, dh2, wout, *more)


def _din(dq, dk, dv, dhq, dhf, dhi, dhg, cos_t, sg_t, win, x, w1, dh1):
    S = x.shape[0]
    tm = 512

    def body(dq_ref, dk_ref, dv_ref, dhq_ref, dhf_ref, dhi_ref, dhg_ref, cos_ref, sg_ref, w_ref, x_ref, w1_ref, dh1_ref,
             dp_ref, gx_ref, gw1_ref):
        @pl.when(pl.program_id(0) == 0)
        def _():
            gw1_ref[...] = jnp.zeros_like(gw1_ref)

        cosv, sgv = jnp.tile(cos_ref[...], (1, ATTN_W // 128)), jnp.tile(sg_ref[...], (1, ATTN_W // 128))
        unrope = lambda d: d * cosv - sgv * _swap_halves(d)
        parts = [(unrope(dq_ref[...]) * (HEAD_DIM ** -0.5)).astype(bf16), unrope(dk_ref[...]).astype(bf16),
                 dv_ref[...].astype(bf16), dhq_ref[...], dhf_ref[...], dhi_ref[...], dhg_ref[...]]
        du = jnp.zeros((tm, D_MODEL), f32)
        for j, pj in enumerate(parts):
            dp_ref[:, j * 512:(j + 1) * 512] = pj
            du = du + _dot_nt(pj, w_ref[:, j * 512:(j + 1) * 512])
        xv = x_ref[...]
        r = lax.rsqrt(jnp.mean(xv * xv, axis=-1, keepdims=True) + EPS)
        xn = xv * r
        gw1_ref[...] += jnp.sum(du * xn, axis=0, keepdims=True)
        gx_ref[...] = dh1_ref[...] + _rms_bwd(du * w1_ref[...], xn, r)

    row = lambda w: pl.BlockSpec((tm, w), lambda i: (i, 0))
    vec = pl.BlockSpec((1, D_MODEL), lambda i: (0, 0))
    return pl.pallas_call(
        body, name="din", grid=(S // tm,),
        in_specs=[row(512)] * 7 + [row(128), row(128), pl.BlockSpec((D_MODEL, IN_W), lambda i: (0, 0)), row(D_MODEL), vec,
                                   row(D_MODEL)],
        out_specs=[row(IN_W), row(D_MODEL), vec],
        out_shape=[jax.ShapeDtypeStruct((S, IN_W), bf16), jax.ShapeDtypeStruct((S, D_MODEL), f32),
                   jax.ShapeDtypeStruct((1, D_MODEL), f32)],
        compiler_params=_cp("arbitrary"),
    )(dq, dk, dv, dhq, dhf, dhi, dhg, cos_t, sg_t, win, x, w1, dh1)


def _gw(a, bs, tn, name, ts=2048):
    S, M = a.shape
    N = bs[0].shape[1]
    k = len(bs)

    def body(a_ref, *refs):
        @pl.when(pl.program_id(1) == 0)
        def _():
            for o_ref in refs[k:]:
                o_ref[...] = jnp.zeros_like(o_ref)

        at = a_ref[...].astype(bf16)
        for b_ref, o_ref in zip(refs[:k], refs[k:]):
            o_ref[...] += _dot_tn(at, b_ref[...].astype(bf16))

    return pl.pallas_call(
        body, name=name, grid=(N // tn, S // ts),
        in_specs=[pl.BlockSpec((ts, M), lambda j, s: (s, 0))] + [pl.BlockSpec((ts, tn), lambda j, s: (s, j))] * k,
        out_specs=[pl.BlockSpec((M, tn), lambda j, s: (0, j))] * k, out_shape=[jax.ShapeDtypeStruct((M, N), f32)] * k,
        compiler_params=_cp("arbitrary", "arbitrary"),
    )(a, *bs)


def _gw_by_owner(a, b3, w, name, ts):
    S, M = a.shape
    G, _, Ng = b3.shape
    tn = 2 * w
    per_group = Ng // tn
    n_s = S // ts

    def body(a_ref, b_ref, o_ref, acc):
        s = pl.program_id(1)

        @pl.when(s == 0)
        def _():
            acc[...] = jnp.zeros_like(acc)

        acc[...] += _dot_tn(a_ref[...].astype(bf16), b_ref[0].astype(bf16))

        @pl.when(s == n_s - 1)
        def _():
            o_ref[0] = acc[:, 0:w]
            o_ref[1] = acc[:, w:tn]

    return pl.pallas_call(
        body, name=name, grid=(G * per_group, n_s),
        in_specs=[pl.BlockSpec((ts, M), lambda j, s: (s, 0)),
                  pl.BlockSpec((1, ts, tn), lambda j, s: (j // per_group, s, j % per_group))],
        out_specs=pl.BlockSpec((2, M, w), lambda j, s: (j, 0, 0)), out_shape=jax.ShapeDtypeStruct((G * Ng // w, M, w), f32),
        scratch_shapes=[pltpu.VMEM((M, tn), f32)], compiler_params=_cp("arbitrary", "arbitrary"),
    )(a, b3)


MESH = pl.DeviceIdType.MESH
ANY = pl.BlockSpec(memory_space=pl.ANY)
VMEM_SPEC = pl.BlockSpec(memory_space=pltpu.VMEM)


def _pos():
    return lax.axis_index("x"), lax.axis_index("y"), lax.axis_index("c")


def _flip(v, bit):
    return 1 - v if bit else v


def _gather_rider(shards):
    n = len(shards)

    def parts(outs, scratch):
        send_sems, recv_sems, local_sems = scratch[n:]
        x, y, c = _pos()
        chips = [(1 - x, y), (x, 1 - y), (1 - x, 1 - y)]

        def copy(a, k, block, to, src=None):
            dst = outs[a].at[4 * block[0] + 2 * block[1] + block[2]]
            return pltpu.make_async_remote_copy(src_ref=dst if src is None else src, dst_ref=dst, send_sem=send_sems.at[a, k],
                                                recv_sem=recv_sems.at[a, k], device_id=to, device_id_type=MESH)

        bufs = scratch[:n]
        me, sibling = (x, y, c), (x, y, 1 - c)
        own = lambda a: pltpu.make_async_copy(bufs[a], outs[a].at[4 * x + 2 * y + c], local_sems.at[a])
        sent = lambda a: [copy(a, 0, me, sibling, src=bufs[a])] + [copy(a, 1 + j, me, (*chip, c), src=bufs[a])
                                                                   for j, chip in enumerate(chips)]
        passed = lambda a: [copy(a, 4 + j, (*chip, c), sibling) for j, chip in enumerate(chips)]
        landed = lambda a: [copy(a, 1 + j, (*chip, c), me) for j, chip in enumerate(chips)]
        from_sibling = lambda a: [copy(a, 0, sibling, me)] + [copy(a, 4 + j, (*chip, 1 - c), me) for j, chip in enumerate(chips)]
        return bufs, local_sems, own, sent, passed, landed, from_sibling

    def first(ins, outs, scratch):
        bufs, local_sems, own, sent, _, _, _ = parts(outs, scratch)
        loads = [pltpu.make_async_copy(ins[a], bufs[a], local_sems.at[a]) for a in range(n)]
        for ld in loads:
            ld.start()
        for a in range(n):
            loads[a].wait()
            own(a).start()
            for cp in sent(a):
                cp.start()

    def middle(ins, outs, scratch):
        _, _, _, _, passed, landed, _ = parts(outs, scratch)
        for a in range(n):
            for got, on in zip(landed(a), passed(a)):
                got.wait_recv()
                on.start()

    def last(ins, outs, scratch):
        _, _, own, sent, passed, _, from_sibling = parts(outs, scratch)
        for a in range(n):
            for cp in from_sibling(a):
                cp.wait_recv()
        for a in range(n):
            for cp in sent(a) + passed(a):
                cp.wait_send()
            own(a).wait()

    return _Rider(shards, [jax.ShapeDtypeStruct((N_DEV,) + s.shape, s.dtype) for s in shards],
                  [pltpu.VMEM(s.shape, s.dtype) for s in shards]
                  + [pltpu.SemaphoreType.DMA((n, 7)), pltpu.SemaphoreType.DMA((n, 7)), pltpu.SemaphoreType.DMA((n,))],
                  first, last, middle)


def _sibling_rider(grads):
    n = len(grads)

    def copies(g, got, scratch):
        send_sems, recv_sems = scratch
        x, y, c = _pos()
        return [pltpu.make_async_remote_copy(src_ref=g[a].at[2 * q + (1 - c)], dst_ref=got[a].at[q], send_sem=send_sems.at[a, q],
                                             recv_sem=recv_sems.at[a, q], device_id=(x, y, 1 - c), device_id_type=MESH)
                for a in range(n) for q in range(4)]

    def first(g, got, scratch):
        for cp in copies(g, got, scratch):
            cp.start()

    def last(g, got, scratch):
        for cp in copies(g, got, scratch):
            cp.wait()

    return _Rider(grads, [jax.ShapeDtypeStruct((4,) + g.shape[1:], g.dtype) for g in grads],
                  [pltpu.SemaphoreType.DMA((n, 4))] * 2, first, last)


def _chips_rider(sums):
    n = len(sums)

    def copies(s, out, scratch):
        send_sems, recv_sems = scratch
        x, y, c = _pos()
        cps = []
        for a in range(n):
            for f in (1, 2, 3):
                peer = (_flip(x, f >> 1), _flip(y, f & 1), c)
                cps.append(pltpu.make_async_remote_copy(
                    src_ref=s[a].at[2 * peer[0] + peer[1]], dst_ref=out[a].at[f - 1], send_sem=send_sems.at[a, f - 1],
                    recv_sem=recv_sems.at[a, f - 1], device_id=peer, device_id_type=MESH))
        return cps

    def first(s, out, scratch):
        for cp in copies(s, out, scratch):
            cp.start()

    def last(s, out, scratch):
        for cp in copies(s, out, scratch):
            cp.wait()

    return _Rider(sums, [jax.ShapeDtypeStruct((3,) + s.shape[1:], s.dtype) for s in sums],
                  [pltpu.SemaphoreType.DMA((n, 3))] * 2, first, last)


def _add_and_send(g, got, name):
    _, r, c = got.shape

    def body(g_ref, got_ref, out_ref, a_buf, b_buf, s_buf, load_sems, send_sems, recv_sems):
        x, y, cc = _pos()
        copies = []
        for f in (1, 2, 3):
            peer = (_flip(x, f >> 1), _flip(y, f & 1), cc)
            qd = 2 * peer[0] + peer[1]
            mine = pltpu.make_async_copy(g_ref.at[2 * qd + cc], a_buf, load_sems.at[0])
            theirs = pltpu.make_async_copy(got_ref.at[qd], b_buf, load_sems.at[1])
            mine.start()
            theirs.start()
            mine.wait()
            theirs.wait()
            s_buf[f - 1] = (a_buf[...] + b_buf[...]).astype(bf16)
            cp = pltpu.make_async_remote_copy(src_ref=s_buf.at[f - 1], dst_ref=out_ref.at[f - 1], send_sem=send_sems.at[f - 1],
                                              recv_sem=recv_sems.at[f - 1], device_id=peer, device_id_type=MESH)
            cp.start()
            copies.append(cp)
        for cp in copies:
            cp.wait()

    return pl.pallas_call(
        body, name=name, in_specs=[ANY, ANY], out_specs=ANY, out_shape=jax.ShapeDtypeStruct((3, r, c), bf16),
        scratch_shapes=[pltpu.VMEM((r, c), f32), pltpu.VMEM((r, c), f32), pltpu.VMEM((3, r, c), bf16),
                        pltpu.SemaphoreType.DMA((2,)), pltpu.SemaphoreType.DMA((3,)), pltpu.SemaphoreType.DMA((3,))],
    )(g, got)


def _both(a, b):
    na = (len(a.ins), len(a.out_shapes), len(a.scratch))

    def split(fa, fb):
        def f(ins, outs, scratch):
            fa(ins[:na[0]], outs[:na[1]], scratch[:na[2]])
            fb(ins[na[0]:], outs[na[1]:], scratch[na[2]:])
        return f

    return _Rider(a.ins + b.ins, a.out_shapes + b.out_shapes, a.scratch + b.scratch, split(a.first, b.first), split(a.last, b.last))


def _alone(rider, name):
    ri, ro = len(rider.ins), len(rider.out_shapes)

    def body(*refs):
        theirs = (refs[:ri], refs[ri:ri + ro], refs[ri + ro:])
        rider.first(*theirs)
        if rider.middle is not None:
            rider.middle(*theirs)
        rider.last(*theirs)

    return pl.pallas_call(body, name=name, in_specs=[ANY] * ri, out_specs=[ANY] * ro, out_shape=rider.out_shapes,
                          scratch_shapes=rider.scratch)(*rider.ins)


def _gather_small(g_w1, g_w2, g_w3, g_lb, g_wn, loss):
    def body(w1_ref, w2_ref, w3_ref, lb_ref, wn_ref, loss_ref, out_ref, pk, send_sems, recv_sems):
        x, y, c = _pos()
        me = 4 * x + 2 * y + c
        pk[...] = jnp.zeros_like(pk)
        pk[0:1, :] = w1_ref[...]
        pk[1:2, :] = w2_ref[...]
        pk[2:3, :] = w3_ref[...]
        pk[3:4, 0:HGRN_W] = lb_ref[...]
        pk[3:4, HGRN_W:2 * HGRN_W] = wn_ref[...]
        pk[4:5, 0:128] = loss_ref[...]
        out_ref[me] = pk[...]
        sends, recvs = [], []
        for k in range(1, N_DEV):
            peer = (_flip(x, k >> 2), _flip(y, (k >> 1) & 1), _flip(c, k & 1))
            cp = pltpu.make_async_remote_copy(src_ref=pk, dst_ref=out_ref.at[me], send_sem=send_sems.at[k - 1],
                                              recv_sem=recv_sems.at[k - 1], device_id=peer, device_id_type=MESH)
            cp.start()
            sends.append(cp)
            recvs.append(pltpu.make_async_remote_copy(src_ref=pk, dst_ref=out_ref.at[4 * peer[0] + 2 * peer[1] + peer[2]],
                                                      send_sem=send_sems.at[k - 1], recv_sem=recv_sems.at[k - 1], device_id=peer,
                                                      device_id_type=MESH))
        for cp in recvs:
            cp.wait_recv()
        for cp in sends:
            cp.wait_send()

    return pl.pallas_call(
        body, name="gather_small", in_specs=[VMEM_SPEC] * 6, out_specs=VMEM_SPEC,
        out_shape=jax.ShapeDtypeStruct((N_DEV, 8, D_MODEL), f32),
        scratch_shapes=[pltpu.VMEM((8, D_MODEL), f32), pltpu.SemaphoreType.DMA((N_DEV - 1,)), pltpu.SemaphoreType.DMA((N_DEV - 1,))],
    )(g_w1, g_w2, g_w3, g_lb, g_wn, loss)


def _row_tile(r):
    return max(t for t in range(16, 513, 16) if r % t == 0)


def _add_sibling(core, g, got, name):
    _, r, c = got.shape
    tr = _row_tile(r)

    def body(core_ref, a_ref, b_ref, o_ref):
        o_ref[...] = (a_ref[...] + b_ref[...]).astype(bf16)

    blk = pl.BlockSpec((1, tr, c), lambda q, i, core_ref: (q, i, 0))
    return pl.pallas_call(
        body, name=name, out_shape=jax.ShapeDtypeStruct(got.shape, bf16),
        grid_spec=pltpu.PrefetchScalarGridSpec(
            num_scalar_prefetch=1, grid=(4, r // tr),
            in_specs=[pl.BlockSpec((1, tr, c), lambda q, i, core_ref: (2 * q + core_ref[0], i, 0)), blk], out_specs=blk),
        compiler_params=_cp("arbitrary", "arbitrary"))(core, g, got)


def _adamw(w, g, m, v):
    m = ADAM_B1 * m + (1.0 - ADAM_B1) * g
    v = ADAM_B2 * v + (1.0 - ADAM_B2) * (g * g)
    m_hat = m / (1.0 - ADAM_B1 ** ADAM_STEP)
    v_hat = v / (1.0 - ADAM_B2 ** ADAM_STEP)
    return -ADAM_LR * (m_hat / (jnp.sqrt(v_hat) + ADAM_EPS) + ADAM_WD * w), m, v


def _adam_shard(where, g, got, pieces, w, m, v, name):
    r, c = w.shape
    tr = _row_tile(r)

    def body(where_ref, g_ref, got_ref, p_ref, w_ref, m_ref, v_ref, g_out, d_out, m_out, v_out):
        gsum = g_ref[0] + got_ref[0]
        for f in range(3):
            gsum = gsum + p_ref[f].astype(f32)
        g_out[...] = gsum
        d_out[...], m_out[...], v_out[...] = _adamw(w_ref[...], gsum, m_ref[...], v_ref[...])

    blk = pl.BlockSpec((tr, c), lambda i, where_ref: (i, 0))
    return pl.pallas_call(
        body, name=name, out_shape=[jax.ShapeDtypeStruct((r, c), f32)] * 4,
        grid_spec=pltpu.PrefetchScalarGridSpec(
            num_scalar_prefetch=1, grid=(r // tr,),
            in_specs=[pl.BlockSpec((1, tr, c), lambda i, where_ref: (where_ref[0], i, 0)),
                      pl.BlockSpec((1, tr, c), lambda i, where_ref: (where_ref[1], i, 0)),
                      pl.BlockSpec((3, tr, c), lambda i, where_ref: (0, i, 0)), blk, blk, blk],
            out_specs=[blk] * 4),
        compiler_params=_cp("arbitrary"),
    )(where, g, got, pieces, w, m, v)


def _small_update(gath, params):
    def body(gath_ref, *refs):
        ins, outs = refs[:15], refs[15:]
        gs = gath_ref[0]
        for k in range(1, N_DEV):
            gs = gs + gath_ref[k]
        outs[0][...] = gs[4:5, 0:128]
        l0, l1 = ins[9][0:1, :], ins[9][1:2, :]
        lb = _sigmoid(l0 - l1)
        d0 = gs[3:4, 0:HGRN_W] * lb * (1.0 - lb)
        first_row = lax.broadcasted_iota(jnp.int32, (2, HGRN_W), 0) == 0
        grads = [gs[0:1, :], gs[1:2, :], gs[2:3, :], jnp.where(first_row, d0, -d0), gs[3:4, HGRN_W:2 * HGRN_W]]
        for i, g in enumerate(grads):
            w_ref, m_ref, v_ref = ins[3 * i:3 * i + 3]
            o = outs[1 + 4 * i:5 + 4 * i]
            o[0][...] = g
            o[1][...], o[2][...], o[3][...] = _adamw(w_ref[...], g, m_ref[...], v_ref[...])

    flat = [a for p in params for a in p]
    out_shape = [jax.ShapeDtypeStruct((1, 128), f32)] + [jax.ShapeDtypeStruct(p[0].shape, f32) for p in params for _ in range(4)]
    outs = pl.pallas_call(body, name="small_update", in_specs=[VMEM_SPEC] * 16, out_specs=[VMEM_SPEC] * 21, out_shape=out_shape)(gath, *flat)
    return outs[0], [outs[1 + 4 * i:5 + 4 * i] for i in range(5)]


def kernel(x, norm1_w, w_in, lb_logits, hgrn_norm_w, w_out, norm2_w, w_gate_up, w_down, final_norm_w, loss_target, m_norm1_w, m_w_in, m_lb_logits, m_hgrn_norm_w, m_w_out, m_norm2_w, m_w_gate_up, m_w_down, m_final_norm_w, v_norm1_w, v_w_in, v_lb_logits, v_hgrn_norm_w, v_w_out, v_norm2_w, v_w_gate_up, v_w_down, v_final_norm_w):
    row = lambda a: a.reshape(1, D_MODEL)
    ix, iy, ic = lax.axis_index("x"), lax.axis_index("y"), lax.axis_index("c")
    core = jnp.stack([ic]).astype(jnp.int32)
    where = jnp.stack([4 * ix + 2 * iy + ic, 2 * ix + iy]).astype(jnp.int32)
    xs, tgt, w3 = x[0], loss_target[0], row(final_norm_w)
    S = xs.shape[0]

    cos_t, sg_t, win_g = _rope_tables(S, _gather_rider([w_in[0].astype(bf16)]))
    u, qkv, hp, win = _in_proj(xs, norm1_w, win_g, cos_t, sg_t)
    ya, lse, wout_g, wgu_g, wdown_g = _attn_fwd(qkv, _gather_rider([w_out[0].astype(bf16), w_gate_up[0].astype(bf16),
                                                                     w_down[0].astype(bf16)]))
    wout = wout_g.reshape(D_MODEL, D_MODEL)
    wdown = wdown_g.reshape(FFN, D_MODEL)
    yb, o_sav, states = _hgrn_fwd(hp, lb_logits, hgrn_norm_w)
    h1, u2, mixed = _out_proj(xs, ya, yb, wout, norm2_w)
    silu, up_dsilu, act, wgate, wup = _gate_up(u2, wgu_g)
    dh2, loss_p, g_w3 = _down_loss(act, wdown, h1, tgt, w3)

    (g_wdown,) = _gw(act, [dh2], 512, "gw_down")
    dgu2 = _dact(dh2, wdown, silu, up_dsilu)
    early = [_gw_by_owner(u2, dgu2, 2 * FFN // N_DEV, "gw_gate_up", 2048), g_wdown.reshape(N_DEV, FFN // N_DEV, D_MODEL)]
    dh1, g_w2, dmix, *got_early = _dgu(dgu2, wgate, wup, h1, norm2_w, dh2, wout, _sibling_rider(early))
    sums_early = [_add_sibling(core, g, o, f"add_sibling_{i}") for i, (g, o) in enumerate(zip(early, got_early))]
    (g_wout,) = _gw(mixed, [dh1], 1024, "gw_out")
    mid = [g_wout.reshape(N_DEV, D_MODEL // N_DEV, D_MODEL)]
    dhq, dhf, dhi, dhg, g_wn, g_lb, *rode = _hgrn_bwd(hp, lb_logits, hgrn_norm_w, o_sav, states, dmix,
                                                      _both(_chips_rider(sums_early), _sibling_rider(mid)))
    pieces_early, got_mid = rode[:2], rode[2:]
    sums_mid = [_add_sibling(core, mid[0], got_mid[0], "add_sibling_2")]
    dq, dk, dv, *pieces_mid = _attn_bwd(qkv, ya, lse, dmix, _chips_rider(sums_mid))
    dproj, gx, g_w1 = _din(dq, dk, dv, dhq, dhf, dhi, dhg, cos_t, sg_t, win, xs, norm1_w, dh1)
    late = [_gw_by_owner(u, dproj[None], IN_W // N_DEV, "gw_in", 2048)]
    got_late = _alone(_sibling_rider(late), "reduce_sibling")
    pieces_late = [_add_and_send(late[0], got_late[0], "reduce_chips")]

    grads = [late[0], mid[0], early[0], early[1]]
    got = [got_late[0], got_mid[0], got_early[0], got_early[1]]
    pieces = [pieces_late[0], pieces_mid[0], pieces_early[0], pieces_early[1]]
    shards = [w_in[0], w_out[0], w_gate_up[0], w_down[0]]
    moms = [(m_w_in[0], v_w_in[0]), (m_w_out[0], v_w_out[0]), (m_w_gate_up[0], v_w_gate_up[0]), (m_w_down[0], v_w_down[0])]
    big = [_adam_shard(where, g, o, p, w, m, v, f"adam_{i}")
           for i, (g, o, p, w, (m, v)) in enumerate(zip(grads, got, pieces, shards, moms))]
    big = [[a[None] for a in four] for four in big]

    gath = _gather_small(g_w1, g_w2, g_w3, g_lb, g_wn, loss_p)
    params = [(norm1_w, m_norm1_w, v_norm1_w), (norm2_w, m_norm2_w, v_norm2_w),
              (row(final_norm_w), row(m_final_norm_w), row(v_final_norm_w)),
              (lb_logits, m_lb_logits, v_lb_logits), (hgrn_norm_w, m_hgrn_norm_w, v_hgrn_norm_w)]
    loss, (s_w1, s_w2, s_w3, s_lb, s_wn) = _small_update(gath, params)
    s_w3 = [a.reshape(D_MODEL) for a in s_w3]
    per_w = [s_w1, big[0], s_lb, s_wn, big[1], s_w2, big[2], big[3], s_w3]
    return (loss[0, 0], gx[None], *[p[0] for p in per_w], *[p[1] for p in per_w], *[p[2] for p in per_w], *[p[3] for p in per_w])
```

```python
import jax
import jax.numpy as jnp
from jax import lax
from jax.experimental import pallas as pl
from jax.experimental.pallas import tpu as pltpu

f32, bf16 = jnp.float32, jnp.bfloat16

D_MODEL = 1024
ATTN_W = 512
HEAD_DIM = 64
ATTN_BLK = 128
DILATIONS = (1, 4, 16)
HGRN_W = 512
HGRN_HD = 128
CHUNK = 64
IN_W = 3 * ATTN_W + 4 * HGRN_W
FFN = 2816
EPS = 1e-6
ROPE_THETA = 10000.0
NEG = -1e30
N_DEV = 8
ADAM_LR, ADAM_B1, ADAM_B2, ADAM_EPS, ADAM_WD, ADAM_STEP = 0.001, 0.9, 0.999, 1e-08, 0.01, 10
VMEM_LIMIT = 56 * 1024 * 1024


def _cp(*sem):
    return pltpu.CompilerParams(dimension_semantics=sem, vmem_limit_bytes=VMEM_LIMIT)


def _dot(a, b):
    return jnp.dot(a, b, preferred_element_type=f32)


def _dot_nt(a, b):
    return lax.dot_general(a, b, (((1,), (1,)), ((), ())), preferred_element_type=f32)


def _dot_tn(a, b):
    return lax.dot_general(a, b, (((0,), (0,)), ((), ())), preferred_element_type=f32)


def _sigmoid(x):
    return 0.5 * jnp.tanh(0.5 * x) + 0.5


class _Rider:
    def __init__(self, ins, out_shapes, scratch, first, last, middle=None):
        self.ins, self.out_shapes, self.scratch = list(ins), list(out_shapes), list(scratch)
        self.first, self.middle, self.last = first, middle, last


def _ride(call, rider, body, step, n_steps, n_in, n_out, n_scratch):
    if rider is None:
        return call, body, []
    ri, ro = len(rider.ins), len(rider.out_shapes)
    any_spec = pl.BlockSpec(memory_space=pl.ANY)
    call = dict(call, in_specs=call["in_specs"] + [any_spec] * ri, out_specs=call["out_specs"] + [any_spec] * ro,
                out_shape=call["out_shape"] + rider.out_shapes, scratch_shapes=call["scratch_shapes"] + rider.scratch)

    def riding(*refs):
        a = n_in + ri
        b = a + n_out + ro
        mine = refs[:n_in] + refs[a:a + n_out] + refs[b:b + n_scratch]
        theirs = (refs[n_in:a], refs[a + n_out:b], refs[b + n_scratch:])
        t = step()

        @pl.when(t == 0)
        def _():
            rider.first(*theirs)

        body(*mine)
        if rider.middle is not None:
            @pl.when(t == n_steps // 2)
            def _():
                rider.middle(*theirs)

        @pl.when(t == n_steps - 1)
        def _():
            rider.last(*theirs)

    return call, riding, rider.ins


def _rope_tables(S, rider=None):
    half = HEAD_DIM // 2
    tm = 256
    inv_freq = jnp.tile(ROPE_THETA ** (-jnp.arange(half, dtype=f32) / half), 128 // half).reshape(1, 128)
    sign = jnp.tile(jnp.concatenate([-jnp.ones((half,), f32), jnp.ones((half,), f32)]), 128 // HEAD_DIM).reshape(1, 128)

    def body(inv_ref, sign_ref, cos_ref, sg_ref):
        pos = (lax.broadcasted_iota(jnp.int32, (tm, 128), 0) + pl.program_id(0) * tm).astype(f32)
        ang = pos * inv_ref[...]
        cos_ref[...] = jnp.cos(ang)
        sg_ref[...] = jnp.sin(ang) * sign_ref[...]

    vec = pl.BlockSpec((1, 128), lambda i: (0, 0))
    out = pl.BlockSpec((tm, 128), lambda i: (i, 0))
    call = dict(in_specs=[vec, vec], out_specs=[out, out], out_shape=[jax.ShapeDtypeStruct((S, 128), f32)] * 2, scratch_shapes=[])
    call, body, more = _ride(call, rider, body, lambda: pl.program_id(0), S // tm, 2, 2, 0)
    return pl.pallas_call(body, name="rope_tables", grid=(S // tm,), compiler_params=_cp("arbitrary"), **call)(inv_freq, sign, *more)


def _swap_halves(v):
    n = v.shape[1]
    lane = lax.broadcasted_iota(jnp.int32, v.shape, 1)
    return jnp.where((lane % HEAD_DIM) < HEAD_DIM // 2, pltpu.roll(v, n - HEAD_DIM // 2, 1), pltpu.roll(v, HEAD_DIM // 2, 1))


def _in_proj(x, w1, win_g, cos_t, sg_t):
    S = x.shape[0]
    tm = 512
    w = IN_W // N_DEV

    def body(x_ref, w1_ref, wg_ref, cos_ref, sg_ref, u_ref, qkv_ref, hp_ref, w_ref):
        @pl.when(pl.program_id(0) == 0)
        def _():
            for d in range(N_DEV):
                w_ref[:, w * d:w * (d + 1)] = wg_ref[d]

        xv = x_ref[...]
        r = lax.rsqrt(jnp.mean(xv * xv, axis=-1, keepdims=True) + EPS)
        u = (xv * r * w1_ref[...]).astype(bf16)
        u_ref[...] = u
        cosv, sgv = jnp.tile(cos_ref[...], (1, ATTN_W // 128)), jnp.tile(sg_ref[...], (1, ATTN_W // 128))
        for j in range(3):
            pj = _dot(u, w_ref[:, j * ATTN_W:(j + 1) * ATTN_W])
            if j < 2:
                pj = pj * cosv + _swap_halves(pj) * sgv
            if j == 0:
                pj = pj * (HEAD_DIM ** -0.5)
            qkv_ref[:, j * ATTN_W:(j + 1) * ATTN_W] = pj.astype(bf16)
        for j in range(4):
            lo = 3 * ATTN_W + j * HGRN_W
            hp_ref[:, j * HGRN_W:(j + 1) * HGRN_W] = _dot(u, w_ref[:, lo:lo + HGRN_W])

    return pl.pallas_call(
        body, name="in_proj", grid=(S // tm,),
        in_specs=[pl.BlockSpec((tm, D_MODEL), lambda i: (i, 0)), pl.BlockSpec((1, D_MODEL), lambda i: (0, 0)),
                  pl.BlockSpec((N_DEV, D_MODEL, w), lambda i: (0, 0, 0)),
                  pl.BlockSpec((tm, 128), lambda i: (i, 0)), pl.BlockSpec((tm, 128), lambda i: (i, 0))],
        out_specs=[pl.BlockSpec((tm, D_MODEL), lambda i: (i, 0)), pl.BlockSpec((tm, 3 * ATTN_W), lambda i: (i, 0)),
                   pl.BlockSpec((tm, 4 * HGRN_W), lambda i: (i, 0)), pl.BlockSpec((D_MODEL, IN_W), lambda i: (0, 0))],
        out_shape=[jax.ShapeDtypeStruct((S, D_MODEL), bf16), jax.ShapeDtypeStruct((S, 3 * ATTN_W), bf16),
                   jax.ShapeDtypeStruct((S, 4 * HGRN_W), f32), jax.ShapeDtypeStruct((D_MODEL, IN_W), bf16)],
        compiler_params=_cp("arbitrary"),
    )(x, w1, win_g, cos_t, sg_t)


def _head_masks():
    lane = lax.broadcasted_iota(jnp.int32, (ATTN_BLK, 128), 1)
    even = lane < HEAD_DIM
    return even, (even, jnp.logical_not(even))


def _pair_fwd(q2, k2, v2, bias):
    even, masks = _head_masks()
    outs, lses = [], []
    for e in range(2):
        qm = jnp.where(masks[e], q2, 0.0).astype(bf16)
        s = _dot_nt(qm, k2) + bias
        m = jnp.max(s, axis=-1, keepdims=True)
        pe = jnp.exp(s - m)
        lsum = jnp.sum(pe, axis=-1, keepdims=True)
        outs.append(_dot(pe.astype(bf16), v2) / lsum)
        lses.append(jnp.broadcast_to(m + jnp.log(lsum), (ATTN_BLK, 128)))
    return jnp.where(even, outs[0], outs[1]), jnp.where(even, lses[0], lses[1])


def _merge(y0, l0, y1, l1):
    mx = jnp.maximum(l0, l1)
    a, b = jnp.exp(l0 - mx), jnp.exp(l1 - mx)
    tot = a + b
    return (a * y0 + b * y1) / tot, mx + jnp.log(tot)


def _pair_bwd(q2, k2f, v2, dy2, lse2, delta2, bias):
    _, masks = _head_masks()
    k2 = k2f.astype(bf16)
    klane = lax.broadcasted_iota(jnp.int32, (2 * ATTN_BLK, 128), 1) < HEAD_DIM
    kmasks = (klane, jnp.logical_not(klane))
    dq2 = jnp.zeros((ATTN_BLK, 128), f32)
    pes, dss, qms, dyms = [], [], [], []
    for e in range(2):
        c0 = e * HEAD_DIM
        qm = jnp.where(masks[e], q2, 0.0).astype(bf16)
        km = jnp.where(kmasks[e], k2f, 0.0).astype(bf16)
        dym = jnp.where(masks[e], dy2, 0.0).astype(bf16)
        pe = jnp.exp(_dot_nt(qm, k2) + bias - lse2[:, c0:c0 + 1])
        ds = (pe * (_dot_nt(dym, v2) - delta2[:, c0:c0 + 1])).astype(bf16)
        dq2 = dq2 + _dot(ds, km)
        pes.append(pe.astype(bf16))
        dss.append(ds)
        qms.append(qm)
        dyms.append(dym)
    dv2 = _dot_tn(jnp.concatenate(pes, axis=0), jnp.concatenate(dyms, axis=0))
    dk2 = _dot_tn(jnp.concatenate(dss, axis=0), jnp.concatenate(qms, axis=0))
    return dq2, dk2, dv2


TOK = 2048


def _key_bias():
    qi = lax.broadcasted_iota(jnp.int32, (ATTN_BLK, 2 * ATTN_BLK), 0)
    kj = lax.broadcasted_iota(jnp.int32, (ATTN_BLK, 2 * ATTN_BLK), 1)
    delta = ATTN_BLK + qi - kj
    seen = (delta >= 0) & (delta <= ATTN_BLK)
    return jnp.where(seen, 0.0, NEG), jnp.where(seen & (kj >= ATTN_BLK), 0.0, NEG)


def _attn_fwd(qkv, rider=None):
    S = qkv.shape[0]
    nS = S // TOK

    def body(q_ref, kp_ref, kc_ref, vp_ref, vc_ref, y_ref, l_ref, qs, k2, v2, ay, al):
        n = pl.program_id(1)
        qs[...] = q_ref[...].astype(f32)
        k2[0:TOK] = kp_ref[...].astype(f32)
        k2[TOK:2 * TOK] = kc_ref[...].astype(f32)
        v2[0:TOK] = vp_ref[...].astype(f32)
        v2[TOK:2 * TOK] = vc_ref[...].astype(f32)
        bias_any, bias_first = _key_bias()

        def block(dil, r, b, step, last):
            start = r + pl.multiple_of(step * b, step)
            rows = pl.ds(start, ATTN_BLK, stride=dil) if dil > 1 else pl.ds(start, ATTN_BLK)
            keys = (pl.ds(TOK + start - step, 2 * ATTN_BLK, stride=dil) if dil > 1
                    else pl.ds(TOK + start - step, 2 * ATTN_BLK))
            bias = jnp.where((n == 0) & (b == 0), bias_first, bias_any)
            out, lse = _pair_fwd(qs[rows, :], k2[keys, :].astype(bf16), v2[keys, :].astype(bf16), bias)
            if dil < DILATIONS[-1]:
                out, lse = _merge(ay[rows, :], al[rows, :], out, lse)
            if last:
                y_ref[rows, :] = out
                l_ref[rows, :] = lse
            else:
                ay[rows, :] = out
                al[rows, :] = lse

        for dil in reversed(DILATIONS):
            def loop(i, carry, dil=dil):
                block(dil, i % dil, i // dil, ATTN_BLK * dil, dil == 1)
                return carry
            lax.fori_loop(0, TOK // ATTN_BLK, loop, 0, unroll=True)

    blk = (TOK, 128)
    cur = lambda c: pl.BlockSpec(blk, lambda p, n: (n, 4 * c + p))
    prv = lambda c: pl.BlockSpec(blk, lambda p, n: (jnp.maximum(n - 1, 0), 4 * c + p))
    out = pl.BlockSpec(blk, lambda p, n: (n, p))
    call = dict(in_specs=[cur(0), prv(1), cur(1), prv(2), cur(2)], out_specs=[out, out],
                out_shape=[jax.ShapeDtypeStruct((S, ATTN_W), f32)] * 2,
                scratch_shapes=[pltpu.VMEM(blk, f32), pltpu.VMEM((2 * TOK, 128), f32), pltpu.VMEM((2 * TOK, 128), f32),
                                pltpu.VMEM(blk, f32), pltpu.VMEM(blk, f32)])
    call, body, more = _ride(call, rider, body, lambda: pl.program_id(0) * nS + pl.program_id(1), (ATTN_W // 128) * nS, 5, 2, 5)
    return pl.pallas_call(body, name="attention_fwd", grid=(ATTN_W // 128, nS), compiler_params=_cp("arbitrary", "arbitrary"),
                          **call)(qkv, qkv, qkv, qkv, qkv, *more)


def _attn_bwd(qkv, ya, lse, dmix, rider=None):
    S = qkv.shape[0]
    nS = S // TOK

    def body(q_ref, kp_ref, kc_ref, vp_ref, vc_ref, y_ref, l_ref, dy_ref, dq_ref, dk_ref, dv_ref, qs, k2, v2, dk2, dv2, dqa, dl):
        n = pl.program_id(1)

        @pl.when(n == 0)
        def _():
            dk2[...] = jnp.zeros_like(dk2)
            dv2[...] = jnp.zeros_like(dv2)

        @pl.when(n < nS)
        def _():
            qs[...] = q_ref[...].astype(f32)
            k2[0:TOK] = kp_ref[...].astype(f32)
            k2[TOK:2 * TOK] = kc_ref[...].astype(f32)
            v2[0:TOK] = vp_ref[...].astype(f32)
            v2[TOK:2 * TOK] = vc_ref[...].astype(f32)
            li = lax.broadcasted_iota(jnp.int32, (128, 128), 0)
            lj = lax.broadcasted_iota(jnp.int32, (128, 128), 1)
            seg = jnp.where((li // HEAD_DIM) == (lj // HEAD_DIM), 1.0, 0.0).astype(bf16)
            bias_any, bias_first = _key_bias()

            def delta_rows(t, carry):
                rows = pl.ds(pl.multiple_of(256 * t, 256), 256)
                dyy = dy_ref[rows, :] * y_ref[rows, :]
                hi = dyy.astype(bf16)
                dl[rows, :] = _dot(hi, seg) + _dot((dyy - hi.astype(f32)).astype(bf16), seg)
                return carry

            lax.fori_loop(0, TOK // 256, delta_rows, 0)

            def block(dil, r, b, step, first_pattern, last):
                start = r + pl.multiple_of(step * b, step)
                rows = pl.ds(start, ATTN_BLK, stride=dil) if dil > 1 else pl.ds(start, ATTN_BLK)
                keys = (pl.ds(TOK + start - step, 2 * ATTN_BLK, stride=dil) if dil > 1
                        else pl.ds(TOK + start - step, 2 * ATTN_BLK))
                bias = jnp.where((n == 0) & (b == 0), bias_first, bias_any)
                dq2, dkk, dvv = _pair_bwd(qs[rows, :], k2[keys, :], v2[keys, :].astype(bf16), dy_ref[rows, :],
                                          l_ref[rows, :], dl[rows, :], bias)
                if last:
                    dq_ref[rows, :] = dqa[rows, :] + dq2
                elif first_pattern:
                    dqa[rows, :] = dq2
                else:
                    dqa[rows, :] += dq2
                dk2[keys, :] += dkk
                dv2[keys, :] += dvv

            for dil in reversed(DILATIONS):
                def loop(i, carry, dil=dil):
                    block(dil, i % dil, i // dil, ATTN_BLK * dil, dil == DILATIONS[-1], dil == 1)
                    return carry
                lax.fori_loop(0, TOK // ATTN_BLK, loop, 0, unroll=True)

        dk_ref[...] = dk2[0:TOK]
        dv_ref[...] = dv2[0:TOK]
        dk2[0:TOK] = dk2[TOK:2 * TOK]
        dv2[0:TOK] = dv2[TOK:2 * TOK]
        dk2[TOK:2 * TOK] = jnp.zeros((TOK, 128), f32)
        dv2[TOK:2 * TOK] = jnp.zeros((TOK, 128), f32)

    blk = (TOK, 128)
    cn = lambda n: jnp.minimum(n, nS - 1)
    pn = lambda n: jnp.clip(n - 1, 0, nS - 1)
    cur = lambda c: pl.BlockSpec(blk, lambda p, n: (cn(n), 4 * c + p))
    prv = lambda c: pl.BlockSpec(blk, lambda p, n: (pn(n), 4 * c + p))
    at_n = pl.BlockSpec(blk, lambda p, n: (cn(n), p))
    at_p = pl.BlockSpec(blk, lambda p, n: (pn(n), p))
    big = lambda: pltpu.VMEM((2 * TOK, 128), f32)
    call = dict(in_specs=[cur(0), prv(1), cur(1), prv(2), cur(2), at_n, at_n, at_n], out_specs=[at_n, at_p, at_p],
                out_shape=[jax.ShapeDtypeStruct((S, ATTN_W), f32)] * 3,
                scratch_shapes=[pltpu.VMEM(blk, f32), big(), big(), big(), big(), pltpu.VMEM(blk, f32), pltpu.VMEM(blk, f32)])
    call, body, more = _ride(call, rider, body, lambda: pl.program_id(0) * (nS + 1) + pl.program_id(1),
                             (ATTN_W // 128) * (nS + 1), 8, 3, 7)
    return pl.pallas_call(body, name="attention_bwd", grid=(ATTN_W // 128, nS + 1), compiler_params=_cp("arbitrary", "arbitrary"),
                          **call)(qkv, qkv, qkv, qkv, qkv, ya, lse, dmix, *more)


HG_T = 512
N_HH = HGRN_W // HGRN_HD
HG_SUB = 128
SAFE_RANGE = 75.0


def _row_in_chunk(n=None):
    return lax.broadcasted_iota(jnp.int32, (HG_T if n is None else n, HGRN_HD), 0) % CHUNK


def _chunk_cumsum(v, rc):
    k = 1
    while k < CHUNK:
        v = v + jnp.where(rc >= k, pltpu.roll(v, k, 0), 0.0)
        k *= 2
    return v


def _chunk_rcumsum(v, rc):
    k = 1
    while k < CHUNK:
        v = v + jnp.where(rc < CHUNK - k, pltpu.roll(v, v.shape[0] - k, 0), 0.0)
        k *= 2
    return v


def _hgrn_gates(qb, fb, lb):
    sf = _sigmoid(fb)
    f = lb + (1.0 - lb) * sf
    sq = _sigmoid(qb)
    return sf, f, jnp.log(f), 1.0 - f, sq, qb * sq


def _hgrn_prep(qb, fb, lbl2, rc):
    lb = _sigmoid(lbl2[0:1, :] - lbl2[1:2, :])
    sf, f, lf, key, sq, qf = _hgrn_gates(qb, fb, lb)
    b = _chunk_cumsum(lf, rc)
    rem = _chunk_rcumsum(lf, rc) - lf
    return dict(lb=lb, sf=sf, f=f, key=key, sq=sq, qf=qf, b=b, rem=rem, eb=jnp.exp(b), er=jnp.exp(rem))


def _chunk_mask():
    r = lax.broadcasted_iota(jnp.int32, (HG_SUB, HG_SUB), 0)
    c = lax.broadcasted_iota(jnp.int32, (HG_SUB, HG_SUB), 1)
    return ((r // CHUNK) == (c // CHUNK)) & (c <= r)


def _hgrn_fwd(hp, lbl, wn):
    S = hp.shape[0]
    nT = S // HG_T

    def body(qb_ref, fb_ref, ib_ref, gb_ref, lbl_ref, wn_ref, yb_ref, o_ref, st_ref, ST, qt_s, kh_s, dec_s, oi_s):
        @pl.when(pl.program_id(0) == 0)
        def _():
            ST[...] = jnp.zeros_like(ST)

        rc = _row_in_chunk(HG_SUB)

        def sub_tile(j, carry):
            rs = pl.ds(pl.multiple_of(j * HG_SUB, HG_SUB), HG_SUB)
            for h in range(N_HH):
                sl = slice(HGRN_HD * h, HGRN_HD * (h + 1))
                p = _hgrn_prep(qb_ref[rs, sl], fb_ref[rs, sl], lbl_ref[:, sl], rc)
                qf, key, b = p["qf"], p["key"], p["b"]
                qt = qf * p["eb"]
                qt_s[rs, sl] = qt.astype(bf16)
                kh_s[rs, sl] = (key * p["er"]).astype(bf16)
                dec_s[rs, sl] = jnp.exp(b + p["rem"])
                rng = jnp.max(-(b + p["rem"]))

                @pl.when(rng < SAFE_RANGE)
                def _():
                    kp = (key * jnp.exp(-b)).astype(bf16)
                    sc = jnp.where(_chunk_mask(), _dot_nt(qt.astype(bf16), kp), 0.0).astype(bf16)
                    oi_s[rs, sl] = _dot(sc, ib_ref[rs, sl].astype(bf16))

                @pl.when(rng >= SAFE_RANGE)
                def _():
                    v = ib_ref[rs, sl]
                    ones = jnp.ones((HGRN_HD, HGRN_HD), bf16)

                    def lag(l, o):
                        e = jnp.exp(jnp.where(rc >= l, b - pltpu.roll(b, l, 0), NEG))
                        pr = qf * pltpu.roll(key, l, 0) * e
                        return o + _dot(pr.astype(bf16), ones) * pltpu.roll(v, l, 0)

                    oi_s[rs, sl] = lax.fori_loop(1, CHUNK, lag, _dot((qf * key).astype(bf16), ones) * v)
            return carry

        lax.fori_loop(0, HG_T // HG_SUB, sub_tile, 0)

        def step(c, carry):
            rows = pl.ds(pl.multiple_of(c * CHUNK, CHUNK), CHUNK)
            row0 = pl.ds(pl.multiple_of(c * CHUNK, CHUNK), 1)
            for h in range(N_HH):
                sl = slice(HGRN_HD * h, HGRN_HD * (h + 1))
                stv = ST[h]
                st_ref[c, sl, :] = stv
                oi_s[rows, sl] += _dot_nt(qt_s[rows, sl], stv.astype(bf16))
                ST[h] = stv * dec_s[row0, sl] + _dot_tn(ib_ref[rows, sl].astype(bf16), kh_s[rows, sl])
            return carry

        lax.fori_loop(0, HG_T // CHUNK, step, 0, unroll=True)

        for h in range(N_HH):
            sl = slice(HGRN_HD * h, HGRN_HD * (h + 1))
            o = oi_s[:, sl]
            o_ref[:, sl] = o
            on = o * lax.rsqrt(jnp.mean(o * o, axis=-1, keepdims=True) + EPS)
            g = gb_ref[:, sl]
            yb_ref[:, sl] = on * wn_ref[:, sl] * (g * _sigmoid(g))

    col = lambda c: pl.BlockSpec((HG_T, HGRN_W), lambda i: (i, c))
    tile = pl.BlockSpec((HG_T, HGRN_W), lambda i: (i, 0))
    whole = lambda a: pl.BlockSpec(a.shape, lambda i: (0, 0))
    return pl.pallas_call(
        body, name="hgrn_fwd", grid=(nT,),
        in_specs=[col(0), col(1), col(2), col(3), whole(lbl), whole(wn)],
        out_specs=[tile, tile, pl.BlockSpec((HG_T // CHUNK, HGRN_W, HGRN_HD), lambda i: (i, 0, 0))],
        out_shape=[jax.ShapeDtypeStruct((S, HGRN_W), f32), jax.ShapeDtypeStruct((S, HGRN_W), f32),
                   jax.ShapeDtypeStruct((S // CHUNK, HGRN_W, HGRN_HD), f32)],
        scratch_shapes=[pltpu.VMEM((N_HH, HGRN_HD, HGRN_HD), f32), pltpu.VMEM((HG_T, HGRN_W), bf16),
                        pltpu.VMEM((HG_T, HGRN_W), bf16), pltpu.VMEM((HG_T, HGRN_W), f32), pltpu.VMEM((HG_T, HGRN_W), f32)],
        compiler_params=_cp("arbitrary"),
    )(hp, hp, hp, hp, lbl, wn)


def _hgrn_bwd(hp, lbl, wn, o_sav, states, dmix, rider=None):
    S = hp.shape[0]
    nT = S // HG_T

    def body(qb_ref, fb_ref, ib_ref, gb_ref, lbl_ref, wn_ref, o_ref, st_ref, dy_ref,
             dq_ref, df_ref, di_ref, dg_ref, gwn_ref, glb_ref,
             DST, qt_s, kh_s, dec_s, do_s, dqt_s, dkh_s, dbl_s, dvi_s, dqi_s, dki_s, dbi_s):
        @pl.when(pl.program_id(0) == 0)
        def _():
            DST[...] = jnp.zeros_like(DST)
            gwn_ref[...] = jnp.zeros_like(gwn_ref)
            glb_ref[...] = jnp.zeros_like(glb_ref)

        rc = _row_in_chunk()
        preps = []
        for h in range(N_HH):
            sl = slice(HGRN_HD * h, HGRN_HD * (h + 1))
            p = _hgrn_prep(qb_ref[:, sl], fb_ref[:, sl], lbl_ref[:, sl], rc)
            preps.append(p)
            qf, key, b = p["qf"], p["key"], p["b"]
            v = ib_ref[:, sl]
            o = o_ref[:, sl]
            rinv = lax.rsqrt(jnp.mean(o * o, axis=-1, keepdims=True) + EPS)
            on = o * rinv
            g = gb_ref[:, sl]
            sgm = _sigmoid(g)
            silu_g = g * sgm
            dy = dy_ref[:, sl]
            wn_v = wn_ref[:, sl]
            gwn_ref[:, sl] += jnp.sum(dy * on * silu_g, axis=0, keepdims=True)
            dg_ref[:, sl] = (dy * on * wn_v * (sgm * (1.0 + g * (1.0 - sgm)))).astype(bf16)
            t1 = dy * wn_v * silu_g
            do = rinv * (t1 - on * jnp.mean(t1 * on, axis=-1, keepdims=True))
            do_s[:, sl] = do.astype(bf16)
            qt = qf * p["eb"]
            qt_s[:, sl] = qt.astype(bf16)
            kh_s[:, sl] = (key * p["er"]).astype(bf16)
            dec_s[:, sl] = jnp.exp(b + p["rem"])
            rng = jnp.max(-(b + p["rem"]))

            @pl.when(rng < SAFE_RANGE)
            def _():
                einv = jnp.exp(-b)
                kp = (key * einv).astype(bf16)
                cmask = _chunk_mask()
                for j in range(HG_T // HG_SUB):
                    rs = slice(HG_SUB * j, HG_SUB * (j + 1))
                    qtb, dob, vb = qt[rs].astype(bf16), do[rs].astype(bf16), v[rs].astype(bf16)
                    sc = jnp.where(cmask, _dot_nt(qtb, kp[rs]), 0.0).astype(bf16)
                    dsc = jnp.where(cmask, _dot_nt(dob, vb), 0.0).astype(bf16)
                    dqp = _dot(dsc, kp[rs])
                    dkp = _dot_tn(dsc, qtb)
                    dvi_s[rs, sl] = _dot_tn(sc, dob)
                    dqi_s[rs, sl] = dqp * p["eb"][rs]
                    dki_s[rs, sl] = dkp * einv[rs]
                    dbi_s[rs, sl] = dqp * qtb.astype(f32) - dkp * kp[rs].astype(f32)

            @pl.when(rng >= SAFE_RANGE)
            def _():
                ones = jnp.ones((HGRN_HD, HGRN_HD), bf16)

                def lag(l, carry):
                    dqf, dkey, db, dv = carry
                    e = jnp.exp(jnp.where(rc >= l, b - pltpu.roll(b, l, 0), NEG))
                    ks, vs, qe = pltpu.roll(key, l, 0), pltpu.roll(v, l, 0), qf * e
                    pr = qe * ks
                    rl = _dot(pr.astype(bf16), ones)
                    drl = jnp.where(rc >= l, _dot((do * vs).astype(bf16), ones), 0.0)
                    gl = drl * pr
                    back = HG_T - l
                    return (dqf + drl * ks * e, dkey + pltpu.roll(drl * qe, back, 0), db + gl - pltpu.roll(gl, back, 0),
                            dv + pltpu.roll(rl * do, back, 0))

                rl0 = _dot((qf * key).astype(bf16), ones)
                drl0 = _dot((do * v).astype(bf16), ones)
                dqf, dkey, db, dv = lax.fori_loop(1, CHUNK, lag, (drl0 * key, drl0 * qf, jnp.zeros((HG_T, HGRN_HD), f32), rl0 * do))
                dvi_s[:, sl] = dv
                dqi_s[:, sl] = dqf
                dki_s[:, sl] = dkey
                dbi_s[:, sl] = db

        def step(k, carry):
            c = HG_T // CHUNK - 1 - k
            rows = pl.ds(pl.multiple_of(c * CHUNK, CHUNK), CHUNK)
            row0 = pl.ds(pl.multiple_of(c * CHUNK, CHUNK), 1)
            for h in range(N_HH):
                sl = slice(HGRN_HD * h, HGRN_HD * (h + 1))
                stp = st_ref[c, sl, :]
                dst = DST[h]
                dstb = dst.astype(bf16)
                dob = do_s[rows, sl]
                khb = kh_s[rows, sl]
                dec = dec_s[row0, sl]
                dqt_s[rows, sl] = _dot(dob, stp.astype(bf16))
                dkh = _dot(ib_ref[rows, sl].astype(bf16), dstb)
                dkh_s[rows, sl] = dkh
                dvi_s[rows, sl] += _dot_nt(khb, dstb)
                dbl = jnp.sum(dst * stp, axis=0, keepdims=True) * dec + jnp.sum(dkh * khb.astype(f32), axis=0, keepdims=True)
                dbl_s[rows, sl] = jnp.broadcast_to(dbl, (CHUNK, HGRN_HD))
                DST[h] = dst * dec + _dot_tn(dob, qt_s[rows, sl])
            return carry

        lax.fori_loop(0, HG_T // CHUNK, step, 0, unroll=True)

        for h in range(N_HH):
            sl = slice(HGRN_HD * h, HGRN_HD * (h + 1))
            qb = qb_ref[:, sl]
            p = preps[h]
            sf, sq, lb = p["sf"], p["sq"], p["lb"]
            dqt, dkh = dqt_s[:, sl], dkh_s[:, sl]
            dqf = dqt * p["eb"] + dqi_s[:, sl]
            dkey = dkh * p["er"] + dki_s[:, sl]
            db = dqt * (p["qf"] * p["eb"]) - dkh * (p["key"] * p["er"]) + jnp.where(rc == CHUNK - 1, dbl_s[:, sl], 0.0) + dbi_s[:, sl]
            df = _chunk_rcumsum(db, rc) / p["f"] - dkey
            df_ref[:, sl] = (df * (1.0 - lb) * sf * (1.0 - sf)).astype(bf16)
            glb_ref[:, sl] += jnp.sum(df * (1.0 - sf), axis=0, keepdims=True)
            dq_ref[:, sl] = (dqf * (sq * (1.0 + qb * (1.0 - sq)))).astype(bf16)
            di_ref[:, sl] = dvi_s[:, sl].astype(bf16)

    rev = lambda i: nT - 1 - i
    col = lambda c: pl.BlockSpec((HG_T, HGRN_W), lambda i: (rev(i), c))
    tile = pl.BlockSpec((HG_T, HGRN_W), lambda i: (rev(i), 0))
    whole = lambda a: pl.BlockSpec(a.shape, lambda i: (0, 0))
    vec = pl.BlockSpec((1, HGRN_W), lambda i: (0, 0))
    tb = lambda: pltpu.VMEM((HG_T, HGRN_W), bf16)
    tf = lambda: pltpu.VMEM((HG_T, HGRN_W), f32)
    call = dict(in_specs=[col(0), col(1), col(2), col(3), whole(lbl), whole(wn), tile,
                          pl.BlockSpec((HG_T // CHUNK, HGRN_W, HGRN_HD), lambda i: (rev(i), 0, 0)),
                          pl.BlockSpec((HG_T, HGRN_W), lambda i: (rev(i), 1))],
                out_specs=[tile, tile, tile, tile, vec, vec],
                out_shape=[jax.ShapeDtypeStruct((S, HGRN_W), bf16)] * 4 + [jax.ShapeDtypeStruct((1, HGRN_W), f32)] * 2,
                scratch_shapes=[pltpu.VMEM((N_HH, HGRN_HD, HGRN_HD), f32), tb(), tb(), tf(), tb(), tf(), tf(), tf(), tf(), tf(),
                                tf(), tf()])
    call, body, more = _ride(call, rider, body, lambda: pl.program_id(0), nT, 9, 6, 12)
    return pl.pallas_call(body, name="hgrn_bwd", grid=(nT,), compiler_params=_cp("arbitrary"), **call)(
        hp, hp, hp, hp, lbl, wn, o_sav, states, dmix, *more)


def _out_proj(x, ya, yb, wout, w2):
    S = x.shape[0]
    tm = 512

    def body(x_ref, ya_ref, yb_ref, w_ref, w2_ref, h1_ref, u2_ref, mix_ref):
        mixed = jnp.concatenate([ya_ref[...], yb_ref[...]], axis=1).astype(bf16)
        mix_ref[...] = mixed
        h1 = x_ref[...] + _dot(mixed, w_ref[...])
        h1_ref[...] = h1
        r = lax.rsqrt(jnp.mean(h1 * h1, axis=-1, keepdims=True) + EPS)
        u2_ref[...] = (h1 * r * w2_ref[...]).astype(bf16)

    row = lambda w: pl.BlockSpec((tm, w), lambda i: (i, 0))
    return pl.pallas_call(
        body, name="out_proj", grid=(S // tm,),
        in_specs=[row(D_MODEL), row(ATTN_W), row(HGRN_W), pl.BlockSpec((D_MODEL, D_MODEL), lambda i: (0, 0)),
                  pl.BlockSpec((1, D_MODEL), lambda i: (0, 0))],
        out_specs=[row(D_MODEL), row(D_MODEL), row(D_MODEL)],
        out_shape=[jax.ShapeDtypeStruct((S, D_MODEL), f32), jax.ShapeDtypeStruct((S, D_MODEL), bf16),
                   jax.ShapeDtypeStruct((S, D_MODEL), bf16)],
        compiler_params=_cp("arbitrary"),
    )(x, ya, yb, wout, w2)


def _gate_up(u2, wgu_g):
    S = u2.shape[0]
    w = 2 * FFN // N_DEV
    tm, tn = 512, 2 * w
    nj = FFN // tn

    def body(u_ref, wgg_ref, wug_ref, g_ref, up_ref, a_ref, wg_ref, wu_ref):
        @pl.when(pl.program_id(1) == 0)
        def _():
            for k in range(2):
                wg_ref[:, w * k:w * (k + 1)] = wgg_ref[k]
                wu_ref[:, w * k:w * (k + 1)] = wug_ref[k]

        u = u_ref[...]
        g = _dot(u, wg_ref[...])
        up = _dot(u, wu_ref[...])
        sg = _sigmoid(g)
        silu = g * sg
        g_ref[...] = silu.astype(bf16)
        up_ref[...] = (up * (sg + silu * (1.0 - sg))).astype(bf16)
        a_ref[...] = (silu * up).astype(bf16)

    out = pl.BlockSpec((tm, tn), lambda j, i: (i, j))
    wout = pl.BlockSpec((D_MODEL, tn), lambda j, i: (0, j))
    return pl.pallas_call(
        body, name="gate_up", grid=(nj, S // tm),
        in_specs=[pl.BlockSpec((tm, D_MODEL), lambda j, i: (i, 0)), pl.BlockSpec((2, D_MODEL, w), lambda j, i: (j, 0, 0)),
                  pl.BlockSpec((2, D_MODEL, w), lambda j, i: (j + nj, 0, 0))],
        out_specs=[out, out, out, wout, wout],
        out_shape=[jax.ShapeDtypeStruct((S, FFN), bf16)] * 3 + [jax.ShapeDtypeStruct((D_MODEL, FFN), bf16)] * 2,
        compiler_params=_cp("arbitrary", "arbitrary"),
    )(u2, wgu_g, wgu_g)


def _rms_bwd(dyw, hn, r):
    return r * (dyw - hn * jnp.mean(dyw * hn, axis=-1, keepdims=True))


def _down_loss(act, wdown, h1, tgt, w3):
    S = act.shape[0]
    tm = 512

    def body(a_ref, w_ref, h1_ref, t_ref, w3_ref, dh2_ref, loss_ref, gw3_ref):
        @pl.when(pl.program_id(0) == 0)
        def _():
            loss_ref[...] = jnp.zeros_like(loss_ref)
            gw3_ref[...] = jnp.zeros_like(gw3_ref)

        h2 = h1_ref[...] + _dot(a_ref[...], w_ref[...])
        r = lax.rsqrt(jnp.mean(h2 * h2, axis=-1, keepdims=True) + EPS)
        hn = h2 * r
        w3 = w3_ref[...]
        err = hn * w3 - t_ref[...]
        loss_ref[...] += (0.5 / D_MODEL) * jnp.sum(err * err)
        dy = err * (1.0 / D_MODEL)
        gw3_ref[...] += jnp.sum(dy * hn, axis=0, keepdims=True)
        dh2_ref[...] = _rms_bwd(dy * w3, hn, r)

    row = lambda w: pl.BlockSpec((tm, w), lambda i: (i, 0))
    return pl.pallas_call(
        body, name="down_loss", grid=(S // tm,),
        in_specs=[row(FFN), pl.BlockSpec((FFN, D_MODEL), lambda i: (0, 0)), row(D_MODEL), row(D_MODEL),
                  pl.BlockSpec((1, D_MODEL), lambda i: (0, 0))],
        out_specs=[row(D_MODEL), pl.BlockSpec((1, 128), lambda i: (0, 0)), pl.BlockSpec((1, D_MODEL), lambda i: (0, 0))],
        out_shape=[jax.ShapeDtypeStruct((S, D_MODEL), f32), jax.ShapeDtypeStruct((1, 128), f32),
                   jax.ShapeDtypeStruct((1, D_MODEL), f32)],
        compiler_params=_cp("arbitrary"),
    )(act, wdown, h1, tgt, w3)


def _dact(dh2, wdown, silu, up_dsilu):
    S = dh2.shape[0]
    tm = 256

    def body(d_ref, w_ref, s_ref, u_ref, o_ref):
        da = _dot_nt(d_ref[...].astype(bf16), w_ref[...])
        o_ref[1] = (da * s_ref[...].astype(f32)).astype(bf16)
        o_ref[0] = (da * u_ref[...].astype(f32)).astype(bf16)

    row = lambda w: pl.BlockSpec((tm, w), lambda i: (i, 0))
    return pl.pallas_call(
        body, name="dact", grid=(S // tm,),
        in_specs=[row(D_MODEL), pl.BlockSpec((FFN, D_MODEL), lambda i: (0, 0)), row(FFN), row(FFN)],
        out_specs=pl.BlockSpec((2, tm, FFN), lambda i: (0, i, 0)),
        out_shape=jax.ShapeDtypeStruct((2, S, FFN), bf16),
        compiler_params=_cp("arbitrary"),
    )(dh2, wdown, silu, up_dsilu)


def _dgu(dgu2, wgate, wup, h1, w2, dh2, wout, rider=None):
    S = dgu2.shape[1]
    tm = 256

    def body(d_ref, wg_ref, wu_ref, h1_ref, w2_ref, dh2_ref, wo_ref, dh1_ref, gw2_ref, dmix_ref):
        @pl.when(pl.program_id(0) == 0)
        def _():
            gw2_ref[...] = jnp.zeros_like(gw2_ref)

        du2 = _dot_nt(d_ref[0], wg_ref[...]) + _dot_nt(d_ref[1], wu_ref[...])
        h1 = h1_ref[...]
        r = lax.rsqrt(jnp.mean(h1 * h1, axis=-1, keepdims=True) + EPS)
        hn = h1 * r
        gw2_ref[...] += jnp.sum(du2 * hn, axis=0, keepdims=True)
        dh1 = dh2_ref[...] + _rms_bwd(du2 * w2_ref[...], hn, r)
        dh1_ref[...] = dh1
        dmix_ref[...] = _dot_nt(dh1.astype(bf16), wo_ref[...])

    row = lambda w: pl.BlockSpec((tm, w), lambda i: (i, 0))
    call = dict(in_specs=[pl.BlockSpec((2, tm, FFN), lambda i: (0, i, 0)), pl.BlockSpec((D_MODEL, FFN), lambda i: (0, 0)),
                          pl.BlockSpec((D_MODEL, FFN), lambda i: (0, 0)), row(D_MODEL),
                          pl.BlockSpec((1, D_MODEL), lambda i: (0, 0)), row(D_MODEL),
                          pl.BlockSpec((D_MODEL, D_MODEL), lambda i: (0, 0))],
                out_specs=[row(D_MODEL), pl.BlockSpec((1, D_MODEL), lambda i: (0, 0)), row(D_MODEL)],
                out_shape=[jax.ShapeDtypeStruct((S, D_MODEL), f32), jax.ShapeDtypeStruct((1, D_MODEL), f32),
                           jax.ShapeDtypeStruct((S, D_MODEL), f32)], scratch_shapes=[])
    call, body, more = _ride(call, rider, body, lambda: pl.program_id(0), S // tm, 7, 3, 0)
    return pl.pallas_call(body, name="dgu", grid=(S // tm,), compiler_params=_cp("arbitrary"), **call)(
        dgu2, wgate, wup, h1, w2, dh2, wout, *more)


def _din(dq, dk, dv, dhq, dhf, dhi, dhg, cos_t, sg_t, win, x, w1, dh1):
    S = x.shape[0]
    tm = 512

    def body(dq_ref, dk_ref, dv_ref, dhq_ref, dhf_ref, dhi_ref, dhg_ref, cos_ref, sg_ref, w_ref, x_ref, w1_ref, dh1_ref,
             dp_ref, gx_ref, gw1_ref):
        @pl.when(pl.program_id(0) == 0)
        def _():
            gw1_ref[...] = jnp.zeros_like(gw1_ref)

        cosv, sgv = jnp.tile(cos_ref[...], (1, ATTN_W // 128)), jnp.tile(sg_ref[...], (1, ATTN_W // 128))
        unrope = lambda d: d * cosv - sgv * _swap_halves(d)
        parts = [(unrope(dq_ref[...]) * (HEAD_DIM ** -0.5)).astype(bf16), unrope(dk_ref[...]).astype(bf16),
                 dv_ref[...].astype(bf16), dhq_ref[...], dhf_ref[...], dhi_ref[...], dhg_ref[...]]
        du = jnp.zeros((tm, D_MODEL), f32)
        for j, pj in enumerate(parts):
            dp_ref[:, j * 512:(j + 1) * 512] = pj
            du = du + _dot_nt(pj, w_ref[:, j * 512:(j + 1) * 512])
        xv = x_ref[...]
        r = lax.rsqrt(jnp.mean(xv * xv, axis=-1, keepdims=True) + EPS)
        xn = xv * r
        gw1_ref[...] += jnp.sum(du * xn, axis=0, keepdims=True)
        gx_ref[...] = dh1_ref[...] + _rms_bwd(du * w1_ref[...], xn, r)

    row = lambda w: pl.BlockSpec((tm, w), lambda i: (i, 0))
    vec = pl.BlockSpec((1, D_MODEL), lambda i: (0, 0))
    return pl.pallas_call(
        body, name="din", grid=(S // tm,),
        in_specs=[row(512)] * 7 + [row(128), row(128), pl.BlockSpec((D_MODEL, IN_W), lambda i: (0, 0)), row(D_MODEL), vec,
                                   row(D_MODEL)],
        out_specs=[row(IN_W), row(D_MODEL), vec],
        out_shape=[jax.ShapeDtypeStruct((S, IN_W), bf16), jax.ShapeDtypeStruct((S, D_MODEL), f32),
                   jax.ShapeDtypeStruct((1, D_MODEL), f32)],
        compiler_params=_cp("arbitrary"),
    )(dq, dk, dv, dhq, dhf, dhi, dhg, cos_t, sg_t, win, x, w1, dh1)


def _gw(a, bs, tn, name, ts=2048):
    S, M = a.shape
    N = bs[0].shape[1]
    k = len(bs)

    def body(a_ref, *refs):
        @pl.when(pl.program_id(1) == 0)
        def _():
            for o_ref in refs[k:]:
                o_ref[...] = jnp.zeros_like(o_ref)

        at = a_ref[...].astype(bf16)
        for b_ref, o_ref in zip(refs[:k], refs[k:]):
            o_ref[...] += _dot_tn(at, b_ref[...].astype(bf16))

    return pl.pallas_call(
        body, name=name, grid=(N // tn, S // ts),
        in_specs=[pl.BlockSpec((ts, M), lambda j, s: (s, 0))] + [pl.BlockSpec((ts, tn), lambda j, s: (s, j))] * k,
        out_specs=[pl.BlockSpec((M, tn), lambda j, s: (0, j))] * k, out_shape=[jax.ShapeDtypeStruct((M, N), f32)] * k,
        compiler_params=_cp("arbitrary", "arbitrary"),
    )(a, *bs)


def _gw_by_owner(a, b3, w, name, ts):
    S, M = a.shape
    G, _, Ng = b3.shape
    tn = 2 * w
    per_group = Ng // tn
    n_s = S // ts

    def body(a_ref, b_ref, o_ref, acc):
        s = pl.program_id(1)

        @pl.when(s == 0)
        def _():
            acc[...] = jnp.zeros_like(acc)

        acc[...] += _dot_tn(a_ref[...].astype(bf16), b_ref[0].astype(bf16))

        @pl.when(s == n_s - 1)
        def _():
            o_ref[0] = acc[:, 0:w]
            o_ref[1] = acc[:, w:tn]

    return pl.pallas_call(
        body, name=name, grid=(G * per_group, n_s),
        in_specs=[pl.BlockSpec((ts, M), lambda j, s: (s, 0)),
                  pl.BlockSpec((1, ts, tn), lambda j, s: (j // per_group, s, j % per_group))],
        out_specs=pl.BlockSpec((2, M, w), lambda j, s: (j, 0, 0)), out_shape=jax.ShapeDtypeStruct((G * Ng // w, M, w), f32),
        scratch_shapes=[pltpu.VMEM((M, tn), f32)], compiler_params=_cp("arbitrary", "arbitrary"),
    )(a, b3)


MESH = pl.DeviceIdType.MESH
ANY = pl.BlockSpec(memory_space=pl.ANY)
VMEM_SPEC = pl.BlockSpec(memory_space=pltpu.VMEM)


def _pos():
    return lax.axis_index("x"), lax.axis_index("y"), lax.axis_index("c")


def _flip(v, bit):
    return 1 - v if bit else v


def _gather_rider(shards):
    n = len(shards)

    def parts(outs, scratch):
        send_sems, recv_sems, local_sems = scratch[n:]
        x, y, c = _pos()
        chips = [(1 - x, y), (x, 1 - y), (1 - x, 1 - y)]

        def copy(a, k, block, to, src=None):
            dst = outs[a].at[4 * block[0] + 2 * block[1] + block[2]]
            return pltpu.make_async_remote_copy(src_ref=dst if src is None else src, dst_ref=dst, send_sem=send_sems.at[a, k],
                                                recv_sem=recv_sems.at[a, k], device_id=to, device_id_type=MESH)

        bufs = scratch[:n]
        me, sibling = (x, y, c), (x, y, 1 - c)
        own = lambda a: pltpu.make_async_copy(bufs[a], outs[a].at[4 * x + 2 * y + c], local_sems.at[a])
        sent = lambda a: [copy(a, 0, me, sibling, src=bufs[a])] + [copy(a, 1 + j, me, (*chip, c), src=bufs[a])
                                                                   for j, chip in enumerate(chips)]
        passed = lambda a: [copy(a, 4 + j, (*chip, c), sibling) for j, chip in enumerate(chips)]
        landed = lambda a: [copy(a, 1 + j, (*chip, c), me) for j, chip in enumerate(chips)]
        from_sibling = lambda a: [copy(a, 0, sibling, me)] + [copy(a, 4 + j, (*chip, 1 - c), me) for j, chip in enumerate(chips)]
        return bufs, local_sems, own, sent, passed, landed, from_sibling

    def first(ins, outs, scratch):
        bufs, local_sems, own, sent, _, _, _ = parts(outs, scratch)
        loads = [pltpu.make_async_copy(ins[a], bufs[a], local_sems.at[a]) for a in range(n)]
        for ld in loads:
            ld.start()
        for a in range(n):
            loads[a].wait()
            own(a).start()
            for cp in sent(a):
                cp.start()

    def middle(ins, outs, scratch):
        _, _, _, _, passed, landed, _ = parts(outs, scratch)
        for a in range(n):
            for got, on in zip(landed(a), passed(a)):
                got.wait_recv()
                on.start()

    def last(ins, outs, scratch):
        _, _, own, sent, passed, _, from_sibling = parts(outs, scratch)
        for a in range(n):
            for cp in from_sibling(a):
                cp.wait_recv()
        for a in range(n):
            for cp in sent(a) + passed(a):
                cp.wait_send()
            own(a).wait()

    return _Rider(shards, [jax.ShapeDtypeStruct((N_DEV,) + s.shape, s.dtype) for s in shards],
                  [pltpu.VMEM(s.shape, s.dtype) for s in shards]
                  + [pltpu.SemaphoreType.DMA((n, 7)), pltpu.SemaphoreType.DMA((n, 7)), pltpu.SemaphoreType.DMA((n,))],
                  first, last, middle)


def _sibling_rider(grads):
    n = len(grads)

    def copies(g, got, scratch):
        send_sems, recv_sems = scratch
        x, y, c = _pos()
        return [pltpu.make_async_remote_copy(src_ref=g[a].at[2 * q + (1 - c)], dst_ref=got[a].at[q], send_sem=send_sems.at[a, q],
                                             recv_sem=recv_sems.at[a, q], device_id=(x, y, 1 - c), device_id_type=MESH)
                for a in range(n) for q in range(4)]

    def first(g, got, scratch):
        for cp in copies(g, got, scratch):
            cp.start()

    def last(g, got, scratch):
        for cp in copies(g, got, scratch):
            cp.wait()

    return _Rider(grads, [jax.ShapeDtypeStruct((4,) + g.shape[1:], g.dtype) for g in grads],
                  [pltpu.SemaphoreType.DMA((n, 4))] * 2, first, last)


def _chips_rider(sums):
    n = len(sums)

    def copies(s, out, scratch):
        send_sems, recv_sems = scratch
        x, y, c = _pos()
        cps = []
        for a in range(n):
            for f in (1, 2, 3):
                peer = (_flip(x, f >> 1), _flip(y, f & 1), c)
                cps.append(pltpu.make_async_remote_copy(
                    src_ref=s[a].at[2 * peer[0] + peer[1]], dst_ref=out[a].at[f - 1], send_sem=send_sems.at[a, f - 1],
                    recv_sem=recv_sems.at[a, f - 1], device_id=peer, device_id_type=MESH))
        return cps

    def first(s, out, scratch):
        for cp in copies(s, out, scratch):
            cp.start()

    def last(s, out, scratch):
        for cp in copies(s, out, scratch):
            cp.wait()

    return _Rider(sums, [jax.ShapeDtypeStruct((3,) + s.shape[1:], s.dtype) for s in sums],
                  [pltpu.SemaphoreType.DMA((n, 3))] * 2, first, last)


def _add_and_send(g, got, name):
    _, r, c = got.shape

    def body(g_ref, got_ref, out_ref, a_buf, b_buf, s_buf, load_sems, send_sems, recv_sems):
        x, y, cc = _pos()
        copies = []
        for f in (1, 2, 3):
            peer = (_flip(x, f >> 1), _flip(y, f & 1), cc)
            qd = 2 * peer[0] + peer[1]
            mine = pltpu.make_async_copy(g_ref.at[2 * qd + cc], a_buf, load_sems.at[0])
            theirs = pltpu.make_async_copy(got_ref.at[qd], b_buf, load_sems.at[1])
            mine.start()
            theirs.start()
            mine.wait()
            theirs.wait()
            s_buf[f - 1] = (a_buf[...] + b_buf[...]).astype(bf16)
            cp = pltpu.make_async_remote_copy(src_ref=s_buf.at[f - 1], dst_ref=out_ref.at[f - 1], send_sem=send_sems.at[f - 1],
                                              recv_sem=recv_sems.at[f - 1], device_id=peer, device_id_type=MESH)
            cp.start()
            copies.append(cp)
        for cp in copies:
            cp.wait()

    return pl.pallas_call(
        body, name=name, in_specs=[ANY, ANY], out_specs=ANY, out_shape=jax.ShapeDtypeStruct((3, r, c), bf16),
        scratch_shapes=[pltpu.VMEM((r, c), f32), pltpu.VMEM((r, c), f32), pltpu.VMEM((3, r, c), bf16),
                        pltpu.SemaphoreType.DMA((2,)), pltpu.SemaphoreType.DMA((3,)), pltpu.SemaphoreType.DMA((3,))],
    )(g, got)


def _both(a, b):
    na = (len(a.ins), len(a.out_shapes), len(a.scratch))

    def split(fa, fb):
        def f(ins, outs, scratch):
            fa(ins[:na[0]], outs[:na[1]], scratch[:na[2]])
            fb(ins[na[0]:], outs[na[1]:], scratch[na[2]:])
        return f

    return _Rider(a.ins + b.ins, a.out_shapes + b.out_shapes, a.scratch + b.scratch, split(a.first, b.first), split(a.last, b.last))


def _alone(rider, name):
    ri, ro = len(rider.ins), len(rider.out_shapes)

    def body(*refs):
        theirs = (refs[:ri], refs[ri:ri + ro], refs[ri + ro:])
        rider.first(*theirs)
        if rider.middle is not None:
            rider.middle(*theirs)
        rider.last(*theirs)

    return pl.pallas_call(body, name=name, in_specs=[ANY] * ri, out_specs=[ANY] * ro, out_shape=rider.out_shapes,
                          scratch_shapes=rider.scratch)(*rider.ins)


def _gather_small(g_w1, g_w2, g_w3, g_lb, g_wn, loss):
    def body(w1_ref, w2_ref, w3_ref, lb_ref, wn_ref, loss_ref, out_ref, pk, send_sems, recv_sems):
        x, y, c = _pos()
        me = 4 * x + 2 * y + c
        pk[...] = jnp.zeros_like(pk)
        pk[0:1, :] = w1_ref[...]
        pk[1:2, :] = w2_ref[...]
        pk[2:3, :] = w3_ref[...]
        pk[3:4, 0:HGRN_W] = lb_ref[...]
        pk[3:4, HGRN_W:2 * HGRN_W] = wn_ref[...]
        pk[4:5, 0:128] = loss_ref[...]
        out_ref[me] = pk[...]
        sends, recvs = [], []
        for k in range(1, N_DEV):
            peer = (_flip(x, k >> 2), _flip(y, (k >> 1) & 1), _flip(c, k & 1))
            cp = pltpu.make_async_remote_copy(src_ref=pk, dst_ref=out_ref.at[me], send_sem=send_sems.at[k - 1],
                                              recv_sem=recv_sems.at[k - 1], device_id=peer, device_id_type=MESH)
            cp.start()
            sends.append(cp)
            recvs.append(pltpu.make_async_remote_copy(src_ref=pk, dst_ref=out_ref.at[4 * peer[0] + 2 * peer[1] + peer[2]],
                                                      send_sem=send_sems.at[k - 1], recv_sem=recv_sems.at[k - 1], device_id=peer,
                                                      device_id_type=MESH))
        for cp in recvs:
            cp.wait_recv()
        for cp in sends:
            cp.wait_send()

    return pl.pallas_call(
        body, name="gather_small", in_specs=[VMEM_SPEC] * 6, out_specs=VMEM_SPEC,
        out_shape=jax.ShapeDtypeStruct((N_DEV, 8, D_MODEL), f32),
        scratch_shapes=[pltpu.VMEM((8, D_MODEL), f32), pltpu.SemaphoreType.DMA((N_DEV - 1,)), pltpu.SemaphoreType.DMA((N_DEV - 1,))],
    )(g_w1, g_w2, g_w3, g_lb, g_wn, loss)


def _row_tile(r):
    return max(t for t in range(16, 513, 16) if r % t == 0)


def _add_sibling(core, g, got, name):
    _, r, c = got.shape
    tr = _row_tile(r)

    def body(core_ref, a_ref, b_ref, o_ref):
        o_ref[...] = (a_ref[...] + b_ref[...]).astype(bf16)

    blk = pl.BlockSpec((1, tr, c), lambda q, i, core_ref: (q, i, 0))
    return pl.pallas_call(
        body, name=name, out_shape=jax.ShapeDtypeStruct(got.shape, bf16),
        grid_spec=pltpu.PrefetchScalarGridSpec(
            num_scalar_prefetch=1, grid=(4, r // tr),
            in_specs=[pl.BlockSpec((1, tr, c), lambda q, i, core_ref: (2 * q + core_ref[0], i, 0)), blk], out_specs=blk),
        compiler_params=_cp("arbitrary", "arbitrary"))(core, g, got)


def _adamw(w, g, m, v):
    m = ADAM_B1 * m + (1.0 - ADAM_B1) * g
    v = ADAM_B2 * v + (1.0 - ADAM_B2) * (g * g)
    m_hat = m / (1.0 - ADAM_B1 ** ADAM_STEP)
    v_hat = v / (1.0 - ADAM_B2 ** ADAM_STEP)
    return -ADAM_LR * (m_hat / (jnp.sqrt(v_hat) + ADAM_EPS) + ADAM_WD * w), m, v


def _adam_shard(where, g, got, pieces, w, m, v, name):
    r, c = w.shape
    tr = _row_tile(r)

    def body(where_ref, g_ref, got_ref, p_ref, w_ref, m_ref, v_ref, g_out, d_out, m_out, v_out):
        gsum = g_ref[0] + got_ref[0]
        for f in range(3):
            gsum = gsum + p_ref[f].astype(f32)
        g_out[...] = gsum
        d_out[...], m_out[...], v_out[...] = _adamw(w_ref[...], gsum, m_ref[...], v_ref[...])

    blk = pl.BlockSpec((tr, c), lambda i, where_ref: (i, 0))
    return pl.pallas_call(
        body, name=name, out_shape=[jax.ShapeDtypeStruct((r, c), f32)] * 4,
        grid_spec=pltpu.PrefetchScalarGridSpec(
            num_scalar_prefetch=1, grid=(r // tr,),
            in_specs=[pl.BlockSpec((1, tr, c), lambda i, where_ref: (where_ref[0], i, 0)),
                      pl.BlockSpec((1, tr, c), lambda i, where_ref: (where_ref[1], i, 0)),
                      pl.BlockSpec((3, tr, c), lambda i, where_ref: (0, i, 0)), blk, blk, blk],
            out_specs=[blk] * 4),
        compiler_params=_cp("arbitrary"),
    )(where, g, got, pieces, w, m, v)


def _small_update(gath, params):
    def body(gath_ref, *refs):
        ins, outs = refs[:15], refs[15:]
        gs = gath_ref[0]
        for k in range(1, N_DEV):
            gs = gs + gath_ref[k]
        outs[0][...] = gs[4:5, 0:128]
        l0, l1 = ins[9][0:1, :], ins[9][1:2, :]
        lb = _sigmoid(l0 - l1)
        d0 = gs[3:4, 0:HGRN_W] * lb * (1.0 - lb)
        first_row = lax.broadcasted_iota(jnp.int32, (2, HGRN_W), 0) == 0
        grads = [gs[0:1, :], gs[1:2, :], gs[2:3, :], jnp.where(first_row, d0, -d0), gs[3:4, HGRN_W:2 * HGRN_W]]
        for i, g in enumerate(grads):
            w_ref, m_ref, v_ref = ins[3 * i:3 * i + 3]
            o = outs[1 + 4 * i:5 + 4 * i]
            o[0][...] = g
            o[1][...], o[2][...], o[3][...] = _adamw(w_ref[...], g, m_ref[...], v_ref[...])

    flat = [a for p in params for a in p]
    out_shape = [jax.ShapeDtypeStruct((1, 128), f32)] + [jax.ShapeDtypeStruct(p[0].shape, f32) for p in params for _ in range(4)]
    outs = pl.pallas_call(body, name="small_update", in_specs=[VMEM_SPEC] * 16, out_specs=[VMEM_SPEC] * 21, out_shape=out_shape)(gath, *flat)
    return outs[0], [outs[1 + 4 * i:5 + 4 * i] for i in range(5)]


def kernel(x, norm1_w, w_in, lb_logits, hgrn_norm_w, w_out, norm2_w, w_gate_up, w_down, final_norm_w, loss_target, m_norm1_w, m_w_in, m_lb_logits, m_hgrn_norm_w, m_w_out, m_norm2_w, m_w_gate_up, m_w_down, m_final_norm_w, v_norm1_w, v_w_in, v_lb_logits, v_hgrn_norm_w, v_w_out, v_norm2_w, v_w_gate_up, v_w_down, v_final_norm_w):
    row = lambda a: a.reshape(1, D_MODEL)
    ix, iy, ic = lax.axis_index("x"), lax.axis_index("y"), lax.axis_index("c")
    core = jnp.stack([ic]).astype(jnp.int32)
    where = jnp.stack([4 * ix + 2 * iy + ic, 2 * ix + iy]).astype(jnp.int32)
    xs, tgt, w3 = x[0], loss_target[0], row(final_norm_w)
    S = xs.shape[0]

    cos_t, sg_t, win_g = _rope_tables(S, _gather_rider([w_in[0].astype(bf16)]))
    u, qkv, hp, win = _in_proj(xs, norm1_w, win_g, cos_t, sg_t)
    ya, lse, wout_g, wgu_g, wdown_g = _attn_fwd(qkv, _gather_rider([w_out[0].astype(bf16), w_gate_up[0].astype(bf16),
                                                                     w_down[0].astype(bf16)]))
    wout = wout_g.reshape(D_MODEL, D_MODEL)
    wdown = wdown_g.reshape(FFN, D_MODEL)
    yb, o_sav, states = _hgrn_fwd(hp, lb_logits, hgrn_norm_w)
    h1, u2, mixed = _out_proj(xs, ya, yb, wout, norm2_w)
    silu, up_dsilu, act, wgate, wup = _gate_up(u2, wgu_g)
    dh2, loss_p, g_w3 = _down_loss(act, wdown, h1, tgt, w3)

    (g_wdown,) = _gw(act, [dh2], 512, "gw_down")
    dgu2 = _dact(dh2, wdown, silu, up_dsilu)
    early = [_gw_by_owner(u2, dgu2, 2 * FFN // N_DEV, "gw_gate_up", 2048), g_wdown.reshape(N_DEV, FFN // N_DEV, D_MODEL)]
    dh1, g_w2, dmix, *got_early = _dgu(dgu2, wgate, wup, h1, norm2_w, dh2, wout, _sibling_rider(early))
    sums_early = [_add_sibling(core, g, o, f"add_sibling_{i}") for i, (g, o) in enumerate(zip(early, got_early))]
    (g_wout,) = _gw(mixed, [dh1], 1024, "gw_out")
    mid = [g_wout.reshape(N_DEV, D_MODEL // N_DEV, D_MODEL)]
    dhq, dhf, dhi, dhg, g_wn, g_lb, *rode = _hgrn_bwd(hp, lb_logits, hgrn_norm_w, o_sav, states, dmix,
                                                      _both(_chips_rider(sums_early), _sibling_rider(mid)))
    pieces_early, got_mid = rode[:2], rode[2:]
    sums_mid = [_add_sibling(core, mid[0], got_mid[0], "add_sibling_2")]
    dq, dk, dv, *pieces_mid = _attn_bwd(qkv, ya, lse, dmix, _chips_rider(sums_mid))
    dproj, gx, g_w1 = _din(dq, dk, dv, dhq, dhf, dhi, dhg, cos_t, sg_t, win, xs, norm1_w, dh1)
    late = [_gw_by_owner(u, dproj[None], IN_W // N_DEV, "gw_in", 2048)]
    got_late = _alone(_sibling_rider(late), "reduce_sibling")
    pieces_late = [_add_and_send(late[0], got_late[0], "reduce_chips")]

    grads = [late[0], mid[0], early[0], early[1]]
    got = [got_late[0], got_mid[0], got_early[0], got_early[1]]
    pieces = [pieces_late[0], pieces_mid[0], pieces_early[0], pieces_early[1]]
    shards = [w_in[0], w_out[0], w_gate_up[0], w_down[0]]
    moms = [(m_w_in[0], v_w_in[0]), (m_w_out[0], v_w_out[0]), (m_w_gate_up[0], v_w_gate_up[0]), (m_w_down[0], v_w_down[0])]
    big = [_adam_shard(where, g, o, p, w, m, v, f"adam_{i}")
           for i, (g, o, p, w, (m, v)) in enumerate(zip(grads, got, pieces, shards, moms))]
    big = [[a[None] for a in four] for four in big]

    gath = _gather_small(g_w1, g_w2, g_w3, g_lb, g_wn, loss_p)
    params = [(norm1_w, m_norm1_w, v_norm1_w), (norm2_w, m_norm2_w, v_norm2_w),
              (row(final_norm_w), row(m_final_norm_w), row(v_final_norm_w)),
              (lb_logits, m_lb_logits, v_lb_logits), (hgrn_norm_w, m_hgrn_norm_w, v_hgrn_norm_w)]
    loss, (s_w1, s_w2, s_w3, s_lb, s_wn) = _small_update(gath, params)
    s_w3 = [a.reshape(D_MODEL) for a in s_w3]
    per_w = [s_w1, big[0], s_lb, s_wn, big[1], s_w2, big[2], big[3], s_w3]
    return (loss[0, 0], gx[None], *[p[0] for p in per_w], *[p[1] for p in per_w], *[p[2] for p in per_w], *[p[3] for p in per_w])
```
